```python
import math
import jax, jax.numpy as jnp
from jax import lax
import numpy as np

D_MODEL = 1024
BATCH = 16
SEQ = 4096
DEPTH = 2

N_HEADS = 16
HEAD_DIM = D_MODEL // N_HEADS
D_FF = ((8 * D_MODEL // 3) + 255) // 256 * 256
CONV_WIDTH = 3
DILATED_BRANCHES = ((128, 1), (512, 4), (2048, 16))
BLOCK = 128
REL_BUCKETS = 32
REL_MAX_DISTANCE = 2048
N_A_LAYERS = DEPTH // 2
N_B_LAYERS = DEPTH - N_A_LAYERS
RMS_EPS = 1e-6

kernel_name = "yoco_shortconv_dilated_attention_trunk"


def rmsnorm(x, g):
    xf = x.astype(jnp.float32)
    y = xf * lax.rsqrt(jnp.mean(xf * xf, axis=-1, keepdims=True) + RMS_EPS)
    return (y * g.astype(jnp.float32)).astype(x.dtype)


def causal_dwconv(x, w, b=None):
    S = x.shape[1]
    xp = jnp.pad(x, ((0, 0), (CONV_WIDTH - 1, 0), (0, 0)))
    y = xp[:, 0:S] * w[0]
    for tap in range(1, CONV_WIDTH):
        y = y + xp[:, tap:tap + S] * w[tap]
    if b is not None:
        y = y + b
    return y


def t5_bucket(dist):
    max_exact = REL_BUCKETS // 2
    n = jnp.maximum(dist, 0)
    nf = jnp.maximum(n, max_exact).astype(jnp.float32)
    large = max_exact + (jnp.log(nf / max_exact) / math.log(REL_MAX_DISTANCE / max_exact)
                         * (REL_BUCKETS - max_exact)).astype(jnp.int32)
    large = jnp.minimum(large, REL_BUCKETS - 1)
    return jnp.where(n < max_exact, n, large)


def short_conv_mixer(xn, w_in, conv_w, w_out):
    b_gate, c_gate, h = jnp.split(xn @ w_in, 3, axis=-1)
    return (b_gate * causal_dwconv(c_gate * h, conv_w)) @ w_out


def conv_ffn(xn, w_up, conv_w, conv_b, w_down):
    u = causal_dwconv(xn @ w_up, conv_w, conv_b)
    g, up = jnp.split(u, 2, axis=-1)
    return (jax.nn.silu(g) * up) @ w_down


def dilated_branch(q, k, v, rel_bias, window, dilation):
    B, S, H, Dh = q.shape
    P = BLOCK
    W = window // dilation
    L = S // dilation
    Lp = -(-L // P) * P
    nb = Lp // P

    def to_sub(t):
        return t.reshape(B, L, dilation, H, Dh).transpose(2, 0, 1, 3, 4)

    def to_blocks(t):
        return (t.reshape(dilation, B, nb, P, H, Dh).transpose(0, 2, 1, 3, 4, 5)
                .reshape(dilation * nb, B, P, H, Dh))

    qs = jnp.pad(to_sub(q), ((0, 0), (0, 0), (0, Lp - L), (0, 0), (0, 0)))
    ks = jnp.pad(to_sub(k), ((0, 0), (0, 0), (P, Lp - L), (0, 0), (0, 0)))
    vs = jnp.pad(to_sub(v), ((0, 0), (0, 0), (P, Lp - L), (0, 0), (0, 0)))
    q_blk = to_blocks(qs)
    k_prev, k_cur = to_blocks(ks[:, :, :Lp]), to_blocks(ks[:, :, P:])
    v_prev, v_cur = to_blocks(vs[:, :, :Lp]), to_blocks(vs[:, :, P:])
    blk_idx = jnp.tile(jnp.arange(nb, dtype=jnp.int32), dilation)

    qi = jnp.arange(P, dtype=jnp.int32)[:, None]
    kc = jnp.arange(2 * P, dtype=jnp.int32)[None, :]
    delta = qi + P - kc
    band = (delta >= 0) & (delta <= W)
    bias = rel_bias[t5_bucket(delta * dilation)].astype(jnp.float32).transpose(2, 0, 1)
    scale = HEAD_DIM ** -0.5

    def block_fn(args):
        qb, kp, kcur, vp, vcur, j = args
        kw = jnp.concatenate([kp, kcur], axis=1).astype(jnp.float32)
        vw = jnp.concatenate([vp, vcur], axis=1).astype(jnp.float32)
        s = jnp.einsum('bqhd,bkhd->bhqk', qb.astype(jnp.float32), kw) * scale + bias
        valid = band & ((j * P + kc - P) >= 0)
        s = jnp.where(valid, s, -jnp.inf)
        m = jnp.max(s, axis=-1)
        p = jnp.exp(s - m[..., None])
        den = jnp.sum(p, axis=-1)
        num = jnp.einsum('bhqk,bkhd->bqhd', p, vw)
        return num, den.transpose(0, 2, 1), m.transpose(0, 2, 1)

    num, den, mx = lax.map(block_fn, (q_blk, k_prev, k_cur, v_prev, v_cur, blk_idx))

    def from_blocks(t):
        t = t.reshape((dilation, nb, B, P) + t.shape[3:])
        t = jnp.moveaxis(jnp.moveaxis(t, 0, 3), 0, 1)
        t = t.reshape((B, Lp, dilation) + t.shape[4:])[:, :L]
        return t.reshape((B, S) + t.shape[3:])

    return from_blocks(num), from_blocks(den), from_blocks(mx)


def dilated_attention(xn, w_q, w_o, k, v, rel_bias):
    B, S, _ = xn.shape
    q = (xn @ w_q).reshape(B, S, N_HEADS, HEAD_DIM)
    branches = [dilated_branch(q, k, v, rel_bias, w, d) for (w, d) in DILATED_BRANCHES]
    m_all = jnp.max(jnp.stack([br[2] for br in branches]), axis=0)
    num_tot = jnp.zeros(q.shape, jnp.float32)
    den_tot = jnp.zeros(m_all.shape, jnp.float32)
    for num, den, mx in branches:
        wgt = jnp.exp(mx - m_all)
        num_tot = num_tot + wgt[..., None] * num
        den_tot = den_tot + wgt * den
    out = (num_tot / den_tot[..., None]).astype(xn.dtype).reshape(B, S, D_MODEL)
    return out @ w_o


def _fwd_setup_inputs(seed: int = 0) -> dict:
    key = jax.random.key(seed)
    ks = jax.random.split(key, 17)
    f32 = jnp.float32
    D, F = D_MODEL, D_FF

    def nrm(k, shape, scale):
        return jax.random.normal(k, shape, f32) * scale

    def gain(k, shape):
        return 1.0 + 0.02 * jax.random.normal(k, shape, f32)

    return {
        "x": nrm(ks[0], (BATCH, SEQ, D), 1.0),
        "a_norm": gain(ks[1], (N_A_LAYERS, D)),
        "a_w_in": nrm(ks[2], (N_A_LAYERS, D, 3 * D), D ** -0.5),
        "a_conv": nrm(ks[3], (N_A_LAYERS, CONV_WIDTH, D), CONV_WIDTH ** -0.5),
        "a_w_out": nrm(ks[4], (N_A_LAYERS, D, D), D ** -0.5),
        "kv_norm": gain(ks[5], (D,)),
        "w_kv": nrm(ks[6], (D, 2 * D), D ** -0.5),
        "b_norm": gain(ks[7], (N_B_LAYERS, D)),
        "b_w_q": nrm(ks[8], (N_B_LAYERS, D, D), D ** -0.5),
        "b_w_o": nrm(ks[9], (N_B_LAYERS, D, D), D ** -0.5),
        "rel_bias": nrm(ks[10], (REL_BUCKETS, N_HEADS), 0.5),
        "ffn_norm": gain(ks[11], (DEPTH, D)),
        "ffn_w_up": nrm(ks[12], (DEPTH, D, 2 * F), D ** -0.5),
        "ffn_conv": nrm(ks[13], (DEPTH, CONV_WIDTH, 2 * F), CONV_WIDTH ** -0.5),
        "ffn_conv_b": nrm(ks[14], (DEPTH, 2 * F), 0.02),
        "ffn_w_down": nrm(ks[15], (DEPTH, F, D), F ** -0.5),
        "final_norm": gain(ks[16], (D,)),
    }


def _fwd_reference(x, a_norm, a_w_in, a_conv, a_w_out, kv_norm, w_kv, b_norm, b_w_q, b_w_o, rel_bias,
              ffn_norm, ffn_w_up, ffn_conv, ffn_conv_b, ffn_w_down, final_norm):
    B, S, _ = x.shape
    h = x
    k = v = None
    for l in range(DEPTH):
        if l < N_A_LAYERS:
            h = h + short_conv_mixer(rmsnorm(h, a_norm[l]), a_w_in[l], a_conv[l], a_w_out[l])
        else:
            j = l - N_A_LAYERS
            h = h + dilated_attention(rmsnorm(h, b_norm[j]), b_w_q[j], b_w_o[j], k, v, rel_bias)
        h = h + conv_ffn(rmsnorm(h, ffn_norm[l]), ffn_w_up[l], ffn_conv[l], ffn_conv_b[l], ffn_w_down[l])
        if l == N_A_LAYERS - 1:
            k_flat, v_flat = jnp.split(rmsnorm(h, kv_norm) @ w_kv, 2, axis=-1)
            k = k_flat.reshape(B, S, N_HEADS, HEAD_DIM)
            v = v_flat.reshape(B, S, N_HEADS, HEAD_DIM)
    return rmsnorm(h, final_norm)


import jax as _jax
import jax.numpy as _jnp

TWIN_FORMAT = 'train_step'
FWD_PARAMS = ['x', 'a_norm', 'a_w_in', 'a_conv', 'a_w_out', 'kv_norm', 'w_kv', 'b_norm', 'b_w_q', 'b_w_o', 'rel_bias', 'ffn_norm', 'ffn_w_up', 'ffn_conv', 'ffn_conv_b', 'ffn_w_down', 'final_norm']
TWIN_WEIGHTS = ['a_norm', 'a_w_in', 'a_conv', 'a_w_out', 'kv_norm', 'w_kv', 'b_norm', 'b_w_q', 'b_w_o', 'rel_bias', 'ffn_norm', 'ffn_w_up', 'ffn_conv', 'ffn_conv_b', 'ffn_w_down', 'final_norm']
TWIN_DIFF_INPUT = 'x'
TWIN_INPUTS = ['x', 'a_norm', 'a_w_in', 'a_conv', 'a_w_out', 'kv_norm', 'w_kv', 'b_norm', 'b_w_q', 'b_w_o', 'rel_bias', 'ffn_norm', 'ffn_w_up', 'ffn_conv', 'ffn_conv_b', 'ffn_w_down', 'final_norm', 'loss_target', 'm_a_norm', 'm_a_w_in', 'm_a_conv', 'm_a_w_out', 'm_kv_norm', 'm_w_kv', 'm_b_norm', 'm_b_w_q', 'm_b_w_o', 'm_rel_bias', 'm_ffn_norm', 'm_ffn_w_up', 'm_ffn_conv', 'm_ffn_conv_b', 'm_ffn_w_down', 'm_final_norm', 'v_a_norm', 'v_a_w_in', 'v_a_conv', 'v_a_w_out', 'v_kv_norm', 'v_w_kv', 'v_b_norm', 'v_b_w_q', 'v_b_w_o', 'v_rel_bias', 'v_ffn_norm', 'v_ffn_w_up', 'v_ffn_conv', 'v_ffn_conv_b', 'v_ffn_w_down', 'v_final_norm']
TWIN_OUTPUTS = ['loss', 'grad_x', 'grad_a_norm', 'grad_a_w_in', 'grad_a_conv', 'grad_a_w_out', 'grad_kv_norm', 'grad_w_kv', 'grad_b_norm', 'grad_b_w_q', 'grad_b_w_o', 'grad_rel_bias', 'grad_ffn_norm', 'grad_ffn_w_up', 'grad_ffn_conv', 'grad_ffn_conv_b', 'grad_ffn_w_down', 'grad_final_norm', 'delta_a_norm', 'delta_a_w_in', 'delta_a_conv', 'delta_a_w_out', 'delta_kv_norm', 'delta_w_kv', 'delta_b_norm', 'delta_b_w_q', 'delta_b_w_o', 'delta_rel_bias', 'delta_ffn_norm', 'delta_ffn_w_up', 'delta_ffn_conv', 'delta_ffn_conv_b', 'delta_ffn_w_down', 'delta_final_norm', 'new_m_a_norm', 'new_m_a_w_in', 'new_m_a_conv', 'new_m_a_w_out', 'new_m_kv_norm', 'new_m_w_kv', 'new_m_b_norm', 'new_m_b_w_q', 'new_m_b_w_o', 'new_m_rel_bias', 'new_m_ffn_norm', 'new_m_ffn_w_up', 'new_m_ffn_conv', 'new_m_ffn_conv_b', 'new_m_ffn_w_down', 'new_m_final_norm', 'new_v_a_norm', 'new_v_a_w_in', 'new_v_a_conv', 'new_v_a_w_out', 'new_v_kv_norm', 'new_v_w_kv', 'new_v_b_norm', 'new_v_b_w_q', 'new_v_b_w_o', 'new_v_rel_bias', 'new_v_ffn_norm', 'new_v_ffn_w_up', 'new_v_ffn_conv', 'new_v_ffn_conv_b', 'new_v_ffn_w_down', 'new_v_final_norm']
TWIN_LEAF_KINDS = {'loss': 'loss', 'grad_x': 'grad_x', 'grad_a_norm': 'grad_w', 'grad_a_w_in': 'grad_w', 'grad_a_conv': 'grad_w', 'grad_a_w_out': 'grad_w', 'grad_kv_norm': 'grad_w', 'grad_w_kv': 'grad_w', 'grad_b_norm': 'grad_w', 'grad_b_w_q': 'grad_w', 'grad_b_w_o': 'grad_w', 'grad_rel_bias': 'grad_w', 'grad_ffn_norm': 'grad_w', 'grad_ffn_w_up': 'grad_w', 'grad_ffn_conv': 'grad_w', 'grad_ffn_conv_b': 'grad_w', 'grad_ffn_w_down': 'grad_w', 'grad_final_norm': 'grad_w', 'delta_a_norm': 'delta_w', 'delta_a_w_in': 'delta_w', 'delta_a_conv': 'delta_w', 'delta_a_w_out': 'delta_w', 'delta_kv_norm': 'delta_w', 'delta_w_kv': 'delta_w', 'delta_b_norm': 'delta_w', 'delta_b_w_q': 'delta_w', 'delta_b_w_o': 'delta_w', 'delta_rel_bias': 'delta_w', 'delta_ffn_norm': 'delta_w', 'delta_ffn_w_up': 'delta_w', 'delta_ffn_conv': 'delta_w', 'delta_ffn_conv_b': 'delta_w', 'delta_ffn_w_down': 'delta_w', 'delta_final_norm': 'delta_w', 'new_m_a_norm': 'new_m', 'new_m_a_w_in': 'new_m', 'new_m_a_conv': 'new_m', 'new_m_a_w_out': 'new_m', 'new_m_kv_norm': 'new_m', 'new_m_w_kv': 'new_m', 'new_m_b_norm': 'new_m', 'new_m_b_w_q': 'new_m', 'new_m_b_w_o': 'new_m', 'new_m_rel_bias': 'new_m', 'new_m_ffn_norm': 'new_m', 'new_m_ffn_w_up': 'new_m', 'new_m_ffn_conv': 'new_m', 'new_m_ffn_conv_b': 'new_m', 'new_m_ffn_w_down': 'new_m', 'new_m_final_norm': 'new_m', 'new_v_a_norm': 'new_v', 'new_v_a_w_in': 'new_v', 'new_v_a_conv': 'new_v', 'new_v_a_w_out': 'new_v', 'new_v_kv_norm': 'new_v', 'new_v_w_kv': 'new_v', 'new_v_b_norm': 'new_v', 'new_v_b_w_q': 'new_v', 'new_v_b_w_o': 'new_v', 'new_v_rel_bias': 'new_v', 'new_v_ffn_norm': 'new_v', 'new_v_ffn_w_up': 'new_v', 'new_v_ffn_conv': 'new_v', 'new_v_ffn_conv_b': 'new_v', 'new_v_ffn_w_down': 'new_v', 'new_v_final_norm': 'new_v'}


def _forward(args):
    return _fwd_reference(*[args[k] for k in FWD_PARAMS])


def _output_shape():
    out = _jax.eval_shape(lambda: _forward(_fwd_setup_inputs(0)))
    return out.shape, out.dtype

N_MICROBATCH = 1
ADAM_LR = 0.001
ADAM_B1 = 0.9
ADAM_B2 = 0.999
ADAM_EPS = 1e-08
ADAM_WD = 0.01
ADAM_STEP = 10
PER_EXAMPLE_BATCH_AXIS = {'x': 0, 'loss_target': 0}
SHARED_INPUTS = []
_WEIGHT_DTYPES = {'a_norm': _jnp.float32, 'a_w_in': _jnp.float32, 'a_conv': _jnp.float32, 'a_w_out': _jnp.float32, 'kv_norm': _jnp.float32, 'w_kv': _jnp.float32, 'b_norm': _jnp.float32, 'b_w_q': _jnp.float32, 'b_w_o': _jnp.float32, 'rel_bias': _jnp.float32, 'ffn_norm': _jnp.float32, 'ffn_w_up': _jnp.float32, 'ffn_conv': _jnp.float32, 'ffn_conv_b': _jnp.float32, 'ffn_w_down': _jnp.float32, 'final_norm': _jnp.float32}
MOMENT_SCALE = {'a_norm': 3.585491e-01, 'a_w_in': 2.072651e-01, 'a_conv': 2.147408e-01, 'a_w_out': 2.060846e-01, 'kv_norm': 5.365747e-02, 'w_kv': 3.847725e-02, 'b_norm': 3.352777e-02, 'b_w_q': 3.385321e-02, 'b_w_o': 4.221276e-02, 'rel_bias': 4.910118e-02, 'ffn_norm': 1.459793e-01, 'ffn_w_up': 6.281901e-02, 'ffn_conv': 6.370323e-02, 'ffn_conv_b': 6.137096e-02, 'ffn_w_down': 1.023815e-01, 'final_norm': 6.395694e+01}


def _to_microbatches(a, axis):
    t = _jnp.moveaxis(a, axis, 0)
    t = t.reshape((N_MICROBATCH, t.shape[0] // N_MICROBATCH) + t.shape[1:])
    return _jnp.moveaxis(t, 1, axis + 1)


def setup_inputs(seed: int = 0) -> dict:
    inp = _fwd_setup_inputs(seed)
    key = _jax.random.fold_in(_jax.random.key(seed), 7919)
    shape, _ = _output_shape()
    out = dict(inp)
    out["loss_target"] = _jax.random.normal(_jax.random.fold_in(key, 0), shape, _jnp.float32)
    for i, name in enumerate(TWIN_WEIGHTS):
        w = inp[name].astype(_jnp.float32)
        if MOMENT_SCALE is None:
            s = _jnp.sqrt(_jnp.mean(_jnp.square(w)) + 1e-30)
        else:
            s = MOMENT_SCALE[name]
        km, kv = _jax.random.split(_jax.random.fold_in(key, i + 1))
        out[name] = w
        out["m_" + name] = s * _jax.random.normal(km, w.shape, _jnp.float32)
        out["v_" + name] = (s * s) * _jax.random.uniform(kv, w.shape, _jnp.float32, 0.5, 1.5)
    if N_MICROBATCH > 1:
        for name, axis in PER_EXAMPLE_BATCH_AXIS.items():
            out[name] = _to_microbatches(out[name], axis)
    return {'x': out['x'], 'a_norm': out['a_norm'], 'a_w_in': out['a_w_in'], 'a_conv': out['a_conv'], 'a_w_out': out['a_w_out'], 'kv_norm': out['kv_norm'], 'w_kv': out['w_kv'], 'b_norm': out['b_norm'], 'b_w_q': out['b_w_q'], 'b_w_o': out['b_w_o'], 'rel_bias': out['rel_bias'], 'ffn_norm': out['ffn_norm'], 'ffn_w_up': out['ffn_w_up'], 'ffn_conv': out['ffn_conv'], 'ffn_conv_b': out['ffn_conv_b'], 'ffn_w_down': out['ffn_w_down'], 'final_norm': out['final_norm'], 'loss_target': out['loss_target'], 'm_a_norm': out['m_a_norm'], 'm_a_w_in': out['m_a_w_in'], 'm_a_conv': out['m_a_conv'], 'm_a_w_out': out['m_a_w_out'], 'm_kv_norm': out['m_kv_norm'], 'm_w_kv': out['m_w_kv'], 'm_b_norm': out['m_b_norm'], 'm_b_w_q': out['m_b_w_q'], 'm_b_w_o': out['m_b_w_o'], 'm_rel_bias': out['m_rel_bias'], 'm_ffn_norm': out['m_ffn_norm'], 'm_ffn_w_up': out['m_ffn_w_up'], 'm_ffn_conv': out['m_ffn_conv'], 'm_ffn_conv_b': out['m_ffn_conv_b'], 'm_ffn_w_down': out['m_ffn_w_down'], 'm_final_norm': out['m_final_norm'], 'v_a_norm': out['v_a_norm'], 'v_a_w_in': out['v_a_w_in'], 'v_a_conv': out['v_a_conv'], 'v_a_w_out': out['v_a_w_out'], 'v_kv_norm': out['v_kv_norm'], 'v_w_kv': out['v_w_kv'], 'v_b_norm': out['v_b_norm'], 'v_b_w_q': out['v_b_w_q'], 'v_b_w_o': out['v_b_w_o'], 'v_rel_bias': out['v_rel_bias'], 'v_ffn_norm': out['v_ffn_norm'], 'v_ffn_w_up': out['v_ffn_w_up'], 'v_ffn_conv': out['v_ffn_conv'], 'v_ffn_conv_b': out['v_ffn_conv_b'], 'v_ffn_w_down': out['v_ffn_w_down'], 'v_final_norm': out['v_final_norm']}


def _loss(weights, diff, rest, loss_target):
    with _jax.named_scope("forward"):
        args = {**rest, TWIN_DIFF_INPUT: diff, **{k: w.astype(_WEIGHT_DTYPES[k]) for k, w in weights.items()}}
        y = _forward(args)
    with _jax.named_scope("loss_head"):
        err = _jnp.square(y.astype(_jnp.float32) - loss_target)
        return 0.5 * _jnp.sum(_jnp.mean(err, axis=-1)) if err.ndim else 0.5 * err


def _adamw(w, g, m, v):
    m = ADAM_B1 * m + (1.0 - ADAM_B1) * g
    v = ADAM_B2 * v + (1.0 - ADAM_B2) * _jnp.square(g)
    m_hat = m / (1.0 - ADAM_B1 ** ADAM_STEP)
    v_hat = v / (1.0 - ADAM_B2 ** ADAM_STEP)
    delta = -ADAM_LR * (m_hat / (_jnp.sqrt(v_hat) + ADAM_EPS) + ADAM_WD * w)
    return delta, m, v


def reference(x, a_norm, a_w_in, a_conv, a_w_out, kv_norm, w_kv, b_norm, b_w_q, b_w_o, rel_bias, ffn_norm, ffn_w_up, ffn_conv, ffn_conv_b, ffn_w_down, final_norm, loss_target, m_a_norm, m_a_w_in, m_a_conv, m_a_w_out, m_kv_norm, m_w_kv, m_b_norm, m_b_w_q, m_b_w_o, m_rel_bias, m_ffn_norm, m_ffn_w_up, m_ffn_conv, m_ffn_conv_b, m_ffn_w_down, m_final_norm, v_a_norm, v_a_w_in, v_a_conv, v_a_w_out, v_kv_norm, v_w_kv, v_b_norm, v_b_w_q, v_b_w_o, v_rel_bias, v_ffn_norm, v_ffn_w_up, v_ffn_conv, v_ffn_conv_b, v_ffn_w_down, v_final_norm):
    given = dict(x=x, a_norm=a_norm, a_w_in=a_w_in, a_conv=a_conv, a_w_out=a_w_out, kv_norm=kv_norm, w_kv=w_kv, b_norm=b_norm, b_w_q=b_w_q, b_w_o=b_w_o, rel_bias=rel_bias, ffn_norm=ffn_norm, ffn_w_up=ffn_w_up, ffn_conv=ffn_conv, ffn_conv_b=ffn_conv_b, ffn_w_down=ffn_w_down, final_norm=final_norm, loss_target=loss_target, m_a_norm=m_a_norm, m_a_w_in=m_a_w_in, m_a_conv=m_a_conv, m_a_w_out=m_a_w_out, m_kv_norm=m_kv_norm, m_w_kv=m_w_kv, m_b_norm=m_b_norm, m_b_w_q=m_b_w_q, m_b_w_o=m_b_w_o, m_rel_bias=m_rel_bias, m_ffn_norm=m_ffn_norm, m_ffn_w_up=m_ffn_w_up, m_ffn_conv=m_ffn_conv, m_ffn_conv_b=m_ffn_conv_b, m_ffn_w_down=m_ffn_w_down, m_final_norm=m_final_norm, v_a_norm=v_a_norm, v_a_w_in=v_a_w_in, v_a_conv=v_a_conv, v_a_w_out=v_a_w_out, v_kv_norm=v_kv_norm, v_w_kv=v_w_kv, v_b_norm=v_b_norm, v_b_w_q=v_b_w_q, v_b_w_o=v_b_w_o, v_rel_bias=v_rel_bias, v_ffn_norm=v_ffn_norm, v_ffn_w_up=v_ffn_w_up, v_ffn_conv=v_ffn_conv, v_ffn_conv_b=v_ffn_conv_b, v_ffn_w_down=v_ffn_w_down, v_final_norm=v_final_norm)
    weights = {n: given[n] for n in TWIN_WEIGHTS}
    shared = {n: given[n] for n in SHARED_INPUTS}
    per_example = {n: given[n] for n in ['x']}
    grad_fn = _jax.value_and_grad(_loss, argnums=(0, 1))

    def one_microbatch(ex, loss_target):
        ex = dict(ex)
        diff = ex.pop(TWIN_DIFF_INPUT)
        return grad_fn(weights, diff, {**shared, **ex}, loss_target)

    if N_MICROBATCH == 1:
        loss, (grad_w, grad_x) = one_microbatch(per_example, given["loss_target"])
    else:
        def body(carry, xs):
            loss_sum, grad_sum = carry
            l_k, (gw_k, gx_k) = one_microbatch(xs[0], xs[1])
            with _jax.named_scope("update"):
                return (loss_sum + l_k, _jax.tree.map(_jnp.add, grad_sum, gw_k)), gx_k

        init = (_jnp.zeros((), _jnp.float32), _jax.tree.map(_jnp.zeros_like, weights))
        (loss, grad_w), grad_x = _jax.lax.scan(body, init, (per_example, given["loss_target"]))
    with _jax.named_scope("update"):
        delta_w, new_m, new_v = {}, {}, {}
        for n in TWIN_WEIGHTS:
            delta_w[n], new_m[n], new_v[n] = _adamw(weights[n], grad_w[n], given["m_" + n], given["v_" + n])
    return (loss, grad_x, *[grad_w[n] for n in TWIN_WEIGHTS], *[delta_w[n] for n in TWIN_WEIGHTS],
            *[new_m[n] for n in TWIN_WEIGHTS], *[new_v[n] for n in TWIN_WEIGHTS])
```

```python
import functools
import math

import jax
import jax.numpy as jnp
from jax import lax
from jax.experimental import pallas as pl
from jax.experimental.pallas import tpu as pltpu

F32 = jnp.float32
BF16 = jnp.bfloat16

N_DEV = 8
N_HEADS = 16
ATT_BLOCK = 128
DILATED_BRANCHES = ((128, 1), (512, 4), (2048, 16))
REL_BUCKETS = 32
REL_MAX_DISTANCE = 2048
RMS_EPS = 1e-6
ADAM_LR = 0.001
ADAM_B1 = 0.9
ADAM_B2 = 0.999
ADAM_EPS = 1e-08
ADAM_WD = 0.01
ADAM_STEP = 10

LANE = 128
HALO = 16
NEG = -1e30
VMEM_CAP = 56 << 20


def _vmem(block_bytes):
    return int(min(VMEM_CAP, max(32 << 20, 3 * block_bytes + (8 << 20))))


def _nbytes(shape, dtype):
    return math.prod(shape) * jnp.dtype(dtype).itemsize


def _tile(dim, target):
    best = None
    t = LANE
    while t <= min(dim, target):
        if dim % t == 0:
            best = t
        t += LANE
    return best if best is not None else dim


def _matmul(a, b, *, mode, out_dtype, name, add=None, a_parts=1, tm=1024, tn=512, tk=1024):
    P = a_parts
    if mode == "nn":
        M, K = (a.shape[0], a.shape[1]) if P == 1 else (a.shape[1], a.shape[2] * P)
        N = b.shape[1]
    elif mode == "nt":
        assert P == 1
        M, K = a.shape
        N = b.shape[0]
    else:
        K = a.shape[0] if P == 1 else a.shape[1]
        M = a.shape[1] if P == 1 else a.shape[2] * P
        N = b.shape[1]
    tm = _tile(M // P if mode == "tn" else M, tm)
    tn = _tile(N, tn)
    tk = _tile(K // P if mode == "nn" else K, tk)
    nm, nn_, nk = M // tm, N // tn, K // tk

    if mode == "nn":
        if P == 1:
            a_spec = pl.BlockSpec((tm, tk), lambda i, j, k: (i, k))
        else:
            nkp = nk // P
            a_spec = pl.BlockSpec((None, tm, tk), lambda i, j, k: (k // nkp, i, k % nkp))
        b_spec = pl.BlockSpec((tk, tn), lambda i, j, k: (k, j))
        dims = (((1,), (0,)), ((), ()))
    elif mode == "nt":
        a_spec = pl.BlockSpec((tm, tk), lambda i, j, k: (i, k))
        b_spec = pl.BlockSpec((tn, tk), lambda i, j, k: (j, k))
        dims = (((1,), (1,)), ((), ()))
    else:
        if P == 1:
            a_spec = pl.BlockSpec((tk, tm), lambda i, j, k: (k, i))
        else:
            nmp = nm // P
            a_spec = pl.BlockSpec((None, tk, tm), lambda i, j, k: (i // nmp, k, i % nmp))
        b_spec = pl.BlockSpec((tk, tn), lambda i, j, k: (k, j))
        dims = (((0,), (0,)), ((), ()))
    o_spec = pl.BlockSpec((tm, tn), lambda i, j, k: (i, j))
    in_specs = [a_spec, b_spec]
    args = [a, b]
    if add is not None:
        in_specs.append(o_spec)
        args.append(add)
    has_add = add is not None

    def body(a_ref, b_ref, *rest):
        if has_add:
            add_ref, o_ref, acc_ref = rest
        else:
            o_ref, acc_ref = rest
        k = pl.program_id(2)
        part = lax.dot_general(a_ref[...].astype(BF16), b_ref[...].astype(BF16), dims,
                               preferred_element_type=F32)

        def finish(r):
            if has_add:
                r = r + add_ref[...].astype(F32)
            o_ref[...] = r.astype(out_dtype)

        if nk == 1:
            finish(part)
        else:
            @pl.when(k == 0)
            def _():
                acc_ref[...] = part

            @pl.when(jnp.logical_and(k > 0, k < nk - 1))
            def _():
                acc_ref[...] += part

            @pl.when(k == nk - 1)
            def _():
                finish(acc_ref[...] + part)

    blk = (_nbytes((tm, tk), a.dtype) + _nbytes((tk, tn), b.dtype) + _nbytes((tm, tn), out_dtype)
           + (_nbytes((tm, tn), add.dtype) if has_add else 0)) * 2 + 3 * _nbytes((tm, tn), F32)
    return pl.pallas_call(
        body, name=name, grid=(nm, nn_, nk),
        in_specs=in_specs, out_specs=o_spec,
        out_shape=jax.ShapeDtypeStruct((M, N), out_dtype),
        scratch_shapes=[pltpu.VMEM((tm, tn), F32)],
        compiler_params=pltpu.CompilerParams(
            dimension_semantics=("parallel", "parallel", "arbitrary"), vmem_limit_bytes=_vmem(blk)),
    )(*args)


def _rmsnorm_fwd(x, gains, name, tm=512):
    T, D = x.shape
    n = gains.shape[0]
    tm = _tile(T, tm)

    def body(x_ref, g_ref, *o_refs):
        xv = x_ref[...]
        xhat = xv * lax.rsqrt(jnp.mean(xv * xv, axis=-1, keepdims=True) + RMS_EPS)
        for i in range(n):
            o_refs[i][...] = (xhat * g_ref[i:i + 1, :]).astype(BF16)

    row = pl.BlockSpec((tm, D), lambda i: (i, 0))
    outs = pl.pallas_call(
        body, name=name, grid=(T // tm,),
        in_specs=[row, pl.BlockSpec((n, D), lambda i: (0, 0))],
        out_specs=[row] * n,
        out_shape=[jax.ShapeDtypeStruct((T, D), BF16)] * n,
        compiler_params=pltpu.CompilerParams(
            dimension_semantics=("parallel",), vmem_limit_bytes=_vmem(4 * _nbytes((tm, D), F32))),
    )(x, gains)
    return tuple(outs)


def _rmsnorm_bwd(x, gains, dxns, dres, name, tm=512):
    T, D = x.shape
    n = gains.shape[0]
    tm = _tile(T, tm)

    def body(x_ref, g_ref, *rest):
        dxn_refs = rest[:n]
        dres_ref, dx_ref, dg_ref = rest[n:]
        xv = x_ref[...]
        rstd = lax.rsqrt(jnp.mean(xv * xv, axis=-1, keepdims=True) + RMS_EPS)
        xhat = xv * rstd
        dx = dres_ref[...]

        @pl.when(pl.program_id(0) == 0)
        def _():
            dg_ref[...] = jnp.zeros_like(dg_ref)

        for i in range(n):
            dy = dxn_refs[i][...].astype(F32)
            dg_ref[i:i + 1, :] += jnp.sum(dy * xhat, axis=0, keepdims=True)
            dxh = dy * g_ref[i:i + 1, :]
            dx = dx + rstd * (dxh - xhat * jnp.mean(dxh * xhat, axis=-1, keepdims=True))
        dx_ref[...] = dx

    row = pl.BlockSpec((tm, D), lambda i: (i, 0))
    par = pl.BlockSpec((n, D), lambda i: (0, 0))
    return pl.pallas_call(
        body, name=name, grid=(T // tm,),
        in_specs=[row, par] + [row] * n + [row],
        out_specs=[row, par],
        out_shape=[jax.ShapeDtypeStruct((T, D), F32), jax.ShapeDtypeStruct((n, D), F32)],
        compiler_params=pltpu.CompilerParams(
            dimension_semantics=("arbitrary",), vmem_limit_bytes=_vmem((4 + n) * _nbytes((tm, D), F32))),
    )(x, gains, *dxns, dres)


def _final_loss_bwd(h, gain, target, name, tm=512):
    T, D = h.shape
    tm = _tile(T, tm)

    def body(h_ref, g_ref, t_ref, dh_ref, dg_ref, sq_ref):
        xv = h_ref[...]
        rstd = lax.rsqrt(jnp.mean(xv * xv, axis=-1, keepdims=True) + RMS_EPS)
        xhat = xv * rstd
        err = xhat * g_ref[...] - t_ref[...]

        @pl.when(pl.program_id(0) == 0)
        def _():
            dg_ref[...] = jnp.zeros_like(dg_ref)
            sq_ref[...] = jnp.zeros_like(sq_ref)

        sq_ref[...] += jnp.sum(err * err, axis=0, keepdims=True)
        dy = err * (1.0 / D)
        dg_ref[...] += jnp.sum(dy * xhat, axis=0, keepdims=True)
        dxh = dy * g_ref[...]
        dh_ref[...] = rstd * (dxh - xhat * jnp.mean(dxh * xhat, axis=-1, keepdims=True))

    row = pl.BlockSpec((tm, D), lambda i: (i, 0))
    par = pl.BlockSpec((1, D), lambda i: (0, 0))
    return pl.pallas_call(
        body, name=name, grid=(T // tm,),
        in_specs=[row, par, row], out_specs=[row, par, par],
        out_shape=[jax.ShapeDtypeStruct((T, D), F32), jax.ShapeDtypeStruct((1, D), F32),
                   jax.ShapeDtypeStruct((1, D), F32)],
        compiler_params=pltpu.CompilerParams(
            dimension_semantics=("arbitrary",), vmem_limit_bytes=_vmem(5 * _nbytes((tm, D), F32))),
    )(h, gain, target)


def _halo_specs(S, ts, tc, col_off, prev, nxt):
    r = ts // HALO
    last = S // HALO - 1
    specs = []
    if prev:
        specs.append(pl.BlockSpec((None, HALO, tc), lambda j, b, s: (b, jnp.maximum(s * r - 1, 0), col_off + j)))
    specs.append(pl.BlockSpec((None, ts, tc), lambda j, b, s: (b, s, col_off + j)))
    if nxt:
        specs.append(pl.BlockSpec((None, HALO, tc), lambda j, b, s: (b, jnp.minimum((s + 1) * r, last), col_off + j)))
    return specs


def _ext(prev_ref, main_ref, next_ref, first, last):
    main = main_ref[...].astype(F32)
    zeros = jnp.zeros((HALO, main.shape[1]), F32)
    top = zeros if prev_ref is None else jnp.where(first, 0.0, prev_ref[...].astype(F32))
    bot = zeros if next_ref is None else jnp.where(last, 0.0, next_ref[...].astype(F32))
    return jnp.concatenate([top, main, bot], axis=0)


def _shift(xe, k):
    return pltpu.roll(xe, k % xe.shape[0], axis=0)


def _cconv(xe, w):
    return w[0:1, :] * _shift(xe, 2) + w[1:2, :] * _shift(xe, 1) + w[2:3, :] * xe


def _cconv_t(de, w):
    return w[2:3, :] * de + w[1:2, :] * _shift(de, -1) + w[0:1, :] * _shift(de, -2)


def _main(xe, ts):
    return xe[HALO:HALO + ts, :]


def _conv_wgrad(dw_ref, de, xe, ts):
    d = _main(de, ts)
    dw_ref[0:1, :] += jnp.sum(d * _main(_shift(xe, 2), ts), axis=0, keepdims=True)
    dw_ref[1:2, :] += jnp.sum(d * _main(_shift(xe, 1), ts), axis=0, keepdims=True)
    dw_ref[2:3, :] += jnp.sum(d * _main(xe, ts), axis=0, keepdims=True)


def _gate_tiles(S, C):
    return _tile(S, 512), _tile(C, 512)


def _shortconv_fwd(bch, conv_w, name):
    B, S, D3 = bch.shape
    D = D3 // 3
    ts, tc = _gate_tiles(S, D)
    nj, ns = D // tc, S // ts

    def body(b_ref, cp_ref, c_ref, hp_ref, h_ref, w_ref, o_ref):
        first = pl.program_id(2) == 0
        ce = _ext(cp_ref, c_ref, None, first, False)
        he = _ext(hp_ref, h_ref, None, first, False)
        cv = _main(_cconv(ce * he, w_ref[...]), ts)
        o_ref[...] = (b_ref[...].astype(F32) * cv).astype(BF16)

    in_specs = (_halo_specs(S, ts, tc, 0, False, False) + _halo_specs(S, ts, tc, nj, True, False)
                + _halo_specs(S, ts, tc, 2 * nj, True, False) + [pl.BlockSpec((3, tc), lambda j, b, s: (0, j))])
    return pl.pallas_call(
        body, name=name, grid=(nj, B, ns), in_specs=in_specs,
        out_specs=pl.BlockSpec((None, ts, tc), lambda j, b, s: (b, s, j)),
        out_shape=jax.ShapeDtypeStruct((B, S, D), BF16),
        compiler_params=pltpu.CompilerParams(
            dimension_semantics=("parallel", "parallel", "parallel"),
            vmem_limit_bytes=_vmem(12 * _nbytes((ts + 2 * HALO, tc), F32))),
    )(bch, bch, bch, bch, bch, conv_w)


def _shortconv_bwd(bch, dg, conv_w, name):
    B, S, D3 = bch.shape
    D = D3 // 3
    ts, tc = _gate_tiles(S, D)
    nj, ns = D // tc, S // ts

    def body(b_ref, bn_ref, cp_ref, c_ref, cn_ref, hp_ref, h_ref, hn_ref, d_ref, dn_ref, w_ref, o_ref, dw_ref):
        s = pl.program_id(2)
        first, last = s == 0, s == ns - 1

        @pl.when(jnp.logical_and(pl.program_id(1) == 0, s == 0))
        def _():
            dw_ref[...] = jnp.zeros_like(dw_ref)

        w = w_ref[...]
        be = _ext(None, b_ref, bn_ref, first, last)
        ce = _ext(cp_ref, c_ref, cn_ref, first, last)
        he = _ext(hp_ref, h_ref, hn_ref, first, last)
        de = _ext(None, d_ref, dn_ref, first, last)
        ch = ce * he
        dcv = de * be
        dch = _main(_cconv_t(dcv, w), ts)
        o_ref[0] = (_main(de, ts) * _main(_cconv(ch, w), ts)).astype(BF16)
        o_ref[1] = (dch * _main(he, ts)).astype(BF16)
        o_ref[2] = (dch * _main(ce, ts)).astype(BF16)
        _conv_wgrad(dw_ref, dcv, ch, ts)

    in_specs = (_halo_specs(S, ts, tc, 0, False, True) + _halo_specs(S, ts, tc, nj, True, True)
                + _halo_specs(S, ts, tc, 2 * nj, True, True) + _halo_specs(S, ts, tc, 0, False, True)
                + [pl.BlockSpec((3, tc), lambda j, b, s: (0, j))])
    return pl.pallas_call(
        body, name=name, grid=(nj, B, ns), in_specs=in_specs,
        out_specs=[pl.BlockSpec((3, None, ts, tc), lambda j, b, s: (0, b, s, j)),
                   pl.BlockSpec((3, tc), lambda j, b, s: (0, j))],
        out_shape=[jax.ShapeDtypeStruct((3, B, S, D), BF16), jax.ShapeDtypeStruct((3, D), F32)],
        compiler_params=pltpu.CompilerParams(
            dimension_semantics=("parallel", "arbitrary", "arbitrary"),
            vmem_limit_bytes=_vmem(24 * _nbytes((ts + 2 * HALO, tc), F32))),
    )(bch, bch, bch, bch, bch, bch, bch, bch, dg, dg, conv_w)


def _sigmoid(x):
    return 1.0 / (1.0 + jnp.exp(-x))


def _ffn_gate_fwd(u0, conv_w, conv_b, name):
    B, S, F2 = u0.shape
    F = F2 // 2
    ts, tc = _gate_tiles(S, F)
    nj, ns = F // tc, S // ts

    def body(gp_ref, g_ref, up_ref, u_ref, wg_ref, wu_ref, bg_ref, bu_ref, o_ref):
        first = pl.program_id(2) == 0
        ug = _main(_cconv(_ext(gp_ref, g_ref, None, first, False), wg_ref[...]), ts) + bg_ref[...]
        uu = _main(_cconv(_ext(up_ref, u_ref, None, first, False), wu_ref[...]), ts) + bu_ref[...]
        o_ref[...] = (ug * _sigmoid(ug) * uu).astype(BF16)

    in_specs = (_halo_specs(S, ts, tc, 0, True, False) + _halo_specs(S, ts, tc, nj, True, False)
                + [pl.BlockSpec((3, tc), lambda j, b, s: (0, j)), pl.BlockSpec((3, tc), lambda j, b, s: (0, nj + j)),
                   pl.BlockSpec((1, tc), lambda j, b, s: (0, j)), pl.BlockSpec((1, tc), lambda j, b, s: (0, nj + j))])
    return pl.pallas_call(
        body, name=name, grid=(nj, B, ns), in_specs=in_specs,
        out_specs=pl.BlockSpec((None, ts, tc), lambda j, b, s: (b, s, j)),
        out_shape=jax.ShapeDtypeStruct((B, S, F), BF16),
        compiler_params=pltpu.CompilerParams(
            dimension_semantics=("parallel", "parallel", "parallel"),
            vmem_limit_bytes=_vmem(12 * _nbytes((ts + 2 * HALO, tc), F32))),
    )(u0, u0, u0, u0, conv_w, conv_w, conv_b, conv_b)


def _ffn_gate_bwd(u0, dact, conv_w, conv_b, name):
    B, S, F2 = u0.shape
    F = F2 // 2
    ts, tc = _gate_tiles(S, F)
    nj, ns = F // tc, S // ts

    def body(gp_ref, g_ref, gn_ref, up_ref, u_ref, un_ref, d_ref, dn_ref, wg_ref, wu_ref, bg_ref, bu_ref,
             o_ref, dwg_ref, dwu_ref, dbg_ref, dbu_ref):
        s = pl.program_id(2)
        first, last = s == 0, s == ns - 1

        @pl.when(jnp.logical_and(pl.program_id(1) == 0, s == 0))
        def _():
            for r in (dwg_ref, dwu_ref, dbg_ref, dbu_ref):
                r[...] = jnp.zeros_like(r)

        wg, wu = wg_ref[...], wu_ref[...]
        ge = _ext(gp_ref, g_ref, gn_ref, first, last)
        ue = _ext(up_ref, u_ref, un_ref, first, last)
        de = _ext(None, d_ref, dn_ref, first, last)
        ug = _cconv(ge, wg) + bg_ref[...]
        uu = _cconv(ue, wu) + bu_ref[...]
        sig = _sigmoid(ug)
        dug = de * uu * (sig * (1.0 + ug * (1.0 - sig)))
        duu = de * (ug * sig)
        o_ref[0] = _main(_cconv_t(dug, wg), ts).astype(BF16)
        o_ref[1] = _main(_cconv_t(duu, wu), ts).astype(BF16)
        _conv_wgrad(dwg_ref, dug, ge, ts)
        _conv_wgrad(dwu_ref, duu, ue, ts)
        dbg_ref[...] += jnp.sum(_main(dug, ts), axis=0, keepdims=True)
        dbu_ref[...] += jnp.sum(_main(duu, ts), axis=0, keepdims=True)

    w3 = lambda off: pl.BlockSpec((3, tc), lambda j, b, s: (0, off + j))
    w1 = lambda off: pl.BlockSpec((1, tc), lambda j, b, s: (0, off + j))
    in_specs = (_halo_specs(S, ts, tc, 0, True, True) + _halo_specs(S, ts, tc, nj, True, True)
                + _halo_specs(S, ts, tc, 0, False, True) + [w3(0), w3(nj), w1(0), w1(nj)])
    outs = pl.pallas_call(
        body, name=name, grid=(nj, B, ns), in_specs=in_specs,
        out_specs=[pl.BlockSpec((2, None, ts, tc), lambda j, b, s: (0, b, s, j)), w3(0), w3(0), w1(0), w1(0)],
        out_shape=[jax.ShapeDtypeStruct((2, B, S, F), BF16), jax.ShapeDtypeStruct((3, F), F32),
                   jax.ShapeDtypeStruct((3, F), F32), jax.ShapeDtypeStruct((1, F), F32),
                   jax.ShapeDtypeStruct((1, F), F32)],
        compiler_params=pltpu.CompilerParams(
            dimension_semantics=("parallel", "arbitrary", "arbitrary"),
            vmem_limit_bytes=_vmem(30 * _nbytes((ts + 2 * HALO, tc), F32))),
    )(u0, u0, u0, u0, u0, u0, dact, dact, conv_w, conv_w, conv_b, conv_b)
    du0, dwg, dwu, dbg, dbu = outs
    return du0, jnp.concatenate([dwg, dwu], axis=1), jnp.concatenate([dbg, dbu], axis=1)


def _t5_bucket(dist):
    max_exact = REL_BUCKETS // 2
    n = jnp.maximum(dist, 0)
    nf = jnp.maximum(n, max_exact).astype(F32)
    large = max_exact + (jnp.log(nf / max_exact) / math.log(REL_MAX_DISTANCE / max_exact)
                         * (REL_BUCKETS - max_exact)).astype(jnp.int32)
    large = jnp.minimum(large, REL_BUCKETS - 1)
    return jnp.where(n < max_exact, n, large)


def _band_tables(window, dilation):
    P = ATT_BLOCK
    qi = jnp.arange(P, dtype=jnp.int32)[:, None]
    kc = jnp.arange(2 * P, dtype=jnp.int32)[None, :]
    delta = qi + P - kc
    band = (delta >= 0) & (delta <= window // dilation)
    bucket = _t5_bucket(delta * dilation).reshape(-1)
    onehot = (bucket[None, :] == jnp.arange(REL_BUCKETS, dtype=jnp.int32)[:, None]).astype(F32)
    return onehot, band


def _bias_lookup(rel_bias_t, onehots, name):
    nb, _, Q = onehots.shape
    H = rel_bias_t.shape[0]

    def body(r_ref, oh_ref, o_ref):
        o_ref[...] = lax.dot_general(r_ref[...], oh_ref[...], (((1,), (0,)), ((), ())),
                                     precision=lax.Precision.HIGHEST, preferred_element_type=F32)

    return pl.pallas_call(
        body, name=name, grid=(nb,),
        in_specs=[pl.BlockSpec((H, REL_BUCKETS), lambda i: (0, 0)),
                  pl.BlockSpec((None, REL_BUCKETS, Q), lambda i: (i, 0, 0))],
        out_specs=pl.BlockSpec((None, H, Q), lambda i: (i, 0, 0)),
        out_shape=jax.ShapeDtypeStruct((nb, H, Q), F32),
        compiler_params=pltpu.CompilerParams(dimension_semantics=("parallel",),
                                             vmem_limit_bytes=_vmem(_nbytes((REL_BUCKETS + H, Q), F32))),
    )(rel_bias_t, onehots)


def _bias_grad(dtabs, onehots, name):
    nb, H, Q = dtabs.shape

    def body(d_ref, oh_ref, o_ref):
        @pl.when(pl.program_id(0) == 0)
        def _():
            o_ref[...] = jnp.zeros_like(o_ref)

        o_ref[...] += lax.dot_general(d_ref[...], oh_ref[...], (((1,), (1,)), ((), ())),
                                      precision=lax.Precision.HIGHEST, preferred_element_type=F32)

    return pl.pallas_call(
        body, name=name, grid=(nb,),
        in_specs=[pl.BlockSpec((None, H, Q), lambda i: (i, 0, 0)),
                  pl.BlockSpec((None, REL_BUCKETS, Q), lambda i: (i, 0, 0))],
        out_specs=pl.BlockSpec((H, REL_BUCKETS), lambda i: (0, 0)),
        out_shape=jax.ShapeDtypeStruct((H, REL_BUCKETS), F32),
        compiler_params=pltpu.CompilerParams(dimension_semantics=("arbitrary",),
                                             vmem_limit_bytes=_vmem(_nbytes((REL_BUCKETS + H, Q), F32))),
    )(dtabs, onehots)


def _lane_masks():
    lane = lax.broadcasted_iota(jnp.int32, (1, LANE), 1)
    lo = lane < LANE // 2
    return lo, jnp.logical_not(lo)


def _dot_nt(a, b):
    return lax.dot_general(a, b, (((1,), (1,)), ((), ())), preferred_element_type=F32)


def _dot_nn(a, b):
    return lax.dot_general(a, b, (((1,), (0,)), ((), ())), preferred_element_type=F32)


def _dot_tn(a, b):
    return lax.dot_general(a, b, (((0,), (0,)), ((), ())), preferred_element_type=F32)


def _attn_fwd_branch(q, kv, tab, dilation, name):
    B, S, D = q.shape
    P, d = ATT_BLOCK, dilation
    L = S // d
    nb = L // P
    H = N_HEADS
    scale = (D // H) ** -0.5
    qv = q.reshape(B, L, d * D)
    kvv = kv.reshape(B, L, d * 2 * D)

    def body(q_ref, kp_ref, kc_ref, vp_ref, vc_ref, tab_ref, o_ref, lse_ref):
        lo, hi = _lane_masks()
        for p in range(H // 2):
            sl = slice(p * LANE, (p + 1) * LANE)
            qp = q_ref[:, sl]
            kp = jnp.concatenate([kp_ref[:, sl], kc_ref[:, sl]], axis=0)
            vp = jnp.concatenate([vp_ref[:, sl], vc_ref[:, sl]], axis=0)
            o_pair = None
            lse_pair = None
            for e, msk in enumerate((lo, hi)):
                s = _dot_nt(jnp.where(msk, qp, 0), kp) * scale + tab_ref[2 * p + e]
                m = jnp.max(s, axis=-1, keepdims=True)
                pe = jnp.exp(s - m)
                den = jnp.sum(pe, axis=-1, keepdims=True)
                o_e = _dot_nn(pe.astype(BF16), jnp.where(msk, vp, 0)) / den
                lse_e = m + jnp.log(den)
                o_pair = o_e if e == 0 else o_pair + o_e
                lse_pair = lse_e if e == 0 else jnp.where(lo, lse_pair, lse_e)
            o_ref[:, sl] = o_pair
            lse_ref[:, sl] = lse_pair

    blk = lambda f: pl.BlockSpec((None, P, D), f)
    prev = lambda i: jnp.maximum(i - 1, 0)
    in_specs = [blk(lambda b, r, i: (b, i, r)),
                blk(lambda b, r, i: (b, prev(i), 2 * r)), blk(lambda b, r, i: (b, i, 2 * r)),
                blk(lambda b, r, i: (b, prev(i), 2 * r + 1)), blk(lambda b, r, i: (b, i, 2 * r + 1)),
                pl.BlockSpec((None, H, P, 2 * P), lambda b, r, i: (jnp.minimum(i, 1), 0, 0, 0))]
    o, lse = pl.pallas_call(
        body, name=name, grid=(B, d, nb), in_specs=in_specs,
        out_specs=[blk(lambda b, r, i: (b, i, r))] * 2,
        out_shape=[jax.ShapeDtypeStruct((B, L, d * D), F32)] * 2,
        compiler_params=pltpu.CompilerParams(
            dimension_semantics=("parallel", "parallel", "parallel"),
            vmem_limit_bytes=_vmem(_nbytes((H, P, 2 * P), F32) + 9 * _nbytes((P, D), F32))),
    )(qv, kvv, kvv, kvv, kvv, tab)
    return o.reshape(B * S, D), lse.reshape(B * S, D)


def _attn_merge(os_, lses, name, tm=512):
    T, D = os_[0].shape
    n = len(os_)
    tm = _tile(T, tm)

    def body(*refs):
        o_refs, l_refs = refs[:n], refs[n:2 * n]
        out_ref, lse_ref = refs[2 * n:]
        ls = [r[...] for r in l_refs]
        m = functools.reduce(jnp.maximum, ls)
        ws = [jnp.exp(l - m) for l in ls]
        tot = functools.reduce(jnp.add, ws)
        acc = functools.reduce(jnp.add, [w * r[...] for w, r in zip(ws, o_refs)])
        out_ref[...] = (acc / tot).astype(BF16)
        lse_ref[...] = m + jnp.log(tot)

    row = pl.BlockSpec((tm, D), lambda i: (i, 0))
    return pl.pallas_call(
        body, name=name, grid=(T // tm,), in_specs=[row] * (2 * n), out_specs=[row, row],
        out_shape=[jax.ShapeDtypeStruct((T, D), BF16), jax.ShapeDtypeStruct((T, D), F32)],
        compiler_params=pltpu.CompilerParams(
            dimension_semantics=("parallel",), vmem_limit_bytes=_vmem((2 * n + 4) * _nbytes((tm, D), F32))),
    )(*os_, *lses)


def _attn_bwd_q(q, kv, do, o, lse, tab, dq_acc, dilation, name):
    B, S, D = q.shape
    P, d = ATT_BLOCK, dilation
    L = S // d
    nb = L // P
    H = N_HEADS
    scale = (D // H) ** -0.5
    view = lambda t, c: t.reshape(B, L, d * c)
    has_acc = dq_acc is not None

    def body(q_ref, kp_ref, kc_ref, vp_ref, vc_ref, do_ref, o_ref, lse_ref, tab_ref, *rest):
        if has_acc:
            acc_ref, dq_ref, dtab_ref = rest
        else:
            dq_ref, dtab_ref = rest

        @pl.when((pl.program_id(0) == 0) & (pl.program_id(1) == 0) & (pl.program_id(2) == 0))
        def _():
            dtab_ref[...] = jnp.zeros_like(dtab_ref)

        lo, hi = _lane_masks()
        for p in range(H // 2):
            sl = slice(p * LANE, (p + 1) * LANE)
            qp = q_ref[:, sl]
            kp = jnp.concatenate([kp_ref[:, sl], kc_ref[:, sl]], axis=0)
            vp = jnp.concatenate([vp_ref[:, sl], vc_ref[:, sl]], axis=0)
            dop = do_ref[:, sl]
            prod = dop.astype(F32) * o_ref[:, sl].astype(F32)
            lsep = lse_ref[:, sl]
            dq_pair = None
            for e, msk in enumerate((lo, hi)):
                s = _dot_nt(jnp.where(msk, qp, 0), kp) * scale + tab_ref[2 * p + e]
                pe = jnp.exp(s - lsep[:, e * (LANE // 2):e * (LANE // 2) + 1])
                dp = _dot_nt(jnp.where(msk, dop, 0).astype(BF16), vp)
                delta = jnp.sum(jnp.where(msk, prod, 0.0), axis=-1, keepdims=True)
                ds = pe * (dp - delta)
                dtab_ref[2 * p + e] += ds
                dq_e = _dot_nn(ds.astype(BF16), jnp.where(msk, kp, 0))
                dq_pair = dq_e if e == 0 else dq_pair + dq_e
            dq_pair = dq_pair * scale
            if has_acc:
                dq_pair = dq_pair + acc_ref[:, sl]
            dq_ref[:, sl] = dq_pair

    blk = lambda f: pl.BlockSpec((None, P, D), f)
    prev = lambda i: jnp.maximum(i - 1, 0)
    own = blk(lambda b, r, i: (b, i, r))
    tab_spec = pl.BlockSpec((None, H, P, 2 * P), lambda b, r, i: (jnp.minimum(i, 1), 0, 0, 0))
    in_specs = [own,
                blk(lambda b, r, i: (b, prev(i), 2 * r)), blk(lambda b, r, i: (b, i, 2 * r)),
                blk(lambda b, r, i: (b, prev(i), 2 * r + 1)), blk(lambda b, r, i: (b, i, 2 * r + 1)),
                own, own, own, tab_spec]
    kvv = view(kv, 2 * D)
    args = [view(q, D), kvv, kvv, kvv, kvv, view(do, D), view(o, D), view(lse, D), tab]
    if has_acc:
        in_specs.append(own)
        args.append(view(dq_acc, D))
    dq, dtab = pl.pallas_call(
        body, name=name, grid=(B, d, nb), in_specs=in_specs,
        out_specs=[own, pl.BlockSpec((H, P, 2 * P), lambda b, r, i: (0, 0, 0))],
        out_shape=[jax.ShapeDtypeStruct((B, L, d * D), F32), jax.ShapeDtypeStruct((H, P, 2 * P), F32)],
        compiler_params=pltpu.CompilerParams(
            dimension_semantics=("arbitrary", "arbitrary", "arbitrary"),
            vmem_limit_bytes=_vmem(2 * _nbytes((H, P, 2 * P), F32) + 12 * _nbytes((P, D), F32))),
    )(*args)
    return dq.reshape(B, S, D), dtab


def _attn_bwd_kv(q, kv, do, o, lse, tabk, dkv_acc, dilation, name):
    B, S, D = q.shape
    P, d = ATT_BLOCK, dilation
    L = S // d
    nb = L // P
    H = N_HEADS
    scale = (D // H) ** -0.5
    view = lambda t, c: t.reshape(B, L, d * c)
    has_acc = dkv_acc is not None

    def body(k_ref, v_ref, qa_ref, qb_ref, doa_ref, dob_ref, oa_ref, ob_ref, la_ref, lb_ref, tab_ref, *rest):
        if has_acc:
            acc_ref, dkv_ref = rest
        else:
            (dkv_ref,) = rest
        lo, hi = _lane_masks()
        for p in range(H // 2):
            sl = slice(p * LANE, (p + 1) * LANE)
            kp, vp = k_ref[:, sl], v_ref[:, sl]
            q2 = jnp.concatenate([qa_ref[:, sl], qb_ref[:, sl]], axis=0)
            do2 = jnp.concatenate([doa_ref[:, sl], dob_ref[:, sl]], axis=0)
            o2 = jnp.concatenate([oa_ref[:, sl], ob_ref[:, sl]], axis=0)
            lse2 = jnp.concatenate([la_ref[:, sl], lb_ref[:, sl]], axis=0)
            prod = do2.astype(F32) * o2.astype(F32)
            dk_pair = None
            dv_pair = None
            for e, msk in enumerate((lo, hi)):
                qm = jnp.where(msk, q2, 0)
                dom = jnp.where(msk, do2, 0).astype(BF16)
                s = _dot_nt(qm, kp) * scale + tab_ref[2 * p + e]
                pe = jnp.exp(s - lse2[:, e * (LANE // 2):e * (LANE // 2) + 1])
                dp = _dot_nt(dom, vp)
                delta = jnp.sum(jnp.where(msk, prod, 0.0), axis=-1, keepdims=True)
                ds = pe * (dp - delta)
                dv_e = _dot_tn(pe.astype(BF16), dom)
                dk_e = _dot_tn(ds.astype(BF16), qm)
                dk_pair = dk_e if e == 0 else dk_pair + dk_e
                dv_pair = dv_e if e == 0 else dv_pair + dv_e
            dk_pair = dk_pair * scale
            if has_acc:
                dk_pair = dk_pair + acc_ref[0, :, sl]
                dv_pair = dv_pair + acc_ref[1, :, sl]
            dkv_ref[0, :, sl] = dk_pair
            dkv_ref[1, :, sl] = dv_pair

    blk = lambda f: pl.BlockSpec((None, P, D), f)
    nxt = lambda i: jnp.minimum(i + 1, nb - 1)
    own = blk(lambda b, r, i: (b, i, r))
    nx = blk(lambda b, r, i: (b, nxt(i), r))
    tab_spec = pl.BlockSpec((None, H, 2 * P, P), lambda b, r, i: (jnp.where(i == nb - 1, 1, 0), 0, 0, 0))
    pair = pl.BlockSpec((2, None, P, D), lambda b, r, i: (0, b, i, r))
    in_specs = [blk(lambda b, r, i: (b, i, 2 * r)), blk(lambda b, r, i: (b, i, 2 * r + 1)),
                own, nx, own, nx, own, nx, own, nx, tab_spec]
    kvv, qv, dov, ov, lv = view(kv, 2 * D), view(q, D), view(do, D), view(o, D), view(lse, D)
    args = [kvv, kvv, qv, qv, dov, dov, ov, ov, lv, lv, tabk]
    if has_acc:
        in_specs.append(pair)
        args.append(dkv_acc.reshape(2, B, L, d * D))
    dkv = pl.pallas_call(
        body, name=name, grid=(B, d, nb), in_specs=in_specs, out_specs=pair,
        out_shape=jax.ShapeDtypeStruct((2, B, L, d * D), F32),
        compiler_params=pltpu.CompilerParams(
            dimension_semantics=("parallel", "parallel", "parallel"),
            vmem_limit_bytes=_vmem(_nbytes((H, P, 2 * P), F32) + 16 * _nbytes((P, D), F32))),
    )(*args)
    return dkv.reshape(2, B, S, D)


def _exchange(gathered, scattered, name):
    ng, ns = len(gathered), len(scattered)
    n = ng + ns

    def body(*refs):
        ins, outs = refs[:n], refs[n:2 * n]
        send_sems, recv_sems, loc_sems = refs[2 * n:]
        x, y, c = lax.axis_index("x"), lax.axis_index("y"), lax.axis_index("c")
        me = 4 * x + 2 * y + c
        peers = []
        for k in range(1, N_DEV):
            px = 1 - x if k & 4 else x
            py = 1 - y if k & 2 else y
            pc = 1 - c if k & 1 else c
            peers.append(((px, py, pc), 4 * px + 2 * py + pc))

        def src(a, dev):
            return ins[a] if a < ng else ins[a].at[dev]

        local = []
        sends = []
        for a in range(n):
            cp = pltpu.make_async_copy(src(a, me), outs[a].at[me], loc_sems.at[a])
            cp.start()
            local.append(cp)
            for k, (peer, pid) in enumerate(peers):
                cp = pltpu.make_async_remote_copy(
                    src_ref=src(a, pid), dst_ref=outs[a].at[me],
                    send_sem=send_sems.at[a, k], recv_sem=recv_sems.at[a, k],
                    device_id=peer, device_id_type=pl.DeviceIdType.MESH)
                cp.start()
                sends.append(cp)
        for cp in sends:
            cp.wait_send()
        for a in range(n):
            for k, (peer, pid) in enumerate(peers):
                pltpu.make_async_remote_copy(
                    src_ref=src(a, pid), dst_ref=outs[a].at[pid],
                    send_sem=send_sems.at[a, k], recv_sem=recv_sems.at[a, k],
                    device_id=peer, device_id_type=pl.DeviceIdType.MESH).wait_recv()
        for cp in local:
            cp.wait()

    out_shape = ([jax.ShapeDtypeStruct((N_DEV,) + g.shape, g.dtype) for g in gathered]
                 + [jax.ShapeDtypeStruct(s.shape, s.dtype) for s in scattered])
    any_spec = pl.BlockSpec(memory_space=pl.ANY)
    outs = pl.pallas_call(
        body, name=name, in_specs=[any_spec] * n, out_specs=[any_spec] * n, out_shape=out_shape,
        scratch_shapes=[pltpu.SemaphoreType.DMA((n, N_DEV - 1)), pltpu.SemaphoreType.DMA((n, N_DEV - 1)),
                        pltpu.SemaphoreType.DMA((n,))],
        compiler_params=pltpu.CompilerParams(has_side_effects=True),
    )(*gathered, *scattered)
    return list(outs[:ng]), list(outs[ng:])


def _row_tile(rows, target):
    best = rows
    for t in range(8, min(rows, target) + 1, 8):
        if rows % t == 0:
            best = t
    return best


def _sum_slots(recv, name, tr=128):
    _, R, C = recv.shape
    tr = _row_tile(R, tr)

    def body(r_ref, o_ref):
        acc = r_ref[0].astype(F32)
        for k in range(1, N_DEV):
            acc = acc + r_ref[k].astype(F32)
        o_ref[...] = acc

    return pl.pallas_call(
        body, name=name, grid=(R // tr,),
        in_specs=[pl.BlockSpec((N_DEV, tr, C), lambda i: (0, i, 0))],
        out_specs=pl.BlockSpec((tr, C), lambda i: (i, 0)),
        out_shape=jax.ShapeDtypeStruct((R, C), F32),
        compiler_params=pltpu.CompilerParams(
            dimension_semantics=("parallel",), vmem_limit_bytes=_vmem(10 * _nbytes((tr, C), F32))),
    )(recv)


def _adamw(g, w, m, v, name, tr=256):
    R, C = g.shape
    tr = _row_tile(R, tr)

    def body(g_ref, w_ref, m_ref, v_ref, d_ref, nm_ref, nv_ref):
        gv = g_ref[...]
        nm = ADAM_B1 * m_ref[...] + (1.0 - ADAM_B1) * gv
        nv = ADAM_B2 * v_ref[...] + (1.0 - ADAM_B2) * (gv * gv)
        m_hat = nm / (1.0 - ADAM_B1 ** ADAM_STEP)
        v_hat = nv / (1.0 - ADAM_B2 ** ADAM_STEP)
        d_ref[...] = -ADAM_LR * (m_hat / (jnp.sqrt(v_hat) + ADAM_EPS) + ADAM_WD * w_ref[...])
        nm_ref[...] = nm
        nv_ref[...] = nv

    row = pl.BlockSpec((tr, C), lambda i: (i, 0))
    return pl.pallas_call(
        body, name=name, grid=(R // tr,), in_specs=[row] * 4, out_specs=[row] * 3,
        out_shape=[jax.ShapeDtypeStruct((R, C), F32)] * 3,
        compiler_params=pltpu.CompilerParams(
            dimension_semantics=("parallel",), vmem_limit_bytes=_vmem(8 * _nbytes((tr, C), F32))),
    )(g, w, m, v)


def _pack_rows(parts):
    flat = jnp.concatenate([p.reshape(-1).astype(F32) for p in parts])
    rows = -(-flat.shape[0] // (8 * LANE)) * 8
    return jnp.pad(flat, (0, rows * LANE - flat.shape[0])).reshape(rows, LANE)


def _unpack(flat, shapes):
    out, off = [], 0
    for shp in shapes:
        n = math.prod(shp)
        out.append(flat[off:off + n].reshape(shp))
        off += n
    return out


def kernel(x, a_norm, a_w_in, a_conv, a_w_out, kv_norm, w_kv, b_norm, b_w_q, b_w_o, rel_bias, ffn_norm, ffn_w_up, ffn_conv, ffn_conv_b, ffn_w_down, final_norm, loss_target, m_a_norm, m_a_w_in, m_a_conv, m_a_w_out, m_kv_norm, m_w_kv, m_b_norm, m_b_w_q, m_b_w_o, m_rel_bias, m_ffn_norm, m_ffn_w_up, m_ffn_conv, m_ffn_conv_b, m_ffn_w_down, m_final_norm, v_a_norm, v_a_w_in, v_a_conv, v_a_w_out, v_kv_norm, v_w_kv, v_b_norm, v_b_w_q, v_b_w_o, v_rel_bias, v_ffn_norm, v_ffn_w_up, v_ffn_conv, v_ffn_conv_b, v_ffn_w_down, v_final_norm):
    B, S, D = x.shape
    T = B * S
    F = ffn_w_down.shape[1] * N_DEV
    me = 4 * lax.axis_index("x") + 2 * lax.axis_index("y") + lax.axis_index("c")

    big_shards = [a_w_in[0].T, a_w_out[0], w_kv.T, b_w_q[0], b_w_o[0],
                  ffn_w_up[0].T, ffn_w_up[1].T, ffn_w_down[0], ffn_w_down[1]]
    small_shapes = [a_norm.shape, a_conv.shape, ffn_conv.shape]
    small_pack = _pack_rows([a_norm, a_conv, ffn_conv])
    gathered, _ = _exchange([s.astype(BF16) for s in big_shards] + [small_pack], [], "gather_weights")
    win_t, wout, wkv_t, wq, wo, wup0_t, wup1_t, wdn0, wdn1 = [
        g.reshape(-1, D) for g in gathered[:9]]
    wup_t, wdn = (wup0_t, wup1_t), (wdn0, wdn1)
    smalls = [_unpack(gathered[9][j].reshape(-1), small_shapes) for j in range(N_DEV)]
    a_norm_f = jnp.concatenate([s[0] for s in smalls], axis=-1)
    a_conv_f = jnp.concatenate([s[1] for s in smalls], axis=-1)[0]
    ffn_conv_f = jnp.concatenate([s[2] for s in smalls], axis=-1)

    x2 = x.reshape(T, D)
    (xn0,) = _rmsnorm_fwd(x2, a_norm_f, "a_norm_fwd")
    bch = _matmul(xn0, win_t, mode="nt", out_dtype=BF16, name="a_in_proj").reshape(B, S, 3 * D)
    gated = _shortconv_fwd(bch, a_conv_f, "a_gate_fwd").reshape(T, D)
    h1 = _matmul(gated, wout, mode="nn", out_dtype=F32, add=x2, name="a_out_proj")

    def ffn_fwd(h, l):
        (xn,) = _rmsnorm_fwd(h, ffn_norm[l:l + 1], f"ffn{l}_norm_fwd")
        u0 = _matmul(xn, wup_t[l], mode="nt", out_dtype=BF16, name=f"ffn{l}_up").reshape(B, S, 2 * F)
        act = _ffn_gate_fwd(u0, ffn_conv_f[l], ffn_conv_b[l:l + 1], f"ffn{l}_gate_fwd").reshape(T, F)
        out = _matmul(act, wdn[l], mode="nn", out_dtype=F32, add=h, name=f"ffn{l}_down", tk=1408)
        return out, (xn, u0, act)

    h2, ffn0_saved = ffn_fwd(h1, 0)
    kvn, xnb = _rmsnorm_fwd(h2, jnp.stack([kv_norm, b_norm[0]]), "kv_b_norm_fwd")
    kv = _matmul(kvn, wkv_t, mode="nt", out_dtype=BF16, name="kv_proj").reshape(B, S, 2 * D)
    q = _matmul(xnb, wq, mode="nn", out_dtype=BF16, name="q_proj").reshape(B, S, D)

    tables = [_band_tables(w, d) for (w, d) in DILATED_BRANCHES]
    onehots = jnp.stack([t[0] for t in tables])
    P = ATT_BLOCK
    bias_vals = _bias_lookup(rel_bias.T, onehots, "rel_bias_lookup").reshape(3, N_HEADS, P, 2 * P)
    in_cur = (jnp.arange(2 * P) >= P)[None, :]
    tabs, tabks = [], []
    for bi in range(3):
        band = tables[bi][1]
        gen = jnp.where(band[None], bias_vals[bi], NEG)
        fst = jnp.where((band & in_cur)[None], bias_vals[bi], NEG)
        tabs.append(jnp.stack([fst, gen]))
        keyed = jnp.concatenate([gen[:, :, P:], gen[:, :, :P]], axis=1)
        tabks.append(jnp.stack([keyed, keyed.at[:, P:, :].set(NEG)]))

    br_o, br_lse = [], []
    for bi, (_, d) in enumerate(DILATED_BRANCHES):
        o_b, lse_b = _attn_fwd_branch(q, kv, tabs[bi], d, f"attn_fwd_d{d}")
        br_o.append(o_b)
        br_lse.append(lse_b)
    attn, lse = _attn_merge(br_o, br_lse, "attn_merge")
    h3 = _matmul(attn, wo, mode="nn", out_dtype=F32, add=h2, name="o_proj")
    h4, ffn1_saved = ffn_fwd(h3, 1)

    dh4, d_final_norm, sq = _final_loss_bwd(h4, final_norm.reshape(1, D), loss_target.reshape(T, D), "loss_bwd")
    loss_part = 0.5 * jnp.sum(sq) / D

    def ffn_bwd(dh_out, h_in, saved, l):
        xn, u0, act = saved
        dact = _matmul(dh_out, wdn[l], mode="nt", out_dtype=BF16, name=f"ffn{l}_down_dx", tk=1024).reshape(B, S, F)
        d_wdn = _matmul(act, dh_out, mode="tn", out_dtype=F32, name=f"ffn{l}_down_dw", tm=1408, tn=1024, tk=512)
        du0, d_conv, d_conv_b = _ffn_gate_bwd(u0, dact, ffn_conv_f[l], ffn_conv_b[l:l + 1], f"ffn{l}_gate_bwd")
        du0 = du0.reshape(2, T, F)
        dxn = _matmul(du0, wup_t[l], mode="nn", out_dtype=F32, a_parts=2, name=f"ffn{l}_up_dx", tk=1408)
        d_wup_t = _matmul(du0, xn, mode="tn", out_dtype=F32, a_parts=2, name=f"ffn{l}_up_dw", tm=1408, tn=1024, tk=512)
        dh_in, d_norm = _rmsnorm_bwd(h_in, ffn_norm[l:l + 1], [dxn], dh_out, f"ffn{l}_norm_bwd")
        return dh_in, (d_wdn, d_wup_t, d_conv, d_conv_b, d_norm)

    dh3, ffn1_grads = ffn_bwd(dh4, h3, ffn1_saved, 1)

    dattn = _matmul(dh3, wo, mode="nt", out_dtype=BF16, name="o_proj_dx").reshape(B, S, D)
    d_wo = _matmul(attn, dh3, mode="tn", out_dtype=F32, name="o_proj_dw", tm=512, tn=1024, tk=512)
    attn3, lse3 = attn.reshape(B, S, D), lse.reshape(B, S, D)
    dq, dkv, dtabs = None, None, []
    for bi, (_, d) in enumerate(DILATED_BRANCHES):
        dq, dtab = _attn_bwd_q(q, kv, dattn, attn3, lse3, tabs[bi], dq, d, f"attn_bwd_q_d{d}")
        dkv = _attn_bwd_kv(q, kv, dattn, attn3, lse3, tabks[bi], dkv, d, f"attn_bwd_kv_d{d}")
        dtabs.append(dtab.reshape(N_HEADS, P * 2 * P))
    d_rel_bias = _bias_grad(jnp.stack(dtabs), onehots, "rel_bias_grad").T
    dq2 = dq.reshape(T, D)
    dkv2 = dkv.reshape(2, T, D)
    dxnb = _matmul(dq2, wq, mode="nt", out_dtype=F32, name="q_proj_dx")
    d_wq = _matmul(xnb, dq2, mode="tn", out_dtype=F32, name="q_proj_dw", tm=512, tn=1024, tk=512)
    dkvn = _matmul(dkv2, wkv_t, mode="nn", out_dtype=F32, a_parts=2, name="kv_proj_dx")
    d_wkv_t = _matmul(dkv2, kvn, mode="tn", out_dtype=F32, a_parts=2, name="kv_proj_dw", tm=512, tn=1024, tk=512)
    dh2, d_kvb_norm = _rmsnorm_bwd(h2, jnp.stack([kv_norm, b_norm[0]]), [dkvn, dxnb], dh3, "kv_b_norm_bwd")

    dh1, ffn0_grads = ffn_bwd(dh2, h1, ffn0_saved, 0)

    dgated = _matmul(dh1, wout, mode="nt", out_dtype=BF16, name="a_out_proj_dx").reshape(B, S, D)
    d_wout = _matmul(gated, dh1, mode="tn", out_dtype=F32, name="a_out_proj_dw", tm=512, tn=1024, tk=512)
    dbch, d_a_conv = _shortconv_bwd(bch, dgated, a_conv_f, "a_gate_bwd")
    dbch = dbch.reshape(3, T, D)
    dxn0 = _matmul(dbch, win_t, mode="nn", out_dtype=F32, a_parts=3, name="a_in_proj_dx")
    d_win_t = _matmul(dbch, xn0, mode="tn", out_dtype=F32, a_parts=3, name="a_in_proj_dw", tm=512, tn=1024, tk=512)
    grad_x, d_a_norm = _rmsnorm_bwd(x2, a_norm_f, [dxn0], dh1, "a_norm_bwd")

    big_grads = [d_win_t, d_wout, d_wkv_t, d_wq, d_wo, ffn0_grads[1], ffn1_grads[1], ffn0_grads[0], ffn1_grads[0]]
    small_full = [d_a_norm, d_a_conv, jnp.stack([ffn0_grads[2], ffn1_grads[2]]),
                  d_kvb_norm[0], d_kvb_norm[1], d_rel_bias, jnp.concatenate([ffn0_grads[4], ffn1_grads[4]]),
                  jnp.concatenate([ffn0_grads[3], ffn1_grads[3]]), d_final_norm, loss_part]
    small_full_shapes = [(1, D), (3, D), (2, 3, 2 * F), (D,), (1, D), rel_bias.shape, (2, D), (2, 2 * F), (D,), ()]
    gath, recv = _exchange([_pack_rows(small_full)],
                           [g.reshape(N_DEV, g.shape[0] // N_DEV, D) for g in big_grads], "exchange_grads")
    small_sum = _sum_slots(gath[0], "small_grad_sum").reshape(-1)
    (g_a_norm_f, g_a_conv_f, g_ffn_conv_f, g_kv_norm, g_b_norm, g_rel_bias, g_ffn_norm, g_ffn_conv_b,
     g_final_norm, loss) = _unpack(small_sum, small_full_shapes)

    def my_cols(full, width):
        return lax.dynamic_slice_in_dim(full, me * width, width, axis=full.ndim - 1)

    g_a_norm = my_cols(g_a_norm_f, D // N_DEV)
    g_a_conv = my_cols(g_a_conv_f, D // N_DEV)[None]
    g_ffn_conv = my_cols(g_ffn_conv_f, 2 * F // N_DEV)

    big_w = [(a_w_in, m_a_w_in, v_a_w_in, True), (a_w_out, m_a_w_out, v_a_w_out, False),
             (w_kv, m_w_kv, v_w_kv, True), (b_w_q, m_b_w_q, v_b_w_q, False), (b_w_o, m_b_w_o, v_b_w_o, False),
             (ffn_w_up[0], m_ffn_w_up[0], v_ffn_w_up[0], True), (ffn_w_up[1], m_ffn_w_up[1], v_ffn_w_up[1], True),
             (ffn_w_down[0], m_ffn_w_down[0], v_ffn_w_down[0], False),
             (ffn_w_down[1], m_ffn_w_down[1], v_ffn_w_down[1], False)]
    big_out = []
    for i, (w, m, v, transposed) in enumerate(big_w):
        g = _sum_slots(recv[i], f"grad_sum_{i}")
        if transposed:
            g = g.T
        w2, m2, v2 = (t.reshape(g.shape) for t in (w, m, v))
        delta, nm, nv = _adamw(g, w2, m2, v2, f"adamw_{i}")
        big_out.append(tuple(t.reshape(w.shape) for t in (g, delta, nm, nv)))

    def pair(i, j):
        return tuple(jnp.stack([big_out[i][t], big_out[j][t]]) for t in range(4))

    small_w = [(a_norm, m_a_norm, v_a_norm, g_a_norm), (a_conv, m_a_conv, v_a_conv, g_a_conv),
               (ffn_conv, m_ffn_conv, v_ffn_conv, g_ffn_conv), (kv_norm, m_kv_norm, v_kv_norm, g_kv_norm),
               (b_norm, m_b_norm, v_b_norm, g_b_norm), (rel_bias, m_rel_bias, v_rel_bias, g_rel_bias),
               (ffn_norm, m_ffn_norm, v_ffn_norm, g_ffn_norm),
               (ffn_conv_b, m_ffn_conv_b, v_ffn_conv_b, g_ffn_conv_b),
               (final_norm, m_final_norm, v_final_norm, g_final_norm)]
    shapes = [t[0].shape for t in small_w]
    packed = [_pack_rows([t[i].reshape(t[0].shape) for t in small_w]) for i in (3, 0, 1, 2)]
    s_delta, s_nm, s_nv = _adamw(*packed, "adamw_small")
    s_g = [t[3].reshape(t[0].shape) for t in small_w]
    s_d, s_m, s_v = (_unpack(t.reshape(-1), shapes) for t in (s_delta, s_nm, s_nv))
    small_out = [(s_g[i], s_d[i], s_m[i], s_v[i]) for i in range(len(small_w))]

    per_weight = [small_out[0], big_out[0], small_out[1], big_out[1], small_out[3], big_out[2], small_out[4],
                  big_out[3], big_out[4], small_out[5], small_out[6], pair(5, 6), small_out[2], small_out[7],
                  pair(7, 8), small_out[8]]
    outs = [loss, grad_x.reshape(B, S, D)]
    for t in range(4):
        outs.extend(pw[t] for pw in per_weight)
    return tuple(outs)
```

```python
import functools
import math

import jax
import jax.numpy as jnp
from jax import lax
from jax.experimental import pallas as pl
from jax.experimental.pallas import tpu as pltpu

F32 = jnp.float32
BF16 = jnp.bfloat16

N_DEV = 8
N_HEADS = 16
ATT_BLOCK = 128
DILATED_BRANCHES = ((128, 1), (512, 4), (2048, 16))
REL_BUCKETS = 32
REL_MAX_DISTANCE = 2048
RMS_EPS = 1e-6
ADAM_LR = 0.001
ADAM_B1 = 0.9
ADAM_B2 = 0.999
ADAM_EPS = 1e-08
ADAM_WD = 0.01
ADAM_STEP = 10

LANE = 128
HALO = 16
NEG = -1e30
VMEM_CAP = 56 << 20


def _vmem(block_bytes):
    return int(min(VMEM_CAP, max(32 << 20, 3 * block_bytes + (8 << 20))))


def _nbytes(shape, dtype):
    return math.prod(shape) * jnp.dtype(dtype).itemsize


def _tile(dim, target):
    best = None
    t = LANE
    while t <= min(dim, target):
        if dim % t == 0:
            best = t
        t += LANE
    return best if best is not None else dim


def _matmul(a, b, *, mode, out_dtype, name, add=None, a_parts=1, tm=1024, tn=512, tk=1024):
    P = a_parts
    if mode == "nn":
        M, K = (a.shape[0], a.shape[1]) if P == 1 else (a.shape[1], a.shape[2] * P)
        N = b.shape[1]
    elif mode == "nt":
        assert P == 1
        M, K = a.shape
        N = b.shape[0]
    else:
        K = a.shape[0] if P == 1 else a.shape[1]
        M = a.shape[1] if P == 1 else a.shape[2] * P
        N = b.shape[1]
    tm = _tile(M // P if mode == "tn" else M, tm)
    tn = _tile(N, tn)
    tk = _tile(K // P if mode == "nn" else K, tk)
    nm, nn_, nk = M // tm, N // tn, K // tk

    if mode == "nn":
        if P == 1:
            a_spec = pl.BlockSpec((tm, tk), lambda i, j, k: (i, k))
        else:
            nkp = nk // P
            a_spec = pl.BlockSpec((None, tm, tk), lambda i, j, k: (k // nkp, i, k % nkp))
        b_spec = pl.BlockSpec((tk, tn), lambda i, j, k: (k, j))
        dims = (((1,), (0,)), ((), ()))
    elif mode == "nt":
        a_spec = pl.BlockSpec((tm, tk), lambda i, j, k: (i, k))
        b_spec = pl.BlockSpec((tn, tk), lambda i, j, k: (j, k))
        dims = (((1,), (1,)), ((), ()))
    else:
        if P == 1:
            a_spec = pl.BlockSpec((tk, tm), lambda i, j, k: (k, i))
        else:
            nmp = nm // P
            a_spec = pl.BlockSpec((None, tk, tm), lambda i, j, k: (i // nmp, k, i % nmp))
        b_spec = pl.BlockSpec((tk, tn), lambda i, j, k: (k, j))
        dims = (((0,), (0,)), ((), ()))
    o_spec = pl.BlockSpec((tm, tn), lambda i, j, k: (i, j))
    in_specs = [a_spec, b_spec]
    args = [a, b]
    if add is not None:
        in_specs.append(o_spec)
        args.append(add)
    has_add = add is not None

    def body(a_ref, b_ref, *rest):
        if has_add:
            add_ref, o_ref, acc_ref = rest
        else:
            o_ref, acc_ref = rest
        k = pl.program_id(2)
        part = lax.dot_general(a_ref[...].astype(BF16), b_ref[...].astype(BF16), dims,
                               preferred_element_type=F32)

        def finish(r):
            if has_add:
                r = r + add_ref[...].astype(F32)
            o_ref[...] = r.astype(out_dtype)

        if nk == 1:
            finish(part)
        else:
            @pl.when(k == 0)
            def _():
                acc_ref[...] = part

            @pl.when(jnp.logical_and(k > 0, k < nk - 1))
            def _():
                acc_ref[...] += part

            @pl.when(k == nk - 1)
            def _():
                finish(acc_ref[...] + part)

    blk = (_nbytes((tm, tk), a.dtype) + _nbytes((tk, tn), b.dtype) + _nbytes((tm, tn), out_dtype)
           + (_nbytes((tm, tn), add.dtype) if has_add else 0)) * 2 + 3 * _nbytes((tm, tn), F32)
    return pl.pallas_call(
        body, name=name, grid=(nm, nn_, nk),
        in_specs=in_specs, out_specs=o_spec,
        out_shape=jax.ShapeDtypeStruct((M, N), out_dtype),
        scratch_shapes=[pltpu.VMEM((tm, tn), F32)],
        compiler_params=pltpu.CompilerParams(
            dimension_semantics=("parallel", "parallel", "arbitrary"), vmem_limit_bytes=_vmem(blk)),
    )(*args)


def _rmsnorm_fwd(x, gains, name, tm=512):
    T, D = x.shape
    n = gains.shape[0]
    tm = _tile(T, tm)

    def body(x_ref, g_ref, *o_refs):
        xv = x_ref[...]
        xhat = xv * lax.rsqrt(jnp.mean(xv * xv, axis=-1, keepdims=True) + RMS_EPS)
        for i in range(n):
            o_refs[i][...] = (xhat * g_ref[i:i + 1, :]).astype(BF16)

    row = pl.BlockSpec((tm, D), lambda i: (i, 0))
    outs = pl.pallas_call(
        body, name=name, grid=(T // tm,),
        in_specs=[row, pl.BlockSpec((n, D), lambda i: (0, 0))],
        out_specs=[row] * n,
        out_shape=[jax.ShapeDtypeStruct((T, D), BF16)] * n,
        compiler_params=pltpu.CompilerParams(
            dimension_semantics=("parallel",), vmem_limit_bytes=_vmem(4 * _nbytes((tm, D), F32))),
    )(x, gains)
    return tuple(outs)


def _rmsnorm_bwd(x, gains, dxns, dres, name, tm=512):
    T, D = x.shape
    n = gains.shape[0]
    tm = _tile(T, tm)

    def body(x_ref, g_ref, *rest):
        dxn_refs = rest[:n]
        dres_ref, dx_ref, dg_ref = rest[n:]
        xv = x_ref[...]
        rstd = lax.rsqrt(jnp.mean(xv * xv, axis=-1, keepdims=True) + RMS_EPS)
        xhat = xv * rstd
        dx = dres_ref[...]

        @pl.when(pl.program_id(0) == 0)
        def _():
            dg_ref[...] = jnp.zeros_like(dg_ref)

        for i in range(n):
            dy = dxn_refs[i][...].astype(F32)
            dg_ref[i:i + 1, :] += jnp.sum(dy * xhat, axis=0, keepdims=True)
            dxh = dy * g_ref[i:i + 1, :]
            dx = dx + rstd * (dxh - xhat * jnp.mean(dxh * xhat, axis=-1, keepdims=True))
        dx_ref[...] = dx

    row = pl.BlockSpec((tm, D), lambda i: (i, 0))
    par = pl.BlockSpec((n, D), lambda i: (0, 0))
    return pl.pallas_call(
        body, name=name, grid=(T // tm,),
        in_specs=[row, par] + [row] * n + [row],
        out_specs=[row, par],
        out_shape=[jax.ShapeDtypeStruct((T, D), F32), jax.ShapeDtypeStruct((n, D), F32)],
        compiler_params=pltpu.CompilerParams(
            dimension_semantics=("arbitrary",), vmem_limit_bytes=_vmem((4 + n) * _nbytes((tm, D), F32))),
    )(x, gains, *dxns, dres)


def _final_loss_bwd(h, gain, target, name, tm=512):
    T, D = h.shape
    tm = _tile(T, tm)

    def body(h_ref, g_ref, t_ref, dh_ref, dg_ref, sq_ref):
        xv = h_ref[...]
        rstd = lax.rsqrt(jnp.mean(xv * xv, axis=-1, keepdims=True) + RMS_EPS)
        xhat = xv * rstd
        err = xhat * g_ref[...] - t_ref[...]

        @pl.when(pl.program_id(0) == 0)
        def _():
            dg_ref[...] = jnp.zeros_like(dg_ref)
            sq_ref[...] = jnp.zeros_like(sq_ref)

        sq_ref[...] += jnp.sum(err * err, axis=0, keepdims=True)
        dy = err * (1.0 / D)
        dg_ref[...] += jnp.sum(dy * xhat, axis=0, keepdims=True)
        dxh = dy * g_ref[...]
        dh_ref[...] = rstd * (dxh - xhat * jnp.mean(dxh * xhat, axis=-1, keepdims=True))

    row = pl.BlockSpec((tm, D), lambda i: (i, 0))
    par = pl.BlockSpec((1, D), lambda i: (0, 0))
    return pl.pallas_call(
        body, name=name, grid=(T // tm,),
        in_specs=[row, par, row], out_specs=[row, par, par],
        out_shape=[jax.ShapeDtypeStruct((T, D), F32), jax.ShapeDtypeStruct((1, D), F32),
                   jax.ShapeDtypeStruct((1, D), F32)],
        compiler_params=pltpu.CompilerParams(
            dimension_semantics=("arbitrary",), vmem_limit_bytes=_vmem(5 * _nbytes((tm, D), F32))),
    )(h, gain, target)


def _halo_specs(S, ts, tc, col_off, prev, nxt):
    r = ts // HALO
    last = S // HALO - 1
    specs = []
    if prev:
        specs.append(pl.BlockSpec((None, HALO, tc), lambda j, b, s: (b, jnp.maximum(s * r - 1, 0), col_off + j)))
    specs.append(pl.BlockSpec((None, ts, tc), lambda j, b, s: (b, s, col_off + j)))
    if nxt:
        specs.append(pl.BlockSpec((None, HALO, tc), lambda j, b, s: (b, jnp.minimum((s + 1) * r, last), col_off + j)))
    return specs


def _ext(prev_ref, main_ref, next_ref, first, last):
    main = main_ref[...].astype(F32)
    zeros = jnp.zeros((HALO, main.shape[1]), F32)
    top = zeros if prev_ref is None else jnp.where(first, 0.0, prev_ref[...].astype(F32))
    bot = zeros if next_ref is None else jnp.where(last, 0.0, next_ref[...].astype(F32))
    return jnp.concatenate([top, main, bot], axis=0)


def _shift(xe, k):
    return pltpu.roll(xe, k % xe.shape[0], axis=0)


def _cconv(xe, w):
    return w[0:1, :] * _shift(xe, 2) + w[1:2, :] * _shift(xe, 1) + w[2:3, :] * xe


def _cconv_t(de, w):
    return w[2:3, :] * de + w[1:2, :] * _shift(de, -1) + w[0:1, :] * _shift(de, -2)


def _main(xe, ts):
    return xe[HALO:HALO + ts, :]


def _conv_wgrad(dw_ref, de, xe, ts):
    d = _main(de, ts)
    dw_ref[0:1, :] += jnp.sum(d * _main(_shift(xe, 2), ts), axis=0, keepdims=True)
    dw_ref[1:2, :] += jnp.sum(d * _main(_shift(xe, 1), ts), axis=0, keepdims=True)
    dw_ref[2:3, :] += jnp.sum(d * _main(xe, ts), axis=0, keepdims=True)


def _gate_tiles(S, C):
    return _tile(S, 512), _tile(C, 512)


def _shortconv_fwd(bch, conv_w, name):
    B, S, D3 = bch.shape
    D = D3 // 3
    ts, tc = _gate_tiles(S, D)
    nj, ns = D // tc, S // ts

    def body(b_ref, cp_ref, c_ref, hp_ref, h_ref, w_ref, o_ref):
        first = pl.program_id(2) == 0
        ce = _ext(cp_ref, c_ref, None, first, False)
        he = _ext(hp_ref, h_ref, None, first, False)
        cv = _main(_cconv(ce * he, w_ref[...]), ts)
        o_ref[...] = (b_ref[...].astype(F32) * cv).astype(BF16)

    in_specs = (_halo_specs(S, ts, tc, 0, False, False) + _halo_specs(S, ts, tc, nj, True, False)
                + _halo_specs(S, ts, tc, 2 * nj, True, False) + [pl.BlockSpec((3, tc), lambda j, b, s: (0, j))])
    return pl.pallas_call(
        body, name=name, grid=(nj, B, ns), in_specs=in_specs,
        out_specs=pl.BlockSpec((None, ts, tc), lambda j, b, s: (b, s, j)),
        out_shape=jax.ShapeDtypeStruct((B, S, D), BF16),
        compiler_params=pltpu.CompilerParams(
            dimension_semantics=("parallel", "parallel", "parallel"),
            vmem_limit_bytes=_vmem(12 * _nbytes((ts + 2 * HALO, tc), F32))),
    )(bch, bch, bch, bch, bch, conv_w)


def _shortconv_bwd(bch, dg, conv_w, name):
    B, S, D3 = bch.shape
    D = D3 // 3
    ts, tc = _gate_tiles(S, D)
    nj, ns = D // tc, S // ts

    def body(b_ref, bn_ref, cp_ref, c_ref, cn_ref, hp_ref, h_ref, hn_ref, d_ref, dn_ref, w_ref, o_ref, dw_ref):
        s = pl.program_id(2)
        first, last = s == 0, s == ns - 1

        @pl.when(jnp.logical_and(pl.program_id(1) == 0, s == 0))
        def _():
            dw_ref[...] = jnp.zeros_like(dw_ref)

        w = w_ref[...]
        be = _ext(None, b_ref, bn_ref, first, last)
        ce = _ext(cp_ref, c_ref, cn_ref, first, last)
        he = _ext(hp_ref, h_ref, hn_ref, first, last)
        de = _ext(None, d_ref, dn_ref, first, last)
        ch = ce * he
        dcv = de * be
        dch = _main(_cconv_t(dcv, w), ts)
        o_ref[0] = (_main(de, ts) * _main(_cconv(ch, w), ts)).astype(BF16)
        o_ref[1] = (dch * _main(he, ts)).astype(BF16)
        o_ref[2] = (dch * _main(ce, ts)).astype(BF16)
        _conv_wgrad(dw_ref, dcv, ch, ts)

    in_specs = (_halo_specs(S, ts, tc, 0, False, True) + _halo_specs(S, ts, tc, nj, True, True)
                + _halo_specs(S, ts, tc, 2 * nj, True, True) + _halo_specs(S, ts, tc, 0, False, True)
                + [pl.BlockSpec((3, tc), lambda j, b, s: (0, j))])
    return pl.pallas_call(
        body, name=name, grid=(nj, B, ns), in_specs=in_specs,
        out_specs=[pl.BlockSpec((3, None, ts, tc), lambda j, b, s: (0, b, s, j)),
                   pl.BlockSpec((3, tc), lambda j, b, s: (0, j))],
        out_shape=[jax.ShapeDtypeStruct((3, B, S, D), BF16), jax.ShapeDtypeStruct((3, D), F32)],
        compiler_params=pltpu.CompilerParams(
            dimension_semantics=("parallel", "arbitrary", "arbitrary"),
            vmem_limit_bytes=_vmem(24 * _nbytes((ts + 2 * HALO, tc), F32))),
    )(bch, bch, bch, bch, bch, bch, bch, bch, dg, dg, conv_w)


def _sigmoid(x):
    return 1.0 / (1.0 + jnp.exp(-x))


def _ffn_gate_fwd(u0, conv_w, conv_b, name):
    B, S, F2 = u0.shape
    F = F2 // 2
    ts, tc = _gate_tiles(S, F)
    nj, ns = F // tc, S // ts

    def body(gp_ref, g_ref, up_ref, u_ref, wg_ref, wu_ref, bg_ref, bu_ref, o_ref):
        first = pl.program_id(2) == 0
        ug = _main(_cconv(_ext(gp_ref, g_ref, None, first, False), wg_ref[...]), ts) + bg_ref[...]
        uu = _main(_cconv(_ext(up_ref, u_ref, None, first, False), wu_ref[...]), ts) + bu_ref[...]
        o_ref[...] = (ug * _sigmoid(ug) * uu).astype(BF16)

    in_specs = (_halo_specs(S, ts, tc, 0, True, False) + _halo_specs(S, ts, tc, nj, True, False)
                + [pl.BlockSpec((3, tc), lambda j, b, s: (0, j)), pl.BlockSpec((3, tc), lambda j, b, s: (0, nj + j)),
                   pl.BlockSpec((1, tc), lambda j, b, s: (0, j)), pl.BlockSpec((1, tc), lambda j, b, s: (0, nj + j))])
    return pl.pallas_call(
        body, name=name, grid=(nj, B, ns), in_specs=in_specs,
        out_specs=pl.BlockSpec((None, ts, tc), lambda j, b, s: (b, s, j)),
        out_shape=jax.ShapeDtypeStruct((B, S, F), BF16),
        compiler_params=pltpu.CompilerParams(
            dimension_semantics=("parallel", "parallel", "parallel"),
            vmem_limit_bytes=_vmem(12 * _nbytes((ts + 2 * HALO, tc), F32))),
    )(u0, u0, u0, u0, conv_w, conv_w, conv_b, conv_b)


def _ffn_gate_bwd(u0, dact, conv_w, conv_b, name):
    B, S, F2 = u0.shape
    F = F2 // 2
    ts, tc = _gate_tiles(S, F)
    nj, ns = F // tc, S // ts

    def body(gp_ref, g_ref, gn_ref, up_ref, u_ref, un_ref, d_ref, dn_ref, wg_ref, wu_ref, bg_ref, bu_ref,
             o_ref, dwg_ref, dwu_ref, dbg_ref, dbu_ref):
        s = pl.program_id(2)
        first, last = s == 0, s == ns - 1

        @pl.when(jnp.logical_and(pl.program_id(1) == 0, s == 0))
        def _():
            for r in (dwg_ref, dwu_ref, dbg_ref, dbu_ref):
                r[...] = jnp.zeros_like(r)

        wg, wu = wg_ref[...], wu_ref[...]
        ge = _ext(gp_ref, g_ref, gn_ref, first, last)
        ue = _ext(up_ref, u_ref, un_ref, first, last)
        de = _ext(None, d_ref, dn_ref, first, last)
        ug = _cconv(ge, wg) + bg_ref[...]
        uu = _cconv(ue, wu) + bu_ref[...]
        sig = _sigmoid(ug)
        dug = de * uu * (sig * (1.0 + ug * (1.0 - sig)))
        duu = de * (ug * sig)
        o_ref[0] = _main(_cconv_t(dug, wg), ts).astype(BF16)
        o_ref[1] = _main(_cconv_t(duu, wu), ts).astype(BF16)
        _conv_wgrad(dwg_ref, dug, ge, ts)
        _conv_wgrad(dwu_ref, duu, ue, ts)
        dbg_ref[...] += jnp.sum(_main(dug, ts), axis=0, keepdims=True)
        dbu_ref[...] += jnp.sum(_main(duu, ts), axis=0, keepdims=True)

    w3 = lambda off: pl.BlockSpec((3, tc), lambda j, b, s: (0, off + j))
    w1 = lambda off: pl.BlockSpec((1, tc), lambda j, b, s: (0, off + j))
    in_specs = (_halo_specs(S, ts, tc, 0, True, True) + _halo_specs(S, ts, tc, nj, True, True)
                + _halo_specs(S, ts, tc, 0, False, True) + [w3(0), w3(nj), w1(0), w1(nj)])
    outs = pl.pallas_call(
        body, name=name, grid=(nj, B, ns), in_specs=in_specs,
        out_specs=[pl.BlockSpec((2, None, ts, tc), lambda j, b, s: (0, b, s, j)), w3(0), w3(0), w1(0), w1(0)],
        out_shape=[jax.ShapeDtypeStruct((2, B, S, F), BF16), jax.ShapeDtypeStruct((3, F), F32),
                   jax.ShapeDtypeStruct((3, F), F32), jax.ShapeDtypeStruct((1, F), F32),
                   jax.ShapeDtypeStruct((1, F), F32)],
        compiler_params=pltpu.CompilerParams(
            dimension_semantics=("parallel", "arbitrary", "arbitrary"),
            vmem_limit_bytes=_vmem(30 * _nbytes((ts + 2 * HALO, tc), F32))),
    )(u0, u0, u0, u0, u0, u0, dact, dact, conv_w, conv_w, conv_b, conv_b)
    du0, dwg, dwu, dbg, dbu = outs
    return du0, jnp.concatenate([dwg, dwu], axis=1), jnp.concatenate([dbg, dbu], axis=1)


def _t5_bucket(dist):
    max_exact = REL_BUCKETS // 2
    n = jnp.maximum(dist, 0)
    nf = jnp.maximum(n, max_exact).astype(F32)
    large = max_exact + (jnp.log(nf / max_exact) / math.log(REL_MAX_DISTANCE / max_exact)
                         * (REL_BUCKETS - max_exact)).astype(jnp.int32)
    large = jnp.minimum(large, REL_BUCKETS - 1)
    return jnp.where(n < max_exact, n, large)


def _band_tables(window, dilation):
    P = ATT_BLOCK
    qi = jnp.arange(P, dtype=jnp.int32)[:, None]
    kc = jnp.arange(2 * P, dtype=jnp.int32)[None, :]
    delta = qi + P - kc
    band = (delta >= 0) & (delta <= window // dilation)
    bucket = _t5_bucket(delta * dilation).reshape(-1)
    onehot = (bucket[None, :] == jnp.arange(REL_BUCKETS, dtype=jnp.int32)[:, None]).astype(F32)
    return onehot, band


def _bias_lookup(rel_bias_t, onehots, name):
    nb, _, Q = onehots.shape
    H = rel_bias_t.shape[0]

    def body(r_ref, oh_ref, o_ref):
        o_ref[...] = lax.dot_general(r_ref[...], oh_ref[...], (((1,), (0,)), ((), ())),
                                     precision=lax.Precision.HIGHEST, preferred_element_type=F32)

    return pl.pallas_call(
        body, name=name, grid=(nb,),
        in_specs=[pl.BlockSpec((H, REL_BUCKETS), lambda i: (0, 0)),
                  pl.BlockSpec((None, REL_BUCKETS, Q), lambda i: (i, 0, 0))],
        out_specs=pl.BlockSpec((None, H, Q), lambda i: (i, 0, 0)),
        out_shape=jax.ShapeDtypeStruct((nb, H, Q), F32),
        compiler_params=pltpu.CompilerParams(dimension_semantics=("parallel",),
                                             vmem_limit_bytes=_vmem(_nbytes((REL_BUCKETS + H, Q), F32))),
    )(rel_bias_t, onehots)


def _bias_grad(dtabs, onehots, name):
    nb, H, Q = dtabs.shape

    def body(d_ref, oh_ref, o_ref):
        @pl.when(pl.program_id(0) == 0)
        def _():
            o_ref[...] = jnp.zeros_like(o_ref)

        o_ref[...] += lax.dot_general(d_ref[...], oh_ref[...], (((1,), (1,)), ((), ())),
                                      precision=lax.Precision.HIGHEST, preferred_element_type=F32)

    return pl.pallas_call(
        body, name=name, grid=(nb,),
        in_specs=[pl.BlockSpec((None, H, Q), lambda i: (i, 0, 0)),
                  pl.BlockSpec((None, REL_BUCKETS, Q), lambda i: (i, 0, 0))],
        out_specs=pl.BlockSpec((H, REL_BUCKETS), lambda i: (0, 0)),
        out_shape=jax.ShapeDtypeStruct((H, REL_BUCKETS), F32),
        compiler_params=pltpu.CompilerParams(dimension_semantics=("arbitrary",),
                                             vmem_limit_bytes=_vmem(_nbytes((REL_BUCKETS + H, Q), F32))),
    )(dtabs, onehots)


def _lane_masks():
    lane = lax.broadcasted_iota(jnp.int32, (1, LANE), 1)
    lo = lane < LANE // 2
    return lo, jnp.logical_not(lo)


def _dot_nt(a, b):
    return lax.dot_general(a, b, (((1,), (1,)), ((), ())), preferred_element_type=F32)


def _dot_nn(a, b):
    return lax.dot_general(a, b, (((1,), (0,)), ((), ())), preferred_element_type=F32)


def _dot_tn(a, b):
    return lax.dot_general(a, b, (((0,), (0,)), ((), ())), preferred_element_type=F32)


def _block_rows(n, dilation, S):
    P = ATT_BLOCK
    nb = S // (dilation * P)
    r, i = n // nb, n % nb
    cur = pl.ds(i * (P * dilation) + r, P, stride=dilation)
    prv = pl.ds(jnp.maximum(i - 1, 0) * (P * dilation) + r, P, stride=dilation)
    return cur, prv, jnp.minimum(i, 1)


def _attn_fwd(q, kv, tabs, name):
    B, S, D = q.shape
    P, H = ATT_BLOCK, N_HEADS
    scale = (D // H) ** -0.5
    half = LANE // 2

    def body(q_ref, k_ref, v_ref, tab_ref, o_ref, lse_ref, acc_ref, m_ref, l_ref):
        lo, hi = _lane_masks()
        for bi, (_, d) in enumerate(DILATED_BRANCHES):
            def block(n, carry, bi=bi, d=d):
                cur, prv, variant = _block_rows(n, d, S)
                qb = q_ref[cur, :].astype(BF16)
                kw = jnp.concatenate([k_ref[prv, :], k_ref[cur, :]], axis=0).astype(BF16)
                vw = jnp.concatenate([v_ref[prv, :], v_ref[cur, :]], axis=0).astype(BF16)
                if bi > 0:
                    m_old, l_old, a_old = m_ref[cur, :], l_ref[cur, :], acc_ref[cur, :]
                m_pair = l_pair = a_pair = None
                for e, msk in enumerate((lo, hi)):
                    s = _dot_nt(jnp.where(msk, qb, 0), kw) * scale + tab_ref[bi, variant, e]
                    m_e = jnp.max(s, axis=-1, keepdims=True)
                    if bi > 0:
                        m_o = m_old[:, e * half:e * half + 1]
                        m_e = jnp.maximum(m_o, m_e)
                        alpha = jnp.exp(m_o - m_e)
                    pe = jnp.exp(s - m_e)
                    l_e = jnp.sum(pe, axis=-1, keepdims=True)
                    a_e = _dot_nn(pe.astype(BF16), jnp.where(msk, vw, 0))
                    if bi > 0:
                        l_e = l_e + alpha * l_old[:, e * half:e * half + 1]
                        a_e = a_e + alpha * jnp.where(msk, a_old, 0.0)
                    m_pair = m_e if e == 0 else jnp.where(lo, m_pair, m_e)
                    l_pair = l_e if e == 0 else jnp.where(lo, l_pair, l_e)
                    a_pair = a_e if e == 0 else a_pair + a_e
                m_ref[cur, :] = m_pair
                l_ref[cur, :] = l_pair
                acc_ref[cur, :] = a_pair
                return carry

            lax.fori_loop(0, S // P, block, 0)
        l = l_ref[...]
        o_ref[...] = (acc_ref[...] / l).astype(BF16)
        lse_ref[...] = m_ref[...] + jnp.log(l)

    nl = D // LANE
    col = lambda off: pl.BlockSpec((None, S, LANE), lambda b, p: (b, 0, off + p))
    tab_spec = pl.BlockSpec((3, 2, 2, P, 2 * P), lambda b, p: (0, 0, p, 0, 0))
    return pl.pallas_call(
        body, name=name, grid=(B, H // 2), in_specs=[col(0), col(0), col(nl), tab_spec],
        out_specs=[col(0), col(0)],
        out_shape=[jax.ShapeDtypeStruct((B, S, D), BF16), jax.ShapeDtypeStruct((B, S, D), F32)],
        scratch_shapes=[pltpu.VMEM((S, LANE), F32)] * 3,
        compiler_params=pltpu.CompilerParams(
            dimension_semantics=("parallel", "parallel"),
            vmem_limit_bytes=_vmem(5 * _nbytes((S, LANE), F32) + _nbytes((12, P, 2 * P), F32))),
    )(q, kv, kv, tabs)


def _attn_bwd(q, kv, do, o, lse, tabs, name):
    B, S, D = q.shape
    P, H = ATT_BLOCK, N_HEADS
    scale = (D // H) ** -0.5
    half = LANE // 2

    def body(q_ref, k_ref, v_ref, do_ref, o_ref, lse_ref, tab_ref, dq_ref, dkv_ref, dtab_ref, delta_ref):
        lo, hi = _lane_masks()

        @pl.when(pl.program_id(1) == 0)
        def _():
            dtab_ref[...] = jnp.zeros_like(dtab_ref)

        dkv_ref[...] = jnp.zeros_like(dkv_ref)
        prod = do_ref[...] * o_ref[...].astype(F32)
        delta_ref[...] = jnp.where(lo, jnp.sum(jnp.where(lo, prod, 0.0), axis=-1, keepdims=True),
                                   jnp.sum(jnp.where(hi, prod, 0.0), axis=-1, keepdims=True))

        for bi, (_, d) in enumerate(DILATED_BRANCHES):
            def block(n, carry, bi=bi, d=d):
                cur, prv, variant = _block_rows(n, d, S)
                qb = q_ref[cur, :].astype(BF16)
                kw = jnp.concatenate([k_ref[prv, :], k_ref[cur, :]], axis=0).astype(BF16)
                vw = jnp.concatenate([v_ref[prv, :], v_ref[cur, :]], axis=0).astype(BF16)
                dob = do_ref[cur, :].astype(BF16)
                lse_b = lse_ref[cur, :]
                dl_b = delta_ref[cur, :]
                dq_pair = dk_pair = dv_pair = None
                for e, msk in enumerate((lo, hi)):
                    qm = jnp.where(msk, qb, 0)
                    dom = jnp.where(msk, dob, 0)
                    s = _dot_nt(qm, kw) * scale + tab_ref[bi, variant, e]
                    pe = jnp.exp(s - lse_b[:, e * half:e * half + 1])
                    ds = pe * (_dot_nt(dom, vw) - dl_b[:, e * half:e * half + 1])
                    dtab_ref[bi, e] += ds
                    dsb = ds.astype(BF16)
                    dq_e = _dot_nn(dsb, jnp.where(msk, kw, 0))
                    dk_e = _dot_tn(dsb, qm)
                    dv_e = _dot_tn(pe.astype(BF16), dom)
                    dq_pair = dq_e if e == 0 else dq_pair + dq_e
                    dk_pair = dk_e if e == 0 else dk_pair + dk_e
                    dv_pair = dv_e if e == 0 else dv_pair + dv_e
                dq_pair = dq_pair * scale
                dk_pair = dk_pair * scale
                if bi > 0:
                    dq_pair = dq_pair + dq_ref[cur, :]
                dq_ref[cur, :] = dq_pair
                dkv_ref[0, cur, :] += dk_pair[P:, :]
                dkv_ref[1, cur, :] += dv_pair[P:, :]
                dkv_ref[0, prv, :] += dk_pair[:P, :]
                dkv_ref[1, prv, :] += dv_pair[:P, :]
                return carry

            lax.fori_loop(0, S // P, block, 0)

    nl = D // LANE
    col = lambda off: pl.BlockSpec((None, S, LANE), lambda p, b: (b, 0, off + p))
    tab_spec = pl.BlockSpec((3, 2, 2, P, 2 * P), lambda p, b: (0, 0, p, 0, 0))
    return pl.pallas_call(
        body, name=name, grid=(H // 2, B),
        in_specs=[col(0), col(0), col(nl), col(0), col(0), col(0), tab_spec],
        out_specs=[col(0), pl.BlockSpec((2, None, S, LANE), lambda p, b: (0, b, 0, p)),
                   pl.BlockSpec((None, 3, 2, P, 2 * P), lambda p, b: (p, 0, 0, 0, 0))],
        out_shape=[jax.ShapeDtypeStruct((B, S, D), F32), jax.ShapeDtypeStruct((2, B, S, D), F32),
                   jax.ShapeDtypeStruct((H // 2, 3, 2, P, 2 * P), F32)],
        scratch_shapes=[pltpu.VMEM((S, LANE), F32)],
        compiler_params=pltpu.CompilerParams(
            dimension_semantics=("parallel", "arbitrary"),
            vmem_limit_bytes=VMEM_CAP),
    )(q, kv, kv, do, o, lse, tabs)


def _attn_fwd_branch(q, kv, tab, dilation, name):
    B, S, D = q.shape
    P, d = ATT_BLOCK, dilation
    L = S // d
    nb = L // P
    H = N_HEADS
    scale = (D // H) ** -0.5
    qv = q.reshape(B, L, d * D)
    kvv = kv.reshape(B, L, d * 2 * D)

    def body(q_ref, kp_ref, kc_ref, vp_ref, vc_ref, tab_ref, o_ref, lse_ref):
        lo, hi = _lane_masks()
        for p in range(H // 2):
            sl = slice(p * LANE, (p + 1) * LANE)
            qp = q_ref[:, sl]
            kp = jnp.concatenate([kp_ref[:, sl], kc_ref[:, sl]], axis=0)
            vp = jnp.concatenate([vp_ref[:, sl], vc_ref[:, sl]], axis=0)
            o_pair = None
            lse_pair = None
            for e, msk in enumerate((lo, hi)):
                s = _dot_nt(jnp.where(msk, qp, 0), kp) * scale + tab_ref[2 * p + e]
                m = jnp.max(s, axis=-1, keepdims=True)
                pe = jnp.exp(s - m)
                den = jnp.sum(pe, axis=-1, keepdims=True)
                o_e = _dot_nn(pe.astype(BF16), jnp.where(msk, vp, 0)) / den
                lse_e = m + jnp.log(den)
                o_pair = o_e if e == 0 else o_pair + o_e
                lse_pair = lse_e if e == 0 else jnp.where(lo, lse_pair, lse_e)
            o_ref[:, sl] = o_pair
            lse_ref[:, sl] = lse_pair

    blk = lambda f: pl.BlockSpec((None, P, D), f)
    prev = lambda i: jnp.maximum(i - 1, 0)
    in_specs = [blk(lambda b, r, i: (b, i, r)),
                blk(lambda b, r, i: (b, prev(i), 2 * r)), blk(lambda b, r, i: (b, i, 2 * r)),
                blk(lambda b, r, i: (b, prev(i), 2 * r + 1)), blk(lambda b, r, i: (b, i, 2 * r + 1)),
                pl.BlockSpec((None, H, P, 2 * P), lambda b, r, i: (jnp.minimum(i, 1), 0, 0, 0))]
    o, lse = pl.pallas_call(
        body, name=name, grid=(B, d, nb), in_specs=in_specs,
        out_specs=[blk(lambda b, r, i: (b, i, r))] * 2,
        out_shape=[jax.ShapeDtypeStruct((B, L, d * D), F32)] * 2,
        compiler_params=pltpu.CompilerParams(
            dimension_semantics=("parallel", "parallel", "parallel"),
            vmem_limit_bytes=_vmem(_nbytes((H, P, 2 * P), F32) + 9 * _nbytes((P, D), F32))),
    )(qv, kvv, kvv, kvv, kvv, tab)
    return o.reshape(B * S, D), lse.reshape(B * S, D)


def _attn_merge(os_, lses, name, tm=512):
    T, D = os_[0].shape
    n = len(os_)
    tm = _tile(T, tm)

    def body(*refs):
        o_refs, l_refs = refs[:n], refs[n:2 * n]
        out_ref, lse_ref = refs[2 * n:]
        ls = [r[...] for r in l_refs]
        m = functools.reduce(jnp.maximum, ls)
        ws = [jnp.exp(l - m) for l in ls]
        tot = functools.reduce(jnp.add, ws)
        acc = functools.reduce(jnp.add, [w * r[...] for w, r in zip(ws, o_refs)])
        out_ref[...] = (acc / tot).astype(BF16)
        lse_ref[...] = m + jnp.log(tot)

    row = pl.BlockSpec((tm, D), lambda i: (i, 0))
    return pl.pallas_call(
        body, name=name, grid=(T // tm,), in_specs=[row] * (2 * n), out_specs=[row, row],
        out_shape=[jax.ShapeDtypeStruct((T, D), BF16), jax.ShapeDtypeStruct((T, D), F32)],
        compiler_params=pltpu.CompilerParams(
            dimension_semantics=("parallel",), vmem_limit_bytes=_vmem((2 * n + 4) * _nbytes((tm, D), F32))),
    )(*os_, *lses)


def _attn_bwd_q(q, kv, do, o, lse, tab, dq_acc, dilation, name):
    B, S, D = q.shape
    P, d = ATT_BLOCK, dilation
    L = S // d
    nb = L // P
    H = N_HEADS
    scale = (D // H) ** -0.5
    view = lambda t, c: t.reshape(B, L, d * c)
    has_acc = dq_acc is not None

    def body(q_ref, kp_ref, kc_ref, vp_ref, vc_ref, do_ref, o_ref, lse_ref, tab_ref, *rest):
        if has_acc:
            acc_ref, dq_ref, dtab_ref = rest
        else:
            dq_ref, dtab_ref = rest

        @pl.when((pl.program_id(0) == 0) & (pl.program_id(1) == 0) & (pl.program_id(2) == 0))
        def _():
            dtab_ref[...] = jnp.zeros_like(dtab_ref)

        lo, hi = _lane_masks()
        for p in range(H // 2):
            sl = slice(p * LANE, (p + 1) * LANE)
            qp = q_ref[:, sl]
            kp = jnp.concatenate([kp_ref[:, sl], kc_ref[:, sl]], axis=0)
            vp = jnp.concatenate([vp_ref[:, sl], vc_ref[:, sl]], axis=0)
            dop = do_ref[:, sl]
            prod = dop.astype(F32) * o_ref[:, sl].astype(F32)
            lsep = lse_ref[:, sl]
            dq_pair = None
            for e, msk in enumerate((lo, hi)):
                s = _dot_nt(jnp.where(msk, qp, 0), kp) * scale + tab_ref[2 * p + e]
                pe = jnp.exp(s - lsep[:, e * (LANE // 2):e * (LANE // 2) + 1])
                dp = _dot_nt(jnp.where(msk, dop, 0).astype(BF16), vp)
                delta = jnp.sum(jnp.where(msk, prod, 0.0), axis=-1, keepdims=True)
                ds = pe * (dp - delta)
                dtab_ref[2 * p + e] += ds
                dq_e = _dot_nn(ds.astype(BF16), jnp.where(msk, kp, 0))
                dq_pair = dq_e if e == 0 else dq_pair + dq_e
            dq_pair = dq_pair * scale
            if has_acc:
                dq_pair = dq_pair + acc_ref[:, sl]
            dq_ref[:, sl] = dq_pair

    blk = lambda f: pl.BlockSpec((None, P, D), f)
    prev = lambda i: jnp.maximum(i - 1, 0)
    own = blk(lambda b, r, i: (b, i, r))
    tab_spec = pl.BlockSpec((None, H, P, 2 * P), lambda b, r, i: (jnp.minimum(i, 1), 0, 0, 0))
    in_specs = [own,
                blk(lambda b, r, i: (b, prev(i), 2 * r)), blk(lambda b, r, i: (b, i, 2 * r)),
                blk(lambda b, r, i: (b, prev(i), 2 * r + 1)), blk(lambda b, r, i: (b, i, 2 * r + 1)),
                own, own, own, tab_spec]
    kvv = view(kv, 2 * D)
    args = [view(q, D), kvv, kvv, kvv, kvv, view(do, D), view(o, D), view(lse, D), tab]
    if has_acc:
        in_specs.append(own)
        args.append(view(dq_acc, D))
    dq, dtab = pl.pallas_call(
        body, name=name, grid=(B, d, nb), in_specs=in_specs,
        out_specs=[own, pl.BlockSpec((H, P, 2 * P), lambda b, r, i: (0, 0, 0))],
        out_shape=[jax.ShapeDtypeStruct((B, L, d * D), F32), jax.ShapeDtypeStruct((H, P, 2 * P), F32)],
        compiler_params=pltpu.CompilerParams(
            dimension_semantics=("arbitrary", "arbitrary", "arbitrary"),
            vmem_limit_bytes=_vmem(2 * _nbytes((H, P, 2 * P), F32) + 12 * _nbytes((P, D), F32))),
    )(*args)
    return dq.reshape(B, S, D), dtab


def _attn_bwd_kv(q, kv, do, o, lse, tabk, dkv_acc, dilation, name):
    B, S, D = q.shape
    P, d = ATT_BLOCK, dilation
    L = S // d
    nb = L // P
    H = N_HEADS
    scale = (D // H) ** -0.5
    view = lambda t, c: t.reshape(B, L, d * c)
    has_acc = dkv_acc is not None

    def body(k_ref, v_ref, qa_ref, qb_ref, doa_ref, dob_ref, oa_ref, ob_ref, la_ref, lb_ref, tab_ref, *rest):
        if has_acc:
            acc_ref, dkv_ref = rest
        else:
            (dkv_ref,) = rest
        lo, hi = _lane_masks()
        for p in range(H // 2):
            sl = slice(p * LANE, (p + 1) * LANE)
            kp, vp = k_ref[:, sl], v_ref[:, sl]
            q2 = jnp.concatenate([qa_ref[:, sl], qb_ref[:, sl]], axis=0)
            do2 = jnp.concatenate([doa_ref[:, sl], dob_ref[:, sl]], axis=0)
            o2 = jnp.concatenate([oa_ref[:, sl], ob_ref[:, sl]], axis=0)
            lse2 = jnp.concatenate([la_ref[:, sl], lb_ref[:, sl]], axis=0)
            prod = do2.astype(F32) * o2.astype(F32)
            dk_pair = None
            dv_pair = None
            for e, msk in enumerate((lo, hi)):
                qm = jnp.where(msk, q2, 0)
                dom = jnp.where(msk, do2, 0).astype(BF16)
                s = _dot_nt(qm, kp) * scale + tab_ref[2 * p + e]
                pe = jnp.exp(s - lse2[:, e * (LANE // 2):e * (LANE // 2) + 1])
                dp = _dot_nt(dom, vp)
                delta = jnp.sum(jnp.where(msk, prod, 0.0), axis=-1, keepdims=True)
                ds = pe * (dp - delta)
                dv_e = _dot_tn(pe.astype(BF16), dom)
                dk_e = _dot_tn(ds.astype(BF16), qm)
                dk_pair = dk_e if e == 0 else dk_pair + dk_e
                dv_pair = dv_e if e == 0 else dv_pair + dv_e
            dk_pair = dk_pair * scale
            if has_acc:
                dk_pair = dk_pair + acc_ref[0, :, sl]
                dv_pair = dv_pair + acc_ref[1, :, sl]
            dkv_ref[0, :, sl] = dk_pair
            dkv_ref[1, :, sl] = dv_pair

    blk = lambda f: pl.BlockSpec((None, P, D), f)
    nxt = lambda i: jnp.minimum(i + 1, nb - 1)
    own = blk(lambda b, r, i: (b, i, r))
    nx = blk(lambda b, r, i: (b, nxt(i), r))
    tab_spec = pl.BlockSpec((None, H, 2 * P, P), lambda b, r, i: (jnp.where(i == nb - 1, 1, 0), 0, 0, 0))
    pair = pl.BlockSpec((2, None, P, D), lambda b, r, i: (0, b, i, r))
    in_specs = [blk(lambda b, r, i: (b, i, 2 * r)), blk(lambda b, r, i: (b, i, 2 * r + 1)),
                own, nx, own, nx, own, nx, own, nx, tab_spec]
    kvv, qv, dov, ov, lv = view(kv, 2 * D), view(q, D), view(do, D), view(o, D), view(lse, D)
    args = [kvv, kvv, qv, qv, dov, dov, ov, ov, lv, lv, tabk]
    if has_acc:
        in_specs.append(pair)
        args.append(dkv_acc.reshape(2, B, L, d * D))
    dkv = pl.pallas_call(
        body, name=name, grid=(B, d, nb), in_specs=in_specs, out_specs=pair,
        out_shape=jax.ShapeDtypeStruct((2, B, L, d * D), F32),
        compiler_params=pltpu.CompilerParams(
            dimension_semantics=("parallel", "parallel", "parallel"),
            vmem_limit_bytes=_vmem(_nbytes((H, P, 2 * P), F32) + 16 * _nbytes((P, D), F32))),
    )(*args)
    return dkv.reshape(2, B, S, D)


def _exchange(gathered, scattered, name):
    ng, ns = len(gathered), len(scattered)
    n = ng + ns

    def body(*refs):
        ins, outs = refs[:n], refs[n:2 * n]
        send_sems, recv_sems, loc_sems = refs[2 * n:]
        x, y, c = lax.axis_index("x"), lax.axis_index("y"), lax.axis_index("c")
        me = 4 * x + 2 * y + c
        peers = []
        for k in range(1, N_DEV):
            px = 1 - x if k & 4 else x
            py = 1 - y if k & 2 else y
            pc = 1 - c if k & 1 else c
            peers.append(((px, py, pc), 4 * px + 2 * py + pc))

        def src(a, dev):
            return ins[a] if a < ng else ins[a].at[dev]

        local = []
        sends = []
        for a in range(n):
            cp = pltpu.make_async_copy(src(a, me), outs[a].at[me], loc_sems.at[a])
            cp.start()
            local.append(cp)
            for k, (peer, pid) in enumerate(peers):
                cp = pltpu.make_async_remote_copy(
                    src_ref=src(a, pid), dst_ref=outs[a].at[me],
                    send_sem=send_sems.at[a, k], recv_sem=recv_sems.at[a, k],
                    device_id=peer, device_id_type=pl.DeviceIdType.MESH)
                cp.start()
                sends.append(cp)
        for cp in sends:
            cp.wait_send()
        for a in range(n):
            for k, (peer, pid) in enumerate(peers):
                pltpu.make_async_remote_copy(
                    src_ref=src(a, pid), dst_ref=outs[a].at[pid],
                    send_sem=send_sems.at[a, k], recv_sem=recv_sems.at[a, k],
                    device_id=peer, device_id_type=pl.DeviceIdType.MESH).wait_recv()
        for cp in local:
            cp.wait()

    out_shape = ([jax.ShapeDtypeStruct((N_DEV,) + g.shape, g.dtype) for g in gathered]
                 + [jax.ShapeDtypeStruct(s.shape, s.dtype) for s in scattered])
    any_spec = pl.BlockSpec(memory_space=pl.ANY)
    outs = pl.pallas_call(
        body, name=name, in_specs=[any_spec] * n, out_specs=[any_spec] * n, out_shape=out_shape,
        scratch_shapes=[pltpu.SemaphoreType.DMA((n, N_DEV - 1)), pltpu.SemaphoreType.DMA((n, N_DEV - 1)),
                        pltpu.SemaphoreType.DMA((n,))],
        compiler_params=pltpu.CompilerParams(has_side_effects=True),
    )(*gathered, *scattered)
    return list(outs[:ng]), list(outs[ng:])


def _row_tile(rows, target):
    best = rows
    for t in range(8, min(rows, target) + 1, 8):
        if rows % t == 0:
            best = t
    return best


def _sum_slots(recv, name, tr=128):
    _, R, C = recv.shape
    tr = _row_tile(R, tr)

    def body(r_ref, o_ref):
        acc = r_ref[0].astype(F32)
        for k in range(1, N_DEV):
            acc = acc + r_ref[k].astype(F32)
        o_ref[...] = acc

    return pl.pallas_call(
        body, name=name, grid=(R // tr,),
        in_specs=[pl.BlockSpec((N_DEV, tr, C), lambda i: (0, i, 0))],
        out_specs=pl.BlockSpec((tr, C), lambda i: (i, 0)),
        out_shape=jax.ShapeDtypeStruct((R, C), F32),
        compiler_params=pltpu.CompilerParams(
            dimension_semantics=("parallel",), vmem_limit_bytes=_vmem(10 * _nbytes((tr, C), F32))),
    )(recv)


def _adamw(g, w, m, v, name, tr=256):
    R, C = g.shape
    tr = _row_tile(R, tr)

    def body(g_ref, w_ref, m_ref, v_ref, d_ref, nm_ref, nv_ref):
        gv = g_ref[...]
        nm = ADAM_B1 * m_ref[...] + (1.0 - ADAM_B1) * gv
        nv = ADAM_B2 * v_ref[...] + (1.0 - ADAM_B2) * (gv * gv)
        m_hat = nm / (1.0 - ADAM_B1 ** ADAM_STEP)
        v_hat = nv / (1.0 - ADAM_B2 ** ADAM_STEP)
        d_ref[...] = -ADAM_LR * (m_hat / (jnp.sqrt(v_hat) + ADAM_EPS) + ADAM_WD * w_ref[...])
        nm_ref[...] = nm
        nv_ref[...] = nv

    row = pl.BlockSpec((tr, C), lambda i: (i, 0))
    return pl.pallas_call(
        body, name=name, grid=(R // tr,), in_specs=[row] * 4, out_specs=[row] * 3,
        out_shape=[jax.ShapeDtypeStruct((R, C), F32)] * 3,
        compiler_params=pltpu.CompilerParams(
            dimension_semantics=("parallel",), vmem_limit_bytes=_vmem(8 * _nbytes((tr, C), F32))),
    )(g, w, m, v)


def _pack_rows(parts):
    flat = jnp.concatenate([p.reshape(-1).astype(F32) for p in parts])
    rows = -(-flat.shape[0] // (8 * LANE)) * 8
    return jnp.pad(flat, (0, rows * LANE - flat.shape[0])).reshape(rows, LANE)


def _unpack(flat, shapes):
    out, off = [], 0
    for shp in shapes:
        n = math.prod(shp)
        out.append(flat[off:off + n].reshape(shp))
        off += n
    return out


def kernel(x, a_norm, a_w_in, a_conv, a_w_out, kv_norm, w_kv, b_norm, b_w_q, b_w_o, rel_bias, ffn_norm, ffn_w_up, ffn_conv, ffn_conv_b, ffn_w_down, final_norm, loss_target, m_a_norm, m_a_w_in, m_a_conv, m_a_w_out, m_kv_norm, m_w_kv, m_b_norm, m_b_w_q, m_b_w_o, m_rel_bias, m_ffn_norm, m_ffn_w_up, m_ffn_conv, m_ffn_conv_b, m_ffn_w_down, m_final_norm, v_a_norm, v_a_w_in, v_a_conv, v_a_w_out, v_kv_norm, v_w_kv, v_b_norm, v_b_w_q, v_b_w_o, v_rel_bias, v_ffn_norm, v_ffn_w_up, v_ffn_conv, v_ffn_conv_b, v_ffn_w_down, v_final_norm):
    B, S, D = x.shape
    T = B * S
    F = ffn_w_down.shape[1] * N_DEV
    me = 4 * lax.axis_index("x") + 2 * lax.axis_index("y") + lax.axis_index("c")

    big_shards = [a_w_in[0].T, a_w_out[0], w_kv.T, b_w_q[0], b_w_o[0],
                  ffn_w_up[0].T, ffn_w_up[1].T, ffn_w_down[0], ffn_w_down[1]]
    small_shapes = [a_norm.shape, a_conv.shape, ffn_conv.shape]
    small_pack = _pack_rows([a_norm, a_conv, ffn_conv])
    gathered, _ = _exchange([s.astype(BF16) for s in big_shards] + [small_pack], [], "gather_weights")
    win_t, wout, wkv_t, wq, wo, wup0_t, wup1_t, wdn0, wdn1 = [
        g.reshape(-1, D) for g in gathered[:9]]
    wup_t, wdn = (wup0_t, wup1_t), (wdn0, wdn1)
    smalls = [_unpack(gathered[9][j].reshape(-1), small_shapes) for j in range(N_DEV)]
    a_norm_f = jnp.concatenate([s[0] for s in smalls], axis=-1)
    a_conv_f = jnp.concatenate([s[1] for s in smalls], axis=-1)[0]
    ffn_conv_f = jnp.concatenate([s[2] for s in smalls], axis=-1)

    x2 = x.reshape(T, D)
    (xn0,) = _rmsnorm_fwd(x2, a_norm_f, "a_norm_fwd")
    bch = _matmul(xn0, win_t, mode="nt", out_dtype=BF16, name="a_in_proj").reshape(B, S, 3 * D)
    gated = _shortconv_fwd(bch, a_conv_f, "a_gate_fwd").reshape(T, D)
    h1 = _matmul(gated, wout, mode="nn", out_dtype=F32, add=x2, name="a_out_proj")

    def ffn_fwd(h, l):
        (xn,) = _rmsnorm_fwd(h, ffn_norm[l:l + 1], f"ffn{l}_norm_fwd")
        u0 = _matmul(xn, wup_t[l], mode="nt", out_dtype=BF16, name=f"ffn{l}_up").reshape(B, S, 2 * F)
        act = _ffn_gate_fwd(u0, ffn_conv_f[l], ffn_conv_b[l:l + 1], f"ffn{l}_gate_fwd").reshape(T, F)
        out = _matmul(act, wdn[l], mode="nn", out_dtype=F32, add=h, name=f"ffn{l}_down", tk=1408)
        return out, (xn, u0, act)

    h2, ffn0_saved = ffn_fwd(h1, 0)
    kvn, xnb = _rmsnorm_fwd(h2, jnp.stack([kv_norm, b_norm[0]]), "kv_b_norm_fwd")
    kv = _matmul(kvn, wkv_t, mode="nt", out_dtype=F32, name="kv_proj").reshape(B, S, 2 * D)
    q = _matmul(xnb, wq, mode="nn", out_dtype=F32, name="q_proj").reshape(B, S, D)

    tables = [_band_tables(w, d) for (w, d) in DILATED_BRANCHES]
    onehots = jnp.stack([t[0] for t in tables])
    P = ATT_BLOCK
    bias_vals = _bias_lookup(rel_bias.T, onehots, "rel_bias_lookup").reshape(3, N_HEADS, P, 2 * P)
    in_cur = (jnp.arange(2 * P) >= P)[None, :]
    tabs = []
    for bi in range(3):
        band = tables[bi][1]
        gen = jnp.where(band[None], bias_vals[bi], NEG)
        fst = jnp.where((band & in_cur)[None], bias_vals[bi], NEG)
        tabs.append(jnp.stack([fst, gen]))
    tabs = jnp.stack(tabs)

    attn3, lse3 = _attn_fwd(q, kv, tabs, "attn_fwd")
    attn = attn3.reshape(T, D)
    h3 = _matmul(attn, wo, mode="nn", out_dtype=F32, add=h2, name="o_proj")
    h4, ffn1_saved = ffn_fwd(h3, 1)

    dh4, d_final_norm, sq = _final_loss_bwd(h4, final_norm.reshape(1, D), loss_target.reshape(T, D), "loss_bwd")
    loss_part = 0.5 * jnp.sum(sq) / D

    def ffn_bwd(dh_out, h_in, saved, l):
        xn, u0, act = saved
        dact = _matmul(dh_out, wdn[l], mode="nt", out_dtype=BF16, name=f"ffn{l}_down_dx", tk=1024).reshape(B, S, F)
        d_wdn = _matmul(act, dh_out, mode="tn", out_dtype=F32, name=f"ffn{l}_down_dw", tm=1408, tn=1024, tk=512)
        du0, d_conv, d_conv_b = _ffn_gate_bwd(u0, dact, ffn_conv_f[l], ffn_conv_b[l:l + 1], f"ffn{l}_gate_bwd")
        du0 = du0.reshape(2, T, F)
        dxn = _matmul(du0, wup_t[l], mode="nn", out_dtype=F32, a_parts=2, name=f"ffn{l}_up_dx", tk=1408)
        d_wup_t = _matmul(du0, xn, mode="tn", out_dtype=F32, a_parts=2, name=f"ffn{l}_up_dw", tm=1408, tn=1024, tk=512)
        dh_in, d_norm = _rmsnorm_bwd(h_in, ffn_norm[l:l + 1], [dxn], dh_out, f"ffn{l}_norm_bwd")
        return dh_in, (d_wdn, d_wup_t, d_conv, d_conv_b, d_norm)

    dh3, ffn1_grads = ffn_bwd(dh4, h3, ffn1_saved, 1)

    dattn = _matmul(dh3, wo, mode="nt", out_dtype=F32, name="o_proj_dx").reshape(B, S, D)
    d_wo = _matmul(attn, dh3, mode="tn", out_dtype=F32, name="o_proj_dw", tm=512, tn=1024, tk=512)
    dq, dkv, dtab = _attn_bwd(q, kv, dattn, attn3, lse3, tabs, "attn_bwd")
    dtabs = dtab.transpose(1, 0, 2, 3, 4).reshape(3, N_HEADS, P * 2 * P)
    d_rel_bias = _bias_grad(dtabs, onehots, "rel_bias_grad").T
    dq2 = dq.reshape(T, D)
    dkv2 = dkv.reshape(2, T, D)
    dxnb = _matmul(dq2, wq, mode="nt", out_dtype=F32, name="q_proj_dx")
    d_wq = _matmul(xnb, dq2, mode="tn", out_dtype=F32, name="q_proj_dw", tm=512, tn=1024, tk=512)
    dkvn = _matmul(dkv2, wkv_t, mode="nn", out_dtype=F32, a_parts=2, name="kv_proj_dx")
    d_wkv_t = _matmul(dkv2, kvn, mode="tn", out_dtype=F32, a_parts=2, name="kv_proj_dw", tm=512, tn=1024, tk=512)
    dh2, d_kvb_norm = _rmsnorm_bwd(h2, jnp.stack([kv_norm, b_norm[0]]), [dkvn, dxnb], dh3, "kv_b_norm_bwd")

    dh1, ffn0_grads = ffn_bwd(dh2, h1, ffn0_saved, 0)

    dgated = _matmul(dh1, wout, mode="nt", out_dtype=BF16, name="a_out_proj_dx").reshape(B, S, D)
    d_wout = _matmul(gated, dh1, mode="tn", out_dtype=F32, name="a_out_proj_dw", tm=512, tn=1024, tk=512)
    dbch, d_a_conv = _shortconv_bwd(bch, dgated, a_conv_f, "a_gate_bwd")
    dbch = dbch.reshape(3, T, D)
    dxn0 = _matmul(dbch, win_t, mode="nn", out_dtype=F32, a_parts=3, name="a_in_proj_dx")
    d_win_t = _matmul(dbch, xn0, mode="tn", out_dtype=F32, a_parts=3, name="a_in_proj_dw", tm=512, tn=1024, tk=512)
    grad_x, d_a_norm = _rmsnorm_bwd(x2, a_norm_f, [dxn0], dh1, "a_norm_bwd")

    big_grads = [d_win_t, d_wout, d_wkv_t, d_wq, d_wo, ffn0_grads[1], ffn1_grads[1], ffn0_grads[0], ffn1_grads[0]]
    small_full = [d_a_norm, d_a_conv, jnp.stack([ffn0_grads[2], ffn1_grads[2]]),
                  d_kvb_norm[0], d_kvb_norm[1], d_rel_bias, jnp.concatenate([ffn0_grads[4], ffn1_grads[4]]),
                  jnp.concatenate([ffn0_grads[3], ffn1_grads[3]]), d_final_norm, loss_part]
    small_full_shapes = [(1, D), (3, D), (2, 3, 2 * F), (D,), (1, D), rel_bias.shape, (2, D), (2, 2 * F), (D,), ()]
    gath, recv = _exchange([_pack_rows(small_full)],
                           [g.reshape(N_DEV, g.shape[0] // N_DEV, D) for g in big_grads], "exchange_grads")
    small_sum = _sum_slots(gath[0], "small_grad_sum").reshape(-1)
    (g_a_norm_f, g_a_conv_f, g_ffn_conv_f, g_kv_norm, g_b_norm, g_rel_bias, g_ffn_norm, g_ffn_conv_b,
     g_final_norm, loss) = _unpack(small_sum, small_full_shapes)

    def my_cols(full, width):
        return lax.dynamic_slice_in_dim(full, me * width, width, axis=full.ndim - 1)

    g_a_norm = my_cols(g_a_norm_f, D // N_DEV)
    g_a_conv = my_cols(g_a_conv_f, D // N_DEV)[None]
    g_ffn_conv = my_cols(g_ffn_conv_f, 2 * F // N_DEV)

    big_w = [(a_w_in, m_a_w_in, v_a_w_in, True), (a_w_out, m_a_w_out, v_a_w_out, False),
             (w_kv, m_w_kv, v_w_kv, True), (b_w_q, m_b_w_q, v_b_w_q, False), (b_w_o, m_b_w_o, v_b_w_o, False),
             (ffn_w_up[0], m_ffn_w_up[0], v_ffn_w_up[0], True), (ffn_w_up[1], m_ffn_w_up[1], v_ffn_w_up[1], True),
             (ffn_w_down[0], m_ffn_w_down[0], v_ffn_w_down[0], False),
             (ffn_w_down[1], m_ffn_w_down[1], v_ffn_w_down[1], False)]
    big_out = []
    for i, (w, m, v, transposed) in enumerate(big_w):
        g = _sum_slots(recv[i], f"grad_sum_{i}")
        if transposed:
            g = g.T
        w2, m2, v2 = (t.reshape(g.shape) for t in (w, m, v))
        delta, nm, nv = _adamw(g, w2, m2, v2, f"adamw_{i}")
        big_out.append(tuple(t.reshape(w.shape) for t in (g, delta, nm, nv)))

    def pair(i, j):
        return tuple(jnp.stack([big_out[i][t], big_out[j][t]]) for t in range(4))

    small_w = [(a_norm, m_a_norm, v_a_norm, g_a_norm), (a_conv, m_a_conv, v_a_conv, g_a_conv),
               (ffn_conv, m_ffn_conv, v_ffn_conv, g_ffn_conv), (kv_norm, m_kv_norm, v_kv_norm, g_kv_norm),
               (b_norm, m_b_norm, v_b_norm, g_b_norm), (rel_bias, m_rel_bias, v_rel_bias, g_rel_bias),
               (ffn_norm, m_ffn_norm, v_ffn_norm, g_ffn_norm),
               (ffn_conv_b, m_ffn_conv_b, v_ffn_conv_b, g_ffn_conv_b),
               (final_norm, m_final_norm, v_final_norm, g_final_norm)]
    shapes = [t[0].shape for t in small_w]
    packed = [_pack_rows([t[i].reshape(t[0].shape) for t in small_w]) for i in (3, 0, 1, 2)]
    s_delta, s_nm, s_nv = _adamw(*packed, "adamw_small")
    s_g = [t[3].reshape(t[0].shape) for t in small_w]
    s_d, s_m, s_v = (_unpack(t.reshape(-1), shapes) for t in (s_delta, s_nm, s_nv))
    small_out = [(s_g[i], s_d[i], s_m[i], s_v[i]) for i in range(len(small_w))]

    per_weight = [small_out[0], big_out[0], small_out[1], big_out[1], small_out[3], big_out[2], small_out[4],
                  big_out[3], big_out[4], small_out[5], small_out[6], pair(5, 6), small_out[2], small_out[7],
                  pair(7, 8), small_out[8]]
    outs = [loss, grad_x.reshape(B, S, D)]
    for t in range(4):
        outs.extend(pw[t] for pw in per_weight)
    return tuple(outs)
```

```python
import functools
import math

import jax
import jax.numpy as jnp
from jax import lax
from jax.experimental import pallas as pl
from jax.experimental.pallas import tpu as pltpu

F32 = jnp.float32
BF16 = jnp.bfloat16

N_DEV = 8
N_HEADS = 16
ATT_BLOCK = 128
ATT_UNROLL = 4
DILATED_BRANCHES = ((128, 1), (512, 4), (2048, 16))
REL_BUCKETS = 32
REL_MAX_DISTANCE = 2048
RMS_EPS = 1e-6
ADAM_LR = 0.001
ADAM_B1 = 0.9
ADAM_B2 = 0.999
ADAM_EPS = 1e-08
ADAM_WD = 0.01
ADAM_STEP = 10

LANE = 128
HALO = 16
NEG = -1e30
VMEM_CAP = 56 << 20


def _vmem(block_bytes):
    return int(min(VMEM_CAP, max(32 << 20, 3 * block_bytes + (8 << 20))))


def _nbytes(shape, dtype):
    return math.prod(shape) * jnp.dtype(dtype).itemsize


def _tile(dim, target):
    best = None
    t = LANE
    while t <= min(dim, target):
        if dim % t == 0:
            best = t
        t += LANE
    return best if best is not None else dim


def _matmul(a, b, *, mode, out_dtype, name, add=None, a_parts=1, tm=1024, tn=512, tk=1024):
    P = a_parts
    if mode == "nn":
        M, K = (a.shape[0], a.shape[1]) if P == 1 else (a.shape[1], a.shape[2] * P)
        N = b.shape[1]
    elif mode == "nt":
        assert P == 1
        M, K = a.shape
        N = b.shape[0]
    else:
        K = a.shape[0] if P == 1 else a.shape[1]
        M = a.shape[1] if P == 1 else a.shape[2] * P
        N = b.shape[1]
    tm = _tile(M // P if mode == "tn" else M, tm)
    tn = _tile(N, tn)
    tk = _tile(K // P if mode == "nn" else K, tk)
    nm, nn_, nk = M // tm, N // tn, K // tk

    if mode == "nn":
        if P == 1:
            a_spec = pl.BlockSpec((tm, tk), lambda i, j, k: (i, k))
        else:
            nkp = nk // P
            a_spec = pl.BlockSpec((None, tm, tk), lambda i, j, k: (k // nkp, i, k % nkp))
        b_spec = pl.BlockSpec((tk, tn), lambda i, j, k: (k, j))
        dims = (((1,), (0,)), ((), ()))
    elif mode == "nt":
        a_spec = pl.BlockSpec((tm, tk), lambda i, j, k: (i, k))
        b_spec = pl.BlockSpec((tn, tk), lambda i, j, k: (j, k))
        dims = (((1,), (1,)), ((), ()))
    else:
        if P == 1:
            a_spec = pl.BlockSpec((tk, tm), lambda i, j, k: (k, i))
        else:
            nmp = nm // P
            a_spec = pl.BlockSpec((None, tk, tm), lambda i, j, k: (i // nmp, k, i % nmp))
        b_spec = pl.BlockSpec((tk, tn), lambda i, j, k: (k, j))
        dims = (((0,), (0,)), ((), ()))
    o_spec = pl.BlockSpec((tm, tn), lambda i, j, k: (i, j))
    in_specs = [a_spec, b_spec]
    args = [a, b]
    if add is not None:
        in_specs.append(o_spec)
        args.append(add)
    has_add = add is not None

    def body(a_ref, b_ref, *rest):
        if has_add:
            add_ref, o_ref, acc_ref = rest
        else:
            o_ref, acc_ref = rest
        k = pl.program_id(2)
        part = lax.dot_general(a_ref[...].astype(BF16), b_ref[...].astype(BF16), dims,
                               preferred_element_type=F32)

        def finish(r):
            if has_add:
                r = r + add_ref[...].astype(F32)
            o_ref[...] = r.astype(out_dtype)

        if nk == 1:
            finish(part)
        else:
            @pl.when(k == 0)
            def _():
                acc_ref[...] = part

            @pl.when(jnp.logical_and(k > 0, k < nk - 1))
            def _():
                acc_ref[...] += part

            @pl.when(k == nk - 1)
            def _():
                finish(acc_ref[...] + part)

    blk = (_nbytes((tm, tk), a.dtype) + _nbytes((tk, tn), b.dtype) + _nbytes((tm, tn), out_dtype)
           + (_nbytes((tm, tn), add.dtype) if has_add else 0)) * 2 + 3 * _nbytes((tm, tn), F32)
    return pl.pallas_call(
        body, name=name, grid=(nm, nn_, nk),
        in_specs=in_specs, out_specs=o_spec,
        out_shape=jax.ShapeDtypeStruct((M, N), out_dtype),
        scratch_shapes=[pltpu.VMEM((tm, tn), F32)],
        compiler_params=pltpu.CompilerParams(
            dimension_semantics=("parallel", "parallel", "arbitrary"), vmem_limit_bytes=_vmem(blk)),
    )(*args)


def _rmsnorm_fwd(x, gains, name, tm=512):
    T, D = x.shape
    n = gains.shape[0]
    tm = _tile(T, tm)

    def body(x_ref, g_ref, *o_refs):
        xv = x_ref[...]
        xhat = xv * lax.rsqrt(jnp.mean(xv * xv, axis=-1, keepdims=True) + RMS_EPS)
        for i in range(n):
            o_refs[i][...] = (xhat * g_ref[i:i + 1, :]).astype(BF16)

    row = pl.BlockSpec((tm, D), lambda i: (i, 0))
    outs = pl.pallas_call(
        body, name=name, grid=(T // tm,),
        in_specs=[row, pl.BlockSpec((n, D), lambda i: (0, 0))],
        out_specs=[row] * n,
        out_shape=[jax.ShapeDtypeStruct((T, D), BF16)] * n,
        compiler_params=pltpu.CompilerParams(
            dimension_semantics=("parallel",), vmem_limit_bytes=_vmem(4 * _nbytes((tm, D), F32))),
    )(x, gains)
    return tuple(outs)


def _rmsnorm_bwd(x, gains, dxns, dres, name, tm=512):
    T, D = x.shape
    n = gains.shape[0]
    tm = _tile(T, tm)

    def body(x_ref, g_ref, *rest):
        dxn_refs = rest[:n]
        dres_ref, dx_ref, dg_ref = rest[n:]
        xv = x_ref[...]
        rstd = lax.rsqrt(jnp.mean(xv * xv, axis=-1, keepdims=True) + RMS_EPS)
        xhat = xv * rstd
        dx = dres_ref[...]

        @pl.when(pl.program_id(0) == 0)
        def _():
            dg_ref[...] = jnp.zeros_like(dg_ref)

        for i in range(n):
            dy = dxn_refs[i][...].astype(F32)
            dg_ref[i:i + 1, :] += jnp.sum(dy * xhat, axis=0, keepdims=True)
            dxh = dy * g_ref[i:i + 1, :]
            dx = dx + rstd * (dxh - xhat * jnp.mean(dxh * xhat, axis=-1, keepdims=True))
        dx_ref[...] = dx

    row = pl.BlockSpec((tm, D), lambda i: (i, 0))
    par = pl.BlockSpec((n, D), lambda i: (0, 0))
    return pl.pallas_call(
        body, name=name, grid=(T // tm,),
        in_specs=[row, par] + [row] * n + [row],
        out_specs=[row, par],
        out_shape=[jax.ShapeDtypeStruct((T, D), F32), jax.ShapeDtypeStruct((n, D), F32)],
        compiler_params=pltpu.CompilerParams(
            dimension_semantics=("arbitrary",), vmem_limit_bytes=_vmem((4 + n) * _nbytes((tm, D), F32))),
    )(x, gains, *dxns, dres)


def _final_loss_bwd(h, gain, target, name, tm=512):
    T, D = h.shape
    tm = _tile(T, tm)

    def body(h_ref, g_ref, t_ref, dh_ref, dg_ref, sq_ref):
        xv = h_ref[...]
        rstd = lax.rsqrt(jnp.mean(xv * xv, axis=-1, keepdims=True) + RMS_EPS)
        xhat = xv * rstd
        err = xhat * g_ref[...] - t_ref[...]

        @pl.when(pl.program_id(0) == 0)
        def _():
            dg_ref[...] = jnp.zeros_like(dg_ref)
            sq_ref[...] = jnp.zeros_like(sq_ref)

        sq_ref[...] += jnp.sum(err * err, axis=0, keepdims=True)
        dy = err * (1.0 / D)
        dg_ref[...] += jnp.sum(dy * xhat, axis=0, keepdims=True)
        dxh = dy * g_ref[...]
        dh_ref[...] = rstd * (dxh - xhat * jnp.mean(dxh * xhat, axis=-1, keepdims=True))

    row = pl.BlockSpec((tm, D), lambda i: (i, 0))
    par = pl.BlockSpec((1, D), lambda i: (0, 0))
    return pl.pallas_call(
        body, name=name, grid=(T // tm,),
        in_specs=[row, par, row], out_specs=[row, par, par],
        out_shape=[jax.ShapeDtypeStruct((T, D), F32), jax.ShapeDtypeStruct((1, D), F32),
                   jax.ShapeDtypeStruct((1, D), F32)],
        compiler_params=pltpu.CompilerParams(
            dimension_semantics=("arbitrary",), vmem_limit_bytes=_vmem(5 * _nbytes((tm, D), F32))),
    )(h, gain, target)


def _halo_specs(S, ts, tc, col_off, prev, nxt):
    r = ts // HALO
    last = S // HALO - 1
    specs = []
    if prev:
        specs.append(pl.BlockSpec((None, HALO, tc), lambda j, b, s: (b, jnp.maximum(s * r - 1, 0), col_off + j)))
    specs.append(pl.BlockSpec((None, ts, tc), lambda j, b, s: (b, s, col_off + j)))
    if nxt:
        specs.append(pl.BlockSpec((None, HALO, tc), lambda j, b, s: (b, jnp.minimum((s + 1) * r, last), col_off + j)))
    return specs


def _ext(prev_ref, main_ref, next_ref, first, last):
    main = main_ref[...].astype(F32)
    zeros = jnp.zeros((HALO, main.shape[1]), F32)
    top = zeros if prev_ref is None else jnp.where(first, 0.0, prev_ref[...].astype(F32))
    bot = zeros if next_ref is None else jnp.where(last, 0.0, next_ref[...].astype(F32))
    return jnp.concatenate([top, main, bot], axis=0)


def _shift(xe, k):
    return pltpu.roll(xe, k % xe.shape[0], axis=0)


def _cconv(xe, w):
    return w[0:1, :] * _shift(xe, 2) + w[1:2, :] * _shift(xe, 1) + w[2:3, :] * xe


def _cconv_t(de, w):
    return w[2:3, :] * de + w[1:2, :] * _shift(de, -1) + w[0:1, :] * _shift(de, -2)


def _main(xe, ts):
    return xe[HALO:HALO + ts, :]


def _conv_wgrad(dw_ref, de, xe, ts):
    d = _main(de, ts)
    dw_ref[0:1, :] += jnp.sum(d * _main(_shift(xe, 2), ts), axis=0, keepdims=True)
    dw_ref[1:2, :] += jnp.sum(d * _main(_shift(xe, 1), ts), axis=0, keepdims=True)
    dw_ref[2:3, :] += jnp.sum(d * _main(xe, ts), axis=0, keepdims=True)


def _gate_tiles(S, C):
    return _tile(S, 512), _tile(C, 512)


def _shortconv_fwd(bch, conv_w, name):
    B, S, D3 = bch.shape
    D = D3 // 3
    ts, tc = _gate_tiles(S, D)
    nj, ns = D // tc, S // ts

    def body(b_ref, cp_ref, c_ref, hp_ref, h_ref, w_ref, o_ref):
        first = pl.program_id(2) == 0
        ce = _ext(cp_ref, c_ref, None, first, False)
        he = _ext(hp_ref, h_ref, None, first, False)
        cv = _main(_cconv(ce * he, w_ref[...]), ts)
        o_ref[...] = (b_ref[...].astype(F32) * cv).astype(BF16)

    in_specs = (_halo_specs(S, ts, tc, 0, False, False) + _halo_specs(S, ts, tc, nj, True, False)
                + _halo_specs(S, ts, tc, 2 * nj, True, False) + [pl.BlockSpec((3, tc), lambda j, b, s: (0, j))])
    return pl.pallas_call(
        body, name=name, grid=(nj, B, ns), in_specs=in_specs,
        out_specs=pl.BlockSpec((None, ts, tc), lambda j, b, s: (b, s, j)),
        out_shape=jax.ShapeDtypeStruct((B, S, D), BF16),
        compiler_params=pltpu.CompilerParams(
            dimension_semantics=("parallel", "parallel", "parallel"),
            vmem_limit_bytes=_vmem(12 * _nbytes((ts + 2 * HALO, tc), F32))),
    )(bch, bch, bch, bch, bch, conv_w)


def _shortconv_bwd(bch, dg, conv_w, name):
    B, S, D3 = bch.shape
    D = D3 // 3
    ts, tc = _gate_tiles(S, D)
    nj, ns = D // tc, S // ts

    def body(b_ref, bn_ref, cp_ref, c_ref, cn_ref, hp_ref, h_ref, hn_ref, d_ref, dn_ref, w_ref, o_ref, dw_ref):
        s = pl.program_id(2)
        first, last = s == 0, s == ns - 1

        @pl.when(jnp.logical_and(pl.program_id(1) == 0, s == 0))
        def _():
            dw_ref[...] = jnp.zeros_like(dw_ref)

        w = w_ref[...]
        be = _ext(None, b_ref, bn_ref, first, last)
        ce = _ext(cp_ref, c_ref, cn_ref, first, last)
        he = _ext(hp_ref, h_ref, hn_ref, first, last)
        de = _ext(None, d_ref, dn_ref, first, last)
        ch = ce * he
        dcv = de * be
        dch = _main(_cconv_t(dcv, w), ts)
        o_ref[0] = (_main(de, ts) * _main(_cconv(ch, w), ts)).astype(BF16)
        o_ref[1] = (dch * _main(he, ts)).astype(BF16)
        o_ref[2] = (dch * _main(ce, ts)).astype(BF16)
        _conv_wgrad(dw_ref, dcv, ch, ts)

    in_specs = (_halo_specs(S, ts, tc, 0, False, True) + _halo_specs(S, ts, tc, nj, True, True)
                + _halo_specs(S, ts, tc, 2 * nj, True, True) + _halo_specs(S, ts, tc, 0, False, True)
                + [pl.BlockSpec((3, tc), lambda j, b, s: (0, j))])
    return pl.pallas_call(
        body, name=name, grid=(nj, B, ns), in_specs=in_specs,
        out_specs=[pl.BlockSpec((3, None, ts, tc), lambda j, b, s: (0, b, s, j)),
                   pl.BlockSpec((3, tc), lambda j, b, s: (0, j))],
        out_shape=[jax.ShapeDtypeStruct((3, B, S, D), BF16), jax.ShapeDtypeStruct((3, D), F32)],
        compiler_params=pltpu.CompilerParams(
            dimension_semantics=("parallel", "arbitrary", "arbitrary"),
            vmem_limit_bytes=_vmem(24 * _nbytes((ts + 2 * HALO, tc), F32))),
    )(bch, bch, bch, bch, bch, bch, bch, bch, dg, dg, conv_w)


def _sigmoid(x):
    return 1.0 / (1.0 + jnp.exp(-x))


def _ffn_gate_fwd(u0, conv_w, conv_b, name):
    B, S, F2 = u0.shape
    F = F2 // 2
    ts, tc = _gate_tiles(S, F)
    nj, ns = F // tc, S // ts

    def body(gp_ref, g_ref, up_ref, u_ref, wg_ref, wu_ref, bg_ref, bu_ref, o_ref):
        first = pl.program_id(2) == 0
        ug = _main(_cconv(_ext(gp_ref, g_ref, None, first, False), wg_ref[...]), ts) + bg_ref[...]
        uu = _main(_cconv(_ext(up_ref, u_ref, None, first, False), wu_ref[...]), ts) + bu_ref[...]
        o_ref[...] = (ug * _sigmoid(ug) * uu).astype(BF16)

    in_specs = (_halo_specs(S, ts, tc, 0, True, False) + _halo_specs(S, ts, tc, nj, True, False)
                + [pl.BlockSpec((3, tc), lambda j, b, s: (0, j)), pl.BlockSpec((3, tc), lambda j, b, s: (0, nj + j)),
                   pl.BlockSpec((1, tc), lambda j, b, s: (0, j)), pl.BlockSpec((1, tc), lambda j, b, s: (0, nj + j))])
    return pl.pallas_call(
        body, name=name, grid=(nj, B, ns), in_specs=in_specs,
        out_specs=pl.BlockSpec((None, ts, tc), lambda j, b, s: (b, s, j)),
        out_shape=jax.ShapeDtypeStruct((B, S, F), BF16),
        compiler_params=pltpu.CompilerParams(
            dimension_semantics=("parallel", "parallel", "parallel"),
            vmem_limit_bytes=_vmem(12 * _nbytes((ts + 2 * HALO, tc), F32))),
    )(u0, u0, u0, u0, conv_w, conv_w, conv_b, conv_b)


def _ffn_gate_bwd(u0, dact, conv_w, conv_b, name):
    B, S, F2 = u0.shape
    F = F2 // 2
    ts, tc = _gate_tiles(S, F)
    nj, ns = F // tc, S // ts

    def body(gp_ref, g_ref, gn_ref, up_ref, u_ref, un_ref, d_ref, dn_ref, wg_ref, wu_ref, bg_ref, bu_ref,
             o_ref, dwg_ref, dwu_ref, dbg_ref, dbu_ref):
        s = pl.program_id(2)
        first, last = s == 0, s == ns - 1

        @pl.when(jnp.logical_and(pl.program_id(1) == 0, s == 0))
        def _():
            for r in (dwg_ref, dwu_ref, dbg_ref, dbu_ref):
                r[...] = jnp.zeros_like(r)

        wg, wu = wg_ref[...], wu_ref[...]
        ge = _ext(gp_ref, g_ref, gn_ref, first, last)
        ue = _ext(up_ref, u_ref, un_ref, first, last)
        de = _ext(None, d_ref, dn_ref, first, last)
        ug = _cconv(ge, wg) + bg_ref[...]
        uu = _cconv(ue, wu) + bu_ref[...]
        sig = _sigmoid(ug)
        dug = de * uu * (sig * (1.0 + ug * (1.0 - sig)))
        duu = de * (ug * sig)
        o_ref[0] = _main(_cconv_t(dug, wg), ts).astype(BF16)
        o_ref[1] = _main(_cconv_t(duu, wu), ts).astype(BF16)
        _conv_wgrad(dwg_ref, dug, ge, ts)
        _conv_wgrad(dwu_ref, duu, ue, ts)
        dbg_ref[...] += jnp.sum(_main(dug, ts), axis=0, keepdims=True)
        dbu_ref[...] += jnp.sum(_main(duu, ts), axis=0, keepdims=True)

    w3 = lambda off: pl.BlockSpec((3, tc), lambda j, b, s: (0, off + j))
    w1 = lambda off: pl.BlockSpec((1, tc), lambda j, b, s: (0, off + j))
    in_specs = (_halo_specs(S, ts, tc, 0, True, True) + _halo_specs(S, ts, tc, nj, True, True)
                + _halo_specs(S, ts, tc, 0, False, True) + [w3(0), w3(nj), w1(0), w1(nj)])
    outs = pl.pallas_call(
        body, name=name, grid=(nj, B, ns), in_specs=in_specs,
        out_specs=[pl.BlockSpec((2, None, ts, tc), lambda j, b, s: (0, b, s, j)), w3(0), w3(0), w1(0), w1(0)],
        out_shape=[jax.ShapeDtypeStruct((2, B, S, F), BF16), jax.ShapeDtypeStruct((3, F), F32),
                   jax.ShapeDtypeStruct((3, F), F32), jax.ShapeDtypeStruct((1, F), F32),
                   jax.ShapeDtypeStruct((1, F), F32)],
        compiler_params=pltpu.CompilerParams(
            dimension_semantics=("parallel", "arbitrary", "arbitrary"),
            vmem_limit_bytes=_vmem(30 * _nbytes((ts + 2 * HALO, tc), F32))),
    )(u0, u0, u0, u0, u0, u0, dact, dact, conv_w, conv_w, conv_b, conv_b)
    du0, dwg, dwu, dbg, dbu = outs
    return du0, jnp.concatenate([dwg, dwu], axis=1), jnp.concatenate([dbg, dbu], axis=1)


def _t5_bucket(dist):
    max_exact = REL_BUCKETS // 2
    n = jnp.maximum(dist, 0)
    nf = jnp.maximum(n, max_exact).astype(F32)
    large = max_exact + (jnp.log(nf / max_exact) / math.log(REL_MAX_DISTANCE / max_exact)
                         * (REL_BUCKETS - max_exact)).astype(jnp.int32)
    large = jnp.minimum(large, REL_BUCKETS - 1)
    return jnp.where(n < max_exact, n, large)


def _band_tables(window, dilation):
    P = ATT_BLOCK
    qi = jnp.arange(P, dtype=jnp.int32)[:, None]
    kc = jnp.arange(2 * P, dtype=jnp.int32)[None, :]
    delta = qi + P - kc
    band = (delta >= 0) & (delta <= window // dilation)
    bucket = _t5_bucket(delta * dilation).reshape(-1)
    onehot = (bucket[None, :] == jnp.arange(REL_BUCKETS, dtype=jnp.int32)[:, None]).astype(F32)
    return onehot, band


def _bias_lookup(rel_bias_t, onehots, name):
    nb, _, Q = onehots.shape
    H = rel_bias_t.shape[0]

    def body(r_ref, oh_ref, o_ref):
        o_ref[...] = lax.dot_general(r_ref[...], oh_ref[...], (((1,), (0,)), ((), ())),
                                     precision=lax.Precision.HIGHEST, preferred_element_type=F32)

    return pl.pallas_call(
        body, name=name, grid=(nb,),
        in_specs=[pl.BlockSpec((H, REL_BUCKETS), lambda i: (0, 0)),
                  pl.BlockSpec((None, REL_BUCKETS, Q), lambda i: (i, 0, 0))],
        out_specs=pl.BlockSpec((None, H, Q), lambda i: (i, 0, 0)),
        out_shape=jax.ShapeDtypeStruct((nb, H, Q), F32),
        compiler_params=pltpu.CompilerParams(dimension_semantics=("parallel",),
                                             vmem_limit_bytes=_vmem(_nbytes((REL_BUCKETS + H, Q), F32))),
    )(rel_bias_t, onehots)


def _bias_grad(dtabs, onehots, name):
    nb, H, Q = dtabs.shape

    def body(d_ref, oh_ref, o_ref):
        @pl.when(pl.program_id(0) == 0)
        def _():
            o_ref[...] = jnp.zeros_like(o_ref)

        o_ref[...] += lax.dot_general(d_ref[...], oh_ref[...], (((1,), (1,)), ((), ())),
                                      precision=lax.Precision.HIGHEST, preferred_element_type=F32)

    return pl.pallas_call(
        body, name=name, grid=(nb,),
        in_specs=[pl.BlockSpec((None, H, Q), lambda i: (i, 0, 0)),
                  pl.BlockSpec((None, REL_BUCKETS, Q), lambda i: (i, 0, 0))],
        out_specs=pl.BlockSpec((H, REL_BUCKETS), lambda i: (0, 0)),
        out_shape=jax.ShapeDtypeStruct((H, REL_BUCKETS), F32),
        compiler_params=pltpu.CompilerParams(dimension_semantics=("arbitrary",),
                                             vmem_limit_bytes=_vmem(_nbytes((REL_BUCKETS + H, Q), F32))),
    )(dtabs, onehots)


def _lane_masks():
    lane = lax.broadcasted_iota(jnp.int32, (1, LANE), 1)
    lo = lane < LANE // 2
    return lo, jnp.logical_not(lo)


def _dot_nt(a, b):
    return lax.dot_general(a, b, (((1,), (1,)), ((), ())), preferred_element_type=F32)


def _dot_nn(a, b):
    return lax.dot_general(a, b, (((1,), (0,)), ((), ())), preferred_element_type=F32)


def _dot_tn(a, b):
    return lax.dot_general(a, b, (((0,), (0,)), ((), ())), preferred_element_type=F32)


def _block_rows(n, dilation, S):
    P = ATT_BLOCK
    nb = S // (dilation * P)
    r, i = n // nb, n % nb
    cur = pl.ds(i * (P * dilation) + r, P, stride=dilation)
    prv = pl.ds(jnp.maximum(i - 1, 0) * (P * dilation) + r, P, stride=dilation)
    return cur, prv, jnp.minimum(i, 1)


def _attn_fwd(q, kv, tabs, name):
    B, S, D = q.shape
    P, H = ATT_BLOCK, N_HEADS
    scale = (D // H) ** -0.5
    half = LANE // 2
    nsteps = S // P // ATT_UNROLL

    def body(q_ref, k_ref, v_ref, tab_ref, o_ref, lse_ref, acc_ref, m_ref, l_ref):
        lo, hi = _lane_masks()
        for bi, (_, d) in enumerate(DILATED_BRANCHES):
            def block(n, carry, bi=bi, d=d):
                rows = [_block_rows(n + j * nsteps, d, S) for j in range(ATT_UNROLL)]
                loaded = []
                for cur, prv, variant in rows:
                    qb = q_ref[cur, :].astype(BF16)
                    kw = jnp.concatenate([k_ref[prv, :], k_ref[cur, :]], axis=0).astype(BF16)
                    vw = jnp.concatenate([v_ref[prv, :], v_ref[cur, :]], axis=0).astype(BF16)
                    old = (m_ref[cur, :], l_ref[cur, :], acc_ref[cur, :]) if bi > 0 else None
                    loaded.append((qb, kw, vw, old))
                results = []
                for (cur, prv, variant), (qb, kw, vw, old) in zip(rows, loaded):
                    m_pair = l_pair = a_pair = None
                    for e, msk in enumerate((lo, hi)):
                        s = _dot_nt(jnp.where(msk, qb, 0), kw) * scale + tab_ref[bi, variant, e]
                        m_e = jnp.max(s, axis=-1, keepdims=True)
                        if bi > 0:
                            m_o = old[0][:, e * half:e * half + 1]
                            m_e = jnp.maximum(m_o, m_e)
                            alpha = jnp.exp(m_o - m_e)
                        pe = jnp.exp(s - m_e)
                        l_e = jnp.sum(pe, axis=-1, keepdims=True)
                        a_e = _dot_nn(pe.astype(BF16), jnp.where(msk, vw, 0))
                        if bi > 0:
                            l_e = l_e + alpha * old[1][:, e * half:e * half + 1]
                            a_e = a_e + alpha * jnp.where(msk, old[2], 0.0)
                        m_pair = m_e if e == 0 else jnp.where(lo, m_pair, m_e)
                        l_pair = l_e if e == 0 else jnp.where(lo, l_pair, l_e)
                        a_pair = a_e if e == 0 else a_pair + a_e
                    results.append((m_pair, l_pair, a_pair))
                for (cur, prv, variant), (m_pair, l_pair, a_pair) in zip(rows, results):
                    m_ref[cur, :] = m_pair
                    l_ref[cur, :] = l_pair
                    acc_ref[cur, :] = a_pair
                return carry

            lax.fori_loop(0, nsteps, block, 0)
        l = l_ref[...]
        o_ref[...] = (acc_ref[...] / l).astype(BF16)
        lse_ref[...] = m_ref[...] + jnp.log(l)

    nl = D // LANE
    col = lambda off: pl.BlockSpec((None, S, LANE), lambda b, p: (b, 0, off + p))
    tab_spec = pl.BlockSpec((3, 2, 2, P, 2 * P), lambda b, p: (0, 0, p, 0, 0))
    return pl.pallas_call(
        body, name=name, grid=(B, H // 2), in_specs=[col(0), col(0), col(nl), tab_spec],
        out_specs=[col(0), col(0)],
        out_shape=[jax.ShapeDtypeStruct((B, S, D), BF16), jax.ShapeDtypeStruct((B, S, D), F32)],
        scratch_shapes=[pltpu.VMEM((S, LANE), F32)] * 3,
        compiler_params=pltpu.CompilerParams(
            dimension_semantics=("parallel", "parallel"),
            vmem_limit_bytes=_vmem(5 * _nbytes((S, LANE), F32) + _nbytes((12, P, 2 * P), F32))),
    )(q, kv, kv, tabs)


def _attn_bwd(q, kv, do, o, lse, tabs, name):
    B, S, D = q.shape
    P, H = ATT_BLOCK, N_HEADS
    scale = (D // H) ** -0.5
    half = LANE // 2
    nsteps = S // P // ATT_UNROLL

    def body(q_ref, k_ref, v_ref, do_ref, o_ref, lse_ref, tab_ref, dq_ref, dkv_ref, dtab_ref, delta_ref):
        lo, hi = _lane_masks()

        @pl.when(pl.program_id(1) == 0)
        def _():
            dtab_ref[...] = jnp.zeros_like(dtab_ref)

        dkv_ref[...] = jnp.zeros_like(dkv_ref)
        prod = do_ref[...] * o_ref[...].astype(F32)
        delta_ref[...] = jnp.where(lo, jnp.sum(jnp.where(lo, prod, 0.0), axis=-1, keepdims=True),
                                   jnp.sum(jnp.where(hi, prod, 0.0), axis=-1, keepdims=True))

        for bi, (_, d) in enumerate(DILATED_BRANCHES):
            def block(n, carry, bi=bi, d=d):
                rows = [_block_rows(n + j * nsteps, d, S) for j in range(ATT_UNROLL)]
                loaded = []
                for cur, prv, variant in rows:
                    qb = q_ref[cur, :].astype(BF16)
                    kw = jnp.concatenate([k_ref[prv, :], k_ref[cur, :]], axis=0).astype(BF16)
                    vw = jnp.concatenate([v_ref[prv, :], v_ref[cur, :]], axis=0).astype(BF16)
                    dob = do_ref[cur, :].astype(BF16)
                    dq_old = dq_ref[cur, :] if bi > 0 else None
                    loaded.append((qb, kw, vw, dob, lse_ref[cur, :], delta_ref[cur, :], dq_old))
                results = []
                for j, ((cur, prv, variant), (qb, kw, vw, dob, lse_b, dl_b, dq_old)) in enumerate(zip(rows, loaded)):
                    dq_pair = dk_pair = dv_pair = None
                    for e, msk in enumerate((lo, hi)):
                        qm = jnp.where(msk, qb, 0)
                        dom = jnp.where(msk, dob, 0)
                        s = _dot_nt(qm, kw) * scale + tab_ref[bi, variant, e]
                        pe = jnp.exp(s - lse_b[:, e * half:e * half + 1])
                        ds = pe * (_dot_nt(dom, vw) - dl_b[:, e * half:e * half + 1])
                        dtab_ref[bi, e] += ds
                        dsb = ds.astype(BF16)
                        dq_e = _dot_nn(dsb, jnp.where(msk, kw, 0))
                        dk_e = _dot_tn(dsb, qm)
                        dv_e = _dot_tn(pe.astype(BF16), dom)
                        dq_pair = dq_e if e == 0 else dq_pair + dq_e
                        dk_pair = dk_e if e == 0 else dk_pair + dk_e
                        dv_pair = dv_e if e == 0 else dv_pair + dv_e
                    dq_pair = dq_pair * scale
                    if bi > 0:
                        dq_pair = dq_pair + dq_old
                    results.append((dq_pair, dk_pair * scale, dv_pair))
                for (cur, prv, variant), (dq_pair, dk_pair, dv_pair) in zip(rows, results):
                    dq_ref[cur, :] = dq_pair
                    dkv_ref[0, cur, :] += dk_pair[P:, :]
                    dkv_ref[1, cur, :] += dv_pair[P:, :]
                    dkv_ref[0, prv, :] += dk_pair[:P, :]
                    dkv_ref[1, prv, :] += dv_pair[:P, :]
                return carry

            lax.fori_loop(0, nsteps, block, 0)

    nl = D // LANE
    col = lambda off: pl.BlockSpec((None, S, LANE), lambda p, b: (b, 0, off + p))
    tab_spec = pl.BlockSpec((3, 2, 2, P, 2 * P), lambda p, b: (0, 0, p, 0, 0))
    return pl.pallas_call(
        body, name=name, grid=(H // 2, B),
        in_specs=[col(0), col(0), col(nl), col(0), col(0), col(0), tab_spec],
        out_specs=[col(0), pl.BlockSpec((2, None, S, LANE), lambda p, b: (0, b, 0, p)),
                   pl.BlockSpec((None, 3, 2, P, 2 * P), lambda p, b: (p, 0, 0, 0, 0))],
        out_shape=[jax.ShapeDtypeStruct((B, S, D), F32), jax.ShapeDtypeStruct((2, B, S, D), F32),
                   jax.ShapeDtypeStruct((H // 2, 3, 2, P, 2 * P), F32)],
        scratch_shapes=[pltpu.VMEM((S, LANE), F32)],
        compiler_params=pltpu.CompilerParams(
            dimension_semantics=("parallel", "arbitrary"),
            vmem_limit_bytes=VMEM_CAP),
    )(q, kv, kv, do, o, lse, tabs)


def _attn_fwd_branch(q, kv, tab, dilation, name):
    B, S, D = q.shape
    P, d = ATT_BLOCK, dilation
    L = S // d
    nb = L // P
    H = N_HEADS
    scale = (D // H) ** -0.5
    qv = q.reshape(B, L, d * D)
    kvv = kv.reshape(B, L, d * 2 * D)

    def body(q_ref, kp_ref, kc_ref, vp_ref, vc_ref, tab_ref, o_ref, lse_ref):
        lo, hi = _lane_masks()
        for p in range(H // 2):
            sl = slice(p * LANE, (p + 1) * LANE)
            qp = q_ref[:, sl]
            kp = jnp.concatenate([kp_ref[:, sl], kc_ref[:, sl]], axis=0)
            vp = jnp.concatenate([vp_ref[:, sl], vc_ref[:, sl]], axis=0)
            o_pair = None
            lse_pair = None
            for e, msk in enumerate((lo, hi)):
                s = _dot_nt(jnp.where(msk, qp, 0), kp) * scale + tab_ref[2 * p + e]
                m = jnp.max(s, axis=-1, keepdims=True)
                pe = jnp.exp(s - m)
                den = jnp.sum(pe, axis=-1, keepdims=True)
                o_e = _dot_nn(pe.astype(BF16), jnp.where(msk, vp, 0)) / den
                lse_e = m + jnp.log(den)
                o_pair = o_e if e == 0 else o_pair + o_e
                lse_pair = lse_e if e == 0 else jnp.where(lo, lse_pair, lse_e)
            o_ref[:, sl] = o_pair
            lse_ref[:, sl] = lse_pair

    blk = lambda f: pl.BlockSpec((None, P, D), f)
    prev = lambda i: jnp.maximum(i - 1, 0)
    in_specs = [blk(lambda b, r, i: (b, i, r)),
                blk(lambda b, r, i: (b, prev(i), 2 * r)), blk(lambda b, r, i: (b, i, 2 * r)),
                blk(lambda b, r, i: (b, prev(i), 2 * r + 1)), blk(lambda b, r, i: (b, i, 2 * r + 1)),
                pl.BlockSpec((None, H, P, 2 * P), lambda b, r, i: (jnp.minimum(i, 1), 0, 0, 0))]
    o, lse = pl.pallas_call(
        body, name=name, grid=(B, d, nb), in_specs=in_specs,
        out_specs=[blk(lambda b, r, i: (b, i, r))] * 2,
        out_shape=[jax.ShapeDtypeStruct((B, L, d * D), F32)] * 2,
        compiler_params=pltpu.CompilerParams(
            dimension_semantics=("parallel", "parallel", "parallel"),
            vmem_limit_bytes=_vmem(_nbytes((H, P, 2 * P), F32) + 9 * _nbytes((P, D), F32))),
    )(qv, kvv, kvv, kvv, kvv, tab)
    return o.reshape(B * S, D), lse.reshape(B * S, D)


def _attn_merge(os_, lses, name, tm=512):
    T, D = os_[0].shape
    n = len(os_)
    tm = _tile(T, tm)

    def body(*refs):
        o_refs, l_refs = refs[:n], refs[n:2 * n]
        out_ref, lse_ref = refs[2 * n:]
        ls = [r[...] for r in l_refs]
        m = functools.reduce(jnp.maximum, ls)
        ws = [jnp.exp(l - m) for l in ls]
        tot = functools.reduce(jnp.add, ws)
        acc = functools.reduce(jnp.add, [w * r[...] for w, r in zip(ws, o_refs)])
        out_ref[...] = (acc / tot).astype(BF16)
        lse_ref[...] = m + jnp.log(tot)

    row = pl.BlockSpec((tm, D), lambda i: (i, 0))
    return pl.pallas_call(
        body, name=name, grid=(T // tm,), in_specs=[row] * (2 * n), out_specs=[row, row],
        out_shape=[jax.ShapeDtypeStruct((T, D), BF16), jax.ShapeDtypeStruct((T, D), F32)],
        compiler_params=pltpu.CompilerParams(
            dimension_semantics=("parallel",), vmem_limit_bytes=_vmem((2 * n + 4) * _nbytes((tm, D), F32))),
    )(*os_, *lses)


def _attn_bwd_q(q, kv, do, o, lse, tab, dq_acc, dilation, name):
    B, S, D = q.shape
    P, d = ATT_BLOCK, dilation
    L = S // d
    nb = L // P
    H = N_HEADS
    scale = (D // H) ** -0.5
    view = lambda t, c: t.reshape(B, L, d * c)
    has_acc = dq_acc is not None

    def body(q_ref, kp_ref, kc_ref, vp_ref, vc_ref, do_ref, o_ref, lse_ref, tab_ref, *rest):
        if has_acc:
            acc_ref, dq_ref, dtab_ref = rest
        else:
            dq_ref, dtab_ref = rest

        @pl.when((pl.program_id(0) == 0) & (pl.program_id(1) == 0) & (pl.program_id(2) == 0))
        def _():
            dtab_ref[...] = jnp.zeros_like(dtab_ref)

        lo, hi = _lane_masks()
        for p in range(H // 2):
            sl = slice(p * LANE, (p + 1) * LANE)
            qp = q_ref[:, sl]
            kp = jnp.concatenate([kp_ref[:, sl], kc_ref[:, sl]], axis=0)
            vp = jnp.concatenate([vp_ref[:, sl], vc_ref[:, sl]], axis=0)
            dop = do_ref[:, sl]
            prod = dop.astype(F32) * o_ref[:, sl].astype(F32)
            lsep = lse_ref[:, sl]
            dq_pair = None
            for e, msk in enumerate((lo, hi)):
                s = _dot_nt(jnp.where(msk, qp, 0), kp) * scale + tab_ref[2 * p + e]
                pe = jnp.exp(s - lsep[:, e * (LANE // 2):e * (LANE // 2) + 1])
                dp = _dot_nt(jnp.where(msk, dop, 0).astype(BF16), vp)
                delta = jnp.sum(jnp.where(msk, prod, 0.0), axis=-1, keepdims=True)
                ds = pe * (dp - delta)
                dtab_ref[2 * p + e] += ds
                dq_e = _dot_nn(ds.astype(BF16), jnp.where(msk, kp, 0))
                dq_pair = dq_e if e == 0 else dq_pair + dq_e
            dq_pair = dq_pair * scale
            if has_acc:
                dq_pair = dq_pair + acc_ref[:, sl]
            dq_ref[:, sl] = dq_pair

    blk = lambda f: pl.BlockSpec((None, P, D), f)
    prev = lambda i: jnp.maximum(i - 1, 0)
    own = blk(lambda b, r, i: (b, i, r))
    tab_spec = pl.BlockSpec((None, H, P, 2 * P), lambda b, r, i: (jnp.minimum(i, 1), 0, 0, 0))
    in_specs = [own,
                blk(lambda b, r, i: (b, prev(i), 2 * r)), blk(lambda b, r, i: (b, i, 2 * r)),
                blk(lambda b, r, i: (b, prev(i), 2 * r + 1)), blk(lambda b, r, i: (b, i, 2 * r + 1)),
                own, own, own, tab_spec]
    kvv = view(kv, 2 * D)
    args = [view(q, D), kvv, kvv, kvv, kvv, view(do, D), view(o, D), view(lse, D), tab]
    if has_acc:
        in_specs.append(own)
        args.append(view(dq_acc, D))
    dq, dtab = pl.pallas_call(
        body, name=name, grid=(B, d, nb), in_specs=in_specs,
        out_specs=[own, pl.BlockSpec((H, P, 2 * P), lambda b, r, i: (0, 0, 0))],
        out_shape=[jax.ShapeDtypeStruct((B, L, d * D), F32), jax.ShapeDtypeStruct((H, P, 2 * P), F32)],
        compiler_params=pltpu.CompilerParams(
            dimension_semantics=("arbitrary", "arbitrary", "arbitrary"),
            vmem_limit_bytes=_vmem(2 * _nbytes((H, P, 2 * P), F32) + 12 * _nbytes((P, D), F32))),
    )(*args)
    return dq.reshape(B, S, D), dtab


def _attn_bwd_kv(q, kv, do, o, lse, tabk, dkv_acc, dilation, name):
    B, S, D = q.shape
    P, d = ATT_BLOCK, dilation
    L = S // d
    nb = L // P
    H = N_HEADS
    scale = (D // H) ** -0.5
    view = lambda t, c: t.reshape(B, L, d * c)
    has_acc = dkv_acc is not None

    def body(k_ref, v_ref, qa_ref, qb_ref, doa_ref, dob_ref, oa_ref, ob_ref, la_ref, lb_ref, tab_ref, *rest):
        if has_acc:
            acc_ref, dkv_ref = rest
        else:
            (dkv_ref,) = rest
        lo, hi = _lane_masks()
        for p in range(H // 2):
            sl = slice(p * LANE, (p + 1) * LANE)
            kp, vp = k_ref[:, sl], v_ref[:, sl]
            q2 = jnp.concatenate([qa_ref[:, sl], qb_ref[:, sl]], axis=0)
            do2 = jnp.concatenate([doa_ref[:, sl], dob_ref[:, sl]], axis=0)
            o2 = jnp.concatenate([oa_ref[:, sl], ob_ref[:, sl]], axis=0)
            lse2 = jnp.concatenate([la_ref[:, sl], lb_ref[:, sl]], axis=0)
            prod = do2.astype(F32) * o2.astype(F32)
            dk_pair = None
            dv_pair = None
            for e, msk in enumerate((lo, hi)):
                qm = jnp.where(msk, q2, 0)
                dom = jnp.where(msk, do2, 0).astype(BF16)
                s = _dot_nt(qm, kp) * scale + tab_ref[2 * p + e]
                pe = jnp.exp(s - lse2[:, e * (LANE // 2):e * (LANE // 2) + 1])
                dp = _dot_nt(dom, vp)
                delta = jnp.sum(jnp.where(msk, prod, 0.0), axis=-1, keepdims=True)
                ds = pe * (dp - delta)
                dv_e = _dot_tn(pe.astype(BF16), dom)
                dk_e = _dot_tn(ds.astype(BF16), qm)
                dk_pair = dk_e if e == 0 else dk_pair + dk_e
                dv_pair = dv_e if e == 0 else dv_pair + dv_e
            dk_pair = dk_pair * scale
            if has_acc:
                dk_pair = dk_pair + acc_ref[0, :, sl]
                dv_pair = dv_pair + acc_ref[1, :, sl]
            dkv_ref[0, :, sl] = dk_pair
            dkv_ref[1, :, sl] = dv_pair

    blk = lambda f: pl.BlockSpec((None, P, D), f)
    nxt = lambda i: jnp.minimum(i + 1, nb - 1)
    own = blk(lambda b, r, i: (b, i, r))
    nx = blk(lambda b, r, i: (b, nxt(i), r))
    tab_spec = pl.BlockSpec((None, H, 2 * P, P), lambda b, r, i: (jnp.where(i == nb - 1, 1, 0), 0, 0, 0))
    pair = pl.BlockSpec((2, None, P, D), lambda b, r, i: (0, b, i, r))
    in_specs = [blk(lambda b, r, i: (b, i, 2 * r)), blk(lambda b, r, i: (b, i, 2 * r + 1)),
                own, nx, own, nx, own, nx, own, nx, tab_spec]
    kvv, qv, dov, ov, lv = view(kv, 2 * D), view(q, D), view(do, D), view(o, D), view(lse, D)
    args = [kvv, kvv, qv, qv, dov, dov, ov, ov, lv, lv, tabk]
    if has_acc:
        in_specs.append(pair)
        args.append(dkv_acc.reshape(2, B, L, d * D))
    dkv = pl.pallas_call(
        body, name=name, grid=(B, d, nb), in_specs=in_specs, out_specs=pair,
        out_shape=jax.ShapeDtypeStruct((2, B, L, d * D), F32),
        compiler_params=pltpu.CompilerParams(
            dimension_semantics=("parallel", "parallel", "parallel"),
            vmem_limit_bytes=_vmem(_nbytes((H, P, 2 * P), F32) + 16 * _nbytes((P, D), F32))),
    )(*args)
    return dkv.reshape(2, B, S, D)


def _mesh_place():
    x, y, c = lax.axis_index("x"), lax.axis_index("y"), lax.axis_index("c")
    return x, y, c, (x, y, 1 - c), [(1 - x, y), (x, 1 - y), (1 - x, 1 - y)]


def _rdma(src, dst, send_sems, recv_sems, idx, to):
    return pltpu.make_async_remote_copy(
        src_ref=src, dst_ref=dst, send_sem=send_sems.at[idx], recv_sem=recv_sems.at[idx],
        device_id=to, device_id_type=pl.DeviceIdType.MESH)


def _comm_call(body, name, args, out_shape, n_sems, n_local):
    any_spec = pl.BlockSpec(memory_space=pl.ANY)
    return pl.pallas_call(
        body, name=name, in_specs=[any_spec] * len(args), out_specs=[any_spec] * len(out_shape),
        out_shape=out_shape,
        scratch_shapes=[pltpu.SemaphoreType.DMA(n_sems), pltpu.SemaphoreType.DMA(n_sems),
                        pltpu.SemaphoreType.DMA((n_local,))],
        compiler_params=pltpu.CompilerParams(has_side_effects=True),
    )(*args)


def _all_gather(arrays, name):
    n = len(arrays)

    def body(*refs):
        ins, outs = refs[:n], refs[n:2 * n]
        send_sems, recv_sems, loc_sems = refs[2 * n:]
        x, y, c, sib, chips = _mesh_place()
        me = (x, y, c)

        def rows(a, dev):
            return outs[a].at[4 * dev[0] + 2 * dev[1] + dev[2]]

        def copy(a, k, block, to, src=None):
            return _rdma(rows(a, block) if src is None else src, rows(a, block), send_sems, recv_sems, (a, k), to)

        local, first, passed = [], [], []
        for a in range(n):
            cp = pltpu.make_async_copy(ins[a], rows(a, me), loc_sems.at[a])
            cp.start()
            local.append(cp)
            first.append(copy(a, 0, me, sib, src=ins[a]))
            first += [copy(a, 1 + j, me, (*chip, c), src=ins[a]) for j, chip in enumerate(chips)]
        for cp in first:
            cp.start()
        for j, chip in enumerate(chips):
            for a in range(n):
                copy(a, 1 + j, (*chip, c), me).wait_recv()
                cp = copy(a, 4 + j, (*chip, c), sib)
                cp.start()
                passed.append(cp)
        for a in range(n):
            copy(a, 0, sib, me).wait_recv()
            for j, chip in enumerate(chips):
                copy(a, 4 + j, (*chip, 1 - c), me).wait_recv()
        for cp in first + passed:
            cp.wait_send()
        for cp in local:
            cp.wait()

    out_shape = [jax.ShapeDtypeStruct((N_DEV,) + g.shape, g.dtype) for g in arrays]
    return list(_comm_call(body, name, arrays, out_shape, (n, N_DEV - 1), n))


def _core_pair_exchange(halves, gathered, name):
    nh, ng = len(halves), len(gathered)

    def body(*refs):
        h_in, g_in = refs[:nh], refs[nh:nh + ng]
        outs = refs[nh + ng:nh + ng + 2 * nh + ng]
        mine, theirs, g_out = outs[:nh], outs[nh:2 * nh], outs[2 * nh:]
        send_sems, recv_sems, loc_sems = refs[nh + ng + 2 * nh + ng:]
        x, y, c, sib, chips = _mesh_place()
        me = 4 * x + 2 * y + c
        peers = [sib] + [(*chip, pc) for chip in chips for pc in (c, 1 - c)]
        local, sends = [], []
        for a in range(nh):
            cp = pltpu.make_async_copy(h_in[a].at[:, c], mine[a], loc_sems.at[a])
            cp.start()
            local.append(cp)
            cp = _rdma(h_in[a].at[:, 1 - c], theirs[a], send_sems, recv_sems, a, sib)
            cp.start()
            sends.append(cp)
        for g in range(ng):
            cp = pltpu.make_async_copy(g_in[g], g_out[g].at[me], loc_sems.at[nh + g])
            cp.start()
            local.append(cp)
            for k, peer in enumerate(peers):
                cp = _rdma(g_in[g], g_out[g].at[me], send_sems, recv_sems, nh + g * (N_DEV - 1) + k, peer)
                cp.start()
                sends.append(cp)
        for cp in sends:
            cp.wait_send()
        for a in range(nh):
            _rdma(h_in[a].at[:, 1 - c], theirs[a], send_sems, recv_sems, a, sib).wait_recv()
        for g in range(ng):
            for k, peer in enumerate(peers):
                pid = 4 * peer[0] + 2 * peer[1] + peer[2]
                _rdma(g_in[g], g_out[g].at[pid], send_sems, recv_sems, nh + g * (N_DEV - 1) + k, peer).wait_recv()
        for cp in local:
            cp.wait()

    half = [jax.ShapeDtypeStruct((h.shape[0],) + h.shape[2:], h.dtype) for h in halves]
    out_shape = half + half + [jax.ShapeDtypeStruct((N_DEV,) + g.shape, g.dtype) for g in gathered]
    outs = _comm_call(body, name, list(halves) + list(gathered), out_shape, (nh + ng * (N_DEV - 1),), nh + ng)
    return list(outs[:nh]), list(outs[nh:2 * nh]), list(outs[2 * nh:])


def _chip_exchange(arrays, name):
    n = len(arrays)

    def body(*refs):
        ins, outs = refs[:n], refs[n:2 * n]
        send_sems, recv_sems, loc_sems = refs[2 * n:]
        x, y, c, sib, chips = _mesh_place()
        my_chip = 2 * x + y
        local, sends = [], []
        for a in range(n):
            cp = pltpu.make_async_copy(ins[a].at[my_chip], outs[a].at[my_chip], loc_sems.at[a])
            cp.start()
            local.append(cp)
            for j, chip in enumerate(chips):
                cp = _rdma(ins[a].at[2 * chip[0] + chip[1]], outs[a].at[my_chip], send_sems, recv_sems, (a, j),
                           (*chip, c))
                cp.start()
                sends.append(cp)
        for cp in sends:
            cp.wait_send()
        for a in range(n):
            for j, chip in enumerate(chips):
                slot = 2 * chip[0] + chip[1]
                _rdma(ins[a].at[slot], outs[a].at[slot], send_sems, recv_sems, (a, j), (*chip, c)).wait_recv()
        for cp in local:
            cp.wait()

    out_shape = [jax.ShapeDtypeStruct(g.shape, g.dtype) for g in arrays]
    return list(_comm_call(body, name, arrays, out_shape, (n, 3), n))


def _add_cast(a, b, out_dtype, name, tr=512):
    R, C = a.shape
    tr = _row_tile(R, tr)

    def body(a_ref, b_ref, o_ref):
        o_ref[...] = (a_ref[...] + b_ref[...]).astype(out_dtype)

    row = pl.BlockSpec((tr, C), lambda i: (i, 0))
    return pl.pallas_call(
        body, name=name, grid=(R // tr,), in_specs=[row, row], out_specs=row,
        out_shape=jax.ShapeDtypeStruct((R, C), out_dtype),
        compiler_params=pltpu.CompilerParams(
            dimension_semantics=("parallel",), vmem_limit_bytes=_vmem(3 * _nbytes((tr, C), F32))),
    )(a, b)


def _row_tile(rows, target):
    best = rows
    for t in range(8, min(rows, target) + 1, 8):
        if rows % t == 0:
            best = t
    return best


def _sum_slots(recv, name, tr=128):
    n, R, C = recv.shape
    tr = _row_tile(R, tr if recv.dtype == F32 else 2 * tr)

    def body(r_ref, o_ref):
        acc = r_ref[0].astype(F32)
        for k in range(1, n):
            acc = acc + r_ref[k].astype(F32)
        o_ref[...] = acc

    return pl.pallas_call(
        body, name=name, grid=(R // tr,),
        in_specs=[pl.BlockSpec((n, tr, C), lambda i: (0, i, 0))],
        out_specs=pl.BlockSpec((tr, C), lambda i: (i, 0)),
        out_shape=jax.ShapeDtypeStruct((R, C), F32),
        compiler_params=pltpu.CompilerParams(
            dimension_semantics=("parallel",), vmem_limit_bytes=_vmem(10 * _nbytes((tr, C), F32))),
    )(recv)


def _adamw(g, w, m, v, name, tr=256):
    R, C = g.shape
    tr = _row_tile(R, tr)

    def body(g_ref, w_ref, m_ref, v_ref, d_ref, nm_ref, nv_ref):
        gv = g_ref[...]
        nm = ADAM_B1 * m_ref[...] + (1.0 - ADAM_B1) * gv
        nv = ADAM_B2 * v_ref[...] + (1.0 - ADAM_B2) * (gv * gv)
        m_hat = nm / (1.0 - ADAM_B1 ** ADAM_STEP)
        v_hat = nv / (1.0 - ADAM_B2 ** ADAM_STEP)
        d_ref[...] = -ADAM_LR * (m_hat / (jnp.sqrt(v_hat) + ADAM_EPS) + ADAM_WD * w_ref[...])
        nm_ref[...] = nm
        nv_ref[...] = nv

    row = pl.BlockSpec((tr, C), lambda i: (i, 0))
    return pl.pallas_call(
        body, name=name, grid=(R // tr,), in_specs=[row] * 4, out_specs=[row] * 3,
        out_shape=[jax.ShapeDtypeStruct((R, C), F32)] * 3,
        compiler_params=pltpu.CompilerParams(
            dimension_semantics=("parallel",), vmem_limit_bytes=_vmem(8 * _nbytes((tr, C), F32))),
    )(g, w, m, v)


def _pack_rows(parts):
    flat = jnp.concatenate([p.reshape(-1).astype(F32) for p in parts])
    rows = -(-flat.shape[0] // (8 * LANE)) * 8
    return jnp.pad(flat, (0, rows * LANE - flat.shape[0])).reshape(rows, LANE)


def _unpack(flat, shapes):
    out, off = [], 0
    for shp in shapes:
        n = math.prod(shp)
        out.append(flat[off:off + n].reshape(shp))
        off += n
    return out


def kernel(x, a_norm, a_w_in, a_conv, a_w_out, kv_norm, w_kv, b_norm, b_w_q, b_w_o, rel_bias, ffn_norm, ffn_w_up, ffn_conv, ffn_conv_b, ffn_w_down, final_norm, loss_target, m_a_norm, m_a_w_in, m_a_conv, m_a_w_out, m_kv_norm, m_w_kv, m_b_norm, m_b_w_q, m_b_w_o, m_rel_bias, m_ffn_norm, m_ffn_w_up, m_ffn_conv, m_ffn_conv_b, m_ffn_w_down, m_final_norm, v_a_norm, v_a_w_in, v_a_conv, v_a_w_out, v_kv_norm, v_w_kv, v_b_norm, v_b_w_q, v_b_w_o, v_rel_bias, v_ffn_norm, v_ffn_w_up, v_ffn_conv, v_ffn_conv_b, v_ffn_w_down, v_final_norm):
    B, S, D = x.shape
    T = B * S
    F = ffn_w_down.shape[1] * N_DEV
    me = 4 * lax.axis_index("x") + 2 * lax.axis_index("y") + lax.axis_index("c")

    big_shards = [a_w_in[0].T, a_w_out[0], w_kv.T, b_w_q[0], b_w_o[0],
                  ffn_w_up[0].T, ffn_w_up[1].T, ffn_w_down[0], ffn_w_down[1]]
    small_shapes = [a_norm.shape, a_conv.shape, ffn_conv.shape]
    small_pack = _pack_rows([a_norm, a_conv, ffn_conv])
    gathered = _all_gather([s.astype(BF16) for s in big_shards] + [small_pack], "gather_weights")
    win_t, wout, wkv_t, wq, wo, wup0_t, wup1_t, wdn0, wdn1 = [
        g.reshape(-1, D) for g in gathered[:9]]
    wup_t, wdn = (wup0_t, wup1_t), (wdn0, wdn1)
    smalls = [_unpack(gathered[9][j].reshape(-1), small_shapes) for j in range(N_DEV)]
    a_norm_f = jnp.concatenate([s[0] for s in smalls], axis=-1)
    a_conv_f = jnp.concatenate([s[1] for s in smalls], axis=-1)[0]
    ffn_conv_f = jnp.concatenate([s[2] for s in smalls], axis=-1)

    x2 = x.reshape(T, D)
    (xn0,) = _rmsnorm_fwd(x2, a_norm_f, "a_norm_fwd")
    bch = _matmul(xn0, win_t, mode="nt", out_dtype=BF16, name="a_in_proj").reshape(B, S, 3 * D)
    gated = _shortconv_fwd(bch, a_conv_f, "a_gate_fwd").reshape(T, D)
    h1 = _matmul(gated, wout, mode="nn", out_dtype=F32, add=x2, name="a_out_proj")

    def ffn_fwd(h, l):
        (xn,) = _rmsnorm_fwd(h, ffn_norm[l:l + 1], f"ffn{l}_norm_fwd")
        u0 = _matmul(xn, wup_t[l], mode="nt", out_dtype=BF16, name=f"ffn{l}_up").reshape(B, S, 2 * F)
        act = _ffn_gate_fwd(u0, ffn_conv_f[l], ffn_conv_b[l:l + 1], f"ffn{l}_gate_fwd").reshape(T, F)
        out = _matmul(act, wdn[l], mode="nn", out_dtype=F32, add=h, name=f"ffn{l}_down", tk=1408)
        return out, (xn, u0, act)

    h2, ffn0_saved = ffn_fwd(h1, 0)
    kvn, xnb = _rmsnorm_fwd(h2, jnp.stack([kv_norm, b_norm[0]]), "kv_b_norm_fwd")
    kv = _matmul(kvn, wkv_t, mode="nt", out_dtype=F32, name="kv_proj").reshape(B, S, 2 * D)
    q = _matmul(xnb, wq, mode="nn", out_dtype=F32, name="q_proj").reshape(B, S, D)

    tables = [_band_tables(w, d) for (w, d) in DILATED_BRANCHES]
    onehots = jnp.stack([t[0] for t in tables])
    P = ATT_BLOCK
    bias_vals = _bias_lookup(rel_bias.T, onehots, "rel_bias_lookup").reshape(3, N_HEADS, P, 2 * P)
    in_cur = (jnp.arange(2 * P) >= P)[None, :]
    tabs = []
    for bi in range(3):
        band = tables[bi][1]
        gen = jnp.where(band[None], bias_vals[bi], NEG)
        fst = jnp.where((band & in_cur)[None], bias_vals[bi], NEG)
        tabs.append(jnp.stack([fst, gen]))
    tabs = jnp.stack(tabs)

    attn3, lse3 = _attn_fwd(q, kv, tabs, "attn_fwd")
    attn = attn3.reshape(T, D)
    h3 = _matmul(attn, wo, mode="nn", out_dtype=F32, add=h2, name="o_proj")
    h4, ffn1_saved = ffn_fwd(h3, 1)

    dh4, d_final_norm, sq = _final_loss_bwd(h4, final_norm.reshape(1, D), loss_target.reshape(T, D), "loss_bwd")
    loss_part = 0.5 * jnp.sum(sq) / D

    def ffn_bwd(dh_out, h_in, saved, l):
        xn, u0, act = saved
        dact = _matmul(dh_out, wdn[l], mode="nt", out_dtype=BF16, name=f"ffn{l}_down_dx", tk=1024).reshape(B, S, F)
        d_wdn = _matmul(act, dh_out, mode="tn", out_dtype=F32, name=f"ffn{l}_down_dw", tm=1408, tn=1024, tk=512)
        du0, d_conv, d_conv_b = _ffn_gate_bwd(u0, dact, ffn_conv_f[l], ffn_conv_b[l:l + 1], f"ffn{l}_gate_bwd")
        du0 = du0.reshape(2, T, F)
        dxn = _matmul(du0, wup_t[l], mode="nn", out_dtype=F32, a_parts=2, name=f"ffn{l}_up_dx", tk=1408)
        d_wup_t = _matmul(du0, xn, mode="tn", out_dtype=F32, a_parts=2, name=f"ffn{l}_up_dw", tm=1408, tn=1024, tk=512)
        dh_in, d_norm = _rmsnorm_bwd(h_in, ffn_norm[l:l + 1], [dxn], dh_out, f"ffn{l}_norm_bwd")
        return dh_in, (d_wdn, d_wup_t, d_conv, d_conv_b, d_norm)

    dh3, ffn1_grads = ffn_bwd(dh4, h3, ffn1_saved, 1)

    dattn = _matmul(dh3, wo, mode="nt", out_dtype=F32, name="o_proj_dx").reshape(B, S, D)
    d_wo = _matmul(attn, dh3, mode="tn", out_dtype=F32, name="o_proj_dw", tm=512, tn=1024, tk=512)
    dq, dkv, dtab = _attn_bwd(q, kv, dattn, attn3, lse3, tabs, "attn_bwd")
    dtabs = dtab.transpose(1, 0, 2, 3, 4).reshape(3, N_HEADS, P * 2 * P)
    d_rel_bias = _bias_grad(dtabs, onehots, "rel_bias_grad").T
    dq2 = dq.reshape(T, D)
    dkv2 = dkv.reshape(2, T, D)
    dxnb = _matmul(dq2, wq, mode="nt", out_dtype=F32, name="q_proj_dx")
    d_wq = _matmul(xnb, dq2, mode="tn", out_dtype=F32, name="q_proj_dw", tm=512, tn=1024, tk=512)
    dkvn = _matmul(dkv2, wkv_t, mode="nn", out_dtype=F32, a_parts=2, name="kv_proj_dx")
    d_wkv_t = _matmul(dkv2, kvn, mode="tn", out_dtype=F32, a_parts=2, name="kv_proj_dw", tm=512, tn=1024, tk=512)
    dh2, d_kvb_norm = _rmsnorm_bwd(h2, jnp.stack([kv_norm, b_norm[0]]), [dkvn, dxnb], dh3, "kv_b_norm_bwd")

    dh1, ffn0_grads = ffn_bwd(dh2, h1, ffn0_saved, 0)

    dgated = _matmul(dh1, wout, mode="nt", out_dtype=BF16, name="a_out_proj_dx").reshape(B, S, D)
    d_wout = _matmul(gated, dh1, mode="tn", out_dtype=F32, name="a_out_proj_dw", tm=512, tn=1024, tk=512)
    dbch, d_a_conv = _shortconv_bwd(bch, dgated, a_conv_f, "a_gate_bwd")
    dbch = dbch.reshape(3, T, D)
    dxn0 = _matmul(dbch, win_t, mode="nn", out_dtype=F32, a_parts=3, name="a_in_proj_dx")
    d_win_t = _matmul(dbch, xn0, mode="tn", out_dtype=F32, a_parts=3, name="a_in_proj_dw", tm=512, tn=1024, tk=512)
    grad_x, d_a_norm = _rmsnorm_bwd(x2, a_norm_f, [dxn0], dh1, "a_norm_bwd")

    big_grads = [d_win_t, d_wout, d_wkv_t, d_wq, d_wo, ffn0_grads[1], ffn1_grads[1], ffn0_grads[0], ffn1_grads[0]]
    small_full = [d_a_norm, d_a_conv, jnp.stack([ffn0_grads[2], ffn1_grads[2]]),
                  d_kvb_norm[0], d_kvb_norm[1], d_rel_bias, jnp.concatenate([ffn0_grads[4], ffn1_grads[4]]),
                  jnp.concatenate([ffn0_grads[3], ffn1_grads[3]]), d_final_norm, loss_part]
    small_full_shapes = [(1, D), (3, D), (2, 3, 2 * F), (D,), (1, D), rel_bias.shape, (2, D), (2, 2 * F), (D,), ()]
    halves = [g.reshape(N_DEV // 2, 2, g.shape[0] // N_DEV, D) for g in big_grads]
    mine, theirs, gath = _core_pair_exchange(halves, [_pack_rows(small_full)], "exchange_grads_core")
    chip_sums = [_add_cast(m_.reshape(-1, D), t_.reshape(-1, D), BF16, f"grad_core_sum_{i}").reshape(m_.shape)
                 for i, (m_, t_) in enumerate(zip(mine, theirs))]
    recv = _chip_exchange(chip_sums, "exchange_grads_chip")
    small_sum = _sum_slots(gath[0], "small_grad_sum").reshape(-1)
    (g_a_norm_f, g_a_conv_f, g_ffn_conv_f, g_kv_norm, g_b_norm, g_rel_bias, g_ffn_norm, g_ffn_conv_b,
     g_final_norm, loss) = _unpack(small_sum, small_full_shapes)

    def my_cols(full, width):
        return lax.dynamic_slice_in_dim(full, me * width, width, axis=full.ndim - 1)

    g_a_norm = my_cols(g_a_norm_f, D // N_DEV)
    g_a_conv = my_cols(g_a_conv_f, D // N_DEV)[None]
    g_ffn_conv = my_cols(g_ffn_conv_f, 2 * F // N_DEV)

    big_w = [(a_w_in, m_a_w_in, v_a_w_in, True), (a_w_out, m_a_w_out, v_a_w_out, False),
             (w_kv, m_w_kv, v_w_kv, True), (b_w_q, m_b_w_q, v_b_w_q, False), (b_w_o, m_b_w_o, v_b_w_o, False),
             (ffn_w_up[0], m_ffn_w_up[0], v_ffn_w_up[0], True), (ffn_w_up[1], m_ffn_w_up[1], v_ffn_w_up[1], True),
             (ffn_w_down[0], m_ffn_w_down[0], v_ffn_w_down[0], False),
             (ffn_w_down[1], m_ffn_w_down[1], v_ffn_w_down[1], False)]
    big_out = []
    for i, (w, m, v, transposed) in enumerate(big_w):
        g = _sum_slots(recv[i], f"grad_sum_{i}")
        if transposed:
            g = g.T
        w2, m2, v2 = (t.reshape(g.shape) for t in (w, m, v))
        delta, nm, nv = _adamw(g, w2, m2, v2, f"adamw_{i}")
        big_out.append(tuple(t.reshape(w.shape) for t in (g, delta, nm, nv)))

    def pair(i, j):
        return tuple(jnp.stack([big_out[i][t], big_out[j][t]]) for t in range(4))

    small_w = [(a_norm, m_a_norm, v_a_norm, g_a_norm), (a_conv, m_a_conv, v_a_conv, g_a_conv),
               (ffn_conv, m_ffn_conv, v_ffn_conv, g_ffn_conv), (kv_norm, m_kv_norm, v_kv_norm, g_kv_norm),
               (b_norm, m_b_norm, v_b_norm, g_b_norm), (rel_bias, m_rel_bias, v_rel_bias, g_rel_bias),
               (ffn_norm, m_ffn_norm, v_ffn_norm, g_ffn_norm),
               (ffn_conv_b, m_ffn_conv_b, v_ffn_conv_b, g_ffn_conv_b),
               (final_norm, m_final_norm, v_final_norm, g_final_norm)]
    shapes = [t[0].shape for t in small_w]
    packed = [_pack_rows([t[i].reshape(t[0].shape) for t in small_w]) for i in (3, 0, 1, 2)]
    s_delta, s_nm, s_nv = _adamw(*packed, "adamw_small")
    s_g = [t[3].reshape(t[0].shape) for t in small_w]
    s_d, s_m, s_v = (_unpack(t.reshape(-1), shapes) for t in (s_delta, s_nm, s_nv))
    small_out = [(s_g[i], s_d[i], s_m[i], s_v[i]) for i in range(len(small_w))]

    per_weight = [small_out[0], big_out[0], small_out[1], big_out[1], small_out[3], big_out[2], small_out[4],
                  big_out[3], big_out[4], small_out[5], small_out[6], pair(5, 6), small_out[2], small_out[7],
                  pair(7, 8), small_out[8]]
    outs = [loss, grad_x.reshape(B, S, D)]
    for t in range(4):
        outs.extend(pw[t] for pw in per_weight)
    return tuple(outs)
```

```python
import functools
import math

import jax
import jax.numpy as jnp
from jax import lax
from jax.experimental import pallas as pl
from jax.experimental.pallas import tpu as pltpu

F32 = jnp.float32
BF16 = jnp.bfloat16

N_DEV = 8
N_HEADS = 16
ATT_BLOCK = 128
ATT_UNROLL = 4
DILATED_BRANCHES = ((128, 1), (512, 4), (2048, 16))
REL_BUCKETS = 32
REL_MAX_DISTANCE = 2048
RMS_EPS = 1e-6
ADAM_LR = 0.001
ADAM_B1 = 0.9
ADAM_B2 = 0.999
ADAM_EPS = 1e-08
ADAM_WD = 0.01
ADAM_STEP = 10

LANE = 128
HALO = 16
NEG = -1e30
VMEM_CAP = 56 << 20


def _vmem(block_bytes):
    return int(min(VMEM_CAP, max(32 << 20, 3 * block_bytes + (8 << 20))))


def _nbytes(shape, dtype):
    return math.prod(shape) * jnp.dtype(dtype).itemsize


def _tile(dim, target):
    best = None
    t = LANE
    while t <= min(dim, target):
        if dim % t == 0:
            best = t
        t += LANE
    return best if best is not None else dim


def _matmul(a, b, *, mode, out_dtype, name, add=None, a_parts=1, tm=1024, tn=512, tk=1024):
    P = a_parts
    if mode == "nn":
        M, K = (a.shape[0], a.shape[1]) if P == 1 else (a.shape[1], a.shape[2] * P)
        N = b.shape[1]
    elif mode == "nt":
        assert P == 1
        M, K = a.shape
        N = b.shape[0]
    else:
        K = a.shape[0] if P == 1 else a.shape[1]
        M = a.shape[1] if P == 1 else a.shape[2] * P
        N = b.shape[1]
    tm = _tile(M // P if mode == "tn" else M, tm)
    tn = _tile(N, tn)
    tk = _tile(K // P if mode == "nn" else K, tk)
    nm, nn_, nk = M // tm, N // tn, K // tk

    if mode == "nn":
        if P == 1:
            a_spec = pl.BlockSpec((tm, tk), lambda i, j, k: (i, k))
        else:
            nkp = nk // P
            a_spec = pl.BlockSpec((None, tm, tk), lambda i, j, k: (k // nkp, i, k % nkp))
        b_spec = pl.BlockSpec((tk, tn), lambda i, j, k: (k, j))
        dims = (((1,), (0,)), ((), ()))
    elif mode == "nt":
        a_spec = pl.BlockSpec((tm, tk), lambda i, j, k: (i, k))
        b_spec = pl.BlockSpec((tn, tk), lambda i, j, k: (j, k))
        dims = (((1,), (1,)), ((), ()))
    else:
        if P == 1:
            a_spec = pl.BlockSpec((tk, tm), lambda i, j, k: (k, i))
        else:
            nmp = nm // P
            a_spec = pl.BlockSpec((None, tk, tm), lambda i, j, k: (i // nmp, k, i % nmp))
        b_spec = pl.BlockSpec((tk, tn), lambda i, j, k: (k, j))
        dims = (((0,), (0,)), ((), ()))
    o_spec = pl.BlockSpec((tm, tn), lambda i, j, k: (i, j))
    in_specs = [a_spec, b_spec]
    args = [a, b]
    if add is not None:
        in_specs.append(o_spec)
        args.append(add)
    has_add = add is not None

    direct = nk > 1 and out_dtype == F32 and not has_add

    def body(a_ref, b_ref, *rest):
        if has_add:
            add_ref, o_ref = rest[:2]
        else:
            o_ref = rest[0]
        acc_ref = o_ref if direct or nk == 1 else rest[-1]
        k = pl.program_id(2)

        def part():
            return lax.dot_general(a_ref[...].astype(BF16), b_ref[...].astype(BF16), dims,
                                   preferred_element_type=F32)

        def finish(r):
            if has_add:
                r = r + add_ref[...].astype(F32)
            o_ref[...] = r.astype(out_dtype)

        if nk == 1:
            finish(part())
        else:
            @pl.when(k == 0)
            def _():
                acc_ref[...] = part()

            @pl.when(jnp.logical_and(k > 0, jnp.logical_or(k < nk - 1, direct)))
            def _():
                acc_ref[...] += part()

            if not direct:
                @pl.when(k == nk - 1)
                def _():
                    finish(acc_ref[...] + part())

    blk = (_nbytes((tm, tk), a.dtype) + _nbytes((tk, tn), b.dtype) + _nbytes((tm, tn), out_dtype)
           + (_nbytes((tm, tn), add.dtype) if has_add else 0)) * 2 + 3 * _nbytes((tm, tn), F32)
    return pl.pallas_call(
        body, name=name, grid=(nm, nn_, nk),
        in_specs=in_specs, out_specs=o_spec,
        out_shape=jax.ShapeDtypeStruct((M, N), out_dtype),
        scratch_shapes=[] if direct or nk == 1 else [pltpu.VMEM((tm, tn), F32)],
        compiler_params=pltpu.CompilerParams(
            dimension_semantics=("parallel", "parallel", "arbitrary"), vmem_limit_bytes=_vmem(blk)),
    )(*args)


def _rmsnorm_fwd(x, gains, name, tm=512):
    T, D = x.shape
    n = gains.shape[0]
    tm = _tile(T, tm)

    def body(x_ref, g_ref, *o_refs):
        xv = x_ref[...]
        xhat = xv * lax.rsqrt(jnp.mean(xv * xv, axis=-1, keepdims=True) + RMS_EPS)
        for i in range(n):
            o_refs[i][...] = (xhat * g_ref[i:i + 1, :]).astype(BF16)

    row = pl.BlockSpec((tm, D), lambda i: (i, 0))
    outs = pl.pallas_call(
        body, name=name, grid=(T // tm,),
        in_specs=[row, pl.BlockSpec((n, D), lambda i: (0, 0))],
        out_specs=[row] * n,
        out_shape=[jax.ShapeDtypeStruct((T, D), BF16)] * n,
        compiler_params=pltpu.CompilerParams(
            dimension_semantics=("parallel",), vmem_limit_bytes=_vmem(4 * _nbytes((tm, D), F32))),
    )(x, gains)
    return tuple(outs)


def _rmsnorm_bwd(x, gains, dxns, dres, name, tm=512):
    T, D = x.shape
    n = gains.shape[0]
    tm = _tile(T, tm)

    def body(x_ref, g_ref, *rest):
        dxn_refs = rest[:n]
        dres_ref, dx_ref, dxb_ref, dg_ref = rest[n:]
        xv = x_ref[...]
        rstd = lax.rsqrt(jnp.mean(xv * xv, axis=-1, keepdims=True) + RMS_EPS)
        xhat = xv * rstd
        dx = dres_ref[...]

        @pl.when(pl.program_id(0) == 0)
        def _():
            dg_ref[...] = jnp.zeros_like(dg_ref)

        for i in range(n):
            dy = dxn_refs[i][...].astype(F32)
            dg_ref[i:i + 1, :] += jnp.sum(dy * xhat, axis=0, keepdims=True)
            dxh = dy * g_ref[i:i + 1, :]
            dx = dx + rstd * (dxh - xhat * jnp.mean(dxh * xhat, axis=-1, keepdims=True))
        dx_ref[...] = dx
        dxb_ref[...] = dx.astype(BF16)

    row = pl.BlockSpec((tm, D), lambda i: (i, 0))
    par = pl.BlockSpec((n, D), lambda i: (0, 0))
    return pl.pallas_call(
        body, name=name, grid=(T // tm,),
        in_specs=[row, par] + [row] * n + [row],
        out_specs=[row, row, par],
        out_shape=[jax.ShapeDtypeStruct((T, D), F32), jax.ShapeDtypeStruct((T, D), BF16),
                   jax.ShapeDtypeStruct((n, D), F32)],
        compiler_params=pltpu.CompilerParams(
            dimension_semantics=("arbitrary",), vmem_limit_bytes=_vmem((4 + n) * _nbytes((tm, D), F32))),
    )(x, gains, *dxns, dres)


def _final_loss_bwd(h, gain, target, name, tm=512):
    T, D = h.shape
    tm = _tile(T, tm)

    def body(h_ref, g_ref, t_ref, dh_ref, dhb_ref, dg_ref, sq_ref):
        xv = h_ref[...]
        rstd = lax.rsqrt(jnp.mean(xv * xv, axis=-1, keepdims=True) + RMS_EPS)
        xhat = xv * rstd
        err = xhat * g_ref[...] - t_ref[...]

        @pl.when(pl.program_id(0) == 0)
        def _():
            dg_ref[...] = jnp.zeros_like(dg_ref)
            sq_ref[...] = jnp.zeros_like(sq_ref)

        sq_ref[...] += jnp.sum(err * err, axis=0, keepdims=True)
        dy = err * (1.0 / D)
        dg_ref[...] += jnp.sum(dy * xhat, axis=0, keepdims=True)
        dxh = dy * g_ref[...]
        dh = rstd * (dxh - xhat * jnp.mean(dxh * xhat, axis=-1, keepdims=True))
        dh_ref[...] = dh
        dhb_ref[...] = dh.astype(BF16)

    row = pl.BlockSpec((tm, D), lambda i: (i, 0))
    par = pl.BlockSpec((1, D), lambda i: (0, 0))
    return pl.pallas_call(
        body, name=name, grid=(T // tm,),
        in_specs=[row, par, row], out_specs=[row, row, par, par],
        out_shape=[jax.ShapeDtypeStruct((T, D), F32), jax.ShapeDtypeStruct((T, D), BF16),
                   jax.ShapeDtypeStruct((1, D), F32), jax.ShapeDtypeStruct((1, D), F32)],
        compiler_params=pltpu.CompilerParams(
            dimension_semantics=("arbitrary",), vmem_limit_bytes=_vmem(5 * _nbytes((tm, D), F32))),
    )(h, gain, target)


def _halo_specs(S, ts, tc, col_off, prev, nxt):
    r = ts // HALO
    last = S // HALO - 1
    specs = []
    if prev:
        specs.append(pl.BlockSpec((None, HALO, tc), lambda j, b, s: (b, jnp.maximum(s * r - 1, 0), col_off + j)))
    specs.append(pl.BlockSpec((None, ts, tc), lambda j, b, s: (b, s, col_off + j)))
    if nxt:
        specs.append(pl.BlockSpec((None, HALO, tc), lambda j, b, s: (b, jnp.minimum((s + 1) * r, last), col_off + j)))
    return specs


def _ext(prev_ref, main_ref, next_ref, first, last):
    main = main_ref[...].astype(F32)
    zeros = jnp.zeros((HALO, main.shape[1]), F32)
    top = zeros if prev_ref is None else jnp.where(first, 0.0, prev_ref[...].astype(F32))
    bot = zeros if next_ref is None else jnp.where(last, 0.0, next_ref[...].astype(F32))
    return jnp.concatenate([top, main, bot], axis=0)


def _shift(xe, k):
    return pltpu.roll(xe, k % xe.shape[0], axis=0)


def _cconv(xe, w):
    return w[0:1, :] * _shift(xe, 2) + w[1:2, :] * _shift(xe, 1) + w[2:3, :] * xe


def _cconv_t(de, w):
    return w[2:3, :] * de + w[1:2, :] * _shift(de, -1) + w[0:1, :] * _shift(de, -2)


def _main(xe, ts):
    return xe[HALO:HALO + ts, :]


def _conv_wgrad(dw_ref, de, xe, ts):
    d = _main(de, ts)
    dw_ref[0:1, :] += jnp.sum(d * _main(_shift(xe, 2), ts), axis=0, keepdims=True)
    dw_ref[1:2, :] += jnp.sum(d * _main(_shift(xe, 1), ts), axis=0, keepdims=True)
    dw_ref[2:3, :] += jnp.sum(d * _main(xe, ts), axis=0, keepdims=True)


def _gate_tiles(S, C):
    return _tile(S, 512), _tile(C, 512)


def _shortconv_fwd(bch, conv_w, name):
    B, S, D3 = bch.shape
    D = D3 // 3
    ts, tc = _gate_tiles(S, D)
    nj, ns = D // tc, S // ts

    def body(b_ref, cp_ref, c_ref, hp_ref, h_ref, w_ref, o_ref):
        first = pl.program_id(2) == 0
        ce = _ext(cp_ref, c_ref, None, first, False)
        he = _ext(hp_ref, h_ref, None, first, False)
        cv = _main(_cconv(ce * he, w_ref[...]), ts)
        o_ref[...] = (b_ref[...].astype(F32) * cv).astype(BF16)

    in_specs = (_halo_specs(S, ts, tc, 0, False, False) + _halo_specs(S, ts, tc, nj, True, False)
                + _halo_specs(S, ts, tc, 2 * nj, True, False) + [pl.BlockSpec((3, tc), lambda j, b, s: (0, j))])
    return pl.pallas_call(
        body, name=name, grid=(nj, B, ns), in_specs=in_specs,
        out_specs=pl.BlockSpec((None, ts, tc), lambda j, b, s: (b, s, j)),
        out_shape=jax.ShapeDtypeStruct((B, S, D), BF16),
        compiler_params=pltpu.CompilerParams(
            dimension_semantics=("parallel", "parallel", "parallel"),
            vmem_limit_bytes=_vmem(12 * _nbytes((ts + 2 * HALO, tc), F32))),
    )(bch, bch, bch, bch, bch, conv_w)


def _shortconv_bwd(bch, dg, conv_w, name):
    B, S, D3 = bch.shape
    D = D3 // 3
    ts, tc = _gate_tiles(S, D)
    nj, ns = D // tc, S // ts

    def body(b_ref, bn_ref, cp_ref, c_ref, cn_ref, hp_ref, h_ref, hn_ref, d_ref, dn_ref, w_ref, o_ref, dw_ref):
        s = pl.program_id(2)
        first, last = s == 0, s == ns - 1

        @pl.when(jnp.logical_and(pl.program_id(1) == 0, s == 0))
        def _():
            dw_ref[...] = jnp.zeros_like(dw_ref)

        w = w_ref[...]
        be = _ext(None, b_ref, bn_ref, first, last)
        ce = _ext(cp_ref, c_ref, cn_ref, first, last)
        he = _ext(hp_ref, h_ref, hn_ref, first, last)
        de = _ext(None, d_ref, dn_ref, first, last)
        ch = ce * he
        dcv = de * be
        dch = _main(_cconv_t(dcv, w), ts)
        o_ref[0] = (_main(de, ts) * _main(_cconv(ch, w), ts)).astype(BF16)
        o_ref[1] = (dch * _main(he, ts)).astype(BF16)
        o_ref[2] = (dch * _main(ce, ts)).astype(BF16)
        _conv_wgrad(dw_ref, dcv, ch, ts)

    in_specs = (_halo_specs(S, ts, tc, 0, False, True) + _halo_specs(S, ts, tc, nj, True, True)
                + _halo_specs(S, ts, tc, 2 * nj, True, True) + _halo_specs(S, ts, tc, 0, False, True)
                + [pl.BlockSpec((3, tc), lambda j, b, s: (0, j))])
    return pl.pallas_call(
        body, name=name, grid=(nj, B, ns), in_specs=in_specs,
        out_specs=[pl.BlockSpec((3, None, ts, tc), lambda j, b, s: (0, b, s, j)),
                   pl.BlockSpec((3, tc), lambda j, b, s: (0, j))],
        out_shape=[jax.ShapeDtypeStruct((3, B, S, D), BF16), jax.ShapeDtypeStruct((3, D), F32)],
        compiler_params=pltpu.CompilerParams(
            dimension_semantics=("parallel", "arbitrary", "arbitrary"),
            vmem_limit_bytes=_vmem(24 * _nbytes((ts + 2 * HALO, tc), F32))),
    )(bch, bch, bch, bch, bch, bch, bch, bch, dg, dg, conv_w)


def _sigmoid(x):
    return 1.0 / (1.0 + jnp.exp(-x))


def _ffn_gate_fwd(u0, conv_w, conv_b, name):
    B, S, F2 = u0.shape
    F = F2 // 2
    ts, tc = _gate_tiles(S, F)
    nj, ns = F // tc, S // ts

    def body(gp_ref, g_ref, up_ref, u_ref, wg_ref, wu_ref, bg_ref, bu_ref, o_ref):
        first = pl.program_id(2) == 0
        ug = _main(_cconv(_ext(gp_ref, g_ref, None, first, False), wg_ref[...]), ts) + bg_ref[...]
        uu = _main(_cconv(_ext(up_ref, u_ref, None, first, False), wu_ref[...]), ts) + bu_ref[...]
        o_ref[...] = (ug * _sigmoid(ug) * uu).astype(BF16)

    in_specs = (_halo_specs(S, ts, tc, 0, True, False) + _halo_specs(S, ts, tc, nj, True, False)
                + [pl.BlockSpec((3, tc), lambda j, b, s: (0, j)), pl.BlockSpec((3, tc), lambda j, b, s: (0, nj + j)),
                   pl.BlockSpec((1, tc), lambda j, b, s: (0, j)), pl.BlockSpec((1, tc), lambda j, b, s: (0, nj + j))])
    return pl.pallas_call(
        body, name=name, grid=(nj, B, ns), in_specs=in_specs,
        out_specs=pl.BlockSpec((None, ts, tc), lambda j, b, s: (b, s, j)),
        out_shape=jax.ShapeDtypeStruct((B, S, F), BF16),
        compiler_params=pltpu.CompilerParams(
            dimension_semantics=("parallel", "parallel", "parallel"),
            vmem_limit_bytes=_vmem(12 * _nbytes((ts + 2 * HALO, tc), F32))),
    )(u0, u0, u0, u0, conv_w, conv_w, conv_b, conv_b)


def _ffn_gate_bwd(u0, dact, conv_w, conv_b, name):
    B, S, F2 = u0.shape
    F = F2 // 2
    ts, tc = _gate_tiles(S, F)
    nj, ns = F // tc, S // ts

    def body(gp_ref, g_ref, gn_ref, up_ref, u_ref, un_ref, d_ref, dn_ref, wg_ref, wu_ref, bg_ref, bu_ref,
             o_ref, dwg_ref, dwu_ref, dbg_ref, dbu_ref):
        s = pl.program_id(2)
        first, last = s == 0, s == ns - 1

        @pl.when(jnp.logical_and(pl.program_id(1) == 0, s == 0))
        def _():
            for r in (dwg_ref, dwu_ref, dbg_ref, dbu_ref):
                r[...] = jnp.zeros_like(r)

        wg, wu = wg_ref[...], wu_ref[...]
        ge = _ext(gp_ref, g_ref, gn_ref, first, last)
        ue = _ext(up_ref, u_ref, un_ref, first, last)
        de = _ext(None, d_ref, dn_ref, first, last)
        ug = _cconv(ge, wg) + bg_ref[...]
        uu = _cconv(ue, wu) + bu_ref[...]
        sig = _sigmoid(ug)
        dug = de * uu * (sig * (1.0 + ug * (1.0 - sig)))
        duu = de * (ug * sig)
        o_ref[0] = _main(_cconv_t(dug, wg), ts).astype(BF16)
        o_ref[1] = _main(_cconv_t(duu, wu), ts).astype(BF16)
        _conv_wgrad(dwg_ref, dug, ge, ts)
        _conv_wgrad(dwu_ref, duu, ue, ts)
        dbg_ref[...] += jnp.sum(_main(dug, ts), axis=0, keepdims=True)
        dbu_ref[...] += jnp.sum(_main(duu, ts), axis=0, keepdims=True)

    w3 = lambda off: pl.BlockSpec((3, tc), lambda j, b, s: (0, off + j))
    w1 = lambda off: pl.BlockSpec((1, tc), lambda j, b, s: (0, off + j))
    in_specs = (_halo_specs(S, ts, tc, 0, True, True) + _halo_specs(S, ts, tc, nj, True, True)
                + _halo_specs(S, ts, tc, 0, False, True) + [w3(0), w3(nj), w1(0), w1(nj)])
    outs = pl.pallas_call(
        body, name=name, grid=(nj, B, ns), in_specs=in_specs,
        out_specs=[pl.BlockSpec((2, None, ts, tc), lambda j, b, s: (0, b, s, j)), w3(0), w3(0), w1(0), w1(0)],
        out_shape=[jax.ShapeDtypeStruct((2, B, S, F), BF16), jax.ShapeDtypeStruct((3, F), F32),
                   jax.ShapeDtypeStruct((3, F), F32), jax.ShapeDtypeStruct((1, F), F32),
                   jax.ShapeDtypeStruct((1, F), F32)],
        compiler_params=pltpu.CompilerParams(
            dimension_semantics=("parallel", "arbitrary", "arbitrary"),
            vmem_limit_bytes=_vmem(30 * _nbytes((ts + 2 * HALO, tc), F32))),
    )(u0, u0, u0, u0, u0, u0, dact, dact, conv_w, conv_w, conv_b, conv_b)
    du0, dwg, dwu, dbg, dbu = outs
    return du0, jnp.concatenate([dwg, dwu], axis=1), jnp.concatenate([dbg, dbu], axis=1)


def _t5_bucket(dist):
    max_exact = REL_BUCKETS // 2
    n = jnp.maximum(dist, 0)
    nf = jnp.maximum(n, max_exact).astype(F32)
    large = max_exact + (jnp.log(nf / max_exact) / math.log(REL_MAX_DISTANCE / max_exact)
                         * (REL_BUCKETS - max_exact)).astype(jnp.int32)
    large = jnp.minimum(large, REL_BUCKETS - 1)
    return jnp.where(n < max_exact, n, large)


def _band_tables(window, dilation):
    P = ATT_BLOCK
    qi = jnp.arange(P, dtype=jnp.int32)[:, None]
    kc = jnp.arange(2 * P, dtype=jnp.int32)[None, :]
    delta = qi + P - kc
    band = (delta >= 0) & (delta <= window // dilation)
    bucket = _t5_bucket(delta * dilation).reshape(-1)
    onehot = (bucket[None, :] == jnp.arange(REL_BUCKETS, dtype=jnp.int32)[:, None]).astype(F32)
    return onehot, band


def _bias_lookup(rel_bias_t, onehots, name):
    nb, _, Q = onehots.shape
    H = rel_bias_t.shape[0]

    def body(r_ref, oh_ref, o_ref):
        o_ref[...] = lax.dot_general(r_ref[...], oh_ref[...], (((1,), (0,)), ((), ())),
                                     precision=lax.Precision.HIGHEST, preferred_element_type=F32)

    return pl.pallas_call(
        body, name=name, grid=(nb,),
        in_specs=[pl.BlockSpec((H, REL_BUCKETS), lambda i: (0, 0)),
                  pl.BlockSpec((None, REL_BUCKETS, Q), lambda i: (i, 0, 0))],
        out_specs=pl.BlockSpec((None, H, Q), lambda i: (i, 0, 0)),
        out_shape=jax.ShapeDtypeStruct((nb, H, Q), F32),
        compiler_params=pltpu.CompilerParams(dimension_semantics=("parallel",),
                                             vmem_limit_bytes=_vmem(_nbytes((REL_BUCKETS + H, Q), F32))),
    )(rel_bias_t, onehots)


def _bias_grad(dtabs, onehots, name):
    nb, H, Q = dtabs.shape

    def body(d_ref, oh_ref, o_ref):
        @pl.when(pl.program_id(0) == 0)
        def _():
            o_ref[...] = jnp.zeros_like(o_ref)

        o_ref[...] += lax.dot_general(d_ref[...], oh_ref[...], (((1,), (1,)), ((), ())),
                                      precision=lax.Precision.HIGHEST, preferred_element_type=F32)

    return pl.pallas_call(
        body, name=name, grid=(nb,),
        in_specs=[pl.BlockSpec((None, H, Q), lambda i: (i, 0, 0)),
                  pl.BlockSpec((None, REL_BUCKETS, Q), lambda i: (i, 0, 0))],
        out_specs=pl.BlockSpec((H, REL_BUCKETS), lambda i: (0, 0)),
        out_shape=jax.ShapeDtypeStruct((H, REL_BUCKETS), F32),
        compiler_params=pltpu.CompilerParams(dimension_semantics=("arbitrary",),
                                             vmem_limit_bytes=_vmem(_nbytes((REL_BUCKETS + H, Q), F32))),
    )(dtabs, onehots)


def _lane_masks():
    lane = lax.broadcasted_iota(jnp.int32, (1, LANE), 1)
    lo = lane < LANE // 2
    return lo, jnp.logical_not(lo)


def _dot_nt(a, b):
    return lax.dot_general(a, b, (((1,), (1,)), ((), ())), preferred_element_type=F32)


def _dot_nn(a, b):
    return lax.dot_general(a, b, (((1,), (0,)), ((), ())), preferred_element_type=F32)


def _dot_tn(a, b):
    return lax.dot_general(a, b, (((0,), (0,)), ((), ())), preferred_element_type=F32)


def _block_rows(n, dilation, S):
    P = ATT_BLOCK
    nb = S // (dilation * P)
    r, i = n // nb, n % nb
    cur = pl.ds(i * (P * dilation) + r, P, stride=dilation)
    prv = pl.ds(jnp.maximum(i - 1, 0) * (P * dilation) + r, P, stride=dilation)
    return cur, prv, jnp.minimum(i, 1)


def _attn_fwd(q, kv, tabs, name):
    B, S, D = q.shape
    P, H = ATT_BLOCK, N_HEADS
    scale = (D // H) ** -0.5
    half = LANE // 2
    nsteps = S // P // ATT_UNROLL

    def body(q_ref, k_ref, v_ref, tab_ref, o_ref, lse_ref, acc_ref, m_ref, l_ref):
        lo, hi = _lane_masks()
        for bi, (_, d) in enumerate(DILATED_BRANCHES):
            def block(n, carry, bi=bi, d=d):
                rows = [_block_rows(n + j * nsteps, d, S) for j in range(ATT_UNROLL)]
                loaded = []
                for cur, prv, variant in rows:
                    qb = q_ref[cur, :].astype(BF16)
                    kw = jnp.concatenate([k_ref[prv, :], k_ref[cur, :]], axis=0).astype(BF16)
                    vw = jnp.concatenate([v_ref[prv, :], v_ref[cur, :]], axis=0).astype(BF16)
                    old = (m_ref[cur, :], l_ref[cur, :], acc_ref[cur, :]) if bi > 0 else None
                    loaded.append((qb, kw, vw, old))
                results = []
                for (cur, prv, variant), (qb, kw, vw, old) in zip(rows, loaded):
                    m_pair = l_pair = a_pair = None
                    for e, msk in enumerate((lo, hi)):
                        s = _dot_nt(jnp.where(msk, qb, 0), kw) * scale + tab_ref[bi, variant, e]
                        m_e = jnp.max(s, axis=-1, keepdims=True)
                        if bi > 0:
                            m_o = old[0][:, e * half:e * half + 1]
                            m_e = jnp.maximum(m_o, m_e)
                            alpha = jnp.exp(m_o - m_e)
                        pe = jnp.exp(s - m_e)
                        l_e = jnp.sum(pe, axis=-1, keepdims=True)
                        a_e = _dot_nn(pe.astype(BF16), jnp.where(msk, vw, 0))
                        if bi > 0:
                            l_e = l_e + alpha * old[1][:, e * half:e * half + 1]
                            a_e = a_e + alpha * jnp.where(msk, old[2], 0.0)
                        m_pair = m_e if e == 0 else jnp.where(lo, m_pair, m_e)
                        l_pair = l_e if e == 0 else jnp.where(lo, l_pair, l_e)
                        a_pair = a_e if e == 0 else a_pair + a_e
                    results.append((m_pair, l_pair, a_pair))
                for (cur, prv, variant), (m_pair, l_pair, a_pair) in zip(rows, results):
                    m_ref[cur, :] = m_pair
                    l_ref[cur, :] = l_pair
                    acc_ref[cur, :] = a_pair
                return carry

            lax.fori_loop(0, nsteps, block, 0)
        l = l_ref[...]
        o_ref[...] = (acc_ref[...] / l).astype(BF16)
        lse_ref[...] = m_ref[...] + jnp.log(l)

    nl = D // LANE
    col = lambda off: pl.BlockSpec((None, S, LANE), lambda b, p: (b, 0, off + p))
    tab_spec = pl.BlockSpec((3, 2, 2, P, 2 * P), lambda b, p: (0, 0, p, 0, 0))
    return pl.pallas_call(
        body, name=name, grid=(B, H // 2), in_specs=[col(0), col(0), col(nl), tab_spec],
        out_specs=[col(0), col(0)],
        out_shape=[jax.ShapeDtypeStruct((B, S, D), BF16), jax.ShapeDtypeStruct((B, S, D), F32)],
        scratch_shapes=[pltpu.VMEM((S, LANE), F32)] * 3,
        compiler_params=pltpu.CompilerParams(
            dimension_semantics=("parallel", "parallel"),
            vmem_limit_bytes=_vmem(5 * _nbytes((S, LANE), F32) + _nbytes((12, P, 2 * P), F32))),
    )(q, kv, kv, tabs)


def _attn_bwd(q, kv, do, o, lse, tabs, name):
    B, S, D = q.shape
    P, H = ATT_BLOCK, N_HEADS
    scale = (D // H) ** -0.5
    half = LANE // 2
    nsteps = S // P // ATT_UNROLL

    def body(q_ref, k_ref, v_ref, do_ref, o_ref, lse_ref, tab_ref, dq_ref, dkv_ref, dtab_ref, delta_ref):
        lo, hi = _lane_masks()

        @pl.when(pl.program_id(1) == 0)
        def _():
            dtab_ref[...] = jnp.zeros_like(dtab_ref)

        dkv_ref[...] = jnp.zeros_like(dkv_ref)
        prod = do_ref[...] * o_ref[...].astype(F32)
        delta_ref[...] = jnp.where(lo, jnp.sum(jnp.where(lo, prod, 0.0), axis=-1, keepdims=True),
                                   jnp.sum(jnp.where(hi, prod, 0.0), axis=-1, keepdims=True))

        for bi, (_, d) in enumerate(DILATED_BRANCHES):
            def block(n, carry, bi=bi, d=d):
                rows = [_block_rows(n + j * nsteps, d, S) for j in range(ATT_UNROLL)]
                loaded = []
                for cur, prv, variant in rows:
                    qb = q_ref[cur, :].astype(BF16)
                    kw = jnp.concatenate([k_ref[prv, :], k_ref[cur, :]], axis=0).astype(BF16)
                    vw = jnp.concatenate([v_ref[prv, :], v_ref[cur, :]], axis=0).astype(BF16)
                    dob = do_ref[cur, :].astype(BF16)
                    dq_old = dq_ref[cur, :] if bi > 0 else None
                    loaded.append((qb, kw, vw, dob, lse_ref[cur, :], delta_ref[cur, :], dq_old))
                results = []
                for j, ((cur, prv, variant), (qb, kw, vw, dob, lse_b, dl_b, dq_old)) in enumerate(zip(rows, loaded)):
                    dq_pair = dk_pair = dv_pair = None
                    for e, msk in enumerate((lo, hi)):
                        qm = jnp.where(msk, qb, 0)
                        dom = jnp.where(msk, dob, 0)
                        s = _dot_nt(qm, kw) * scale + tab_ref[bi, variant, e]
                        pe = jnp.exp(s - lse_b[:, e * half:e * half + 1])
                        ds = pe * (_dot_nt(dom, vw) - dl_b[:, e * half:e * half + 1])
                        dtab_ref[bi, e] += ds
                        dsb = ds.astype(BF16)
                        dq_e = _dot_nn(dsb, jnp.where(msk, kw, 0))
                        dk_e = _dot_tn(dsb, qm)
                        dv_e = _dot_tn(pe.astype(BF16), dom)
                        dq_pair = dq_e if e == 0 else dq_pair + dq_e
                        dk_pair = dk_e if e == 0 else dk_pair + dk_e
                        dv_pair = dv_e if e == 0 else dv_pair + dv_e
                    dq_pair = dq_pair * scale
                    if bi > 0:
                        dq_pair = dq_pair + dq_old
                    results.append((dq_pair, dk_pair * scale, dv_pair))
                for (cur, prv, variant), (dq_pair, dk_pair, dv_pair) in zip(rows, results):
                    dq_ref[cur, :] = dq_pair
                    dkv_ref[0, cur, :] += dk_pair[P:, :]
                    dkv_ref[1, cur, :] += dv_pair[P:, :]
                    dkv_ref[0, prv, :] += dk_pair[:P, :]
                    dkv_ref[1, prv, :] += dv_pair[:P, :]
                return carry

            lax.fori_loop(0, nsteps, block, 0)

    nl = D // LANE
    col = lambda off: pl.BlockSpec((None, S, LANE), lambda p, b: (b, 0, off + p))
    tab_spec = pl.BlockSpec((3, 2, 2, P, 2 * P), lambda p, b: (0, 0, p, 0, 0))
    return pl.pallas_call(
        body, name=name, grid=(H // 2, B),
        in_specs=[col(0), col(0), col(nl), col(0), col(0), col(0), tab_spec],
        out_specs=[col(0), pl.BlockSpec((2, None, S, LANE), lambda p, b: (0, b, 0, p)),
                   pl.BlockSpec((None, 3, 2, P, 2 * P), lambda p, b: (p, 0, 0, 0, 0))],
        out_shape=[jax.ShapeDtypeStruct((B, S, D), F32), jax.ShapeDtypeStruct((2, B, S, D), F32),
                   jax.ShapeDtypeStruct((H // 2, 3, 2, P, 2 * P), F32)],
        scratch_shapes=[pltpu.VMEM((S, LANE), F32)],
        compiler_params=pltpu.CompilerParams(
            dimension_semantics=("parallel", "arbitrary"),
            vmem_limit_bytes=VMEM_CAP),
    )(q, kv, kv, do, o, lse, tabs)


def _attn_fwd_branch(q, kv, tab, dilation, name):
    B, S, D = q.shape
    P, d = ATT_BLOCK, dilation
    L = S // d
    nb = L // P
    H = N_HEADS
    scale = (D // H) ** -0.5
    qv = q.reshape(B, L, d * D)
    kvv = kv.reshape(B, L, d * 2 * D)

    def body(q_ref, kp_ref, kc_ref, vp_ref, vc_ref, tab_ref, o_ref, lse_ref):
        lo, hi = _lane_masks()
        for p in range(H // 2):
            sl = slice(p * LANE, (p + 1) * LANE)
            qp = q_ref[:, sl]
            kp = jnp.concatenate([kp_ref[:, sl], kc_ref[:, sl]], axis=0)
            vp = jnp.concatenate([vp_ref[:, sl], vc_ref[:, sl]], axis=0)
            o_pair = None
            lse_pair = None
            for e, msk in enumerate((lo, hi)):
                s = _dot_nt(jnp.where(msk, qp, 0), kp) * scale + tab_ref[2 * p + e]
                m = jnp.max(s, axis=-1, keepdims=True)
                pe = jnp.exp(s - m)
                den = jnp.sum(pe, axis=-1, keepdims=True)
                o_e = _dot_nn(pe.astype(BF16), jnp.where(msk, vp, 0)) / den
                lse_e = m + jnp.log(den)
                o_pair = o_e if e == 0 else o_pair + o_e
                lse_pair = lse_e if e == 0 else jnp.where(lo, lse_pair, lse_e)
            o_ref[:, sl] = o_pair
            lse_ref[:, sl] = lse_pair

    blk = lambda f: pl.BlockSpec((None, P, D), f)
    prev = lambda i: jnp.maximum(i - 1, 0)
    in_specs = [blk(lambda b, r, i: (b, i, r)),
                blk(lambda b, r, i: (b, prev(i), 2 * r)), blk(lambda b, r, i: (b, i, 2 * r)),
                blk(lambda b, r, i: (b, prev(i), 2 * r + 1)), blk(lambda b, r, i: (b, i, 2 * r + 1)),
                pl.BlockSpec((None, H, P, 2 * P), lambda b, r, i: (jnp.minimum(i, 1), 0, 0, 0))]
    o, lse = pl.pallas_call(
        body, name=name, grid=(B, d, nb), in_specs=in_specs,
        out_specs=[blk(lambda b, r, i: (b, i, r))] * 2,
        out_shape=[jax.ShapeDtypeStruct((B, L, d * D), F32)] * 2,
        compiler_params=pltpu.CompilerParams(
            dimension_semantics=("parallel", "parallel", "parallel"),
            vmem_limit_bytes=_vmem(_nbytes((H, P, 2 * P), F32) + 9 * _nbytes((P, D), F32))),
    )(qv, kvv, kvv, kvv, kvv, tab)
    return o.reshape(B * S, D), lse.reshape(B * S, D)


def _attn_merge(os_, lses, name, tm=512):
    T, D = os_[0].shape
    n = len(os_)
    tm = _tile(T, tm)

    def body(*refs):
        o_refs, l_refs = refs[:n], refs[n:2 * n]
        out_ref, lse_ref = refs[2 * n:]
        ls = [r[...] for r in l_refs]
        m = functools.reduce(jnp.maximum, ls)
        ws = [jnp.exp(l - m) for l in ls]
        tot = functools.reduce(jnp.add, ws)
        acc = functools.reduce(jnp.add, [w * r[...] for w, r in zip(ws, o_refs)])
        out_ref[...] = (acc / tot).astype(BF16)
        lse_ref[...] = m + jnp.log(tot)

    row = pl.BlockSpec((tm, D), lambda i: (i, 0))
    return pl.pallas_call(
        body, name=name, grid=(T // tm,), in_specs=[row] * (2 * n), out_specs=[row, row],
        out_shape=[jax.ShapeDtypeStruct((T, D), BF16), jax.ShapeDtypeStruct((T, D), F32)],
        compiler_params=pltpu.CompilerParams(
            dimension_semantics=("parallel",), vmem_limit_bytes=_vmem((2 * n + 4) * _nbytes((tm, D), F32))),
    )(*os_, *lses)


def _attn_bwd_q(q, kv, do, o, lse, tab, dq_acc, dilation, name):
    B, S, D = q.shape
    P, d = ATT_BLOCK, dilation
    L = S // d
    nb = L // P
    H = N_HEADS
    scale = (D // H) ** -0.5
    view = lambda t, c: t.reshape(B, L, d * c)
    has_acc = dq_acc is not None

    def body(q_ref, kp_ref, kc_ref, vp_ref, vc_ref, do_ref, o_ref, lse_ref, tab_ref, *rest):
        if has_acc:
            acc_ref, dq_ref, dtab_ref = rest
        else:
            dq_ref, dtab_ref = rest

        @pl.when((pl.program_id(0) == 0) & (pl.program_id(1) == 0) & (pl.program_id(2) == 0))
        def _():
            dtab_ref[...] = jnp.zeros_like(dtab_ref)

        lo, hi = _lane_masks()
        for p in range(H // 2):
            sl = slice(p * LANE, (p + 1) * LANE)
            qp = q_ref[:, sl]
            kp = jnp.concatenate([kp_ref[:, sl], kc_ref[:, sl]], axis=0)
            vp = jnp.concatenate([vp_ref[:, sl], vc_ref[:, sl]], axis=0)
            dop = do_ref[:, sl]
            prod = dop.astype(F32) * o_ref[:, sl].astype(F32)
            lsep = lse_ref[:, sl]
            dq_pair = None
            for e, msk in enumerate((lo, hi)):
                s = _dot_nt(jnp.where(msk, qp, 0), kp) * scale + tab_ref[2 * p + e]
                pe = jnp.exp(s - lsep[:, e * (LANE // 2):e * (LANE // 2) + 1])
                dp = _dot_nt(jnp.where(msk, dop, 0).astype(BF16), vp)
                delta = jnp.sum(jnp.where(msk, prod, 0.0), axis=-1, keepdims=True)
                ds = pe * (dp - delta)
                dtab_ref[2 * p + e] += ds
                dq_e = _dot_nn(ds.astype(BF16), jnp.where(msk, kp, 0))
                dq_pair = dq_e if e == 0 else dq_pair + dq_e
            dq_pair = dq_pair * scale
            if has_acc:
                dq_pair = dq_pair + acc_ref[:, sl]
            dq_ref[:, sl] = dq_pair

    blk = lambda f: pl.BlockSpec((None, P, D), f)
    prev = lambda i: jnp.maximum(i - 1, 0)
    own = blk(lambda b, r, i: (b, i, r))
    tab_spec = pl.BlockSpec((None, H, P, 2 * P), lambda b, r, i: (jnp.minimum(i, 1), 0, 0, 0))
    in_specs = [own,
                blk(lambda b, r, i: (b, prev(i), 2 * r)), blk(lambda b, r, i: (b, i, 2 * r)),
                blk(lambda b, r, i: (b, prev(i), 2 * r + 1)), blk(lambda b, r, i: (b, i, 2 * r + 1)),
                own, own, own, tab_spec]
    kvv = view(kv, 2 * D)
    args = [view(q, D), kvv, kvv, kvv, kvv, view(do, D), view(o, D), view(lse, D), tab]
    if has_acc:
        in_specs.append(own)
        args.append(view(dq_acc, D))
    dq, dtab = pl.pallas_call(
        body, name=name, grid=(B, d, nb), in_specs=in_specs,
        out_specs=[own, pl.BlockSpec((H, P, 2 * P), lambda b, r, i: (0, 0, 0))],
        out_shape=[jax.ShapeDtypeStruct((B, L, d * D), F32), jax.ShapeDtypeStruct((H, P, 2 * P), F32)],
        compiler_params=pltpu.CompilerParams(
            dimension_semantics=("arbitrary", "arbitrary", "arbitrary"),
            vmem_limit_bytes=_vmem(2 * _nbytes((H, P, 2 * P), F32) + 12 * _nbytes((P, D), F32))),
    )(*args)
    return dq.reshape(B, S, D), dtab


def _attn_bwd_kv(q, kv, do, o, lse, tabk, dkv_acc, dilation, name):
    B, S, D = q.shape
    P, d = ATT_BLOCK, dilation
    L = S // d
    nb = L // P
    H = N_HEADS
    scale = (D // H) ** -0.5
    view = lambda t, c: t.reshape(B, L, d * c)
    has_acc = dkv_acc is not None

    def body(k_ref, v_ref, qa_ref, qb_ref, doa_ref, dob_ref, oa_ref, ob_ref, la_ref, lb_ref, tab_ref, *rest):
        if has_acc:
            acc_ref, dkv_ref = rest
        else:
            (dkv_ref,) = rest
        lo, hi = _lane_masks()
        for p in range(H // 2):
            sl = slice(p * LANE, (p + 1) * LANE)
            kp, vp = k_ref[:, sl], v_ref[:, sl]
            q2 = jnp.concatenate([qa_ref[:, sl], qb_ref[:, sl]], axis=0)
            do2 = jnp.concatenate([doa_ref[:, sl], dob_ref[:, sl]], axis=0)
            o2 = jnp.concatenate([oa_ref[:, sl], ob_ref[:, sl]], axis=0)
            lse2 = jnp.concatenate([la_ref[:, sl], lb_ref[:, sl]], axis=0)
            prod = do2.astype(F32) * o2.astype(F32)
            dk_pair = None
            dv_pair = None
            for e, msk in enumerate((lo, hi)):
                qm = jnp.where(msk, q2, 0)
                dom = jnp.where(msk, do2, 0).astype(BF16)
                s = _dot_nt(qm, kp) * scale + tab_ref[2 * p + e]
                pe = jnp.exp(s - lse2[:, e * (LANE // 2):e * (LANE // 2) + 1])
                dp = _dot_nt(dom, vp)
                delta = jnp.sum(jnp.where(msk, prod, 0.0), axis=-1, keepdims=True)
                ds = pe * (dp - delta)
                dv_e = _dot_tn(pe.astype(BF16), dom)
                dk_e = _dot_tn(ds.astype(BF16), qm)
                dk_pair = dk_e if e == 0 else dk_pair + dk_e
                dv_pair = dv_e if e == 0 else dv_pair + dv_e
            dk_pair = dk_pair * scale
            if has_acc:
                dk_pair = dk_pair + acc_ref[0, :, sl]
                dv_pair = dv_pair + acc_ref[1, :, sl]
            dkv_ref[0, :, sl] = dk_pair
            dkv_ref[1, :, sl] = dv_pair

    blk = lambda f: pl.BlockSpec((None, P, D), f)
    nxt = lambda i: jnp.minimum(i + 1, nb - 1)
    own = blk(lambda b, r, i: (b, i, r))
    nx = blk(lambda b, r, i: (b, nxt(i), r))
    tab_spec = pl.BlockSpec((None, H, 2 * P, P), lambda b, r, i: (jnp.where(i == nb - 1, 1, 0), 0, 0, 0))
    pair = pl.BlockSpec((2, None, P, D), lambda b, r, i: (0, b, i, r))
    in_specs = [blk(lambda b, r, i: (b, i, 2 * r)), blk(lambda b, r, i: (b, i, 2 * r + 1)),
                own, nx, own, nx, own, nx, own, nx, tab_spec]
    kvv, qv, dov, ov, lv = view(kv, 2 * D), view(q, D), view(do, D), view(o, D), view(lse, D)
    args = [kvv, kvv, qv, qv, dov, dov, ov, ov, lv, lv, tabk]
    if has_acc:
        in_specs.append(pair)
        args.append(dkv_acc.reshape(2, B, L, d * D))
    dkv = pl.pallas_call(
        body, name=name, grid=(B, d, nb), in_specs=in_specs, out_specs=pair,
        out_shape=jax.ShapeDtypeStruct((2, B, L, d * D), F32),
        compiler_params=pltpu.CompilerParams(
            dimension_semantics=("parallel", "parallel", "parallel"),
            vmem_limit_bytes=_vmem(_nbytes((H, P, 2 * P), F32) + 16 * _nbytes((P, D), F32))),
    )(*args)
    return dkv.reshape(2, B, S, D)


def _mesh_place():
    x, y, c = lax.axis_index("x"), lax.axis_index("y"), lax.axis_index("c")
    return x, y, c, (x, y, 1 - c), [(1 - x, y), (x, 1 - y), (1 - x, 1 - y)]


def _rdma(src, dst, send_sems, recv_sems, idx, to):
    return pltpu.make_async_remote_copy(
        src_ref=src, dst_ref=dst, send_sem=send_sems.at[idx], recv_sem=recv_sems.at[idx],
        device_id=to, device_id_type=pl.DeviceIdType.MESH)


def _comm_call(body, name, args, out_shape, n_sems, n_local):
    any_spec = pl.BlockSpec(memory_space=pl.ANY)
    return pl.pallas_call(
        body, name=name, in_specs=[any_spec] * len(args), out_specs=[any_spec] * len(out_shape),
        out_shape=out_shape,
        scratch_shapes=[pltpu.SemaphoreType.DMA(n_sems), pltpu.SemaphoreType.DMA(n_sems),
                        pltpu.SemaphoreType.DMA((n_local,))],
        compiler_params=pltpu.CompilerParams(has_side_effects=True),
    )(*args)


def _all_gather(arrays, name):
    n = len(arrays)

    def body(*refs):
        ins, outs = refs[:n], refs[n:2 * n]
        send_sems, recv_sems, loc_sems = refs[2 * n:]
        x, y, c, sib, chips = _mesh_place()
        me = (x, y, c)

        def rows(a, dev):
            return outs[a].at[4 * dev[0] + 2 * dev[1] + dev[2]]

        def copy(a, k, block, to, src=None):
            return _rdma(rows(a, block) if src is None else src, rows(a, block), send_sems, recv_sems, (a, k), to)

        local, first, passed = [], [], []
        for a in range(n):
            cp = pltpu.make_async_copy(ins[a], rows(a, me), loc_sems.at[a])
            cp.start()
            local.append(cp)
            first.append(copy(a, 0, me, sib, src=ins[a]))
            first += [copy(a, 1 + j, me, (*chip, c), src=ins[a]) for j, chip in enumerate(chips)]
        for cp in first:
            cp.start()
        for j, chip in enumerate(chips):
            for a in range(n):
                copy(a, 1 + j, (*chip, c), me).wait_recv()
                cp = copy(a, 4 + j, (*chip, c), sib)
                cp.start()
                passed.append(cp)
        for a in range(n):
            copy(a, 0, sib, me).wait_recv()
            for j, chip in enumerate(chips):
                copy(a, 4 + j, (*chip, 1 - c), me).wait_recv()
        for cp in first + passed:
            cp.wait_send()
        for cp in local:
            cp.wait()

    out_shape = [jax.ShapeDtypeStruct((N_DEV,) + g.shape, g.dtype) for g in arrays]
    return list(_comm_call(body, name, arrays, out_shape, (n, N_DEV - 1), n))


def _core_pair_exchange(halves, gathered, name):
    nh, ng = len(halves), len(gathered)

    def body(*refs):
        h_in, g_in = refs[:nh], refs[nh:nh + ng]
        outs = refs[nh + ng:nh + ng + 2 * nh + ng]
        mine, theirs, g_out = outs[:nh], outs[nh:2 * nh], outs[2 * nh:]
        send_sems, recv_sems, loc_sems = refs[nh + ng + 2 * nh + ng:]
        x, y, c, sib, chips = _mesh_place()
        me = 4 * x + 2 * y + c
        peers = [sib] + [(*chip, pc) for chip in chips for pc in (c, 1 - c)]
        nchip = N_DEV // 2
        gbase = nh * nchip
        local, sends = [], []
        for a in range(nh):
            for ch in range(nchip):
                cp = pltpu.make_async_copy(h_in[a].at[ch, c], mine[a].at[ch], loc_sems.at[a * nchip + ch])
                cp.start()
                local.append(cp)
                cp = _rdma(h_in[a].at[ch, 1 - c], theirs[a].at[ch], send_sems, recv_sems, a * nchip + ch, sib)
                cp.start()
                sends.append(cp)
        for g in range(ng):
            cp = pltpu.make_async_copy(g_in[g], g_out[g].at[me], loc_sems.at[gbase + g])
            cp.start()
            local.append(cp)
            for k, peer in enumerate(peers):
                cp = _rdma(g_in[g], g_out[g].at[me], send_sems, recv_sems, gbase + g * (N_DEV - 1) + k, peer)
                cp.start()
                sends.append(cp)
        for cp in sends:
            cp.wait_send()
        for a in range(nh):
            for ch in range(nchip):
                _rdma(h_in[a].at[ch, 1 - c], theirs[a].at[ch], send_sems, recv_sems, a * nchip + ch, sib).wait_recv()
        for g in range(ng):
            for k, peer in enumerate(peers):
                pid = 4 * peer[0] + 2 * peer[1] + peer[2]
                _rdma(g_in[g], g_out[g].at[pid], send_sems, recv_sems, gbase + g * (N_DEV - 1) + k, peer).wait_recv()
        for cp in local:
            cp.wait()

    half = [jax.ShapeDtypeStruct((h.shape[0],) + h.shape[2:], h.dtype) for h in halves]
    out_shape = half + half + [jax.ShapeDtypeStruct((N_DEV,) + g.shape, g.dtype) for g in gathered]
    outs = _comm_call(body, name, list(halves) + list(gathered), out_shape,
                      (nh * (N_DEV // 2) + ng * (N_DEV - 1),), nh * (N_DEV // 2) + ng)
    return list(outs[:nh]), list(outs[nh:2 * nh]), list(outs[2 * nh:])


def _chip_exchange(arrays, name):
    n = len(arrays)

    def body(*refs):
        ins, outs = refs[:n], refs[n:2 * n]
        send_sems, recv_sems, loc_sems = refs[2 * n:]
        x, y, c, sib, chips = _mesh_place()
        my_chip = 2 * x + y
        local, sends = [], []
        for a in range(n):
            cp = pltpu.make_async_copy(ins[a].at[my_chip], outs[a].at[my_chip], loc_sems.at[a])
            cp.start()
            local.append(cp)
            for j, chip in enumerate(chips):
                cp = _rdma(ins[a].at[2 * chip[0] + chip[1]], outs[a].at[my_chip], send_sems, recv_sems, (a, j),
                           (*chip, c))
                cp.start()
                sends.append(cp)
        for cp in sends:
            cp.wait_send()
        for a in range(n):
            for j, chip in enumerate(chips):
                slot = 2 * chip[0] + chip[1]
                _rdma(ins[a].at[slot], outs[a].at[slot], send_sems, recv_sems, (a, j), (*chip, c)).wait_recv()
        for cp in local:
            cp.wait()

    out_shape = [jax.ShapeDtypeStruct(g.shape, g.dtype) for g in arrays]
    return list(_comm_call(body, name, arrays, out_shape, (n, 3), n))


def _add_cast(a, b, out_dtype, name, tr=512):
    R, C = a.shape
    tr = _row_tile(R, tr)

    def body(a_ref, b_ref, o_ref):
        o_ref[...] = (a_ref[...] + b_ref[...]).astype(out_dtype)

    row = pl.BlockSpec((tr, C), lambda i: (i, 0))
    return pl.pallas_call(
        body, name=name, grid=(R // tr,), in_specs=[row, row], out_specs=row,
        out_shape=jax.ShapeDtypeStruct((R, C), out_dtype),
        compiler_params=pltpu.CompilerParams(
            dimension_semantics=("parallel",), vmem_limit_bytes=_vmem(3 * _nbytes((tr, C), F32))),
    )(a, b)


def _row_tile(rows, target):
    best = rows
    for t in range(8, min(rows, target) + 1, 8):
        if rows % t == 0:
            best = t
    return best


def _sum_slots(recv, name, tr=128):
    n, R, C = recv.shape
    tr = _row_tile(R, tr if recv.dtype == F32 else 2 * tr)

    def body(r_ref, o_ref):
        acc = r_ref[0].astype(F32)
        for k in range(1, n):
            acc = acc + r_ref[k].astype(F32)
        o_ref[...] = acc

    return pl.pallas_call(
        body, name=name, grid=(R // tr,),
        in_specs=[pl.BlockSpec((n, tr, C), lambda i: (0, i, 0))],
        out_specs=pl.BlockSpec((tr, C), lambda i: (i, 0)),
        out_shape=jax.ShapeDtypeStruct((R, C), F32),
        compiler_params=pltpu.CompilerParams(
            dimension_semantics=("parallel",), vmem_limit_bytes=_vmem(10 * _nbytes((tr, C), F32))),
    )(recv)


def _adamw(g, w, m, v, name, tr=256):
    R, C = g.shape
    tr = _row_tile(R, tr)

    def body(g_ref, w_ref, m_ref, v_ref, d_ref, nm_ref, nv_ref):
        gv = g_ref[...]
        nm = ADAM_B1 * m_ref[...] + (1.0 - ADAM_B1) * gv
        nv = ADAM_B2 * v_ref[...] + (1.0 - ADAM_B2) * (gv * gv)
        m_hat = nm / (1.0 - ADAM_B1 ** ADAM_STEP)
        v_hat = nv / (1.0 - ADAM_B2 ** ADAM_STEP)
        d_ref[...] = -ADAM_LR * (m_hat / (jnp.sqrt(v_hat) + ADAM_EPS) + ADAM_WD * w_ref[...])
        nm_ref[...] = nm
        nv_ref[...] = nv

    row = pl.BlockSpec((tr, C), lambda i: (i, 0))
    return pl.pallas_call(
        body, name=name, grid=(R // tr,), in_specs=[row] * 4, out_specs=[row] * 3,
        out_shape=[jax.ShapeDtypeStruct((R, C), F32)] * 3,
        compiler_params=pltpu.CompilerParams(
            dimension_semantics=("parallel",), vmem_limit_bytes=_vmem(8 * _nbytes((tr, C), F32))),
    )(g, w, m, v)


def _pack_rows(parts):
    flat = jnp.concatenate([p.reshape(-1).astype(F32) for p in parts])
    rows = -(-flat.shape[0] // (8 * LANE)) * 8
    return jnp.pad(flat, (0, rows * LANE - flat.shape[0])).reshape(rows, LANE)


def _unpack(flat, shapes):
    out, off = [], 0
    for shp in shapes:
        n = math.prod(shp)
        out.append(flat[off:off + n].reshape(shp))
        off += n
    return out


def kernel(x, a_norm, a_w_in, a_conv, a_w_out, kv_norm, w_kv, b_norm, b_w_q, b_w_o, rel_bias, ffn_norm, ffn_w_up, ffn_conv, ffn_conv_b, ffn_w_down, final_norm, loss_target, m_a_norm, m_a_w_in, m_a_conv, m_a_w_out, m_kv_norm, m_w_kv, m_b_norm, m_b_w_q, m_b_w_o, m_rel_bias, m_ffn_norm, m_ffn_w_up, m_ffn_conv, m_ffn_conv_b, m_ffn_w_down, m_final_norm, v_a_norm, v_a_w_in, v_a_conv, v_a_w_out, v_kv_norm, v_w_kv, v_b_norm, v_b_w_q, v_b_w_o, v_rel_bias, v_ffn_norm, v_ffn_w_up, v_ffn_conv, v_ffn_conv_b, v_ffn_w_down, v_final_norm):
    B, S, D = x.shape
    T = B * S
    F = ffn_w_down.shape[1] * N_DEV
    me = 4 * lax.axis_index("x") + 2 * lax.axis_index("y") + lax.axis_index("c")

    big_shards = [a_w_in[0].T, a_w_out[0], w_kv.T, b_w_q[0], b_w_o[0],
                  ffn_w_up[0].T, ffn_w_up[1].T, ffn_w_down[0], ffn_w_down[1]]
    small_shapes = [a_norm.shape, a_conv.shape, ffn_conv.shape]
    small_pack = _pack_rows([a_norm, a_conv, ffn_conv])
    gathered = _all_gather([s.astype(BF16) for s in big_shards] + [small_pack], "gather_weights")
    win_t, wout, wkv_t, wq, wo, wup0_t, wup1_t, wdn0, wdn1 = [
        g.reshape(-1, D) for g in gathered[:9]]
    wup_t, wdn = (wup0_t, wup1_t), (wdn0, wdn1)
    smalls = [_unpack(gathered[9][j].reshape(-1), small_shapes) for j in range(N_DEV)]
    a_norm_f = jnp.concatenate([s[0] for s in smalls], axis=-1)
    a_conv_f = jnp.concatenate([s[1] for s in smalls], axis=-1)[0]
    ffn_conv_f = jnp.concatenate([s[2] for s in smalls], axis=-1)

    x2 = x.reshape(T, D)
    (xn0,) = _rmsnorm_fwd(x2, a_norm_f, "a_norm_fwd")
    bch = _matmul(xn0, win_t, mode="nt", out_dtype=BF16, name="a_in_proj").reshape(B, S, 3 * D)
    gated = _shortconv_fwd(bch, a_conv_f, "a_gate_fwd").reshape(T, D)
    h1 = _matmul(gated, wout, mode="nn", out_dtype=F32, add=x2, name="a_out_proj")

    def ffn_fwd(h, l):
        (xn,) = _rmsnorm_fwd(h, ffn_norm[l:l + 1], f"ffn{l}_norm_fwd")
        u0 = _matmul(xn, wup_t[l], mode="nt", out_dtype=BF16, name=f"ffn{l}_up").reshape(B, S, 2 * F)
        act = _ffn_gate_fwd(u0, ffn_conv_f[l], ffn_conv_b[l:l + 1], f"ffn{l}_gate_fwd").reshape(T, F)
        out = _matmul(act, wdn[l], mode="nn", out_dtype=F32, add=h, name=f"ffn{l}_down", tk=2816)
        return out, (xn, u0, act)

    h2, ffn0_saved = ffn_fwd(h1, 0)
    kvn, xnb = _rmsnorm_fwd(h2, jnp.stack([kv_norm, b_norm[0]]), "kv_b_norm_fwd")
    kv = _matmul(kvn, wkv_t, mode="nt", out_dtype=F32, name="kv_proj").reshape(B, S, 2 * D)
    q = _matmul(xnb, wq, mode="nn", out_dtype=F32, name="q_proj").reshape(B, S, D)

    tables = [_band_tables(w, d) for (w, d) in DILATED_BRANCHES]
    onehots = jnp.stack([t[0] for t in tables])
    P = ATT_BLOCK
    bias_vals = _bias_lookup(rel_bias.T, onehots, "rel_bias_lookup").reshape(3, N_HEADS, P, 2 * P)
    in_cur = (jnp.arange(2 * P) >= P)[None, :]
    tabs = []
    for bi in range(3):
        band = tables[bi][1]
        gen = jnp.where(band[None], bias_vals[bi], NEG)
        fst = jnp.where((band & in_cur)[None], bias_vals[bi], NEG)
        tabs.append(jnp.stack([fst, gen]))
    tabs = jnp.stack(tabs)

    attn3, lse3 = _attn_fwd(q, kv, tabs, "attn_fwd")
    attn = attn3.reshape(T, D)
    h3 = _matmul(attn, wo, mode="nn", out_dtype=F32, add=h2, name="o_proj")
    h4, ffn1_saved = ffn_fwd(h3, 1)

    dh4, dh4b, d_final_norm, sq = _final_loss_bwd(h4, final_norm.reshape(1, D), loss_target.reshape(T, D), "loss_bwd")
    loss_part = 0.5 * jnp.sum(sq) / D

    def ffn_bwd(dh_out, dh_out_b, h_in, saved, l):
        xn, u0, act = saved
        dact = _matmul(dh_out_b, wdn[l], mode="nt", out_dtype=BF16, name=f"ffn{l}_down_dx", tn=1408).reshape(B, S, F)
        d_wdn = _matmul(act, dh_out_b, mode="tn", out_dtype=F32, name=f"ffn{l}_down_dw", tm=1408, tn=1024, tk=2048)
        du0, d_conv, d_conv_b = _ffn_gate_bwd(u0, dact, ffn_conv_f[l], ffn_conv_b[l:l + 1], f"ffn{l}_gate_bwd")
        du0 = du0.reshape(2, T, F)
        dxn = _matmul(du0, wup_t[l], mode="nn", out_dtype=F32, a_parts=2, name=f"ffn{l}_up_dx", tk=2816)
        d_wup_t = _matmul(du0, xn, mode="tn", out_dtype=F32, a_parts=2, name=f"ffn{l}_up_dw", tm=1408, tn=1024, tk=2048)
        dh_in, dh_in_b, d_norm = _rmsnorm_bwd(h_in, ffn_norm[l:l + 1], [dxn], dh_out, f"ffn{l}_norm_bwd")
        return (dh_in, dh_in_b), (d_wdn, d_wup_t, d_conv, d_conv_b, d_norm)

    (dh3, dh3b), ffn1_grads = ffn_bwd(dh4, dh4b, h3, ffn1_saved, 1)

    dattn = _matmul(dh3b, wo, mode="nt", out_dtype=F32, name="o_proj_dx").reshape(B, S, D)
    d_wo = _matmul(attn, dh3b, mode="tn", out_dtype=F32, name="o_proj_dw", tm=1024, tn=1024, tk=2048)
    dq, dkv, dtab = _attn_bwd(q, kv, dattn, attn3, lse3, tabs, "attn_bwd")
    dtabs = dtab.transpose(1, 0, 2, 3, 4).reshape(3, N_HEADS, P * 2 * P)
    d_rel_bias = _bias_grad(dtabs, onehots, "rel_bias_grad").T
    dq2 = dq.reshape(T, D)
    dkv2 = dkv.reshape(2, T, D)
    dxnb = _matmul(dq2, wq, mode="nt", out_dtype=F32, name="q_proj_dx")
    d_wq = _matmul(xnb, dq2, mode="tn", out_dtype=F32, name="q_proj_dw", tm=1024, tn=1024, tk=2048)
    dkvn = _matmul(dkv2, wkv_t, mode="nn", out_dtype=F32, a_parts=2, name="kv_proj_dx")
    d_wkv_t = _matmul(dkv2, kvn, mode="tn", out_dtype=F32, a_parts=2, name="kv_proj_dw", tm=1024, tn=1024, tk=2048)
    dh2, dh2b, d_kvb_norm = _rmsnorm_bwd(h2, jnp.stack([kv_norm, b_norm[0]]), [dkvn, dxnb], dh3, "kv_b_norm_bwd")

    (dh1, dh1b), ffn0_grads = ffn_bwd(dh2, dh2b, h1, ffn0_saved, 0)

    dgated = _matmul(dh1b, wout, mode="nt", out_dtype=BF16, name="a_out_proj_dx").reshape(B, S, D)
    d_wout = _matmul(gated, dh1b, mode="tn", out_dtype=F32, name="a_out_proj_dw", tm=1024, tn=1024, tk=2048)
    dbch, d_a_conv = _shortconv_bwd(bch, dgated, a_conv_f, "a_gate_bwd")
    dbch = dbch.reshape(3, T, D)
    dxn0 = _matmul(dbch, win_t, mode="nn", out_dtype=F32, a_parts=3, name="a_in_proj_dx")
    d_win_t = _matmul(dbch, xn0, mode="tn", out_dtype=F32, a_parts=3, name="a_in_proj_dw", tm=1024, tn=1024, tk=2048)
    grad_x, _, d_a_norm = _rmsnorm_bwd(x2, a_norm_f, [dxn0], dh1, "a_norm_bwd")

    big_grads = [d_win_t, d_wout, d_wkv_t, d_wq, d_wo, ffn0_grads[1], ffn1_grads[1], ffn0_grads[0], ffn1_grads[0]]
    small_full = [d_a_norm, d_a_conv, jnp.stack([ffn0_grads[2], ffn1_grads[2]]),
                  d_kvb_norm[0], d_kvb_norm[1], d_rel_bias, jnp.concatenate([ffn0_grads[4], ffn1_grads[4]]),
                  jnp.concatenate([ffn0_grads[3], ffn1_grads[3]]), d_final_norm, loss_part]
    small_full_shapes = [(1, D), (3, D), (2, 3, 2 * F), (D,), (1, D), rel_bias.shape, (2, D), (2, 2 * F), (D,), ()]
    halves = [g.reshape(N_DEV // 2, 2, g.shape[0] // N_DEV, D) for g in big_grads]
    mine, theirs, gath = _core_pair_exchange(halves, [_pack_rows(small_full)], "exchange_grads_core")
    chip_sums = [_add_cast(m_.reshape(-1, D), t_.reshape(-1, D), BF16, f"grad_core_sum_{i}").reshape(m_.shape)
                 for i, (m_, t_) in enumerate(zip(mine, theirs))]
    recv = _chip_exchange(chip_sums, "exchange_grads_chip")
    small_sum = _sum_slots(gath[0], "small_grad_sum").reshape(-1)
    (g_a_norm_f, g_a_conv_f, g_ffn_conv_f, g_kv_norm, g_b_norm, g_rel_bias, g_ffn_norm, g_ffn_conv_b,
     g_final_norm, loss) = _unpack(small_sum, small_full_shapes)

    def my_cols(full, width):
        return lax.dynamic_slice_in_dim(full, me * width, width, axis=full.ndim - 1)

    g_a_norm = my_cols(g_a_norm_f, D // N_DEV)
    g_a_conv = my_cols(g_a_conv_f, D // N_DEV)[None]
    g_ffn_conv = my_cols(g_ffn_conv_f, 2 * F // N_DEV)

    big_w = [(a_w_in, m_a_w_in, v_a_w_in, True), (a_w_out, m_a_w_out, v_a_w_out, False),
             (w_kv, m_w_kv, v_w_kv, True), (b_w_q, m_b_w_q, v_b_w_q, False), (b_w_o, m_b_w_o, v_b_w_o, False),
             (ffn_w_up[0], m_ffn_w_up[0], v_ffn_w_up[0], True), (ffn_w_up[1], m_ffn_w_up[1], v_ffn_w_up[1], True),
             (ffn_w_down[0], m_ffn_w_down[0], v_ffn_w_down[0], False),
             (ffn_w_down[1], m_ffn_w_down[1], v_ffn_w_down[1], False)]
    big_out = []
    for i, (w, m, v, transposed) in enumerate(big_w):
        g = _sum_slots(recv[i], f"grad_sum_{i}")
        if transposed:
            g = g.T
        w2, m2, v2 = (t.reshape(g.shape) for t in (w, m, v))
        delta, nm, nv = _adamw(g, w2, m2, v2, f"adamw_{i}")
        big_out.append(tuple(t.reshape(w.shape) for t in (g, delta, nm, nv)))

    def pair(i, j):
        return tuple(jnp.stack([big_out[i][t], big_out[j][t]]) for t in range(4))

    small_w = [(a_norm, m_a_norm, v_a_norm, g_a_norm), (a_conv, m_a_conv, v_a_conv, g_a_conv),
               (ffn_conv, m_ffn_conv, v_ffn_conv, g_ffn_conv), (kv_norm, m_kv_norm, v_kv_norm, g_kv_norm),
               (b_norm, m_b_norm, v_b_norm, g_b_norm), (rel_bias, m_rel_bias, v_rel_bias, g_rel_bias),
               (ffn_norm, m_ffn_norm, v_ffn_norm, g_ffn_norm),
               (ffn_conv_b, m_ffn_conv_b, v_ffn_conv_b, g_ffn_conv_b),
               (final_norm, m_final_norm, v_final_norm, g_final_norm)]
    shapes = [t[0].shape for t in small_w]
    packed = [_pack_rows([t[i].reshape(t[0].shape) for t in small_w]) for i in (3, 0, 1, 2)]
    s_delta, s_nm, s_nv = _adamw(*packed, "adamw_small")
    s_g = [t[3].reshape(t[0].shape) for t in small_w]
    s_d, s_m, s_v = (_unpack(t.reshape(-1), shapes) for t in (s_delta, s_nm, s_nv))
    small_out = [(s_g[i], s_d[i], s_m[i], s_v[i]) for i in range(len(small_w))]

    per_weight = [small_out[0], big_out[0], small_out[1], big_out[1], small_out[3], big_out[2], small_out[4],
                  big_out[3], big_out[4], small_out[5], small_out[6], pair(5, 6), small_out[2], small_out[7],
                  pair(7, 8), small_out[8]]
    outs = [loss, grad_x.reshape(B, S, D)]
    for t in range(4):
        outs.extend(pw[t] for pw in per_weight)
    return tuple(outs)
```

```python
import functools
import math

import jax
import jax.numpy as jnp
from jax import lax
from jax.experimental import pallas as pl
from jax.experimental.pallas import tpu as pltpu

F32 = jnp.float32
BF16 = jnp.bfloat16

N_DEV = 8
N_HEADS = 16
ATT_BLOCK = 128
ATT_UNROLL = 4
DILATED_BRANCHES = ((128, 1), (512, 4), (2048, 16))
REL_BUCKETS = 32
REL_MAX_DISTANCE = 2048
RMS_EPS = 1e-6
ADAM_LR = 0.001
ADAM_B1 = 0.9
ADAM_B2 = 0.999
ADAM_EPS = 1e-08
ADAM_WD = 0.01
ADAM_STEP = 10

LANE = 128
HALO = 16
NEG = -1e30
VMEM_CAP = 56 << 20


def _vmem(block_bytes):
    return int(min(VMEM_CAP, max(32 << 20, 3 * block_bytes + (8 << 20))))


def _nbytes(shape, dtype):
    return math.prod(shape) * jnp.dtype(dtype).itemsize


def _tile(dim, target):
    best = None
    t = LANE
    while t <= min(dim, target):
        if dim % t == 0:
            best = t
        t += LANE
    return best if best is not None else dim


def _matmul(a, b, *, mode, out_dtype, name, add=None, a_parts=1, tm=1024, tn=512, tk=1024):
    P = a_parts
    if mode == "nn":
        M, K = (a.shape[0], a.shape[1]) if P == 1 else (a.shape[1], a.shape[2] * P)
        N = b.shape[1]
    elif mode == "nt":
        assert P == 1
        M, K = a.shape
        N = b.shape[0]
    else:
        K = a.shape[0] if P == 1 else a.shape[1]
        M = a.shape[1] if P == 1 else a.shape[2] * P
        N = b.shape[1]
    tm = _tile(M // P if mode == "tn" else M, tm)
    tn = _tile(N, tn)
    tk = _tile(K // P if mode == "nn" else K, tk)
    nm, nn_, nk = M // tm, N // tn, K // tk

    if mode == "nn":
        if P == 1:
            a_spec = pl.BlockSpec((tm, tk), lambda i, j, k: (i, k))
        else:
            nkp = nk // P
            a_spec = pl.BlockSpec((None, tm, tk), lambda i, j, k: (k // nkp, i, k % nkp))
        b_spec = pl.BlockSpec((tk, tn), lambda i, j, k: (k, j))
        dims = (((1,), (0,)), ((), ()))
    elif mode == "nt":
        a_spec = pl.BlockSpec((tm, tk), lambda i, j, k: (i, k))
        b_spec = pl.BlockSpec((tn, tk), lambda i, j, k: (j, k))
        dims = (((1,), (1,)), ((), ()))
    else:
        if P == 1:
            a_spec = pl.BlockSpec((tk, tm), lambda i, j, k: (k, i))
        else:
            nmp = nm // P
            a_spec = pl.BlockSpec((None, tk, tm), lambda i, j, k: (i // nmp, k, i % nmp))
        b_spec = pl.BlockSpec((tk, tn), lambda i, j, k: (k, j))
        dims = (((0,), (0,)), ((), ()))
    o_spec = pl.BlockSpec((tm, tn), lambda i, j, k: (i, j))
    in_specs = [a_spec, b_spec]
    args = [a, b]
    if add is not None:
        in_specs.append(o_spec)
        args.append(add)
    has_add = add is not None

    direct = nk > 1 and out_dtype == F32 and not has_add

    def body(a_ref, b_ref, *rest):
        if has_add:
            add_ref, o_ref = rest[:2]
        else:
            o_ref = rest[0]
        acc_ref = o_ref if direct or nk == 1 else rest[-1]
        k = pl.program_id(2)

        def part():
            return lax.dot_general(a_ref[...].astype(BF16), b_ref[...].astype(BF16), dims,
                                   preferred_element_type=F32)

        def finish(r):
            if has_add:
                r = r + add_ref[...].astype(F32)
            o_ref[...] = r.astype(out_dtype)

        if nk == 1:
            finish(part())
        else:
            @pl.when(k == 0)
            def _():
                acc_ref[...] = part()

            @pl.when(jnp.logical_and(k > 0, jnp.logical_or(k < nk - 1, direct)))
            def _():
                acc_ref[...] += part()

            if not direct:
                @pl.when(k == nk - 1)
                def _():
                    finish(acc_ref[...] + part())

    blk = (_nbytes((tm, tk), a.dtype) + _nbytes((tk, tn), b.dtype) + _nbytes((tm, tn), out_dtype)
           + (_nbytes((tm, tn), add.dtype) if has_add else 0)) * 2 + 3 * _nbytes((tm, tn), F32)
    return pl.pallas_call(
        body, name=name, grid=(nm, nn_, nk),
        in_specs=in_specs, out_specs=o_spec,
        out_shape=jax.ShapeDtypeStruct((M, N), out_dtype),
        scratch_shapes=[] if direct or nk == 1 else [pltpu.VMEM((tm, tn), F32)],
        compiler_params=pltpu.CompilerParams(
            dimension_semantics=("parallel", "parallel", "arbitrary"), vmem_limit_bytes=_vmem(blk)),
    )(*args)


def _rmsnorm_fwd(x, gains, name, tm=512):
    T, D = x.shape
    n = gains.shape[0]
    tm = _tile(T, tm)

    def body(x_ref, g_ref, *o_refs):
        xv = x_ref[...]
        xhat = xv * lax.rsqrt(jnp.mean(xv * xv, axis=-1, keepdims=True) + RMS_EPS)
        for i in range(n):
            o_refs[i][...] = (xhat * g_ref[i:i + 1, :]).astype(BF16)

    row = pl.BlockSpec((tm, D), lambda i: (i, 0))
    outs = pl.pallas_call(
        body, name=name, grid=(T // tm,),
        in_specs=[row, pl.BlockSpec((n, D), lambda i: (0, 0))],
        out_specs=[row] * n,
        out_shape=[jax.ShapeDtypeStruct((T, D), BF16)] * n,
        compiler_params=pltpu.CompilerParams(
            dimension_semantics=("parallel",), vmem_limit_bytes=_vmem(4 * _nbytes((tm, D), F32))),
    )(x, gains)
    return tuple(outs)


def _rmsnorm_bwd(x, gains, dxns, dres, name, tm=512):
    T, D = x.shape
    n = gains.shape[0]
    tm = _tile(T, tm)

    def body(x_ref, g_ref, *rest):
        dxn_refs = rest[:n]
        dres_ref, dx_ref, dxb_ref, dg_ref = rest[n:]
        xv = x_ref[...]
        rstd = lax.rsqrt(jnp.mean(xv * xv, axis=-1, keepdims=True) + RMS_EPS)
        xhat = xv * rstd
        dx = dres_ref[...]

        @pl.when(pl.program_id(0) == 0)
        def _():
            dg_ref[...] = jnp.zeros_like(dg_ref)

        for i in range(n):
            dy = dxn_refs[i][...].astype(F32)
            dg_ref[i:i + 1, :] += jnp.sum(dy * xhat, axis=0, keepdims=True)
            dxh = dy * g_ref[i:i + 1, :]
            dx = dx + rstd * (dxh - xhat * jnp.mean(dxh * xhat, axis=-1, keepdims=True))
        dx_ref[...] = dx
        dxb_ref[...] = dx.astype(BF16)

    row = pl.BlockSpec((tm, D), lambda i: (i, 0))
    par = pl.BlockSpec((n, D), lambda i: (0, 0))
    return pl.pallas_call(
        body, name=name, grid=(T // tm,),
        in_specs=[row, par] + [row] * n + [row],
        out_specs=[row, row, par],
        out_shape=[jax.ShapeDtypeStruct((T, D), F32), jax.ShapeDtypeStruct((T, D), BF16),
                   jax.ShapeDtypeStruct((n, D), F32)],
        compiler_params=pltpu.CompilerParams(
            dimension_semantics=("arbitrary",), vmem_limit_bytes=_vmem((4 + n) * _nbytes((tm, D), F32))),
    )(x, gains, *dxns, dres)


def _final_loss_bwd(h, gain, target, name, tm=512):
    T, D = h.shape
    tm = _tile(T, tm)

    def body(h_ref, g_ref, t_ref, dh_ref, dhb_ref, dg_ref, sq_ref):
        xv = h_ref[...]
        rstd = lax.rsqrt(jnp.mean(xv * xv, axis=-1, keepdims=True) + RMS_EPS)
        xhat = xv * rstd
        err = xhat * g_ref[...] - t_ref[...]

        @pl.when(pl.program_id(0) == 0)
        def _():
            dg_ref[...] = jnp.zeros_like(dg_ref)
            sq_ref[...] = jnp.zeros_like(sq_ref)

        sq_ref[...] += jnp.sum(err * err, axis=0, keepdims=True)
        dy = err * (1.0 / D)
        dg_ref[...] += jnp.sum(dy * xhat, axis=0, keepdims=True)
        dxh = dy * g_ref[...]
        dh = rstd * (dxh - xhat * jnp.mean(dxh * xhat, axis=-1, keepdims=True))
        dh_ref[...] = dh
        dhb_ref[...] = dh.astype(BF16)

    row = pl.BlockSpec((tm, D), lambda i: (i, 0))
    par = pl.BlockSpec((1, D), lambda i: (0, 0))
    return pl.pallas_call(
        body, name=name, grid=(T // tm,),
        in_specs=[row, par, row], out_specs=[row, row, par, par],
        out_shape=[jax.ShapeDtypeStruct((T, D), F32), jax.ShapeDtypeStruct((T, D), BF16),
                   jax.ShapeDtypeStruct((1, D), F32), jax.ShapeDtypeStruct((1, D), F32)],
        compiler_params=pltpu.CompilerParams(
            dimension_semantics=("arbitrary",), vmem_limit_bytes=_vmem(5 * _nbytes((tm, D), F32))),
    )(h, gain, target)


def _halo_specs(S, ts, tc, col_off, prev, nxt):
    r = ts // HALO
    last = S // HALO - 1
    specs = []
    if prev:
        specs.append(pl.BlockSpec((None, HALO, tc), lambda j, b, s: (b, jnp.maximum(s * r - 1, 0), col_off + j)))
    specs.append(pl.BlockSpec((None, ts, tc), lambda j, b, s: (b, s, col_off + j)))
    if nxt:
        specs.append(pl.BlockSpec((None, HALO, tc), lambda j, b, s: (b, jnp.minimum((s + 1) * r, last), col_off + j)))
    return specs


def _ext(prev_ref, main_ref, next_ref, first, last):
    main = main_ref[...].astype(F32)
    zeros = jnp.zeros((HALO, main.shape[1]), F32)
    top = zeros if prev_ref is None else jnp.where(first, 0.0, prev_ref[...].astype(F32))
    bot = zeros if next_ref is None else jnp.where(last, 0.0, next_ref[...].astype(F32))
    return jnp.concatenate([top, main, bot], axis=0)


def _shift(xe, k):
    return pltpu.roll(xe, k % xe.shape[0], axis=0)


def _cconv(xe, w):
    return w[0:1, :] * _shift(xe, 2) + w[1:2, :] * _shift(xe, 1) + w[2:3, :] * xe


def _cconv_t(de, w):
    return w[2:3, :] * de + w[1:2, :] * _shift(de, -1) + w[0:1, :] * _shift(de, -2)


def _main(xe, ts):
    return xe[HALO:HALO + ts, :]


def _conv_wgrad(dw_ref, de, xe, ts):
    d = _main(de, ts)
    dw_ref[0:1, :] += jnp.sum(d * _main(_shift(xe, 2), ts), axis=0, keepdims=True)
    dw_ref[1:2, :] += jnp.sum(d * _main(_shift(xe, 1), ts), axis=0, keepdims=True)
    dw_ref[2:3, :] += jnp.sum(d * _main(xe, ts), axis=0, keepdims=True)


def _gate_tiles(S, C):
    return _tile(S, 512), _tile(C, 512)


def _shortconv_fwd(bch, conv_w, name):
    B, S, D3 = bch.shape
    D = D3 // 3
    ts, tc = _gate_tiles(S, D)
    nj, ns = D // tc, S // ts

    def body(b_ref, cp_ref, c_ref, hp_ref, h_ref, w_ref, o_ref):
        first = pl.program_id(2) == 0
        ce = _ext(cp_ref, c_ref, None, first, False)
        he = _ext(hp_ref, h_ref, None, first, False)
        cv = _main(_cconv(ce * he, w_ref[...]), ts)
        o_ref[...] = (b_ref[...].astype(F32) * cv).astype(BF16)

    in_specs = (_halo_specs(S, ts, tc, 0, False, False) + _halo_specs(S, ts, tc, nj, True, False)
                + _halo_specs(S, ts, tc, 2 * nj, True, False) + [pl.BlockSpec((3, tc), lambda j, b, s: (0, j))])
    return pl.pallas_call(
        body, name=name, grid=(nj, B, ns), in_specs=in_specs,
        out_specs=pl.BlockSpec((None, ts, tc), lambda j, b, s: (b, s, j)),
        out_shape=jax.ShapeDtypeStruct((B, S, D), BF16),
        compiler_params=pltpu.CompilerParams(
            dimension_semantics=("parallel", "parallel", "parallel"),
            vmem_limit_bytes=_vmem(12 * _nbytes((ts + 2 * HALO, tc), F32))),
    )(bch, bch, bch, bch, bch, conv_w)


def _shortconv_bwd(bch, dg, conv_w, name):
    B, S, D3 = bch.shape
    D = D3 // 3
    ts, tc = _gate_tiles(S, D)
    nj, ns = D // tc, S // ts

    def body(b_ref, bn_ref, cp_ref, c_ref, cn_ref, hp_ref, h_ref, hn_ref, d_ref, dn_ref, w_ref, o_ref, dw_ref):
        s = pl.program_id(2)
        first, last = s == 0, s == ns - 1

        @pl.when(jnp.logical_and(pl.program_id(1) == 0, s == 0))
        def _():
            dw_ref[...] = jnp.zeros_like(dw_ref)

        w = w_ref[...]
        be = _ext(None, b_ref, bn_ref, first, last)
        ce = _ext(cp_ref, c_ref, cn_ref, first, last)
        he = _ext(hp_ref, h_ref, hn_ref, first, last)
        de = _ext(None, d_ref, dn_ref, first, last)
        ch = ce * he
        dcv = de * be
        dch = _main(_cconv_t(dcv, w), ts)
        o_ref[0] = (_main(de, ts) * _main(_cconv(ch, w), ts)).astype(BF16)
        o_ref[1] = (dch * _main(he, ts)).astype(BF16)
        o_ref[2] = (dch * _main(ce, ts)).astype(BF16)
        _conv_wgrad(dw_ref, dcv, ch, ts)

    in_specs = (_halo_specs(S, ts, tc, 0, False, True) + _halo_specs(S, ts, tc, nj, True, True)
                + _halo_specs(S, ts, tc, 2 * nj, True, True) + _halo_specs(S, ts, tc, 0, False, True)
                + [pl.BlockSpec((3, tc), lambda j, b, s: (0, j))])
    return pl.pallas_call(
        body, name=name, grid=(nj, B, ns), in_specs=in_specs,
        out_specs=[pl.BlockSpec((3, None, ts, tc), lambda j, b, s: (0, b, s, j)),
                   pl.BlockSpec((3, tc), lambda j, b, s: (0, j))],
        out_shape=[jax.ShapeDtypeStruct((3, B, S, D), BF16), jax.ShapeDtypeStruct((3, D), F32)],
        compiler_params=pltpu.CompilerParams(
            dimension_semantics=("parallel", "arbitrary", "arbitrary"),
            vmem_limit_bytes=_vmem(24 * _nbytes((ts + 2 * HALO, tc), F32))),
    )(bch, bch, bch, bch, bch, bch, bch, bch, dg, dg, conv_w)


def _sigmoid(x):
    return 1.0 / (1.0 + jnp.exp(-x))


def _ffn_gate_fwd(u0, conv_w, conv_b, name):
    B, S, F2 = u0.shape
    F = F2 // 2
    ts, tc = _gate_tiles(S, F)
    nj, ns = F // tc, S // ts

    def body(gp_ref, g_ref, up_ref, u_ref, wg_ref, wu_ref, bg_ref, bu_ref, o_ref):
        first = pl.program_id(2) == 0
        ug = _main(_cconv(_ext(gp_ref, g_ref, None, first, False), wg_ref[...]), ts) + bg_ref[...]
        uu = _main(_cconv(_ext(up_ref, u_ref, None, first, False), wu_ref[...]), ts) + bu_ref[...]
        o_ref[...] = (ug * _sigmoid(ug) * uu).astype(BF16)

    in_specs = (_halo_specs(S, ts, tc, 0, True, False) + _halo_specs(S, ts, tc, nj, True, False)
                + [pl.BlockSpec((3, tc), lambda j, b, s: (0, j)), pl.BlockSpec((3, tc), lambda j, b, s: (0, nj + j)),
                   pl.BlockSpec((1, tc), lambda j, b, s: (0, j)), pl.BlockSpec((1, tc), lambda j, b, s: (0, nj + j))])
    return pl.pallas_call(
        body, name=name, grid=(nj, B, ns), in_specs=in_specs,
        out_specs=pl.BlockSpec((None, ts, tc), lambda j, b, s: (b, s, j)),
        out_shape=jax.ShapeDtypeStruct((B, S, F), BF16),
        compiler_params=pltpu.CompilerParams(
            dimension_semantics=("parallel", "parallel", "parallel"),
            vmem_limit_bytes=_vmem(12 * _nbytes((ts + 2 * HALO, tc), F32))),
    )(u0, u0, u0, u0, conv_w, conv_w, conv_b, conv_b)


def _ffn_gate_bwd(u0, dact, conv_w, conv_b, name):
    B, S, F2 = u0.shape
    F = F2 // 2
    ts, tc = _gate_tiles(S, F)
    nj, ns = F // tc, S // ts

    def body(gp_ref, g_ref, gn_ref, up_ref, u_ref, un_ref, d_ref, dn_ref, wg_ref, wu_ref, bg_ref, bu_ref,
             o_ref, dwg_ref, dwu_ref, dbg_ref, dbu_ref):
        s = pl.program_id(2)
        first, last = s == 0, s == ns - 1

        @pl.when(jnp.logical_and(pl.program_id(1) == 0, s == 0))
        def _():
            for r in (dwg_ref, dwu_ref, dbg_ref, dbu_ref):
                r[...] = jnp.zeros_like(r)

        wg, wu = wg_ref[...], wu_ref[...]
        ge = _ext(gp_ref, g_ref, gn_ref, first, last)
        ue = _ext(up_ref, u_ref, un_ref, first, last)
        de = _ext(None, d_ref, dn_ref, first, last)
        ug = _cconv(ge, wg) + bg_ref[...]
        uu = _cconv(ue, wu) + bu_ref[...]
        sig = _sigmoid(ug)
        dug = de * uu * (sig * (1.0 + ug * (1.0 - sig)))
        duu = de * (ug * sig)
        o_ref[0] = _main(_cconv_t(dug, wg), ts).astype(BF16)
        o_ref[1] = _main(_cconv_t(duu, wu), ts).astype(BF16)
        _conv_wgrad(dwg_ref, dug, ge, ts)
        _conv_wgrad(dwu_ref, duu, ue, ts)
        dbg_ref[...] += jnp.sum(_main(dug, ts), axis=0, keepdims=True)
        dbu_ref[...] += jnp.sum(_main(duu, ts), axis=0, keepdims=True)

    w3 = lambda off: pl.BlockSpec((3, tc), lambda j, b, s: (0, off + j))
    w1 = lambda off: pl.BlockSpec((1, tc), lambda j, b, s: (0, off + j))
    in_specs = (_halo_specs(S, ts, tc, 0, True, True) + _halo_specs(S, ts, tc, nj, True, True)
                + _halo_specs(S, ts, tc, 0, False, True) + [w3(0), w3(nj), w1(0), w1(nj)])
    outs = pl.pallas_call(
        body, name=name, grid=(nj, B, ns), in_specs=in_specs,
        out_specs=[pl.BlockSpec((2, None, ts, tc), lambda j, b, s: (0, b, s, j)), w3(0), w3(0), w1(0), w1(0)],
        out_shape=[jax.ShapeDtypeStruct((2, B, S, F), BF16), jax.ShapeDtypeStruct((3, F), F32),
                   jax.ShapeDtypeStruct((3, F), F32), jax.ShapeDtypeStruct((1, F), F32),
                   jax.ShapeDtypeStruct((1, F), F32)],
        compiler_params=pltpu.CompilerParams(
            dimension_semantics=("parallel", "arbitrary", "arbitrary"),
            vmem_limit_bytes=_vmem(30 * _nbytes((ts + 2 * HALO, tc), F32))),
    )(u0, u0, u0, u0, u0, u0, dact, dact, conv_w, conv_w, conv_b, conv_b)
    du0, dwg, dwu, dbg, dbu = outs
    return du0, jnp.concatenate([dwg, dwu], axis=1), jnp.concatenate([dbg, dbu], axis=1)


def _t5_bucket(dist):
    max_exact = REL_BUCKETS // 2
    n = jnp.maximum(dist, 0)
    nf = jnp.maximum(n, max_exact).astype(F32)
    large = max_exact + (jnp.log(nf / max_exact) / math.log(REL_MAX_DISTANCE / max_exact)
                         * (REL_BUCKETS - max_exact)).astype(jnp.int32)
    large = jnp.minimum(large, REL_BUCKETS - 1)
    return jnp.where(n < max_exact, n, large)


def _band_tables(window, dilation):
    P = ATT_BLOCK
    qi = jnp.arange(P, dtype=jnp.int32)[:, None]
    kc = jnp.arange(2 * P, dtype=jnp.int32)[None, :]
    delta = qi + P - kc
    band = (delta >= 0) & (delta <= window // dilation)
    bucket = _t5_bucket(delta * dilation).reshape(-1)
    onehot = (bucket[None, :] == jnp.arange(REL_BUCKETS, dtype=jnp.int32)[:, None]).astype(F32)
    return onehot, band


def _bias_lookup(rel_bias_t, onehots, name):
    nb, _, Q = onehots.shape
    H = rel_bias_t.shape[0]

    def body(r_ref, oh_ref, o_ref):
        o_ref[...] = lax.dot_general(r_ref[...], oh_ref[...], (((1,), (0,)), ((), ())),
                                     precision=lax.Precision.HIGHEST, preferred_element_type=F32)

    return pl.pallas_call(
        body, name=name, grid=(nb,),
        in_specs=[pl.BlockSpec((H, REL_BUCKETS), lambda i: (0, 0)),
                  pl.BlockSpec((None, REL_BUCKETS, Q), lambda i: (i, 0, 0))],
        out_specs=pl.BlockSpec((None, H, Q), lambda i: (i, 0, 0)),
        out_shape=jax.ShapeDtypeStruct((nb, H, Q), F32),
        compiler_params=pltpu.CompilerParams(dimension_semantics=("parallel",),
                                             vmem_limit_bytes=_vmem(_nbytes((REL_BUCKETS + H, Q), F32))),
    )(rel_bias_t, onehots)


def _bias_grad(dtabs, onehots, name):
    nb, H, Q = dtabs.shape

    def body(d_ref, oh_ref, o_ref):
        @pl.when(pl.program_id(0) == 0)
        def _():
            o_ref[...] = jnp.zeros_like(o_ref)

        o_ref[...] += lax.dot_general(d_ref[...], oh_ref[...], (((1,), (1,)), ((), ())),
                                      precision=lax.Precision.HIGHEST, preferred_element_type=F32)

    return pl.pallas_call(
        body, name=name, grid=(nb,),
        in_specs=[pl.BlockSpec((None, H, Q), lambda i: (i, 0, 0)),
                  pl.BlockSpec((None, REL_BUCKETS, Q), lambda i: (i, 0, 0))],
        out_specs=pl.BlockSpec((H, REL_BUCKETS), lambda i: (0, 0)),
        out_shape=jax.ShapeDtypeStruct((H, REL_BUCKETS), F32),
        compiler_params=pltpu.CompilerParams(dimension_semantics=("arbitrary",),
                                             vmem_limit_bytes=_vmem(_nbytes((REL_BUCKETS + H, Q), F32))),
    )(dtabs, onehots)


def _lane_masks():
    lane = lax.broadcasted_iota(jnp.int32, (1, LANE), 1)
    lo = lane < LANE // 2
    return lo, jnp.logical_not(lo)


def _dot_nt(a, b):
    return lax.dot_general(a, b, (((1,), (1,)), ((), ())), preferred_element_type=F32)


def _dot_nn(a, b):
    return lax.dot_general(a, b, (((1,), (0,)), ((), ())), preferred_element_type=F32)


def _dot_tn(a, b):
    return lax.dot_general(a, b, (((0,), (0,)), ((), ())), preferred_element_type=F32)


def _block_rows(n, dilation, S):
    P = ATT_BLOCK
    nb = S // (dilation * P)
    r, i = n // nb, n % nb
    cur = pl.ds(i * (P * dilation) + r, P, stride=dilation)
    prv = pl.ds(jnp.maximum(i - 1, 0) * (P * dilation) + r, P, stride=dilation)
    return cur, prv, jnp.minimum(i, 1)


def _attn_fwd(q, kv, tabs, name):
    B, S, D = q.shape
    P, H = ATT_BLOCK, N_HEADS
    scale = (D // H) ** -0.5
    half = LANE // 2
    nsteps = S // P // ATT_UNROLL

    def body(q_ref, k_ref, v_ref, tab_ref, o_ref, lse_ref, acc_ref, m_ref, l_ref):
        lo, hi = _lane_masks()
        for bi, (_, d) in enumerate(DILATED_BRANCHES):
            def block(n, carry, bi=bi, d=d):
                rows = [_block_rows(n + j * nsteps, d, S) for j in range(ATT_UNROLL)]
                loaded = []
                for cur, prv, variant in rows:
                    qb = q_ref[cur, :].astype(BF16)
                    kw = jnp.concatenate([k_ref[prv, :], k_ref[cur, :]], axis=0).astype(BF16)
                    vw = jnp.concatenate([v_ref[prv, :], v_ref[cur, :]], axis=0).astype(BF16)
                    old = (m_ref[cur, :], l_ref[cur, :], acc_ref[cur, :]) if bi > 0 else None
                    loaded.append((qb, kw, vw, old))
                results = []
                for (cur, prv, variant), (qb, kw, vw, old) in zip(rows, loaded):
                    m_pair = l_pair = a_pair = None
                    for e, msk in enumerate((lo, hi)):
                        s = _dot_nt(jnp.where(msk, qb, 0), kw) * scale + tab_ref[bi, variant, e]
                        m_e = jnp.max(s, axis=-1, keepdims=True)
                        if bi > 0:
                            m_o = old[0][:, e * half:e * half + 1]
                            m_e = jnp.maximum(m_o, m_e)
                            alpha = jnp.exp(m_o - m_e)
                        pe = jnp.exp(s - m_e)
                        l_e = jnp.sum(pe, axis=-1, keepdims=True)
                        a_e = _dot_nn(pe.astype(BF16), jnp.where(msk, vw, 0))
                        if bi > 0:
                            l_e = l_e + alpha * old[1][:, e * half:e * half + 1]
                            a_e = a_e + alpha * jnp.where(msk, old[2], 0.0)
                        m_pair = m_e if e == 0 else jnp.where(lo, m_pair, m_e)
                        l_pair = l_e if e == 0 else jnp.where(lo, l_pair, l_e)
                        a_pair = a_e if e == 0 else a_pair + a_e
                    results.append((m_pair, l_pair, a_pair))
                for (cur, prv, variant), (m_pair, l_pair, a_pair) in zip(rows, results):
                    m_ref[cur, :] = m_pair
                    l_ref[cur, :] = l_pair
                    acc_ref[cur, :] = a_pair
                return carry

            lax.fori_loop(0, nsteps, block, 0)
        l = l_ref[...]
        o_ref[...] = (acc_ref[...] / l).astype(BF16)
        lse_ref[...] = m_ref[...] + jnp.log(l)

    nl = D // LANE
    col = lambda off: pl.BlockSpec((None, S, LANE), lambda b, p: (b, 0, off + p))
    tab_spec = pl.BlockSpec((3, 2, 2, P, 2 * P), lambda b, p: (0, 0, p, 0, 0))
    return pl.pallas_call(
        body, name=name, grid=(B, H // 2), in_specs=[col(0), col(0), col(nl), tab_spec],
        out_specs=[col(0), col(0)],
        out_shape=[jax.ShapeDtypeStruct((B, S, D), BF16), jax.ShapeDtypeStruct((B, S, D), F32)],
        scratch_shapes=[pltpu.VMEM((S, LANE), F32)] * 3,
        compiler_params=pltpu.CompilerParams(
            dimension_semantics=("parallel", "parallel"),
            vmem_limit_bytes=_vmem(5 * _nbytes((S, LANE), F32) + _nbytes((12, P, 2 * P), F32))),
    )(q, kv, kv, tabs)


def _attn_bwd(q, kv, do, o, lse, tabs, name):
    B, S, D = q.shape
    P, H = ATT_BLOCK, N_HEADS
    scale = (D // H) ** -0.5
    half = LANE // 2
    nsteps = S // P // ATT_UNROLL

    def body(q_ref, k_ref, v_ref, do_ref, o_ref, lse_ref, tab_ref, dq_ref, dkv_ref, dtab_ref, delta_ref):
        lo, hi = _lane_masks()

        @pl.when(pl.program_id(1) == 0)
        def _():
            dtab_ref[...] = jnp.zeros_like(dtab_ref)

        dkv_ref[...] = jnp.zeros_like(dkv_ref)
        prod = do_ref[...] * o_ref[...].astype(F32)
        delta_ref[...] = jnp.where(lo, jnp.sum(jnp.where(lo, prod, 0.0), axis=-1, keepdims=True),
                                   jnp.sum(jnp.where(hi, prod, 0.0), axis=-1, keepdims=True))

        for bi, (_, d) in enumerate(DILATED_BRANCHES):
            def block(n, carry, bi=bi, d=d):
                rows = [_block_rows(n + j * nsteps, d, S) for j in range(ATT_UNROLL)]
                loaded = []
                for cur, prv, variant in rows:
                    qb = q_ref[cur, :].astype(BF16)
                    kw = jnp.concatenate([k_ref[prv, :], k_ref[cur, :]], axis=0).astype(BF16)
                    vw = jnp.concatenate([v_ref[prv, :], v_ref[cur, :]], axis=0).astype(BF16)
                    dob = do_ref[cur, :].astype(BF16)
                    dq_old = dq_ref[cur, :] if bi > 0 else None
                    loaded.append((qb, kw, vw, dob, lse_ref[cur, :], delta_ref[cur, :], dq_old))
                results = []
                for j, ((cur, prv, variant), (qb, kw, vw, dob, lse_b, dl_b, dq_old)) in enumerate(zip(rows, loaded)):
                    dq_pair = dk_pair = dv_pair = None
                    for e, msk in enumerate((lo, hi)):
                        qm = jnp.where(msk, qb, 0)
                        dom = jnp.where(msk, dob, 0)
                        s = _dot_nt(qm, kw) * scale + tab_ref[bi, variant, e]
                        pe = jnp.exp(s - lse_b[:, e * half:e * half + 1])
                        ds = pe * (_dot_nt(dom, vw) - dl_b[:, e * half:e * half + 1])
                        dtab_ref[bi, e] += ds
                        dsb = ds.astype(BF16)
                        dq_e = _dot_nn(dsb, jnp.where(msk, kw, 0))
                        dk_e = _dot_tn(dsb, qm)
                        dv_e = _dot_tn(pe.astype(BF16), dom)
                        dq_pair = dq_e if e == 0 else dq_pair + dq_e
                        dk_pair = dk_e if e == 0 else dk_pair + dk_e
                        dv_pair = dv_e if e == 0 else dv_pair + dv_e
                    dq_pair = dq_pair * scale
                    if bi > 0:
                        dq_pair = dq_pair + dq_old
                    results.append((dq_pair, dk_pair * scale, dv_pair))
                for (cur, prv, variant), (dq_pair, dk_pair, dv_pair) in zip(rows, results):
                    dq_ref[cur, :] = dq_pair
                    dkv_ref[0, cur, :] += dk_pair[P:, :]
                    dkv_ref[1, cur, :] += dv_pair[P:, :]
                    dkv_ref[0, prv, :] += dk_pair[:P, :]
                    dkv_ref[1, prv, :] += dv_pair[:P, :]
                return carry

            lax.fori_loop(0, nsteps, block, 0)

    nl = D // LANE
    col = lambda off: pl.BlockSpec((None, S, LANE), lambda p, b: (b, 0, off + p))
    tab_spec = pl.BlockSpec((3, 2, 2, P, 2 * P), lambda p, b: (0, 0, p, 0, 0))
    return pl.pallas_call(
        body, name=name, grid=(H // 2, B),
        in_specs=[col(0), col(0), col(nl), col(0), col(0), col(0), tab_spec],
        out_specs=[col(0), pl.BlockSpec((2, None, S, LANE), lambda p, b: (0, b, 0, p)),
                   pl.BlockSpec((None, 3, 2, P, 2 * P), lambda p, b: (p, 0, 0, 0, 0))],
        out_shape=[jax.ShapeDtypeStruct((B, S, D), F32), jax.ShapeDtypeStruct((2, B, S, D), F32),
                   jax.ShapeDtypeStruct((H // 2, 3, 2, P, 2 * P), F32)],
        scratch_shapes=[pltpu.VMEM((S, LANE), F32)],
        compiler_params=pltpu.CompilerParams(
            dimension_semantics=("parallel", "arbitrary"),
            vmem_limit_bytes=VMEM_CAP),
    )(q, kv, kv, do, o, lse, tabs)


def _attn_fwd_branch(q, kv, tab, dilation, name):
    B, S, D = q.shape
    P, d = ATT_BLOCK, dilation
    L = S // d
    nb = L // P
    H = N_HEADS
    scale = (D // H) ** -0.5
    qv = q.reshape(B, L, d * D)
    kvv = kv.reshape(B, L, d * 2 * D)

    def body(q_ref, kp_ref, kc_ref, vp_ref, vc_ref, tab_ref, o_ref, lse_ref):
        lo, hi = _lane_masks()
        for p in range(H // 2):
            sl = slice(p * LANE, (p + 1) * LANE)
            qp = q_ref[:, sl]
            kp = jnp.concatenate([kp_ref[:, sl], kc_ref[:, sl]], axis=0)
            vp = jnp.concatenate([vp_ref[:, sl], vc_ref[:, sl]], axis=0)
            o_pair = None
            lse_pair = None
            for e, msk in enumerate((lo, hi)):
                s = _dot_nt(jnp.where(msk, qp, 0), kp) * scale + tab_ref[2 * p + e]
                m = jnp.max(s, axis=-1, keepdims=True)
                pe = jnp.exp(s - m)
                den = jnp.sum(pe, axis=-1, keepdims=True)
                o_e = _dot_nn(pe.astype(BF16), jnp.where(msk, vp, 0)) / den
                lse_e = m + jnp.log(den)
                o_pair = o_e if e == 0 else o_pair + o_e
                lse_pair = lse_e if e == 0 else jnp.where(lo, lse_pair, lse_e)
            o_ref[:, sl] = o_pair
            lse_ref[:, sl] = lse_pair

    blk = lambda f: pl.BlockSpec((None, P, D), f)
    prev = lambda i: jnp.maximum(i - 1, 0)
    in_specs = [blk(lambda b, r, i: (b, i, r)),
                blk(lambda b, r, i: (b, prev(i), 2 * r)), blk(lambda b, r, i: (b, i, 2 * r)),
                blk(lambda b, r, i: (b, prev(i), 2 * r + 1)), blk(lambda b, r, i: (b, i, 2 * r + 1)),
                pl.BlockSpec((None, H, P, 2 * P), lambda b, r, i: (jnp.minimum(i, 1), 0, 0, 0))]
    o, lse = pl.pallas_call(
        body, name=name, grid=(B, d, nb), in_specs=in_specs,
        out_specs=[blk(lambda b, r, i: (b, i, r))] * 2,
        out_shape=[jax.ShapeDtypeStruct((B, L, d * D), F32)] * 2,
        compiler_params=pltpu.CompilerParams(
            dimension_semantics=("parallel", "parallel", "parallel"),
            vmem_limit_bytes=_vmem(_nbytes((H, P, 2 * P), F32) + 9 * _nbytes((P, D), F32))),
    )(qv, kvv, kvv, kvv, kvv, tab)
    return o.reshape(B * S, D), lse.reshape(B * S, D)


def _attn_merge(os_, lses, name, tm=512):
    T, D = os_[0].shape
    n = len(os_)
    tm = _tile(T, tm)

    def body(*refs):
        o_refs, l_refs = refs[:n], refs[n:2 * n]
        out_ref, lse_ref = refs[2 * n:]
        ls = [r[...] for r in l_refs]
        m = functools.reduce(jnp.maximum, ls)
        ws = [jnp.exp(l - m) for l in ls]
        tot = functools.reduce(jnp.add, ws)
        acc = functools.reduce(jnp.add, [w * r[...] for w, r in zip(ws, o_refs)])
        out_ref[...] = (acc / tot).astype(BF16)
        lse_ref[...] = m + jnp.log(tot)

    row = pl.BlockSpec((tm, D), lambda i: (i, 0))
    return pl.pallas_call(
        body, name=name, grid=(T // tm,), in_specs=[row] * (2 * n), out_specs=[row, row],
        out_shape=[jax.ShapeDtypeStruct((T, D), BF16), jax.ShapeDtypeStruct((T, D), F32)],
        compiler_params=pltpu.CompilerParams(
            dimension_semantics=("parallel",), vmem_limit_bytes=_vmem((2 * n + 4) * _nbytes((tm, D), F32))),
    )(*os_, *lses)


def _attn_bwd_q(q, kv, do, o, lse, tab, dq_acc, dilation, name):
    B, S, D = q.shape
    P, d = ATT_BLOCK, dilation
    L = S // d
    nb = L // P
    H = N_HEADS
    scale = (D // H) ** -0.5
    view = lambda t, c: t.reshape(B, L, d * c)
    has_acc = dq_acc is not None

    def body(q_ref, kp_ref, kc_ref, vp_ref, vc_ref, do_ref, o_ref, lse_ref, tab_ref, *rest):
        if has_acc:
            acc_ref, dq_ref, dtab_ref = rest
        else:
            dq_ref, dtab_ref = rest

        @pl.when((pl.program_id(0) == 0) & (pl.program_id(1) == 0) & (pl.program_id(2) == 0))
        def _():
            dtab_ref[...] = jnp.zeros_like(dtab_ref)

        lo, hi = _lane_masks()
        for p in range(H // 2):
            sl = slice(p * LANE, (p + 1) * LANE)
            qp = q_ref[:, sl]
            kp = jnp.concatenate([kp_ref[:, sl], kc_ref[:, sl]], axis=0)
            vp = jnp.concatenate([vp_ref[:, sl], vc_ref[:, sl]], axis=0)
            dop = do_ref[:, sl]
            prod = dop.astype(F32) * o_ref[:, sl].astype(F32)
            lsep = lse_ref[:, sl]
            dq_pair = None
            for e, msk in enumerate((lo, hi)):
                s = _dot_nt(jnp.where(msk, qp, 0), kp) * scale + tab_ref[2 * p + e]
                pe = jnp.exp(s - lsep[:, e * (LANE // 2):e * (LANE // 2) + 1])
                dp = _dot_nt(jnp.where(msk, dop, 0).astype(BF16), vp)
                delta = jnp.sum(jnp.where(msk, prod, 0.0), axis=-1, keepdims=True)
                ds = pe * (dp - delta)
                dtab_ref[2 * p + e] += ds
                dq_e = _dot_nn(ds.astype(BF16), jnp.where(msk, kp, 0))
                dq_pair = dq_e if e == 0 else dq_pair + dq_e
            dq_pair = dq_pair * scale
            if has_acc:
                dq_pair = dq_pair + acc_ref[:, sl]
            dq_ref[:, sl] = dq_pair

    blk = lambda f: pl.BlockSpec((None, P, D), f)
    prev = lambda i: jnp.maximum(i - 1, 0)
    own = blk(lambda b, r, i: (b, i, r))
    tab_spec = pl.BlockSpec((None, H, P, 2 * P), lambda b, r, i: (jnp.minimum(i, 1), 0, 0, 0))
    in_specs = [own,
                blk(lambda b, r, i: (b, prev(i), 2 * r)), blk(lambda b, r, i: (b, i, 2 * r)),
                blk(lambda b, r, i: (b, prev(i), 2 * r + 1)), blk(lambda b, r, i: (b, i, 2 * r + 1)),
                own, own, own, tab_spec]
    kvv = view(kv, 2 * D)
    args = [view(q, D), kvv, kvv, kvv, kvv, view(do, D), view(o, D), view(lse, D), tab]
    if has_acc:
        in_specs.append(own)
        args.append(view(dq_acc, D))
    dq, dtab = pl.pallas_call(
        body, name=name, grid=(B, d, nb), in_specs=in_specs,
        out_specs=[own, pl.BlockSpec((H, P, 2 * P), lambda b, r, i: (0, 0, 0))],
        out_shape=[jax.ShapeDtypeStruct((B, L, d * D), F32), jax.ShapeDtypeStruct((H, P, 2 * P), F32)],
        compiler_params=pltpu.CompilerParams(
            dimension_semantics=("arbitrary", "arbitrary", "arbitrary"),
            vmem_limit_bytes=_vmem(2 * _nbytes((H, P, 2 * P), F32) + 12 * _nbytes((P, D), F32))),
    )(*args)
    return dq.reshape(B, S, D), dtab


def _attn_bwd_kv(q, kv, do, o, lse, tabk, dkv_acc, dilation, name):
    B, S, D = q.shape
    P, d = ATT_BLOCK, dilation
    L = S // d
    nb = L // P
    H = N_HEADS
    scale = (D // H) ** -0.5
    view = lambda t, c: t.reshape(B, L, d * c)
    has_acc = dkv_acc is not None

    def body(k_ref, v_ref, qa_ref, qb_ref, doa_ref, dob_ref, oa_ref, ob_ref, la_ref, lb_ref, tab_ref, *rest):
        if has_acc:
            acc_ref, dkv_ref = rest
        else:
            (dkv_ref,) = rest
        lo, hi = _lane_masks()
        for p in range(H // 2):
            sl = slice(p * LANE, (p + 1) * LANE)
            kp, vp = k_ref[:, sl], v_ref[:, sl]
            q2 = jnp.concatenate([qa_ref[:, sl], qb_ref[:, sl]], axis=0)
            do2 = jnp.concatenate([doa_ref[:, sl], dob_ref[:, sl]], axis=0)
            o2 = jnp.concatenate([oa_ref[:, sl], ob_ref[:, sl]], axis=0)
            lse2 = jnp.concatenate([la_ref[:, sl], lb_ref[:, sl]], axis=0)
            prod = do2.astype(F32) * o2.astype(F32)
            dk_pair = None
            dv_pair = None
            for e, msk in enumerate((lo, hi)):
                qm = jnp.where(msk, q2, 0)
                dom = jnp.where(msk, do2, 0).astype(BF16)
                s = _dot_nt(qm, kp) * scale + tab_ref[2 * p + e]
                pe = jnp.exp(s - lse2[:, e * (LANE // 2):e * (LANE // 2) + 1])
                dp = _dot_nt(dom, vp)
                delta = jnp.sum(jnp.where(msk, prod, 0.0), axis=-1, keepdims=True)
                ds = pe * (dp - delta)
                dv_e = _dot_tn(pe.astype(BF16), dom)
                dk_e = _dot_tn(ds.astype(BF16), qm)
                dk_pair = dk_e if e == 0 else dk_pair + dk_e
                dv_pair = dv_e if e == 0 else dv_pair + dv_e
            dk_pair = dk_pair * scale
            if has_acc:
                dk_pair = dk_pair + acc_ref[0, :, sl]
                dv_pair = dv_pair + acc_ref[1, :, sl]
            dkv_ref[0, :, sl] = dk_pair
            dkv_ref[1, :, sl] = dv_pair

    blk = lambda f: pl.BlockSpec((None, P, D), f)
    nxt = lambda i: jnp.minimum(i + 1, nb - 1)
    own = blk(lambda b, r, i: (b, i, r))
    nx = blk(lambda b, r, i: (b, nxt(i), r))
    tab_spec = pl.BlockSpec((None, H, 2 * P, P), lambda b, r, i: (jnp.where(i == nb - 1, 1, 0), 0, 0, 0))
    pair = pl.BlockSpec((2, None, P, D), lambda b, r, i: (0, b, i, r))
    in_specs = [blk(lambda b, r, i: (b, i, 2 * r)), blk(lambda b, r, i: (b, i, 2 * r + 1)),
                own, nx, own, nx, own, nx, own, nx, tab_spec]
    kvv, qv, dov, ov, lv = view(kv, 2 * D), view(q, D), view(do, D), view(o, D), view(lse, D)
    args = [kvv, kvv, qv, qv, dov, dov, ov, ov, lv, lv, tabk]
    if has_acc:
        in_specs.append(pair)
        args.append(dkv_acc.reshape(2, B, L, d * D))
    dkv = pl.pallas_call(
        body, name=name, grid=(B, d, nb), in_specs=in_specs, out_specs=pair,
        out_shape=jax.ShapeDtypeStruct((2, B, L, d * D), F32),
        compiler_params=pltpu.CompilerParams(
            dimension_semantics=("parallel", "parallel", "parallel"),
            vmem_limit_bytes=_vmem(_nbytes((H, P, 2 * P), F32) + 16 * _nbytes((P, D), F32))),
    )(*args)
    return dkv.reshape(2, B, S, D)


def _mesh_place():
    x, y, c = lax.axis_index("x"), lax.axis_index("y"), lax.axis_index("c")
    return x, y, c, (x, y, 1 - c), [(1 - x, y), (x, 1 - y), (1 - x, 1 - y)]


def _rdma(src, dst, send_sems, recv_sems, idx, to):
    return pltpu.make_async_remote_copy(
        src_ref=src, dst_ref=dst, send_sem=send_sems.at[idx], recv_sem=recv_sems.at[idx],
        device_id=to, device_id_type=pl.DeviceIdType.MESH)


def _comm_call(body, name, args, out_shape, n_sems, n_local):
    any_spec = pl.BlockSpec(memory_space=pl.ANY)
    return pl.pallas_call(
        body, name=name, in_specs=[any_spec] * len(args), out_specs=[any_spec] * len(out_shape),
        out_shape=out_shape,
        scratch_shapes=[pltpu.SemaphoreType.DMA(n_sems), pltpu.SemaphoreType.DMA(n_sems),
                        pltpu.SemaphoreType.DMA((n_local,))],
        compiler_params=pltpu.CompilerParams(has_side_effects=True),
    )(*args)


def _all_gather(arrays, name):
    n = len(arrays)

    def body(*refs):
        ins, outs = refs[:n], refs[n:2 * n]
        send_sems, recv_sems, loc_sems = refs[2 * n:]
        x, y, c, sib, chips = _mesh_place()
        me = (x, y, c)

        def rows(a, dev):
            return outs[a].at[4 * dev[0] + 2 * dev[1] + dev[2]]

        def copy(a, k, block, to, src=None):
            return _rdma(rows(a, block) if src is None else src, rows(a, block), send_sems, recv_sems, (a, k), to)

        local, first, passed = [], [], []
        for a in range(n):
            cp = pltpu.make_async_copy(ins[a], rows(a, me), loc_sems.at[a])
            cp.start()
            local.append(cp)
            first.append(copy(a, 0, me, sib, src=ins[a]))
            first += [copy(a, 1 + j, me, (*chip, c), src=ins[a]) for j, chip in enumerate(chips)]
        for cp in first:
            cp.start()
        for j, chip in enumerate(chips):
            for a in range(n):
                copy(a, 1 + j, (*chip, c), me).wait_recv()
                cp = copy(a, 4 + j, (*chip, c), sib)
                cp.start()
                passed.append(cp)
        for a in range(n):
            copy(a, 0, sib, me).wait_recv()
            for j, chip in enumerate(chips):
                copy(a, 4 + j, (*chip, 1 - c), me).wait_recv()
        for cp in first + passed:
            cp.wait_send()
        for cp in local:
            cp.wait()

    out_shape = [jax.ShapeDtypeStruct((N_DEV,) + g.shape, g.dtype) for g in arrays]
    return list(_comm_call(body, name, arrays, out_shape, (n, N_DEV - 1), n))


def _core_pair_exchange(halves, gathered, name):
    nh, ng = len(halves), len(gathered)

    def body(*refs):
        h_in, g_in = refs[:nh], refs[nh:nh + ng]
        outs = refs[nh + ng:nh + ng + nh + ng]
        theirs, g_out = outs[:nh], outs[nh:]
        send_sems, recv_sems, loc_sems = refs[nh + ng + nh + ng:]
        x, y, c, sib, chips = _mesh_place()
        me = 4 * x + 2 * y + c
        peers = [sib] + [(*chip, pc) for chip in chips for pc in (c, 1 - c)]
        nchip = N_DEV // 2
        gbase = nh * nchip
        local, sends = [], []
        for a in range(nh):
            for ch in range(nchip):
                cp = _rdma(h_in[a].at[ch, 1 - c], theirs[a].at[ch], send_sems, recv_sems, a * nchip + ch, sib)
                cp.start()
                sends.append(cp)
        for g in range(ng):
            cp = pltpu.make_async_copy(g_in[g], g_out[g].at[me], loc_sems.at[g])
            cp.start()
            local.append(cp)
            for k, peer in enumerate(peers):
                cp = _rdma(g_in[g], g_out[g].at[me], send_sems, recv_sems, gbase + g * (N_DEV - 1) + k, peer)
                cp.start()
                sends.append(cp)
        for cp in sends:
            cp.wait_send()
        for a in range(nh):
            for ch in range(nchip):
                _rdma(h_in[a].at[ch, 1 - c], theirs[a].at[ch], send_sems, recv_sems, a * nchip + ch, sib).wait_recv()
        for g in range(ng):
            for k, peer in enumerate(peers):
                pid = 4 * peer[0] + 2 * peer[1] + peer[2]
                _rdma(g_in[g], g_out[g].at[pid], send_sems, recv_sems, gbase + g * (N_DEV - 1) + k, peer).wait_recv()
        for cp in local:
            cp.wait()

    half = [jax.ShapeDtypeStruct((h.shape[0],) + h.shape[2:], h.dtype) for h in halves]
    out_shape = half + [jax.ShapeDtypeStruct((N_DEV,) + g.shape, g.dtype) for g in gathered]
    outs = _comm_call(body, name, list(halves) + list(gathered), out_shape,
                      (nh * (N_DEV // 2) + ng * (N_DEV - 1),), max(ng, 1))
    return list(outs[:nh]), list(outs[nh:])


def _chip_exchange(arrays, name):
    n = len(arrays)

    def body(*refs):
        ins, outs = refs[:n], refs[n:2 * n]
        send_sems, recv_sems, _ = refs[2 * n:]
        x, y, c, sib, chips = _mesh_place()
        sends = []
        for a in range(n):
            for j, chip in enumerate(chips):
                cp = _rdma(ins[a].at[2 * chip[0] + chip[1]], outs[a].at[j], send_sems, recv_sems, (a, j), (*chip, c))
                cp.start()
                sends.append(cp)
        for cp in sends:
            cp.wait_send()
        for a in range(n):
            for j, chip in enumerate(chips):
                _rdma(ins[a].at[j], outs[a].at[j], send_sems, recv_sems, (a, j), (*chip, c)).wait_recv()

    out_shape = [jax.ShapeDtypeStruct((3,) + g.shape[1:], g.dtype) for g in arrays]
    return list(_comm_call(body, name, arrays, out_shape, (n, 3), 1))


def _pair_sum(halves, theirs, core, name, tr=256):
    nchip, _, R, C = halves.shape
    tr = _row_tile(R, tr)

    def body(core_ref, h_ref, t_ref, o_ref):
        o_ref[...] = (h_ref[...] + t_ref[...]).astype(BF16)

    grid_spec = pltpu.PrefetchScalarGridSpec(
        num_scalar_prefetch=1, grid=(nchip, R // tr),
        in_specs=[pl.BlockSpec((None, None, tr, C), lambda ch, i, core_ref: (ch, core_ref[0], i, 0)),
                  pl.BlockSpec((None, tr, C), lambda ch, i, core_ref: (ch, i, 0))],
        out_specs=pl.BlockSpec((None, tr, C), lambda ch, i, core_ref: (ch, i, 0)))
    return pl.pallas_call(
        body, name=name, grid_spec=grid_spec, out_shape=jax.ShapeDtypeStruct((nchip, R, C), BF16),
        compiler_params=pltpu.CompilerParams(
            dimension_semantics=("parallel", "parallel"), vmem_limit_bytes=_vmem(3 * _nbytes((tr, C), F32))),
    )(core, halves, theirs)


def _chip_sum(own, recv, chip, name, tr=256):
    _, R, C = own.shape
    tr = _row_tile(R, tr)

    def body(chip_ref, o_ref, r_ref, out_ref):
        acc = o_ref[...].astype(F32)
        for j in range(3):
            acc = acc + r_ref[j].astype(F32)
        out_ref[...] = acc

    grid_spec = pltpu.PrefetchScalarGridSpec(
        num_scalar_prefetch=1, grid=(R // tr,),
        in_specs=[pl.BlockSpec((None, tr, C), lambda i, chip_ref: (chip_ref[0], i, 0)),
                  pl.BlockSpec((3, tr, C), lambda i, chip_ref: (0, i, 0))],
        out_specs=pl.BlockSpec((tr, C), lambda i, chip_ref: (i, 0)))
    return pl.pallas_call(
        body, name=name, grid_spec=grid_spec, out_shape=jax.ShapeDtypeStruct((R, C), F32),
        compiler_params=pltpu.CompilerParams(
            dimension_semantics=("parallel",), vmem_limit_bytes=_vmem(4 * _nbytes((tr, C), F32))),
    )(chip, own, recv)


def _row_tile(rows, target):
    best = rows
    for t in range(8, min(rows, target) + 1, 8):
        if rows % t == 0:
            best = t
    return best


def _sum_slots(recv, name, tr=128):
    n, R, C = recv.shape
    tr = _row_tile(R, tr if recv.dtype == F32 else 2 * tr)

    def body(r_ref, o_ref):
        acc = r_ref[0].astype(F32)
        for k in range(1, n):
            acc = acc + r_ref[k].astype(F32)
        o_ref[...] = acc

    return pl.pallas_call(
        body, name=name, grid=(R // tr,),
        in_specs=[pl.BlockSpec((n, tr, C), lambda i: (0, i, 0))],
        out_specs=pl.BlockSpec((tr, C), lambda i: (i, 0)),
        out_shape=jax.ShapeDtypeStruct((R, C), F32),
        compiler_params=pltpu.CompilerParams(
            dimension_semantics=("parallel",), vmem_limit_bytes=_vmem(10 * _nbytes((tr, C), F32))),
    )(recv)


def _adamw(g, w, m, v, name, tr=256):
    R, C = g.shape
    tr = _row_tile(R, tr)

    def body(g_ref, w_ref, m_ref, v_ref, d_ref, nm_ref, nv_ref):
        gv = g_ref[...]
        nm = ADAM_B1 * m_ref[...] + (1.0 - ADAM_B1) * gv
        nv = ADAM_B2 * v_ref[...] + (1.0 - ADAM_B2) * (gv * gv)
        m_hat = nm / (1.0 - ADAM_B1 ** ADAM_STEP)
        v_hat = nv / (1.0 - ADAM_B2 ** ADAM_STEP)
        d_ref[...] = -ADAM_LR * (m_hat / (jnp.sqrt(v_hat) + ADAM_EPS) + ADAM_WD * w_ref[...])
        nm_ref[...] = nm
        nv_ref[...] = nv

    row = pl.BlockSpec((tr, C), lambda i: (i, 0))
    return pl.pallas_call(
        body, name=name, grid=(R // tr,), in_specs=[row] * 4, out_specs=[row] * 3,
        out_shape=[jax.ShapeDtypeStruct((R, C), F32)] * 3,
        compiler_params=pltpu.CompilerParams(
            dimension_semantics=("parallel",), vmem_limit_bytes=_vmem(8 * _nbytes((tr, C), F32))),
    )(g, w, m, v)


def _pack_rows(parts):
    flat = jnp.concatenate([p.reshape(-1).astype(F32) for p in parts])
    rows = -(-flat.shape[0] // (8 * LANE)) * 8
    return jnp.pad(flat, (0, rows * LANE - flat.shape[0])).reshape(rows, LANE)


def _unpack(flat, shapes):
    out, off = [], 0
    for shp in shapes:
        n = math.prod(shp)
        out.append(flat[off:off + n].reshape(shp))
        off += n
    return out


def kernel(x, a_norm, a_w_in, a_conv, a_w_out, kv_norm, w_kv, b_norm, b_w_q, b_w_o, rel_bias, ffn_norm, ffn_w_up, ffn_conv, ffn_conv_b, ffn_w_down, final_norm, loss_target, m_a_norm, m_a_w_in, m_a_conv, m_a_w_out, m_kv_norm, m_w_kv, m_b_norm, m_b_w_q, m_b_w_o, m_rel_bias, m_ffn_norm, m_ffn_w_up, m_ffn_conv, m_ffn_conv_b, m_ffn_w_down, m_final_norm, v_a_norm, v_a_w_in, v_a_conv, v_a_w_out, v_kv_norm, v_w_kv, v_b_norm, v_b_w_q, v_b_w_o, v_rel_bias, v_ffn_norm, v_ffn_w_up, v_ffn_conv, v_ffn_conv_b, v_ffn_w_down, v_final_norm):
    B, S, D = x.shape
    T = B * S
    F = ffn_w_down.shape[1] * N_DEV
    me = 4 * lax.axis_index("x") + 2 * lax.axis_index("y") + lax.axis_index("c")

    big_shards = [a_w_in[0].T, a_w_out[0], w_kv.T, b_w_q[0], b_w_o[0],
                  ffn_w_up[0].T, ffn_w_up[1].T, ffn_w_down[0], ffn_w_down[1]]
    small_shapes = [a_norm.shape, a_conv.shape, ffn_conv.shape]
    small_pack = _pack_rows([a_norm, a_conv, ffn_conv])
    gathered = _all_gather([s.astype(BF16) for s in big_shards] + [small_pack], "gather_weights")
    win_t, wout, wkv_t, wq, wo, wup0_t, wup1_t, wdn0, wdn1 = [
        g.reshape(-1, D) for g in gathered[:9]]
    wup_t, wdn = (wup0_t, wup1_t), (wdn0, wdn1)
    smalls = [_unpack(gathered[9][j].reshape(-1), small_shapes) for j in range(N_DEV)]
    a_norm_f = jnp.concatenate([s[0] for s in smalls], axis=-1)
    a_conv_f = jnp.concatenate([s[1] for s in smalls], axis=-1)[0]
    ffn_conv_f = jnp.concatenate([s[2] for s in smalls], axis=-1)

    x2 = x.reshape(T, D)
    (xn0,) = _rmsnorm_fwd(x2, a_norm_f, "a_norm_fwd")
    bch = _matmul(xn0, win_t, mode="nt", out_dtype=BF16, name="a_in_proj").reshape(B, S, 3 * D)
    gated = _shortconv_fwd(bch, a_conv_f, "a_gate_fwd").reshape(T, D)
    h1 = _matmul(gated, wout, mode="nn", out_dtype=F32, add=x2, name="a_out_proj")

    def ffn_fwd(h, l):
        (xn,) = _rmsnorm_fwd(h, ffn_norm[l:l + 1], f"ffn{l}_norm_fwd")
        u0 = _matmul(xn, wup_t[l], mode="nt", out_dtype=BF16, name=f"ffn{l}_up").reshape(B, S, 2 * F)
        act = _ffn_gate_fwd(u0, ffn_conv_f[l], ffn_conv_b[l:l + 1], f"ffn{l}_gate_fwd").reshape(T, F)
        out = _matmul(act, wdn[l], mode="nn", out_dtype=F32, add=h, name=f"ffn{l}_down", tk=2816)
        return out, (xn, u0, act)

    h2, ffn0_saved = ffn_fwd(h1, 0)
    kvn, xnb = _rmsnorm_fwd(h2, jnp.stack([kv_norm, b_norm[0]]), "kv_b_norm_fwd")
    kv = _matmul(kvn, wkv_t, mode="nt", out_dtype=F32, name="kv_proj").reshape(B, S, 2 * D)
    q = _matmul(xnb, wq, mode="nn", out_dtype=F32, name="q_proj").reshape(B, S, D)

    tables = [_band_tables(w, d) for (w, d) in DILATED_BRANCHES]
    onehots = jnp.stack([t[0] for t in tables])
    P = ATT_BLOCK
    bias_vals = _bias_lookup(rel_bias.T, onehots, "rel_bias_lookup").reshape(3, N_HEADS, P, 2 * P)
    in_cur = (jnp.arange(2 * P) >= P)[None, :]
    tabs = []
    for bi in range(3):
        band = tables[bi][1]
        gen = jnp.where(band[None], bias_vals[bi], NEG)
        fst = jnp.where((band & in_cur)[None], bias_vals[bi], NEG)
        tabs.append(jnp.stack([fst, gen]))
    tabs = jnp.stack(tabs)

    attn3, lse3 = _attn_fwd(q, kv, tabs, "attn_fwd")
    attn = attn3.reshape(T, D)
    h3 = _matmul(attn, wo, mode="nn", out_dtype=F32, add=h2, name="o_proj")
    h4, ffn1_saved = ffn_fwd(h3, 1)

    dh4, dh4b, d_final_norm, sq = _final_loss_bwd(h4, final_norm.reshape(1, D), loss_target.reshape(T, D), "loss_bwd")
    loss_part = 0.5 * jnp.sum(sq) / D

    def ffn_bwd(dh_out, dh_out_b, h_in, saved, l):
        xn, u0, act = saved
        dact = _matmul(dh_out_b, wdn[l], mode="nt", out_dtype=BF16, name=f"ffn{l}_down_dx", tn=1408).reshape(B, S, F)
        d_wdn = _matmul(act, dh_out_b, mode="tn", out_dtype=F32, name=f"ffn{l}_down_dw", tm=1408, tn=1024, tk=2048)
        du0, d_conv, d_conv_b = _ffn_gate_bwd(u0, dact, ffn_conv_f[l], ffn_conv_b[l:l + 1], f"ffn{l}_gate_bwd")
        du0 = du0.reshape(2, T, F)
        dxn = _matmul(du0, wup_t[l], mode="nn", out_dtype=F32, a_parts=2, name=f"ffn{l}_up_dx", tk=2816)
        d_wup_t = _matmul(du0, xn, mode="tn", out_dtype=F32, a_parts=2, name=f"ffn{l}_up_dw", tm=1408, tn=1024, tk=2048)
        dh_in, dh_in_b, d_norm = _rmsnorm_bwd(h_in, ffn_norm[l:l + 1], [dxn], dh_out, f"ffn{l}_norm_bwd")
        return (dh_in, dh_in_b), (d_wdn, d_wup_t, d_conv, d_conv_b, d_norm)

    (dh3, dh3b), ffn1_grads = ffn_bwd(dh4, dh4b, h3, ffn1_saved, 1)

    dattn = _matmul(dh3b, wo, mode="nt", out_dtype=F32, name="o_proj_dx").reshape(B, S, D)
    d_wo = _matmul(attn, dh3b, mode="tn", out_dtype=F32, name="o_proj_dw", tm=1024, tn=1024, tk=2048)
    dq, dkv, dtab = _attn_bwd(q, kv, dattn, attn3, lse3, tabs, "attn_bwd")
    dtabs = dtab.transpose(1, 0, 2, 3, 4).reshape(3, N_HEADS, P * 2 * P)
    d_rel_bias = _bias_grad(dtabs, onehots, "rel_bias_grad").T
    dq2 = dq.reshape(T, D)
    dkv2 = dkv.reshape(2, T, D)
    dxnb = _matmul(dq2, wq, mode="nt", out_dtype=F32, name="q_proj_dx")
    d_wq = _matmul(xnb, dq2, mode="tn", out_dtype=F32, name="q_proj_dw", tm=1024, tn=1024, tk=2048)
    dkvn = _matmul(dkv2, wkv_t, mode="nn", out_dtype=F32, a_parts=2, name="kv_proj_dx")
    d_wkv_t = _matmul(dkv2, kvn, mode="tn", out_dtype=F32, a_parts=2, name="kv_proj_dw", tm=1024, tn=1024, tk=2048)
    dh2, dh2b, d_kvb_norm = _rmsnorm_bwd(h2, jnp.stack([kv_norm, b_norm[0]]), [dkvn, dxnb], dh3, "kv_b_norm_bwd")

    (dh1, dh1b), ffn0_grads = ffn_bwd(dh2, dh2b, h1, ffn0_saved, 0)

    dgated = _matmul(dh1b, wout, mode="nt", out_dtype=BF16, name="a_out_proj_dx").reshape(B, S, D)
    d_wout = _matmul(gated, dh1b, mode="tn", out_dtype=F32, name="a_out_proj_dw", tm=1024, tn=1024, tk=2048)
    dbch, d_a_conv = _shortconv_bwd(bch, dgated, a_conv_f, "a_gate_bwd")
    dbch = dbch.reshape(3, T, D)
    dxn0 = _matmul(dbch, win_t, mode="nn", out_dtype=F32, a_parts=3, name="a_in_proj_dx")
    d_win_t = _matmul(dbch, xn0, mode="tn", out_dtype=F32, a_parts=3, name="a_in_proj_dw", tm=1024, tn=1024, tk=2048)
    grad_x, _, d_a_norm = _rmsnorm_bwd(x2, a_norm_f, [dxn0], dh1, "a_norm_bwd")

    big_grads = [d_win_t, d_wout, d_wkv_t, d_wq, d_wo, ffn0_grads[1], ffn1_grads[1], ffn0_grads[0], ffn1_grads[0]]
    small_full = [d_a_norm, d_a_conv, jnp.stack([ffn0_grads[2], ffn1_grads[2]]),
                  d_kvb_norm[0], d_kvb_norm[1], d_rel_bias, jnp.concatenate([ffn0_grads[4], ffn1_grads[4]]),
                  jnp.concatenate([ffn0_grads[3], ffn1_grads[3]]), d_final_norm, loss_part]
    small_full_shapes = [(1, D), (3, D), (2, 3, 2 * F), (D,), (1, D), rel_bias.shape, (2, D), (2, 2 * F), (D,), ()]
    halves = [g.reshape(N_DEV // 2, 2, g.shape[0] // N_DEV, D) for g in big_grads]
    theirs, gath = _core_pair_exchange(halves, [_pack_rows(small_full)], "exchange_grads_core")
    core_idx = lax.axis_index("c").astype(jnp.int32).reshape(1)
    chip_idx = (2 * lax.axis_index("x") + lax.axis_index("y")).astype(jnp.int32).reshape(1)
    chip_sums = [_pair_sum(h_, t_, core_idx, f"grad_core_sum_{i}") for i, (h_, t_) in enumerate(zip(halves, theirs))]
    recv = _chip_exchange(chip_sums, "exchange_grads_chip")
    small_sum = _sum_slots(gath[0], "small_grad_sum").reshape(-1)
    (g_a_norm_f, g_a_conv_f, g_ffn_conv_f, g_kv_norm, g_b_norm, g_rel_bias, g_ffn_norm, g_ffn_conv_b,
     g_final_norm, loss) = _unpack(small_sum, small_full_shapes)

    def my_cols(full, width):
        return lax.dynamic_slice_in_dim(full, me * width, width, axis=full.ndim - 1)

    g_a_norm = my_cols(g_a_norm_f, D // N_DEV)
    g_a_conv = my_cols(g_a_conv_f, D // N_DEV)[None]
    g_ffn_conv = my_cols(g_ffn_conv_f, 2 * F // N_DEV)

    big_w = [(a_w_in, m_a_w_in, v_a_w_in, True), (a_w_out, m_a_w_out, v_a_w_out, False),
             (w_kv, m_w_kv, v_w_kv, True), (b_w_q, m_b_w_q, v_b_w_q, False), (b_w_o, m_b_w_o, v_b_w_o, False),
             (ffn_w_up[0], m_ffn_w_up[0], v_ffn_w_up[0], True), (ffn_w_up[1], m_ffn_w_up[1], v_ffn_w_up[1], True),
             (ffn_w_down[0], m_ffn_w_down[0], v_ffn_w_down[0], False),
             (ffn_w_down[1], m_ffn_w_down[1], v_ffn_w_down[1], False)]
    big_out = []
    for i, (w, m, v, transposed) in enumerate(big_w):
        g = _chip_sum(chip_sums[i], recv[i], chip_idx, f"grad_sum_{i}")
        if transposed:
            g = g.T
        w2, m2, v2 = (t.reshape(g.shape) for t in (w, m, v))
        delta, nm, nv = _adamw(g, w2, m2, v2, f"adamw_{i}")
        big_out.append(tuple(t.reshape(w.shape) for t in (g, delta, nm, nv)))

    def pair(i, j):
        return tuple(jnp.stack([big_out[i][t], big_out[j][t]]) for t in range(4))

    small_w = [(a_norm, m_a_norm, v_a_norm, g_a_norm), (a_conv, m_a_conv, v_a_conv, g_a_conv),
               (ffn_conv, m_ffn_conv, v_ffn_conv, g_ffn_conv), (kv_norm, m_kv_norm, v_kv_norm, g_kv_norm),
               (b_norm, m_b_norm, v_b_norm, g_b_norm), (rel_bias, m_rel_bias, v_rel_bias, g_rel_bias),
               (ffn_norm, m_ffn_norm, v_ffn_norm, g_ffn_norm),
               (ffn_conv_b, m_ffn_conv_b, v_ffn_conv_b, g_ffn_conv_b),
               (final_norm, m_final_norm, v_final_norm, g_final_norm)]
    shapes = [t[0].shape for t in small_w]
    packed = [_pack_rows([t[i].reshape(t[0].shape) for t in small_w]) for i in (3, 0, 1, 2)]
    s_delta, s_nm, s_nv = _adamw(*packed, "adamw_small")
    s_g = [t[3].reshape(t[0].shape) for t in small_w]
    s_d, s_m, s_v = (_unpack(t.reshape(-1), shapes) for t in (s_delta, s_nm, s_nv))
    small_out = [(s_g[i], s_d[i], s_m[i], s_v[i]) for i in range(len(small_w))]

    per_weight = [small_out[0], big_out[0], small_out[1], big_out[1], small_out[3], big_out[2], small_out[4],
                  big_out[3], big_out[4], small_out[5], small_out[6], pair(5, 6), small_out[2], small_out[7],
                  pair(7, 8), small_out[8]]
    outs = [loss, grad_x.reshape(B, S, D)]
    for t in range(4):
        outs.extend(pw[t] for pw in per_weight)
    return tuple(outs)
```

```python
import functools
import math

import jax
import jax.numpy as jnp
from jax import lax
from jax.experimental import pallas as pl
from jax.experimental.pallas import tpu as pltpu

F32 = jnp.float32
BF16 = jnp.bfloat16

N_DEV = 8
N_HEADS = 16
ATT_BLOCK = 128
ATT_UNROLL = 8
DILATED_BRANCHES = ((128, 1), (512, 4), (2048, 16))
REL_BUCKETS = 32
REL_MAX_DISTANCE = 2048
RMS_EPS = 1e-6
ADAM_LR = 0.001
ADAM_B1 = 0.9
ADAM_B2 = 0.999
ADAM_EPS = 1e-08
ADAM_WD = 0.01
ADAM_STEP = 10

LANE = 128
HALO = 16
NEG = -1e30
VMEM_CAP = 56 << 20


def _vmem(block_bytes):
    return int(min(VMEM_CAP, max(32 << 20, 3 * block_bytes + (8 << 20))))


def _nbytes(shape, dtype):
    return math.prod(shape) * jnp.dtype(dtype).itemsize


def _tile(dim, target):
    best = None
    t = LANE
    while t <= min(dim, target):
        if dim % t == 0:
            best = t
        t += LANE
    return best if best is not None else dim


def _matmul(a, b, *, mode, out_dtype, name, add=None, a_parts=1, tm=1024, tn=512, tk=1024):
    P = a_parts
    if mode == "nn":
        M, K = (a.shape[0], a.shape[1]) if P == 1 else (a.shape[1], a.shape[2] * P)
        N = b.shape[1]
    elif mode == "nt":
        assert P == 1
        M, K = a.shape
        N = b.shape[0]
    else:
        K = a.shape[0] if P == 1 else a.shape[1]
        M = a.shape[1] if P == 1 else a.shape[2] * P
        N = b.shape[1]
    tm = _tile(M // P if mode == "tn" else M, tm)
    tn = _tile(N, tn)
    tk = _tile(K // P if mode == "nn" else K, tk)
    nm, nn_, nk = M // tm, N // tn, K // tk

    if mode == "nn":
        if P == 1:
            a_spec = pl.BlockSpec((tm, tk), lambda i, j, k: (i, k))
        else:
            nkp = nk // P
            a_spec = pl.BlockSpec((None, tm, tk), lambda i, j, k: (k // nkp, i, k % nkp))
        b_spec = pl.BlockSpec((tk, tn), lambda i, j, k: (k, j))
        dims = (((1,), (0,)), ((), ()))
    elif mode == "nt":
        a_spec = pl.BlockSpec((tm, tk), lambda i, j, k: (i, k))
        b_spec = pl.BlockSpec((tn, tk), lambda i, j, k: (j, k))
        dims = (((1,), (1,)), ((), ()))
    else:
        if P == 1:
            a_spec = pl.BlockSpec((tk, tm), lambda i, j, k: (k, i))
        else:
            nmp = nm // P
            a_spec = pl.BlockSpec((None, tk, tm), lambda i, j, k: (i // nmp, k, i % nmp))
        b_spec = pl.BlockSpec((tk, tn), lambda i, j, k: (k, j))
        dims = (((0,), (0,)), ((), ()))
    o_spec = pl.BlockSpec((tm, tn), lambda i, j, k: (i, j))
    in_specs = [a_spec, b_spec]
    args = [a, b]
    if add is not None:
        in_specs.append(o_spec)
        args.append(add)
    has_add = add is not None

    direct = nk > 1 and out_dtype == F32 and not has_add

    def body(a_ref, b_ref, *rest):
        if has_add:
            add_ref, o_ref = rest[:2]
        else:
            o_ref = rest[0]
        acc_ref = o_ref if direct or nk == 1 else rest[-1]
        k = pl.program_id(2)

        def part():
            return lax.dot_general(a_ref[...].astype(BF16), b_ref[...].astype(BF16), dims,
                                   preferred_element_type=F32)

        def finish(r):
            if has_add:
                r = r + add_ref[...].astype(F32)
            o_ref[...] = r.astype(out_dtype)

        if nk == 1:
            finish(part())
        else:
            @pl.when(k == 0)
            def _():
                acc_ref[...] = part()

            @pl.when(jnp.logical_and(k > 0, jnp.logical_or(k < nk - 1, direct)))
            def _():
                acc_ref[...] += part()

            if not direct:
                @pl.when(k == nk - 1)
                def _():
                    finish(acc_ref[...] + part())

    blk = (_nbytes((tm, tk), a.dtype) + _nbytes((tk, tn), b.dtype) + _nbytes((tm, tn), out_dtype)
           + (_nbytes((tm, tn), add.dtype) if has_add else 0)) * 2 + 3 * _nbytes((tm, tn), F32)
    return pl.pallas_call(
        body, name=name, grid=(nm, nn_, nk),
        in_specs=in_specs, out_specs=o_spec,
        out_shape=jax.ShapeDtypeStruct((M, N), out_dtype),
        scratch_shapes=[] if direct or nk == 1 else [pltpu.VMEM((tm, tn), F32)],
        compiler_params=pltpu.CompilerParams(
            dimension_semantics=("parallel", "parallel", "arbitrary"), vmem_limit_bytes=_vmem(blk)),
    )(*args)


def _rmsnorm_fwd(x, gains, name, tm=512):
    T, D = x.shape
    n = gains.shape[0]
    tm = _tile(T, tm)

    def body(x_ref, g_ref, *o_refs):
        xv = x_ref[...]
        xhat = xv * lax.rsqrt(jnp.mean(xv * xv, axis=-1, keepdims=True) + RMS_EPS)
        for i in range(n):
            o_refs[i][...] = (xhat * g_ref[i:i + 1, :]).astype(BF16)

    row = pl.BlockSpec((tm, D), lambda i: (i, 0))
    outs = pl.pallas_call(
        body, name=name, grid=(T // tm,),
        in_specs=[row, pl.BlockSpec((n, D), lambda i: (0, 0))],
        out_specs=[row] * n,
        out_shape=[jax.ShapeDtypeStruct((T, D), BF16)] * n,
        compiler_params=pltpu.CompilerParams(
            dimension_semantics=("parallel",), vmem_limit_bytes=_vmem(4 * _nbytes((tm, D), F32))),
    )(x, gains)
    return tuple(outs)


def _rmsnorm_bwd(x, gains, dxns, dres, name, tm=512):
    T, D = x.shape
    n = gains.shape[0]
    tm = _tile(T, tm)

    def body(x_ref, g_ref, *rest):
        dxn_refs = rest[:n]
        dres_ref, dx_ref, dxb_ref, dg_ref = rest[n:]
        xv = x_ref[...]
        rstd = lax.rsqrt(jnp.mean(xv * xv, axis=-1, keepdims=True) + RMS_EPS)
        xhat = xv * rstd
        dx = dres_ref[...]

        @pl.when(pl.program_id(0) == 0)
        def _():
            dg_ref[...] = jnp.zeros_like(dg_ref)

        for i in range(n):
            dy = dxn_refs[i][...].astype(F32)
            dg_ref[i:i + 1, :] += jnp.sum(dy * xhat, axis=0, keepdims=True)
            dxh = dy * g_ref[i:i + 1, :]
            dx = dx + rstd * (dxh - xhat * jnp.mean(dxh * xhat, axis=-1, keepdims=True))
        dx_ref[...] = dx
        dxb_ref[...] = dx.astype(BF16)

    row = pl.BlockSpec((tm, D), lambda i: (i, 0))
    par = pl.BlockSpec((n, D), lambda i: (0, 0))
    return pl.pallas_call(
        body, name=name, grid=(T // tm,),
        in_specs=[row, par] + [row] * n + [row],
        out_specs=[row, row, par],
        out_shape=[jax.ShapeDtypeStruct((T, D), F32), jax.ShapeDtypeStruct((T, D), BF16),
                   jax.ShapeDtypeStruct((n, D), F32)],
        compiler_params=pltpu.CompilerParams(
            dimension_semantics=("arbitrary",), vmem_limit_bytes=_vmem((4 + n) * _nbytes((tm, D), F32))),
    )(x, gains, *dxns, dres)


def _final_loss_bwd(h, gain, target, name, tm=512):
    T, D = h.shape
    tm = _tile(T, tm)

    def body(h_ref, g_ref, t_ref, dh_ref, dhb_ref, dg_ref, sq_ref):
        xv = h_ref[...]
        rstd = lax.rsqrt(jnp.mean(xv * xv, axis=-1, keepdims=True) + RMS_EPS)
        xhat = xv * rstd
        err = xhat * g_ref[...] - t_ref[...]

        @pl.when(pl.program_id(0) == 0)
        def _():
            dg_ref[...] = jnp.zeros_like(dg_ref)
            sq_ref[...] = jnp.zeros_like(sq_ref)

        sq_ref[...] += jnp.sum(err * err, axis=0, keepdims=True)
        dy = err * (1.0 / D)
        dg_ref[...] += jnp.sum(dy * xhat, axis=0, keepdims=True)
        dxh = dy * g_ref[...]
        dh = rstd * (dxh - xhat * jnp.mean(dxh * xhat, axis=-1, keepdims=True))
        dh_ref[...] = dh
        dhb_ref[...] = dh.astype(BF16)

    row = pl.BlockSpec((tm, D), lambda i: (i, 0))
    par = pl.BlockSpec((1, D), lambda i: (0, 0))
    return pl.pallas_call(
        body, name=name, grid=(T // tm,),
        in_specs=[row, par, row], out_specs=[row, row, par, par],
        out_shape=[jax.ShapeDtypeStruct((T, D), F32), jax.ShapeDtypeStruct((T, D), BF16),
                   jax.ShapeDtypeStruct((1, D), F32), jax.ShapeDtypeStruct((1, D), F32)],
        compiler_params=pltpu.CompilerParams(
            dimension_semantics=("arbitrary",), vmem_limit_bytes=_vmem(5 * _nbytes((tm, D), F32))),
    )(h, gain, target)


def _halo_specs(S, ts, tc, col_off, prev, nxt):
    r = ts // HALO
    last = S // HALO - 1
    specs = []
    if prev:
        specs.append(pl.BlockSpec((None, HALO, tc), lambda j, b, s: (b, jnp.maximum(s * r - 1, 0), col_off + j)))
    specs.append(pl.BlockSpec((None, ts, tc), lambda j, b, s: (b, s, col_off + j)))
    if nxt:
        specs.append(pl.BlockSpec((None, HALO, tc), lambda j, b, s: (b, jnp.minimum((s + 1) * r, last), col_off + j)))
    return specs


def _ext(prev_ref, main_ref, next_ref, first, last):
    main = main_ref[...].astype(F32)
    zeros = jnp.zeros((HALO, main.shape[1]), F32)
    top = zeros if prev_ref is None else jnp.where(first, 0.0, prev_ref[...].astype(F32))
    bot = zeros if next_ref is None else jnp.where(last, 0.0, next_ref[...].astype(F32))
    return jnp.concatenate([top, main, bot], axis=0)


def _shift(xe, k):
    return pltpu.roll(xe, k % xe.shape[0], axis=0)


def _cconv(xe, w):
    return w[0:1, :] * _shift(xe, 2) + w[1:2, :] * _shift(xe, 1) + w[2:3, :] * xe


def _cconv_t(de, w):
    return w[2:3, :] * de + w[1:2, :] * _shift(de, -1) + w[0:1, :] * _shift(de, -2)


def _main(xe, ts):
    return xe[HALO:HALO + ts, :]


def _conv_wgrad(dw_ref, de, xe, ts):
    d = _main(de, ts)
    dw_ref[0:1, :] += jnp.sum(d * _main(_shift(xe, 2), ts), axis=0, keepdims=True)
    dw_ref[1:2, :] += jnp.sum(d * _main(_shift(xe, 1), ts), axis=0, keepdims=True)
    dw_ref[2:3, :] += jnp.sum(d * _main(xe, ts), axis=0, keepdims=True)


def _gate_tiles(S, C):
    return _tile(S, 512), _tile(C, 512)


def _shortconv_fwd(bch, conv_w, name):
    B, S, D3 = bch.shape
    D = D3 // 3
    ts, tc = _gate_tiles(S, D)
    nj, ns = D // tc, S // ts

    def body(b_ref, cp_ref, c_ref, hp_ref, h_ref, w_ref, o_ref):
        first = pl.program_id(2) == 0
        ce = _ext(cp_ref, c_ref, None, first, False)
        he = _ext(hp_ref, h_ref, None, first, False)
        cv = _main(_cconv(ce * he, w_ref[...]), ts)
        o_ref[...] = (b_ref[...].astype(F32) * cv).astype(BF16)

    in_specs = (_halo_specs(S, ts, tc, 0, False, False) + _halo_specs(S, ts, tc, nj, True, False)
                + _halo_specs(S, ts, tc, 2 * nj, True, False) + [pl.BlockSpec((3, tc), lambda j, b, s: (0, j))])
    return pl.pallas_call(
        body, name=name, grid=(nj, B, ns), in_specs=in_specs,
        out_specs=pl.BlockSpec((None, ts, tc), lambda j, b, s: (b, s, j)),
        out_shape=jax.ShapeDtypeStruct((B, S, D), BF16),
        compiler_params=pltpu.CompilerParams(
            dimension_semantics=("parallel", "parallel", "parallel"),
            vmem_limit_bytes=_vmem(12 * _nbytes((ts + 2 * HALO, tc), F32))),
    )(bch, bch, bch, bch, bch, conv_w)


def _shortconv_bwd(bch, dg, conv_w, name):
    B, S, D3 = bch.shape
    D = D3 // 3
    ts, tc = _gate_tiles(S, D)
    nj, ns = D // tc, S // ts

    def body(b_ref, bn_ref, cp_ref, c_ref, cn_ref, hp_ref, h_ref, hn_ref, d_ref, dn_ref, w_ref, o_ref, dw_ref):
        s = pl.program_id(2)
        first, last = s == 0, s == ns - 1

        @pl.when(jnp.logical_and(pl.program_id(1) == 0, s == 0))
        def _():
            dw_ref[...] = jnp.zeros_like(dw_ref)

        w = w_ref[...]
        be = _ext(None, b_ref, bn_ref, first, last)
        ce = _ext(cp_ref, c_ref, cn_ref, first, last)
        he = _ext(hp_ref, h_ref, hn_ref, first, last)
        de = _ext(None, d_ref, dn_ref, first, last)
        ch = ce * he
        dcv = de * be
        dch = _main(_cconv_t(dcv, w), ts)
        o_ref[0] = (_main(de, ts) * _main(_cconv(ch, w), ts)).astype(BF16)
        o_ref[1] = (dch * _main(he, ts)).astype(BF16)
        o_ref[2] = (dch * _main(ce, ts)).astype(BF16)
        _conv_wgrad(dw_ref, dcv, ch, ts)

    in_specs = (_halo_specs(S, ts, tc, 0, False, True) + _halo_specs(S, ts, tc, nj, True, True)
                + _halo_specs(S, ts, tc, 2 * nj, True, True) + _halo_specs(S, ts, tc, 0, False, True)
                + [pl.BlockSpec((3, tc), lambda j, b, s: (0, j))])
    return pl.pallas_call(
        body, name=name, grid=(nj, B, ns), in_specs=in_specs,
        out_specs=[pl.BlockSpec((3, None, ts, tc), lambda j, b, s: (0, b, s, j)),
                   pl.BlockSpec((3, tc), lambda j, b, s: (0, j))],
        out_shape=[jax.ShapeDtypeStruct((3, B, S, D), BF16), jax.ShapeDtypeStruct((3, D), F32)],
        compiler_params=pltpu.CompilerParams(
            dimension_semantics=("parallel", "arbitrary", "arbitrary"),
            vmem_limit_bytes=_vmem(24 * _nbytes((ts + 2 * HALO, tc), F32))),
    )(bch, bch, bch, bch, bch, bch, bch, bch, dg, dg, conv_w)


def _sigmoid(x):
    return 1.0 / (1.0 + jnp.exp(-x))


def _ffn_gate_fwd(u0, conv_w, conv_b, name):
    B, S, F2 = u0.shape
    F = F2 // 2
    ts, tc = _gate_tiles(S, F)
    nj, ns = F // tc, S // ts

    def body(gp_ref, g_ref, up_ref, u_ref, wg_ref, wu_ref, bg_ref, bu_ref, o_ref):
        first = pl.program_id(2) == 0
        ug = _main(_cconv(_ext(gp_ref, g_ref, None, first, False), wg_ref[...]), ts) + bg_ref[...]
        uu = _main(_cconv(_ext(up_ref, u_ref, None, first, False), wu_ref[...]), ts) + bu_ref[...]
        o_ref[...] = (ug * _sigmoid(ug) * uu).astype(BF16)

    in_specs = (_halo_specs(S, ts, tc, 0, True, False) + _halo_specs(S, ts, tc, nj, True, False)
                + [pl.BlockSpec((3, tc), lambda j, b, s: (0, j)), pl.BlockSpec((3, tc), lambda j, b, s: (0, nj + j)),
                   pl.BlockSpec((1, tc), lambda j, b, s: (0, j)), pl.BlockSpec((1, tc), lambda j, b, s: (0, nj + j))])
    return pl.pallas_call(
        body, name=name, grid=(nj, B, ns), in_specs=in_specs,
        out_specs=pl.BlockSpec((None, ts, tc), lambda j, b, s: (b, s, j)),
        out_shape=jax.ShapeDtypeStruct((B, S, F), BF16),
        compiler_params=pltpu.CompilerParams(
            dimension_semantics=("parallel", "parallel", "parallel"),
            vmem_limit_bytes=_vmem(12 * _nbytes((ts + 2 * HALO, tc), F32))),
    )(u0, u0, u0, u0, conv_w, conv_w, conv_b, conv_b)


def _ffn_gate_bwd(u0, dact, conv_w, conv_b, name):
    B, S, F2 = u0.shape
    F = F2 // 2
    ts, tc = _gate_tiles(S, F)
    nj, ns = F // tc, S // ts

    def body(gp_ref, g_ref, gn_ref, up_ref, u_ref, un_ref, d_ref, dn_ref, wg_ref, wu_ref, bg_ref, bu_ref,
             o_ref, dwg_ref, dwu_ref, dbg_ref, dbu_ref):
        s = pl.program_id(2)
        first, last = s == 0, s == ns - 1

        @pl.when(jnp.logical_and(pl.program_id(1) == 0, s == 0))
        def _():
            for r in (dwg_ref, dwu_ref, dbg_ref, dbu_ref):
                r[...] = jnp.zeros_like(r)

        wg, wu = wg_ref[...], wu_ref[...]
        ge = _ext(gp_ref, g_ref, gn_ref, first, last)
        ue = _ext(up_ref, u_ref, un_ref, first, last)
        de = _ext(None, d_ref, dn_ref, first, last)
        ug = _cconv(ge, wg) + bg_ref[...]
        uu = _cconv(ue, wu) + bu_ref[...]
        sig = _sigmoid(ug)
        dug = de * uu * (sig * (1.0 + ug * (1.0 - sig)))
        duu = de * (ug * sig)
        o_ref[0] = _main(_cconv_t(dug, wg), ts).astype(BF16)
        o_ref[1] = _main(_cconv_t(duu, wu), ts).astype(BF16)
        _conv_wgrad(dwg_ref, dug, ge, ts)
        _conv_wgrad(dwu_ref, duu, ue, ts)
        dbg_ref[...] += jnp.sum(_main(dug, ts), axis=0, keepdims=True)
        dbu_ref[...] += jnp.sum(_main(duu, ts), axis=0, keepdims=True)

    w3 = lambda off: pl.BlockSpec((3, tc), lambda j, b, s: (0, off + j))
    w1 = lambda off: pl.BlockSpec((1, tc), lambda j, b, s: (0, off + j))
    in_specs = (_halo_specs(S, ts, tc, 0, True, True) + _halo_specs(S, ts, tc, nj, True, True)
                + _halo_specs(S, ts, tc, 0, False, True) + [w3(0), w3(nj), w1(0), w1(nj)])
    outs = pl.pallas_call(
        body, name=name, grid=(nj, B, ns), in_specs=in_specs,
        out_specs=[pl.BlockSpec((2, None, ts, tc), lambda j, b, s: (0, b, s, j)), w3(0), w3(0), w1(0), w1(0)],
        out_shape=[jax.ShapeDtypeStruct((2, B, S, F), BF16), jax.ShapeDtypeStruct((3, F), F32),
                   jax.ShapeDtypeStruct((3, F), F32), jax.ShapeDtypeStruct((1, F), F32),
                   jax.ShapeDtypeStruct((1, F), F32)],
        compiler_params=pltpu.CompilerParams(
            dimension_semantics=("parallel", "arbitrary", "arbitrary"),
            vmem_limit_bytes=_vmem(30 * _nbytes((ts + 2 * HALO, tc), F32))),
    )(u0, u0, u0, u0, u0, u0, dact, dact, conv_w, conv_w, conv_b, conv_b)
    du0, dwg, dwu, dbg, dbu = outs
    return du0, jnp.concatenate([dwg, dwu], axis=1), jnp.concatenate([dbg, dbu], axis=1)


def _t5_bucket(dist):
    max_exact = REL_BUCKETS // 2
    n = jnp.maximum(dist, 0)
    nf = jnp.maximum(n, max_exact).astype(F32)
    large = max_exact + (jnp.log(nf / max_exact) / math.log(REL_MAX_DISTANCE / max_exact)
                         * (REL_BUCKETS - max_exact)).astype(jnp.int32)
    large = jnp.minimum(large, REL_BUCKETS - 1)
    return jnp.where(n < max_exact, n, large)


def _band_tables(window, dilation):
    P = ATT_BLOCK
    qi = jnp.arange(P, dtype=jnp.int32)[:, None]
    kc = jnp.arange(2 * P, dtype=jnp.int32)[None, :]
    delta = qi + P - kc
    band = (delta >= 0) & (delta <= window // dilation)
    bucket = _t5_bucket(delta * dilation).reshape(-1)
    onehot = (bucket[None, :] == jnp.arange(REL_BUCKETS, dtype=jnp.int32)[:, None]).astype(F32)
    return onehot, band


def _bias_lookup(rel_bias_t, onehots, name):
    nb, _, Q = onehots.shape
    H = rel_bias_t.shape[0]

    def body(r_ref, oh_ref, o_ref):
        o_ref[...] = lax.dot_general(r_ref[...], oh_ref[...], (((1,), (0,)), ((), ())),
                                     precision=lax.Precision.HIGHEST, preferred_element_type=F32)

    return pl.pallas_call(
        body, name=name, grid=(nb,),
        in_specs=[pl.BlockSpec((H, REL_BUCKETS), lambda i: (0, 0)),
                  pl.BlockSpec((None, REL_BUCKETS, Q), lambda i: (i, 0, 0))],
        out_specs=pl.BlockSpec((None, H, Q), lambda i: (i, 0, 0)),
        out_shape=jax.ShapeDtypeStruct((nb, H, Q), F32),
        compiler_params=pltpu.CompilerParams(dimension_semantics=("parallel",),
                                             vmem_limit_bytes=_vmem(_nbytes((REL_BUCKETS + H, Q), F32))),
    )(rel_bias_t, onehots)


def _bias_grad(dtabs, onehots, name):
    nb, H, Q = dtabs.shape

    def body(d_ref, oh_ref, o_ref):
        @pl.when(pl.program_id(0) == 0)
        def _():
            o_ref[...] = jnp.zeros_like(o_ref)

        o_ref[...] += lax.dot_general(d_ref[...], oh_ref[...], (((1,), (1,)), ((), ())),
                                      precision=lax.Precision.HIGHEST, preferred_element_type=F32)

    return pl.pallas_call(
        body, name=name, grid=(nb,),
        in_specs=[pl.BlockSpec((None, H, Q), lambda i: (i, 0, 0)),
                  pl.BlockSpec((None, REL_BUCKETS, Q), lambda i: (i, 0, 0))],
        out_specs=pl.BlockSpec((H, REL_BUCKETS), lambda i: (0, 0)),
        out_shape=jax.ShapeDtypeStruct((H, REL_BUCKETS), F32),
        compiler_params=pltpu.CompilerParams(dimension_semantics=("arbitrary",),
                                             vmem_limit_bytes=_vmem(_nbytes((REL_BUCKETS + H, Q), F32))),
    )(dtabs, onehots)


def _lane_masks():
    lane = lax.broadcasted_iota(jnp.int32, (1, LANE), 1)
    lo = lane < LANE // 2
    return lo, jnp.logical_not(lo)


def _dot_nt(a, b):
    return lax.dot_general(a, b, (((1,), (1,)), ((), ())), preferred_element_type=F32)


def _dot_nn(a, b):
    return lax.dot_general(a, b, (((1,), (0,)), ((), ())), preferred_element_type=F32)


def _dot_tn(a, b):
    return lax.dot_general(a, b, (((0,), (0,)), ((), ())), preferred_element_type=F32)


def _block_rows(n, dilation, S):
    P = ATT_BLOCK
    nb = S // (dilation * P)
    r, i = n // nb, n % nb
    cur = pl.ds(i * (P * dilation) + r, P, stride=dilation)
    prv = pl.ds(jnp.maximum(i - 1, 0) * (P * dilation) + r, P, stride=dilation)
    return cur, prv, jnp.minimum(i, 1)


def _attn_fwd(q, kv, tabs, name):
    B, S, D = q.shape
    P, H = ATT_BLOCK, N_HEADS
    scale = (D // H) ** -0.5
    half = LANE // 2
    nsteps = S // P // ATT_UNROLL

    def body(q_ref, k_ref, v_ref, tab_ref, o_ref, lse_ref, acc_ref, m_ref, l_ref):
        lo, hi = _lane_masks()
        for bi, (_, d) in enumerate(DILATED_BRANCHES):
            def block(n, carry, bi=bi, d=d):
                rows = [_block_rows(n + j * nsteps, d, S) for j in range(ATT_UNROLL)]
                loaded = []
                for cur, prv, variant in rows:
                    qb = q_ref[cur, :].astype(BF16)
                    kw = jnp.concatenate([k_ref[prv, :], k_ref[cur, :]], axis=0).astype(BF16)
                    vw = jnp.concatenate([v_ref[prv, :], v_ref[cur, :]], axis=0).astype(BF16)
                    loaded.append((qb, kw, vw))
                results = []
                for (cur, prv, variant), (qb, kw, vw) in zip(rows, loaded):
                    m_pair = l_pair = a_pair = None
                    for e, msk in enumerate((lo, hi)):
                        s = _dot_nt(jnp.where(msk, qb, 0), kw) * scale + tab_ref[bi, variant, e]
                        m_e = jnp.max(s, axis=-1, keepdims=True)
                        pe = jnp.exp(s - m_e)
                        l_e = jnp.sum(pe, axis=-1, keepdims=True)
                        a_e = _dot_nn(pe.astype(BF16), jnp.where(msk, vw, 0))
                        m_pair = m_e if e == 0 else jnp.where(lo, m_pair, m_e)
                        l_pair = l_e if e == 0 else jnp.where(lo, l_pair, l_e)
                        a_pair = a_e if e == 0 else a_pair + a_e
                    results.append((m_pair, l_pair, a_pair))
                for (cur, prv, variant), (m_pair, l_pair, a_pair) in zip(rows, results):
                    m_ref[bi, cur, :] = m_pair
                    l_ref[bi, cur, :] = l_pair
                    acc_ref[bi, cur, :] = a_pair
                return carry

            lax.fori_loop(0, nsteps, block, 0)

        nbr = len(DILATED_BRANCHES)
        chunk = 512

        def merge(t, carry):
            rs = pl.ds(pl.multiple_of(t * chunk, chunk), chunk)
            ms = [m_ref[b, rs, :] for b in range(nbr)]
            m = functools.reduce(jnp.maximum, ms)
            ws = [jnp.exp(mb - m) for mb in ms]
            l = functools.reduce(jnp.add, [w * l_ref[b, rs, :] for b, w in enumerate(ws)])
            acc = functools.reduce(jnp.add, [w * acc_ref[b, rs, :] for b, w in enumerate(ws)])
            o_ref[rs, :] = (acc / l).astype(BF16)
            lse_ref[rs, :] = m + jnp.log(l)
            return carry

        lax.fori_loop(0, S // chunk, merge, 0)

    nl = D // LANE
    col = lambda off: pl.BlockSpec((None, S, LANE), lambda b, p: (b, 0, off + p))
    tab_spec = pl.BlockSpec((3, 2, 2, P, 2 * P), lambda b, p: (0, 0, p, 0, 0))
    return pl.pallas_call(
        body, name=name, grid=(B, H // 2), in_specs=[col(0), col(0), col(nl), tab_spec],
        out_specs=[col(0), col(0)],
        out_shape=[jax.ShapeDtypeStruct((B, S, D), BF16), jax.ShapeDtypeStruct((B, S, D), F32)],
        scratch_shapes=[pltpu.VMEM((len(DILATED_BRANCHES), S, LANE), F32)] * 3,
        compiler_params=pltpu.CompilerParams(
            dimension_semantics=("parallel", "parallel"), vmem_limit_bytes=VMEM_CAP),
    )(q, kv, kv, tabs)


def _attn_bwd(q, kv, do, o, lse, tabs, name):
    B, S, D = q.shape
    P, H = ATT_BLOCK, N_HEADS
    scale = (D // H) ** -0.5
    half = LANE // 2
    nsteps = S // P // ATT_UNROLL

    def body(q_ref, k_ref, v_ref, do_ref, o_ref, lse_ref, tab_ref, dq_ref, dkv_ref, dtab_ref, delta_ref):
        lo, hi = _lane_masks()

        @pl.when(pl.program_id(1) == 0)
        def _():
            dtab_ref[...] = jnp.zeros_like(dtab_ref)

        dkv_ref[...] = jnp.zeros_like(dkv_ref)
        prod = do_ref[...] * o_ref[...].astype(F32)
        delta_ref[...] = jnp.where(lo, jnp.sum(jnp.where(lo, prod, 0.0), axis=-1, keepdims=True),
                                   jnp.sum(jnp.where(hi, prod, 0.0), axis=-1, keepdims=True))

        for bi, (_, d) in enumerate(DILATED_BRANCHES):
            def block(n, carry, bi=bi, d=d):
                rows = [_block_rows(n + j * nsteps, d, S) for j in range(ATT_UNROLL)]
                loaded = []
                for cur, prv, variant in rows:
                    qb = q_ref[cur, :].astype(BF16)
                    kw = jnp.concatenate([k_ref[prv, :], k_ref[cur, :]], axis=0).astype(BF16)
                    vw = jnp.concatenate([v_ref[prv, :], v_ref[cur, :]], axis=0).astype(BF16)
                    dob = do_ref[cur, :].astype(BF16)
                    dq_old = dq_ref[cur, :] if bi > 0 else None
                    loaded.append((qb, kw, vw, dob, lse_ref[cur, :], delta_ref[cur, :], dq_old))
                results = []
                for j, ((cur, prv, variant), (qb, kw, vw, dob, lse_b, dl_b, dq_old)) in enumerate(zip(rows, loaded)):
                    dq_pair = dk_pair = dv_pair = None
                    for e, msk in enumerate((lo, hi)):
                        qm = jnp.where(msk, qb, 0)
                        dom = jnp.where(msk, dob, 0)
                        s = _dot_nt(qm, kw) * scale + tab_ref[bi, variant, e]
                        pe = jnp.exp(s - lse_b[:, e * half:e * half + 1])
                        ds = pe * (_dot_nt(dom, vw) - dl_b[:, e * half:e * half + 1])
                        dtab_ref[bi, e] += ds
                        dsb = ds.astype(BF16)
                        dq_e = _dot_nn(dsb, jnp.where(msk, kw, 0))
                        dk_e = _dot_tn(dsb, qm)
                        dv_e = _dot_tn(pe.astype(BF16), dom)
                        dq_pair = dq_e if e == 0 else dq_pair + dq_e
                        dk_pair = dk_e if e == 0 else dk_pair + dk_e
                        dv_pair = dv_e if e == 0 else dv_pair + dv_e
                    dq_pair = dq_pair * scale
                    if bi > 0:
                        dq_pair = dq_pair + dq_old
                    results.append((dq_pair, dk_pair * scale, dv_pair))
                for (cur, prv, variant), (dq_pair, dk_pair, dv_pair) in zip(rows, results):
                    dq_ref[cur, :] = dq_pair
                    dkv_ref[0, cur, :] += dk_pair[P:, :]
                    dkv_ref[1, cur, :] += dv_pair[P:, :]
                    dkv_ref[0, prv, :] += dk_pair[:P, :]
                    dkv_ref[1, prv, :] += dv_pair[:P, :]
                return carry

            lax.fori_loop(0, nsteps, block, 0)

    nl = D // LANE
    col = lambda off: pl.BlockSpec((None, S, LANE), lambda p, b: (b, 0, off + p))
    tab_spec = pl.BlockSpec((3, 2, 2, P, 2 * P), lambda p, b: (0, 0, p, 0, 0))
    return pl.pallas_call(
        body, name=name, grid=(H // 2, B),
        in_specs=[col(0), col(0), col(nl), col(0), col(0), col(0), tab_spec],
        out_specs=[col(0), pl.BlockSpec((2, None, S, LANE), lambda p, b: (0, b, 0, p)),
                   pl.BlockSpec((None, 3, 2, P, 2 * P), lambda p, b: (p, 0, 0, 0, 0))],
        out_shape=[jax.ShapeDtypeStruct((B, S, D), F32), jax.ShapeDtypeStruct((2, B, S, D), F32),
                   jax.ShapeDtypeStruct((H // 2, 3, 2, P, 2 * P), F32)],
        scratch_shapes=[pltpu.VMEM((S, LANE), F32)],
        compiler_params=pltpu.CompilerParams(
            dimension_semantics=("parallel", "arbitrary"),
            vmem_limit_bytes=VMEM_CAP),
    )(q, kv, kv, do, o, lse, tabs)


def _attn_fwd_branch(q, kv, tab, dilation, name):
    B, S, D = q.shape
    P, d = ATT_BLOCK, dilation
    L = S // d
    nb = L // P
    H = N_HEADS
    scale = (D // H) ** -0.5
    qv = q.reshape(B, L, d * D)
    kvv = kv.reshape(B, L, d * 2 * D)

    def body(q_ref, kp_ref, kc_ref, vp_ref, vc_ref, tab_ref, o_ref, lse_ref):
        lo, hi = _lane_masks()
        for p in range(H // 2):
            sl = slice(p * LANE, (p + 1) * LANE)
            qp = q_ref[:, sl]
            kp = jnp.concatenate([kp_ref[:, sl], kc_ref[:, sl]], axis=0)
            vp = jnp.concatenate([vp_ref[:, sl], vc_ref[:, sl]], axis=0)
            o_pair = None
            lse_pair = None
            for e, msk in enumerate((lo, hi)):
                s = _dot_nt(jnp.where(msk, qp, 0), kp) * scale + tab_ref[2 * p + e]
                m = jnp.max(s, axis=-1, keepdims=True)
                pe = jnp.exp(s - m)
                den = jnp.sum(pe, axis=-1, keepdims=True)
                o_e = _dot_nn(pe.astype(BF16), jnp.where(msk, vp, 0)) / den
                lse_e = m + jnp.log(den)
                o_pair = o_e if e == 0 else o_pair + o_e
                lse_pair = lse_e if e == 0 else jnp.where(lo, lse_pair, lse_e)
            o_ref[:, sl] = o_pair
            lse_ref[:, sl] = lse_pair

    blk = lambda f: pl.BlockSpec((None, P, D), f)
    prev = lambda i: jnp.maximum(i - 1, 0)
    in_specs = [blk(lambda b, r, i: (b, i, r)),
                blk(lambda b, r, i: (b, prev(i), 2 * r)), blk(lambda b, r, i: (b, i, 2 * r)),
                blk(lambda b, r, i: (b, prev(i), 2 * r + 1)), blk(lambda b, r, i: (b, i, 2 * r + 1)),
                pl.BlockSpec((None, H, P, 2 * P), lambda b, r, i: (jnp.minimum(i, 1), 0, 0, 0))]
    o, lse = pl.pallas_call(
        body, name=name, grid=(B, d, nb), in_specs=in_specs,
        out_specs=[blk(lambda b, r, i: (b, i, r))] * 2,
        out_shape=[jax.ShapeDtypeStruct((B, L, d * D), F32)] * 2,
        compiler_params=pltpu.CompilerParams(
            dimension_semantics=("parallel", "parallel", "parallel"),
            vmem_limit_bytes=_vmem(_nbytes((H, P, 2 * P), F32) + 9 * _nbytes((P, D), F32))),
    )(qv, kvv, kvv, kvv, kvv, tab)
    return o.reshape(B * S, D), lse.reshape(B * S, D)


def _attn_merge(os_, lses, name, tm=512):
    T, D = os_[0].shape
    n = len(os_)
    tm = _tile(T, tm)

    def body(*refs):
        o_refs, l_refs = refs[:n], refs[n:2 * n]
        out_ref, lse_ref = refs[2 * n:]
        ls = [r[...] for r in l_refs]
        m = functools.reduce(jnp.maximum, ls)
        ws = [jnp.exp(l - m) for l in ls]
        tot = functools.reduce(jnp.add, ws)
        acc = functools.reduce(jnp.add, [w * r[...] for w, r in zip(ws, o_refs)])
        out_ref[...] = (acc / tot).astype(BF16)
        lse_ref[...] = m + jnp.log(tot)

    row = pl.BlockSpec((tm, D), lambda i: (i, 0))
    return pl.pallas_call(
        body, name=name, grid=(T // tm,), in_specs=[row] * (2 * n), out_specs=[row, row],
        out_shape=[jax.ShapeDtypeStruct((T, D), BF16), jax.ShapeDtypeStruct((T, D), F32)],
        compiler_params=pltpu.CompilerParams(
            dimension_semantics=("parallel",), vmem_limit_bytes=_vmem((2 * n + 4) * _nbytes((tm, D), F32))),
    )(*os_, *lses)


def _attn_bwd_q(q, kv, do, o, lse, tab, dq_acc, dilation, name):
    B, S, D = q.shape
    P, d = ATT_BLOCK, dilation
    L = S // d
    nb = L // P
    H = N_HEADS
    scale = (D // H) ** -0.5
    view = lambda t, c: t.reshape(B, L, d * c)
    has_acc = dq_acc is not None

    def body(q_ref, kp_ref, kc_ref, vp_ref, vc_ref, do_ref, o_ref, lse_ref, tab_ref, *rest):
        if has_acc:
            acc_ref, dq_ref, dtab_ref = rest
        else:
            dq_ref, dtab_ref = rest

        @pl.when((pl.program_id(0) == 0) & (pl.program_id(1) == 0) & (pl.program_id(2) == 0))
        def _():
            dtab_ref[...] = jnp.zeros_like(dtab_ref)

        lo, hi = _lane_masks()
        for p in range(H // 2):
            sl = slice(p * LANE, (p + 1) * LANE)
            qp = q_ref[:, sl]
            kp = jnp.concatenate([kp_ref[:, sl], kc_ref[:, sl]], axis=0)
            vp = jnp.concatenate([vp_ref[:, sl], vc_ref[:, sl]], axis=0)
            dop = do_ref[:, sl]
            prod = dop.astype(F32) * o_ref[:, sl].astype(F32)
            lsep = lse_ref[:, sl]
            dq_pair = None
            for e, msk in enumerate((lo, hi)):
                s = _dot_nt(jnp.where(msk, qp, 0), kp) * scale + tab_ref[2 * p + e]
                pe = jnp.exp(s - lsep[:, e * (LANE // 2):e * (LANE // 2) + 1])
                dp = _dot_nt(jnp.where(msk, dop, 0).astype(BF16), vp)
                delta = jnp.sum(jnp.where(msk, prod, 0.0), axis=-1, keepdims=True)
                ds = pe * (dp - delta)
                dtab_ref[2 * p + e] += ds
                dq_e = _dot_nn(ds.astype(BF16), jnp.where(msk, kp, 0))
                dq_pair = dq_e if e == 0 else dq_pair + dq_e
            dq_pair = dq_pair * scale
            if has_acc:
                dq_pair = dq_pair + acc_ref[:, sl]
            dq_ref[:, sl] = dq_pair

    blk = lambda f: pl.BlockSpec((None, P, D), f)
    prev = lambda i: jnp.maximum(i - 1, 0)
    own = blk(lambda b, r, i: (b, i, r))
    tab_spec = pl.BlockSpec((None, H, P, 2 * P), lambda b, r, i: (jnp.minimum(i, 1), 0, 0, 0))
    in_specs = [own,
                blk(lambda b, r, i: (b, prev(i), 2 * r)), blk(lambda b, r, i: (b, i, 2 * r)),
                blk(lambda b, r, i: (b, prev(i), 2 * r + 1)), blk(lambda b, r, i: (b, i, 2 * r + 1)),
                own, own, own, tab_spec]
    kvv = view(kv, 2 * D)
    args = [view(q, D), kvv, kvv, kvv, kvv, view(do, D), view(o, D), view(lse, D), tab]
    if has_acc:
        in_specs.append(own)
        args.append(view(dq_acc, D))
    dq, dtab = pl.pallas_call(
        body, name=name, grid=(B, d, nb), in_specs=in_specs,
        out_specs=[own, pl.BlockSpec((H, P, 2 * P), lambda b, r, i: (0, 0, 0))],
        out_shape=[jax.ShapeDtypeStruct((B, L, d * D), F32), jax.ShapeDtypeStruct((H, P, 2 * P), F32)],
        compiler_params=pltpu.CompilerParams(
            dimension_semantics=("arbitrary", "arbitrary", "arbitrary"),
            vmem_limit_bytes=_vmem(2 * _nbytes((H, P, 2 * P), F32) + 12 * _nbytes((P, D), F32))),
    )(*args)
    return dq.reshape(B, S, D), dtab


def _attn_bwd_kv(q, kv, do, o, lse, tabk, dkv_acc, dilation, name):
    B, S, D = q.shape
    P, d = ATT_BLOCK, dilation
    L = S // d
    nb = L // P
    H = N_HEADS
    scale = (D // H) ** -0.5
    view = lambda t, c: t.reshape(B, L, d * c)
    has_acc = dkv_acc is not None

    def body(k_ref, v_ref, qa_ref, qb_ref, doa_ref, dob_ref, oa_ref, ob_ref, la_ref, lb_ref, tab_ref, *rest):
        if has_acc:
            acc_ref, dkv_ref = rest
        else:
            (dkv_ref,) = rest
        lo, hi = _lane_masks()
        for p in range(H // 2):
            sl = slice(p * LANE, (p + 1) * LANE)
            kp, vp = k_ref[:, sl], v_ref[:, sl]
            q2 = jnp.concatenate([qa_ref[:, sl], qb_ref[:, sl]], axis=0)
            do2 = jnp.concatenate([doa_ref[:, sl], dob_ref[:, sl]], axis=0)
            o2 = jnp.concatenate([oa_ref[:, sl], ob_ref[:, sl]], axis=0)
            lse2 = jnp.concatenate([la_ref[:, sl], lb_ref[:, sl]], axis=0)
            prod = do2.astype(F32) * o2.astype(F32)
            dk_pair = None
            dv_pair = None
            for e, msk in enumerate((lo, hi)):
                qm = jnp.where(msk, q2, 0)
                dom = jnp.where(msk, do2, 0).astype(BF16)
                s = _dot_nt(qm, kp) * scale + tab_ref[2 * p + e]
                pe = jnp.exp(s - lse2[:, e * (LANE // 2):e * (LANE // 2) + 1])
                dp = _dot_nt(dom, vp)
                delta = jnp.sum(jnp.where(msk, prod, 0.0), axis=-1, keepdims=True)
                ds = pe * (dp - delta)
                dv_e = _dot_tn(pe.astype(BF16), dom)
                dk_e = _dot_tn(ds.astype(BF16), qm)
                dk_pair = dk_e if e == 0 else dk_pair + dk_e
                dv_pair = dv_e if e == 0 else dv_pair + dv_e
            dk_pair = dk_pair * scale
            if has_acc:
                dk_pair = dk_pair + acc_ref[0, :, sl]
                dv_pair = dv_pair + acc_ref[1, :, sl]
            dkv_ref[0, :, sl] = dk_pair
            dkv_ref[1, :, sl] = dv_pair

    blk = lambda f: pl.BlockSpec((None, P, D), f)
    nxt = lambda i: jnp.minimum(i + 1, nb - 1)
    own = blk(lambda b, r, i: (b, i, r))
    nx = blk(lambda b, r, i: (b, nxt(i), r))
    tab_spec = pl.BlockSpec((None, H, 2 * P, P), lambda b, r, i: (jnp.where(i == nb - 1, 1, 0), 0, 0, 0))
    pair = pl.BlockSpec((2, None, P, D), lambda b, r, i: (0, b, i, r))
    in_specs = [blk(lambda b, r, i: (b, i, 2 * r)), blk(lambda b, r, i: (b, i, 2 * r + 1)),
                own, nx, own, nx, own, nx, own, nx, tab_spec]
    kvv, qv, dov, ov, lv = view(kv, 2 * D), view(q, D), view(do, D), view(o, D), view(lse, D)
    args = [kvv, kvv, qv, qv, dov, dov, ov, ov, lv, lv, tabk]
    if has_acc:
        in_specs.append(pair)
        args.append(dkv_acc.reshape(2, B, L, d * D))
    dkv = pl.pallas_call(
        body, name=name, grid=(B, d, nb), in_specs=in_specs, out_specs=pair,
        out_shape=jax.ShapeDtypeStruct((2, B, L, d * D), F32),
        compiler_params=pltpu.CompilerParams(
            dimension_semantics=("parallel", "parallel", "parallel"),
            vmem_limit_bytes=_vmem(_nbytes((H, P, 2 * P), F32) + 16 * _nbytes((P, D), F32))),
    )(*args)
    return dkv.reshape(2, B, S, D)


def _mesh_place():
    x, y, c = lax.axis_index("x"), lax.axis_index("y"), lax.axis_index("c")
    return x, y, c, (x, y, 1 - c), [(1 - x, y), (x, 1 - y), (1 - x, 1 - y)]


def _rdma(src, dst, send_sems, recv_sems, idx, to):
    return pltpu.make_async_remote_copy(
        src_ref=src, dst_ref=dst, send_sem=send_sems.at[idx], recv_sem=recv_sems.at[idx],
        device_id=to, device_id_type=pl.DeviceIdType.MESH)


def _comm_call(body, name, args, out_shape, n_sems, n_local):
    any_spec = pl.BlockSpec(memory_space=pl.ANY)
    return pl.pallas_call(
        body, name=name, in_specs=[any_spec] * len(args), out_specs=[any_spec] * len(out_shape),
        out_shape=out_shape,
        scratch_shapes=[pltpu.SemaphoreType.DMA(n_sems), pltpu.SemaphoreType.DMA(n_sems),
                        pltpu.SemaphoreType.DMA((n_local,))],
        compiler_params=pltpu.CompilerParams(has_side_effects=True),
    )(*args)


def _all_gather(arrays, name):
    n = len(arrays)

    def body(*refs):
        ins, outs = refs[:n], refs[n:2 * n]
        send_sems, recv_sems, loc_sems = refs[2 * n:]
        x, y, c, sib, chips = _mesh_place()
        me = (x, y, c)

        def rows(a, dev):
            return outs[a].at[4 * dev[0] + 2 * dev[1] + dev[2]]

        def copy(a, k, block, to, src=None):
            return _rdma(rows(a, block) if src is None else src, rows(a, block), send_sems, recv_sems, (a, k), to)

        local, first, passed = [], [], []
        for a in range(n):
            cp = pltpu.make_async_copy(ins[a], rows(a, me), loc_sems.at[a])
            cp.start()
            local.append(cp)
            first.append(copy(a, 0, me, sib, src=ins[a]))
            first += [copy(a, 1 + j, me, (*chip, c), src=ins[a]) for j, chip in enumerate(chips)]
        for cp in first:
            cp.start()
        for j, chip in enumerate(chips):
            for a in range(n):
                copy(a, 1 + j, (*chip, c), me).wait_recv()
                cp = copy(a, 4 + j, (*chip, c), sib)
                cp.start()
                passed.append(cp)
        for a in range(n):
            copy(a, 0, sib, me).wait_recv()
            for j, chip in enumerate(chips):
                copy(a, 4 + j, (*chip, 1 - c), me).wait_recv()
        for cp in first + passed:
            cp.wait_send()
        for cp in local:
            cp.wait()

    out_shape = [jax.ShapeDtypeStruct((N_DEV,) + g.shape, g.dtype) for g in arrays]
    return list(_comm_call(body, name, arrays, out_shape, (n, N_DEV - 1), n))


def _core_pair_exchange(halves, gathered, name):
    nh, ng = len(halves), len(gathered)

    def body(*refs):
        h_in, g_in = refs[:nh], refs[nh:nh + ng]
        outs = refs[nh + ng:nh + ng + nh + ng]
        theirs, g_out = outs[:nh], outs[nh:]
        send_sems, recv_sems, loc_sems = refs[nh + ng + nh + ng:]
        x, y, c, sib, chips = _mesh_place()
        me = 4 * x + 2 * y + c
        peers = [sib] + [(*chip, pc) for chip in chips for pc in (c, 1 - c)]
        nchip = N_DEV // 2
        gbase = nh * nchip
        local, sends = [], []
        for a in range(nh):
            for ch in range(nchip):
                cp = _rdma(h_in[a].at[ch, 1 - c], theirs[a].at[ch], send_sems, recv_sems, a * nchip + ch, sib)
                cp.start()
                sends.append(cp)
        for g in range(ng):
            cp = pltpu.make_async_copy(g_in[g], g_out[g].at[me], loc_sems.at[g])
            cp.start()
            local.append(cp)
            for k, peer in enumerate(peers):
                cp = _rdma(g_in[g], g_out[g].at[me], send_sems, recv_sems, gbase + g * (N_DEV - 1) + k, peer)
                cp.start()
                sends.append(cp)
        for cp in sends:
            cp.wait_send()
        for a in range(nh):
            for ch in range(nchip):
                _rdma(h_in[a].at[ch, 1 - c], theirs[a].at[ch], send_sems, recv_sems, a * nchip + ch, sib).wait_recv()
        for g in range(ng):
            for k, peer in enumerate(peers):
                pid = 4 * peer[0] + 2 * peer[1] + peer[2]
                _rdma(g_in[g], g_out[g].at[pid], send_sems, recv_sems, gbase + g * (N_DEV - 1) + k, peer).wait_recv()
        for cp in local:
            cp.wait()

    half = [jax.ShapeDtypeStruct((h.shape[0],) + h.shape[2:], h.dtype) for h in halves]
    out_shape = half + [jax.ShapeDtypeStruct((N_DEV,) + g.shape, g.dtype) for g in gathered]
    outs = _comm_call(body, name, list(halves) + list(gathered), out_shape,
                      (nh * (N_DEV // 2) + ng * (N_DEV - 1),), max(ng, 1))
    return list(outs[:nh]), list(outs[nh:])


def _chip_exchange(arrays, name):
    n = len(arrays)

    def body(*refs):
        ins, outs = refs[:n], refs[n:2 * n]
        send_sems, recv_sems, _ = refs[2 * n:]
        x, y, c, sib, chips = _mesh_place()
        sends = []
        for a in range(n):
            for j, chip in enumerate(chips):
                cp = _rdma(ins[a].at[2 * chip[0] + chip[1]], outs[a].at[j], send_sems, recv_sems, (a, j), (*chip, c))
                cp.start()
                sends.append(cp)
        for cp in sends:
            cp.wait_send()
        for a in range(n):
            for j, chip in enumerate(chips):
                _rdma(ins[a].at[j], outs[a].at[j], send_sems, recv_sems, (a, j), (*chip, c)).wait_recv()

    out_shape = [jax.ShapeDtypeStruct((3,) + g.shape[1:], g.dtype) for g in arrays]
    return list(_comm_call(body, name, arrays, out_shape, (n, 3), 1))


def _pair_sum(halves, theirs, core, name, tr=256):
    nchip, _, R, C = halves.shape
    tr = _row_tile(R, tr)

    def body(core_ref, h_ref, t_ref, o_ref):
        o_ref[...] = (h_ref[...] + t_ref[...]).astype(BF16)

    grid_spec = pltpu.PrefetchScalarGridSpec(
        num_scalar_prefetch=1, grid=(nchip, R // tr),
        in_specs=[pl.BlockSpec((None, None, tr, C), lambda ch, i, core_ref: (ch, core_ref[0], i, 0)),
                  pl.BlockSpec((None, tr, C), lambda ch, i, core_ref: (ch, i, 0))],
        out_specs=pl.BlockSpec((None, tr, C), lambda ch, i, core_ref: (ch, i, 0)))
    return pl.pallas_call(
        body, name=name, grid_spec=grid_spec, out_shape=jax.ShapeDtypeStruct((nchip, R, C), BF16),
        compiler_params=pltpu.CompilerParams(
            dimension_semantics=("parallel", "parallel"), vmem_limit_bytes=_vmem(3 * _nbytes((tr, C), F32))),
    )(core, halves, theirs)


def _chip_sum(own, recv, chip, name, tr=256):
    _, R, C = own.shape
    tr = _row_tile(R, tr)

    def body(chip_ref, o_ref, r_ref, out_ref):
        acc = o_ref[...].astype(F32)
        for j in range(3):
            acc = acc + r_ref[j].astype(F32)
        out_ref[...] = acc

    grid_spec = pltpu.PrefetchScalarGridSpec(
        num_scalar_prefetch=1, grid=(R // tr,),
        in_specs=[pl.BlockSpec((None, tr, C), lambda i, chip_ref: (chip_ref[0], i, 0)),
                  pl.BlockSpec((3, tr, C), lambda i, chip_ref: (0, i, 0))],
        out_specs=pl.BlockSpec((tr, C), lambda i, chip_ref: (i, 0)))
    return pl.pallas_call(
        body, name=name, grid_spec=grid_spec, out_shape=jax.ShapeDtypeStruct((R, C), F32),
        compiler_params=pltpu.CompilerParams(
            dimension_semantics=("parallel",), vmem_limit_bytes=_vmem(4 * _nbytes((tr, C), F32))),
    )(chip, own, recv)


def _row_tile(rows, target):
    best = rows
    for t in range(8, min(rows, target) + 1, 8):
        if rows % t == 0:
            best = t
    return best


def _sum_slots(recv, name, tr=128):
    n, R, C = recv.shape
    tr = _row_tile(R, tr if recv.dtype == F32 else 2 * tr)

    def body(r_ref, o_ref):
        acc = r_ref[0].astype(F32)
        for k in range(1, n):
            acc = acc + r_ref[k].astype(F32)
        o_ref[...] = acc

    return pl.pallas_call(
        body, name=name, grid=(R // tr,),
        in_specs=[pl.BlockSpec((n, tr, C), lambda i: (0, i, 0))],
        out_specs=pl.BlockSpec((tr, C), lambda i: (i, 0)),
        out_shape=jax.ShapeDtypeStruct((R, C), F32),
        compiler_params=pltpu.CompilerParams(
            dimension_semantics=("parallel",), vmem_limit_bytes=_vmem(10 * _nbytes((tr, C), F32))),
    )(recv)


def _adamw(g, w, m, v, name, tr=256):
    R, C = g.shape
    tr = _row_tile(R, tr)

    def body(g_ref, w_ref, m_ref, v_ref, d_ref, nm_ref, nv_ref):
        gv = g_ref[...]
        nm = ADAM_B1 * m_ref[...] + (1.0 - ADAM_B1) * gv
        nv = ADAM_B2 * v_ref[...] + (1.0 - ADAM_B2) * (gv * gv)
        m_hat = nm / (1.0 - ADAM_B1 ** ADAM_STEP)
        v_hat = nv / (1.0 - ADAM_B2 ** ADAM_STEP)
        d_ref[...] = -ADAM_LR * (m_hat / (jnp.sqrt(v_hat) + ADAM_EPS) + ADAM_WD * w_ref[...])
        nm_ref[...] = nm
        nv_ref[...] = nv

    row = pl.BlockSpec((tr, C), lambda i: (i, 0))
    return pl.pallas_call(
        body, name=name, grid=(R // tr,), in_specs=[row] * 4, out_specs=[row] * 3,
        out_shape=[jax.ShapeDtypeStruct((R, C), F32)] * 3,
        compiler_params=pltpu.CompilerParams(
            dimension_semantics=("parallel",), vmem_limit_bytes=_vmem(8 * _nbytes((tr, C), F32))),
    )(g, w, m, v)


def _pack_rows(parts):
    flat = jnp.concatenate([p.reshape(-1).astype(F32) for p in parts])
    rows = -(-flat.shape[0] // (8 * LANE)) * 8
    return jnp.pad(flat, (0, rows * LANE - flat.shape[0])).reshape(rows, LANE)


def _unpack(flat, shapes):
    out, off = [], 0
    for shp in shapes:
        n = math.prod(shp)
        out.append(flat[off:off + n].reshape(shp))
        off += n
    return out


def kernel(x, a_norm, a_w_in, a_conv, a_w_out, kv_norm, w_kv, b_norm, b_w_q, b_w_o, rel_bias, ffn_norm, ffn_w_up, ffn_conv, ffn_conv_b, ffn_w_down, final_norm, loss_target, m_a_norm, m_a_w_in, m_a_conv, m_a_w_out, m_kv_norm, m_w_kv, m_b_norm, m_b_w_q, m_b_w_o, m_rel_bias, m_ffn_norm, m_ffn_w_up, m_ffn_conv, m_ffn_conv_b, m_ffn_w_down, m_final_norm, v_a_norm, v_a_w_in, v_a_conv, v_a_w_out, v_kv_norm, v_w_kv, v_b_norm, v_b_w_q, v_b_w_o, v_rel_bias, v_ffn_norm, v_ffn_w_up, v_ffn_conv, v_ffn_conv_b, v_ffn_w_down, v_final_norm):
    B, S, D = x.shape
    T = B * S
    F = ffn_w_down.shape[1] * N_DEV
    me = 4 * lax.axis_index("x") + 2 * lax.axis_index("y") + lax.axis_index("c")

    big_shards = [a_w_in[0].T, a_w_out[0], w_kv.T, b_w_q[0], b_w_o[0],
                  ffn_w_up[0].T, ffn_w_up[1].T, ffn_w_down[0], ffn_w_down[1]]
    small_shapes = [a_norm.shape, a_conv.shape, ffn_conv.shape]
    small_pack = _pack_rows([a_norm, a_conv, ffn_conv])
    gathered = _all_gather([s.astype(BF16) for s in big_shards] + [small_pack], "gather_weights")
    win_t, wout, wkv_t, wq, wo, wup0_t, wup1_t, wdn0, wdn1 = [
        g.reshape(-1, D) for g in gathered[:9]]
    wup_t, wdn = (wup0_t, wup1_t), (wdn0, wdn1)
    smalls = [_unpack(gathered[9][j].reshape(-1), small_shapes) for j in range(N_DEV)]
    a_norm_f = jnp.concatenate([s[0] for s in smalls], axis=-1)
    a_conv_f = jnp.concatenate([s[1] for s in smalls], axis=-1)[0]
    ffn_conv_f = jnp.concatenate([s[2] for s in smalls], axis=-1)

    x2 = x.reshape(T, D)
    (xn0,) = _rmsnorm_fwd(x2, a_norm_f, "a_norm_fwd")
    bch = _matmul(xn0, win_t, mode="nt", out_dtype=BF16, name="a_in_proj").reshape(B, S, 3 * D)
    gated = _shortconv_fwd(bch, a_conv_f, "a_gate_fwd").reshape(T, D)
    h1 = _matmul(gated, wout, mode="nn", out_dtype=F32, add=x2, name="a_out_proj")

    def ffn_fwd(h, l):
        (xn,) = _rmsnorm_fwd(h, ffn_norm[l:l + 1], f"ffn{l}_norm_fwd")
        u0 = _matmul(xn, wup_t[l], mode="nt", out_dtype=BF16, name=f"ffn{l}_up").reshape(B, S, 2 * F)
        act = _ffn_gate_fwd(u0, ffn_conv_f[l], ffn_conv_b[l:l + 1], f"ffn{l}_gate_fwd").reshape(T, F)
        out = _matmul(act, wdn[l], mode="nn", out_dtype=F32, add=h, name=f"ffn{l}_down", tk=2816)
        return out, (xn, u0, act)

    h2, ffn0_saved = ffn_fwd(h1, 0)
    kvn, xnb = _rmsnorm_fwd(h2, jnp.stack([kv_norm, b_norm[0]]), "kv_b_norm_fwd")
    kv = _matmul(kvn, wkv_t, mode="nt", out_dtype=F32, name="kv_proj").reshape(B, S, 2 * D)
    q = _matmul(xnb, wq, mode="nn", out_dtype=F32, name="q_proj").reshape(B, S, D)

    tables = [_band_tables(w, d) for (w, d) in DILATED_BRANCHES]
    onehots = jnp.stack([t[0] for t in tables])
    P = ATT_BLOCK
    bias_vals = _bias_lookup(rel_bias.T, onehots, "rel_bias_lookup").reshape(3, N_HEADS, P, 2 * P)
    in_cur = (jnp.arange(2 * P) >= P)[None, :]
    tabs = []
    for bi in range(3):
        band = tables[bi][1]
        gen = jnp.where(band[None], bias_vals[bi], NEG)
        fst = jnp.where((band & in_cur)[None], bias_vals[bi], NEG)
        tabs.append(jnp.stack([fst, gen]))
    tabs = jnp.stack(tabs)

    attn3, lse3 = _attn_fwd(q, kv, tabs, "attn_fwd")
    attn = attn3.reshape(T, D)
    h3 = _matmul(attn, wo, mode="nn", out_dtype=F32, add=h2, name="o_proj")
    h4, ffn1_saved = ffn_fwd(h3, 1)

    dh4, dh4b, d_final_norm, sq = _final_loss_bwd(h4, final_norm.reshape(1, D), loss_target.reshape(T, D), "loss_bwd")
    loss_part = 0.5 * jnp.sum(sq) / D

    def ffn_bwd(dh_out, dh_out_b, h_in, saved, l):
        xn, u0, act = saved
        dact = _matmul(dh_out_b, wdn[l], mode="nt", out_dtype=BF16, name=f"ffn{l}_down_dx", tn=1408).reshape(B, S, F)
        d_wdn = _matmul(act, dh_out_b, mode="tn", out_dtype=F32, name=f"ffn{l}_down_dw", tm=1408, tn=1024, tk=2048)
        du0, d_conv, d_conv_b = _ffn_gate_bwd(u0, dact, ffn_conv_f[l], ffn_conv_b[l:l + 1], f"ffn{l}_gate_bwd")
        du0 = du0.reshape(2, T, F)
        dxn = _matmul(du0, wup_t[l], mode="nn", out_dtype=F32, a_parts=2, name=f"ffn{l}_up_dx", tk=2816)
        d_wup_t = _matmul(du0, xn, mode="tn", out_dtype=F32, a_parts=2, name=f"ffn{l}_up_dw", tm=1408, tn=1024, tk=2048)
        dh_in, dh_in_b, d_norm = _rmsnorm_bwd(h_in, ffn_norm[l:l + 1], [dxn], dh_out, f"ffn{l}_norm_bwd")
        return (dh_in, dh_in_b), (d_wdn, d_wup_t, d_conv, d_conv_b, d_norm)

    (dh3, dh3b), ffn1_grads = ffn_bwd(dh4, dh4b, h3, ffn1_saved, 1)

    dattn = _matmul(dh3b, wo, mode="nt", out_dtype=F32, name="o_proj_dx").reshape(B, S, D)
    d_wo = _matmul(attn, dh3b, mode="tn", out_dtype=F32, name="o_proj_dw", tm=1024, tn=1024, tk=2048)
    dq, dkv, dtab = _attn_bwd(q, kv, dattn, attn3, lse3, tabs, "attn_bwd")
    dtabs = dtab.transpose(1, 0, 2, 3, 4).reshape(3, N_HEADS, P * 2 * P)
    d_rel_bias = _bias_grad(dtabs, onehots, "rel_bias_grad").T
    dq2 = dq.reshape(T, D)
    dkv2 = dkv.reshape(2, T, D)
    dxnb = _matmul(dq2, wq, mode="nt", out_dtype=F32, name="q_proj_dx")
    d_wq = _matmul(xnb, dq2, mode="tn", out_dtype=F32, name="q_proj_dw", tm=1024, tn=1024, tk=2048)
    dkvn = _matmul(dkv2, wkv_t, mode="nn", out_dtype=F32, a_parts=2, name="kv_proj_dx")
    d_wkv_t = _matmul(dkv2, kvn, mode="tn", out_dtype=F32, a_parts=2, name="kv_proj_dw", tm=1024, tn=1024, tk=2048)
    dh2, dh2b, d_kvb_norm = _rmsnorm_bwd(h2, jnp.stack([kv_norm, b_norm[0]]), [dkvn, dxnb], dh3, "kv_b_norm_bwd")

    (dh1, dh1b), ffn0_grads = ffn_bwd(dh2, dh2b, h1, ffn0_saved, 0)

    dgated = _matmul(dh1b, wout, mode="nt", out_dtype=BF16, name="a_out_proj_dx").reshape(B, S, D)
    d_wout = _matmul(gated, dh1b, mode="tn", out_dtype=F32, name="a_out_proj_dw", tm=1024, tn=1024, tk=2048)
    dbch, d_a_conv = _shortconv_bwd(bch, dgated, a_conv_f, "a_gate_bwd")
    dbch = dbch.reshape(3, T, D)
    dxn0 = _matmul(dbch, win_t, mode="nn", out_dtype=F32, a_parts=3, name="a_in_proj_dx")
    d_win_t = _matmul(dbch, xn0, mode="tn", out_dtype=F32, a_parts=3, name="a_in_proj_dw", tm=1024, tn=1024, tk=2048)
    grad_x, _, d_a_norm = _rmsnorm_bwd(x2, a_norm_f, [dxn0], dh1, "a_norm_bwd")

    big_grads = [d_win_t, d_wout, d_wkv_t, d_wq, d_wo, ffn0_grads[1], ffn1_grads[1], ffn0_grads[0], ffn1_grads[0]]
    small_full = [d_a_norm, d_a_conv, jnp.stack([ffn0_grads[2], ffn1_grads[2]]),
                  d_kvb_norm[0], d_kvb_norm[1], d_rel_bias, jnp.concatenate([ffn0_grads[4], ffn1_grads[4]]),
                  jnp.concatenate([ffn0_grads[3], ffn1_grads[3]]), d_final_norm, loss_part]
    small_full_shapes = [(1, D), (3, D), (2, 3, 2 * F), (D,), (1, D), rel_bias.shape, (2, D), (2, 2 * F), (D,), ()]
    halves = [g.reshape(N_DEV // 2, 2, g.shape[0] // N_DEV, D) for g in big_grads]
    theirs, gath = _core_pair_exchange(halves, [_pack_rows(small_full)], "exchange_grads_core")
    core_idx = lax.axis_index("c").astype(jnp.int32).reshape(1)
    chip_idx = (2 * lax.axis_index("x") + lax.axis_index("y")).astype(jnp.int32).reshape(1)
    chip_sums = [_pair_sum(h_, t_, core_idx, f"grad_core_sum_{i}") for i, (h_, t_) in enumerate(zip(halves, theirs))]
    recv = _chip_exchange(chip_sums, "exchange_grads_chip")
    small_sum = _sum_slots(gath[0], "small_grad_sum").reshape(-1)
    (g_a_norm_f, g_a_conv_f, g_ffn_conv_f, g_kv_norm, g_b_norm, g_rel_bias, g_ffn_norm, g_ffn_conv_b,
     g_final_norm, loss) = _unpack(small_sum, small_full_shapes)

    def my_cols(full, width):
        return lax.dynamic_slice_in_dim(full, me * width, width, axis=full.ndim - 1)

    g_a_norm = my_cols(g_a_norm_f, D // N_DEV)
    g_a_conv = my_cols(g_a_conv_f, D // N_DEV)[None]
    g_ffn_conv = my_cols(g_ffn_conv_f, 2 * F // N_DEV)

    big_w = [(a_w_in, m_a_w_in, v_a_w_in, True), (a_w_out, m_a_w_out, v_a_w_out, False),
             (w_kv, m_w_kv, v_w_kv, True), (b_w_q, m_b_w_q, v_b_w_q, False), (b_w_o, m_b_w_o, v_b_w_o, False),
             (ffn_w_up[0], m_ffn_w_up[0], v_ffn_w_up[0], True), (ffn_w_up[1], m_ffn_w_up[1], v_ffn_w_up[1], True),
             (ffn_w_down[0], m_ffn_w_down[0], v_ffn_w_down[0], False),
             (ffn_w_down[1], m_ffn_w_down[1], v_ffn_w_down[1], False)]
    big_out = []
    for i, (w, m, v, transposed) in enumerate(big_w):
        g = _chip_sum(chip_sums[i], recv[i], chip_idx, f"grad_sum_{i}")
        if transposed:
            g = g.T
        w2, m2, v2 = (t.reshape(g.shape) for t in (w, m, v))
        delta, nm, nv = _adamw(g, w2, m2, v2, f"adamw_{i}")
        big_out.append(tuple(t.reshape(w.shape) for t in (g, delta, nm, nv)))

    def pair(i, j):
        return tuple(jnp.stack([big_out[i][t], big_out[j][t]]) for t in range(4))

    small_w = [(a_norm, m_a_norm, v_a_norm, g_a_norm), (a_conv, m_a_conv, v_a_conv, g_a_conv),
               (ffn_conv, m_ffn_conv, v_ffn_conv, g_ffn_conv), (kv_norm, m_kv_norm, v_kv_norm, g_kv_norm),
               (b_norm, m_b_norm, v_b_norm, g_b_norm), (rel_bias, m_rel_bias, v_rel_bias, g_rel_bias),
               (ffn_norm, m_ffn_norm, v_ffn_norm, g_ffn_norm),
               (ffn_conv_b, m_ffn_conv_b, v_ffn_conv_b, g_ffn_conv_b),
               (final_norm, m_final_norm, v_final_norm, g_final_norm)]
    shapes = [t[0].shape for t in small_w]
    packed = [_pack_rows([t[i].reshape(t[0].shape) for t in small_w]) for i in (3, 0, 1, 2)]
    s_delta, s_nm, s_nv = _adamw(*packed, "adamw_small")
    s_g = [t[3].reshape(t[0].shape) for t in small_w]
    s_d, s_m, s_v = (_unpack(t.reshape(-1), shapes) for t in (s_delta, s_nm, s_nv))
    small_out = [(s_g[i], s_d[i], s_m[i], s_v[i]) for i in range(len(small_w))]

    per_weight = [small_out[0], big_out[0], small_out[1], big_out[1], small_out[3], big_out[2], small_out[4],
                  big_out[3], big_out[4], small_out[5], small_out[6], pair(5, 6), small_out[2], small_out[7],
                  pair(7, 8), small_out[8]]
    outs = [loss, grad_x.reshape(B, S, D)]
    for t in range(4):
        outs.extend(pw[t] for pw in per_weight)
    return tuple(outs)
```

```python
import functools
import math

import jax
import jax.numpy as jnp
from jax import lax
from jax.experimental import pallas as pl
from jax.experimental.pallas import tpu as pltpu

F32 = jnp.float32
BF16 = jnp.bfloat16

N_DEV = 8
N_HEADS = 16
ATT_BLOCK = 128
ATT_UNROLL = 8
DILATED_BRANCHES = ((128, 1), (512, 4), (2048, 16))
REL_BUCKETS = 32
REL_MAX_DISTANCE = 2048
RMS_EPS = 1e-6
ADAM_LR = 0.001
ADAM_B1 = 0.9
ADAM_B2 = 0.999
ADAM_EPS = 1e-08
ADAM_WD = 0.01
ADAM_STEP = 10

LANE = 128
HALO = 16
NEG = -1e30
VMEM_CAP = 56 << 20


def _vmem(block_bytes):
    return int(min(VMEM_CAP, max(32 << 20, 3 * block_bytes + (8 << 20))))


def _nbytes(shape, dtype):
    return math.prod(shape) * jnp.dtype(dtype).itemsize


def _tile(dim, target):
    best = None
    t = LANE
    while t <= min(dim, target):
        if dim % t == 0:
            best = t
        t += LANE
    return best if best is not None else dim


def _matmul(a, b, *, mode, out_dtype, name, add=None, after=None, a_parts=1, tm=1024, tn=512, tk=1024):
    P = a_parts
    if mode == "nn":
        M, K = (a.shape[0], a.shape[1]) if P == 1 else (a.shape[1], a.shape[2] * P)
        N = b.shape[1]
    elif mode == "nt":
        assert P == 1
        M, K = a.shape
        N = b.shape[0]
    else:
        K = a.shape[0] if P == 1 else a.shape[1]
        M = a.shape[1] if P == 1 else a.shape[2] * P
        N = b.shape[1]
    tm = _tile(M // P if mode == "tn" else M, tm)
    tn = _tile(N, tn)
    tk = _tile(K // P if mode == "nn" else K, tk)
    nm, nn_, nk = M // tm, N // tn, K // tk

    if mode == "nn":
        if P == 1:
            a_spec = pl.BlockSpec((tm, tk), lambda i, j, k: (i, k))
        else:
            nkp = nk // P
            a_spec = pl.BlockSpec((None, tm, tk), lambda i, j, k: (k // nkp, i, k % nkp))
        b_spec = pl.BlockSpec((tk, tn), lambda i, j, k: (k, j))
        dims = (((1,), (0,)), ((), ()))
    elif mode == "nt":
        a_spec = pl.BlockSpec((tm, tk), lambda i, j, k: (i, k))
        b_spec = pl.BlockSpec((tn, tk), lambda i, j, k: (j, k))
        dims = (((1,), (1,)), ((), ()))
    else:
        if P == 1:
            a_spec = pl.BlockSpec((tk, tm), lambda i, j, k: (k, i))
        else:
            nmp = nm // P
            a_spec = pl.BlockSpec((None, tk, tm), lambda i, j, k: (i // nmp, k, i % nmp))
        b_spec = pl.BlockSpec((tk, tn), lambda i, j, k: (k, j))
        dims = (((0,), (0,)), ((), ()))
    o_spec = pl.BlockSpec((tm, tn), lambda i, j, k: (i, j))
    in_specs = [a_spec, b_spec]
    args = [a, b]
    if add is not None:
        in_specs.append(o_spec)
        args.append(add)
    has_add = add is not None
    if after is not None:
        in_specs.append(pl.BlockSpec(memory_space=pl.ANY))
        args.append(after)
    n_extra = len(args) - 2

    direct = nk > 1 and out_dtype == F32 and not has_add

    def body(a_ref, b_ref, *rest):
        add_ref = rest[0] if has_add else None
        o_ref = rest[n_extra]
        acc_ref = o_ref if direct or nk == 1 else rest[-1]
        k = pl.program_id(2)

        def part():
            return lax.dot_general(a_ref[...].astype(BF16), b_ref[...].astype(BF16), dims,
                                   preferred_element_type=F32)

        def finish(r):
            if has_add:
                r = r + add_ref[...].astype(F32)
            o_ref[...] = r.astype(out_dtype)

        if nk == 1:
            finish(part())
        else:
            @pl.when(k == 0)
            def _():
                acc_ref[...] = part()

            @pl.when(jnp.logical_and(k > 0, jnp.logical_or(k < nk - 1, direct)))
            def _():
                acc_ref[...] += part()

            if not direct:
                @pl.when(k == nk - 1)
                def _():
                    finish(acc_ref[...] + part())

    blk = (_nbytes((tm, tk), a.dtype) + _nbytes((tk, tn), b.dtype) + _nbytes((tm, tn), out_dtype)
           + (_nbytes((tm, tn), add.dtype) if has_add else 0)) * 2 + 3 * _nbytes((tm, tn), F32)
    return pl.pallas_call(
        body, name=name, grid=(nm, nn_, nk),
        in_specs=in_specs, out_specs=o_spec,
        out_shape=jax.ShapeDtypeStruct((M, N), out_dtype),
        scratch_shapes=[] if direct or nk == 1 else [pltpu.VMEM((tm, tn), F32)],
        compiler_params=pltpu.CompilerParams(
            dimension_semantics=("parallel", "parallel", "arbitrary"), vmem_limit_bytes=_vmem(blk)),
    )(*args)


def _rmsnorm_fwd(x, gains, name, tm=512):
    T, D = x.shape
    n = gains.shape[0]
    tm = _tile(T, tm)

    def body(x_ref, g_ref, *o_refs):
        xv = x_ref[...]
        xhat = xv * lax.rsqrt(jnp.mean(xv * xv, axis=-1, keepdims=True) + RMS_EPS)
        for i in range(n):
            o_refs[i][...] = (xhat * g_ref[i:i + 1, :]).astype(BF16)

    row = pl.BlockSpec((tm, D), lambda i: (i, 0))
    outs = pl.pallas_call(
        body, name=name, grid=(T // tm,),
        in_specs=[row, pl.BlockSpec((n, D), lambda i: (0, 0))],
        out_specs=[row] * n,
        out_shape=[jax.ShapeDtypeStruct((T, D), BF16)] * n,
        compiler_params=pltpu.CompilerParams(
            dimension_semantics=("parallel",), vmem_limit_bytes=_vmem(4 * _nbytes((tm, D), F32))),
    )(x, gains)
    return tuple(outs)


def _rmsnorm_bwd(x, gains, dxns, dres, name, tm=512):
    T, D = x.shape
    n = gains.shape[0]
    tm = _tile(T, tm)

    def body(x_ref, g_ref, *rest):
        dxn_refs = rest[:n]
        dres_ref, dx_ref, dxb_ref, dg_ref = rest[n:]
        xv = x_ref[...]
        rstd = lax.rsqrt(jnp.mean(xv * xv, axis=-1, keepdims=True) + RMS_EPS)
        xhat = xv * rstd
        dx = dres_ref[...]

        @pl.when(pl.program_id(0) == 0)
        def _():
            dg_ref[...] = jnp.zeros_like(dg_ref)

        for i in range(n):
            dy = dxn_refs[i][...].astype(F32)
            dg_ref[i:i + 1, :] += jnp.sum(dy * xhat, axis=0, keepdims=True)
            dxh = dy * g_ref[i:i + 1, :]
            dx = dx + rstd * (dxh - xhat * jnp.mean(dxh * xhat, axis=-1, keepdims=True))
        dx_ref[...] = dx
        dxb_ref[...] = dx.astype(BF16)

    row = pl.BlockSpec((tm, D), lambda i: (i, 0))
    par = pl.BlockSpec((n, D), lambda i: (0, 0))
    return pl.pallas_call(
        body, name=name, grid=(T // tm,),
        in_specs=[row, par] + [row] * n + [row],
        out_specs=[row, row, par],
        out_shape=[jax.ShapeDtypeStruct((T, D), F32), jax.ShapeDtypeStruct((T, D), BF16),
                   jax.ShapeDtypeStruct((n, D), F32)],
        compiler_params=pltpu.CompilerParams(
            dimension_semantics=("arbitrary",), vmem_limit_bytes=_vmem((4 + n) * _nbytes((tm, D), F32))),
    )(x, gains, *dxns, dres)


def _final_loss_bwd(h, gain, target, name, tm=512):
    T, D = h.shape
    tm = _tile(T, tm)

    def body(h_ref, g_ref, t_ref, dh_ref, dhb_ref, dg_ref, sq_ref):
        xv = h_ref[...]
        rstd = lax.rsqrt(jnp.mean(xv * xv, axis=-1, keepdims=True) + RMS_EPS)
        xhat = xv * rstd
        err = xhat * g_ref[...] - t_ref[...]

        @pl.when(pl.program_id(0) == 0)
        def _():
            dg_ref[...] = jnp.zeros_like(dg_ref)
            sq_ref[...] = jnp.zeros_like(sq_ref)

        sq_ref[...] += jnp.sum(err * err, axis=0, keepdims=True)
        dy = err * (1.0 / D)
        dg_ref[...] += jnp.sum(dy * xhat, axis=0, keepdims=True)
        dxh = dy * g_ref[...]
        dh = rstd * (dxh - xhat * jnp.mean(dxh * xhat, axis=-1, keepdims=True))
        dh_ref[...] = dh
        dhb_ref[...] = dh.astype(BF16)

    row = pl.BlockSpec((tm, D), lambda i: (i, 0))
    par = pl.BlockSpec((1, D), lambda i: (0, 0))
    return pl.pallas_call(
        body, name=name, grid=(T // tm,),
        in_specs=[row, par, row], out_specs=[row, row, par, par],
        out_shape=[jax.ShapeDtypeStruct((T, D), F32), jax.ShapeDtypeStruct((T, D), BF16),
                   jax.ShapeDtypeStruct((1, D), F32), jax.ShapeDtypeStruct((1, D), F32)],
        compiler_params=pltpu.CompilerParams(
            dimension_semantics=("arbitrary",), vmem_limit_bytes=_vmem(5 * _nbytes((tm, D), F32))),
    )(h, gain, target)


def _halo_specs(S, ts, tc, col_off, prev, nxt):
    r = ts // HALO
    last = S // HALO - 1
    specs = []
    if prev:
        specs.append(pl.BlockSpec((None, HALO, tc), lambda j, b, s: (b, jnp.maximum(s * r - 1, 0), col_off + j)))
    specs.append(pl.BlockSpec((None, ts, tc), lambda j, b, s: (b, s, col_off + j)))
    if nxt:
        specs.append(pl.BlockSpec((None, HALO, tc), lambda j, b, s: (b, jnp.minimum((s + 1) * r, last), col_off + j)))
    return specs


def _ext(prev_ref, main_ref, next_ref, first, last):
    main = main_ref[...].astype(F32)
    zeros = jnp.zeros((HALO, main.shape[1]), F32)
    top = zeros if prev_ref is None else jnp.where(first, 0.0, prev_ref[...].astype(F32))
    bot = zeros if next_ref is None else jnp.where(last, 0.0, next_ref[...].astype(F32))
    return jnp.concatenate([top, main, bot], axis=0)


def _shift(xe, k):
    return pltpu.roll(xe, k % xe.shape[0], axis=0)


def _cconv(xe, w):
    return w[0:1, :] * _shift(xe, 2) + w[1:2, :] * _shift(xe, 1) + w[2:3, :] * xe


def _cconv_t(de, w):
    return w[2:3, :] * de + w[1:2, :] * _shift(de, -1) + w[0:1, :] * _shift(de, -2)


def _main(xe, ts):
    return xe[HALO:HALO + ts, :]


def _conv_wgrad(dw_ref, de, xe, ts):
    d = _main(de, ts)
    dw_ref[0:1, :] += jnp.sum(d * _main(_shift(xe, 2), ts), axis=0, keepdims=True)
    dw_ref[1:2, :] += jnp.sum(d * _main(_shift(xe, 1), ts), axis=0, keepdims=True)
    dw_ref[2:3, :] += jnp.sum(d * _main(xe, ts), axis=0, keepdims=True)


def _gate_tiles(S, C):
    return _tile(S, 512), _tile(C, 512)


def _shortconv_fwd(bch, conv_w, name):
    B, S, D3 = bch.shape
    D = D3 // 3
    ts, tc = _gate_tiles(S, D)
    nj, ns = D // tc, S // ts

    def body(b_ref, cp_ref, c_ref, hp_ref, h_ref, w_ref, o_ref):
        first = pl.program_id(2) == 0
        ce = _ext(cp_ref, c_ref, None, first, False)
        he = _ext(hp_ref, h_ref, None, first, False)
        cv = _main(_cconv(ce * he, w_ref[...]), ts)
        o_ref[...] = (b_ref[...].astype(F32) * cv).astype(BF16)

    in_specs = (_halo_specs(S, ts, tc, 0, False, False) + _halo_specs(S, ts, tc, nj, True, False)
                + _halo_specs(S, ts, tc, 2 * nj, True, False) + [pl.BlockSpec((3, tc), lambda j, b, s: (0, j))])
    return pl.pallas_call(
        body, name=name, grid=(nj, B, ns), in_specs=in_specs,
        out_specs=pl.BlockSpec((None, ts, tc), lambda j, b, s: (b, s, j)),
        out_shape=jax.ShapeDtypeStruct((B, S, D), BF16),
        compiler_params=pltpu.CompilerParams(
            dimension_semantics=("parallel", "parallel", "parallel"),
            vmem_limit_bytes=_vmem(12 * _nbytes((ts + 2 * HALO, tc), F32))),
    )(bch, bch, bch, bch, bch, conv_w)


def _shortconv_bwd(bch, dg, conv_w, name):
    B, S, D3 = bch.shape
    D = D3 // 3
    ts, tc = _gate_tiles(S, D)
    nj, ns = D // tc, S // ts

    def body(b_ref, bn_ref, cp_ref, c_ref, cn_ref, hp_ref, h_ref, hn_ref, d_ref, dn_ref, w_ref, o_ref, dw_ref):
        s = pl.program_id(2)
        first, last = s == 0, s == ns - 1

        @pl.when(jnp.logical_and(pl.program_id(1) == 0, s == 0))
        def _():
            dw_ref[...] = jnp.zeros_like(dw_ref)

        w = w_ref[...]
        be = _ext(None, b_ref, bn_ref, first, last)
        ce = _ext(cp_ref, c_ref, cn_ref, first, last)
        he = _ext(hp_ref, h_ref, hn_ref, first, last)
        de = _ext(None, d_ref, dn_ref, first, last)
        ch = ce * he
        dcv = de * be
        dch = _main(_cconv_t(dcv, w), ts)
        o_ref[0] = (_main(de, ts) * _main(_cconv(ch, w), ts)).astype(BF16)
        o_ref[1] = (dch * _main(he, ts)).astype(BF16)
        o_ref[2] = (dch * _main(ce, ts)).astype(BF16)
        _conv_wgrad(dw_ref, dcv, ch, ts)

    in_specs = (_halo_specs(S, ts, tc, 0, False, True) + _halo_specs(S, ts, tc, nj, True, True)
                + _halo_specs(S, ts, tc, 2 * nj, True, True) + _halo_specs(S, ts, tc, 0, False, True)
                + [pl.BlockSpec((3, tc), lambda j, b, s: (0, j))])
    return pl.pallas_call(
        body, name=name, grid=(nj, B, ns), in_specs=in_specs,
        out_specs=[pl.BlockSpec((3, None, ts, tc), lambda j, b, s: (0, b, s, j)),
                   pl.BlockSpec((3, tc), lambda j, b, s: (0, j))],
        out_shape=[jax.ShapeDtypeStruct((3, B, S, D), BF16), jax.ShapeDtypeStruct((3, D), F32)],
        compiler_params=pltpu.CompilerParams(
            dimension_semantics=("parallel", "arbitrary", "arbitrary"),
            vmem_limit_bytes=_vmem(24 * _nbytes((ts + 2 * HALO, tc), F32))),
    )(bch, bch, bch, bch, bch, bch, bch, bch, dg, dg, conv_w)


def _sigmoid(x):
    return 1.0 / (1.0 + jnp.exp(-x))


def _ffn_gate_fwd(u0, conv_w, conv_b, name):
    B, S, F2 = u0.shape
    F = F2 // 2
    ts, tc = _gate_tiles(S, F)
    nj, ns = F // tc, S // ts

    def body(gp_ref, g_ref, up_ref, u_ref, wg_ref, wu_ref, bg_ref, bu_ref, o_ref):
        first = pl.program_id(2) == 0
        ug = _main(_cconv(_ext(gp_ref, g_ref, None, first, False), wg_ref[...]), ts) + bg_ref[...]
        uu = _main(_cconv(_ext(up_ref, u_ref, None, first, False), wu_ref[...]), ts) + bu_ref[...]
        o_ref[...] = (ug * _sigmoid(ug) * uu).astype(BF16)

    in_specs = (_halo_specs(S, ts, tc, 0, True, False) + _halo_specs(S, ts, tc, nj, True, False)
                + [pl.BlockSpec((3, tc), lambda j, b, s: (0, j)), pl.BlockSpec((3, tc), lambda j, b, s: (0, nj + j)),
                   pl.BlockSpec((1, tc), lambda j, b, s: (0, j)), pl.BlockSpec((1, tc), lambda j, b, s: (0, nj + j))])
    return pl.pallas_call(
        body, name=name, grid=(nj, B, ns), in_specs=in_specs,
        out_specs=pl.BlockSpec((None, ts, tc), lambda j, b, s: (b, s, j)),
        out_shape=jax.ShapeDtypeStruct((B, S, F), BF16),
        compiler_params=pltpu.CompilerParams(
            dimension_semantics=("parallel", "parallel", "parallel"),
            vmem_limit_bytes=_vmem(12 * _nbytes((ts + 2 * HALO, tc), F32))),
    )(u0, u0, u0, u0, conv_w, conv_w, conv_b, conv_b)


def _ffn_gate_bwd(u0, dact, conv_w, conv_b, name):
    B, S, F2 = u0.shape
    F = F2 // 2
    ts, tc = _gate_tiles(S, F)
    nj, ns = F // tc, S // ts

    def body(gp_ref, g_ref, gn_ref, up_ref, u_ref, un_ref, d_ref, dn_ref, wg_ref, wu_ref, bg_ref, bu_ref,
             o_ref, dwg_ref, dwu_ref, dbg_ref, dbu_ref):
        s = pl.program_id(2)
        first, last = s == 0, s == ns - 1

        @pl.when(jnp.logical_and(pl.program_id(1) == 0, s == 0))
        def _():
            for r in (dwg_ref, dwu_ref, dbg_ref, dbu_ref):
                r[...] = jnp.zeros_like(r)

        wg, wu = wg_ref[...], wu_ref[...]
        ge = _ext(gp_ref, g_ref, gn_ref, first, last)
        ue = _ext(up_ref, u_ref, un_ref, first, last)
        de = _ext(None, d_ref, dn_ref, first, last)
        ug = _cconv(ge, wg) + bg_ref[...]
        uu = _cconv(ue, wu) + bu_ref[...]
        sig = _sigmoid(ug)
        dug = de * uu * (sig * (1.0 + ug * (1.0 - sig)))
        duu = de * (ug * sig)
        o_ref[0] = _main(_cconv_t(dug, wg), ts).astype(BF16)
        o_ref[1] = _main(_cconv_t(duu, wu), ts).astype(BF16)
        _conv_wgrad(dwg_ref, dug, ge, ts)
        _conv_wgrad(dwu_ref, duu, ue, ts)
        dbg_ref[...] += jnp.sum(_main(dug, ts), axis=0, keepdims=True)
        dbu_ref[...] += jnp.sum(_main(duu, ts), axis=0, keepdims=True)

    w3 = lambda off: pl.BlockSpec((3, tc), lambda j, b, s: (0, off + j))
    w1 = lambda off: pl.BlockSpec((1, tc), lambda j, b, s: (0, off + j))
    in_specs = (_halo_specs(S, ts, tc, 0, True, True) + _halo_specs(S, ts, tc, nj, True, True)
                + _halo_specs(S, ts, tc, 0, False, True) + [w3(0), w3(nj), w1(0), w1(nj)])
    outs = pl.pallas_call(
        body, name=name, grid=(nj, B, ns), in_specs=in_specs,
        out_specs=[pl.BlockSpec((2, None, ts, tc), lambda j, b, s: (0, b, s, j)), w3(0), w3(0), w1(0), w1(0)],
        out_shape=[jax.ShapeDtypeStruct((2, B, S, F), BF16), jax.ShapeDtypeStruct((3, F), F32),
                   jax.ShapeDtypeStruct((3, F), F32), jax.ShapeDtypeStruct((1, F), F32),
                   jax.ShapeDtypeStruct((1, F), F32)],
        compiler_params=pltpu.CompilerParams(
            dimension_semantics=("parallel", "arbitrary", "arbitrary"),
            vmem_limit_bytes=_vmem(30 * _nbytes((ts + 2 * HALO, tc), F32))),
    )(u0, u0, u0, u0, u0, u0, dact, dact, conv_w, conv_w, conv_b, conv_b)
    du0, dwg, dwu, dbg, dbu = outs
    return du0, jnp.concatenate([dwg, dwu], axis=1), jnp.concatenate([dbg, dbu], axis=1)


def _t5_bucket(dist):
    max_exact = REL_BUCKETS // 2
    n = jnp.maximum(dist, 0)
    nf = jnp.maximum(n, max_exact).astype(F32)
    large = max_exact + (jnp.log(nf / max_exact) / math.log(REL_MAX_DISTANCE / max_exact)
                         * (REL_BUCKETS - max_exact)).astype(jnp.int32)
    large = jnp.minimum(large, REL_BUCKETS - 1)
    return jnp.where(n < max_exact, n, large)


def _band_tables(window, dilation):
    P = ATT_BLOCK
    qi = jnp.arange(P, dtype=jnp.int32)[:, None]
    kc = jnp.arange(2 * P, dtype=jnp.int32)[None, :]
    delta = qi + P - kc
    band = (delta >= 0) & (delta <= window // dilation)
    bucket = _t5_bucket(delta * dilation).reshape(-1)
    onehot = (bucket[None, :] == jnp.arange(REL_BUCKETS, dtype=jnp.int32)[:, None]).astype(F32)
    return onehot, band


def _bias_lookup(rel_bias_t, onehots, name):
    nb, _, Q = onehots.shape
    H = rel_bias_t.shape[0]

    def body(r_ref, oh_ref, o_ref):
        o_ref[...] = lax.dot_general(r_ref[...], oh_ref[...], (((1,), (0,)), ((), ())),
                                     precision=lax.Precision.HIGHEST, preferred_element_type=F32)

    return pl.pallas_call(
        body, name=name, grid=(nb,),
        in_specs=[pl.BlockSpec((H, REL_BUCKETS), lambda i: (0, 0)),
                  pl.BlockSpec((None, REL_BUCKETS, Q), lambda i: (i, 0, 0))],
        out_specs=pl.BlockSpec((None, H, Q), lambda i: (i, 0, 0)),
        out_shape=jax.ShapeDtypeStruct((nb, H, Q), F32),
        compiler_params=pltpu.CompilerParams(dimension_semantics=("parallel",),
                                             vmem_limit_bytes=_vmem(_nbytes((REL_BUCKETS + H, Q), F32))),
    )(rel_bias_t, onehots)


def _bias_grad(dtabs, onehots, name):
    nb, H, Q = dtabs.shape

    def body(d_ref, oh_ref, o_ref):
        @pl.when(pl.program_id(0) == 0)
        def _():
            o_ref[...] = jnp.zeros_like(o_ref)

        o_ref[...] += lax.dot_general(d_ref[...], oh_ref[...], (((1,), (1,)), ((), ())),
                                      precision=lax.Precision.HIGHEST, preferred_element_type=F32)

    return pl.pallas_call(
        body, name=name, grid=(nb,),
        in_specs=[pl.BlockSpec((None, H, Q), lambda i: (i, 0, 0)),
                  pl.BlockSpec((None, REL_BUCKETS, Q), lambda i: (i, 0, 0))],
        out_specs=pl.BlockSpec((H, REL_BUCKETS), lambda i: (0, 0)),
        out_shape=jax.ShapeDtypeStruct((H, REL_BUCKETS), F32),
        compiler_params=pltpu.CompilerParams(dimension_semantics=("arbitrary",),
                                             vmem_limit_bytes=_vmem(_nbytes((REL_BUCKETS + H, Q), F32))),
    )(dtabs, onehots)


def _lane_masks():
    lane = lax.broadcasted_iota(jnp.int32, (1, LANE), 1)
    lo = lane < LANE // 2
    return lo, jnp.logical_not(lo)


def _dot_nt(a, b):
    return lax.dot_general(a, b, (((1,), (1,)), ((), ())), preferred_element_type=F32)


def _dot_nn(a, b):
    return lax.dot_general(a, b, (((1,), (0,)), ((), ())), preferred_element_type=F32)


def _dot_tn(a, b):
    return lax.dot_general(a, b, (((0,), (0,)), ((), ())), preferred_element_type=F32)


def _block_rows(n, dilation, S):
    P = ATT_BLOCK
    nb = S // (dilation * P)
    r, i = n // nb, n % nb
    cur = pl.ds(i * (P * dilation) + r, P, stride=dilation)
    prv = pl.ds(jnp.maximum(i - 1, 0) * (P * dilation) + r, P, stride=dilation)
    return cur, prv, jnp.minimum(i, 1)


def _attn_fwd(q, kv, tabs, name):
    B, S, D = q.shape
    P, H = ATT_BLOCK, N_HEADS
    scale = (D // H) ** -0.5
    half = LANE // 2
    nsteps = S // P // ATT_UNROLL

    def body(q_ref, k_ref, v_ref, tab_ref, o_ref, lse_ref, acc_ref, m_ref, l_ref):
        lo, hi = _lane_masks()
        for bi, (_, d) in enumerate(DILATED_BRANCHES):
            def block(n, carry, bi=bi, d=d):
                rows = [_block_rows(n + j * nsteps, d, S) for j in range(ATT_UNROLL)]
                loaded = []
                for cur, prv, variant in rows:
                    qb = q_ref[cur, :].astype(BF16)
                    kw = jnp.concatenate([k_ref[prv, :], k_ref[cur, :]], axis=0).astype(BF16)
                    vw = jnp.concatenate([v_ref[prv, :], v_ref[cur, :]], axis=0).astype(BF16)
                    loaded.append((qb, kw, vw))
                results = []
                for (cur, prv, variant), (qb, kw, vw) in zip(rows, loaded):
                    m_pair = l_pair = a_pair = None
                    for e, msk in enumerate((lo, hi)):
                        s = _dot_nt(jnp.where(msk, qb, 0), kw) * scale + tab_ref[bi, variant, e]
                        m_e = jnp.max(s, axis=-1, keepdims=True)
                        pe = jnp.exp(s - m_e)
                        l_e = jnp.sum(pe, axis=-1, keepdims=True)
                        a_e = _dot_nn(pe.astype(BF16), jnp.where(msk, vw, 0))
                        m_pair = m_e if e == 0 else jnp.where(lo, m_pair, m_e)
                        l_pair = l_e if e == 0 else jnp.where(lo, l_pair, l_e)
                        a_pair = a_e if e == 0 else a_pair + a_e
                    results.append((m_pair, l_pair, a_pair))
                for (cur, prv, variant), (m_pair, l_pair, a_pair) in zip(rows, results):
                    m_ref[bi, cur, :] = m_pair
                    l_ref[bi, cur, :] = l_pair
                    acc_ref[bi, cur, :] = a_pair
                return carry

            lax.fori_loop(0, nsteps, block, 0)

        nbr = len(DILATED_BRANCHES)
        chunk = 512

        def merge(t, carry):
            rs = pl.ds(pl.multiple_of(t * chunk, chunk), chunk)
            ms = [m_ref[b, rs, :] for b in range(nbr)]
            m = functools.reduce(jnp.maximum, ms)
            ws = [jnp.exp(mb - m) for mb in ms]
            l = functools.reduce(jnp.add, [w * l_ref[b, rs, :] for b, w in enumerate(ws)])
            acc = functools.reduce(jnp.add, [w * acc_ref[b, rs, :] for b, w in enumerate(ws)])
            o_ref[rs, :] = (acc / l).astype(BF16)
            lse_ref[rs, :] = m + jnp.log(l)
            return carry

        lax.fori_loop(0, S // chunk, merge, 0)

    nl = D // LANE
    col = lambda off: pl.BlockSpec((None, S, LANE), lambda b, p: (b, 0, off + p))
    tab_spec = pl.BlockSpec((3, 2, 2, P, 2 * P), lambda b, p: (0, 0, p, 0, 0))
    return pl.pallas_call(
        body, name=name, grid=(B, H // 2), in_specs=[col(0), col(0), col(nl), tab_spec],
        out_specs=[col(0), col(0)],
        out_shape=[jax.ShapeDtypeStruct((B, S, D), BF16), jax.ShapeDtypeStruct((B, S, D), F32)],
        scratch_shapes=[pltpu.VMEM((len(DILATED_BRANCHES), S, LANE), F32)] * 3,
        compiler_params=pltpu.CompilerParams(
            dimension_semantics=("parallel", "parallel"), vmem_limit_bytes=VMEM_CAP),
    )(q, kv, kv, tabs)


def _attn_bwd(q, kv, do, o, lse, tabs, name):
    B, S, D = q.shape
    P, H = ATT_BLOCK, N_HEADS
    scale = (D // H) ** -0.5
    half = LANE // 2
    nsteps = S // P // ATT_UNROLL

    def body(q_ref, k_ref, v_ref, do_ref, o_ref, lse_ref, tab_ref, dq_ref, dkv_ref, dtab_ref, delta_ref):
        lo, hi = _lane_masks()

        @pl.when(pl.program_id(1) == 0)
        def _():
            dtab_ref[...] = jnp.zeros_like(dtab_ref)

        dkv_ref[...] = jnp.zeros_like(dkv_ref)
        prod = do_ref[...] * o_ref[...].astype(F32)
        delta_ref[...] = jnp.where(lo, jnp.sum(jnp.where(lo, prod, 0.0), axis=-1, keepdims=True),
                                   jnp.sum(jnp.where(hi, prod, 0.0), axis=-1, keepdims=True))

        for bi, (_, d) in enumerate(DILATED_BRANCHES):
            def block(n, carry, bi=bi, d=d):
                rows = [_block_rows(n + j * nsteps, d, S) for j in range(ATT_UNROLL)]
                loaded = []
                for cur, prv, variant in rows:
                    qb = q_ref[cur, :].astype(BF16)
                    kw = jnp.concatenate([k_ref[prv, :], k_ref[cur, :]], axis=0).astype(BF16)
                    vw = jnp.concatenate([v_ref[prv, :], v_ref[cur, :]], axis=0).astype(BF16)
                    dob = do_ref[cur, :].astype(BF16)
                    dq_old = dq_ref[cur, :] if bi > 0 else None
                    loaded.append((qb, kw, vw, dob, lse_ref[cur, :], delta_ref[cur, :], dq_old))
                results = []
                for j, ((cur, prv, variant), (qb, kw, vw, dob, lse_b, dl_b, dq_old)) in enumerate(zip(rows, loaded)):
                    dq_pair = dk_pair = dv_pair = None
                    for e, msk in enumerate((lo, hi)):
                        qm = jnp.where(msk, qb, 0)
                        dom = jnp.where(msk, dob, 0)
                        s = _dot_nt(qm, kw) * scale + tab_ref[bi, variant, e]
                        pe = jnp.exp(s - lse_b[:, e * half:e * half + 1])
                        ds = pe * (_dot_nt(dom, vw) - dl_b[:, e * half:e * half + 1])
                        dtab_ref[bi, e] += ds
                        dsb = ds.astype(BF16)
                        dq_e = _dot_nn(dsb, jnp.where(msk, kw, 0))
                        dk_e = _dot_tn(dsb, qm)
                        dv_e = _dot_tn(pe.astype(BF16), dom)
                        dq_pair = dq_e if e == 0 else dq_pair + dq_e
                        dk_pair = dk_e if e == 0 else dk_pair + dk_e
                        dv_pair = dv_e if e == 0 else dv_pair + dv_e
                    dq_pair = dq_pair * scale
                    if bi > 0:
                        dq_pair = dq_pair + dq_old
                    results.append((dq_pair, dk_pair * scale, dv_pair))
                for (cur, prv, variant), (dq_pair, dk_pair, dv_pair) in zip(rows, results):
                    dq_ref[cur, :] = dq_pair
                    dkv_ref[0, cur, :] += dk_pair[P:, :]
                    dkv_ref[1, cur, :] += dv_pair[P:, :]
                    dkv_ref[0, prv, :] += dk_pair[:P, :]
                    dkv_ref[1, prv, :] += dv_pair[:P, :]
                return carry

            lax.fori_loop(0, nsteps, block, 0)

    nl = D // LANE
    col = lambda off: pl.BlockSpec((None, S, LANE), lambda p, b: (b, 0, off + p))
    tab_spec = pl.BlockSpec((3, 2, 2, P, 2 * P), lambda p, b: (0, 0, p, 0, 0))
    return pl.pallas_call(
        body, name=name, grid=(H // 2, B),
        in_specs=[col(0), col(0), col(nl), col(0), col(0), col(0), tab_spec],
        out_specs=[col(0), pl.BlockSpec((2, None, S, LANE), lambda p, b: (0, b, 0, p)),
                   pl.BlockSpec((None, 3, 2, P, 2 * P), lambda p, b: (p, 0, 0, 0, 0))],
        out_shape=[jax.ShapeDtypeStruct((B, S, D), F32), jax.ShapeDtypeStruct((2, B, S, D), F32),
                   jax.ShapeDtypeStruct((H // 2, 3, 2, P, 2 * P), F32)],
        scratch_shapes=[pltpu.VMEM((S, LANE), F32)],
        compiler_params=pltpu.CompilerParams(
            dimension_semantics=("parallel", "arbitrary"),
            vmem_limit_bytes=VMEM_CAP),
    )(q, kv, kv, do, o, lse, tabs)


def _attn_fwd_branch(q, kv, tab, dilation, name):
    B, S, D = q.shape
    P, d = ATT_BLOCK, dilation
    L = S // d
    nb = L // P
    H = N_HEADS
    scale = (D // H) ** -0.5
    qv = q.reshape(B, L, d * D)
    kvv = kv.reshape(B, L, d * 2 * D)

    def body(q_ref, kp_ref, kc_ref, vp_ref, vc_ref, tab_ref, o_ref, lse_ref):
        lo, hi = _lane_masks()
        for p in range(H // 2):
            sl = slice(p * LANE, (p + 1) * LANE)
            qp = q_ref[:, sl]
            kp = jnp.concatenate([kp_ref[:, sl], kc_ref[:, sl]], axis=0)
            vp = jnp.concatenate([vp_ref[:, sl], vc_ref[:, sl]], axis=0)
            o_pair = None
            lse_pair = None
            for e, msk in enumerate((lo, hi)):
                s = _dot_nt(jnp.where(msk, qp, 0), kp) * scale + tab_ref[2 * p + e]
                m = jnp.max(s, axis=-1, keepdims=True)
                pe = jnp.exp(s - m)
                den = jnp.sum(pe, axis=-1, keepdims=True)
                o_e = _dot_nn(pe.astype(BF16), jnp.where(msk, vp, 0)) / den
                lse_e = m + jnp.log(den)
                o_pair = o_e if e == 0 else o_pair + o_e
                lse_pair = lse_e if e == 0 else jnp.where(lo, lse_pair, lse_e)
            o_ref[:, sl] = o_pair
            lse_ref[:, sl] = lse_pair

    blk = lambda f: pl.BlockSpec((None, P, D), f)
    prev = lambda i: jnp.maximum(i - 1, 0)
    in_specs = [blk(lambda b, r, i: (b, i, r)),
                blk(lambda b, r, i: (b, prev(i), 2 * r)), blk(lambda b, r, i: (b, i, 2 * r)),
                blk(lambda b, r, i: (b, prev(i), 2 * r + 1)), blk(lambda b, r, i: (b, i, 2 * r + 1)),
                pl.BlockSpec((None, H, P, 2 * P), lambda b, r, i: (jnp.minimum(i, 1), 0, 0, 0))]
    o, lse = pl.pallas_call(
        body, name=name, grid=(B, d, nb), in_specs=in_specs,
        out_specs=[blk(lambda b, r, i: (b, i, r))] * 2,
        out_shape=[jax.ShapeDtypeStruct((B, L, d * D), F32)] * 2,
        compiler_params=pltpu.CompilerParams(
            dimension_semantics=("parallel", "parallel", "parallel"),
            vmem_limit_bytes=_vmem(_nbytes((H, P, 2 * P), F32) + 9 * _nbytes((P, D), F32))),
    )(qv, kvv, kvv, kvv, kvv, tab)
    return o.reshape(B * S, D), lse.reshape(B * S, D)


def _attn_merge(os_, lses, name, tm=512):
    T, D = os_[0].shape
    n = len(os_)
    tm = _tile(T, tm)

    def body(*refs):
        o_refs, l_refs = refs[:n], refs[n:2 * n]
        out_ref, lse_ref = refs[2 * n:]
        ls = [r[...] for r in l_refs]
        m = functools.reduce(jnp.maximum, ls)
        ws = [jnp.exp(l - m) for l in ls]
        tot = functools.reduce(jnp.add, ws)
        acc = functools.reduce(jnp.add, [w * r[...] for w, r in zip(ws, o_refs)])
        out_ref[...] = (acc / tot).astype(BF16)
        lse_ref[...] = m + jnp.log(tot)

    row = pl.BlockSpec((tm, D), lambda i: (i, 0))
    return pl.pallas_call(
        body, name=name, grid=(T // tm,), in_specs=[row] * (2 * n), out_specs=[row, row],
        out_shape=[jax.ShapeDtypeStruct((T, D), BF16), jax.ShapeDtypeStruct((T, D), F32)],
        compiler_params=pltpu.CompilerParams(
            dimension_semantics=("parallel",), vmem_limit_bytes=_vmem((2 * n + 4) * _nbytes((tm, D), F32))),
    )(*os_, *lses)


def _attn_bwd_q(q, kv, do, o, lse, tab, dq_acc, dilation, name):
    B, S, D = q.shape
    P, d = ATT_BLOCK, dilation
    L = S // d
    nb = L // P
    H = N_HEADS
    scale = (D // H) ** -0.5
    view = lambda t, c: t.reshape(B, L, d * c)
    has_acc = dq_acc is not None

    def body(q_ref, kp_ref, kc_ref, vp_ref, vc_ref, do_ref, o_ref, lse_ref, tab_ref, *rest):
        if has_acc:
            acc_ref, dq_ref, dtab_ref = rest
        else:
            dq_ref, dtab_ref = rest

        @pl.when((pl.program_id(0) == 0) & (pl.program_id(1) == 0) & (pl.program_id(2) == 0))
        def _():
            dtab_ref[...] = jnp.zeros_like(dtab_ref)

        lo, hi = _lane_masks()
        for p in range(H // 2):
            sl = slice(p * LANE, (p + 1) * LANE)
            qp = q_ref[:, sl]
            kp = jnp.concatenate([kp_ref[:, sl], kc_ref[:, sl]], axis=0)
            vp = jnp.concatenate([vp_ref[:, sl], vc_ref[:, sl]], axis=0)
            dop = do_ref[:, sl]
            prod = dop.astype(F32) * o_ref[:, sl].astype(F32)
            lsep = lse_ref[:, sl]
            dq_pair = None
            for e, msk in enumerate((lo, hi)):
                s = _dot_nt(jnp.where(msk, qp, 0), kp) * scale + tab_ref[2 * p + e]
                pe = jnp.exp(s - lsep[:, e * (LANE // 2):e * (LANE // 2) + 1])
                dp = _dot_nt(jnp.where(msk, dop, 0).astype(BF16), vp)
                delta = jnp.sum(jnp.where(msk, prod, 0.0), axis=-1, keepdims=True)
                ds = pe * (dp - delta)
                dtab_ref[2 * p + e] += ds
                dq_e = _dot_nn(ds.astype(BF16), jnp.where(msk, kp, 0))
                dq_pair = dq_e if e == 0 else dq_pair + dq_e
            dq_pair = dq_pair * scale
            if has_acc:
                dq_pair = dq_pair + acc_ref[:, sl]
            dq_ref[:, sl] = dq_pair

    blk = lambda f: pl.BlockSpec((None, P, D), f)
    prev = lambda i: jnp.maximum(i - 1, 0)
    own = blk(lambda b, r, i: (b, i, r))
    tab_spec = pl.BlockSpec((None, H, P, 2 * P), lambda b, r, i: (jnp.minimum(i, 1), 0, 0, 0))
    in_specs = [own,
                blk(lambda b, r, i: (b, prev(i), 2 * r)), blk(lambda b, r, i: (b, i, 2 * r)),
                blk(lambda b, r, i: (b, prev(i), 2 * r + 1)), blk(lambda b, r, i: (b, i, 2 * r + 1)),
                own, own, own, tab_spec]
    kvv = view(kv, 2 * D)
    args = [view(q, D), kvv, kvv, kvv, kvv, view(do, D), view(o, D), view(lse, D), tab]
    if has_acc:
        in_specs.append(own)
        args.append(view(dq_acc, D))
    dq, dtab = pl.pallas_call(
        body, name=name, grid=(B, d, nb), in_specs=in_specs,
        out_specs=[own, pl.BlockSpec((H, P, 2 * P), lambda b, r, i: (0, 0, 0))],
        out_shape=[jax.ShapeDtypeStruct((B, L, d * D), F32), jax.ShapeDtypeStruct((H, P, 2 * P), F32)],
        compiler_params=pltpu.CompilerParams(
            dimension_semantics=("arbitrary", "arbitrary", "arbitrary"),
            vmem_limit_bytes=_vmem(2 * _nbytes((H, P, 2 * P), F32) + 12 * _nbytes((P, D), F32))),
    )(*args)
    return dq.reshape(B, S, D), dtab


def _attn_bwd_kv(q, kv, do, o, lse, tabk, dkv_acc, dilation, name):
    B, S, D = q.shape
    P, d = ATT_BLOCK, dilation
    L = S // d
    nb = L // P
    H = N_HEADS
    scale = (D // H) ** -0.5
    view = lambda t, c: t.reshape(B, L, d * c)
    has_acc = dkv_acc is not None

    def body(k_ref, v_ref, qa_ref, qb_ref, doa_ref, dob_ref, oa_ref, ob_ref, la_ref, lb_ref, tab_ref, *rest):
        if has_acc:
            acc_ref, dkv_ref = rest
        else:
            (dkv_ref,) = rest
        lo, hi = _lane_masks()
        for p in range(H // 2):
            sl = slice(p * LANE, (p + 1) * LANE)
            kp, vp = k_ref[:, sl], v_ref[:, sl]
            q2 = jnp.concatenate([qa_ref[:, sl], qb_ref[:, sl]], axis=0)
            do2 = jnp.concatenate([doa_ref[:, sl], dob_ref[:, sl]], axis=0)
            o2 = jnp.concatenate([oa_ref[:, sl], ob_ref[:, sl]], axis=0)
            lse2 = jnp.concatenate([la_ref[:, sl], lb_ref[:, sl]], axis=0)
            prod = do2.astype(F32) * o2.astype(F32)
            dk_pair = None
            dv_pair = None
            for e, msk in enumerate((lo, hi)):
                qm = jnp.where(msk, q2, 0)
                dom = jnp.where(msk, do2, 0).astype(BF16)
                s = _dot_nt(qm, kp) * scale + tab_ref[2 * p + e]
                pe = jnp.exp(s - lse2[:, e * (LANE // 2):e * (LANE // 2) + 1])
                dp = _dot_nt(dom, vp)
                delta = jnp.sum(jnp.where(msk, prod, 0.0), axis=-1, keepdims=True)
                ds = pe * (dp - delta)
                dv_e = _dot_tn(pe.astype(BF16), dom)
                dk_e = _dot_tn(ds.astype(BF16), qm)
                dk_pair = dk_e if e == 0 else dk_pair + dk_e
                dv_pair = dv_e if e == 0 else dv_pair + dv_e
            dk_pair = dk_pair * scale
            if has_acc:
                dk_pair = dk_pair + acc_ref[0, :, sl]
                dv_pair = dv_pair + acc_ref[1, :, sl]
            dkv_ref[0, :, sl] = dk_pair
            dkv_ref[1, :, sl] = dv_pair

    blk = lambda f: pl.BlockSpec((None, P, D), f)
    nxt = lambda i: jnp.minimum(i + 1, nb - 1)
    own = blk(lambda b, r, i: (b, i, r))
    nx = blk(lambda b, r, i: (b, nxt(i), r))
    tab_spec = pl.BlockSpec((None, H, 2 * P, P), lambda b, r, i: (jnp.where(i == nb - 1, 1, 0), 0, 0, 0))
    pair = pl.BlockSpec((2, None, P, D), lambda b, r, i: (0, b, i, r))
    in_specs = [blk(lambda b, r, i: (b, i, 2 * r)), blk(lambda b, r, i: (b, i, 2 * r + 1)),
                own, nx, own, nx, own, nx, own, nx, tab_spec]
    kvv, qv, dov, ov, lv = view(kv, 2 * D), view(q, D), view(do, D), view(o, D), view(lse, D)
    args = [kvv, kvv, qv, qv, dov, dov, ov, ov, lv, lv, tabk]
    if has_acc:
        in_specs.append(pair)
        args.append(dkv_acc.reshape(2, B, L, d * D))
    dkv = pl.pallas_call(
        body, name=name, grid=(B, d, nb), in_specs=in_specs, out_specs=pair,
        out_shape=jax.ShapeDtypeStruct((2, B, L, d * D), F32),
        compiler_params=pltpu.CompilerParams(
            dimension_semantics=("parallel", "parallel", "parallel"),
            vmem_limit_bytes=_vmem(_nbytes((H, P, 2 * P), F32) + 16 * _nbytes((P, D), F32))),
    )(*args)
    return dkv.reshape(2, B, S, D)


def _mesh_place():
    x, y, c = lax.axis_index("x"), lax.axis_index("y"), lax.axis_index("c")
    return x, y, c, (x, y, 1 - c), [(1 - x, y), (x, 1 - y), (1 - x, 1 - y)]


def _rdma(src, dst, send_sems, recv_sems, idx, to):
    return pltpu.make_async_remote_copy(
        src_ref=src, dst_ref=dst, send_sem=send_sems.at[idx], recv_sem=recv_sems.at[idx],
        device_id=to, device_id_type=pl.DeviceIdType.MESH)


def _comm_call(body, name, args, out_shape, n_sems, n_local):
    any_spec = pl.BlockSpec(memory_space=pl.ANY)
    return pl.pallas_call(
        body, name=name, in_specs=[any_spec] * len(args), out_specs=[any_spec] * len(out_shape),
        out_shape=out_shape,
        scratch_shapes=[pltpu.SemaphoreType.DMA(n_sems), pltpu.SemaphoreType.DMA(n_sems),
                        pltpu.SemaphoreType.DMA((n_local,))],
        compiler_params=pltpu.CompilerParams(has_side_effects=True),
    )(*args)


def _all_gather(arrays, name):
    n = len(arrays)

    def body(*refs):
        ins, outs = refs[:n], refs[n:2 * n]
        send_sems, recv_sems, loc_sems = refs[2 * n:]
        x, y, c, sib, chips = _mesh_place()
        me = (x, y, c)

        def rows(a, dev):
            return outs[a].at[4 * dev[0] + 2 * dev[1] + dev[2]]

        def copy(a, k, block, to, src=None):
            return _rdma(rows(a, block) if src is None else src, rows(a, block), send_sems, recv_sems, (a, k), to)

        local, first, passed = [], [], []
        for a in range(n):
            cp = pltpu.make_async_copy(ins[a], rows(a, me), loc_sems.at[a])
            cp.start()
            local.append(cp)
            first.append(copy(a, 0, me, sib, src=ins[a]))
            first += [copy(a, 1 + j, me, (*chip, c), src=ins[a]) for j, chip in enumerate(chips)]
        for cp in first:
            cp.start()
        for j, chip in enumerate(chips):
            for a in range(n):
                copy(a, 1 + j, (*chip, c), me).wait_recv()
                cp = copy(a, 4 + j, (*chip, c), sib)
                cp.start()
                passed.append(cp)
        for a in range(n):
            copy(a, 0, sib, me).wait_recv()
            for j, chip in enumerate(chips):
                copy(a, 4 + j, (*chip, 1 - c), me).wait_recv()
        for cp in first + passed:
            cp.wait_send()
        for cp in local:
            cp.wait()

    out_shape = [jax.ShapeDtypeStruct((N_DEV,) + g.shape, g.dtype) for g in arrays]
    return list(_comm_call(body, name, arrays, out_shape, (n, N_DEV - 1), n))


def _core_pair_exchange(halves, gathered, name):
    nh, ng = len(halves), len(gathered)

    def body(*refs):
        h_in, g_in = refs[:nh], refs[nh:nh + ng]
        outs = refs[nh + ng:nh + ng + nh + ng]
        theirs, g_out = outs[:nh], outs[nh:]
        send_sems, recv_sems, loc_sems = refs[nh + ng + nh + ng:]
        x, y, c, sib, chips = _mesh_place()
        me = 4 * x + 2 * y + c
        peers = [sib] + [(*chip, pc) for chip in chips for pc in (c, 1 - c)]
        nchip = N_DEV // 2
        gbase = nh * nchip
        local, sends = [], []
        for a in range(nh):
            for ch in range(nchip):
                cp = _rdma(h_in[a].at[ch, 1 - c], theirs[a].at[ch], send_sems, recv_sems, a * nchip + ch, sib)
                cp.start()
                sends.append(cp)
        for g in range(ng):
            cp = pltpu.make_async_copy(g_in[g], g_out[g].at[me], loc_sems.at[g])
            cp.start()
            local.append(cp)
            for k, peer in enumerate(peers):
                cp = _rdma(g_in[g], g_out[g].at[me], send_sems, recv_sems, gbase + g * (N_DEV - 1) + k, peer)
                cp.start()
                sends.append(cp)
        for cp in sends:
            cp.wait_send()
        for a in range(nh):
            for ch in range(nchip):
                _rdma(h_in[a].at[ch, 1 - c], theirs[a].at[ch], send_sems, recv_sems, a * nchip + ch, sib).wait_recv()
        for g in range(ng):
            for k, peer in enumerate(peers):
                pid = 4 * peer[0] + 2 * peer[1] + peer[2]
                _rdma(g_in[g], g_out[g].at[pid], send_sems, recv_sems, gbase + g * (N_DEV - 1) + k, peer).wait_recv()
        for cp in local:
            cp.wait()

    half = [jax.ShapeDtypeStruct((h.shape[0],) + h.shape[2:], h.dtype) for h in halves]
    out_shape = half + [jax.ShapeDtypeStruct((N_DEV,) + g.shape, g.dtype) for g in gathered]
    outs = _comm_call(body, name, list(halves) + list(gathered), out_shape,
                      (nh * (N_DEV // 2) + ng * (N_DEV - 1),), max(ng, 1))
    return list(outs[:nh]), list(outs[nh:])


def _chip_exchange(arrays, name):
    n = len(arrays)

    def body(*refs):
        ins, outs = refs[:n], refs[n:2 * n]
        send_sems, recv_sems, _ = refs[2 * n:]
        x, y, c, sib, chips = _mesh_place()
        sends = []
        for a in range(n):
            for j, chip in enumerate(chips):
                cp = _rdma(ins[a].at[2 * chip[0] + chip[1]], outs[a].at[j], send_sems, recv_sems, (a, j), (*chip, c))
                cp.start()
                sends.append(cp)
        for cp in sends:
            cp.wait_send()
        for a in range(n):
            for j, chip in enumerate(chips):
                _rdma(ins[a].at[j], outs[a].at[j], send_sems, recv_sems, (a, j), (*chip, c)).wait_recv()

    out_shape = [jax.ShapeDtypeStruct((3,) + g.shape[1:], g.dtype) for g in arrays]
    return list(_comm_call(body, name, arrays, out_shape, (n, 3), 1))


def _split_start(copies_fn, srcs, land_shapes, n_sems, name):
    ns, nl = len(srcs), len(land_shapes)
    nc = n_sems
    hbm = pl.BlockSpec(memory_space=pltpu.HBM)
    sem = pl.BlockSpec(memory_space=pltpu.SEMAPHORE)

    def body(*refs):
        src_refs, land_refs = refs[:ns], refs[ns:ns + nl]
        send_sems, recv_sems = refs[ns + nl:ns + nl + nc], refs[ns + nl + nc:ns + nl + 2 * nc]
        token = refs[-1]
        for cp in copies_fn(src_refs, land_refs, send_sems, recv_sems):
            cp.start()
        token[...] = jnp.zeros_like(token)

    lands = [lax.empty(s.shape, s.dtype) for s in land_shapes]
    thru = [pltpu.HBM(t.shape, t.dtype) for t in list(srcs) + lands]
    outs = pl.pallas_call(
        body, name=name,
        out_shape=(*([pltpu.SemaphoreType.DMA(())] * (2 * nc)), *thru, jax.ShapeDtypeStruct((8, LANE), F32)),
        in_specs=[hbm] * (ns + nl),
        out_specs=(*([sem] * (2 * nc)), *([hbm] * (ns + nl)), pl.BlockSpec(memory_space=pltpu.VMEM)),
        input_output_aliases={i: 2 * nc + i for i in range(ns + nl)},
        compiler_params=pltpu.CompilerParams(has_side_effects=pltpu.SideEffectType.DATAFLOW_SIDE_EFFECTING),
    )(*[pltpu.with_memory_space_constraint(t, pltpu.HBM) for t in list(srcs) + lands])
    thru_out = outs[2 * nc:2 * nc + ns + nl]
    return list(outs[:nc]), list(outs[nc:2 * nc]), list(thru_out[:ns]), list(thru_out[ns:]), outs[-1]


def _split_wait(copies_fn, send_sems, recv_sems, srcs, lands, after, name):
    ns, nl, nc = len(srcs), len(lands), len(send_sems)
    hbm = pl.BlockSpec(memory_space=pltpu.HBM)
    sem = pl.BlockSpec(memory_space=pltpu.SEMAPHORE)

    def body(*refs):
        src_refs, land_refs = refs[:ns], refs[ns:ns + nl]
        send_refs, recv_refs = refs[ns + nl:ns + nl + nc], refs[ns + nl + nc:ns + nl + 2 * nc]
        for cp in copies_fn(src_refs, land_refs, send_refs, recv_refs):
            cp.wait_send()
            cp.wait_recv()

    outs = pl.pallas_call(
        body, name=name,
        out_shape=[pltpu.HBM(t.shape, t.dtype) for t in list(srcs) + list(lands)],
        in_specs=[hbm] * (ns + nl) + [sem] * (2 * nc) + [pl.BlockSpec(memory_space=pl.ANY)],
        out_specs=[hbm] * (ns + nl),
        input_output_aliases={i: i for i in range(ns + nl)},
        compiler_params=pltpu.CompilerParams(has_side_effects=pltpu.SideEffectType.DATAFLOW_SIDE_EFFECTING),
    )(*srcs, *lands, *send_sems, *recv_sems, after)
    return list(outs[:ns]), list(outs[ns:])


def _chip_copies(src_refs, land_refs, send_sems, recv_sems):
    x, y, c, sib, chips = _mesh_place()
    return [pltpu.make_async_remote_copy(
                src_ref=src.at[2 * chip[0] + chip[1]], dst_ref=land.at[j], send_sem=send_sems[3 * a + j],
                recv_sem=recv_sems[3 * a + j], device_id=(*chip, c), device_id_type=pl.DeviceIdType.MESH)
            for a, (src, land) in enumerate(zip(src_refs, land_refs)) for j, chip in enumerate(chips)]


def _pair_sum(halves, theirs, core, name, tr=256):
    nchip, _, R, C = halves.shape
    tr = _row_tile(R, tr)

    def body(core_ref, h_ref, t_ref, o_ref):
        o_ref[...] = (h_ref[...] + t_ref[...]).astype(BF16)

    grid_spec = pltpu.PrefetchScalarGridSpec(
        num_scalar_prefetch=1, grid=(nchip, R // tr),
        in_specs=[pl.BlockSpec((None, None, tr, C), lambda ch, i, core_ref: (ch, core_ref[0], i, 0)),
                  pl.BlockSpec((None, tr, C), lambda ch, i, core_ref: (ch, i, 0))],
        out_specs=pl.BlockSpec((None, tr, C), lambda ch, i, core_ref: (ch, i, 0)))
    return pl.pallas_call(
        body, name=name, grid_spec=grid_spec, out_shape=jax.ShapeDtypeStruct((nchip, R, C), BF16),
        compiler_params=pltpu.CompilerParams(
            dimension_semantics=("parallel", "parallel"), vmem_limit_bytes=_vmem(3 * _nbytes((tr, C), F32))),
    )(core, halves, theirs)


def _chip_sum(own, recv, chip, name, tr=256):
    _, R, C = own.shape
    tr = _row_tile(R, tr)

    def body(chip_ref, o_ref, r_ref, out_ref):
        acc = o_ref[...].astype(F32)
        for j in range(3):
            acc = acc + r_ref[j].astype(F32)
        out_ref[...] = acc

    grid_spec = pltpu.PrefetchScalarGridSpec(
        num_scalar_prefetch=1, grid=(R // tr,),
        in_specs=[pl.BlockSpec((None, tr, C), lambda i, chip_ref: (chip_ref[0], i, 0)),
                  pl.BlockSpec((3, tr, C), lambda i, chip_ref: (0, i, 0))],
        out_specs=pl.BlockSpec((tr, C), lambda i, chip_ref: (i, 0)))
    return pl.pallas_call(
        body, name=name, grid_spec=grid_spec, out_shape=jax.ShapeDtypeStruct((R, C), F32),
        compiler_params=pltpu.CompilerParams(
            dimension_semantics=("parallel",), vmem_limit_bytes=_vmem(4 * _nbytes((tr, C), F32))),
    )(chip, own, recv)


def _row_tile(rows, target):
    best = rows
    for t in range(8, min(rows, target) + 1, 8):
        if rows % t == 0:
            best = t
    return best


def _sum_slots(recv, name, tr=128):
    n, R, C = recv.shape
    tr = _row_tile(R, tr if recv.dtype == F32 else 2 * tr)

    def body(r_ref, o_ref):
        acc = r_ref[0].astype(F32)
        for k in range(1, n):
            acc = acc + r_ref[k].astype(F32)
        o_ref[...] = acc

    return pl.pallas_call(
        body, name=name, grid=(R // tr,),
        in_specs=[pl.BlockSpec((n, tr, C), lambda i: (0, i, 0))],
        out_specs=pl.BlockSpec((tr, C), lambda i: (i, 0)),
        out_shape=jax.ShapeDtypeStruct((R, C), F32),
        compiler_params=pltpu.CompilerParams(
            dimension_semantics=("parallel",), vmem_limit_bytes=_vmem(10 * _nbytes((tr, C), F32))),
    )(recv)


def _adamw(g, w, m, v, name, tr=256):
    R, C = g.shape
    tr = _row_tile(R, tr)

    def body(g_ref, w_ref, m_ref, v_ref, d_ref, nm_ref, nv_ref):
        gv = g_ref[...]
        nm = ADAM_B1 * m_ref[...] + (1.0 - ADAM_B1) * gv
        nv = ADAM_B2 * v_ref[...] + (1.0 - ADAM_B2) * (gv * gv)
        m_hat = nm / (1.0 - ADAM_B1 ** ADAM_STEP)
        v_hat = nv / (1.0 - ADAM_B2 ** ADAM_STEP)
        d_ref[...] = -ADAM_LR * (m_hat / (jnp.sqrt(v_hat) + ADAM_EPS) + ADAM_WD * w_ref[...])
        nm_ref[...] = nm
        nv_ref[...] = nv

    row = pl.BlockSpec((tr, C), lambda i: (i, 0))
    return pl.pallas_call(
        body, name=name, grid=(R // tr,), in_specs=[row] * 4, out_specs=[row] * 3,
        out_shape=[jax.ShapeDtypeStruct((R, C), F32)] * 3,
        compiler_params=pltpu.CompilerParams(
            dimension_semantics=("parallel",), vmem_limit_bytes=_vmem(8 * _nbytes((tr, C), F32))),
    )(g, w, m, v)


def _pack_rows(parts):
    flat = jnp.concatenate([p.reshape(-1).astype(F32) for p in parts])
    rows = -(-flat.shape[0] // (8 * LANE)) * 8
    return jnp.pad(flat, (0, rows * LANE - flat.shape[0])).reshape(rows, LANE)


def _unpack(flat, shapes):
    out, off = [], 0
    for shp in shapes:
        n = math.prod(shp)
        out.append(flat[off:off + n].reshape(shp))
        off += n
    return out


def kernel(x, a_norm, a_w_in, a_conv, a_w_out, kv_norm, w_kv, b_norm, b_w_q, b_w_o, rel_bias, ffn_norm, ffn_w_up, ffn_conv, ffn_conv_b, ffn_w_down, final_norm, loss_target, m_a_norm, m_a_w_in, m_a_conv, m_a_w_out, m_kv_norm, m_w_kv, m_b_norm, m_b_w_q, m_b_w_o, m_rel_bias, m_ffn_norm, m_ffn_w_up, m_ffn_conv, m_ffn_conv_b, m_ffn_w_down, m_final_norm, v_a_norm, v_a_w_in, v_a_conv, v_a_w_out, v_kv_norm, v_w_kv, v_b_norm, v_b_w_q, v_b_w_o, v_rel_bias, v_ffn_norm, v_ffn_w_up, v_ffn_conv, v_ffn_conv_b, v_ffn_w_down, v_final_norm):
    B, S, D = x.shape
    T = B * S
    F = ffn_w_down.shape[1] * N_DEV
    me = 4 * lax.axis_index("x") + 2 * lax.axis_index("y") + lax.axis_index("c")

    big_shards = [a_w_in[0].T, a_w_out[0], w_kv.T, b_w_q[0], b_w_o[0],
                  ffn_w_up[0].T, ffn_w_up[1].T, ffn_w_down[0], ffn_w_down[1]]
    small_shapes = [a_norm.shape, a_conv.shape, ffn_conv.shape]
    small_pack = _pack_rows([a_norm, a_conv, ffn_conv])
    gathered = _all_gather([s.astype(BF16) for s in big_shards] + [small_pack], "gather_weights")
    win_t, wout, wkv_t, wq, wo, wup0_t, wup1_t, wdn0, wdn1 = [
        g.reshape(-1, D) for g in gathered[:9]]
    wup_t, wdn = (wup0_t, wup1_t), (wdn0, wdn1)
    smalls = [_unpack(gathered[9][j].reshape(-1), small_shapes) for j in range(N_DEV)]
    a_norm_f = jnp.concatenate([s[0] for s in smalls], axis=-1)
    a_conv_f = jnp.concatenate([s[1] for s in smalls], axis=-1)[0]
    ffn_conv_f = jnp.concatenate([s[2] for s in smalls], axis=-1)

    x2 = x.reshape(T, D)
    (xn0,) = _rmsnorm_fwd(x2, a_norm_f, "a_norm_fwd")
    bch = _matmul(xn0, win_t, mode="nt", out_dtype=BF16, name="a_in_proj").reshape(B, S, 3 * D)
    gated = _shortconv_fwd(bch, a_conv_f, "a_gate_fwd").reshape(T, D)
    h1 = _matmul(gated, wout, mode="nn", out_dtype=F32, add=x2, name="a_out_proj")

    def ffn_fwd(h, l):
        (xn,) = _rmsnorm_fwd(h, ffn_norm[l:l + 1], f"ffn{l}_norm_fwd")
        u0 = _matmul(xn, wup_t[l], mode="nt", out_dtype=BF16, name=f"ffn{l}_up").reshape(B, S, 2 * F)
        act = _ffn_gate_fwd(u0, ffn_conv_f[l], ffn_conv_b[l:l + 1], f"ffn{l}_gate_fwd").reshape(T, F)
        out = _matmul(act, wdn[l], mode="nn", out_dtype=F32, add=h, name=f"ffn{l}_down", tk=2816)
        return out, (xn, u0, act)

    h2, ffn0_saved = ffn_fwd(h1, 0)
    kvn, xnb = _rmsnorm_fwd(h2, jnp.stack([kv_norm, b_norm[0]]), "kv_b_norm_fwd")
    kv = _matmul(kvn, wkv_t, mode="nt", out_dtype=F32, name="kv_proj").reshape(B, S, 2 * D)
    q = _matmul(xnb, wq, mode="nn", out_dtype=F32, name="q_proj").reshape(B, S, D)

    tables = [_band_tables(w, d) for (w, d) in DILATED_BRANCHES]
    onehots = jnp.stack([t[0] for t in tables])
    P = ATT_BLOCK
    bias_vals = _bias_lookup(rel_bias.T, onehots, "rel_bias_lookup").reshape(3, N_HEADS, P, 2 * P)
    in_cur = (jnp.arange(2 * P) >= P)[None, :]
    tabs = []
    for bi in range(3):
        band = tables[bi][1]
        gen = jnp.where(band[None], bias_vals[bi], NEG)
        fst = jnp.where((band & in_cur)[None], bias_vals[bi], NEG)
        tabs.append(jnp.stack([fst, gen]))
    tabs = jnp.stack(tabs)

    attn3, lse3 = _attn_fwd(q, kv, tabs, "attn_fwd")
    attn = attn3.reshape(T, D)
    h3 = _matmul(attn, wo, mode="nn", out_dtype=F32, add=h2, name="o_proj")
    h4, ffn1_saved = ffn_fwd(h3, 1)

    dh4, dh4b, d_final_norm, sq = _final_loss_bwd(h4, final_norm.reshape(1, D), loss_target.reshape(T, D), "loss_bwd")
    loss_part = 0.5 * jnp.sum(sq) / D

    core_idx = lax.axis_index("c").astype(jnp.int32).reshape(1)
    chip_idx = (2 * lax.axis_index("x") + lax.axis_index("y")).astype(jnp.int32).reshape(1)
    in_flight = []

    def send_group(tag, indexed_grads, small=()):
        halves = [g.reshape(N_DEV // 2, 2, g.shape[0] // N_DEV, D) for _, g in indexed_grads]
        theirs, gath = _core_pair_exchange(halves, list(small), f"exchange_core_{tag}")
        sums = [_pair_sum(h_, t_, core_idx, f"grad_core_sum_{tag}{k}")
                for k, (h_, t_) in enumerate(zip(halves, theirs))]
        lands = [jax.ShapeDtypeStruct((3,) + s.shape[1:], s.dtype) for s in sums]
        send, recv, srcs, lnd, token = _split_start(_chip_copies, sums, lands, 3 * len(sums),
                                                    f"exchange_chip_start_{tag}")
        in_flight.append((tag, [i for i, _ in indexed_grads], send, recv, srcs, lnd))
        return token, gath

    def ffn_bwd(dh_out, dh_out_b, h_in, saved, l, after):
        xn, u0, act = saved
        dact = _matmul(dh_out_b, wdn[l], mode="nt", out_dtype=BF16, after=after, name=f"ffn{l}_down_dx",
                       tn=1408).reshape(B, S, F)
        d_wdn = _matmul(act, dh_out_b, mode="tn", out_dtype=F32, name=f"ffn{l}_down_dw", tm=1408, tn=1024, tk=2048)
        du0, d_conv, d_conv_b = _ffn_gate_bwd(u0, dact, ffn_conv_f[l], ffn_conv_b[l:l + 1], f"ffn{l}_gate_bwd")
        du0 = du0.reshape(2, T, F)
        dxn = _matmul(du0, wup_t[l], mode="nn", out_dtype=F32, a_parts=2, name=f"ffn{l}_up_dx", tk=2816)
        d_wup_t = _matmul(du0, xn, mode="tn", out_dtype=F32, a_parts=2, name=f"ffn{l}_up_dw", tm=1408, tn=1024, tk=2048)
        dh_in, dh_in_b, d_norm = _rmsnorm_bwd(h_in, ffn_norm[l:l + 1], [dxn], dh_out, f"ffn{l}_norm_bwd")
        return (dh_in, dh_in_b), (d_wdn, d_wup_t, d_conv, d_conv_b, d_norm)

    (dh3, dh3b), ffn1_grads = ffn_bwd(dh4, dh4b, h3, ffn1_saved, 1, None)
    token, _ = send_group("ffn1", [(6, ffn1_grads[1]), (8, ffn1_grads[0])])

    dattn = _matmul(dh3b, wo, mode="nt", out_dtype=F32, after=token, name="o_proj_dx").reshape(B, S, D)
    d_wo = _matmul(attn, dh3b, mode="tn", out_dtype=F32, name="o_proj_dw", tm=1024, tn=1024, tk=2048)
    dq, dkv, dtab = _attn_bwd(q, kv, dattn, attn3, lse3, tabs, "attn_bwd")
    dtabs = dtab.transpose(1, 0, 2, 3, 4).reshape(3, N_HEADS, P * 2 * P)
    d_rel_bias = _bias_grad(dtabs, onehots, "rel_bias_grad").T
    dq2 = dq.reshape(T, D)
    dkv2 = dkv.reshape(2, T, D)
    dxnb = _matmul(dq2, wq, mode="nt", out_dtype=F32, name="q_proj_dx")
    d_wq = _matmul(xnb, dq2, mode="tn", out_dtype=F32, name="q_proj_dw", tm=1024, tn=1024, tk=2048)
    dkvn = _matmul(dkv2, wkv_t, mode="nn", out_dtype=F32, a_parts=2, name="kv_proj_dx")
    d_wkv_t = _matmul(dkv2, kvn, mode="tn", out_dtype=F32, a_parts=2, name="kv_proj_dw", tm=1024, tn=1024, tk=2048)
    dh2, dh2b, d_kvb_norm = _rmsnorm_bwd(h2, jnp.stack([kv_norm, b_norm[0]]), [dkvn, dxnb], dh3, "kv_b_norm_bwd")

    token, _ = send_group("attn", [(2, d_wkv_t), (3, d_wq), (4, d_wo)])

    (dh1, dh1b), ffn0_grads = ffn_bwd(dh2, dh2b, h1, ffn0_saved, 0, token)
    token, _ = send_group("ffn0", [(5, ffn0_grads[1]), (7, ffn0_grads[0])])

    dgated = _matmul(dh1b, wout, mode="nt", out_dtype=BF16, after=token, name="a_out_proj_dx").reshape(B, S, D)
    d_wout = _matmul(gated, dh1b, mode="tn", out_dtype=F32, name="a_out_proj_dw", tm=1024, tn=1024, tk=2048)
    dbch, d_a_conv = _shortconv_bwd(bch, dgated, a_conv_f, "a_gate_bwd")
    dbch = dbch.reshape(3, T, D)
    dxn0 = _matmul(dbch, win_t, mode="nn", out_dtype=F32, a_parts=3, name="a_in_proj_dx")
    d_win_t = _matmul(dbch, xn0, mode="tn", out_dtype=F32, a_parts=3, name="a_in_proj_dw", tm=1024, tn=1024, tk=2048)
    grad_x, _, d_a_norm = _rmsnorm_bwd(x2, a_norm_f, [dxn0], dh1, "a_norm_bwd")

    small_full = [d_a_norm, d_a_conv, jnp.stack([ffn0_grads[2], ffn1_grads[2]]),
                  d_kvb_norm[0], d_kvb_norm[1], d_rel_bias, jnp.concatenate([ffn0_grads[4], ffn1_grads[4]]),
                  jnp.concatenate([ffn0_grads[3], ffn1_grads[3]]), d_final_norm, loss_part]
    small_full_shapes = [(1, D), (3, D), (2, 3, 2 * F), (D,), (1, D), rel_bias.shape, (2, D), (2, 2 * F), (D,), ()]
    token, gath = send_group("a", [(0, d_win_t), (1, d_wout)], small=[_pack_rows(small_full)])
    chip_sums, recv = {}, {}
    for tag, idxs, send, rcv, srcs, lnd in in_flight:
        srcs, lnd = _split_wait(_chip_copies, send, rcv, srcs, lnd, token, f"exchange_chip_wait_{tag}")
        for i, s_, l_ in zip(idxs, srcs, lnd):
            chip_sums[i], recv[i] = s_, l_
    small_sum = _sum_slots(gath[0], "small_grad_sum").reshape(-1)
    (g_a_norm_f, g_a_conv_f, g_ffn_conv_f, g_kv_norm, g_b_norm, g_rel_bias, g_ffn_norm, g_ffn_conv_b,
     g_final_norm, loss) = _unpack(small_sum, small_full_shapes)

    def my_cols(full, width):
        return lax.dynamic_slice_in_dim(full, me * width, width, axis=full.ndim - 1)

    g_a_norm = my_cols(g_a_norm_f, D // N_DEV)
    g_a_conv = my_cols(g_a_conv_f, D // N_DEV)[None]
    g_ffn_conv = my_cols(g_ffn_conv_f, 2 * F // N_DEV)

    big_w = [(a_w_in, m_a_w_in, v_a_w_in, True), (a_w_out, m_a_w_out, v_a_w_out, False),
             (w_kv, m_w_kv, v_w_kv, True), (b_w_q, m_b_w_q, v_b_w_q, False), (b_w_o, m_b_w_o, v_b_w_o, False),
             (ffn_w_up[0], m_ffn_w_up[0], v_ffn_w_up[0], True), (ffn_w_up[1], m_ffn_w_up[1], v_ffn_w_up[1], True),
             (ffn_w_down[0], m_ffn_w_down[0], v_ffn_w_down[0], False),
             (ffn_w_down[1], m_ffn_w_down[1], v_ffn_w_down[1], False)]
    big_out = []
    for i, (w, m, v, transposed) in enumerate(big_w):
        g = _chip_sum(chip_sums[i], recv[i], chip_idx, f"grad_sum_{i}")
        if transposed:
            g = g.T
        w2, m2, v2 = (t.reshape(g.shape) for t in (w, m, v))
        delta, nm, nv = _adamw(g, w2, m2, v2, f"adamw_{i}")
        big_out.append(tuple(t.reshape(w.shape) for t in (g, delta, nm, nv)))

    def pair(i, j):
        return tuple(jnp.stack([big_out[i][t], big_out[j][t]]) for t in range(4))

    small_w = [(a_norm, m_a_norm, v_a_norm, g_a_norm), (a_conv, m_a_conv, v_a_conv, g_a_conv),
               (ffn_conv, m_ffn_conv, v_ffn_conv, g_ffn_conv), (kv_norm, m_kv_norm, v_kv_norm, g_kv_norm),
               (b_norm, m_b_norm, v_b_norm, g_b_norm), (rel_bias, m_rel_bias, v_rel_bias, g_rel_bias),
               (ffn_norm, m_ffn_norm, v_ffn_norm, g_ffn_norm),
               (ffn_conv_b, m_ffn_conv_b, v_ffn_conv_b, g_ffn_conv_b),
               (final_norm, m_final_norm, v_final_norm, g_final_norm)]
    shapes = [t[0].shape for t in small_w]
    packed = [_pack_rows([t[i].reshape(t[0].shape) for t in small_w]) for i in (3, 0, 1, 2)]
    s_delta, s_nm, s_nv = _adamw(*packed, "adamw_small")
    s_g = [t[3].reshape(t[0].shape) for t in small_w]
    s_d, s_m, s_v = (_unpack(t.reshape(-1), shapes) for t in (s_delta, s_nm, s_nv))
    small_out = [(s_g[i], s_d[i], s_m[i], s_v[i]) for i in range(len(small_w))]

    per_weight = [small_out[0], big_out[0], small_out[1], big_out[1], small_out[3], big_out[2], small_out[4],
                  big_out[3], big_out[4], small_out[5], small_out[6], pair(5, 6), small_out[2], small_out[7],
                  pair(7, 8), small_out[8]]
    outs = [loss, grad_x.reshape(B, S, D)]
    for t in range(4):
        outs.extend(pw[t] for pw in per_weight)
    return tuple(outs)
```

```python
import functools
import math

import jax
import jax.numpy as jnp
from jax import lax
from jax.experimental import pallas as pl
from jax.experimental.pallas import tpu as pltpu

F32 = jnp.float32
BF16 = jnp.bfloat16

N_DEV = 8
N_HEADS = 16
ATT_BLOCK = 128
ATT_UNROLL = 8
DILATED_BRANCHES = ((128, 1), (512, 4), (2048, 16))
REL_BUCKETS = 32
REL_MAX_DISTANCE = 2048
RMS_EPS = 1e-6
ADAM_LR = 0.001
ADAM_B1 = 0.9
ADAM_B2 = 0.999
ADAM_EPS = 1e-08
ADAM_WD = 0.01
ADAM_STEP = 10

LANE = 128
HALO = 16
NEG = -1e30
VMEM_CAP = 56 << 20


def _vmem(block_bytes):
    return int(min(VMEM_CAP, max(32 << 20, 3 * block_bytes + (8 << 20))))


def _nbytes(shape, dtype):
    return math.prod(shape) * jnp.dtype(dtype).itemsize


def _tile(dim, target):
    best = None
    t = LANE
    while t <= min(dim, target):
        if dim % t == 0:
            best = t
        t += LANE
    return best if best is not None else dim


def _matmul(a, b, *, mode, out_dtype, name, add=None, after=None, a_parts=1, tm=1024, tn=512, tk=1024):
    P = a_parts
    if mode == "nn":
        M, K = (a.shape[0], a.shape[1]) if P == 1 else (a.shape[1], a.shape[2] * P)
        N = b.shape[1]
    elif mode == "nt":
        assert P == 1
        M, K = a.shape
        N = b.shape[0]
    else:
        K = a.shape[0] if P == 1 else a.shape[1]
        M = a.shape[1] if P == 1 else a.shape[2] * P
        N = b.shape[1]
    tm = _tile(M // P if mode == "tn" else M, tm)
    tn = _tile(N, tn)
    tk = _tile(K // P if mode == "nn" else K, tk)
    nm, nn_, nk = M // tm, N // tn, K // tk

    if mode == "nn":
        if P == 1:
            a_spec = pl.BlockSpec((tm, tk), lambda i, j, k: (i, k))
        else:
            nkp = nk // P
            a_spec = pl.BlockSpec((None, tm, tk), lambda i, j, k: (k // nkp, i, k % nkp))
        b_spec = pl.BlockSpec((tk, tn), lambda i, j, k: (k, j))
        dims = (((1,), (0,)), ((), ()))
    elif mode == "nt":
        a_spec = pl.BlockSpec((tm, tk), lambda i, j, k: (i, k))
        b_spec = pl.BlockSpec((tn, tk), lambda i, j, k: (j, k))
        dims = (((1,), (1,)), ((), ()))
    else:
        if P == 1:
            a_spec = pl.BlockSpec((tk, tm), lambda i, j, k: (k, i))
        else:
            nmp = nm // P
            a_spec = pl.BlockSpec((None, tk, tm), lambda i, j, k: (i // nmp, k, i % nmp))
        b_spec = pl.BlockSpec((tk, tn), lambda i, j, k: (k, j))
        dims = (((0,), (0,)), ((), ()))
    o_spec = pl.BlockSpec((tm, tn), lambda i, j, k: (i, j))
    in_specs = [a_spec, b_spec]
    args = [a, b]
    if add is not None:
        in_specs.append(o_spec)
        args.append(add)
    has_add = add is not None
    if after is not None:
        in_specs.append(pl.BlockSpec(memory_space=pl.ANY))
        args.append(after)
    n_extra = len(args) - 2

    direct = nk > 1 and out_dtype == F32 and not has_add

    def body(a_ref, b_ref, *rest):
        add_ref = rest[0] if has_add else None
        o_ref = rest[n_extra]
        acc_ref = o_ref if direct or nk == 1 else rest[-1]
        k = pl.program_id(2)

        def part():
            return lax.dot_general(a_ref[...].astype(BF16), b_ref[...].astype(BF16), dims,
                                   preferred_element_type=F32)

        def finish(r):
            if has_add:
                r = r + add_ref[...].astype(F32)
            o_ref[...] = r.astype(out_dtype)

        if nk == 1:
            finish(part())
        else:
            @pl.when(k == 0)
            def _():
                acc_ref[...] = part()

            @pl.when(jnp.logical_and(k > 0, jnp.logical_or(k < nk - 1, direct)))
            def _():
                acc_ref[...] += part()

            if not direct:
                @pl.when(k == nk - 1)
                def _():
                    finish(acc_ref[...] + part())

    blk = (_nbytes((tm, tk), a.dtype) + _nbytes((tk, tn), b.dtype) + _nbytes((tm, tn), out_dtype)
           + (_nbytes((tm, tn), add.dtype) if has_add else 0)) * 2 + 3 * _nbytes((tm, tn), F32)
    return pl.pallas_call(
        body, name=name, grid=(nm, nn_, nk),
        in_specs=in_specs, out_specs=o_spec,
        out_shape=jax.ShapeDtypeStruct((M, N), out_dtype),
        scratch_shapes=[] if direct or nk == 1 else [pltpu.VMEM((tm, tn), F32)],
        compiler_params=pltpu.CompilerParams(
            dimension_semantics=("parallel", "parallel", "arbitrary"), vmem_limit_bytes=_vmem(blk)),
    )(*args)


def _rmsnorm_fwd(x, gains, name, tm=512):
    T, D = x.shape
    n = gains.shape[0]
    tm = _tile(T, tm)

    def body(x_ref, g_ref, *o_refs):
        xv = x_ref[...]
        xhat = xv * lax.rsqrt(jnp.mean(xv * xv, axis=-1, keepdims=True) + RMS_EPS)
        for i in range(n):
            o_refs[i][...] = (xhat * g_ref[i:i + 1, :]).astype(BF16)

    row = pl.BlockSpec((tm, D), lambda i: (i, 0))
    outs = pl.pallas_call(
        body, name=name, grid=(T // tm,),
        in_specs=[row, pl.BlockSpec((n, D), lambda i: (0, 0))],
        out_specs=[row] * n,
        out_shape=[jax.ShapeDtypeStruct((T, D), BF16)] * n,
        compiler_params=pltpu.CompilerParams(
            dimension_semantics=("parallel",), vmem_limit_bytes=_vmem(4 * _nbytes((tm, D), F32))),
    )(x, gains)
    return tuple(outs)


def _rmsnorm_bwd(x, gains, dxns, dres, name, tm=512):
    T, D = x.shape
    n = gains.shape[0]
    tm = _tile(T, tm)

    def body(x_ref, g_ref, *rest):
        dxn_refs = rest[:n]
        dres_ref, dx_ref, dxb_ref, dg_ref = rest[n:]
        xv = x_ref[...]
        rstd = lax.rsqrt(jnp.mean(xv * xv, axis=-1, keepdims=True) + RMS_EPS)
        xhat = xv * rstd
        dx = dres_ref[...]

        @pl.when(pl.program_id(0) == 0)
        def _():
            dg_ref[...] = jnp.zeros_like(dg_ref)

        for i in range(n):
            dy = dxn_refs[i][...].astype(F32)
            dg_ref[i:i + 1, :] += jnp.sum(dy * xhat, axis=0, keepdims=True)
            dxh = dy * g_ref[i:i + 1, :]
            dx = dx + rstd * (dxh - xhat * jnp.mean(dxh * xhat, axis=-1, keepdims=True))
        dx_ref[...] = dx
        dxb_ref[...] = dx.astype(BF16)

    row = pl.BlockSpec((tm, D), lambda i: (i, 0))
    par = pl.BlockSpec((n, D), lambda i: (0, 0))
    return pl.pallas_call(
        body, name=name, grid=(T // tm,),
        in_specs=[row, par] + [row] * n + [row],
        out_specs=[row, row, par],
        out_shape=[jax.ShapeDtypeStruct((T, D), F32), jax.ShapeDtypeStruct((T, D), BF16),
                   jax.ShapeDtypeStruct((n, D), F32)],
        compiler_params=pltpu.CompilerParams(
            dimension_semantics=("arbitrary",), vmem_limit_bytes=_vmem((4 + n) * _nbytes((tm, D), F32))),
    )(x, gains, *dxns, dres)


def _final_loss_bwd(h, gain, target, name, tm=512):
    T, D = h.shape
    tm = _tile(T, tm)

    def body(h_ref, g_ref, t_ref, dh_ref, dhb_ref, dg_ref, sq_ref):
        xv = h_ref[...]
        rstd = lax.rsqrt(jnp.mean(xv * xv, axis=-1, keepdims=True) + RMS_EPS)
        xhat = xv * rstd
        err = xhat * g_ref[...] - t_ref[...]

        @pl.when(pl.program_id(0) == 0)
        def _():
            dg_ref[...] = jnp.zeros_like(dg_ref)
            sq_ref[...] = jnp.zeros_like(sq_ref)

        sq_ref[...] += jnp.sum(err * err, axis=0, keepdims=True)
        dy = err * (1.0 / D)
        dg_ref[...] += jnp.sum(dy * xhat, axis=0, keepdims=True)
        dxh = dy * g_ref[...]
        dh = rstd * (dxh - xhat * jnp.mean(dxh * xhat, axis=-1, keepdims=True))
        dh_ref[...] = dh
        dhb_ref[...] = dh.astype(BF16)

    row = pl.BlockSpec((tm, D), lambda i: (i, 0))
    par = pl.BlockSpec((1, D), lambda i: (0, 0))
    return pl.pallas_call(
        body, name=name, grid=(T // tm,),
        in_specs=[row, par, row], out_specs=[row, row, par, par],
        out_shape=[jax.ShapeDtypeStruct((T, D), F32), jax.ShapeDtypeStruct((T, D), BF16),
                   jax.ShapeDtypeStruct((1, D), F32), jax.ShapeDtypeStruct((1, D), F32)],
        compiler_params=pltpu.CompilerParams(
            dimension_semantics=("arbitrary",), vmem_limit_bytes=_vmem(5 * _nbytes((tm, D), F32))),
    )(h, gain, target)


def _halo_specs(S, ts, tc, col_off, prev, nxt):
    r = ts // HALO
    last = S // HALO - 1
    specs = []
    if prev:
        specs.append(pl.BlockSpec((None, HALO, tc), lambda j, b, s: (b, jnp.maximum(s * r - 1, 0), col_off + j)))
    specs.append(pl.BlockSpec((None, ts, tc), lambda j, b, s: (b, s, col_off + j)))
    if nxt:
        specs.append(pl.BlockSpec((None, HALO, tc), lambda j, b, s: (b, jnp.minimum((s + 1) * r, last), col_off + j)))
    return specs


def _ext(prev_ref, main_ref, next_ref, first, last):
    main = main_ref[...].astype(F32)
    zeros = jnp.zeros((HALO, main.shape[1]), F32)
    top = zeros if prev_ref is None else jnp.where(first, 0.0, prev_ref[...].astype(F32))
    bot = zeros if next_ref is None else jnp.where(last, 0.0, next_ref[...].astype(F32))
    return jnp.concatenate([top, main, bot], axis=0)


def _shift(xe, k):
    return pltpu.roll(xe, k % xe.shape[0], axis=0)


def _cconv(xe, w):
    return w[0:1, :] * _shift(xe, 2) + w[1:2, :] * _shift(xe, 1) + w[2:3, :] * xe


def _cconv_t(de, w):
    return w[2:3, :] * de + w[1:2, :] * _shift(de, -1) + w[0:1, :] * _shift(de, -2)


def _main(xe, ts):
    return xe[HALO:HALO + ts, :]


def _conv_wgrad(dw_ref, de, xe, ts):
    d = _main(de, ts)
    dw_ref[0:1, :] += jnp.sum(d * _main(_shift(xe, 2), ts), axis=0, keepdims=True)
    dw_ref[1:2, :] += jnp.sum(d * _main(_shift(xe, 1), ts), axis=0, keepdims=True)
    dw_ref[2:3, :] += jnp.sum(d * _main(xe, ts), axis=0, keepdims=True)


def _gate_tiles(S, C):
    return _tile(S, 512), _tile(C, 512)


def _shortconv_fwd(bch, conv_w, name):
    B, S, D3 = bch.shape
    D = D3 // 3
    ts, tc = _gate_tiles(S, D)
    nj, ns = D // tc, S // ts

    def body(b_ref, cp_ref, c_ref, hp_ref, h_ref, w_ref, o_ref):
        first = pl.program_id(2) == 0
        ce = _ext(cp_ref, c_ref, None, first, False)
        he = _ext(hp_ref, h_ref, None, first, False)
        cv = _main(_cconv(ce * he, w_ref[...]), ts)
        o_ref[...] = (b_ref[...].astype(F32) * cv).astype(BF16)

    in_specs = (_halo_specs(S, ts, tc, 0, False, False) + _halo_specs(S, ts, tc, nj, True, False)
                + _halo_specs(S, ts, tc, 2 * nj, True, False) + [pl.BlockSpec((3, tc), lambda j, b, s: (0, j))])
    return pl.pallas_call(
        body, name=name, grid=(nj, B, ns), in_specs=in_specs,
        out_specs=pl.BlockSpec((None, ts, tc), lambda j, b, s: (b, s, j)),
        out_shape=jax.ShapeDtypeStruct((B, S, D), BF16),
        compiler_params=pltpu.CompilerParams(
            dimension_semantics=("parallel", "parallel", "parallel"),
            vmem_limit_bytes=_vmem(12 * _nbytes((ts + 2 * HALO, tc), F32))),
    )(bch, bch, bch, bch, bch, conv_w)


def _shortconv_bwd(bch, dg, conv_w, name):
    B, S, D3 = bch.shape
    D = D3 // 3
    ts, tc = _gate_tiles(S, D)
    nj, ns = D // tc, S // ts

    def body(b_ref, bn_ref, cp_ref, c_ref, cn_ref, hp_ref, h_ref, hn_ref, d_ref, dn_ref, w_ref, o_ref, dw_ref):
        s = pl.program_id(2)
        first, last = s == 0, s == ns - 1

        @pl.when(jnp.logical_and(pl.program_id(1) == 0, s == 0))
        def _():
            dw_ref[...] = jnp.zeros_like(dw_ref)

        w = w_ref[...]
        be = _ext(None, b_ref, bn_ref, first, last)
        ce = _ext(cp_ref, c_ref, cn_ref, first, last)
        he = _ext(hp_ref, h_ref, hn_ref, first, last)
        de = _ext(None, d_ref, dn_ref, first, last)
        ch = ce * he
        dcv = de * be
        dch = _main(_cconv_t(dcv, w), ts)
        o_ref[0] = (_main(de, ts) * _main(_cconv(ch, w), ts)).astype(BF16)
        o_ref[1] = (dch * _main(he, ts)).astype(BF16)
        o_ref[2] = (dch * _main(ce, ts)).astype(BF16)
        _conv_wgrad(dw_ref, dcv, ch, ts)

    in_specs = (_halo_specs(S, ts, tc, 0, False, True) + _halo_specs(S, ts, tc, nj, True, True)
                + _halo_specs(S, ts, tc, 2 * nj, True, True) + _halo_specs(S, ts, tc, 0, False, True)
                + [pl.BlockSpec((3, tc), lambda j, b, s: (0, j))])
    return pl.pallas_call(
        body, name=name, grid=(nj, B, ns), in_specs=in_specs,
        out_specs=[pl.BlockSpec((3, None, ts, tc), lambda j, b, s: (0, b, s, j)),
                   pl.BlockSpec((3, tc), lambda j, b, s: (0, j))],
        out_shape=[jax.ShapeDtypeStruct((3, B, S, D), BF16), jax.ShapeDtypeStruct((3, D), F32)],
        compiler_params=pltpu.CompilerParams(
            dimension_semantics=("parallel", "arbitrary", "arbitrary"),
            vmem_limit_bytes=_vmem(24 * _nbytes((ts + 2 * HALO, tc), F32))),
    )(bch, bch, bch, bch, bch, bch, bch, bch, dg, dg, conv_w)


def _sigmoid(x):
    return 1.0 / (1.0 + jnp.exp(-x))


def _ffn_gate_fwd(u0, conv_w, conv_b, name):
    B, S, F2 = u0.shape
    F = F2 // 2
    ts, tc = _gate_tiles(S, F)
    nj, ns = F // tc, S // ts

    def body(gp_ref, g_ref, up_ref, u_ref, wg_ref, wu_ref, bg_ref, bu_ref, o_ref):
        first = pl.program_id(2) == 0
        ug = _main(_cconv(_ext(gp_ref, g_ref, None, first, False), wg_ref[...]), ts) + bg_ref[...]
        uu = _main(_cconv(_ext(up_ref, u_ref, None, first, False), wu_ref[...]), ts) + bu_ref[...]
        o_ref[...] = (ug * _sigmoid(ug) * uu).astype(BF16)

    in_specs = (_halo_specs(S, ts, tc, 0, True, False) + _halo_specs(S, ts, tc, nj, True, False)
                + [pl.BlockSpec((3, tc), lambda j, b, s: (0, j)), pl.BlockSpec((3, tc), lambda j, b, s: (0, nj + j)),
                   pl.BlockSpec((1, tc), lambda j, b, s: (0, j)), pl.BlockSpec((1, tc), lambda j, b, s: (0, nj + j))])
    return pl.pallas_call(
        body, name=name, grid=(nj, B, ns), in_specs=in_specs,
        out_specs=pl.BlockSpec((None, ts, tc), lambda j, b, s: (b, s, j)),
        out_shape=jax.ShapeDtypeStruct((B, S, F), BF16),
        compiler_params=pltpu.CompilerParams(
            dimension_semantics=("parallel", "parallel", "parallel"),
            vmem_limit_bytes=_vmem(12 * _nbytes((ts + 2 * HALO, tc), F32))),
    )(u0, u0, u0, u0, conv_w, conv_w, conv_b, conv_b)


def _ffn_gate_bwd(u0, dact, conv_w, conv_b, name):
    B, S, F2 = u0.shape
    F = F2 // 2
    ts, tc = _gate_tiles(S, F)
    nj, ns = F // tc, S // ts

    def body(gp_ref, g_ref, gn_ref, up_ref, u_ref, un_ref, d_ref, dn_ref, wg_ref, wu_ref, bg_ref, bu_ref,
             o_ref, dwg_ref, dwu_ref, dbg_ref, dbu_ref):
        s = pl.program_id(2)
        first, last = s == 0, s == ns - 1

        @pl.when(jnp.logical_and(pl.program_id(1) == 0, s == 0))
        def _():
            for r in (dwg_ref, dwu_ref, dbg_ref, dbu_ref):
                r[...] = jnp.zeros_like(r)

        wg, wu = wg_ref[...], wu_ref[...]
        ge = _ext(gp_ref, g_ref, gn_ref, first, last)
        ue = _ext(up_ref, u_ref, un_ref, first, last)
        de = _ext(None, d_ref, dn_ref, first, last)
        ug = _cconv(ge, wg) + bg_ref[...]
        uu = _cconv(ue, wu) + bu_ref[...]
        sig = _sigmoid(ug)
        dug = de * uu * (sig * (1.0 + ug * (1.0 - sig)))
        duu = de * (ug * sig)
        o_ref[0] = _main(_cconv_t(dug, wg), ts).astype(BF16)
        o_ref[1] = _main(_cconv_t(duu, wu), ts).astype(BF16)
        _conv_wgrad(dwg_ref, dug, ge, ts)
        _conv_wgrad(dwu_ref, duu, ue, ts)
        dbg_ref[...] += jnp.sum(_main(dug, ts), axis=0, keepdims=True)
        dbu_ref[...] += jnp.sum(_main(duu, ts), axis=0, keepdims=True)

    w3 = lambda off: pl.BlockSpec((3, tc), lambda j, b, s: (0, off + j))
    w1 = lambda off: pl.BlockSpec((1, tc), lambda j, b, s: (0, off + j))
    in_specs = (_halo_specs(S, ts, tc, 0, True, True) + _halo_specs(S, ts, tc, nj, True, True)
                + _halo_specs(S, ts, tc, 0, False, True) + [w3(0), w3(nj), w1(0), w1(nj)])
    outs = pl.pallas_call(
        body, name=name, grid=(nj, B, ns), in_specs=in_specs,
        out_specs=[pl.BlockSpec((2, None, ts, tc), lambda j, b, s: (0, b, s, j)), w3(0), w3(0), w1(0), w1(0)],
        out_shape=[jax.ShapeDtypeStruct((2, B, S, F), BF16), jax.ShapeDtypeStruct((3, F), F32),
                   jax.ShapeDtypeStruct((3, F), F32), jax.ShapeDtypeStruct((1, F), F32),
                   jax.ShapeDtypeStruct((1, F), F32)],
        compiler_params=pltpu.CompilerParams(
            dimension_semantics=("parallel", "arbitrary", "arbitrary"),
            vmem_limit_bytes=_vmem(30 * _nbytes((ts + 2 * HALO, tc), F32))),
    )(u0, u0, u0, u0, u0, u0, dact, dact, conv_w, conv_w, conv_b, conv_b)
    du0, dwg, dwu, dbg, dbu = outs
    return du0, jnp.concatenate([dwg, dwu], axis=1), jnp.concatenate([dbg, dbu], axis=1)


def _t5_bucket(dist):
    max_exact = REL_BUCKETS // 2
    n = jnp.maximum(dist, 0)
    nf = jnp.maximum(n, max_exact).astype(F32)
    large = max_exact + (jnp.log(nf / max_exact) / math.log(REL_MAX_DISTANCE / max_exact)
                         * (REL_BUCKETS - max_exact)).astype(jnp.int32)
    large = jnp.minimum(large, REL_BUCKETS - 1)
    return jnp.where(n < max_exact, n, large)


def _band_tables(window, dilation):
    P = ATT_BLOCK
    qi = jnp.arange(P, dtype=jnp.int32)[:, None]
    kc = jnp.arange(2 * P, dtype=jnp.int32)[None, :]
    delta = qi + P - kc
    band = (delta >= 0) & (delta <= window // dilation)
    bucket = _t5_bucket(delta * dilation).reshape(-1)
    onehot = (bucket[None, :] == jnp.arange(REL_BUCKETS, dtype=jnp.int32)[:, None]).astype(F32)
    return onehot, band


def _bias_lookup(rel_bias_t, onehots, name):
    nb, _, Q = onehots.shape
    H = rel_bias_t.shape[0]

    def body(r_ref, oh_ref, o_ref):
        o_ref[...] = lax.dot_general(r_ref[...], oh_ref[...], (((1,), (0,)), ((), ())),
                                     precision=lax.Precision.HIGHEST, preferred_element_type=F32)

    return pl.pallas_call(
        body, name=name, grid=(nb,),
        in_specs=[pl.BlockSpec((H, REL_BUCKETS), lambda i: (0, 0)),
                  pl.BlockSpec((None, REL_BUCKETS, Q), lambda i: (i, 0, 0))],
        out_specs=pl.BlockSpec((None, H, Q), lambda i: (i, 0, 0)),
        out_shape=jax.ShapeDtypeStruct((nb, H, Q), F32),
        compiler_params=pltpu.CompilerParams(dimension_semantics=("parallel",),
                                             vmem_limit_bytes=_vmem(_nbytes((REL_BUCKETS + H, Q), F32))),
    )(rel_bias_t, onehots)


def _bias_grad(dtabs, onehots, name):
    nb, H, Q = dtabs.shape

    def body(d_ref, oh_ref, o_ref):
        @pl.when(pl.program_id(0) == 0)
        def _():
            o_ref[...] = jnp.zeros_like(o_ref)

        o_ref[...] += lax.dot_general(d_ref[...], oh_ref[...], (((1,), (1,)), ((), ())),
                                      precision=lax.Precision.HIGHEST, preferred_element_type=F32)

    return pl.pallas_call(
        body, name=name, grid=(nb,),
        in_specs=[pl.BlockSpec((None, H, Q), lambda i: (i, 0, 0)),
                  pl.BlockSpec((None, REL_BUCKETS, Q), lambda i: (i, 0, 0))],
        out_specs=pl.BlockSpec((H, REL_BUCKETS), lambda i: (0, 0)),
        out_shape=jax.ShapeDtypeStruct((H, REL_BUCKETS), F32),
        compiler_params=pltpu.CompilerParams(dimension_semantics=("arbitrary",),
                                             vmem_limit_bytes=_vmem(_nbytes((REL_BUCKETS + H, Q), F32))),
    )(dtabs, onehots)


def _lane_masks():
    lane = lax.broadcasted_iota(jnp.int32, (1, LANE), 1)
    lo = lane < LANE // 2
    return lo, jnp.logical_not(lo)


def _dot_nt(a, b):
    return lax.dot_general(a, b, (((1,), (1,)), ((), ())), preferred_element_type=F32)


def _dot_nn(a, b):
    return lax.dot_general(a, b, (((1,), (0,)), ((), ())), preferred_element_type=F32)


def _dot_tn(a, b):
    return lax.dot_general(a, b, (((0,), (0,)), ((), ())), preferred_element_type=F32)


def _block_rows(n, dilation, S):
    P = ATT_BLOCK
    nb = S // (dilation * P)
    r, i = n // nb, n % nb
    cur = pl.ds(i * (P * dilation) + r, P, stride=dilation)
    prv = pl.ds(jnp.maximum(i - 1, 0) * (P * dilation) + r, P, stride=dilation)
    return cur, prv, jnp.minimum(i, 1)


def _attn_fwd(q, kv, tabs, name):
    B, S, D = q.shape
    P, H = ATT_BLOCK, N_HEADS
    scale = (D // H) ** -0.5
    half = LANE // 2
    nsteps = S // P // ATT_UNROLL

    def body(q_ref, k_ref, v_ref, tab_ref, o_ref, lse_ref, acc_ref, m_ref, l_ref):
        lo, hi = _lane_masks()
        for bi, (_, d) in enumerate(DILATED_BRANCHES):
            def block(n, carry, bi=bi, d=d):
                rows = [_block_rows(n + j * nsteps, d, S) for j in range(ATT_UNROLL)]
                loaded = []
                for cur, prv, variant in rows:
                    qb = q_ref[cur, :].astype(BF16)
                    kw = jnp.concatenate([k_ref[prv, :], k_ref[cur, :]], axis=0).astype(BF16)
                    vw = jnp.concatenate([v_ref[prv, :], v_ref[cur, :]], axis=0).astype(BF16)
                    loaded.append((qb, kw, vw))
                results = []
                for (cur, prv, variant), (qb, kw, vw) in zip(rows, loaded):
                    m_pair = l_pair = a_pair = None
                    for e, msk in enumerate((lo, hi)):
                        s = _dot_nt(jnp.where(msk, qb, 0), kw) * scale + tab_ref[bi, variant, e]
                        m_e = jnp.max(s, axis=-1, keepdims=True)
                        pe = jnp.exp(s - m_e)
                        l_e = jnp.sum(pe, axis=-1, keepdims=True)
                        a_e = _dot_nn(pe.astype(BF16), jnp.where(msk, vw, 0))
                        m_pair = m_e if e == 0 else jnp.where(lo, m_pair, m_e)
                        l_pair = l_e if e == 0 else jnp.where(lo, l_pair, l_e)
                        a_pair = a_e if e == 0 else a_pair + a_e
                    results.append((m_pair, l_pair, a_pair))
                for (cur, prv, variant), (m_pair, l_pair, a_pair) in zip(rows, results):
                    m_ref[bi, cur, :] = m_pair
                    l_ref[bi, cur, :] = l_pair
                    acc_ref[bi, cur, :] = a_pair
                return carry

            lax.fori_loop(0, nsteps, block, 0)

        nbr = len(DILATED_BRANCHES)
        chunk = 512

        def merge(t, carry):
            rs = pl.ds(pl.multiple_of(t * chunk, chunk), chunk)
            ms = [m_ref[b, rs, :] for b in range(nbr)]
            m = functools.reduce(jnp.maximum, ms)
            ws = [jnp.exp(mb - m) for mb in ms]
            l = functools.reduce(jnp.add, [w * l_ref[b, rs, :] for b, w in enumerate(ws)])
            acc = functools.reduce(jnp.add, [w * acc_ref[b, rs, :] for b, w in enumerate(ws)])
            o_ref[rs, :] = (acc / l).astype(BF16)
            lse_ref[rs, :] = m + jnp.log(l)
            return carry

        lax.fori_loop(0, S // chunk, merge, 0)

    nl = D // LANE
    col = lambda off: pl.BlockSpec((None, S, LANE), lambda b, p: (b, 0, off + p))
    tab_spec = pl.BlockSpec((3, 2, 2, P, 2 * P), lambda b, p: (0, 0, p, 0, 0))
    return pl.pallas_call(
        body, name=name, grid=(B, H // 2), in_specs=[col(0), col(0), col(nl), tab_spec],
        out_specs=[col(0), col(0)],
        out_shape=[jax.ShapeDtypeStruct((B, S, D), BF16), jax.ShapeDtypeStruct((B, S, D), F32)],
        scratch_shapes=[pltpu.VMEM((len(DILATED_BRANCHES), S, LANE), F32)] * 3,
        compiler_params=pltpu.CompilerParams(
            dimension_semantics=("parallel", "parallel"), vmem_limit_bytes=VMEM_CAP),
    )(q, kv, kv, tabs)


def _attn_bwd(q, kv, do, o, lse, tabs, name):
    B, S, D = q.shape
    P, H = ATT_BLOCK, N_HEADS
    scale = (D // H) ** -0.5
    half = LANE // 2
    nsteps = S // P // ATT_UNROLL

    def body(q_ref, k_ref, v_ref, do_ref, o_ref, lse_ref, tab_ref, dq_ref, dkv_ref, dtab_ref, delta_ref):
        lo, hi = _lane_masks()

        @pl.when(pl.program_id(1) == 0)
        def _():
            dtab_ref[...] = jnp.zeros_like(dtab_ref)

        dkv_ref[...] = jnp.zeros_like(dkv_ref)
        prod = do_ref[...] * o_ref[...].astype(F32)
        delta_ref[...] = jnp.where(lo, jnp.sum(jnp.where(lo, prod, 0.0), axis=-1, keepdims=True),
                                   jnp.sum(jnp.where(hi, prod, 0.0), axis=-1, keepdims=True))

        for bi, (_, d) in enumerate(DILATED_BRANCHES):
            def block(n, carry, bi=bi, d=d):
                rows = [_block_rows(n + j * nsteps, d, S) for j in range(ATT_UNROLL)]
                loaded = []
                for cur, prv, variant in rows:
                    qb = q_ref[cur, :].astype(BF16)
                    kw = jnp.concatenate([k_ref[prv, :], k_ref[cur, :]], axis=0).astype(BF16)
                    vw = jnp.concatenate([v_ref[prv, :], v_ref[cur, :]], axis=0).astype(BF16)
                    dob = do_ref[cur, :].astype(BF16)
                    dq_old = dq_ref[cur, :] if bi > 0 else None
                    loaded.append((qb, kw, vw, dob, lse_ref[cur, :], delta_ref[cur, :], dq_old))
                results = []
                for j, ((cur, prv, variant), (qb, kw, vw, dob, lse_b, dl_b, dq_old)) in enumerate(zip(rows, loaded)):
                    dq_pair = dk_pair = dv_pair = None
                    for e, msk in enumerate((lo, hi)):
                        qm = jnp.where(msk, qb, 0)
                        dom = jnp.where(msk, dob, 0)
                        s = _dot_nt(qm, kw) * scale + tab_ref[bi, variant, e]
                        pe = jnp.exp(s - lse_b[:, e * half:e * half + 1])
                        ds = pe * (_dot_nt(dom, vw) - dl_b[:, e * half:e * half + 1])
                        dtab_ref[bi, e] += ds
                        dsb = ds.astype(BF16)
                        dq_e = _dot_nn(dsb, jnp.where(msk, kw, 0))
                        dk_e = _dot_tn(dsb, qm)
                        dv_e = _dot_tn(pe.astype(BF16), dom)
                        dq_pair = dq_e if e == 0 else dq_pair + dq_e
                        dk_pair = dk_e if e == 0 else dk_pair + dk_e
                        dv_pair = dv_e if e == 0 else dv_pair + dv_e
                    dq_pair = dq_pair * scale
                    if bi > 0:
                        dq_pair = dq_pair + dq_old
                    results.append((dq_pair, dk_pair * scale, dv_pair))
                for (cur, prv, variant), (dq_pair, dk_pair, dv_pair) in zip(rows, results):
                    dq_ref[cur, :] = dq_pair
                    dkv_ref[0, cur, :] += dk_pair[P:, :]
                    dkv_ref[1, cur, :] += dv_pair[P:, :]
                    dkv_ref[0, prv, :] += dk_pair[:P, :]
                    dkv_ref[1, prv, :] += dv_pair[:P, :]
                return carry

            lax.fori_loop(0, nsteps, block, 0)

    nl = D // LANE
    col = lambda off: pl.BlockSpec((None, S, LANE), lambda p, b: (b, 0, off + p))
    tab_spec = pl.BlockSpec((3, 2, 2, P, 2 * P), lambda p, b: (0, 0, p, 0, 0))
    return pl.pallas_call(
        body, name=name, grid=(H // 2, B),
        in_specs=[col(0), col(0), col(nl), col(0), col(0), col(0), tab_spec],
        out_specs=[col(0), pl.BlockSpec((2, None, S, LANE), lambda p, b: (0, b, 0, p)),
                   pl.BlockSpec((None, 3, 2, P, 2 * P), lambda p, b: (p, 0, 0, 0, 0))],
        out_shape=[jax.ShapeDtypeStruct((B, S, D), F32), jax.ShapeDtypeStruct((2, B, S, D), F32),
                   jax.ShapeDtypeStruct((H // 2, 3, 2, P, 2 * P), F32)],
        scratch_shapes=[pltpu.VMEM((S, LANE), F32)],
        compiler_params=pltpu.CompilerParams(
            dimension_semantics=("parallel", "arbitrary"),
            vmem_limit_bytes=VMEM_CAP),
    )(q, kv, kv, do, o, lse, tabs)


def _attn_fwd_branch(q, kv, tab, dilation, name):
    B, S, D = q.shape
    P, d = ATT_BLOCK, dilation
    L = S // d
    nb = L // P
    H = N_HEADS
    scale = (D // H) ** -0.5
    qv = q.reshape(B, L, d * D)
    kvv = kv.reshape(B, L, d * 2 * D)

    def body(q_ref, kp_ref, kc_ref, vp_ref, vc_ref, tab_ref, o_ref, lse_ref):
        lo, hi = _lane_masks()
        for p in range(H // 2):
            sl = slice(p * LANE, (p + 1) * LANE)
            qp = q_ref[:, sl]
            kp = jnp.concatenate([kp_ref[:, sl], kc_ref[:, sl]], axis=0)
            vp = jnp.concatenate([vp_ref[:, sl], vc_ref[:, sl]], axis=0)
            o_pair = None
            lse_pair = None
            for e, msk in enumerate((lo, hi)):
                s = _dot_nt(jnp.where(msk, qp, 0), kp) * scale + tab_ref[2 * p + e]
                m = jnp.max(s, axis=-1, keepdims=True)
                pe = jnp.exp(s - m)
                den = jnp.sum(pe, axis=-1, keepdims=True)
                o_e = _dot_nn(pe.astype(BF16), jnp.where(msk, vp, 0)) / den
                lse_e = m + jnp.log(den)
                o_pair = o_e if e == 0 else o_pair + o_e
                lse_pair = lse_e if e == 0 else jnp.where(lo, lse_pair, lse_e)
            o_ref[:, sl] = o_pair
            lse_ref[:, sl] = lse_pair

    blk = lambda f: pl.BlockSpec((None, P, D), f)
    prev = lambda i: jnp.maximum(i - 1, 0)
    in_specs = [blk(lambda b, r, i: (b, i, r)),
                blk(lambda b, r, i: (b, prev(i), 2 * r)), blk(lambda b, r, i: (b, i, 2 * r)),
                blk(lambda b, r, i: (b, prev(i), 2 * r + 1)), blk(lambda b, r, i: (b, i, 2 * r + 1)),
                pl.BlockSpec((None, H, P, 2 * P), lambda b, r, i: (jnp.minimum(i, 1), 0, 0, 0))]
    o, lse = pl.pallas_call(
        body, name=name, grid=(B, d, nb), in_specs=in_specs,
        out_specs=[blk(lambda b, r, i: (b, i, r))] * 2,
        out_shape=[jax.ShapeDtypeStruct((B, L, d * D), F32)] * 2,
        compiler_params=pltpu.CompilerParams(
            dimension_semantics=("parallel", "parallel", "parallel"),
            vmem_limit_bytes=_vmem(_nbytes((H, P, 2 * P), F32) + 9 * _nbytes((P, D), F32))),
    )(qv, kvv, kvv, kvv, kvv, tab)
    return o.reshape(B * S, D), lse.reshape(B * S, D)


def _attn_merge(os_, lses, name, tm=512):
    T, D = os_[0].shape
    n = len(os_)
    tm = _tile(T, tm)

    def body(*refs):
        o_refs, l_refs = refs[:n], refs[n:2 * n]
        out_ref, lse_ref = refs[2 * n:]
        ls = [r[...] for r in l_refs]
        m = functools.reduce(jnp.maximum, ls)
        ws = [jnp.exp(l - m) for l in ls]
        tot = functools.reduce(jnp.add, ws)
        acc = functools.reduce(jnp.add, [w * r[...] for w, r in zip(ws, o_refs)])
        out_ref[...] = (acc / tot).astype(BF16)
        lse_ref[...] = m + jnp.log(tot)

    row = pl.BlockSpec((tm, D), lambda i: (i, 0))
    return pl.pallas_call(
        body, name=name, grid=(T // tm,), in_specs=[row] * (2 * n), out_specs=[row, row],
        out_shape=[jax.ShapeDtypeStruct((T, D), BF16), jax.ShapeDtypeStruct((T, D), F32)],
        compiler_params=pltpu.CompilerParams(
            dimension_semantics=("parallel",), vmem_limit_bytes=_vmem((2 * n + 4) * _nbytes((tm, D), F32))),
    )(*os_, *lses)


def _attn_bwd_q(q, kv, do, o, lse, tab, dq_acc, dilation, name):
    B, S, D = q.shape
    P, d = ATT_BLOCK, dilation
    L = S // d
    nb = L // P
    H = N_HEADS
    scale = (D // H) ** -0.5
    view = lambda t, c: t.reshape(B, L, d * c)
    has_acc = dq_acc is not None

    def body(q_ref, kp_ref, kc_ref, vp_ref, vc_ref, do_ref, o_ref, lse_ref, tab_ref, *rest):
        if has_acc:
            acc_ref, dq_ref, dtab_ref = rest
        else:
            dq_ref, dtab_ref = rest

        @pl.when((pl.program_id(0) == 0) & (pl.program_id(1) == 0) & (pl.program_id(2) == 0))
        def _():
            dtab_ref[...] = jnp.zeros_like(dtab_ref)

        lo, hi = _lane_masks()
        for p in range(H // 2):
            sl = slice(p * LANE, (p + 1) * LANE)
            qp = q_ref[:, sl]
            kp = jnp.concatenate([kp_ref[:, sl], kc_ref[:, sl]], axis=0)
            vp = jnp.concatenate([vp_ref[:, sl], vc_ref[:, sl]], axis=0)
            dop = do_ref[:, sl]
            prod = dop.astype(F32) * o_ref[:, sl].astype(F32)
            lsep = lse_ref[:, sl]
            dq_pair = None
            for e, msk in enumerate((lo, hi)):
                s = _dot_nt(jnp.where(msk, qp, 0), kp) * scale + tab_ref[2 * p + e]
                pe = jnp.exp(s - lsep[:, e * (LANE // 2):e * (LANE // 2) + 1])
                dp = _dot_nt(jnp.where(msk, dop, 0).astype(BF16), vp)
                delta = jnp.sum(jnp.where(msk, prod, 0.0), axis=-1, keepdims=True)
                ds = pe * (dp - delta)
                dtab_ref[2 * p + e] += ds
                dq_e = _dot_nn(ds.astype(BF16), jnp.where(msk, kp, 0))
                dq_pair = dq_e if e == 0 else dq_pair + dq_e
            dq_pair = dq_pair * scale
            if has_acc:
                dq_pair = dq_pair + acc_ref[:, sl]
            dq_ref[:, sl] = dq_pair

    blk = lambda f: pl.BlockSpec((None, P, D), f)
    prev = lambda i: jnp.maximum(i - 1, 0)
    own = blk(lambda b, r, i: (b, i, r))
    tab_spec = pl.BlockSpec((None, H, P, 2 * P), lambda b, r, i: (jnp.minimum(i, 1), 0, 0, 0))
    in_specs = [own,
                blk(lambda b, r, i: (b, prev(i), 2 * r)), blk(lambda b, r, i: (b, i, 2 * r)),
                blk(lambda b, r, i: (b, prev(i), 2 * r + 1)), blk(lambda b, r, i: (b, i, 2 * r + 1)),
                own, own, own, tab_spec]
    kvv = view(kv, 2 * D)
    args = [view(q, D), kvv, kvv, kvv, kvv, view(do, D), view(o, D), view(lse, D), tab]
    if has_acc:
        in_specs.append(own)
        args.append(view(dq_acc, D))
    dq, dtab = pl.pallas_call(
        body, name=name, grid=(B, d, nb), in_specs=in_specs,
        out_specs=[own, pl.BlockSpec((H, P, 2 * P), lambda b, r, i: (0, 0, 0))],
        out_shape=[jax.ShapeDtypeStruct((B, L, d * D), F32), jax.ShapeDtypeStruct((H, P, 2 * P), F32)],
        compiler_params=pltpu.CompilerParams(
            dimension_semantics=("arbitrary", "arbitrary", "arbitrary"),
            vmem_limit_bytes=_vmem(2 * _nbytes((H, P, 2 * P), F32) + 12 * _nbytes((P, D), F32))),
    )(*args)
    return dq.reshape(B, S, D), dtab


def _attn_bwd_kv(q, kv, do, o, lse, tabk, dkv_acc, dilation, name):
    B, S, D = q.shape
    P, d = ATT_BLOCK, dilation
    L = S // d
    nb = L // P
    H = N_HEADS
    scale = (D // H) ** -0.5
    view = lambda t, c: t.reshape(B, L, d * c)
    has_acc = dkv_acc is not None

    def body(k_ref, v_ref, qa_ref, qb_ref, doa_ref, dob_ref, oa_ref, ob_ref, la_ref, lb_ref, tab_ref, *rest):
        if has_acc:
            acc_ref, dkv_ref = rest
        else:
            (dkv_ref,) = rest
        lo, hi = _lane_masks()
        for p in range(H // 2):
            sl = slice(p * LANE, (p + 1) * LANE)
            kp, vp = k_ref[:, sl], v_ref[:, sl]
            q2 = jnp.concatenate([qa_ref[:, sl], qb_ref[:, sl]], axis=0)
            do2 = jnp.concatenate([doa_ref[:, sl], dob_ref[:, sl]], axis=0)
            o2 = jnp.concatenate([oa_ref[:, sl], ob_ref[:, sl]], axis=0)
            lse2 = jnp.concatenate([la_ref[:, sl], lb_ref[:, sl]], axis=0)
            prod = do2.astype(F32) * o2.astype(F32)
            dk_pair = None
            dv_pair = None
            for e, msk in enumerate((lo, hi)):
                qm = jnp.where(msk, q2, 0)
                dom = jnp.where(msk, do2, 0).astype(BF16)
                s = _dot_nt(qm, kp) * scale + tab_ref[2 * p + e]
                pe = jnp.exp(s - lse2[:, e * (LANE // 2):e * (LANE // 2) + 1])
                dp = _dot_nt(dom, vp)
                delta = jnp.sum(jnp.where(msk, prod, 0.0), axis=-1, keepdims=True)
                ds = pe * (dp - delta)
                dv_e = _dot_tn(pe.astype(BF16), dom)
                dk_e = _dot_tn(ds.astype(BF16), qm)
                dk_pair = dk_e if e == 0 else dk_pair + dk_e
                dv_pair = dv_e if e == 0 else dv_pair + dv_e
            dk_pair = dk_pair * scale
            if has_acc:
                dk_pair = dk_pair + acc_ref[0, :, sl]
                dv_pair = dv_pair + acc_ref[1, :, sl]
            dkv_ref[0, :, sl] = dk_pair
            dkv_ref[1, :, sl] = dv_pair

    blk = lambda f: pl.BlockSpec((None, P, D), f)
    nxt = lambda i: jnp.minimum(i + 1, nb - 1)
    own = blk(lambda b, r, i: (b, i, r))
    nx = blk(lambda b, r, i: (b, nxt(i), r))
    tab_spec = pl.BlockSpec((None, H, 2 * P, P), lambda b, r, i: (jnp.where(i == nb - 1, 1, 0), 0, 0, 0))
    pair = pl.BlockSpec((2, None, P, D), lambda b, r, i: (0, b, i, r))
    in_specs = [blk(lambda b, r, i: (b, i, 2 * r)), blk(lambda b, r, i: (b, i, 2 * r + 1)),
                own, nx, own, nx, own, nx, own, nx, tab_spec]
    kvv, qv, dov, ov, lv = view(kv, 2 * D), view(q, D), view(do, D), view(o, D), view(lse, D)
    args = [kvv, kvv, qv, qv, dov, dov, ov, ov, lv, lv, tabk]
    if has_acc:
        in_specs.append(pair)
        args.append(dkv_acc.reshape(2, B, L, d * D))
    dkv = pl.pallas_call(
        body, name=name, grid=(B, d, nb), in_specs=in_specs, out_specs=pair,
        out_shape=jax.ShapeDtypeStruct((2, B, L, d * D), F32),
        compiler_params=pltpu.CompilerParams(
            dimension_semantics=("parallel", "parallel", "parallel"),
            vmem_limit_bytes=_vmem(_nbytes((H, P, 2 * P), F32) + 16 * _nbytes((P, D), F32))),
    )(*args)
    return dkv.reshape(2, B, S, D)


def _mesh_place():
    x, y, c = lax.axis_index("x"), lax.axis_index("y"), lax.axis_index("c")
    return x, y, c, (x, y, 1 - c), [(1 - x, y), (x, 1 - y), (1 - x, 1 - y)]


def _rdma(src, dst, send_sems, recv_sems, idx, to):
    return pltpu.make_async_remote_copy(
        src_ref=src, dst_ref=dst, send_sem=send_sems.at[idx], recv_sem=recv_sems.at[idx],
        device_id=to, device_id_type=pl.DeviceIdType.MESH)


def _comm_call(body, name, args, out_shape, n_sems, n_local):
    any_spec = pl.BlockSpec(memory_space=pl.ANY)
    return pl.pallas_call(
        body, name=name, in_specs=[any_spec] * len(args), out_specs=[any_spec] * len(out_shape),
        out_shape=out_shape,
        scratch_shapes=[pltpu.SemaphoreType.DMA(n_sems), pltpu.SemaphoreType.DMA(n_sems),
                        pltpu.SemaphoreType.DMA((n_local,))],
        compiler_params=pltpu.CompilerParams(has_side_effects=True),
    )(*args)


def _all_gather(arrays, name):
    n = len(arrays)

    def body(*refs):
        ins, outs = refs[:n], refs[n:2 * n]
        send_sems, recv_sems, loc_sems = refs[2 * n:]
        x, y, c, sib, chips = _mesh_place()
        me = (x, y, c)

        def rows(a, dev):
            return outs[a].at[4 * dev[0] + 2 * dev[1] + dev[2]]

        def copy(a, k, block, to, src=None):
            return _rdma(rows(a, block) if src is None else src, rows(a, block), send_sems, recv_sems, (a, k), to)

        local, first, passed = [], [], []
        for a in range(n):
            cp = pltpu.make_async_copy(ins[a], rows(a, me), loc_sems.at[a])
            cp.start()
            local.append(cp)
            first.append(copy(a, 0, me, sib, src=ins[a]))
            first += [copy(a, 1 + j, me, (*chip, c), src=ins[a]) for j, chip in enumerate(chips)]
        for cp in first:
            cp.start()
        for j, chip in enumerate(chips):
            for a in range(n):
                copy(a, 1 + j, (*chip, c), me).wait_recv()
                cp = copy(a, 4 + j, (*chip, c), sib)
                cp.start()
                passed.append(cp)
        for a in range(n):
            copy(a, 0, sib, me).wait_recv()
            for j, chip in enumerate(chips):
                copy(a, 4 + j, (*chip, 1 - c), me).wait_recv()
        for cp in first + passed:
            cp.wait_send()
        for cp in local:
            cp.wait()

    out_shape = [jax.ShapeDtypeStruct((N_DEV,) + g.shape, g.dtype) for g in arrays]
    return list(_comm_call(body, name, arrays, out_shape, (n, N_DEV - 1), n))


def _core_pair_exchange(halves, gathered, name):
    nh, ng = len(halves), len(gathered)

    def body(*refs):
        h_in, g_in = refs[:nh], refs[nh:nh + ng]
        outs = refs[nh + ng:nh + ng + nh + ng]
        theirs, g_out = outs[:nh], outs[nh:]
        send_sems, recv_sems, loc_sems = refs[nh + ng + nh + ng:]
        x, y, c, sib, chips = _mesh_place()
        me = 4 * x + 2 * y + c
        peers = [sib] + [(*chip, pc) for chip in chips for pc in (c, 1 - c)]
        nchip = N_DEV // 2
        gbase = nh * nchip
        local, sends = [], []
        for a in range(nh):
            for ch in range(nchip):
                cp = _rdma(h_in[a].at[ch, 1 - c], theirs[a].at[ch], send_sems, recv_sems, a * nchip + ch, sib)
                cp.start()
                sends.append(cp)
        for g in range(ng):
            cp = pltpu.make_async_copy(g_in[g], g_out[g].at[me], loc_sems.at[g])
            cp.start()
            local.append(cp)
            for k, peer in enumerate(peers):
                cp = _rdma(g_in[g], g_out[g].at[me], send_sems, recv_sems, gbase + g * (N_DEV - 1) + k, peer)
                cp.start()
                sends.append(cp)
        for cp in sends:
            cp.wait_send()
        for a in range(nh):
            for ch in range(nchip):
                _rdma(h_in[a].at[ch, 1 - c], theirs[a].at[ch], send_sems, recv_sems, a * nchip + ch, sib).wait_recv()
        for g in range(ng):
            for k, peer in enumerate(peers):
                pid = 4 * peer[0] + 2 * peer[1] + peer[2]
                _rdma(g_in[g], g_out[g].at[pid], send_sems, recv_sems, gbase + g * (N_DEV - 1) + k, peer).wait_recv()
        for cp in local:
            cp.wait()

    half = [jax.ShapeDtypeStruct((h.shape[0],) + h.shape[2:], h.dtype) for h in halves]
    out_shape = half + [jax.ShapeDtypeStruct((N_DEV,) + g.shape, g.dtype) for g in gathered]
    outs = _comm_call(body, name, list(halves) + list(gathered), out_shape,
                      (nh * (N_DEV // 2) + ng * (N_DEV - 1),), max(ng, 1))
    return list(outs[:nh]), list(outs[nh:])


def _chip_exchange(arrays, name):
    n = len(arrays)

    def body(*refs):
        ins, outs = refs[:n], refs[n:2 * n]
        send_sems, recv_sems, _ = refs[2 * n:]
        x, y, c, sib, chips = _mesh_place()
        sends = []
        for a in range(n):
            for j, chip in enumerate(chips):
                cp = _rdma(ins[a].at[2 * chip[0] + chip[1]], outs[a].at[j], send_sems, recv_sems, (a, j), (*chip, c))
                cp.start()
                sends.append(cp)
        for cp in sends:
            cp.wait_send()
        for a in range(n):
            for j, chip in enumerate(chips):
                _rdma(ins[a].at[j], outs[a].at[j], send_sems, recv_sems, (a, j), (*chip, c)).wait_recv()

    out_shape = [jax.ShapeDtypeStruct((3,) + g.shape[1:], g.dtype) for g in arrays]
    return list(_comm_call(body, name, arrays, out_shape, (n, 3), 1))


def _gather_copies(src_refs, land_refs, send_sems, recv_sems):
    x, y, c, sib, chips = _mesh_place()
    me = 4 * x + 2 * y + c
    peers = [sib] + [(*chip, c) for chip in chips]
    return [pltpu.make_async_remote_copy(
                src_ref=src, dst_ref=land.at[me], send_sem=send_sems[4 * a + k], recv_sem=recv_sems[4 * a + k],
                device_id=peer, device_id_type=pl.DeviceIdType.MESH)
            for a, (src, land) in enumerate(zip(src_refs, land_refs)) for k, peer in enumerate(peers)]


def _forward_to_sibling(gathered, name):
    n = len(gathered)

    def body(*refs):
        outs = refs[n:2 * n]
        send_sems, recv_sems, _ = refs[2 * n:]
        x, y, c, sib, chips = _mesh_place()
        sends = []
        for a in range(n):
            for j, chip in enumerate(chips):
                rows = outs[a].at[4 * chip[0] + 2 * chip[1] + c]
                cp = _rdma(rows, rows, send_sems, recv_sems, (a, j), sib)
                cp.start()
                sends.append(cp)
        for cp in sends:
            cp.wait_send()
        for a in range(n):
            for j, chip in enumerate(chips):
                rows = outs[a].at[4 * chip[0] + 2 * chip[1] + (1 - c)]
                _rdma(rows, rows, send_sems, recv_sems, (a, j), sib).wait_recv()

    any_spec = pl.BlockSpec(memory_space=pl.ANY)
    return list(pl.pallas_call(
        body, name=name, in_specs=[any_spec] * n, out_specs=[any_spec] * n,
        out_shape=[jax.ShapeDtypeStruct(g.shape, g.dtype) for g in gathered],
        input_output_aliases={i: i for i in range(n)},
        scratch_shapes=[pltpu.SemaphoreType.DMA((n, 3)), pltpu.SemaphoreType.DMA((n, 3)),
                        pltpu.SemaphoreType.DMA((1,))],
        compiler_params=pltpu.CompilerParams(has_side_effects=True),
    )(*gathered))


def _split_start(copies_fn, srcs, land_shapes, n_sems, name, after=None):
    ns, nl = len(srcs), len(land_shapes)
    nc = n_sems
    hbm = pl.BlockSpec(memory_space=pltpu.HBM)
    sem = pl.BlockSpec(memory_space=pltpu.SEMAPHORE)

    n_in = ns + nl + (0 if after is None else 1)

    def body(*refs):
        src_refs, land_refs = refs[:ns], refs[ns:ns + nl]
        send_sems, recv_sems = refs[n_in:n_in + nc], refs[n_in + nc:n_in + 2 * nc]
        token = refs[-1]
        for cp in copies_fn(src_refs, land_refs, send_sems, recv_sems):
            cp.start()
        token[...] = jnp.zeros_like(token)

    lands = [lax.empty(s.shape, s.dtype) for s in land_shapes]
    thru = [pltpu.HBM(t.shape, t.dtype) for t in list(srcs) + lands]
    order = [] if after is None else [after]
    outs = pl.pallas_call(
        body, name=name,
        out_shape=(*([pltpu.SemaphoreType.DMA(())] * (2 * nc)), *thru, jax.ShapeDtypeStruct((8, LANE), F32)),
        in_specs=[hbm] * (ns + nl) + [pl.BlockSpec(memory_space=pl.ANY)] * len(order),
        out_specs=(*([sem] * (2 * nc)), *([hbm] * (ns + nl)), pl.BlockSpec(memory_space=pltpu.VMEM)),
        input_output_aliases={i: 2 * nc + i for i in range(ns + nl)},
        compiler_params=pltpu.CompilerParams(has_side_effects=pltpu.SideEffectType.DATAFLOW_SIDE_EFFECTING),
    )(*[pltpu.with_memory_space_constraint(t, pltpu.HBM) for t in list(srcs) + lands], *order)
    thru_out = outs[2 * nc:2 * nc + ns + nl]
    return list(outs[:nc]), list(outs[nc:2 * nc]), list(thru_out[:ns]), list(thru_out[ns:]), outs[-1]


def _split_wait(copies_fn, send_sems, recv_sems, srcs, lands, after, name):
    ns, nl, nc = len(srcs), len(lands), len(send_sems)
    hbm = pl.BlockSpec(memory_space=pltpu.HBM)
    sem = pl.BlockSpec(memory_space=pltpu.SEMAPHORE)

    def body(*refs):
        src_refs, land_refs = refs[:ns], refs[ns:ns + nl]
        send_refs, recv_refs = refs[ns + nl:ns + nl + nc], refs[ns + nl + nc:ns + nl + 2 * nc]
        for cp in copies_fn(src_refs, land_refs, send_refs, recv_refs):
            cp.wait_send()
            cp.wait_recv()

    outs = pl.pallas_call(
        body, name=name,
        out_shape=[pltpu.HBM(t.shape, t.dtype) for t in list(srcs) + list(lands)],
        in_specs=[hbm] * (ns + nl) + [sem] * (2 * nc) + [pl.BlockSpec(memory_space=pl.ANY)],
        out_specs=[hbm] * (ns + nl),
        input_output_aliases={i: i for i in range(ns + nl)},
        compiler_params=pltpu.CompilerParams(has_side_effects=pltpu.SideEffectType.DATAFLOW_SIDE_EFFECTING),
    )(*srcs, *lands, *send_sems, *recv_sems, after)
    return list(outs[:ns]), list(outs[ns:])


def _chip_copies(src_refs, land_refs, send_sems, recv_sems):
    x, y, c, sib, chips = _mesh_place()
    return [pltpu.make_async_remote_copy(
                src_ref=src.at[2 * chip[0] + chip[1]], dst_ref=land.at[j], send_sem=send_sems[3 * a + j],
                recv_sem=recv_sems[3 * a + j], device_id=(*chip, c), device_id_type=pl.DeviceIdType.MESH)
            for a, (src, land) in enumerate(zip(src_refs, land_refs)) for j, chip in enumerate(chips)]


def _pair_sum(halves, theirs, core, name, tr=256):
    nchip, _, R, C = halves.shape
    tr = _row_tile(R, tr)

    def body(core_ref, h_ref, t_ref, o_ref):
        o_ref[...] = (h_ref[...] + t_ref[...]).astype(BF16)

    grid_spec = pltpu.PrefetchScalarGridSpec(
        num_scalar_prefetch=1, grid=(nchip, R // tr),
        in_specs=[pl.BlockSpec((None, None, tr, C), lambda ch, i, core_ref: (ch, core_ref[0], i, 0)),
                  pl.BlockSpec((None, tr, C), lambda ch, i, core_ref: (ch, i, 0))],
        out_specs=pl.BlockSpec((None, tr, C), lambda ch, i, core_ref: (ch, i, 0)))
    return pl.pallas_call(
        body, name=name, grid_spec=grid_spec, out_shape=jax.ShapeDtypeStruct((nchip, R, C), BF16),
        compiler_params=pltpu.CompilerParams(
            dimension_semantics=("parallel", "parallel"), vmem_limit_bytes=_vmem(3 * _nbytes((tr, C), F32))),
    )(core, halves, theirs)


def _chip_sum(own, recv, chip, name, tr=256):
    _, R, C = own.shape
    tr = _row_tile(R, tr)

    def body(chip_ref, o_ref, r_ref, out_ref):
        acc = o_ref[...].astype(F32)
        for j in range(3):
            acc = acc + r_ref[j].astype(F32)
        out_ref[...] = acc

    grid_spec = pltpu.PrefetchScalarGridSpec(
        num_scalar_prefetch=1, grid=(R // tr,),
        in_specs=[pl.BlockSpec((None, tr, C), lambda i, chip_ref: (chip_ref[0], i, 0)),
                  pl.BlockSpec((3, tr, C), lambda i, chip_ref: (0, i, 0))],
        out_specs=pl.BlockSpec((tr, C), lambda i, chip_ref: (i, 0)))
    return pl.pallas_call(
        body, name=name, grid_spec=grid_spec, out_shape=jax.ShapeDtypeStruct((R, C), F32),
        compiler_params=pltpu.CompilerParams(
            dimension_semantics=("parallel",), vmem_limit_bytes=_vmem(4 * _nbytes((tr, C), F32))),
    )(chip, own, recv)


def _row_tile(rows, target):
    best = rows
    for t in range(8, min(rows, target) + 1, 8):
        if rows % t == 0:
            best = t
    return best


def _sum_slots(recv, name, tr=128):
    n, R, C = recv.shape
    tr = _row_tile(R, tr if recv.dtype == F32 else 2 * tr)

    def body(r_ref, o_ref):
        acc = r_ref[0].astype(F32)
        for k in range(1, n):
            acc = acc + r_ref[k].astype(F32)
        o_ref[...] = acc

    return pl.pallas_call(
        body, name=name, grid=(R // tr,),
        in_specs=[pl.BlockSpec((n, tr, C), lambda i: (0, i, 0))],
        out_specs=pl.BlockSpec((tr, C), lambda i: (i, 0)),
        out_shape=jax.ShapeDtypeStruct((R, C), F32),
        compiler_params=pltpu.CompilerParams(
            dimension_semantics=("parallel",), vmem_limit_bytes=_vmem(10 * _nbytes((tr, C), F32))),
    )(recv)


def _adamw(g, w, m, v, name, tr=256):
    R, C = g.shape
    tr = _row_tile(R, tr)

    def body(g_ref, w_ref, m_ref, v_ref, d_ref, nm_ref, nv_ref):
        gv = g_ref[...]
        nm = ADAM_B1 * m_ref[...] + (1.0 - ADAM_B1) * gv
        nv = ADAM_B2 * v_ref[...] + (1.0 - ADAM_B2) * (gv * gv)
        m_hat = nm / (1.0 - ADAM_B1 ** ADAM_STEP)
        v_hat = nv / (1.0 - ADAM_B2 ** ADAM_STEP)
        d_ref[...] = -ADAM_LR * (m_hat / (jnp.sqrt(v_hat) + ADAM_EPS) + ADAM_WD * w_ref[...])
        nm_ref[...] = nm
        nv_ref[...] = nv

    row = pl.BlockSpec((tr, C), lambda i: (i, 0))
    return pl.pallas_call(
        body, name=name, grid=(R // tr,), in_specs=[row] * 4, out_specs=[row] * 3,
        out_shape=[jax.ShapeDtypeStruct((R, C), F32)] * 3,
        compiler_params=pltpu.CompilerParams(
            dimension_semantics=("parallel",), vmem_limit_bytes=_vmem(8 * _nbytes((tr, C), F32))),
    )(g, w, m, v)


def _pack_rows(parts):
    flat = jnp.concatenate([p.reshape(-1).astype(F32) for p in parts])
    rows = -(-flat.shape[0] // (8 * LANE)) * 8
    return jnp.pad(flat, (0, rows * LANE - flat.shape[0])).reshape(rows, LANE)


def _unpack(flat, shapes):
    out, off = [], 0
    for shp in shapes:
        n = math.prod(shp)
        out.append(flat[off:off + n].reshape(shp))
        off += n
    return out


def kernel(x, a_norm, a_w_in, a_conv, a_w_out, kv_norm, w_kv, b_norm, b_w_q, b_w_o, rel_bias, ffn_norm, ffn_w_up, ffn_conv, ffn_conv_b, ffn_w_down, final_norm, loss_target, m_a_norm, m_a_w_in, m_a_conv, m_a_w_out, m_kv_norm, m_w_kv, m_b_norm, m_b_w_q, m_b_w_o, m_rel_bias, m_ffn_norm, m_ffn_w_up, m_ffn_conv, m_ffn_conv_b, m_ffn_w_down, m_final_norm, v_a_norm, v_a_w_in, v_a_conv, v_a_w_out, v_kv_norm, v_w_kv, v_b_norm, v_b_w_q, v_b_w_o, v_rel_bias, v_ffn_norm, v_ffn_w_up, v_ffn_conv, v_ffn_conv_b, v_ffn_w_down, v_final_norm):
    B, S, D = x.shape
    T = B * S
    F = ffn_w_down.shape[1] * N_DEV
    me = 4 * lax.axis_index("x") + 2 * lax.axis_index("y") + lax.axis_index("c")

    big_shards = [a_w_in[0].T, a_w_out[0], w_kv.T, b_w_q[0], b_w_o[0],
                  ffn_w_up[0].T, ffn_w_up[1].T, ffn_w_down[0], ffn_w_down[1]]
    small_shapes = [a_norm.shape, a_conv.shape, ffn_conv.shape]
    small_pack = _pack_rows([a_norm, a_conv, ffn_conv])
    shards = [s.astype(BF16) for s in big_shards]
    first = _all_gather([shards[0], shards[1], small_pack], "gather_weights_first")
    win_t, wout = first[0].reshape(-1, D), first[1].reshape(-1, D)
    smalls = [_unpack(first[2][j].reshape(-1), small_shapes) for j in range(N_DEV)]
    a_norm_f = jnp.concatenate([s[0] for s in smalls], axis=-1)
    a_conv_f = jnp.concatenate([s[1] for s in smalls], axis=-1)[0]
    ffn_conv_f = jnp.concatenate([s[2] for s in smalls], axis=-1)

    def gather_start(tag, idxs, after):
        srcs = [shards[i] for i in idxs]
        lands = [jax.ShapeDtypeStruct((N_DEV,) + s.shape, s.dtype) for s in srcs]
        return _split_start(_gather_copies, srcs, lands, 4 * len(srcs), f"gather_start_{tag}", after=after)

    def gather_finish(tag, handle, after):
        send, recv, srcs, lands, _ = handle
        srcs, lands = _split_wait(_gather_copies, send, recv, srcs, lands, after, f"gather_wait_{tag}")
        lands = _forward_to_sibling(lands, f"gather_forward_{tag}")
        return [lax.dynamic_update_slice(g, s[None], (me, 0, 0)).reshape(-1, D) for g, s in zip(lands, srcs)]

    ffn0_w = gather_start("ffn0", [5, 7], first[0])
    rest_w = gather_start("rest", [2, 3, 4, 6, 8], ffn0_w[4])

    x2 = x.reshape(T, D)
    (xn0,) = _rmsnorm_fwd(x2, a_norm_f, "a_norm_fwd")
    bch = _matmul(xn0, win_t, mode="nt", out_dtype=BF16, after=rest_w[4], name="a_in_proj").reshape(B, S, 3 * D)
    gated = _shortconv_fwd(bch, a_conv_f, "a_gate_fwd").reshape(T, D)
    h1 = _matmul(gated, wout, mode="nn", out_dtype=F32, add=x2, name="a_out_proj")
    wup0_t, wdn0 = gather_finish("ffn0", ffn0_w, h1)
    wup_t, wdn = [wup0_t, None], [wdn0, None]

    def ffn_fwd(h, l):
        (xn,) = _rmsnorm_fwd(h, ffn_norm[l:l + 1], f"ffn{l}_norm_fwd")
        u0 = _matmul(xn, wup_t[l], mode="nt", out_dtype=BF16, name=f"ffn{l}_up").reshape(B, S, 2 * F)
        act = _ffn_gate_fwd(u0, ffn_conv_f[l], ffn_conv_b[l:l + 1], f"ffn{l}_gate_fwd").reshape(T, F)
        out = _matmul(act, wdn[l], mode="nn", out_dtype=F32, add=h, name=f"ffn{l}_down", tk=2816)
        return out, (xn, u0, act)

    h2, ffn0_saved = ffn_fwd(h1, 0)
    wkv_t, wq, wo, wup_t[1], wdn[1] = gather_finish("rest", rest_w, h2)
    kvn, xnb = _rmsnorm_fwd(h2, jnp.stack([kv_norm, b_norm[0]]), "kv_b_norm_fwd")
    kv = _matmul(kvn, wkv_t, mode="nt", out_dtype=F32, name="kv_proj").reshape(B, S, 2 * D)
    q = _matmul(xnb, wq, mode="nn", out_dtype=F32, name="q_proj").reshape(B, S, D)

    tables = [_band_tables(w, d) for (w, d) in DILATED_BRANCHES]
    onehots = jnp.stack([t[0] for t in tables])
    P = ATT_BLOCK
    bias_vals = _bias_lookup(rel_bias.T, onehots, "rel_bias_lookup").reshape(3, N_HEADS, P, 2 * P)
    in_cur = (jnp.arange(2 * P) >= P)[None, :]
    tabs = []
    for bi in range(3):
        band = tables[bi][1]
        gen = jnp.where(band[None], bias_vals[bi], NEG)
        fst = jnp.where((band & in_cur)[None], bias_vals[bi], NEG)
        tabs.append(jnp.stack([fst, gen]))
    tabs = jnp.stack(tabs)

    attn3, lse3 = _attn_fwd(q, kv, tabs, "attn_fwd")
    attn = attn3.reshape(T, D)
    h3 = _matmul(attn, wo, mode="nn", out_dtype=F32, add=h2, name="o_proj")
    h4, ffn1_saved = ffn_fwd(h3, 1)

    dh4, dh4b, d_final_norm, sq = _final_loss_bwd(h4, final_norm.reshape(1, D), loss_target.reshape(T, D), "loss_bwd")
    loss_part = 0.5 * jnp.sum(sq) / D

    core_idx = lax.axis_index("c").astype(jnp.int32).reshape(1)
    chip_idx = (2 * lax.axis_index("x") + lax.axis_index("y")).astype(jnp.int32).reshape(1)
    in_flight = []

    def send_group(tag, indexed_grads, small=()):
        halves = [g.reshape(N_DEV // 2, 2, g.shape[0] // N_DEV, D) for _, g in indexed_grads]
        theirs, gath = _core_pair_exchange(halves, list(small), f"exchange_core_{tag}")
        sums = [_pair_sum(h_, t_, core_idx, f"grad_core_sum_{tag}{k}")
                for k, (h_, t_) in enumerate(zip(halves, theirs))]
        lands = [jax.ShapeDtypeStruct((3,) + s.shape[1:], s.dtype) for s in sums]
        send, recv, srcs, lnd, token = _split_start(_chip_copies, sums, lands, 3 * len(sums),
                                                    f"exchange_chip_start_{tag}")
        in_flight.append((tag, [i for i, _ in indexed_grads], send, recv, srcs, lnd))
        return token, gath

    def ffn_bwd(dh_out, dh_out_b, h_in, saved, l, after):
        xn, u0, act = saved
        dact = _matmul(dh_out_b, wdn[l], mode="nt", out_dtype=BF16, after=after, name=f"ffn{l}_down_dx",
                       tn=1408).reshape(B, S, F)
        d_wdn = _matmul(act, dh_out_b, mode="tn", out_dtype=F32, name=f"ffn{l}_down_dw", tm=1408, tn=1024, tk=2048)
        du0, d_conv, d_conv_b = _ffn_gate_bwd(u0, dact, ffn_conv_f[l], ffn_conv_b[l:l + 1], f"ffn{l}_gate_bwd")
        du0 = du0.reshape(2, T, F)
        dxn = _matmul(du0, wup_t[l], mode="nn", out_dtype=F32, a_parts=2, name=f"ffn{l}_up_dx", tk=2816)
        d_wup_t = _matmul(du0, xn, mode="tn", out_dtype=F32, a_parts=2, name=f"ffn{l}_up_dw", tm=1408, tn=1024, tk=2048)
        dh_in, dh_in_b, d_norm = _rmsnorm_bwd(h_in, ffn_norm[l:l + 1], [dxn], dh_out, f"ffn{l}_norm_bwd")
        return (dh_in, dh_in_b), (d_wdn, d_wup_t, d_conv, d_conv_b, d_norm)

    (dh3, dh3b), ffn1_grads = ffn_bwd(dh4, dh4b, h3, ffn1_saved, 1, None)
    token, _ = send_group("ffn1", [(6, ffn1_grads[1]), (8, ffn1_grads[0])])

    dattn = _matmul(dh3b, wo, mode="nt", out_dtype=F32, after=token, name="o_proj_dx").reshape(B, S, D)
    d_wo = _matmul(attn, dh3b, mode="tn", out_dtype=F32, name="o_proj_dw", tm=1024, tn=1024, tk=2048)
    dq, dkv, dtab = _attn_bwd(q, kv, dattn, attn3, lse3, tabs, "attn_bwd")
    dtabs = dtab.transpose(1, 0, 2, 3, 4).reshape(3, N_HEADS, P * 2 * P)
    d_rel_bias = _bias_grad(dtabs, onehots, "rel_bias_grad").T
    dq2 = dq.reshape(T, D)
    dkv2 = dkv.reshape(2, T, D)
    dxnb = _matmul(dq2, wq, mode="nt", out_dtype=F32, name="q_proj_dx")
    d_wq = _matmul(xnb, dq2, mode="tn", out_dtype=F32, name="q_proj_dw", tm=1024, tn=1024, tk=2048)
    dkvn = _matmul(dkv2, wkv_t, mode="nn", out_dtype=F32, a_parts=2, name="kv_proj_dx")
    d_wkv_t = _matmul(dkv2, kvn, mode="tn", out_dtype=F32, a_parts=2, name="kv_proj_dw", tm=1024, tn=1024, tk=2048)
    dh2, dh2b, d_kvb_norm = _rmsnorm_bwd(h2, jnp.stack([kv_norm, b_norm[0]]), [dkvn, dxnb], dh3, "kv_b_norm_bwd")

    token, _ = send_group("attn", [(2, d_wkv_t), (3, d_wq), (4, d_wo)])

    (dh1, dh1b), ffn0_grads = ffn_bwd(dh2, dh2b, h1, ffn0_saved, 0, token)
    token, _ = send_group("ffn0", [(5, ffn0_grads[1]), (7, ffn0_grads[0])])

    dgated = _matmul(dh1b, wout, mode="nt", out_dtype=BF16, after=token, name="a_out_proj_dx").reshape(B, S, D)
    d_wout = _matmul(gated, dh1b, mode="tn", out_dtype=F32, name="a_out_proj_dw", tm=1024, tn=1024, tk=2048)
    dbch, d_a_conv = _shortconv_bwd(bch, dgated, a_conv_f, "a_gate_bwd")
    dbch = dbch.reshape(3, T, D)
    dxn0 = _matmul(dbch, win_t, mode="nn", out_dtype=F32, a_parts=3, name="a_in_proj_dx")
    d_win_t = _matmul(dbch, xn0, mode="tn", out_dtype=F32, a_parts=3, name="a_in_proj_dw", tm=1024, tn=1024, tk=2048)
    grad_x, _, d_a_norm = _rmsnorm_bwd(x2, a_norm_f, [dxn0], dh1, "a_norm_bwd")

    small_full = [d_a_norm, d_a_conv, jnp.stack([ffn0_grads[2], ffn1_grads[2]]),
                  d_kvb_norm[0], d_kvb_norm[1], d_rel_bias, jnp.concatenate([ffn0_grads[4], ffn1_grads[4]]),
                  jnp.concatenate([ffn0_grads[3], ffn1_grads[3]]), d_final_norm, loss_part]
    small_full_shapes = [(1, D), (3, D), (2, 3, 2 * F), (D,), (1, D), rel_bias.shape, (2, D), (2, 2 * F), (D,), ()]
    token, gath = send_group("a", [(0, d_win_t), (1, d_wout)], small=[_pack_rows(small_full)])
    chip_sums, recv = {}, {}
    for tag, idxs, send, rcv, srcs, lnd in in_flight:
        srcs, lnd = _split_wait(_chip_copies, send, rcv, srcs, lnd, token, f"exchange_chip_wait_{tag}")
        for i, s_, l_ in zip(idxs, srcs, lnd):
            chip_sums[i], recv[i] = s_, l_
    small_sum = _sum_slots(gath[0], "small_grad_sum").reshape(-1)
    (g_a_norm_f, g_a_conv_f, g_ffn_conv_f, g_kv_norm, g_b_norm, g_rel_bias, g_ffn_norm, g_ffn_conv_b,
     g_final_norm, loss) = _unpack(small_sum, small_full_shapes)

    def my_cols(full, width):
        return lax.dynamic_slice_in_dim(full, me * width, width, axis=full.ndim - 1)

    g_a_norm = my_cols(g_a_norm_f, D // N_DEV)
    g_a_conv = my_cols(g_a_conv_f, D // N_DEV)[None]
    g_ffn_conv = my_cols(g_ffn_conv_f, 2 * F // N_DEV)

    big_w = [(a_w_in, m_a_w_in, v_a_w_in, True), (a_w_out, m_a_w_out, v_a_w_out, False),
             (w_kv, m_w_kv, v_w_kv, True), (b_w_q, m_b_w_q, v_b_w_q, False), (b_w_o, m_b_w_o, v_b_w_o, False),
             (ffn_w_up[0], m_ffn_w_up[0], v_ffn_w_up[0], True), (ffn_w_up[1], m_ffn_w_up[1], v_ffn_w_up[1], True),
             (ffn_w_down[0], m_ffn_w_down[0], v_ffn_w_down[0], False),
             (ffn_w_down[1], m_ffn_w_down[1], v_ffn_w_down[1], False)]
    big_out = []
    for i, (w, m, v, transposed) in enumerate(big_w):
        g = _chip_sum(chip_sums[i], recv[i], chip_idx, f"grad_sum_{i}")
        if transposed:
            g = g.T
        w2, m2, v2 = (t.reshape(g.shape) for t in (w, m, v))
        delta, nm, nv = _adamw(g, w2, m2, v2, f"adamw_{i}")
        big_out.append(tuple(t.reshape(w.shape) for t in (g, delta, nm, nv)))

    def pair(i, j):
        return tuple(jnp.stack([big_out[i][t], big_out[j][t]]) for t in range(4))

    small_w = [(a_norm, m_a_norm, v_a_norm, g_a_norm), (a_conv, m_a_conv, v_a_conv, g_a_conv),
               (ffn_conv, m_ffn_conv, v_ffn_conv, g_ffn_conv), (kv_norm, m_kv_norm, v_kv_norm, g_kv_norm),
               (b_norm, m_b_norm, v_b_norm, g_b_norm), (rel_bias, m_rel_bias, v_rel_bias, g_rel_bias),
               (ffn_norm, m_ffn_norm, v_ffn_norm, g_ffn_norm),
               (ffn_conv_b, m_ffn_conv_b, v_ffn_conv_b, g_ffn_conv_b),
               (final_norm, m_final_norm, v_final_norm, g_final_norm)]
    shapes = [t[0].shape for t in small_w]
    packed = [_pack_rows([t[i].reshape(t[0].shape) for t in small_w]) for i in (3, 0, 1, 2)]
    s_delta, s_nm, s_nv = _adamw(*packed, "adamw_small")
    s_g = [t[3].reshape(t[0].shape) for t in small_w]
    s_d, s_m, s_v = (_unpack(t.reshape(-1), shapes) for t in (s_delta, s_nm, s_nv))
    small_out = [(s_g[i], s_d[i], s_m[i], s_v[i]) for i in range(len(small_w))]

    per_weight = [small_out[0], big_out[0], small_out[1], big_out[1], small_out[3], big_out[2], small_out[4],
                  big_out[3], big_out[4], small_out[5], small_out[6], pair(5, 6), small_out[2], small_out[7],
                  pair(7, 8), small_out[8]]
    outs = [loss, grad_x.reshape(B, S, D)]
    for t in range(4):
        outs.extend(pw[t] for pw in per_weight)
    return tuple(outs)
```

```python
import functools
import math

import jax
import jax.numpy as jnp
from jax import lax
from jax.experimental import pallas as pl
from jax.experimental.pallas import tpu as pltpu

F32 = jnp.float32
BF16 = jnp.bfloat16

N_DEV = 8
N_HEADS = 16
ATT_BLOCK = 128
ATT_UNROLL = 8
DILATED_BRANCHES = ((128, 1), (512, 4), (2048, 16))
REL_BUCKETS = 32
REL_MAX_DISTANCE = 2048
RMS_EPS = 1e-6
ADAM_LR = 0.001
ADAM_B1 = 0.9
ADAM_B2 = 0.999
ADAM_EPS = 1e-08
ADAM_WD = 0.01
ADAM_STEP = 10

LANE = 128
HALO = 16
NEG = -1e30
VMEM_CAP = 56 << 20


def _vmem(block_bytes):
    return int(min(VMEM_CAP, max(32 << 20, 3 * block_bytes + (8 << 20))))


def _nbytes(shape, dtype):
    return math.prod(shape) * jnp.dtype(dtype).itemsize


def _tile(dim, target):
    best = None
    t = LANE
    while t <= min(dim, target):
        if dim % t == 0:
            best = t
        t += LANE
    return best if best is not None else dim


def _matmul(a, b, *, mode, out_dtype, name, add=None, after=None, a_parts=1, tm=1024, tn=1024, tk=1024):
    P = a_parts
    if mode == "nn":
        M, K = (a.shape[0], a.shape[1]) if P == 1 else (a.shape[1], a.shape[2] * P)
        N = b.shape[1]
    elif mode == "nt":
        assert P == 1
        M, K = a.shape
        N = b.shape[0]
    else:
        K = a.shape[0] if P == 1 else a.shape[1]
        M = a.shape[1] if P == 1 else a.shape[2] * P
        N = b.shape[1]
    tm = _tile(M // P if mode == "tn" else M, tm)
    tn = _tile(N, tn)
    tk = _tile(K // P if mode == "nn" else K, tk)
    nm, nn_, nk = M // tm, N // tn, K // tk

    if mode == "nn":
        if P == 1:
            a_spec = pl.BlockSpec((tm, tk), lambda i, j, k: (i, k))
        else:
            nkp = nk // P
            a_spec = pl.BlockSpec((None, tm, tk), lambda i, j, k: (k // nkp, i, k % nkp))
        b_spec = pl.BlockSpec((tk, tn), lambda i, j, k: (k, j))
        dims = (((1,), (0,)), ((), ()))
    elif mode == "nt":
        a_spec = pl.BlockSpec((tm, tk), lambda i, j, k: (i, k))
        b_spec = pl.BlockSpec((tn, tk), lambda i, j, k: (j, k))
        dims = (((1,), (1,)), ((), ()))
    else:
        if P == 1:
            a_spec = pl.BlockSpec((tk, tm), lambda i, j, k: (k, i))
        else:
            nmp = nm // P
            a_spec = pl.BlockSpec((None, tk, tm), lambda i, j, k: (i // nmp, k, i % nmp))
        b_spec = pl.BlockSpec((tk, tn), lambda i, j, k: (k, j))
        dims = (((0,), (0,)), ((), ()))
    o_spec = pl.BlockSpec((tm, tn), lambda i, j, k: (i, j))
    in_specs = [a_spec, b_spec]
    args = [a, b]
    if add is not None:
        in_specs.append(o_spec)
        args.append(add)
    has_add = add is not None
    if after is not None:
        in_specs.append(pl.BlockSpec(memory_space=pl.ANY))
        args.append(after)
    n_extra = len(args) - 2

    direct = nk > 1 and out_dtype == F32 and not has_add

    def body(a_ref, b_ref, *rest):
        add_ref = rest[0] if has_add else None
        o_ref = rest[n_extra]
        acc_ref = o_ref if direct or nk == 1 else rest[-1]
        k = pl.program_id(2)

        def part():
            return lax.dot_general(a_ref[...].astype(BF16), b_ref[...].astype(BF16), dims,
                                   preferred_element_type=F32)

        def finish(r):
            if has_add:
                r = r + add_ref[...].astype(F32)
            o_ref[...] = r.astype(out_dtype)

        if nk == 1:
            finish(part())
        else:
            @pl.when(k == 0)
            def _():
                acc_ref[...] = part()

            @pl.when(jnp.logical_and(k > 0, jnp.logical_or(k < nk - 1, direct)))
            def _():
                acc_ref[...] += part()

            if not direct:
                @pl.when(k == nk - 1)
                def _():
                    finish(acc_ref[...] + part())

    blk = (_nbytes((tm, tk), a.dtype) + _nbytes((tk, tn), b.dtype) + _nbytes((tm, tn), out_dtype)
           + (_nbytes((tm, tn), add.dtype) if has_add else 0)) * 2 + 3 * _nbytes((tm, tn), F32)
    return pl.pallas_call(
        body, name=name, grid=(nm, nn_, nk),
        in_specs=in_specs, out_specs=o_spec,
        out_shape=jax.ShapeDtypeStruct((M, N), out_dtype),
        scratch_shapes=[] if direct or nk == 1 else [pltpu.VMEM((tm, tn), F32)],
        compiler_params=pltpu.CompilerParams(
            dimension_semantics=("parallel", "parallel", "arbitrary"), vmem_limit_bytes=_vmem(blk)),
    )(*args)


def _rmsnorm_fwd(x, gains, name, tm=512):
    T, D = x.shape
    n = gains.shape[0]
    tm = _tile(T, tm)

    def body(x_ref, g_ref, *o_refs):
        xv = x_ref[...]
        xhat = xv * lax.rsqrt(jnp.mean(xv * xv, axis=-1, keepdims=True) + RMS_EPS)
        for i in range(n):
            o_refs[i][...] = (xhat * g_ref[i:i + 1, :]).astype(BF16)

    row = pl.BlockSpec((tm, D), lambda i: (i, 0))
    outs = pl.pallas_call(
        body, name=name, grid=(T // tm,),
        in_specs=[row, pl.BlockSpec((n, D), lambda i: (0, 0))],
        out_specs=[row] * n,
        out_shape=[jax.ShapeDtypeStruct((T, D), BF16)] * n,
        compiler_params=pltpu.CompilerParams(
            dimension_semantics=("parallel",), vmem_limit_bytes=_vmem(4 * _nbytes((tm, D), F32))),
    )(x, gains)
    return tuple(outs)


def _rmsnorm_bwd(x, gains, dxns, dres, name, tm=512):
    T, D = x.shape
    n = gains.shape[0]
    tm = _tile(T, tm)

    def body(x_ref, g_ref, *rest):
        dxn_refs = rest[:n]
        dres_ref, dx_ref, dxb_ref, dg_ref = rest[n:]
        xv = x_ref[...]
        rstd = lax.rsqrt(jnp.mean(xv * xv, axis=-1, keepdims=True) + RMS_EPS)
        xhat = xv * rstd
        dx = dres_ref[...]

        @pl.when(pl.program_id(0) == 0)
        def _():
            dg_ref[...] = jnp.zeros_like(dg_ref)

        for i in range(n):
            dy = dxn_refs[i][...].astype(F32)
            dg_ref[i:i + 1, :] += jnp.sum(dy * xhat, axis=0, keepdims=True)
            dxh = dy * g_ref[i:i + 1, :]
            dx = dx + rstd * (dxh - xhat * jnp.mean(dxh * xhat, axis=-1, keepdims=True))
        dx_ref[...] = dx
        dxb_ref[...] = dx.astype(BF16)

    row = pl.BlockSpec((tm, D), lambda i: (i, 0))
    par = pl.BlockSpec((n, D), lambda i: (0, 0))
    return pl.pallas_call(
        body, name=name, grid=(T // tm,),
        in_specs=[row, par] + [row] * n + [row],
        out_specs=[row, row, par],
        out_shape=[jax.ShapeDtypeStruct((T, D), F32), jax.ShapeDtypeStruct((T, D), BF16),
                   jax.ShapeDtypeStruct((n, D), F32)],
        compiler_params=pltpu.CompilerParams(
            dimension_semantics=("arbitrary",), vmem_limit_bytes=_vmem((4 + n) * _nbytes((tm, D), F32))),
    )(x, gains, *dxns, dres)


def _final_loss_bwd(h, gain, target, name, tm=512):
    T, D = h.shape
    tm = _tile(T, tm)

    def body(h_ref, g_ref, t_ref, dh_ref, dhb_ref, dg_ref, sq_ref):
        xv = h_ref[...]
        rstd = lax.rsqrt(jnp.mean(xv * xv, axis=-1, keepdims=True) + RMS_EPS)
        xhat = xv * rstd
        err = xhat * g_ref[...] - t_ref[...]

        @pl.when(pl.program_id(0) == 0)
        def _():
            dg_ref[...] = jnp.zeros_like(dg_ref)
            sq_ref[...] = jnp.zeros_like(sq_ref)

        sq_ref[...] += jnp.sum(err * err, axis=0, keepdims=True)
        dy = err * (1.0 / D)
        dg_ref[...] += jnp.sum(dy * xhat, axis=0, keepdims=True)
        dxh = dy * g_ref[...]
        dh = rstd * (dxh - xhat * jnp.mean(dxh * xhat, axis=-1, keepdims=True))
        dh_ref[...] = dh
        dhb_ref[...] = dh.astype(BF16)

    row = pl.BlockSpec((tm, D), lambda i: (i, 0))
    par = pl.BlockSpec((1, D), lambda i: (0, 0))
    return pl.pallas_call(
        body, name=name, grid=(T // tm,),
        in_specs=[row, par, row], out_specs=[row, row, par, par],
        out_shape=[jax.ShapeDtypeStruct((T, D), F32), jax.ShapeDtypeStruct((T, D), BF16),
                   jax.ShapeDtypeStruct((1, D), F32), jax.ShapeDtypeStruct((1, D), F32)],
        compiler_params=pltpu.CompilerParams(
            dimension_semantics=("arbitrary",), vmem_limit_bytes=_vmem(5 * _nbytes((tm, D), F32))),
    )(h, gain, target)


def _halo_specs(S, ts, tc, col_off, prev, nxt):
    r = ts // HALO
    last = S // HALO - 1
    specs = []
    if prev:
        specs.append(pl.BlockSpec((None, HALO, tc), lambda j, b, s: (b, jnp.maximum(s * r - 1, 0), col_off + j)))
    specs.append(pl.BlockSpec((None, ts, tc), lambda j, b, s: (b, s, col_off + j)))
    if nxt:
        specs.append(pl.BlockSpec((None, HALO, tc), lambda j, b, s: (b, jnp.minimum((s + 1) * r, last), col_off + j)))
    return specs


def _ext(prev_ref, main_ref, next_ref, first, last):
    main = main_ref[...].astype(F32)
    zeros = jnp.zeros((HALO, main.shape[1]), F32)
    top = zeros if prev_ref is None else jnp.where(first, 0.0, prev_ref[...].astype(F32))
    bot = zeros if next_ref is None else jnp.where(last, 0.0, next_ref[...].astype(F32))
    return jnp.concatenate([top, main, bot], axis=0)


def _shift(xe, k):
    return pltpu.roll(xe, k % xe.shape[0], axis=0)


def _cconv(xe, w):
    return w[0:1, :] * _shift(xe, 2) + w[1:2, :] * _shift(xe, 1) + w[2:3, :] * xe


def _main(xe, ts):
    return xe[HALO:HALO + ts, :]


def _cconv_t(de, w):
    return w[2:3, :] * de + w[1:2, :] * _shift(de, -1) + w[0:1, :] * _shift(de, -2)


def _conv_wgrad(dw_ref, de, xe, ts):
    d = _main(de, ts)
    dw_ref[0:1, :] += jnp.sum(d * _main(_shift(xe, 2), ts), axis=0, keepdims=True)
    dw_ref[1:2, :] += jnp.sum(d * _main(_shift(xe, 1), ts), axis=0, keepdims=True)
    dw_ref[2:3, :] += jnp.sum(d * _main(xe, ts), axis=0, keepdims=True)


def _gate_tiles(S, C):
    return _tile(S, 1024), _tile(C, 512)


def _shortconv_fwd(bch, conv_w, name):
    B, S, D3 = bch.shape
    D = D3 // 3
    ts, tc = _gate_tiles(S, D)
    nj, ns = D // tc, S // ts

    def body(b_ref, cp_ref, c_ref, hp_ref, h_ref, w_ref, o_ref):
        first = pl.program_id(2) == 0
        ce = _ext(cp_ref, c_ref, None, first, False)
        he = _ext(hp_ref, h_ref, None, first, False)
        cv = _main(_cconv(ce * he, w_ref[...]), ts)
        o_ref[...] = (b_ref[...].astype(F32) * cv).astype(BF16)

    in_specs = (_halo_specs(S, ts, tc, 0, False, False) + _halo_specs(S, ts, tc, nj, True, False)
                + _halo_specs(S, ts, tc, 2 * nj, True, False) + [pl.BlockSpec((3, tc), lambda j, b, s: (0, j))])
    return pl.pallas_call(
        body, name=name, grid=(nj, B, ns), in_specs=in_specs,
        out_specs=pl.BlockSpec((None, ts, tc), lambda j, b, s: (b, s, j)),
        out_shape=jax.ShapeDtypeStruct((B, S, D), BF16),
        compiler_params=pltpu.CompilerParams(
            dimension_semantics=("parallel", "parallel", "parallel"),
            vmem_limit_bytes=_vmem(12 * _nbytes((ts + 2 * HALO, tc), F32))),
    )(bch, bch, bch, bch, bch, conv_w)


def _shortconv_bwd(bch, dg, conv_w, name):
    B, S, D3 = bch.shape
    D = D3 // 3
    ts, tc = _gate_tiles(S, D)
    nj, ns = D // tc, S // ts

    def body(b_ref, bn_ref, cp_ref, c_ref, cn_ref, hp_ref, h_ref, hn_ref, d_ref, dn_ref, w_ref, o_ref, dw_ref):
        s = pl.program_id(2)
        first, last = s == 0, s == ns - 1

        @pl.when(jnp.logical_and(pl.program_id(1) == 0, s == 0))
        def _():
            dw_ref[...] = jnp.zeros_like(dw_ref)

        w = w_ref[...]
        be = _ext(None, b_ref, bn_ref, first, last)
        ce = _ext(cp_ref, c_ref, cn_ref, first, last)
        he = _ext(hp_ref, h_ref, hn_ref, first, last)
        de = _ext(None, d_ref, dn_ref, first, last)
        ch = ce * he
        dcv = de * be
        dch = _main(_cconv_t(dcv, w), ts)
        o_ref[0] = (_main(de, ts) * _main(_cconv(ch, w), ts)).astype(BF16)
        o_ref[1] = (dch * _main(he, ts)).astype(BF16)
        o_ref[2] = (dch * _main(ce, ts)).astype(BF16)
        _conv_wgrad(dw_ref, dcv, ch, ts)

    in_specs = (_halo_specs(S, ts, tc, 0, False, True) + _halo_specs(S, ts, tc, nj, True, True)
                + _halo_specs(S, ts, tc, 2 * nj, True, True) + _halo_specs(S, ts, tc, 0, False, True)
                + [pl.BlockSpec((3, tc), lambda j, b, s: (0, j))])
    return pl.pallas_call(
        body, name=name, grid=(nj, B, ns), in_specs=in_specs,
        out_specs=[pl.BlockSpec((3, None, ts, tc), lambda j, b, s: (0, b, s, j)),
                   pl.BlockSpec((3, tc), lambda j, b, s: (0, j))],
        out_shape=[jax.ShapeDtypeStruct((3, B, S, D), BF16), jax.ShapeDtypeStruct((3, D), F32)],
        compiler_params=pltpu.CompilerParams(
            dimension_semantics=("parallel", "arbitrary", "arbitrary"),
            vmem_limit_bytes=_vmem(24 * _nbytes((ts + 2 * HALO, tc), F32))),
    )(bch, bch, bch, bch, bch, bch, bch, bch, dg, dg, conv_w)


def _sigmoid(x):
    return 1.0 / (1.0 + jnp.exp(-x))


def _ffn_gate_fwd(u0, conv_w, conv_b, name):
    B, S, F2 = u0.shape
    F = F2 // 2
    ts, tc = _gate_tiles(S, F)
    nj, ns = F // tc, S // ts

    def body(gp_ref, g_ref, up_ref, u_ref, wg_ref, wu_ref, bg_ref, bu_ref, o_ref):
        first = pl.program_id(2) == 0
        ug = _main(_cconv(_ext(gp_ref, g_ref, None, first, False), wg_ref[...]), ts) + bg_ref[...]
        uu = _main(_cconv(_ext(up_ref, u_ref, None, first, False), wu_ref[...]), ts) + bu_ref[...]
        o_ref[...] = (ug * _sigmoid(ug) * uu).astype(BF16)

    in_specs = (_halo_specs(S, ts, tc, 0, True, False) + _halo_specs(S, ts, tc, nj, True, False)
                + [pl.BlockSpec((3, tc), lambda j, b, s: (0, j)), pl.BlockSpec((3, tc), lambda j, b, s: (0, nj + j)),
                   pl.BlockSpec((1, tc), lambda j, b, s: (0, j)), pl.BlockSpec((1, tc), lambda j, b, s: (0, nj + j))])
    return pl.pallas_call(
        body, name=name, grid=(nj, B, ns), in_specs=in_specs,
        out_specs=pl.BlockSpec((None, ts, tc), lambda j, b, s: (b, s, j)),
        out_shape=jax.ShapeDtypeStruct((B, S, F), BF16),
        compiler_params=pltpu.CompilerParams(
            dimension_semantics=("parallel", "parallel", "parallel"),
            vmem_limit_bytes=_vmem(12 * _nbytes((ts + 2 * HALO, tc), F32))),
    )(u0, u0, u0, u0, conv_w, conv_w, conv_b, conv_b)


def _ffn_gate_bwd(u0, dact, conv_w, conv_b, name):
    B, S, F2 = u0.shape
    F = F2 // 2
    ts, tc = _gate_tiles(S, F)
    nj, ns = F // tc, S // ts

    def body(gp_ref, g_ref, gn_ref, up_ref, u_ref, un_ref, d_ref, dn_ref, wg_ref, wu_ref, bg_ref, bu_ref,
             o_ref, dwg_ref, dwu_ref, dbg_ref, dbu_ref):
        s = pl.program_id(2)
        first, last = s == 0, s == ns - 1

        @pl.when(jnp.logical_and(pl.program_id(1) == 0, s == 0))
        def _():
            for r in (dwg_ref, dwu_ref, dbg_ref, dbu_ref):
                r[...] = jnp.zeros_like(r)

        wg, wu = wg_ref[...], wu_ref[...]
        ge = _ext(gp_ref, g_ref, gn_ref, first, last)
        ue = _ext(up_ref, u_ref, un_ref, first, last)
        de = _ext(None, d_ref, dn_ref, first, last)
        ug = _cconv(ge, wg) + bg_ref[...]
        uu = _cconv(ue, wu) + bu_ref[...]
        sig = _sigmoid(ug)
        dug = de * uu * (sig * (1.0 + ug * (1.0 - sig)))
        duu = de * (ug * sig)
        o_ref[0] = _main(_cconv_t(dug, wg), ts).astype(BF16)
        o_ref[1] = _main(_cconv_t(duu, wu), ts).astype(BF16)
        _conv_wgrad(dwg_ref, dug, ge, ts)
        _conv_wgrad(dwu_ref, duu, ue, ts)
        dbg_ref[...] += jnp.sum(_main(dug, ts), axis=0, keepdims=True)
        dbu_ref[...] += jnp.sum(_main(duu, ts), axis=0, keepdims=True)

    w3 = lambda off: pl.BlockSpec((3, tc), lambda j, b, s: (0, off + j))
    w1 = lambda off: pl.BlockSpec((1, tc), lambda j, b, s: (0, off + j))
    in_specs = (_halo_specs(S, ts, tc, 0, True, True) + _halo_specs(S, ts, tc, nj, True, True)
                + _halo_specs(S, ts, tc, 0, False, True) + [w3(0), w3(nj), w1(0), w1(nj)])
    outs = pl.pallas_call(
        body, name=name, grid=(nj, B, ns), in_specs=in_specs,
        out_specs=[pl.BlockSpec((2, None, ts, tc), lambda j, b, s: (0, b, s, j)), w3(0), w3(0), w1(0), w1(0)],
        out_shape=[jax.ShapeDtypeStruct((2, B, S, F), BF16), jax.ShapeDtypeStruct((3, F), F32),
                   jax.ShapeDtypeStruct((3, F), F32), jax.ShapeDtypeStruct((1, F), F32),
                   jax.ShapeDtypeStruct((1, F), F32)],
        compiler_params=pltpu.CompilerParams(
            dimension_semantics=("parallel", "arbitrary", "arbitrary"),
            vmem_limit_bytes=_vmem(30 * _nbytes((ts + 2 * HALO, tc), F32))),
    )(u0, u0, u0, u0, u0, u0, dact, dact, conv_w, conv_w, conv_b, conv_b)
    du0, dwg, dwu, dbg, dbu = outs
    return du0, jnp.concatenate([dwg, dwu], axis=1), jnp.concatenate([dbg, dbu], axis=1)


def _t5_bucket(dist):
    max_exact = REL_BUCKETS // 2
    n = jnp.maximum(dist, 0)
    nf = jnp.maximum(n, max_exact).astype(F32)
    large = max_exact + (jnp.log(nf / max_exact) / math.log(REL_MAX_DISTANCE / max_exact)
                         * (REL_BUCKETS - max_exact)).astype(jnp.int32)
    large = jnp.minimum(large, REL_BUCKETS - 1)
    return jnp.where(n < max_exact, n, large)


def _band_tables(window, dilation):
    P = ATT_BLOCK
    qi = jnp.arange(P, dtype=jnp.int32)[:, None]
    kc = jnp.arange(2 * P, dtype=jnp.int32)[None, :]
    delta = qi + P - kc
    band = (delta >= 0) & (delta <= window // dilation)
    bucket = _t5_bucket(delta * dilation).reshape(-1)
    onehot = (bucket[None, :] == jnp.arange(REL_BUCKETS, dtype=jnp.int32)[:, None]).astype(F32)
    return onehot, band


def _bias_lookup(rel_bias_t, onehots, name):
    nb, _, Q = onehots.shape
    H = rel_bias_t.shape[0]

    def body(r_ref, oh_ref, o_ref):
        o_ref[...] = lax.dot_general(r_ref[...], oh_ref[...], (((1,), (0,)), ((), ())),
                                     precision=lax.Precision.HIGHEST, preferred_element_type=F32)

    return pl.pallas_call(
        body, name=name, grid=(nb,),
        in_specs=[pl.BlockSpec((H, REL_BUCKETS), lambda i: (0, 0)),
                  pl.BlockSpec((None, REL_BUCKETS, Q), lambda i: (i, 0, 0))],
        out_specs=pl.BlockSpec((None, H, Q), lambda i: (i, 0, 0)),
        out_shape=jax.ShapeDtypeStruct((nb, H, Q), F32),
        compiler_params=pltpu.CompilerParams(dimension_semantics=("parallel",),
                                             vmem_limit_bytes=_vmem(_nbytes((REL_BUCKETS + H, Q), F32))),
    )(rel_bias_t, onehots)


def _bias_grad(dtabs, onehots, name):
    nb, H, Q = dtabs.shape

    def body(d_ref, oh_ref, o_ref):
        @pl.when(pl.program_id(0) == 0)
        def _():
            o_ref[...] = jnp.zeros_like(o_ref)

        o_ref[...] += lax.dot_general(d_ref[...], oh_ref[...], (((1,), (1,)), ((), ())),
                                      precision=lax.Precision.HIGHEST, preferred_element_type=F32)

    return pl.pallas_call(
        body, name=name, grid=(nb,),
        in_specs=[pl.BlockSpec((None, H, Q), lambda i: (i, 0, 0)),
                  pl.BlockSpec((None, REL_BUCKETS, Q), lambda i: (i, 0, 0))],
        out_specs=pl.BlockSpec((H, REL_BUCKETS), lambda i: (0, 0)),
        out_shape=jax.ShapeDtypeStruct((H, REL_BUCKETS), F32),
        compiler_params=pltpu.CompilerParams(dimension_semantics=("arbitrary",),
                                             vmem_limit_bytes=_vmem(_nbytes((REL_BUCKETS + H, Q), F32))),
    )(dtabs, onehots)


def _lane_masks():
    lane = lax.broadcasted_iota(jnp.int32, (1, LANE), 1)
    lo = lane < LANE // 2
    return lo, jnp.logical_not(lo)


def _dot_nt(a, b):
    return lax.dot_general(a, b, (((1,), (1,)), ((), ())), preferred_element_type=F32)


def _dot_nn(a, b):
    return lax.dot_general(a, b, (((1,), (0,)), ((), ())), preferred_element_type=F32)


def _dot_tn(a, b):
    return lax.dot_general(a, b, (((0,), (0,)), ((), ())), preferred_element_type=F32)


def _block_rows(n, dilation, S):
    P = ATT_BLOCK
    nb = S // (dilation * P)
    r, i = n // nb, n % nb
    cur = pl.ds(i * (P * dilation) + r, P, stride=dilation)
    prv = pl.ds(jnp.maximum(i - 1, 0) * (P * dilation) + r, P, stride=dilation)
    return cur, prv, jnp.minimum(i, 1)


def _attn_fwd(q, kv, tabs, name):
    B, S, D = q.shape
    P, H = ATT_BLOCK, N_HEADS
    scale = (D // H) ** -0.5
    half = LANE // 2
    nsteps = S // P // ATT_UNROLL

    def body(q_ref, k_ref, v_ref, tab_ref, o_ref, lse_ref, acc_ref, m_ref, l_ref):
        lo, hi = _lane_masks()
        for bi, (_, d) in enumerate(DILATED_BRANCHES):
            def block(n, carry, bi=bi, d=d):
                rows = [_block_rows(n + j * nsteps, d, S) for j in range(ATT_UNROLL)]
                loaded = []
                for cur, prv, variant in rows:
                    qb = q_ref[cur, :].astype(BF16)
                    kw = jnp.concatenate([k_ref[prv, :], k_ref[cur, :]], axis=0).astype(BF16)
                    vw = jnp.concatenate([v_ref[prv, :], v_ref[cur, :]], axis=0).astype(BF16)
                    loaded.append((qb, kw, vw))
                results = []
                for (cur, prv, variant), (qb, kw, vw) in zip(rows, loaded):
                    m_pair = l_pair = a_pair = None
                    for e, msk in enumerate((lo, hi)):
                        s = _dot_nt(jnp.where(msk, qb, 0), kw) * scale + tab_ref[bi, variant, e]
                        m_e = jnp.max(s, axis=-1, keepdims=True)
                        pe = jnp.exp(s - m_e)
                        l_e = jnp.sum(pe, axis=-1, keepdims=True)
                        a_e = _dot_nn(pe.astype(BF16), jnp.where(msk, vw, 0))
                        m_pair = m_e if e == 0 else jnp.where(lo, m_pair, m_e)
                        l_pair = l_e if e == 0 else jnp.where(lo, l_pair, l_e)
                        a_pair = a_e if e == 0 else a_pair + a_e
                    results.append((m_pair, l_pair, a_pair))
                for (cur, prv, variant), (m_pair, l_pair, a_pair) in zip(rows, results):
                    m_ref[bi, cur, :] = m_pair
                    l_ref[bi, cur, :] = l_pair
                    acc_ref[bi, cur, :] = a_pair
                return carry

            lax.fori_loop(0, nsteps, block, 0)

        nbr = len(DILATED_BRANCHES)
        chunk = 512

        def merge(t, carry):
            rs = pl.ds(pl.multiple_of(t * chunk, chunk), chunk)
            ms = [m_ref[b, rs, :] for b in range(nbr)]
            m = functools.reduce(jnp.maximum, ms)
            ws = [jnp.exp(mb - m) for mb in ms]
            l = functools.reduce(jnp.add, [w * l_ref[b, rs, :] for b, w in enumerate(ws)])
            acc = functools.reduce(jnp.add, [w * acc_ref[b, rs, :] for b, w in enumerate(ws)])
            o_ref[rs, :] = (acc / l).astype(BF16)
            lse_ref[rs, :] = m + jnp.log(l)
            return carry

        lax.fori_loop(0, S // chunk, merge, 0)

    nl = D // LANE
    col = lambda off: pl.BlockSpec((None, S, LANE), lambda b, p: (b, 0, off + p))
    tab_spec = pl.BlockSpec((3, 2, 2, P, 2 * P), lambda b, p: (0, 0, p, 0, 0))
    return pl.pallas_call(
        body, name=name, grid=(B, H // 2), in_specs=[col(0), col(0), col(nl), tab_spec],
        out_specs=[col(0), col(0)],
        out_shape=[jax.ShapeDtypeStruct((B, S, D), BF16), jax.ShapeDtypeStruct((B, S, D), F32)],
        scratch_shapes=[pltpu.VMEM((len(DILATED_BRANCHES), S, LANE), F32)] * 3,
        compiler_params=pltpu.CompilerParams(
            dimension_semantics=("parallel", "parallel"), vmem_limit_bytes=VMEM_CAP),
    )(q, kv, kv, tabs)


def _attn_bwd(q, kv, do, o, lse, tabs, name):
    B, S, D = q.shape
    P, H = ATT_BLOCK, N_HEADS
    scale = (D // H) ** -0.5
    half = LANE // 2
    nsteps = S // P // ATT_UNROLL

    def body(q_ref, k_ref, v_ref, do_ref, o_ref, lse_ref, tab_ref, dq_ref, dkv_ref, dtab_ref, delta_ref):
        lo, hi = _lane_masks()

        @pl.when(pl.program_id(1) == 0)
        def _():
            dtab_ref[...] = jnp.zeros_like(dtab_ref)

        dkv_ref[...] = jnp.zeros_like(dkv_ref)
        prod = do_ref[...] * o_ref[...].astype(F32)
        delta_ref[...] = jnp.where(lo, jnp.sum(jnp.where(lo, prod, 0.0), axis=-1, keepdims=True),
                                   jnp.sum(jnp.where(hi, prod, 0.0), axis=-1, keepdims=True))

        for bi, (_, d) in enumerate(DILATED_BRANCHES):
            def block(n, carry, bi=bi, d=d):
                rows = [_block_rows(n + j * nsteps, d, S) for j in range(ATT_UNROLL)]
                loaded = []
                for cur, prv, variant in rows:
                    qb = q_ref[cur, :].astype(BF16)
                    kw = jnp.concatenate([k_ref[prv, :], k_ref[cur, :]], axis=0).astype(BF16)
                    vw = jnp.concatenate([v_ref[prv, :], v_ref[cur, :]], axis=0).astype(BF16)
                    dob = do_ref[cur, :].astype(BF16)
                    dq_old = dq_ref[cur, :] if bi > 0 else None
                    loaded.append((qb, kw, vw, dob, lse_ref[cur, :], delta_ref[cur, :], dq_old))
                results = []
                for j, ((cur, prv, variant), (qb, kw, vw, dob, lse_b, dl_b, dq_old)) in enumerate(zip(rows, loaded)):
                    dq_pair = dk_pair = dv_pair = None
                    for e, msk in enumerate((lo, hi)):
                        qm = jnp.where(msk, qb, 0)
                        dom = jnp.where(msk, dob, 0)
                        s = _dot_nt(qm, kw) * scale + tab_ref[bi, variant, e]
                        pe = jnp.exp(s - lse_b[:, e * half:e * half + 1])
                        ds = pe * (_dot_nt(dom, vw) - dl_b[:, e * half:e * half + 1])
                        dtab_ref[bi, e] += ds
                        dsb = ds.astype(BF16)
                        dq_e = _dot_nn(dsb, jnp.where(msk, kw, 0))
                        dk_e = _dot_tn(dsb, qm)
                        dv_e = _dot_tn(pe.astype(BF16), dom)
                        dq_pair = dq_e if e == 0 else dq_pair + dq_e
                        dk_pair = dk_e if e == 0 else dk_pair + dk_e
                        dv_pair = dv_e if e == 0 else dv_pair + dv_e
                    dq_pair = dq_pair * scale
                    if bi > 0:
                        dq_pair = dq_pair + dq_old
                    results.append((dq_pair, dk_pair * scale, dv_pair))
                for (cur, prv, variant), (dq_pair, dk_pair, dv_pair) in zip(rows, results):
                    dq_ref[cur, :] = dq_pair
                    dkv_ref[0, cur, :] += dk_pair[P:, :]
                    dkv_ref[1, cur, :] += dv_pair[P:, :]
                    dkv_ref[0, prv, :] += dk_pair[:P, :]
                    dkv_ref[1, prv, :] += dv_pair[:P, :]
                return carry

            lax.fori_loop(0, nsteps, block, 0)

    nl = D // LANE
    col = lambda off: pl.BlockSpec((None, S, LANE), lambda p, b: (b, 0, off + p))
    tab_spec = pl.BlockSpec((3, 2, 2, P, 2 * P), lambda p, b: (0, 0, p, 0, 0))
    return pl.pallas_call(
        body, name=name, grid=(H // 2, B),
        in_specs=[col(0), col(0), col(nl), col(0), col(0), col(0), tab_spec],
        out_specs=[col(0), pl.BlockSpec((2, None, S, LANE), lambda p, b: (0, b, 0, p)),
                   pl.BlockSpec((None, 3, 2, P, 2 * P), lambda p, b: (p, 0, 0, 0, 0))],
        out_shape=[jax.ShapeDtypeStruct((B, S, D), F32), jax.ShapeDtypeStruct((2, B, S, D), F32),
                   jax.ShapeDtypeStruct((H // 2, 3, 2, P, 2 * P), F32)],
        scratch_shapes=[pltpu.VMEM((S, LANE), F32)],
        compiler_params=pltpu.CompilerParams(
            dimension_semantics=("parallel", "arbitrary"),
            vmem_limit_bytes=VMEM_CAP),
    )(q, kv, kv, do, o, lse, tabs)


def _attn_fwd_branch(q, kv, tab, dilation, name):
    B, S, D = q.shape
    P, d = ATT_BLOCK, dilation
    L = S // d
    nb = L // P
    H = N_HEADS
    scale = (D // H) ** -0.5
    qv = q.reshape(B, L, d * D)
    kvv = kv.reshape(B, L, d * 2 * D)

    def body(q_ref, kp_ref, kc_ref, vp_ref, vc_ref, tab_ref, o_ref, lse_ref):
        lo, hi = _lane_masks()
        for p in range(H // 2):
            sl = slice(p * LANE, (p + 1) * LANE)
            qp = q_ref[:, sl]
            kp = jnp.concatenate([kp_ref[:, sl], kc_ref[:, sl]], axis=0)
            vp = jnp.concatenate([vp_ref[:, sl], vc_ref[:, sl]], axis=0)
            o_pair = None
            lse_pair = None
            for e, msk in enumerate((lo, hi)):
                s = _dot_nt(jnp.where(msk, qp, 0), kp) * scale + tab_ref[2 * p + e]
                m = jnp.max(s, axis=-1, keepdims=True)
                pe = jnp.exp(s - m)
                den = jnp.sum(pe, axis=-1, keepdims=True)
                o_e = _dot_nn(pe.astype(BF16), jnp.where(msk, vp, 0)) / den
                lse_e = m + jnp.log(den)
                o_pair = o_e if e == 0 else o_pair + o_e
                lse_pair = lse_e if e == 0 else jnp.where(lo, lse_pair, lse_e)
            o_ref[:, sl] = o_pair
            lse_ref[:, sl] = lse_pair

    blk = lambda f: pl.BlockSpec((None, P, D), f)
    prev = lambda i: jnp.maximum(i - 1, 0)
    in_specs = [blk(lambda b, r, i: (b, i, r)),
                blk(lambda b, r, i: (b, prev(i), 2 * r)), blk(lambda b, r, i: (b, i, 2 * r)),
                blk(lambda b, r, i: (b, prev(i), 2 * r + 1)), blk(lambda b, r, i: (b, i, 2 * r + 1)),
                pl.BlockSpec((None, H, P, 2 * P), lambda b, r, i: (jnp.minimum(i, 1), 0, 0, 0))]
    o, lse = pl.pallas_call(
        body, name=name, grid=(B, d, nb), in_specs=in_specs,
        out_specs=[blk(lambda b, r, i: (b, i, r))] * 2,
        out_shape=[jax.ShapeDtypeStruct((B, L, d * D), F32)] * 2,
        compiler_params=pltpu.CompilerParams(
            dimension_semantics=("parallel", "parallel", "parallel"),
            vmem_limit_bytes=_vmem(_nbytes((H, P, 2 * P), F32) + 9 * _nbytes((P, D), F32))),
    )(qv, kvv, kvv, kvv, kvv, tab)
    return o.reshape(B * S, D), lse.reshape(B * S, D)


def _attn_merge(os_, lses, name, tm=512):
    T, D = os_[0].shape
    n = len(os_)
    tm = _tile(T, tm)

    def body(*refs):
        o_refs, l_refs = refs[:n], refs[n:2 * n]
        out_ref, lse_ref = refs[2 * n:]
        ls = [r[...] for r in l_refs]
        m = functools.reduce(jnp.maximum, ls)
        ws = [jnp.exp(l - m) for l in ls]
        tot = functools.reduce(jnp.add, ws)
        acc = functools.reduce(jnp.add, [w * r[...] for w, r in zip(ws, o_refs)])
        out_ref[...] = (acc / tot).astype(BF16)
        lse_ref[...] = m + jnp.log(tot)

    row = pl.BlockSpec((tm, D), lambda i: (i, 0))
    return pl.pallas_call(
        body, name=name, grid=(T // tm,), in_specs=[row] * (2 * n), out_specs=[row, row],
        out_shape=[jax.ShapeDtypeStruct((T, D), BF16), jax.ShapeDtypeStruct((T, D), F32)],
        compiler_params=pltpu.CompilerParams(
            dimension_semantics=("parallel",), vmem_limit_bytes=_vmem((2 * n + 4) * _nbytes((tm, D), F32))),
    )(*os_, *lses)


def _attn_bwd_q(q, kv, do, o, lse, tab, dq_acc, dilation, name):
    B, S, D = q.shape
    P, d = ATT_BLOCK, dilation
    L = S // d
    nb = L // P
    H = N_HEADS
    scale = (D // H) ** -0.5
    view = lambda t, c: t.reshape(B, L, d * c)
    has_acc = dq_acc is not None

    def body(q_ref, kp_ref, kc_ref, vp_ref, vc_ref, do_ref, o_ref, lse_ref, tab_ref, *rest):
        if has_acc:
            acc_ref, dq_ref, dtab_ref = rest
        else:
            dq_ref, dtab_ref = rest

        @pl.when((pl.program_id(0) == 0) & (pl.program_id(1) == 0) & (pl.program_id(2) == 0))
        def _():
            dtab_ref[...] = jnp.zeros_like(dtab_ref)

        lo, hi = _lane_masks()
        for p in range(H // 2):
            sl = slice(p * LANE, (p + 1) * LANE)
            qp = q_ref[:, sl]
            kp = jnp.concatenate([kp_ref[:, sl], kc_ref[:, sl]], axis=0)
            vp = jnp.concatenate([vp_ref[:, sl], vc_ref[:, sl]], axis=0)
            dop = do_ref[:, sl]
            prod = dop.astype(F32) * o_ref[:, sl].astype(F32)
            lsep = lse_ref[:, sl]
            dq_pair = None
            for e, msk in enumerate((lo, hi)):
                s = _dot_nt(jnp.where(msk, qp, 0), kp) * scale + tab_ref[2 * p + e]
                pe = jnp.exp(s - lsep[:, e * (LANE // 2):e * (LANE // 2) + 1])
                dp = _dot_nt(jnp.where(msk, dop, 0).astype(BF16), vp)
                delta = jnp.sum(jnp.where(msk, prod, 0.0), axis=-1, keepdims=True)
                ds = pe * (dp - delta)
                dtab_ref[2 * p + e] += ds
                dq_e = _dot_nn(ds.astype(BF16), jnp.where(msk, kp, 0))
                dq_pair = dq_e if e == 0 else dq_pair + dq_e
            dq_pair = dq_pair * scale
            if has_acc:
                dq_pair = dq_pair + acc_ref[:, sl]
            dq_ref[:, sl] = dq_pair

    blk = lambda f: pl.BlockSpec((None, P, D), f)
    prev = lambda i: jnp.maximum(i - 1, 0)
    own = blk(lambda b, r, i: (b, i, r))
    tab_spec = pl.BlockSpec((None, H, P, 2 * P), lambda b, r, i: (jnp.minimum(i, 1), 0, 0, 0))
    in_specs = [own,
                blk(lambda b, r, i: (b, prev(i), 2 * r)), blk(lambda b, r, i: (b, i, 2 * r)),
                blk(lambda b, r, i: (b, prev(i), 2 * r + 1)), blk(lambda b, r, i: (b, i, 2 * r + 1)),
                own, own, own, tab_spec]
    kvv = view(kv, 2 * D)
    args = [view(q, D), kvv, kvv, kvv, kvv, view(do, D), view(o, D), view(lse, D), tab]
    if has_acc:
        in_specs.append(own)
        args.append(view(dq_acc, D))
    dq, dtab = pl.pallas_call(
        body, name=name, grid=(B, d, nb), in_specs=in_specs,
        out_specs=[own, pl.BlockSpec((H, P, 2 * P), lambda b, r, i: (0, 0, 0))],
        out_shape=[jax.ShapeDtypeStruct((B, L, d * D), F32), jax.ShapeDtypeStruct((H, P, 2 * P), F32)],
        compiler_params=pltpu.CompilerParams(
            dimension_semantics=("arbitrary", "arbitrary", "arbitrary"),
            vmem_limit_bytes=_vmem(2 * _nbytes((H, P, 2 * P), F32) + 12 * _nbytes((P, D), F32))),
    )(*args)
    return dq.reshape(B, S, D), dtab


def _attn_bwd_kv(q, kv, do, o, lse, tabk, dkv_acc, dilation, name):
    B, S, D = q.shape
    P, d = ATT_BLOCK, dilation
    L = S // d
    nb = L // P
    H = N_HEADS
    scale = (D // H) ** -0.5
    view = lambda t, c: t.reshape(B, L, d * c)
    has_acc = dkv_acc is not None

    def body(k_ref, v_ref, qa_ref, qb_ref, doa_ref, dob_ref, oa_ref, ob_ref, la_ref, lb_ref, tab_ref, *rest):
        if has_acc:
            acc_ref, dkv_ref = rest
        else:
            (dkv_ref,) = rest
        lo, hi = _lane_masks()
        for p in range(H // 2):
            sl = slice(p * LANE, (p + 1) * LANE)
            kp, vp = k_ref[:, sl], v_ref[:, sl]
            q2 = jnp.concatenate([qa_ref[:, sl], qb_ref[:, sl]], axis=0)
            do2 = jnp.concatenate([doa_ref[:, sl], dob_ref[:, sl]], axis=0)
            o2 = jnp.concatenate([oa_ref[:, sl], ob_ref[:, sl]], axis=0)
            lse2 = jnp.concatenate([la_ref[:, sl], lb_ref[:, sl]], axis=0)
            prod = do2.astype(F32) * o2.astype(F32)
            dk_pair = None
            dv_pair = None
            for e, msk in enumerate((lo, hi)):
                qm = jnp.where(msk, q2, 0)
                dom = jnp.where(msk, do2, 0).astype(BF16)
                s = _dot_nt(qm, kp) * scale + tab_ref[2 * p + e]
                pe = jnp.exp(s - lse2[:, e * (LANE // 2):e * (LANE // 2) + 1])
                dp = _dot_nt(dom, vp)
                delta = jnp.sum(jnp.where(msk, prod, 0.0), axis=-1, keepdims=True)
                ds = pe * (dp - delta)
                dv_e = _dot_tn(pe.astype(BF16), dom)
                dk_e = _dot_tn(ds.astype(BF16), qm)
                dk_pair = dk_e if e == 0 else dk_pair + dk_e
                dv_pair = dv_e if e == 0 else dv_pair + dv_e
            dk_pair = dk_pair * scale
            if has_acc:
                dk_pair = dk_pair + acc_ref[0, :, sl]
                dv_pair = dv_pair + acc_ref[1, :, sl]
            dkv_ref[0, :, sl] = dk_pair
            dkv_ref[1, :, sl] = dv_pair

    blk = lambda f: pl.BlockSpec((None, P, D), f)
    nxt = lambda i: jnp.minimum(i + 1, nb - 1)
    own = blk(lambda b, r, i: (b, i, r))
    nx = blk(lambda b, r, i: (b, nxt(i), r))
    tab_spec = pl.BlockSpec((None, H, 2 * P, P), lambda b, r, i: (jnp.where(i == nb - 1, 1, 0), 0, 0, 0))
    pair = pl.BlockSpec((2, None, P, D), lambda b, r, i: (0, b, i, r))
    in_specs = [blk(lambda b, r, i: (b, i, 2 * r)), blk(lambda b, r, i: (b, i, 2 * r + 1)),
                own, nx, own, nx, own, nx, own, nx, tab_spec]
    kvv, qv, dov, ov, lv = view(kv, 2 * D), view(q, D), view(do, D), view(o, D), view(lse, D)
    args = [kvv, kvv, qv, qv, dov, dov, ov, ov, lv, lv, tabk]
    if has_acc:
        in_specs.append(pair)
        args.append(dkv_acc.reshape(2, B, L, d * D))
    dkv = pl.pallas_call(
        body, name=name, grid=(B, d, nb), in_specs=in_specs, out_specs=pair,
        out_shape=jax.ShapeDtypeStruct((2, B, L, d * D), F32),
        compiler_params=pltpu.CompilerParams(
            dimension_semantics=("parallel", "parallel", "parallel"),
            vmem_limit_bytes=_vmem(_nbytes((H, P, 2 * P), F32) + 16 * _nbytes((P, D), F32))),
    )(*args)
    return dkv.reshape(2, B, S, D)


def _mesh_place():
    x, y, c = lax.axis_index("x"), lax.axis_index("y"), lax.axis_index("c")
    return x, y, c, (x, y, 1 - c), [(1 - x, y), (x, 1 - y), (1 - x, 1 - y)]


def _rdma(src, dst, send_sems, recv_sems, idx, to):
    return pltpu.make_async_remote_copy(
        src_ref=src, dst_ref=dst, send_sem=send_sems.at[idx], recv_sem=recv_sems.at[idx],
        device_id=to, device_id_type=pl.DeviceIdType.MESH)


def _comm_call(body, name, args, out_shape, n_sems, n_local):
    any_spec = pl.BlockSpec(memory_space=pl.ANY)
    return pl.pallas_call(
        body, name=name, in_specs=[any_spec] * len(args), out_specs=[any_spec] * len(out_shape),
        out_shape=out_shape,
        scratch_shapes=[pltpu.SemaphoreType.DMA(n_sems), pltpu.SemaphoreType.DMA(n_sems),
                        pltpu.SemaphoreType.DMA((n_local,))],
        compiler_params=pltpu.CompilerParams(has_side_effects=True),
    )(*args)


def _all_gather(arrays, name):
    n = len(arrays)

    def body(*refs):
        ins, outs = refs[:n], refs[n:2 * n]
        send_sems, recv_sems, loc_sems = refs[2 * n:]
        x, y, c, sib, chips = _mesh_place()
        me = (x, y, c)

        def rows(a, dev):
            return outs[a].at[4 * dev[0] + 2 * dev[1] + dev[2]]

        def copy(a, k, block, to, src=None):
            return _rdma(rows(a, block) if src is None else src, rows(a, block), send_sems, recv_sems, (a, k), to)

        local, first, passed = [], [], []
        for a in range(n):
            cp = pltpu.make_async_copy(ins[a], rows(a, me), loc_sems.at[a])
            cp.start()
            local.append(cp)
            first.append(copy(a, 0, me, sib, src=ins[a]))
            first += [copy(a, 1 + j, me, (*chip, c), src=ins[a]) for j, chip in enumerate(chips)]
        for cp in first:
            cp.start()
        for j, chip in enumerate(chips):
            for a in range(n):
                copy(a, 1 + j, (*chip, c), me).wait_recv()
                cp = copy(a, 4 + j, (*chip, c), sib)
                cp.start()
                passed.append(cp)
        for a in range(n):
            copy(a, 0, sib, me).wait_recv()
            for j, chip in enumerate(chips):
                copy(a, 4 + j, (*chip, 1 - c), me).wait_recv()
        for cp in first + passed:
            cp.wait_send()
        for cp in local:
            cp.wait()

    out_shape = [jax.ShapeDtypeStruct((N_DEV,) + g.shape, g.dtype) for g in arrays]
    return list(_comm_call(body, name, arrays, out_shape, (n, N_DEV - 1), n))


def _core_pair_exchange(halves, gathered, name):
    nh, ng = len(halves), len(gathered)

    def body(*refs):
        h_in, g_in = refs[:nh], refs[nh:nh + ng]
        outs = refs[nh + ng:nh + ng + nh + ng]
        theirs, g_out = outs[:nh], outs[nh:]
        send_sems, recv_sems, loc_sems = refs[nh + ng + nh + ng:]
        x, y, c, sib, chips = _mesh_place()
        me = 4 * x + 2 * y + c
        peers = [sib] + [(*chip, pc) for chip in chips for pc in (c, 1 - c)]
        nchip = N_DEV // 2
        gbase = nh * nchip
        local, sends = [], []
        for a in range(nh):
            for ch in range(nchip):
                cp = _rdma(h_in[a].at[ch, 1 - c], theirs[a].at[ch], send_sems, recv_sems, a * nchip + ch, sib)
                cp.start()
                sends.append(cp)
        for g in range(ng):
            cp = pltpu.make_async_copy(g_in[g], g_out[g].at[me], loc_sems.at[g])
            cp.start()
            local.append(cp)
            for k, peer in enumerate(peers):
                cp = _rdma(g_in[g], g_out[g].at[me], send_sems, recv_sems, gbase + g * (N_DEV - 1) + k, peer)
                cp.start()
                sends.append(cp)
        for cp in sends:
            cp.wait_send()
        for a in range(nh):
            for ch in range(nchip):
                _rdma(h_in[a].at[ch, 1 - c], theirs[a].at[ch], send_sems, recv_sems, a * nchip + ch, sib).wait_recv()
        for g in range(ng):
            for k, peer in enumerate(peers):
                pid = 4 * peer[0] + 2 * peer[1] + peer[2]
                _rdma(g_in[g], g_out[g].at[pid], send_sems, recv_sems, gbase + g * (N_DEV - 1) + k, peer).wait_recv()
        for cp in local:
            cp.wait()

    half = [jax.ShapeDtypeStruct((h.shape[0],) + h.shape[2:], h.dtype) for h in halves]
    out_shape = half + [jax.ShapeDtypeStruct((N_DEV,) + g.shape, g.dtype) for g in gathered]
    outs = _comm_call(body, name, list(halves) + list(gathered), out_shape,
                      (nh * (N_DEV // 2) + ng * (N_DEV - 1),), max(ng, 1))
    return list(outs[:nh]), list(outs[nh:])


def _chip_exchange(arrays, name):
    n = len(arrays)

    def body(*refs):
        ins, outs = refs[:n], refs[n:2 * n]
        send_sems, recv_sems, _ = refs[2 * n:]
        x, y, c, sib, chips = _mesh_place()
        sends = []
        for a in range(n):
            for j, chip in enumerate(chips):
                cp = _rdma(ins[a].at[2 * chip[0] + chip[1]], outs[a].at[j], send_sems, recv_sems, (a, j), (*chip, c))
                cp.start()
                sends.append(cp)
        for cp in sends:
            cp.wait_send()
        for a in range(n):
            for j, chip in enumerate(chips):
                _rdma(ins[a].at[j], outs[a].at[j], send_sems, recv_sems, (a, j), (*chip, c)).wait_recv()

    out_shape = [jax.ShapeDtypeStruct((3,) + g.shape[1:], g.dtype) for g in arrays]
    return list(_comm_call(body, name, arrays, out_shape, (n, 3), 1))


def _gather_copies(src_refs, land_refs, send_sems, recv_sems):
    x, y, c, sib, chips = _mesh_place()
    me = 4 * x + 2 * y + c
    peers = [sib] + [(*chip, c) for chip in chips]
    return [pltpu.make_async_remote_copy(
                src_ref=src, dst_ref=land.at[me], send_sem=send_sems[4 * a + k], recv_sem=recv_sems[4 * a + k],
                device_id=peer, device_id_type=pl.DeviceIdType.MESH)
            for a, (src, land) in enumerate(zip(src_refs, land_refs)) for k, peer in enumerate(peers)]


def _forward_to_sibling(gathered, name):
    n = len(gathered)

    def body(*refs):
        outs = refs[n:2 * n]
        send_sems, recv_sems, _ = refs[2 * n:]
        x, y, c, sib, chips = _mesh_place()
        sends = []
        for a in range(n):
            for j, chip in enumerate(chips):
                rows = outs[a].at[4 * chip[0] + 2 * chip[1] + c]
                cp = _rdma(rows, rows, send_sems, recv_sems, (a, j), sib)
                cp.start()
                sends.append(cp)
        for cp in sends:
            cp.wait_send()
        for a in range(n):
            for j, chip in enumerate(chips):
                rows = outs[a].at[4 * chip[0] + 2 * chip[1] + (1 - c)]
                _rdma(rows, rows, send_sems, recv_sems, (a, j), sib).wait_recv()

    any_spec = pl.BlockSpec(memory_space=pl.ANY)
    return list(pl.pallas_call(
        body, name=name, in_specs=[any_spec] * n, out_specs=[any_spec] * n,
        out_shape=[jax.ShapeDtypeStruct(g.shape, g.dtype) for g in gathered],
        input_output_aliases={i: i for i in range(n)},
        scratch_shapes=[pltpu.SemaphoreType.DMA((n, 3)), pltpu.SemaphoreType.DMA((n, 3)),
                        pltpu.SemaphoreType.DMA((1,))],
        compiler_params=pltpu.CompilerParams(has_side_effects=True),
    )(*gathered))


def _split_start(copies_fn, srcs, land_shapes, n_sems, name, after=None):
    ns, nl = len(srcs), len(land_shapes)
    nc = n_sems
    hbm = pl.BlockSpec(memory_space=pltpu.HBM)
    sem = pl.BlockSpec(memory_space=pltpu.SEMAPHORE)

    n_in = ns + nl + (0 if after is None else 1)

    def body(*refs):
        src_refs, land_refs = refs[:ns], refs[ns:ns + nl]
        send_sems, recv_sems = refs[n_in:n_in + nc], refs[n_in + nc:n_in + 2 * nc]
        token = refs[-1]
        for cp in copies_fn(src_refs, land_refs, send_sems, recv_sems):
            cp.start()
        token[...] = jnp.zeros_like(token)

    lands = [lax.empty(s.shape, s.dtype) for s in land_shapes]
    thru = [pltpu.HBM(t.shape, t.dtype) for t in list(srcs) + lands]
    order = [] if after is None else [after]
    outs = pl.pallas_call(
        body, name=name,
        out_shape=(*([pltpu.SemaphoreType.DMA(())] * (2 * nc)), *thru, jax.ShapeDtypeStruct((8, LANE), F32)),
        in_specs=[hbm] * (ns + nl) + [pl.BlockSpec(memory_space=pl.ANY)] * len(order),
        out_specs=(*([sem] * (2 * nc)), *([hbm] * (ns + nl)), pl.BlockSpec(memory_space=pltpu.VMEM)),
        input_output_aliases={i: 2 * nc + i for i in range(ns + nl)},
        compiler_params=pltpu.CompilerParams(has_side_effects=pltpu.SideEffectType.DATAFLOW_SIDE_EFFECTING),
    )(*[pltpu.with_memory_space_constraint(t, pltpu.HBM) for t in list(srcs) + lands], *order)
    thru_out = outs[2 * nc:2 * nc + ns + nl]
    return list(outs[:nc]), list(outs[nc:2 * nc]), list(thru_out[:ns]), list(thru_out[ns:]), outs[-1]


def _split_wait(copies_fn, send_sems, recv_sems, srcs, lands, after, name):
    ns, nl, nc = len(srcs), len(lands), len(send_sems)
    hbm = pl.BlockSpec(memory_space=pltpu.HBM)
    sem = pl.BlockSpec(memory_space=pltpu.SEMAPHORE)

    def body(*refs):
        src_refs, land_refs = refs[:ns], refs[ns:ns + nl]
        send_refs, recv_refs = refs[ns + nl:ns + nl + nc], refs[ns + nl + nc:ns + nl + 2 * nc]
        for cp in copies_fn(src_refs, land_refs, send_refs, recv_refs):
            cp.wait_send()
            cp.wait_recv()

    outs = pl.pallas_call(
        body, name=name,
        out_shape=[pltpu.HBM(t.shape, t.dtype) for t in list(srcs) + list(lands)],
        in_specs=[hbm] * (ns + nl) + [sem] * (2 * nc) + [pl.BlockSpec(memory_space=pl.ANY)],
        out_specs=[hbm] * (ns + nl),
        input_output_aliases={i: i for i in range(ns + nl)},
        compiler_params=pltpu.CompilerParams(has_side_effects=pltpu.SideEffectType.DATAFLOW_SIDE_EFFECTING),
    )(*srcs, *lands, *send_sems, *recv_sems, after)
    return list(outs[:ns]), list(outs[ns:])


def _chip_copies(src_refs, land_refs, send_sems, recv_sems):
    x, y, c, sib, chips = _mesh_place()
    return [pltpu.make_async_remote_copy(
                src_ref=src.at[2 * chip[0] + chip[1]], dst_ref=land.at[j], send_sem=send_sems[3 * a + j],
                recv_sem=recv_sems[3 * a + j], device_id=(*chip, c), device_id_type=pl.DeviceIdType.MESH)
            for a, (src, land) in enumerate(zip(src_refs, land_refs)) for j, chip in enumerate(chips)]


def _pair_sum(halves, theirs, core, name, tr=256):
    nchip, _, R, C = halves.shape
    tr = _row_tile(R, tr)

    def body(core_ref, h_ref, t_ref, o_ref):
        o_ref[...] = (h_ref[...] + t_ref[...]).astype(BF16)

    grid_spec = pltpu.PrefetchScalarGridSpec(
        num_scalar_prefetch=1, grid=(nchip, R // tr),
        in_specs=[pl.BlockSpec((None, None, tr, C), lambda ch, i, core_ref: (ch, core_ref[0], i, 0)),
                  pl.BlockSpec((None, tr, C), lambda ch, i, core_ref: (ch, i, 0))],
        out_specs=pl.BlockSpec((None, tr, C), lambda ch, i, core_ref: (ch, i, 0)))
    return pl.pallas_call(
        body, name=name, grid_spec=grid_spec, out_shape=jax.ShapeDtypeStruct((nchip, R, C), BF16),
        compiler_params=pltpu.CompilerParams(
            dimension_semantics=("parallel", "parallel"), vmem_limit_bytes=_vmem(3 * _nbytes((tr, C), F32))),
    )(core, halves, theirs)


def _chip_sum(own, recv, chip, name, tr=256):
    _, R, C = own.shape
    tr = _row_tile(R, tr)

    def body(chip_ref, o_ref, r_ref, out_ref):
        acc = o_ref[...].astype(F32)
        for j in range(3):
            acc = acc + r_ref[j].astype(F32)
        out_ref[...] = acc

    grid_spec = pltpu.PrefetchScalarGridSpec(
        num_scalar_prefetch=1, grid=(R // tr,),
        in_specs=[pl.BlockSpec((None, tr, C), lambda i, chip_ref: (chip_ref[0], i, 0)),
                  pl.BlockSpec((3, tr, C), lambda i, chip_ref: (0, i, 0))],
        out_specs=pl.BlockSpec((tr, C), lambda i, chip_ref: (i, 0)))
    return pl.pallas_call(
        body, name=name, grid_spec=grid_spec, out_shape=jax.ShapeDtypeStruct((R, C), F32),
        compiler_params=pltpu.CompilerParams(
            dimension_semantics=("parallel",), vmem_limit_bytes=_vmem(4 * _nbytes((tr, C), F32))),
    )(chip, own, recv)


def _row_tile(rows, target):
    best = rows
    for t in range(8, min(rows, target) + 1, 8):
        if rows % t == 0:
            best = t
    return best


def _sum_slots(recv, name, tr=128):
    n, R, C = recv.shape
    tr = _row_tile(R, tr if recv.dtype == F32 else 2 * tr)

    def body(r_ref, o_ref):
        acc = r_ref[0].astype(F32)
        for k in range(1, n):
            acc = acc + r_ref[k].astype(F32)
        o_ref[...] = acc

    return pl.pallas_call(
        body, name=name, grid=(R // tr,),
        in_specs=[pl.BlockSpec((n, tr, C), lambda i: (0, i, 0))],
        out_specs=pl.BlockSpec((tr, C), lambda i: (i, 0)),
        out_shape=jax.ShapeDtypeStruct((R, C), F32),
        compiler_params=pltpu.CompilerParams(
            dimension_semantics=("parallel",), vmem_limit_bytes=_vmem(10 * _nbytes((tr, C), F32))),
    )(recv)


def _adamw(g, w, m, v, name, tr=256):
    R, C = g.shape
    tr = _row_tile(R, tr)

    def body(g_ref, w_ref, m_ref, v_ref, d_ref, nm_ref, nv_ref):
        gv = g_ref[...]
        nm = ADAM_B1 * m_ref[...] + (1.0 - ADAM_B1) * gv
        nv = ADAM_B2 * v_ref[...] + (1.0 - ADAM_B2) * (gv * gv)
        m_hat = nm / (1.0 - ADAM_B1 ** ADAM_STEP)
        v_hat = nv / (1.0 - ADAM_B2 ** ADAM_STEP)
        d_ref[...] = -ADAM_LR * (m_hat / (jnp.sqrt(v_hat) + ADAM_EPS) + ADAM_WD * w_ref[...])
        nm_ref[...] = nm
        nv_ref[...] = nv

    row = pl.BlockSpec((tr, C), lambda i: (i, 0))
    return pl.pallas_call(
        body, name=name, grid=(R // tr,), in_specs=[row] * 4, out_specs=[row] * 3,
        out_shape=[jax.ShapeDtypeStruct((R, C), F32)] * 3,
        compiler_params=pltpu.CompilerParams(
            dimension_semantics=("parallel",), vmem_limit_bytes=_vmem(8 * _nbytes((tr, C), F32))),
    )(g, w, m, v)


def _pack_rows(parts):
    flat = jnp.concatenate([p.reshape(-1).astype(F32) for p in parts])
    rows = -(-flat.shape[0] // (8 * LANE)) * 8
    return jnp.pad(flat, (0, rows * LANE - flat.shape[0])).reshape(rows, LANE)


def _unpack(flat, shapes):
    out, off = [], 0
    for shp in shapes:
        n = math.prod(shp)
        out.append(flat[off:off + n].reshape(shp))
        off += n
    return out


def kernel(x, a_norm, a_w_in, a_conv, a_w_out, kv_norm, w_kv, b_norm, b_w_q, b_w_o, rel_bias, ffn_norm, ffn_w_up, ffn_conv, ffn_conv_b, ffn_w_down, final_norm, loss_target, m_a_norm, m_a_w_in, m_a_conv, m_a_w_out, m_kv_norm, m_w_kv, m_b_norm, m_b_w_q, m_b_w_o, m_rel_bias, m_ffn_norm, m_ffn_w_up, m_ffn_conv, m_ffn_conv_b, m_ffn_w_down, m_final_norm, v_a_norm, v_a_w_in, v_a_conv, v_a_w_out, v_kv_norm, v_w_kv, v_b_norm, v_b_w_q, v_b_w_o, v_rel_bias, v_ffn_norm, v_ffn_w_up, v_ffn_conv, v_ffn_conv_b, v_ffn_w_down, v_final_norm):
    B, S, D = x.shape
    T = B * S
    F = ffn_w_down.shape[1] * N_DEV
    me = 4 * lax.axis_index("x") + 2 * lax.axis_index("y") + lax.axis_index("c")

    big_shards = [a_w_in[0].T, a_w_out[0], w_kv.T, b_w_q[0], b_w_o[0],
                  ffn_w_up[0].T, ffn_w_up[1].T, ffn_w_down[0], ffn_w_down[1]]
    small_shapes = [a_norm.shape, a_conv.shape, ffn_conv.shape]
    small_pack = _pack_rows([a_norm, a_conv, ffn_conv])
    shards = [s.astype(BF16) for s in big_shards]
    first = _all_gather([shards[0], shards[1], small_pack], "gather_weights_first")
    win_t, wout = first[0].reshape(-1, D), first[1].reshape(-1, D)
    smalls = [_unpack(first[2][j].reshape(-1), small_shapes) for j in range(N_DEV)]
    a_norm_f = jnp.concatenate([s[0] for s in smalls], axis=-1)
    a_conv_f = jnp.concatenate([s[1] for s in smalls], axis=-1)[0]
    ffn_conv_f = jnp.concatenate([s[2] for s in smalls], axis=-1)

    def gather_start(tag, idxs, after):
        srcs = [shards[i] for i in idxs]
        lands = [jax.ShapeDtypeStruct((N_DEV,) + s.shape, s.dtype) for s in srcs]
        return _split_start(_gather_copies, srcs, lands, 4 * len(srcs), f"gather_start_{tag}", after=after)

    def gather_finish(tag, handle, after):
        send, recv, srcs, lands, _ = handle
        srcs, lands = _split_wait(_gather_copies, send, recv, srcs, lands, after, f"gather_wait_{tag}")
        lands = _forward_to_sibling(lands, f"gather_forward_{tag}")
        return [lax.dynamic_update_slice(g, s[None], (me, 0, 0)).reshape(-1, D) for g, s in zip(lands, srcs)]

    ffn0_w = gather_start("ffn0", [5, 7], first[0])
    rest_w = gather_start("rest", [2, 3, 4, 6, 8], ffn0_w[4])

    x2 = x.reshape(T, D)
    (xn0,) = _rmsnorm_fwd(x2, a_norm_f, "a_norm_fwd")
    bch = _matmul(xn0, win_t, mode="nt", out_dtype=BF16, after=rest_w[4], name="a_in_proj").reshape(B, S, 3 * D)
    gated = _shortconv_fwd(bch, a_conv_f, "a_gate_fwd").reshape(T, D)
    h1 = _matmul(gated, wout, mode="nn", out_dtype=F32, add=x2, name="a_out_proj")
    wup0_t, wdn0 = gather_finish("ffn0", ffn0_w, h1)
    wup_t, wdn = [wup0_t, None], [wdn0, None]

    def ffn_fwd(h, l):
        (xn,) = _rmsnorm_fwd(h, ffn_norm[l:l + 1], f"ffn{l}_norm_fwd")
        u0 = _matmul(xn, wup_t[l], mode="nt", out_dtype=BF16, name=f"ffn{l}_up", tn=1408).reshape(B, S, 2 * F)
        act = _ffn_gate_fwd(u0, ffn_conv_f[l], ffn_conv_b[l:l + 1], f"ffn{l}_gate_fwd").reshape(T, F)
        out = _matmul(act, wdn[l], mode="nn", out_dtype=F32, add=h, name=f"ffn{l}_down", tn=512, tk=2816)
        return out, (xn, u0, act)

    h2, ffn0_saved = ffn_fwd(h1, 0)
    wkv_t, wq, wo, wup_t[1], wdn[1] = gather_finish("rest", rest_w, h2)
    kvn, xnb = _rmsnorm_fwd(h2, jnp.stack([kv_norm, b_norm[0]]), "kv_b_norm_fwd")
    kv = _matmul(kvn, wkv_t, mode="nt", out_dtype=F32, name="kv_proj").reshape(B, S, 2 * D)
    q = _matmul(xnb, wq, mode="nn", out_dtype=F32, name="q_proj").reshape(B, S, D)

    tables = [_band_tables(w, d) for (w, d) in DILATED_BRANCHES]
    onehots = jnp.stack([t[0] for t in tables])
    P = ATT_BLOCK
    bias_vals = _bias_lookup(rel_bias.T, onehots, "rel_bias_lookup").reshape(3, N_HEADS, P, 2 * P)
    in_cur = (jnp.arange(2 * P) >= P)[None, :]
    tabs = []
    for bi in range(3):
        band = tables[bi][1]
        gen = jnp.where(band[None], bias_vals[bi], NEG)
        fst = jnp.where((band & in_cur)[None], bias_vals[bi], NEG)
        tabs.append(jnp.stack([fst, gen]))
    tabs = jnp.stack(tabs)

    attn3, lse3 = _attn_fwd(q, kv, tabs, "attn_fwd")
    attn = attn3.reshape(T, D)
    h3 = _matmul(attn, wo, mode="nn", out_dtype=F32, add=h2, name="o_proj")
    h4, ffn1_saved = ffn_fwd(h3, 1)

    dh4, dh4b, d_final_norm, sq = _final_loss_bwd(h4, final_norm.reshape(1, D), loss_target.reshape(T, D), "loss_bwd")
    loss_part = 0.5 * jnp.sum(sq) / D

    core_idx = lax.axis_index("c").astype(jnp.int32).reshape(1)
    chip_idx = (2 * lax.axis_index("x") + lax.axis_index("y")).astype(jnp.int32).reshape(1)
    in_flight = []

    def send_group(tag, indexed_grads, small=()):
        halves = [g.reshape(N_DEV // 2, 2, g.shape[0] // N_DEV, D) for _, g in indexed_grads]
        theirs, gath = _core_pair_exchange(halves, list(small), f"exchange_core_{tag}")
        sums = [_pair_sum(h_, t_, core_idx, f"grad_core_sum_{tag}{k}")
                for k, (h_, t_) in enumerate(zip(halves, theirs))]
        lands = [jax.ShapeDtypeStruct((3,) + s.shape[1:], s.dtype) for s in sums]
        send, recv, srcs, lnd, token = _split_start(_chip_copies, sums, lands, 3 * len(sums),
                                                    f"exchange_chip_start_{tag}")
        in_flight.append((tag, [i for i, _ in indexed_grads], send, recv, srcs, lnd))
        return token, gath

    def ffn_bwd(dh_out, dh_out_b, h_in, saved, l, after):
        xn, u0, act = saved
        dact = _matmul(dh_out_b, wdn[l], mode="nt", out_dtype=BF16, after=after, name=f"ffn{l}_down_dx",
                       tn=1408).reshape(B, S, F)
        d_wdn = _matmul(act, dh_out_b, mode="tn", out_dtype=F32, name=f"ffn{l}_down_dw", tm=1408, tn=1024, tk=2048)
        du0, d_conv, d_conv_b = _ffn_gate_bwd(u0, dact, ffn_conv_f[l], ffn_conv_b[l:l + 1], f"ffn{l}_gate_bwd")
        du0 = du0.reshape(2, T, F)
        dxn = _matmul(du0, wup_t[l], mode="nn", out_dtype=F32, a_parts=2, name=f"ffn{l}_up_dx", tk=2816)
        d_wup_t = _matmul(du0, xn, mode="tn", out_dtype=F32, a_parts=2, name=f"ffn{l}_up_dw", tm=1408, tn=1024, tk=2048)
        dh_in, dh_in_b, d_norm = _rmsnorm_bwd(h_in, ffn_norm[l:l + 1], [dxn], dh_out, f"ffn{l}_norm_bwd")
        return (dh_in, dh_in_b), (d_wdn, d_wup_t, d_conv, d_conv_b, d_norm)

    (dh3, dh3b), ffn1_grads = ffn_bwd(dh4, dh4b, h3, ffn1_saved, 1, None)
    token, _ = send_group("ffn1", [(6, ffn1_grads[1]), (8, ffn1_grads[0])])

    dattn = _matmul(dh3b, wo, mode="nt", out_dtype=F32, after=token, name="o_proj_dx").reshape(B, S, D)
    d_wo = _matmul(attn, dh3b, mode="tn", out_dtype=F32, name="o_proj_dw", tm=1024, tn=1024, tk=2048)
    dq, dkv, dtab = _attn_bwd(q, kv, dattn, attn3, lse3, tabs, "attn_bwd")
    dtabs = dtab.transpose(1, 0, 2, 3, 4).reshape(3, N_HEADS, P * 2 * P)
    d_rel_bias = _bias_grad(dtabs, onehots, "rel_bias_grad").T
    dq2 = dq.reshape(T, D)
    dkv2 = dkv.reshape(2, T, D)
    dxnb = _matmul(dq2, wq, mode="nt", out_dtype=F32, name="q_proj_dx")
    d_wq = _matmul(xnb, dq2, mode="tn", out_dtype=F32, name="q_proj_dw", tm=1024, tn=1024, tk=2048)
    dkvn = _matmul(dkv2, wkv_t, mode="nn", out_dtype=F32, a_parts=2, name="kv_proj_dx")
    d_wkv_t = _matmul(dkv2, kvn, mode="tn", out_dtype=F32, a_parts=2, name="kv_proj_dw", tm=1024, tn=1024, tk=2048)
    dh2, dh2b, d_kvb_norm = _rmsnorm_bwd(h2, jnp.stack([kv_norm, b_norm[0]]), [dkvn, dxnb], dh3, "kv_b_norm_bwd")

    token, _ = send_group("attn", [(2, d_wkv_t), (3, d_wq), (4, d_wo)])

    (dh1, dh1b), ffn0_grads = ffn_bwd(dh2, dh2b, h1, ffn0_saved, 0, token)
    token, _ = send_group("ffn0", [(5, ffn0_grads[1]), (7, ffn0_grads[0])])

    dgated = _matmul(dh1b, wout, mode="nt", out_dtype=BF16, after=token, name="a_out_proj_dx").reshape(B, S, D)
    d_wout = _matmul(gated, dh1b, mode="tn", out_dtype=F32, name="a_out_proj_dw", tm=1024, tn=1024, tk=2048)
    dbch, d_a_conv = _shortconv_bwd(bch, dgated, a_conv_f, "a_gate_bwd")
    dbch = dbch.reshape(3, T, D)
    dxn0 = _matmul(dbch, win_t, mode="nn", out_dtype=F32, a_parts=3, name="a_in_proj_dx")
    d_win_t = _matmul(dbch, xn0, mode="tn", out_dtype=F32, a_parts=3, name="a_in_proj_dw", tm=1024, tn=1024, tk=2048)
    grad_x, _, d_a_norm = _rmsnorm_bwd(x2, a_norm_f, [dxn0], dh1, "a_norm_bwd")

    small_full = [d_a_norm, d_a_conv, jnp.stack([ffn0_grads[2], ffn1_grads[2]]),
                  d_kvb_norm[0], d_kvb_norm[1], d_rel_bias, jnp.concatenate([ffn0_grads[4], ffn1_grads[4]]),
                  jnp.concatenate([ffn0_grads[3], ffn1_grads[3]]), d_final_norm, loss_part]
    small_full_shapes = [(1, D), (3, D), (2, 3, 2 * F), (D,), (1, D), rel_bias.shape, (2, D), (2, 2 * F), (D,), ()]
    token, gath = send_group("a", [(0, d_win_t), (1, d_wout)], small=[_pack_rows(small_full)])
    chip_sums, recv = {}, {}
    for tag, idxs, send, rcv, srcs, lnd in in_flight:
        srcs, lnd = _split_wait(_chip_copies, send, rcv, srcs, lnd, token, f"exchange_chip_wait_{tag}")
        for i, s_, l_ in zip(idxs, srcs, lnd):
            chip_sums[i], recv[i] = s_, l_
    small_sum = _sum_slots(gath[0], "small_grad_sum").reshape(-1)
    (g_a_norm_f, g_a_conv_f, g_ffn_conv_f, g_kv_norm, g_b_norm, g_rel_bias, g_ffn_norm, g_ffn_conv_b,
     g_final_norm, loss) = _unpack(small_sum, small_full_shapes)

    def my_cols(full, width):
        return lax.dynamic_slice_in_dim(full, me * width, width, axis=full.ndim - 1)

    g_a_norm = my_cols(g_a_norm_f, D // N_DEV)
    g_a_conv = my_cols(g_a_conv_f, D // N_DEV)[None]
    g_ffn_conv = my_cols(g_ffn_conv_f, 2 * F // N_DEV)

    big_w = [(a_w_in, m_a_w_in, v_a_w_in, True), (a_w_out, m_a_w_out, v_a_w_out, False),
             (w_kv, m_w_kv, v_w_kv, True), (b_w_q, m_b_w_q, v_b_w_q, False), (b_w_o, m_b_w_o, v_b_w_o, False),
             (ffn_w_up[0], m_ffn_w_up[0], v_ffn_w_up[0], True), (ffn_w_up[1], m_ffn_w_up[1], v_ffn_w_up[1], True),
             (ffn_w_down[0], m_ffn_w_down[0], v_ffn_w_down[0], False),
             (ffn_w_down[1], m_ffn_w_down[1], v_ffn_w_down[1], False)]
    big_out = []
    for i, (w, m, v, transposed) in enumerate(big_w):
        g = _chip_sum(chip_sums[i], recv[i], chip_idx, f"grad_sum_{i}")
        if transposed:
            g = g.T
        w2, m2, v2 = (t.reshape(g.shape) for t in (w, m, v))
        delta, nm, nv = _adamw(g, w2, m2, v2, f"adamw_{i}")
        big_out.append(tuple(t.reshape(w.shape) for t in (g, delta, nm, nv)))

    def pair(i, j):
        return tuple(jnp.stack([big_out[i][t], big_out[j][t]]) for t in range(4))

    small_w = [(a_norm, m_a_norm, v_a_norm, g_a_norm), (a_conv, m_a_conv, v_a_conv, g_a_conv),
               (ffn_conv, m_ffn_conv, v_ffn_conv, g_ffn_conv), (kv_norm, m_kv_norm, v_kv_norm, g_kv_norm),
               (b_norm, m_b_norm, v_b_norm, g_b_norm), (rel_bias, m_rel_bias, v_rel_bias, g_rel_bias),
               (ffn_norm, m_ffn_norm, v_ffn_norm, g_ffn_norm),
               (ffn_conv_b, m_ffn_conv_b, v_ffn_conv_b, g_ffn_conv_b),
               (final_norm, m_final_norm, v_final_norm, g_final_norm)]
    shapes = [t[0].shape for t in small_w]
    packed = [_pack_rows([t[i].reshape(t[0].shape) for t in small_w]) for i in (3, 0, 1, 2)]
    s_delta, s_nm, s_nv = _adamw(*packed, "adamw_small")
    s_g = [t[3].reshape(t[0].shape) for t in small_w]
    s_d, s_m, s_v = (_unpack(t.reshape(-1), shapes) for t in (s_delta, s_nm, s_nv))
    small_out = [(s_g[i], s_d[i], s_m[i], s_v[i]) for i in range(len(small_w))]

    per_weight = [small_out[0], big_out[0], small_out[1], big_out[1], small_out[3], big_out[2], small_out[4],
                  big_out[3], big_out[4], small_out[5], small_out[6], pair(5, 6), small_out[2], small_out[7],
                  pair(7, 8), small_out[8]]
    outs = [loss, grad_x.reshape(B, S, D)]
    for t in range(4):
        outs.extend(pw[t] for pw in per_weight)
    return tuple(outs)
```

```python
import functools
import math

import jax
import jax.numpy as jnp
from jax import lax
from jax.experimental import pallas as pl
from jax.experimental.pallas import tpu as pltpu

F32 = jnp.float32
BF16 = jnp.bfloat16

N_DEV = 8
N_HEADS = 16
ATT_BLOCK = 128
ATT_UNROLL = 8
DILATED_BRANCHES = ((128, 1), (512, 4), (2048, 16))
REL_BUCKETS = 32
REL_MAX_DISTANCE = 2048
RMS_EPS = 1e-6
ADAM_LR = 0.001
ADAM_B1 = 0.9
ADAM_B2 = 0.999
ADAM_EPS = 1e-08
ADAM_WD = 0.01
ADAM_STEP = 10

LANE = 128
HALO = 16
NEG = -1e30
VMEM_CAP = 56 << 20


def _vmem(block_bytes):
    return int(min(VMEM_CAP, max(32 << 20, 3 * block_bytes + (8 << 20))))


def _nbytes(shape, dtype):
    return math.prod(shape) * jnp.dtype(dtype).itemsize


def _tile(dim, target):
    best = None
    t = LANE
    while t <= min(dim, target):
        if dim % t == 0:
            best = t
        t += LANE
    return best if best is not None else dim


def _matmul(a, b, *, mode, out_dtype, name, add=None, after=None, a_parts=1, tm=1024, tn=1024, tk=1024):
    P = a_parts
    if mode == "nn":
        M, K = (a.shape[0], a.shape[1]) if P == 1 else (a.shape[1], a.shape[2] * P)
        N = b.shape[1]
    elif mode == "nt":
        assert P == 1
        M, K = a.shape
        N = b.shape[0]
    else:
        K = a.shape[0] if P == 1 else a.shape[1]
        M = a.shape[1] if P == 1 else a.shape[2] * P
        N = b.shape[1]
    tm = _tile(M // P if mode == "tn" else M, tm)
    tn = _tile(N, tn)
    tk = _tile(K // P if mode == "nn" else K, tk)
    nm, nn_, nk = M // tm, N // tn, K // tk

    if mode == "nn":
        if P == 1:
            a_spec = pl.BlockSpec((tm, tk), lambda i, j, k: (i, k))
        else:
            nkp = nk // P
            a_spec = pl.BlockSpec((None, tm, tk), lambda i, j, k: (k // nkp, i, k % nkp))
        b_spec = pl.BlockSpec((tk, tn), lambda i, j, k: (k, j))
        dims = (((1,), (0,)), ((), ()))
    elif mode == "nt":
        a_spec = pl.BlockSpec((tm, tk), lambda i, j, k: (i, k))
        b_spec = pl.BlockSpec((tn, tk), lambda i, j, k: (j, k))
        dims = (((1,), (1,)), ((), ()))
    else:
        if P == 1:
            a_spec = pl.BlockSpec((tk, tm), lambda i, j, k: (k, i))
        else:
            nmp = nm // P
            a_spec = pl.BlockSpec((None, tk, tm), lambda i, j, k: (i // nmp, k, i % nmp))
        b_spec = pl.BlockSpec((tk, tn), lambda i, j, k: (k, j))
        dims = (((0,), (0,)), ((), ()))
    o_spec = pl.BlockSpec((tm, tn), lambda i, j, k: (i, j))
    in_specs = [a_spec, b_spec]
    args = [a, b]
    if add is not None:
        in_specs.append(o_spec)
        args.append(add)
    has_add = add is not None
    if after is not None:
        in_specs.append(pl.BlockSpec(memory_space=pl.ANY))
        args.append(after)
    n_extra = len(args) - 2

    direct = nk > 1 and out_dtype == F32 and not has_add

    def body(a_ref, b_ref, *rest):
        add_ref = rest[0] if has_add else None
        o_ref = rest[n_extra]
        acc_ref = o_ref if direct or nk == 1 else rest[-1]
        k = pl.program_id(2)

        def part():
            return lax.dot_general(a_ref[...].astype(BF16), b_ref[...].astype(BF16), dims,
                                   preferred_element_type=F32)

        def finish(r):
            if has_add:
                r = r + add_ref[...].astype(F32)
            o_ref[...] = r.astype(out_dtype)

        if nk == 1:
            finish(part())
        else:
            @pl.when(k == 0)
            def _():
                acc_ref[...] = part()

            @pl.when(jnp.logical_and(k > 0, jnp.logical_or(k < nk - 1, direct)))
            def _():
                acc_ref[...] += part()

            if not direct:
                @pl.when(k == nk - 1)
                def _():
                    finish(acc_ref[...] + part())

    blk = (_nbytes((tm, tk), a.dtype) + _nbytes((tk, tn), b.dtype) + _nbytes((tm, tn), out_dtype)
           + (_nbytes((tm, tn), add.dtype) if has_add else 0)) * 2 + 3 * _nbytes((tm, tn), F32)
    return pl.pallas_call(
        body, name=name, grid=(nm, nn_, nk),
        in_specs=in_specs, out_specs=o_spec,
        out_shape=jax.ShapeDtypeStruct((M, N), out_dtype),
        scratch_shapes=[] if direct or nk == 1 else [pltpu.VMEM((tm, tn), F32)],
        compiler_params=pltpu.CompilerParams(
            dimension_semantics=("parallel", "parallel", "arbitrary"), vmem_limit_bytes=_vmem(blk)),
    )(*args)


def _rmsnorm_fwd(x, gains, name, tm=1024):
    T, D = x.shape
    n = gains.shape[0]
    tm = _tile(T, tm)

    def body(x_ref, g_ref, *o_refs):
        xv = x_ref[...]
        xhat = xv * lax.rsqrt(jnp.mean(xv * xv, axis=-1, keepdims=True) + RMS_EPS)
        for i in range(n):
            o_refs[i][...] = (xhat * g_ref[i:i + 1, :]).astype(BF16)

    row = pl.BlockSpec((tm, D), lambda i: (i, 0))
    outs = pl.pallas_call(
        body, name=name, grid=(T // tm,),
        in_specs=[row, pl.BlockSpec((n, D), lambda i: (0, 0))],
        out_specs=[row] * n,
        out_shape=[jax.ShapeDtypeStruct((T, D), BF16)] * n,
        compiler_params=pltpu.CompilerParams(
            dimension_semantics=("parallel",), vmem_limit_bytes=_vmem(4 * _nbytes((tm, D), F32))),
    )(x, gains)
    return tuple(outs)


def _rmsnorm_bwd(x, gains, dxns, dres, name, tm=1024):
    T, D = x.shape
    n = gains.shape[0]
    tm = _tile(T, tm)

    def body(x_ref, g_ref, *rest):
        dxn_refs = rest[:n]
        dres_ref, dx_ref, dxb_ref, dg_ref = rest[n:]
        xv = x_ref[...]
        rstd = lax.rsqrt(jnp.mean(xv * xv, axis=-1, keepdims=True) + RMS_EPS)
        xhat = xv * rstd
        dx = dres_ref[...]

        @pl.when(pl.program_id(0) == 0)
        def _():
            dg_ref[...] = jnp.zeros_like(dg_ref)

        for i in range(n):
            dy = dxn_refs[i][...].astype(F32)
            dg_ref[i:i + 1, :] += jnp.sum(dy * xhat, axis=0, keepdims=True)
            dxh = dy * g_ref[i:i + 1, :]
            dx = dx + rstd * (dxh - xhat * jnp.mean(dxh * xhat, axis=-1, keepdims=True))
        dx_ref[...] = dx
        dxb_ref[...] = dx.astype(BF16)

    row = pl.BlockSpec((tm, D), lambda i: (i, 0))
    par = pl.BlockSpec((n, D), lambda i: (0, 0))
    return pl.pallas_call(
        body, name=name, grid=(T // tm,),
        in_specs=[row, par] + [row] * n + [row],
        out_specs=[row, row, par],
        out_shape=[jax.ShapeDtypeStruct((T, D), F32), jax.ShapeDtypeStruct((T, D), BF16),
                   jax.ShapeDtypeStruct((n, D), F32)],
        compiler_params=pltpu.CompilerParams(
            dimension_semantics=("arbitrary",), vmem_limit_bytes=_vmem((4 + n) * _nbytes((tm, D), F32))),
    )(x, gains, *dxns, dres)


def _final_loss_bwd(h, gain, target, name, tm=1024):
    T, D = h.shape
    tm = _tile(T, tm)

    def body(h_ref, g_ref, t_ref, dh_ref, dhb_ref, dg_ref, sq_ref):
        xv = h_ref[...]
        rstd = lax.rsqrt(jnp.mean(xv * xv, axis=-1, keepdims=True) + RMS_EPS)
        xhat = xv * rstd
        err = xhat * g_ref[...] - t_ref[...]

        @pl.when(pl.program_id(0) == 0)
        def _():
            dg_ref[...] = jnp.zeros_like(dg_ref)
            sq_ref[...] = jnp.zeros_like(sq_ref)

        sq_ref[...] += jnp.sum(err * err, axis=0, keepdims=True)
        dy = err * (1.0 / D)
        dg_ref[...] += jnp.sum(dy * xhat, axis=0, keepdims=True)
        dxh = dy * g_ref[...]
        dh = rstd * (dxh - xhat * jnp.mean(dxh * xhat, axis=-1, keepdims=True))
        dh_ref[...] = dh
        dhb_ref[...] = dh.astype(BF16)

    row = pl.BlockSpec((tm, D), lambda i: (i, 0))
    par = pl.BlockSpec((1, D), lambda i: (0, 0))
    return pl.pallas_call(
        body, name=name, grid=(T // tm,),
        in_specs=[row, par, row], out_specs=[row, row, par, par],
        out_shape=[jax.ShapeDtypeStruct((T, D), F32), jax.ShapeDtypeStruct((T, D), BF16),
                   jax.ShapeDtypeStruct((1, D), F32), jax.ShapeDtypeStruct((1, D), F32)],
        compiler_params=pltpu.CompilerParams(
            dimension_semantics=("arbitrary",), vmem_limit_bytes=_vmem(5 * _nbytes((tm, D), F32))),
    )(h, gain, target)


def _halo_specs(S, ts, tc, col_off, prev, nxt):
    r = ts // HALO
    last = S // HALO - 1
    specs = []
    if prev:
        specs.append(pl.BlockSpec((None, HALO, tc), lambda j, b, s: (b, jnp.maximum(s * r - 1, 0), col_off + j)))
    specs.append(pl.BlockSpec((None, ts, tc), lambda j, b, s: (b, s, col_off + j)))
    if nxt:
        specs.append(pl.BlockSpec((None, HALO, tc), lambda j, b, s: (b, jnp.minimum((s + 1) * r, last), col_off + j)))
    return specs


def _ext(prev_ref, main_ref, next_ref, first, last):
    main = main_ref[...].astype(F32)
    zeros = jnp.zeros((HALO, main.shape[1]), F32)
    top = zeros if prev_ref is None else jnp.where(first, 0.0, prev_ref[...].astype(F32))
    bot = zeros if next_ref is None else jnp.where(last, 0.0, next_ref[...].astype(F32))
    return jnp.concatenate([top, main, bot], axis=0)


def _shift(xe, k):
    return pltpu.roll(xe, k % xe.shape[0], axis=0)


def _cconv(xe, w):
    return w[0:1, :] * _shift(xe, 2) + w[1:2, :] * _shift(xe, 1) + w[2:3, :] * xe


def _main(xe, ts):
    return xe[HALO:HALO + ts, :]


def _cconv_t(de, w):
    return w[2:3, :] * de + w[1:2, :] * _shift(de, -1) + w[0:1, :] * _shift(de, -2)


def _conv_wgrad(dw_ref, de, xe, ts):
    d = _main(de, ts)
    dw_ref[0:1, :] += jnp.sum(d * _main(_shift(xe, 2), ts), axis=0, keepdims=True)
    dw_ref[1:2, :] += jnp.sum(d * _main(_shift(xe, 1), ts), axis=0, keepdims=True)
    dw_ref[2:3, :] += jnp.sum(d * _main(xe, ts), axis=0, keepdims=True)


def _gate_tiles(S, C, rows=1024):
    return _tile(S, rows), _tile(C, 512)


def _shortconv_fwd(bch, conv_w, name):
    B, S, D3 = bch.shape
    D = D3 // 3
    ts, tc = _gate_tiles(S, D, rows=2048)
    nj, ns = D // tc, S // ts

    def body(b_ref, cp_ref, c_ref, hp_ref, h_ref, w_ref, o_ref):
        first = pl.program_id(2) == 0
        ce = _ext(cp_ref, c_ref, None, first, False)
        he = _ext(hp_ref, h_ref, None, first, False)
        cv = _main(_cconv(ce * he, w_ref[...]), ts)
        o_ref[...] = (b_ref[...].astype(F32) * cv).astype(BF16)

    in_specs = (_halo_specs(S, ts, tc, 0, False, False) + _halo_specs(S, ts, tc, nj, True, False)
                + _halo_specs(S, ts, tc, 2 * nj, True, False) + [pl.BlockSpec((3, tc), lambda j, b, s: (0, j))])
    return pl.pallas_call(
        body, name=name, grid=(nj, B, ns), in_specs=in_specs,
        out_specs=pl.BlockSpec((None, ts, tc), lambda j, b, s: (b, s, j)),
        out_shape=jax.ShapeDtypeStruct((B, S, D), BF16),
        compiler_params=pltpu.CompilerParams(
            dimension_semantics=("parallel", "parallel", "parallel"),
            vmem_limit_bytes=_vmem(12 * _nbytes((ts + 2 * HALO, tc), F32))),
    )(bch, bch, bch, bch, bch, conv_w)


def _shortconv_bwd(bch, dg, conv_w, name):
    B, S, D3 = bch.shape
    D = D3 // 3
    ts, tc = _gate_tiles(S, D)
    nj, ns = D // tc, S // ts

    def body(b_ref, bn_ref, cp_ref, c_ref, cn_ref, hp_ref, h_ref, hn_ref, d_ref, dn_ref, w_ref, o_ref, dw_ref):
        s = pl.program_id(2)
        first, last = s == 0, s == ns - 1

        @pl.when(jnp.logical_and(pl.program_id(1) == 0, s == 0))
        def _():
            dw_ref[...] = jnp.zeros_like(dw_ref)

        w = w_ref[...]
        be = _ext(None, b_ref, bn_ref, first, last)
        ce = _ext(cp_ref, c_ref, cn_ref, first, last)
        he = _ext(hp_ref, h_ref, hn_ref, first, last)
        de = _ext(None, d_ref, dn_ref, first, last)
        ch = ce * he
        dcv = de * be
        dch = _main(_cconv_t(dcv, w), ts)
        o_ref[0] = (_main(de, ts) * _main(_cconv(ch, w), ts)).astype(BF16)
        o_ref[1] = (dch * _main(he, ts)).astype(BF16)
        o_ref[2] = (dch * _main(ce, ts)).astype(BF16)
        _conv_wgrad(dw_ref, dcv, ch, ts)

    in_specs = (_halo_specs(S, ts, tc, 0, False, True) + _halo_specs(S, ts, tc, nj, True, True)
                + _halo_specs(S, ts, tc, 2 * nj, True, True) + _halo_specs(S, ts, tc, 0, False, True)
                + [pl.BlockSpec((3, tc), lambda j, b, s: (0, j))])
    return pl.pallas_call(
        body, name=name, grid=(nj, B, ns), in_specs=in_specs,
        out_specs=[pl.BlockSpec((3, None, ts, tc), lambda j, b, s: (0, b, s, j)),
                   pl.BlockSpec((3, tc), lambda j, b, s: (0, j))],
        out_shape=[jax.ShapeDtypeStruct((3, B, S, D), BF16), jax.ShapeDtypeStruct((3, D), F32)],
        compiler_params=pltpu.CompilerParams(
            dimension_semantics=("parallel", "arbitrary", "arbitrary"),
            vmem_limit_bytes=_vmem(24 * _nbytes((ts + 2 * HALO, tc), F32))),
    )(bch, bch, bch, bch, bch, bch, bch, bch, dg, dg, conv_w)


def _sigmoid(x):
    return 1.0 / (1.0 + jnp.exp(-x))


def _ffn_gate_fwd(u0, conv_w, conv_b, name):
    B, S, F2 = u0.shape
    F = F2 // 2
    ts, tc = _gate_tiles(S, F, rows=2048)
    nj, ns = F // tc, S // ts

    def body(gp_ref, g_ref, up_ref, u_ref, wg_ref, wu_ref, bg_ref, bu_ref, o_ref):
        first = pl.program_id(2) == 0
        ug = _main(_cconv(_ext(gp_ref, g_ref, None, first, False), wg_ref[...]), ts) + bg_ref[...]
        uu = _main(_cconv(_ext(up_ref, u_ref, None, first, False), wu_ref[...]), ts) + bu_ref[...]
        o_ref[...] = (ug * _sigmoid(ug) * uu).astype(BF16)

    in_specs = (_halo_specs(S, ts, tc, 0, True, False) + _halo_specs(S, ts, tc, nj, True, False)
                + [pl.BlockSpec((3, tc), lambda j, b, s: (0, j)), pl.BlockSpec((3, tc), lambda j, b, s: (0, nj + j)),
                   pl.BlockSpec((1, tc), lambda j, b, s: (0, j)), pl.BlockSpec((1, tc), lambda j, b, s: (0, nj + j))])
    return pl.pallas_call(
        body, name=name, grid=(nj, B, ns), in_specs=in_specs,
        out_specs=pl.BlockSpec((None, ts, tc), lambda j, b, s: (b, s, j)),
        out_shape=jax.ShapeDtypeStruct((B, S, F), BF16),
        compiler_params=pltpu.CompilerParams(
            dimension_semantics=("parallel", "parallel", "parallel"),
            vmem_limit_bytes=_vmem(12 * _nbytes((ts + 2 * HALO, tc), F32))),
    )(u0, u0, u0, u0, conv_w, conv_w, conv_b, conv_b)


def _ffn_gate_bwd(u0, dact, conv_w, conv_b, name):
    B, S, F2 = u0.shape
    F = F2 // 2
    ts, tc = _gate_tiles(S, F)
    nj, ns = F // tc, S // ts

    def body(gp_ref, g_ref, gn_ref, up_ref, u_ref, un_ref, d_ref, dn_ref, wg_ref, wu_ref, bg_ref, bu_ref,
             o_ref, dwg_ref, dwu_ref, dbg_ref, dbu_ref):
        s = pl.program_id(2)
        first, last = s == 0, s == ns - 1

        @pl.when(jnp.logical_and(pl.program_id(1) == 0, s == 0))
        def _():
            for r in (dwg_ref, dwu_ref, dbg_ref, dbu_ref):
                r[...] = jnp.zeros_like(r)

        wg, wu = wg_ref[...], wu_ref[...]
        ge = _ext(gp_ref, g_ref, gn_ref, first, last)
        ue = _ext(up_ref, u_ref, un_ref, first, last)
        de = _ext(None, d_ref, dn_ref, first, last)
        ug = _cconv(ge, wg) + bg_ref[...]
        uu = _cconv(ue, wu) + bu_ref[...]
        sig = _sigmoid(ug)
        dug = de * uu * (sig * (1.0 + ug * (1.0 - sig)))
        duu = de * (ug * sig)
        o_ref[0] = _main(_cconv_t(dug, wg), ts).astype(BF16)
        o_ref[1] = _main(_cconv_t(duu, wu), ts).astype(BF16)
        _conv_wgrad(dwg_ref, dug, ge, ts)
        _conv_wgrad(dwu_ref, duu, ue, ts)
        dbg_ref[...] += jnp.sum(_main(dug, ts), axis=0, keepdims=True)
        dbu_ref[...] += jnp.sum(_main(duu, ts), axis=0, keepdims=True)

    w3 = lambda off: pl.BlockSpec((3, tc), lambda j, b, s: (0, off + j))
    w1 = lambda off: pl.BlockSpec((1, tc), lambda j, b, s: (0, off + j))
    in_specs = (_halo_specs(S, ts, tc, 0, True, True) + _halo_specs(S, ts, tc, nj, True, True)
                + _halo_specs(S, ts, tc, 0, False, True) + [w3(0), w3(nj), w1(0), w1(nj)])
    outs = pl.pallas_call(
        body, name=name, grid=(nj, B, ns), in_specs=in_specs,
        out_specs=[pl.BlockSpec((2, None, ts, tc), lambda j, b, s: (0, b, s, j)), w3(0), w3(0), w1(0), w1(0)],
        out_shape=[jax.ShapeDtypeStruct((2, B, S, F), BF16), jax.ShapeDtypeStruct((3, F), F32),
                   jax.ShapeDtypeStruct((3, F), F32), jax.ShapeDtypeStruct((1, F), F32),
                   jax.ShapeDtypeStruct((1, F), F32)],
        compiler_params=pltpu.CompilerParams(
            dimension_semantics=("parallel", "arbitrary", "arbitrary"),
            vmem_limit_bytes=_vmem(30 * _nbytes((ts + 2 * HALO, tc), F32))),
    )(u0, u0, u0, u0, u0, u0, dact, dact, conv_w, conv_w, conv_b, conv_b)
    du0, dwg, dwu, dbg, dbu = outs
    return du0, jnp.concatenate([dwg, dwu], axis=1), jnp.concatenate([dbg, dbu], axis=1)


def _t5_bucket(dist):
    max_exact = REL_BUCKETS // 2
    n = jnp.maximum(dist, 0)
    nf = jnp.maximum(n, max_exact).astype(F32)
    large = max_exact + (jnp.log(nf / max_exact) / math.log(REL_MAX_DISTANCE / max_exact)
                         * (REL_BUCKETS - max_exact)).astype(jnp.int32)
    large = jnp.minimum(large, REL_BUCKETS - 1)
    return jnp.where(n < max_exact, n, large)


def _band_tables(window, dilation):
    P = ATT_BLOCK
    qi = jnp.arange(P, dtype=jnp.int32)[:, None]
    kc = jnp.arange(2 * P, dtype=jnp.int32)[None, :]
    delta = qi + P - kc
    band = (delta >= 0) & (delta <= window // dilation)
    bucket = _t5_bucket(delta * dilation).reshape(-1)
    onehot = (bucket[None, :] == jnp.arange(REL_BUCKETS, dtype=jnp.int32)[:, None]).astype(F32)
    return onehot, band


def _bias_lookup(rel_bias_t, onehots, name):
    nb, _, Q = onehots.shape
    H = rel_bias_t.shape[0]

    def body(r_ref, oh_ref, o_ref):
        o_ref[...] = lax.dot_general(r_ref[...], oh_ref[...], (((1,), (0,)), ((), ())),
                                     precision=lax.Precision.HIGHEST, preferred_element_type=F32)

    return pl.pallas_call(
        body, name=name, grid=(nb,),
        in_specs=[pl.BlockSpec((H, REL_BUCKETS), lambda i: (0, 0)),
                  pl.BlockSpec((None, REL_BUCKETS, Q), lambda i: (i, 0, 0))],
        out_specs=pl.BlockSpec((None, H, Q), lambda i: (i, 0, 0)),
        out_shape=jax.ShapeDtypeStruct((nb, H, Q), F32),
        compiler_params=pltpu.CompilerParams(dimension_semantics=("parallel",),
                                             vmem_limit_bytes=_vmem(_nbytes((REL_BUCKETS + H, Q), F32))),
    )(rel_bias_t, onehots)


def _bias_grad(dtabs, onehots, name):
    nb, H, Q = dtabs.shape

    def body(d_ref, oh_ref, o_ref):
        @pl.when(pl.program_id(0) == 0)
        def _():
            o_ref[...] = jnp.zeros_like(o_ref)

        o_ref[...] += lax.dot_general(d_ref[...], oh_ref[...], (((1,), (1,)), ((), ())),
                                      precision=lax.Precision.HIGHEST, preferred_element_type=F32)

    return pl.pallas_call(
        body, name=name, grid=(nb,),
        in_specs=[pl.BlockSpec((None, H, Q), lambda i: (i, 0, 0)),
                  pl.BlockSpec((None, REL_BUCKETS, Q), lambda i: (i, 0, 0))],
        out_specs=pl.BlockSpec((H, REL_BUCKETS), lambda i: (0, 0)),
        out_shape=jax.ShapeDtypeStruct((H, REL_BUCKETS), F32),
        compiler_params=pltpu.CompilerParams(dimension_semantics=("arbitrary",),
                                             vmem_limit_bytes=_vmem(_nbytes((REL_BUCKETS + H, Q), F32))),
    )(dtabs, onehots)


def _lane_masks():
    lane = lax.broadcasted_iota(jnp.int32, (1, LANE), 1)
    lo = lane < LANE // 2
    return lo, jnp.logical_not(lo)


def _dot_nt(a, b):
    return lax.dot_general(a, b, (((1,), (1,)), ((), ())), preferred_element_type=F32)


def _dot_nn(a, b):
    return lax.dot_general(a, b, (((1,), (0,)), ((), ())), preferred_element_type=F32)


def _dot_tn(a, b):
    return lax.dot_general(a, b, (((0,), (0,)), ((), ())), preferred_element_type=F32)


def _block_rows(n, dilation, S):
    P = ATT_BLOCK
    nb = S // (dilation * P)
    r, i = n // nb, n % nb
    cur = pl.ds(i * (P * dilation) + r, P, stride=dilation)
    prv = pl.ds(jnp.maximum(i - 1, 0) * (P * dilation) + r, P, stride=dilation)
    return cur, prv, jnp.minimum(i, 1)


def _attn_fwd(q, kv, tabs, name):
    B, S, D = q.shape
    P, H = ATT_BLOCK, N_HEADS
    scale = (D // H) ** -0.5
    half = LANE // 2
    nsteps = S // P // ATT_UNROLL

    def body(q_ref, k_ref, v_ref, tab_ref, o_ref, lse_ref, acc_ref, m_ref, l_ref):
        lo, hi = _lane_masks()
        for bi, (_, d) in enumerate(DILATED_BRANCHES):
            def block(n, carry, bi=bi, d=d):
                rows = [_block_rows(n + j * nsteps, d, S) for j in range(ATT_UNROLL)]
                loaded = []
                for cur, prv, variant in rows:
                    qb = q_ref[cur, :].astype(BF16)
                    kw = jnp.concatenate([k_ref[prv, :], k_ref[cur, :]], axis=0).astype(BF16)
                    vw = jnp.concatenate([v_ref[prv, :], v_ref[cur, :]], axis=0).astype(BF16)
                    loaded.append((qb, kw, vw))
                results = []
                for (cur, prv, variant), (qb, kw, vw) in zip(rows, loaded):
                    m_pair = l_pair = a_pair = None
                    for e, msk in enumerate((lo, hi)):
                        s = _dot_nt(jnp.where(msk, qb, 0), kw) * scale + tab_ref[bi, variant, e]
                        m_e = jnp.max(s, axis=-1, keepdims=True)
                        pe = jnp.exp(s - m_e)
                        l_e = jnp.sum(pe, axis=-1, keepdims=True)
                        a_e = _dot_nn(pe.astype(BF16), jnp.where(msk, vw, 0))
                        m_pair = m_e if e == 0 else jnp.where(lo, m_pair, m_e)
                        l_pair = l_e if e == 0 else jnp.where(lo, l_pair, l_e)
                        a_pair = a_e if e == 0 else a_pair + a_e
                    results.append((m_pair, l_pair, a_pair))
                for (cur, prv, variant), (m_pair, l_pair, a_pair) in zip(rows, results):
                    m_ref[bi, cur, :] = m_pair
                    l_ref[bi, cur, :] = l_pair
                    acc_ref[bi, cur, :] = a_pair
                return carry

            lax.fori_loop(0, nsteps, block, 0)

        nbr = len(DILATED_BRANCHES)
        chunk = 512

        def merge(t, carry):
            rs = pl.ds(pl.multiple_of(t * chunk, chunk), chunk)
            ms = [m_ref[b, rs, :] for b in range(nbr)]
            m = functools.reduce(jnp.maximum, ms)
            ws = [jnp.exp(mb - m) for mb in ms]
            l = functools.reduce(jnp.add, [w * l_ref[b, rs, :] for b, w in enumerate(ws)])
            acc = functools.reduce(jnp.add, [w * acc_ref[b, rs, :] for b, w in enumerate(ws)])
            o_ref[rs, :] = (acc / l).astype(BF16)
            lse_ref[rs, :] = m + jnp.log(l)
            return carry

        lax.fori_loop(0, S // chunk, merge, 0)

    nl = D // LANE
    col = lambda off: pl.BlockSpec((None, S, LANE), lambda b, p: (b, 0, off + p))
    tab_spec = pl.BlockSpec((3, 2, 2, P, 2 * P), lambda b, p: (0, 0, p, 0, 0))
    return pl.pallas_call(
        body, name=name, grid=(B, H // 2), in_specs=[col(0), col(0), col(nl), tab_spec],
        out_specs=[col(0), col(0)],
        out_shape=[jax.ShapeDtypeStruct((B, S, D), BF16), jax.ShapeDtypeStruct((B, S, D), F32)],
        scratch_shapes=[pltpu.VMEM((len(DILATED_BRANCHES), S, LANE), F32)] * 3,
        compiler_params=pltpu.CompilerParams(
            dimension_semantics=("parallel", "parallel"), vmem_limit_bytes=VMEM_CAP),
    )(q, kv, kv, tabs)


def _attn_bwd(q, kv, do, o, lse, tabs, name):
    B, S, D = q.shape
    P, H = ATT_BLOCK, N_HEADS
    scale = (D // H) ** -0.5
    half = LANE // 2
    nsteps = S // P // ATT_UNROLL

    def body(q_ref, k_ref, v_ref, do_ref, o_ref, lse_ref, tab_ref, dq_ref, dkv_ref, dtab_ref, delta_ref):
        lo, hi = _lane_masks()

        @pl.when(pl.program_id(1) == 0)
        def _():
            dtab_ref[...] = jnp.zeros_like(dtab_ref)

        dkv_ref[...] = jnp.zeros_like(dkv_ref)
        prod = do_ref[...] * o_ref[...].astype(F32)
        delta_ref[...] = jnp.where(lo, jnp.sum(jnp.where(lo, prod, 0.0), axis=-1, keepdims=True),
                                   jnp.sum(jnp.where(hi, prod, 0.0), axis=-1, keepdims=True))

        for bi, (_, d) in enumerate(DILATED_BRANCHES):
            def block(n, carry, bi=bi, d=d):
                rows = [_block_rows(n + j * nsteps, d, S) for j in range(ATT_UNROLL)]
                loaded = []
                for cur, prv, variant in rows:
                    qb = q_ref[cur, :].astype(BF16)
                    kw = jnp.concatenate([k_ref[prv, :], k_ref[cur, :]], axis=0).astype(BF16)
                    vw = jnp.concatenate([v_ref[prv, :], v_ref[cur, :]], axis=0).astype(BF16)
                    dob = do_ref[cur, :].astype(BF16)
                    dq_old = dq_ref[cur, :] if bi > 0 else None
                    loaded.append((qb, kw, vw, dob, lse_ref[cur, :], delta_ref[cur, :], dq_old))
                results = []
                for j, ((cur, prv, variant), (qb, kw, vw, dob, lse_b, dl_b, dq_old)) in enumerate(zip(rows, loaded)):
                    dq_pair = dk_pair = dv_pair = None
                    for e, msk in enumerate((lo, hi)):
                        qm = jnp.where(msk, qb, 0)
                        dom = jnp.where(msk, dob, 0)
                        s = _dot_nt(qm, kw) * scale + tab_ref[bi, variant, e]
                        pe = jnp.exp(s - lse_b[:, e * half:e * half + 1])
                        ds = pe * (_dot_nt(dom, vw) - dl_b[:, e * half:e * half + 1])
                        dtab_ref[bi, e] += ds
                        dsb = ds.astype(BF16)
                        dq_e = _dot_nn(dsb, jnp.where(msk, kw, 0))
                        dk_e = _dot_tn(dsb, qm)
                        dv_e = _dot_tn(pe.astype(BF16), dom)
                        dq_pair = dq_e if e == 0 else dq_pair + dq_e
                        dk_pair = dk_e if e == 0 else dk_pair + dk_e
                        dv_pair = dv_e if e == 0 else dv_pair + dv_e
                    dq_pair = dq_pair * scale
                    if bi > 0:
                        dq_pair = dq_pair + dq_old
                    results.append((dq_pair, dk_pair * scale, dv_pair))
                for (cur, prv, variant), (dq_pair, dk_pair, dv_pair) in zip(rows, results):
                    dq_ref[cur, :] = dq_pair
                    dkv_ref[0, cur, :] += dk_pair[P:, :]
                    dkv_ref[1, cur, :] += dv_pair[P:, :]
                    dkv_ref[0, prv, :] += dk_pair[:P, :]
                    dkv_ref[1, prv, :] += dv_pair[:P, :]
                return carry

            lax.fori_loop(0, nsteps, block, 0)

    nl = D // LANE
    col = lambda off: pl.BlockSpec((None, S, LANE), lambda p, b: (b, 0, off + p))
    tab_spec = pl.BlockSpec((3, 2, 2, P, 2 * P), lambda p, b: (0, 0, p, 0, 0))
    return pl.pallas_call(
        body, name=name, grid=(H // 2, B),
        in_specs=[col(0), col(0), col(nl), col(0), col(0), col(0), tab_spec],
        out_specs=[col(0), pl.BlockSpec((2, None, S, LANE), lambda p, b: (0, b, 0, p)),
                   pl.BlockSpec((None, 3, 2, P, 2 * P), lambda p, b: (p, 0, 0, 0, 0))],
        out_shape=[jax.ShapeDtypeStruct((B, S, D), F32), jax.ShapeDtypeStruct((2, B, S, D), F32),
                   jax.ShapeDtypeStruct((H // 2, 3, 2, P, 2 * P), F32)],
        scratch_shapes=[pltpu.VMEM((S, LANE), F32)],
        compiler_params=pltpu.CompilerParams(
            dimension_semantics=("parallel", "arbitrary"),
            vmem_limit_bytes=VMEM_CAP),
    )(q, kv, kv, do, o, lse, tabs)


def _attn_fwd_branch(q, kv, tab, dilation, name):
    B, S, D = q.shape
    P, d = ATT_BLOCK, dilation
    L = S // d
    nb = L // P
    H = N_HEADS
    scale = (D // H) ** -0.5
    qv = q.reshape(B, L, d * D)
    kvv = kv.reshape(B, L, d * 2 * D)

    def body(q_ref, kp_ref, kc_ref, vp_ref, vc_ref, tab_ref, o_ref, lse_ref):
        lo, hi = _lane_masks()
        for p in range(H // 2):
            sl = slice(p * LANE, (p + 1) * LANE)
            qp = q_ref[:, sl]
            kp = jnp.concatenate([kp_ref[:, sl], kc_ref[:, sl]], axis=0)
            vp = jnp.concatenate([vp_ref[:, sl], vc_ref[:, sl]], axis=0)
            o_pair = None
            lse_pair = None
            for e, msk in enumerate((lo, hi)):
                s = _dot_nt(jnp.where(msk, qp, 0), kp) * scale + tab_ref[2 * p + e]
                m = jnp.max(s, axis=-1, keepdims=True)
                pe = jnp.exp(s - m)
                den = jnp.sum(pe, axis=-1, keepdims=True)
                o_e = _dot_nn(pe.astype(BF16), jnp.where(msk, vp, 0)) / den
                lse_e = m + jnp.log(den)
                o_pair = o_e if e == 0 else o_pair + o_e
                lse_pair = lse_e if e == 0 else jnp.where(lo, lse_pair, lse_e)
            o_ref[:, sl] = o_pair
            lse_ref[:, sl] = lse_pair

    blk = lambda f: pl.BlockSpec((None, P, D), f)
    prev = lambda i: jnp.maximum(i - 1, 0)
    in_specs = [blk(lambda b, r, i: (b, i, r)),
                blk(lambda b, r, i: (b, prev(i), 2 * r)), blk(lambda b, r, i: (b, i, 2 * r)),
                blk(lambda b, r, i: (b, prev(i), 2 * r + 1)), blk(lambda b, r, i: (b, i, 2 * r + 1)),
                pl.BlockSpec((None, H, P, 2 * P), lambda b, r, i: (jnp.minimum(i, 1), 0, 0, 0))]
    o, lse = pl.pallas_call(
        body, name=name, grid=(B, d, nb), in_specs=in_specs,
        out_specs=[blk(lambda b, r, i: (b, i, r))] * 2,
        out_shape=[jax.ShapeDtypeStruct((B, L, d * D), F32)] * 2,
        compiler_params=pltpu.CompilerParams(
            dimension_semantics=("parallel", "parallel", "parallel"),
            vmem_limit_bytes=_vmem(_nbytes((H, P, 2 * P), F32) + 9 * _nbytes((P, D), F32))),
    )(qv, kvv, kvv, kvv, kvv, tab)
    return o.reshape(B * S, D), lse.reshape(B * S, D)


def _attn_merge(os_, lses, name, tm=512):
    T, D = os_[0].shape
    n = len(os_)
    tm = _tile(T, tm)

    def body(*refs):
        o_refs, l_refs = refs[:n], refs[n:2 * n]
        out_ref, lse_ref = refs[2 * n:]
        ls = [r[...] for r in l_refs]
        m = functools.reduce(jnp.maximum, ls)
        ws = [jnp.exp(l - m) for l in ls]
        tot = functools.reduce(jnp.add, ws)
        acc = functools.reduce(jnp.add, [w * r[...] for w, r in zip(ws, o_refs)])
        out_ref[...] = (acc / tot).astype(BF16)
        lse_ref[...] = m + jnp.log(tot)

    row = pl.BlockSpec((tm, D), lambda i: (i, 0))
    return pl.pallas_call(
        body, name=name, grid=(T // tm,), in_specs=[row] * (2 * n), out_specs=[row, row],
        out_shape=[jax.ShapeDtypeStruct((T, D), BF16), jax.ShapeDtypeStruct((T, D), F32)],
        compiler_params=pltpu.CompilerParams(
            dimension_semantics=("parallel",), vmem_limit_bytes=_vmem((2 * n + 4) * _nbytes((tm, D), F32))),
    )(*os_, *lses)


def _attn_bwd_q(q, kv, do, o, lse, tab, dq_acc, dilation, name):
    B, S, D = q.shape
    P, d = ATT_BLOCK, dilation
    L = S // d
    nb = L // P
    H = N_HEADS
    scale = (D // H) ** -0.5
    view = lambda t, c: t.reshape(B, L, d * c)
    has_acc = dq_acc is not None

    def body(q_ref, kp_ref, kc_ref, vp_ref, vc_ref, do_ref, o_ref, lse_ref, tab_ref, *rest):
        if has_acc:
            acc_ref, dq_ref, dtab_ref = rest
        else:
            dq_ref, dtab_ref = rest

        @pl.when((pl.program_id(0) == 0) & (pl.program_id(1) == 0) & (pl.program_id(2) == 0))
        def _():
            dtab_ref[...] = jnp.zeros_like(dtab_ref)

        lo, hi = _lane_masks()
        for p in range(H // 2):
            sl = slice(p * LANE, (p + 1) * LANE)
            qp = q_ref[:, sl]
            kp = jnp.concatenate([kp_ref[:, sl], kc_ref[:, sl]], axis=0)
            vp = jnp.concatenate([vp_ref[:, sl], vc_ref[:, sl]], axis=0)
            dop = do_ref[:, sl]
            prod = dop.astype(F32) * o_ref[:, sl].astype(F32)
            lsep = lse_ref[:, sl]
            dq_pair = None
            for e, msk in enumerate((lo, hi)):
                s = _dot_nt(jnp.where(msk, qp, 0), kp) * scale + tab_ref[2 * p + e]
                pe = jnp.exp(s - lsep[:, e * (LANE // 2):e * (LANE // 2) + 1])
                dp = _dot_nt(jnp.where(msk, dop, 0).astype(BF16), vp)
                delta = jnp.sum(jnp.where(msk, prod, 0.0), axis=-1, keepdims=True)
                ds = pe * (dp - delta)
                dtab_ref[2 * p + e] += ds
                dq_e = _dot_nn(ds.astype(BF16), jnp.where(msk, kp, 0))
                dq_pair = dq_e if e == 0 else dq_pair + dq_e
            dq_pair = dq_pair * scale
            if has_acc:
                dq_pair = dq_pair + acc_ref[:, sl]
            dq_ref[:, sl] = dq_pair

    blk = lambda f: pl.BlockSpec((None, P, D), f)
    prev = lambda i: jnp.maximum(i - 1, 0)
    own = blk(lambda b, r, i: (b, i, r))
    tab_spec = pl.BlockSpec((None, H, P, 2 * P), lambda b, r, i: (jnp.minimum(i, 1), 0, 0, 0))
    in_specs = [own,
                blk(lambda b, r, i: (b, prev(i), 2 * r)), blk(lambda b, r, i: (b, i, 2 * r)),
                blk(lambda b, r, i: (b, prev(i), 2 * r + 1)), blk(lambda b, r, i: (b, i, 2 * r + 1)),
                own, own, own, tab_spec]
    kvv = view(kv, 2 * D)
    args = [view(q, D), kvv, kvv, kvv, kvv, view(do, D), view(o, D), view(lse, D), tab]
    if has_acc:
        in_specs.append(own)
        args.append(view(dq_acc, D))
    dq, dtab = pl.pallas_call(
        body, name=name, grid=(B, d, nb), in_specs=in_specs,
        out_specs=[own, pl.BlockSpec((H, P, 2 * P), lambda b, r, i: (0, 0, 0))],
        out_shape=[jax.ShapeDtypeStruct((B, L, d * D), F32), jax.ShapeDtypeStruct((H, P, 2 * P), F32)],
        compiler_params=pltpu.CompilerParams(
            dimension_semantics=("arbitrary", "arbitrary", "arbitrary"),
            vmem_limit_bytes=_vmem(2 * _nbytes((H, P, 2 * P), F32) + 12 * _nbytes((P, D), F32))),
    )(*args)
    return dq.reshape(B, S, D), dtab


def _attn_bwd_kv(q, kv, do, o, lse, tabk, dkv_acc, dilation, name):
    B, S, D = q.shape
    P, d = ATT_BLOCK, dilation
    L = S // d
    nb = L // P
    H = N_HEADS
    scale = (D // H) ** -0.5
    view = lambda t, c: t.reshape(B, L, d * c)
    has_acc = dkv_acc is not None

    def body(k_ref, v_ref, qa_ref, qb_ref, doa_ref, dob_ref, oa_ref, ob_ref, la_ref, lb_ref, tab_ref, *rest):
        if has_acc:
            acc_ref, dkv_ref = rest
        else:
            (dkv_ref,) = rest
        lo, hi = _lane_masks()
        for p in range(H // 2):
            sl = slice(p * LANE, (p + 1) * LANE)
            kp, vp = k_ref[:, sl], v_ref[:, sl]
            q2 = jnp.concatenate([qa_ref[:, sl], qb_ref[:, sl]], axis=0)
            do2 = jnp.concatenate([doa_ref[:, sl], dob_ref[:, sl]], axis=0)
            o2 = jnp.concatenate([oa_ref[:, sl], ob_ref[:, sl]], axis=0)
            lse2 = jnp.concatenate([la_ref[:, sl], lb_ref[:, sl]], axis=0)
            prod = do2.astype(F32) * o2.astype(F32)
            dk_pair = None
            dv_pair = None
            for e, msk in enumerate((lo, hi)):
                qm = jnp.where(msk, q2, 0)
                dom = jnp.where(msk, do2, 0).astype(BF16)
                s = _dot_nt(qm, kp) * scale + tab_ref[2 * p + e]
                pe = jnp.exp(s - lse2[:, e * (LANE // 2):e * (LANE // 2) + 1])
                dp = _dot_nt(dom, vp)
                delta = jnp.sum(jnp.where(msk, prod, 0.0), axis=-1, keepdims=True)
                ds = pe * (dp - delta)
                dv_e = _dot_tn(pe.astype(BF16), dom)
                dk_e = _dot_tn(ds.astype(BF16), qm)
                dk_pair = dk_e if e == 0 else dk_pair + dk_e
                dv_pair = dv_e if e == 0 else dv_pair + dv_e
            dk_pair = dk_pair * scale
            if has_acc:
                dk_pair = dk_pair + acc_ref[0, :, sl]
                dv_pair = dv_pair + acc_ref[1, :, sl]
            dkv_ref[0, :, sl] = dk_pair
            dkv_ref[1, :, sl] = dv_pair

    blk = lambda f: pl.BlockSpec((None, P, D), f)
    nxt = lambda i: jnp.minimum(i + 1, nb - 1)
    own = blk(lambda b, r, i: (b, i, r))
    nx = blk(lambda b, r, i: (b, nxt(i), r))
    tab_spec = pl.BlockSpec((None, H, 2 * P, P), lambda b, r, i: (jnp.where(i == nb - 1, 1, 0), 0, 0, 0))
    pair = pl.BlockSpec((2, None, P, D), lambda b, r, i: (0, b, i, r))
    in_specs = [blk(lambda b, r, i: (b, i, 2 * r)), blk(lambda b, r, i: (b, i, 2 * r + 1)),
                own, nx, own, nx, own, nx, own, nx, tab_spec]
    kvv, qv, dov, ov, lv = view(kv, 2 * D), view(q, D), view(do, D), view(o, D), view(lse, D)
    args = [kvv, kvv, qv, qv, dov, dov, ov, ov, lv, lv, tabk]
    if has_acc:
        in_specs.append(pair)
        args.append(dkv_acc.reshape(2, B, L, d * D))
    dkv = pl.pallas_call(
        body, name=name, grid=(B, d, nb), in_specs=in_specs, out_specs=pair,
        out_shape=jax.ShapeDtypeStruct((2, B, L, d * D), F32),
        compiler_params=pltpu.CompilerParams(
            dimension_semantics=("parallel", "parallel", "parallel"),
            vmem_limit_bytes=_vmem(_nbytes((H, P, 2 * P), F32) + 16 * _nbytes((P, D), F32))),
    )(*args)
    return dkv.reshape(2, B, S, D)


def _mesh_place():
    x, y, c = lax.axis_index("x"), lax.axis_index("y"), lax.axis_index("c")
    return x, y, c, (x, y, 1 - c), [(1 - x, y), (x, 1 - y), (1 - x, 1 - y)]


def _rdma(src, dst, send_sems, recv_sems, idx, to):
    return pltpu.make_async_remote_copy(
        src_ref=src, dst_ref=dst, send_sem=send_sems.at[idx], recv_sem=recv_sems.at[idx],
        device_id=to, device_id_type=pl.DeviceIdType.MESH)


def _comm_call(body, name, args, out_shape, n_sems, n_local):
    any_spec = pl.BlockSpec(memory_space=pl.ANY)
    return pl.pallas_call(
        body, name=name, in_specs=[any_spec] * len(args), out_specs=[any_spec] * len(out_shape),
        out_shape=out_shape,
        scratch_shapes=[pltpu.SemaphoreType.DMA(n_sems), pltpu.SemaphoreType.DMA(n_sems),
                        pltpu.SemaphoreType.DMA((n_local,))],
        compiler_params=pltpu.CompilerParams(has_side_effects=True),
    )(*args)


def _all_gather(arrays, name):
    n = len(arrays)

    def body(*refs):
        ins, outs = refs[:n], refs[n:2 * n]
        send_sems, recv_sems, loc_sems = refs[2 * n:]
        x, y, c, sib, chips = _mesh_place()
        me = (x, y, c)

        def rows(a, dev):
            return outs[a].at[4 * dev[0] + 2 * dev[1] + dev[2]]

        def copy(a, k, block, to, src=None):
            return _rdma(rows(a, block) if src is None else src, rows(a, block), send_sems, recv_sems, (a, k), to)

        local, first, passed = [], [], []
        for a in range(n):
            cp = pltpu.make_async_copy(ins[a], rows(a, me), loc_sems.at[a])
            cp.start()
            local.append(cp)
            first.append(copy(a, 0, me, sib, src=ins[a]))
            first += [copy(a, 1 + j, me, (*chip, c), src=ins[a]) for j, chip in enumerate(chips)]
        for cp in first:
            cp.start()
        for j, chip in enumerate(chips):
            for a in range(n):
                copy(a, 1 + j, (*chip, c), me).wait_recv()
                cp = copy(a, 4 + j, (*chip, c), sib)
                cp.start()
                passed.append(cp)
        for a in range(n):
            copy(a, 0, sib, me).wait_recv()
            for j, chip in enumerate(chips):
                copy(a, 4 + j, (*chip, 1 - c), me).wait_recv()
        for cp in first + passed:
            cp.wait_send()
        for cp in local:
            cp.wait()

    out_shape = [jax.ShapeDtypeStruct((N_DEV,) + g.shape, g.dtype) for g in arrays]
    return list(_comm_call(body, name, arrays, out_shape, (n, N_DEV - 1), n))


def _core_pair_exchange(halves, gathered, name):
    nh, ng = len(halves), len(gathered)

    def body(*refs):
        h_in, g_in = refs[:nh], refs[nh:nh + ng]
        outs = refs[nh + ng:nh + ng + nh + ng]
        theirs, g_out = outs[:nh], outs[nh:]
        send_sems, recv_sems, loc_sems = refs[nh + ng + nh + ng:]
        x, y, c, sib, chips = _mesh_place()
        me = 4 * x + 2 * y + c
        peers = [sib] + [(*chip, pc) for chip in chips for pc in (c, 1 - c)]
        nchip = N_DEV // 2
        gbase = nh * nchip
        local, sends = [], []
        for a in range(nh):
            for ch in range(nchip):
                cp = _rdma(h_in[a].at[ch, 1 - c], theirs[a].at[ch], send_sems, recv_sems, a * nchip + ch, sib)
                cp.start()
                sends.append(cp)
        for g in range(ng):
            cp = pltpu.make_async_copy(g_in[g], g_out[g].at[me], loc_sems.at[g])
            cp.start()
            local.append(cp)
            for k, peer in enumerate(peers):
                cp = _rdma(g_in[g], g_out[g].at[me], send_sems, recv_sems, gbase + g * (N_DEV - 1) + k, peer)
                cp.start()
                sends.append(cp)
        for cp in sends:
            cp.wait_send()
        for a in range(nh):
            for ch in range(nchip):
                _rdma(h_in[a].at[ch, 1 - c], theirs[a].at[ch], send_sems, recv_sems, a * nchip + ch, sib).wait_recv()
        for g in range(ng):
            for k, peer in enumerate(peers):
                pid = 4 * peer[0] + 2 * peer[1] + peer[2]
                _rdma(g_in[g], g_out[g].at[pid], send_sems, recv_sems, gbase + g * (N_DEV - 1) + k, peer).wait_recv()
        for cp in local:
            cp.wait()

    half = [jax.ShapeDtypeStruct((h.shape[0],) + h.shape[2:], h.dtype) for h in halves]
    out_shape = half + [jax.ShapeDtypeStruct((N_DEV,) + g.shape, g.dtype) for g in gathered]
    outs = _comm_call(body, name, list(halves) + list(gathered), out_shape,
                      (nh * (N_DEV // 2) + ng * (N_DEV - 1),), max(ng, 1))
    return list(outs[:nh]), list(outs[nh:])


def _chip_exchange(arrays, name):
    n = len(arrays)

    def body(*refs):
        ins, outs = refs[:n], refs[n:2 * n]
        send_sems, recv_sems, _ = refs[2 * n:]
        x, y, c, sib, chips = _mesh_place()
        sends = []
        for a in range(n):
            for j, chip in enumerate(chips):
                cp = _rdma(ins[a].at[2 * chip[0] + chip[1]], outs[a].at[j], send_sems, recv_sems, (a, j), (*chip, c))
                cp.start()
                sends.append(cp)
        for cp in sends:
            cp.wait_send()
        for a in range(n):
            for j, chip in enumerate(chips):
                _rdma(ins[a].at[j], outs[a].at[j], send_sems, recv_sems, (a, j), (*chip, c)).wait_recv()

    out_shape = [jax.ShapeDtypeStruct((3,) + g.shape[1:], g.dtype) for g in arrays]
    return list(_comm_call(body, name, arrays, out_shape, (n, 3), 1))


def _gather_copies(src_refs, land_refs, send_sems, recv_sems):
    x, y, c, sib, chips = _mesh_place()
    me = 4 * x + 2 * y + c
    peers = [sib] + [(*chip, c) for chip in chips]
    return [pltpu.make_async_remote_copy(
                src_ref=src, dst_ref=land.at[me], send_sem=send_sems[4 * a + k], recv_sem=recv_sems[4 * a + k],
                device_id=peer, device_id_type=pl.DeviceIdType.MESH)
            for a, (src, land) in enumerate(zip(src_refs, land_refs)) for k, peer in enumerate(peers)]


def _forward_to_sibling(gathered, name):
    n = len(gathered)

    def body(*refs):
        outs = refs[n:2 * n]
        send_sems, recv_sems, _ = refs[2 * n:]
        x, y, c, sib, chips = _mesh_place()
        sends = []
        for a in range(n):
            for j, chip in enumerate(chips):
                rows = outs[a].at[4 * chip[0] + 2 * chip[1] + c]
                cp = _rdma(rows, rows, send_sems, recv_sems, (a, j), sib)
                cp.start()
                sends.append(cp)
        for cp in sends:
            cp.wait_send()
        for a in range(n):
            for j, chip in enumerate(chips):
                rows = outs[a].at[4 * chip[0] + 2 * chip[1] + (1 - c)]
                _rdma(rows, rows, send_sems, recv_sems, (a, j), sib).wait_recv()

    any_spec = pl.BlockSpec(memory_space=pl.ANY)
    return list(pl.pallas_call(
        body, name=name, in_specs=[any_spec] * n, out_specs=[any_spec] * n,
        out_shape=[jax.ShapeDtypeStruct(g.shape, g.dtype) for g in gathered],
        input_output_aliases={i: i for i in range(n)},
        scratch_shapes=[pltpu.SemaphoreType.DMA((n, 3)), pltpu.SemaphoreType.DMA((n, 3)),
                        pltpu.SemaphoreType.DMA((1,))],
        compiler_params=pltpu.CompilerParams(has_side_effects=True),
    )(*gathered))


def _split_start(copies_fn, srcs, land_shapes, n_sems, name, after=None):
    ns, nl = len(srcs), len(land_shapes)
    nc = n_sems
    hbm = pl.BlockSpec(memory_space=pltpu.HBM)
    sem = pl.BlockSpec(memory_space=pltpu.SEMAPHORE)

    n_in = ns + nl + (0 if after is None else 1)

    def body(*refs):
        src_refs, land_refs = refs[:ns], refs[ns:ns + nl]
        send_sems, recv_sems = refs[n_in:n_in + nc], refs[n_in + nc:n_in + 2 * nc]
        token = refs[-1]
        for cp in copies_fn(src_refs, land_refs, send_sems, recv_sems):
            cp.start()
        token[...] = jnp.zeros_like(token)

    lands = [lax.empty(s.shape, s.dtype) for s in land_shapes]
    thru = [pltpu.HBM(t.shape, t.dtype) for t in list(srcs) + lands]
    order = [] if after is None else [after]
    outs = pl.pallas_call(
        body, name=name,
        out_shape=(*([pltpu.SemaphoreType.DMA(())] * (2 * nc)), *thru, jax.ShapeDtypeStruct((8, LANE), F32)),
        in_specs=[hbm] * (ns + nl) + [pl.BlockSpec(memory_space=pl.ANY)] * len(order),
        out_specs=(*([sem] * (2 * nc)), *([hbm] * (ns + nl)), pl.BlockSpec(memory_space=pltpu.VMEM)),
        input_output_aliases={i: 2 * nc + i for i in range(ns + nl)},
        compiler_params=pltpu.CompilerParams(has_side_effects=pltpu.SideEffectType.DATAFLOW_SIDE_EFFECTING),
    )(*[pltpu.with_memory_space_constraint(t, pltpu.HBM) for t in list(srcs) + lands], *order)
    thru_out = outs[2 * nc:2 * nc + ns + nl]
    return list(outs[:nc]), list(outs[nc:2 * nc]), list(thru_out[:ns]), list(thru_out[ns:]), outs[-1]


def _split_wait(copies_fn, send_sems, recv_sems, srcs, lands, after, name):
    ns, nl, nc = len(srcs), len(lands), len(send_sems)
    hbm = pl.BlockSpec(memory_space=pltpu.HBM)
    sem = pl.BlockSpec(memory_space=pltpu.SEMAPHORE)

    def body(*refs):
        src_refs, land_refs = refs[:ns], refs[ns:ns + nl]
        send_refs, recv_refs = refs[ns + nl:ns + nl + nc], refs[ns + nl + nc:ns + nl + 2 * nc]
        for cp in copies_fn(src_refs, land_refs, send_refs, recv_refs):
            cp.wait_send()
            cp.wait_recv()

    outs = pl.pallas_call(
        body, name=name,
        out_shape=[pltpu.HBM(t.shape, t.dtype) for t in list(srcs) + list(lands)],
        in_specs=[hbm] * (ns + nl) + [sem] * (2 * nc) + [pl.BlockSpec(memory_space=pl.ANY)],
        out_specs=[hbm] * (ns + nl),
        input_output_aliases={i: i for i in range(ns + nl)},
        compiler_params=pltpu.CompilerParams(has_side_effects=pltpu.SideEffectType.DATAFLOW_SIDE_EFFECTING),
    )(*srcs, *lands, *send_sems, *recv_sems, after)
    return list(outs[:ns]), list(outs[ns:])


def _chip_copies(src_refs, land_refs, send_sems, recv_sems):
    x, y, c, sib, chips = _mesh_place()
    return [pltpu.make_async_remote_copy(
                src_ref=src.at[2 * chip[0] + chip[1]], dst_ref=land.at[j], send_sem=send_sems[3 * a + j],
                recv_sem=recv_sems[3 * a + j], device_id=(*chip, c), device_id_type=pl.DeviceIdType.MESH)
            for a, (src, land) in enumerate(zip(src_refs, land_refs)) for j, chip in enumerate(chips)]


def _pair_sum(halves, theirs, core, name, tr=256):
    nchip, _, R, C = halves.shape
    tr = _row_tile(R, tr)

    def body(core_ref, h_ref, t_ref, o_ref):
        o_ref[...] = (h_ref[...] + t_ref[...]).astype(BF16)

    grid_spec = pltpu.PrefetchScalarGridSpec(
        num_scalar_prefetch=1, grid=(nchip, R // tr),
        in_specs=[pl.BlockSpec((None, None, tr, C), lambda ch, i, core_ref: (ch, core_ref[0], i, 0)),
                  pl.BlockSpec((None, tr, C), lambda ch, i, core_ref: (ch, i, 0))],
        out_specs=pl.BlockSpec((None, tr, C), lambda ch, i, core_ref: (ch, i, 0)))
    return pl.pallas_call(
        body, name=name, grid_spec=grid_spec, out_shape=jax.ShapeDtypeStruct((nchip, R, C), BF16),
        compiler_params=pltpu.CompilerParams(
            dimension_semantics=("parallel", "parallel"), vmem_limit_bytes=_vmem(3 * _nbytes((tr, C), F32))),
    )(core, halves, theirs)


def _chip_sum(own, recv, chip, name, tr=256):
    _, R, C = own.shape
    tr = _row_tile(R, tr)

    def body(chip_ref, o_ref, r_ref, out_ref):
        acc = o_ref[...].astype(F32)
        for j in range(3):
            acc = acc + r_ref[j].astype(F32)
        out_ref[...] = acc

    grid_spec = pltpu.PrefetchScalarGridSpec(
        num_scalar_prefetch=1, grid=(R // tr,),
        in_specs=[pl.BlockSpec((None, tr, C), lambda i, chip_ref: (chip_ref[0], i, 0)),
                  pl.BlockSpec((3, tr, C), lambda i, chip_ref: (0, i, 0))],
        out_specs=pl.BlockSpec((tr, C), lambda i, chip_ref: (i, 0)))
    return pl.pallas_call(
        body, name=name, grid_spec=grid_spec, out_shape=jax.ShapeDtypeStruct((R, C), F32),
        compiler_params=pltpu.CompilerParams(
            dimension_semantics=("parallel",), vmem_limit_bytes=_vmem(4 * _nbytes((tr, C), F32))),
    )(chip, own, recv)


def _row_tile(rows, target):
    best = rows
    for t in range(8, min(rows, target) + 1, 8):
        if rows % t == 0:
            best = t
    return best


def _sum_slots(recv, name, tr=128):
    n, R, C = recv.shape
    tr = _row_tile(R, tr if recv.dtype == F32 else 2 * tr)

    def body(r_ref, o_ref):
        acc = r_ref[0].astype(F32)
        for k in range(1, n):
            acc = acc + r_ref[k].astype(F32)
        o_ref[...] = acc

    return pl.pallas_call(
        body, name=name, grid=(R // tr,),
        in_specs=[pl.BlockSpec((n, tr, C), lambda i: (0, i, 0))],
        out_specs=pl.BlockSpec((tr, C), lambda i: (i, 0)),
        out_shape=jax.ShapeDtypeStruct((R, C), F32),
        compiler_params=pltpu.CompilerParams(
            dimension_semantics=("parallel",), vmem_limit_bytes=_vmem(10 * _nbytes((tr, C), F32))),
    )(recv)


def _adamw_math(g_ref, w_ref, m_ref, v_ref, d_ref, nm_ref, nv_ref):
    gv = g_ref[...]
    nm = ADAM_B1 * m_ref[...] + (1.0 - ADAM_B1) * gv
    nv = ADAM_B2 * v_ref[...] + (1.0 - ADAM_B2) * (gv * gv)
    m_hat = nm / (1.0 - ADAM_B1 ** ADAM_STEP)
    v_hat = nv / (1.0 - ADAM_B2 ** ADAM_STEP)
    d_ref[...] = -ADAM_LR * (m_hat / (jnp.sqrt(v_hat) + ADAM_EPS) + ADAM_WD * w_ref[...])
    nm_ref[...] = nm
    nv_ref[...] = nv


def _adamw_small(params, name):
    n = len(params)
    two_d = [(-1, w.shape[-1]) for w, _, _, _ in params]
    flat = [t.reshape(two_d[i]) for i, (w, m, v, g) in enumerate(params) for t in (g, w, m, v)]

    def body(*refs):
        ins, outs = refs[:4 * n], refs[4 * n:]
        for i in range(n):
            _adamw_math(*ins[4 * i:4 * i + 4], *outs[3 * i:3 * i + 3])

    outs = pl.pallas_call(
        body, name=name,
        out_shape=[jax.ShapeDtypeStruct(flat[4 * i].shape, F32) for i in range(n) for _ in range(3)],
    )(*flat)
    return [(g.reshape(w.shape),) + tuple(outs[3 * i + t].reshape(w.shape) for t in range(3))
            for i, (w, m, v, g) in enumerate(params)]


def _adamw(g, w, m, v, name, tr=256):
    R, C = g.shape
    tr = _row_tile(R, tr)
    body = functools.partial(_adamw_math)

    row = pl.BlockSpec((tr, C), lambda i: (i, 0))
    return pl.pallas_call(
        body, name=name, grid=(R // tr,), in_specs=[row] * 4, out_specs=[row] * 3,
        out_shape=[jax.ShapeDtypeStruct((R, C), F32)] * 3,
        compiler_params=pltpu.CompilerParams(
            dimension_semantics=("parallel",), vmem_limit_bytes=_vmem(8 * _nbytes((tr, C), F32))),
    )(g, w, m, v)


def _pack_rows(parts):
    flat = jnp.concatenate([p.reshape(-1).astype(F32) for p in parts])
    rows = -(-flat.shape[0] // (8 * LANE)) * 8
    return jnp.pad(flat, (0, rows * LANE - flat.shape[0])).reshape(rows, LANE)


def _unpack(flat, shapes):
    out, off = [], 0
    for shp in shapes:
        n = math.prod(shp)
        out.append(flat[off:off + n].reshape(shp))
        off += n
    return out


def kernel(x, a_norm, a_w_in, a_conv, a_w_out, kv_norm, w_kv, b_norm, b_w_q, b_w_o, rel_bias, ffn_norm, ffn_w_up, ffn_conv, ffn_conv_b, ffn_w_down, final_norm, loss_target, m_a_norm, m_a_w_in, m_a_conv, m_a_w_out, m_kv_norm, m_w_kv, m_b_norm, m_b_w_q, m_b_w_o, m_rel_bias, m_ffn_norm, m_ffn_w_up, m_ffn_conv, m_ffn_conv_b, m_ffn_w_down, m_final_norm, v_a_norm, v_a_w_in, v_a_conv, v_a_w_out, v_kv_norm, v_w_kv, v_b_norm, v_b_w_q, v_b_w_o, v_rel_bias, v_ffn_norm, v_ffn_w_up, v_ffn_conv, v_ffn_conv_b, v_ffn_w_down, v_final_norm):
    B, S, D = x.shape
    T = B * S
    F = ffn_w_down.shape[1] * N_DEV
    me = 4 * lax.axis_index("x") + 2 * lax.axis_index("y") + lax.axis_index("c")

    big_shards = [a_w_in[0].T, a_w_out[0], w_kv.T, b_w_q[0], b_w_o[0],
                  ffn_w_up[0].T, ffn_w_up[1].T, ffn_w_down[0], ffn_w_down[1]]
    small_shapes = [a_norm.shape, a_conv.shape, ffn_conv.shape]
    small_pack = _pack_rows([a_norm, a_conv, ffn_conv])
    shards = [s.astype(BF16) for s in big_shards]
    first = _all_gather([shards[0], shards[1], small_pack], "gather_weights_first")
    win_t, wout = first[0].reshape(-1, D), first[1].reshape(-1, D)
    smalls = [_unpack(first[2][j].reshape(-1), small_shapes) for j in range(N_DEV)]
    a_norm_f = jnp.concatenate([s[0] for s in smalls], axis=-1)
    a_conv_f = jnp.concatenate([s[1] for s in smalls], axis=-1)[0]
    ffn_conv_f = jnp.concatenate([s[2] for s in smalls], axis=-1)

    def gather_start(tag, idxs, after):
        srcs = [shards[i] for i in idxs]
        lands = [jax.ShapeDtypeStruct((N_DEV,) + s.shape, s.dtype) for s in srcs]
        return _split_start(_gather_copies, srcs, lands, 4 * len(srcs), f"gather_start_{tag}", after=after)

    def gather_finish(tag, handle, after):
        send, recv, srcs, lands, _ = handle
        srcs, lands = _split_wait(_gather_copies, send, recv, srcs, lands, after, f"gather_wait_{tag}")
        lands = _forward_to_sibling(lands, f"gather_forward_{tag}")
        return [lax.dynamic_update_slice(g, s[None], (me, 0, 0)).reshape(-1, D) for g, s in zip(lands, srcs)]

    ffn0_w = gather_start("ffn0", [5, 7], first[0])
    rest_w = gather_start("rest", [2, 3, 4, 6, 8], ffn0_w[4])

    x2 = x.reshape(T, D)
    (xn0,) = _rmsnorm_fwd(x2, a_norm_f, "a_norm_fwd")
    bch = _matmul(xn0, win_t, mode="nt", out_dtype=BF16, after=rest_w[4], name="a_in_proj").reshape(B, S, 3 * D)
    gated = _shortconv_fwd(bch, a_conv_f, "a_gate_fwd").reshape(T, D)
    h1 = _matmul(gated, wout, mode="nn", out_dtype=F32, add=x2, name="a_out_proj")
    wup0_t, wdn0 = gather_finish("ffn0", ffn0_w, h1)
    wup_t, wdn = [wup0_t, None], [wdn0, None]

    def ffn_fwd(h, l):
        (xn,) = _rmsnorm_fwd(h, ffn_norm[l:l + 1], f"ffn{l}_norm_fwd")
        u0 = _matmul(xn, wup_t[l], mode="nt", out_dtype=BF16, name=f"ffn{l}_up", tn=1408).reshape(B, S, 2 * F)
        act = _ffn_gate_fwd(u0, ffn_conv_f[l], ffn_conv_b[l:l + 1], f"ffn{l}_gate_fwd").reshape(T, F)
        out = _matmul(act, wdn[l], mode="nn", out_dtype=F32, add=h, name=f"ffn{l}_down", tn=512, tk=2816)
        return out, (xn, u0, act)

    h2, ffn0_saved = ffn_fwd(h1, 0)
    wkv_t, wq, wo, wup_t[1], wdn[1] = gather_finish("rest", rest_w, h2)
    kvn, xnb = _rmsnorm_fwd(h2, jnp.stack([kv_norm, b_norm[0]]), "kv_b_norm_fwd")
    kv = _matmul(kvn, wkv_t, mode="nt", out_dtype=F32, name="kv_proj").reshape(B, S, 2 * D)
    q = _matmul(xnb, wq, mode="nn", out_dtype=F32, name="q_proj").reshape(B, S, D)

    tables = [_band_tables(w, d) for (w, d) in DILATED_BRANCHES]
    onehots = jnp.stack([t[0] for t in tables])
    P = ATT_BLOCK
    bias_vals = _bias_lookup(rel_bias.T, onehots, "rel_bias_lookup").reshape(3, N_HEADS, P, 2 * P)
    in_cur = (jnp.arange(2 * P) >= P)[None, :]
    tabs = []
    for bi in range(3):
        band = tables[bi][1]
        gen = jnp.where(band[None], bias_vals[bi], NEG)
        fst = jnp.where((band & in_cur)[None], bias_vals[bi], NEG)
        tabs.append(jnp.stack([fst, gen]))
    tabs = jnp.stack(tabs)

    attn3, lse3 = _attn_fwd(q, kv, tabs, "attn_fwd")
    attn = attn3.reshape(T, D)
    h3 = _matmul(attn, wo, mode="nn", out_dtype=F32, add=h2, name="o_proj")
    h4, ffn1_saved = ffn_fwd(h3, 1)

    dh4, dh4b, d_final_norm, sq = _final_loss_bwd(h4, final_norm.reshape(1, D), loss_target.reshape(T, D), "loss_bwd")
    loss_part = 0.5 * jnp.sum(sq) / D

    core_idx = lax.axis_index("c").astype(jnp.int32).reshape(1)
    chip_idx = (2 * lax.axis_index("x") + lax.axis_index("y")).astype(jnp.int32).reshape(1)
    in_flight = []

    def send_group(tag, indexed_grads, small=()):
        halves = [g.reshape(N_DEV // 2, 2, g.shape[0] // N_DEV, D) for _, g in indexed_grads]
        theirs, gath = _core_pair_exchange(halves, list(small), f"exchange_core_{tag}")
        sums = [_pair_sum(h_, t_, core_idx, f"grad_core_sum_{tag}{k}")
                for k, (h_, t_) in enumerate(zip(halves, theirs))]
        lands = [jax.ShapeDtypeStruct((3,) + s.shape[1:], s.dtype) for s in sums]
        send, recv, srcs, lnd, token = _split_start(_chip_copies, sums, lands, 3 * len(sums),
                                                    f"exchange_chip_start_{tag}")
        in_flight.append((tag, [i for i, _ in indexed_grads], send, recv, srcs, lnd))
        return token, gath

    def ffn_bwd(dh_out, dh_out_b, h_in, saved, l, after):
        xn, u0, act = saved
        dact = _matmul(dh_out_b, wdn[l], mode="nt", out_dtype=BF16, after=after, name=f"ffn{l}_down_dx",
                       tn=1408).reshape(B, S, F)
        d_wdn = _matmul(act, dh_out_b, mode="tn", out_dtype=F32, name=f"ffn{l}_down_dw", tm=1408, tn=1024, tk=2048)
        du0, d_conv, d_conv_b = _ffn_gate_bwd(u0, dact, ffn_conv_f[l], ffn_conv_b[l:l + 1], f"ffn{l}_gate_bwd")
        du0 = du0.reshape(2, T, F)
        dxn = _matmul(du0, wup_t[l], mode="nn", out_dtype=BF16, a_parts=2, name=f"ffn{l}_up_dx", tk=2816)
        d_wup_t = _matmul(du0, xn, mode="tn", out_dtype=F32, a_parts=2, name=f"ffn{l}_up_dw", tm=1408, tn=1024, tk=2048)
        dh_in, dh_in_b, d_norm = _rmsnorm_bwd(h_in, ffn_norm[l:l + 1], [dxn], dh_out, f"ffn{l}_norm_bwd")
        return (dh_in, dh_in_b), (d_wdn, d_wup_t, d_conv, d_conv_b, d_norm)

    (dh3, dh3b), ffn1_grads = ffn_bwd(dh4, dh4b, h3, ffn1_saved, 1, None)
    token, _ = send_group("ffn1", [(6, ffn1_grads[1]), (8, ffn1_grads[0])])

    dattn = _matmul(dh3b, wo, mode="nt", out_dtype=F32, after=token, name="o_proj_dx").reshape(B, S, D)
    d_wo = _matmul(attn, dh3b, mode="tn", out_dtype=F32, name="o_proj_dw", tm=1024, tn=1024, tk=2048)
    dq, dkv, dtab = _attn_bwd(q, kv, dattn, attn3, lse3, tabs, "attn_bwd")
    dtabs = dtab.transpose(1, 0, 2, 3, 4).reshape(3, N_HEADS, P * 2 * P)
    d_rel_bias = _bias_grad(dtabs, onehots, "rel_bias_grad").T
    dq2 = dq.reshape(T, D)
    dkv2 = dkv.reshape(2, T, D)
    dxnb = _matmul(dq2, wq, mode="nt", out_dtype=BF16, name="q_proj_dx")
    d_wq = _matmul(xnb, dq2, mode="tn", out_dtype=F32, name="q_proj_dw", tm=1024, tn=1024, tk=2048)
    dkvn = _matmul(dkv2, wkv_t, mode="nn", out_dtype=BF16, a_parts=2, name="kv_proj_dx")
    d_wkv_t = _matmul(dkv2, kvn, mode="tn", out_dtype=F32, a_parts=2, name="kv_proj_dw", tm=1024, tn=1024, tk=2048)
    dh2, dh2b, d_kvb_norm = _rmsnorm_bwd(h2, jnp.stack([kv_norm, b_norm[0]]), [dkvn, dxnb], dh3, "kv_b_norm_bwd")

    token, _ = send_group("attn", [(2, d_wkv_t), (3, d_wq), (4, d_wo)])

    (dh1, dh1b), ffn0_grads = ffn_bwd(dh2, dh2b, h1, ffn0_saved, 0, token)
    token, _ = send_group("ffn0", [(5, ffn0_grads[1]), (7, ffn0_grads[0])])

    dgated = _matmul(dh1b, wout, mode="nt", out_dtype=BF16, after=token, name="a_out_proj_dx").reshape(B, S, D)
    d_wout = _matmul(gated, dh1b, mode="tn", out_dtype=F32, name="a_out_proj_dw", tm=1024, tn=1024, tk=2048)
    dbch, d_a_conv = _shortconv_bwd(bch, dgated, a_conv_f, "a_gate_bwd")
    dbch = dbch.reshape(3, T, D)
    dxn0 = _matmul(dbch, win_t, mode="nn", out_dtype=BF16, a_parts=3, name="a_in_proj_dx")
    d_win_t = _matmul(dbch, xn0, mode="tn", out_dtype=F32, a_parts=3, name="a_in_proj_dw", tm=1024, tn=1024, tk=2048)
    grad_x, _, d_a_norm = _rmsnorm_bwd(x2, a_norm_f, [dxn0], dh1, "a_norm_bwd")

    small_full = [d_a_norm, d_a_conv, jnp.stack([ffn0_grads[2], ffn1_grads[2]]),
                  d_kvb_norm[0], d_kvb_norm[1], d_rel_bias, jnp.concatenate([ffn0_grads[4], ffn1_grads[4]]),
                  jnp.concatenate([ffn0_grads[3], ffn1_grads[3]]), d_final_norm, loss_part]
    small_full_shapes = [(1, D), (3, D), (2, 3, 2 * F), (D,), (1, D), rel_bias.shape, (2, D), (2, 2 * F), (D,), ()]
    token, gath = send_group("a", [(0, d_win_t), (1, d_wout)], small=[_pack_rows(small_full)])
    chip_sums, recv = {}, {}
    for tag, idxs, send, rcv, srcs, lnd in in_flight:
        srcs, lnd = _split_wait(_chip_copies, send, rcv, srcs, lnd, token, f"exchange_chip_wait_{tag}")
        for i, s_, l_ in zip(idxs, srcs, lnd):
            chip_sums[i], recv[i] = s_, l_
    small_sum = _sum_slots(gath[0], "small_grad_sum").reshape(-1)
    (g_a_norm_f, g_a_conv_f, g_ffn_conv_f, g_kv_norm, g_b_norm, g_rel_bias, g_ffn_norm, g_ffn_conv_b,
     g_final_norm, loss) = _unpack(small_sum, small_full_shapes)

    def my_cols(full, width):
        return lax.dynamic_slice_in_dim(full, me * width, width, axis=full.ndim - 1)

    g_a_norm = my_cols(g_a_norm_f, D // N_DEV)
    g_a_conv = my_cols(g_a_conv_f, D // N_DEV)[None]
    g_ffn_conv = my_cols(g_ffn_conv_f, 2 * F // N_DEV)

    big_w = [(a_w_in, m_a_w_in, v_a_w_in, True), (a_w_out, m_a_w_out, v_a_w_out, False),
             (w_kv, m_w_kv, v_w_kv, True), (b_w_q, m_b_w_q, v_b_w_q, False), (b_w_o, m_b_w_o, v_b_w_o, False),
             (ffn_w_up[0], m_ffn_w_up[0], v_ffn_w_up[0], True), (ffn_w_up[1], m_ffn_w_up[1], v_ffn_w_up[1], True),
             (ffn_w_down[0], m_ffn_w_down[0], v_ffn_w_down[0], False),
             (ffn_w_down[1], m_ffn_w_down[1], v_ffn_w_down[1], False)]
    big_out = []
    for i, (w, m, v, transposed) in enumerate(big_w):
        g = _chip_sum(chip_sums[i], recv[i], chip_idx, f"grad_sum_{i}")
        if transposed:
            g = g.T
        w2, m2, v2 = (t.reshape(g.shape) for t in (w, m, v))
        delta, nm, nv = _adamw(g, w2, m2, v2, f"adamw_{i}")
        big_out.append(tuple(t.reshape(w.shape) for t in (g, delta, nm, nv)))

    def pair(i, j):
        return tuple(jnp.stack([big_out[i][t], big_out[j][t]]) for t in range(4))

    small_w = [(a_norm, m_a_norm, v_a_norm, g_a_norm), (a_conv, m_a_conv, v_a_conv, g_a_conv),
               (ffn_conv, m_ffn_conv, v_ffn_conv, g_ffn_conv), (kv_norm, m_kv_norm, v_kv_norm, g_kv_norm),
               (b_norm, m_b_norm, v_b_norm, g_b_norm), (rel_bias, m_rel_bias, v_rel_bias, g_rel_bias),
               (ffn_norm, m_ffn_norm, v_ffn_norm, g_ffn_norm),
               (ffn_conv_b, m_ffn_conv_b, v_ffn_conv_b, g_ffn_conv_b),
               (final_norm, m_final_norm, v_final_norm, g_final_norm)]
    small_out = _adamw_small(small_w, "adamw_small")

    per_weight = [small_out[0], big_out[0], small_out[1], big_out[1], small_out[3], big_out[2], small_out[4],
                  big_out[3], big_out[4], small_out[5], small_out[6], pair(5, 6), small_out[2], small_out[7],
                  pair(7, 8), small_out[8]]
    outs = [loss, grad_x.reshape(B, S, D)]
    for t in range(4):
        outs.extend(pw[t] for pw in per_weight)
    return tuple(outs)
```

```python
import functools
import math

import jax
import jax.numpy as jnp
from jax import lax
from jax.experimental import pallas as pl
from jax.experimental.pallas import tpu as pltpu

F32 = jnp.float32
BF16 = jnp.bfloat16

N_DEV = 8
N_HEADS = 16
ATT_BLOCK = 128
ATT_UNROLL = 8
DILATED_BRANCHES = ((128, 1), (512, 4), (2048, 16))
REL_BUCKETS = 32
REL_MAX_DISTANCE = 2048
RMS_EPS = 1e-6
ADAM_LR = 0.001
ADAM_B1 = 0.9
ADAM_B2 = 0.999
ADAM_EPS = 1e-08
ADAM_WD = 0.01
ADAM_STEP = 10

LANE = 128
HALO = 16
NEG = -1e30
VMEM_CAP = 56 << 20


def _vmem(block_bytes):
    return int(min(VMEM_CAP, max(32 << 20, 3 * block_bytes + (8 << 20))))


def _nbytes(shape, dtype):
    return math.prod(shape) * jnp.dtype(dtype).itemsize


def _tile(dim, target):
    best = None
    t = LANE
    while t <= min(dim, target):
        if dim % t == 0:
            best = t
        t += LANE
    return best if best is not None else dim


def _matmul(a, b, *, mode, out_dtype, name, add=None, after=None, a_parts=1, tm=1024, tn=1024, tk=1024):
    P = a_parts
    if mode == "nn":
        M, K = (a.shape[0], a.shape[1]) if P == 1 else (a.shape[1], a.shape[2] * P)
        N = b.shape[1]
    elif mode == "nt":
        assert P == 1
        M, K = a.shape
        N = b.shape[0]
    else:
        K = a.shape[0] if P == 1 else a.shape[1]
        M = a.shape[1] if P == 1 else a.shape[2] * P
        N = b.shape[1]
    tm = _tile(M // P if mode == "tn" else M, tm)
    tn = _tile(N, tn)
    tk = _tile(K // P if mode == "nn" else K, tk)
    nm, nn_, nk = M // tm, N // tn, K // tk

    if mode == "nn":
        if P == 1:
            a_spec = pl.BlockSpec((tm, tk), lambda i, j, k: (i, k))
        else:
            nkp = nk // P
            a_spec = pl.BlockSpec((None, tm, tk), lambda i, j, k: (k // nkp, i, k % nkp))
        b_spec = pl.BlockSpec((tk, tn), lambda i, j, k: (k, j))
        dims = (((1,), (0,)), ((), ()))
    elif mode == "nt":
        a_spec = pl.BlockSpec((tm, tk), lambda i, j, k: (i, k))
        b_spec = pl.BlockSpec((tn, tk), lambda i, j, k: (j, k))
        dims = (((1,), (1,)), ((), ()))
    else:
        if P == 1:
            a_spec = pl.BlockSpec((tk, tm), lambda i, j, k: (k, i))
        else:
            nmp = nm // P
            a_spec = pl.BlockSpec((None, tk, tm), lambda i, j, k: (i // nmp, k, i % nmp))
        b_spec = pl.BlockSpec((tk, tn), lambda i, j, k: (k, j))
        dims = (((0,), (0,)), ((), ()))
    o_spec = pl.BlockSpec((tm, tn), lambda i, j, k: (i, j))
    in_specs = [a_spec, b_spec]
    args = [a, b]
    if add is not None:
        in_specs.append(o_spec)
        args.append(add)
    has_add = add is not None
    if after is not None:
        in_specs.append(pl.BlockSpec(memory_space=pl.ANY))
        args.append(after)
    n_extra = len(args) - 2

    direct = nk > 1 and out_dtype == F32 and not has_add

    def body(a_ref, b_ref, *rest):
        add_ref = rest[0] if has_add else None
        o_ref = rest[n_extra]
        acc_ref = o_ref if direct or nk == 1 else rest[-1]
        k = pl.program_id(2)

        def part():
            return lax.dot_general(a_ref[...].astype(BF16), b_ref[...].astype(BF16), dims,
                                   preferred_element_type=F32)

        def finish(r):
            if has_add:
                r = r + add_ref[...].astype(F32)
            o_ref[...] = r.astype(out_dtype)

        if nk == 1:
            finish(part())
        else:
            @pl.when(k == 0)
            def _():
                acc_ref[...] = part()

            @pl.when(jnp.logical_and(k > 0, jnp.logical_or(k < nk - 1, direct)))
            def _():
                acc_ref[...] += part()

            if not direct:
                @pl.when(k == nk - 1)
                def _():
                    finish(acc_ref[...] + part())

    blk = (_nbytes((tm, tk), a.dtype) + _nbytes((tk, tn), b.dtype) + _nbytes((tm, tn), out_dtype)
           + (_nbytes((tm, tn), add.dtype) if has_add else 0)) * 2 + 3 * _nbytes((tm, tn), F32)
    return pl.pallas_call(
        body, name=name, grid=(nm, nn_, nk),
        in_specs=in_specs, out_specs=o_spec,
        out_shape=jax.ShapeDtypeStruct((M, N), out_dtype),
        scratch_shapes=[] if direct or nk == 1 else [pltpu.VMEM((tm, tn), F32)],
        compiler_params=pltpu.CompilerParams(
            dimension_semantics=("parallel", "parallel", "arbitrary"), vmem_limit_bytes=_vmem(blk)),
    )(*args)


def _rmsnorm_fwd(x, gains, name, tm=1024):
    T, D = x.shape
    n = gains.shape[0]
    tm = _tile(T, tm)

    def body(x_ref, g_ref, *o_refs):
        xv = x_ref[...]
        xhat = xv * lax.rsqrt(jnp.mean(xv * xv, axis=-1, keepdims=True) + RMS_EPS)
        for i in range(n):
            o_refs[i][...] = (xhat * g_ref[i:i + 1, :]).astype(BF16)

    row = pl.BlockSpec((tm, D), lambda i: (i, 0))
    outs = pl.pallas_call(
        body, name=name, grid=(T // tm,),
        in_specs=[row, pl.BlockSpec((n, D), lambda i: (0, 0))],
        out_specs=[row] * n,
        out_shape=[jax.ShapeDtypeStruct((T, D), BF16)] * n,
        compiler_params=pltpu.CompilerParams(
            dimension_semantics=("parallel",), vmem_limit_bytes=_vmem(4 * _nbytes((tm, D), F32))),
    )(x, gains)
    return tuple(outs)


def _rmsnorm_bwd(x, gains, dxns, dres, name, tm=1024):
    T, D = x.shape
    n = gains.shape[0]
    tm = _tile(T, tm)

    def body(x_ref, g_ref, *rest):
        dxn_refs = rest[:n]
        dres_ref, dx_ref, dxb_ref, dg_ref = rest[n:]
        xv = x_ref[...]
        rstd = lax.rsqrt(jnp.mean(xv * xv, axis=-1, keepdims=True) + RMS_EPS)
        xhat = xv * rstd
        dx = dres_ref[...]

        @pl.when(pl.program_id(0) == 0)
        def _():
            dg_ref[...] = jnp.zeros_like(dg_ref)

        for i in range(n):
            dy = dxn_refs[i][...].astype(F32)
            dg_ref[i:i + 1, :] += jnp.sum(dy * xhat, axis=0, keepdims=True)
            dxh = dy * g_ref[i:i + 1, :]
            dx = dx + rstd * (dxh - xhat * jnp.mean(dxh * xhat, axis=-1, keepdims=True))
        dx_ref[...] = dx
        dxb_ref[...] = dx.astype(BF16)

    row = pl.BlockSpec((tm, D), lambda i: (i, 0))
    par = pl.BlockSpec((n, D), lambda i: (0, 0))
    return pl.pallas_call(
        body, name=name, grid=(T // tm,),
        in_specs=[row, par] + [row] * n + [row],
        out_specs=[row, row, par],
        out_shape=[jax.ShapeDtypeStruct((T, D), F32), jax.ShapeDtypeStruct((T, D), BF16),
                   jax.ShapeDtypeStruct((n, D), F32)],
        compiler_params=pltpu.CompilerParams(
            dimension_semantics=("arbitrary",), vmem_limit_bytes=_vmem((4 + n) * _nbytes((tm, D), F32))),
    )(x, gains, *dxns, dres)


def _final_loss_bwd(h, gain, target, name, tm=1024):
    T, D = h.shape
    tm = _tile(T, tm)

    def body(h_ref, g_ref, t_ref, dh_ref, dhb_ref, dg_ref, sq_ref):
        xv = h_ref[...]
        rstd = lax.rsqrt(jnp.mean(xv * xv, axis=-1, keepdims=True) + RMS_EPS)
        xhat = xv * rstd
        err = xhat * g_ref[...] - t_ref[...]

        @pl.when(pl.program_id(0) == 0)
        def _():
            dg_ref[...] = jnp.zeros_like(dg_ref)
            sq_ref[...] = jnp.zeros_like(sq_ref)

        sq_ref[...] += jnp.sum(err * err, axis=0, keepdims=True)
        dy = err * (1.0 / D)
        dg_ref[...] += jnp.sum(dy * xhat, axis=0, keepdims=True)
        dxh = dy * g_ref[...]
        dh = rstd * (dxh - xhat * jnp.mean(dxh * xhat, axis=-1, keepdims=True))
        dh_ref[...] = dh
        dhb_ref[...] = dh.astype(BF16)

    row = pl.BlockSpec((tm, D), lambda i: (i, 0))
    par = pl.BlockSpec((1, D), lambda i: (0, 0))
    return pl.pallas_call(
        body, name=name, grid=(T // tm,),
        in_specs=[row, par, row], out_specs=[row, row, par, par],
        out_shape=[jax.ShapeDtypeStruct((T, D), F32), jax.ShapeDtypeStruct((T, D), BF16),
                   jax.ShapeDtypeStruct((1, D), F32), jax.ShapeDtypeStruct((1, D), F32)],
        compiler_params=pltpu.CompilerParams(
            dimension_semantics=("arbitrary",), vmem_limit_bytes=_vmem(5 * _nbytes((tm, D), F32))),
    )(h, gain, target)


def _halo_specs(S, ts, tc, col_off, prev, nxt):
    r = ts // HALO
    last = S // HALO - 1
    specs = []
    if prev:
        specs.append(pl.BlockSpec((None, HALO, tc), lambda j, b, s: (b, jnp.maximum(s * r - 1, 0), col_off + j)))
    specs.append(pl.BlockSpec((None, ts, tc), lambda j, b, s: (b, s, col_off + j)))
    if nxt:
        specs.append(pl.BlockSpec((None, HALO, tc), lambda j, b, s: (b, jnp.minimum((s + 1) * r, last), col_off + j)))
    return specs


def _ext(prev_ref, main_ref, next_ref, first, last):
    main = main_ref[...].astype(F32)
    zeros = jnp.zeros((HALO, main.shape[1]), F32)
    top = zeros if prev_ref is None else jnp.where(first, 0.0, prev_ref[...].astype(F32))
    bot = zeros if next_ref is None else jnp.where(last, 0.0, next_ref[...].astype(F32))
    return jnp.concatenate([top, main, bot], axis=0)


def _shift(xe, k):
    return pltpu.roll(xe, k % xe.shape[0], axis=0)


def _cconv(xe, w):
    return w[0:1, :] * _shift(xe, 2) + w[1:2, :] * _shift(xe, 1) + w[2:3, :] * xe


def _main(xe, ts):
    return xe[HALO:HALO + ts, :]


def _cconv_t(de, w):
    return w[2:3, :] * de + w[1:2, :] * _shift(de, -1) + w[0:1, :] * _shift(de, -2)


def _conv_wgrad(dw_ref, de, xe, ts):
    d = _main(de, ts)
    dw_ref[0:1, :] += jnp.sum(d * _main(_shift(xe, 2), ts), axis=0, keepdims=True)
    dw_ref[1:2, :] += jnp.sum(d * _main(_shift(xe, 1), ts), axis=0, keepdims=True)
    dw_ref[2:3, :] += jnp.sum(d * _main(xe, ts), axis=0, keepdims=True)


def _gate_tiles(S, C, rows=1024):
    return _tile(S, rows), _tile(C, 512)


def _shortconv_fwd(bch, conv_w, name):
    B, S, D3 = bch.shape
    D = D3 // 3
    ts, tc = _gate_tiles(S, D, rows=2048)
    nj, ns = D // tc, S // ts

    def body(b_ref, cp_ref, c_ref, hp_ref, h_ref, w_ref, o_ref):
        first = pl.program_id(2) == 0
        ce = _ext(cp_ref, c_ref, None, first, False)
        he = _ext(hp_ref, h_ref, None, first, False)
        cv = _main(_cconv(ce * he, w_ref[...]), ts)
        o_ref[...] = (b_ref[...].astype(F32) * cv).astype(BF16)

    in_specs = (_halo_specs(S, ts, tc, 0, False, False) + _halo_specs(S, ts, tc, nj, True, False)
                + _halo_specs(S, ts, tc, 2 * nj, True, False) + [pl.BlockSpec((3, tc), lambda j, b, s: (0, j))])
    return pl.pallas_call(
        body, name=name, grid=(nj, B, ns), in_specs=in_specs,
        out_specs=pl.BlockSpec((None, ts, tc), lambda j, b, s: (b, s, j)),
        out_shape=jax.ShapeDtypeStruct((B, S, D), BF16),
        compiler_params=pltpu.CompilerParams(
            dimension_semantics=("parallel", "parallel", "parallel"),
            vmem_limit_bytes=_vmem(12 * _nbytes((ts + 2 * HALO, tc), F32))),
    )(bch, bch, bch, bch, bch, conv_w)


def _shortconv_bwd(bch, dg, conv_w, name):
    B, S, D3 = bch.shape
    D = D3 // 3
    ts, tc = _gate_tiles(S, D)
    nj, ns = D // tc, S // ts

    def body(b_ref, bn_ref, cp_ref, c_ref, cn_ref, hp_ref, h_ref, hn_ref, d_ref, dn_ref, w_ref, o_ref, dw_ref):
        s = pl.program_id(2)
        first, last = s == 0, s == ns - 1

        @pl.when(jnp.logical_and(pl.program_id(1) == 0, s == 0))
        def _():
            dw_ref[...] = jnp.zeros_like(dw_ref)

        w = w_ref[...]
        be = _ext(None, b_ref, bn_ref, first, last)
        ce = _ext(cp_ref, c_ref, cn_ref, first, last)
        he = _ext(hp_ref, h_ref, hn_ref, first, last)
        de = _ext(None, d_ref, dn_ref, first, last)
        ch = ce * he
        dcv = de * be
        dch = _main(_cconv_t(dcv, w), ts)
        o_ref[0] = (_main(de, ts) * _main(_cconv(ch, w), ts)).astype(BF16)
        o_ref[1] = (dch * _main(he, ts)).astype(BF16)
        o_ref[2] = (dch * _main(ce, ts)).astype(BF16)
        _conv_wgrad(dw_ref, dcv, ch, ts)

    in_specs = (_halo_specs(S, ts, tc, 0, False, True) + _halo_specs(S, ts, tc, nj, True, True)
                + _halo_specs(S, ts, tc, 2 * nj, True, True) + _halo_specs(S, ts, tc, 0, False, True)
                + [pl.BlockSpec((3, tc), lambda j, b, s: (0, j))])
    return pl.pallas_call(
        body, name=name, grid=(nj, B, ns), in_specs=in_specs,
        out_specs=[pl.BlockSpec((3, None, ts, tc), lambda j, b, s: (0, b, s, j)),
                   pl.BlockSpec((3, tc), lambda j, b, s: (0, j))],
        out_shape=[jax.ShapeDtypeStruct((3, B, S, D), BF16), jax.ShapeDtypeStruct((3, D), F32)],
        compiler_params=pltpu.CompilerParams(
            dimension_semantics=("parallel", "arbitrary", "arbitrary"),
            vmem_limit_bytes=_vmem(24 * _nbytes((ts + 2 * HALO, tc), F32))),
    )(bch, bch, bch, bch, bch, bch, bch, bch, dg, dg, conv_w)


def _sigmoid(x):
    return 1.0 / (1.0 + jnp.exp(-x))


def _ffn_gate_fwd(u0, conv_w, conv_b, name):
    B, S, F2 = u0.shape
    F = F2 // 2
    ts, tc = _gate_tiles(S, F, rows=2048)
    nj, ns = F // tc, S // ts

    def body(gp_ref, g_ref, up_ref, u_ref, wg_ref, wu_ref, bg_ref, bu_ref, o_ref):
        first = pl.program_id(2) == 0
        ug = _main(_cconv(_ext(gp_ref, g_ref, None, first, False), wg_ref[...]), ts) + bg_ref[...]
        uu = _main(_cconv(_ext(up_ref, u_ref, None, first, False), wu_ref[...]), ts) + bu_ref[...]
        o_ref[...] = (ug * _sigmoid(ug) * uu).astype(BF16)

    in_specs = (_halo_specs(S, ts, tc, 0, True, False) + _halo_specs(S, ts, tc, nj, True, False)
                + [pl.BlockSpec((3, tc), lambda j, b, s: (0, j)), pl.BlockSpec((3, tc), lambda j, b, s: (0, nj + j)),
                   pl.BlockSpec((1, tc), lambda j, b, s: (0, j)), pl.BlockSpec((1, tc), lambda j, b, s: (0, nj + j))])
    return pl.pallas_call(
        body, name=name, grid=(nj, B, ns), in_specs=in_specs,
        out_specs=pl.BlockSpec((None, ts, tc), lambda j, b, s: (b, s, j)),
        out_shape=jax.ShapeDtypeStruct((B, S, F), BF16),
        compiler_params=pltpu.CompilerParams(
            dimension_semantics=("parallel", "parallel", "parallel"),
            vmem_limit_bytes=_vmem(12 * _nbytes((ts + 2 * HALO, tc), F32))),
    )(u0, u0, u0, u0, conv_w, conv_w, conv_b, conv_b)


def _ffn_gate_bwd(u0, dact, conv_w, conv_b, name):
    B, S, F2 = u0.shape
    F = F2 // 2
    ts, tc = _gate_tiles(S, F)
    nj, ns = F // tc, S // ts

    def body(gp_ref, g_ref, gn_ref, up_ref, u_ref, un_ref, d_ref, dn_ref, wg_ref, wu_ref, bg_ref, bu_ref,
             o_ref, dwg_ref, dwu_ref, dbg_ref, dbu_ref):
        s = pl.program_id(2)
        first, last = s == 0, s == ns - 1

        @pl.when(jnp.logical_and(pl.program_id(1) == 0, s == 0))
        def _():
            for r in (dwg_ref, dwu_ref, dbg_ref, dbu_ref):
                r[...] = jnp.zeros_like(r)

        wg, wu = wg_ref[...], wu_ref[...]
        ge = _ext(gp_ref, g_ref, gn_ref, first, last)
        ue = _ext(up_ref, u_ref, un_ref, first, last)
        de = _ext(None, d_ref, dn_ref, first, last)
        ug = _cconv(ge, wg) + bg_ref[...]
        uu = _cconv(ue, wu) + bu_ref[...]
        sig = _sigmoid(ug)
        dug = de * uu * (sig * (1.0 + ug * (1.0 - sig)))
        duu = de * (ug * sig)
        o_ref[0] = _main(_cconv_t(dug, wg), ts).astype(BF16)
        o_ref[1] = _main(_cconv_t(duu, wu), ts).astype(BF16)
        _conv_wgrad(dwg_ref, dug, ge, ts)
        _conv_wgrad(dwu_ref, duu, ue, ts)
        dbg_ref[...] += jnp.sum(_main(dug, ts), axis=0, keepdims=True)
        dbu_ref[...] += jnp.sum(_main(duu, ts), axis=0, keepdims=True)

    w3 = lambda off: pl.BlockSpec((3, tc), lambda j, b, s: (0, off + j))
    w1 = lambda off: pl.BlockSpec((1, tc), lambda j, b, s: (0, off + j))
    in_specs = (_halo_specs(S, ts, tc, 0, True, True) + _halo_specs(S, ts, tc, nj, True, True)
                + _halo_specs(S, ts, tc, 0, False, True) + [w3(0), w3(nj), w1(0), w1(nj)])
    outs = pl.pallas_call(
        body, name=name, grid=(nj, B, ns), in_specs=in_specs,
        out_specs=[pl.BlockSpec((2, None, ts, tc), lambda j, b, s: (0, b, s, j)), w3(0), w3(0), w1(0), w1(0)],
        out_shape=[jax.ShapeDtypeStruct((2, B, S, F), BF16), jax.ShapeDtypeStruct((3, F), F32),
                   jax.ShapeDtypeStruct((3, F), F32), jax.ShapeDtypeStruct((1, F), F32),
                   jax.ShapeDtypeStruct((1, F), F32)],
        compiler_params=pltpu.CompilerParams(
            dimension_semantics=("parallel", "arbitrary", "arbitrary"),
            vmem_limit_bytes=_vmem(30 * _nbytes((ts + 2 * HALO, tc), F32))),
    )(u0, u0, u0, u0, u0, u0, dact, dact, conv_w, conv_w, conv_b, conv_b)
    du0, dwg, dwu, dbg, dbu = outs
    return du0, jnp.concatenate([dwg, dwu], axis=1), jnp.concatenate([dbg, dbu], axis=1)


def _t5_bucket(dist):
    max_exact = REL_BUCKETS // 2
    n = jnp.maximum(dist, 0)
    nf = jnp.maximum(n, max_exact).astype(F32)
    large = max_exact + (jnp.log(nf / max_exact) / math.log(REL_MAX_DISTANCE / max_exact)
                         * (REL_BUCKETS - max_exact)).astype(jnp.int32)
    large = jnp.minimum(large, REL_BUCKETS - 1)
    return jnp.where(n < max_exact, n, large)


def _band_tables(window, dilation):
    P = ATT_BLOCK
    qi = jnp.arange(P, dtype=jnp.int32)[:, None]
    kc = jnp.arange(2 * P, dtype=jnp.int32)[None, :]
    delta = qi + P - kc
    band = (delta >= 0) & (delta <= window // dilation)
    bucket = _t5_bucket(delta * dilation).reshape(-1)
    onehot = (bucket[None, :] == jnp.arange(REL_BUCKETS, dtype=jnp.int32)[:, None]).astype(F32)
    return onehot, band


def _bias_lookup(rel_bias_t, onehots, name):
    nb, _, Q = onehots.shape
    H = rel_bias_t.shape[0]

    def body(r_ref, oh_ref, o_ref):
        o_ref[...] = lax.dot_general(r_ref[...], oh_ref[...], (((1,), (0,)), ((), ())),
                                     precision=lax.Precision.HIGHEST, preferred_element_type=F32)

    return pl.pallas_call(
        body, name=name, grid=(nb,),
        in_specs=[pl.BlockSpec((H, REL_BUCKETS), lambda i: (0, 0)),
                  pl.BlockSpec((None, REL_BUCKETS, Q), lambda i: (i, 0, 0))],
        out_specs=pl.BlockSpec((None, H, Q), lambda i: (i, 0, 0)),
        out_shape=jax.ShapeDtypeStruct((nb, H, Q), F32),
        compiler_params=pltpu.CompilerParams(dimension_semantics=("parallel",),
                                             vmem_limit_bytes=_vmem(_nbytes((REL_BUCKETS + H, Q), F32))),
    )(rel_bias_t, onehots)


def _bias_grad(dtabs, onehots, name):
    nb, H, Q = dtabs.shape

    def body(d_ref, oh_ref, o_ref):
        @pl.when(pl.program_id(0) == 0)
        def _():
            o_ref[...] = jnp.zeros_like(o_ref)

        o_ref[...] += lax.dot_general(d_ref[...], oh_ref[...], (((1,), (1,)), ((), ())),
                                      precision=lax.Precision.HIGHEST, preferred_element_type=F32)

    return pl.pallas_call(
        body, name=name, grid=(nb,),
        in_specs=[pl.BlockSpec((None, H, Q), lambda i: (i, 0, 0)),
                  pl.BlockSpec((None, REL_BUCKETS, Q), lambda i: (i, 0, 0))],
        out_specs=pl.BlockSpec((H, REL_BUCKETS), lambda i: (0, 0)),
        out_shape=jax.ShapeDtypeStruct((H, REL_BUCKETS), F32),
        compiler_params=pltpu.CompilerParams(dimension_semantics=("arbitrary",),
                                             vmem_limit_bytes=_vmem(_nbytes((REL_BUCKETS + H, Q), F32))),
    )(dtabs, onehots)


def _lane_masks():
    lane = lax.broadcasted_iota(jnp.int32, (1, LANE), 1)
    lo = lane < LANE // 2
    return lo, jnp.logical_not(lo)


def _dot_nt(a, b):
    return lax.dot_general(a, b, (((1,), (1,)), ((), ())), preferred_element_type=F32)


def _dot_nn(a, b):
    return lax.dot_general(a, b, (((1,), (0,)), ((), ())), preferred_element_type=F32)


def _dot_tn(a, b):
    return lax.dot_general(a, b, (((0,), (0,)), ((), ())), preferred_element_type=F32)


def _block_rows(n, dilation, S):
    P = ATT_BLOCK
    nb = S // (dilation * P)
    r, i = n // nb, n % nb
    cur = pl.ds(i * (P * dilation) + r, P, stride=dilation)
    prv = pl.ds(jnp.maximum(i - 1, 0) * (P * dilation) + r, P, stride=dilation)
    return cur, prv, jnp.minimum(i, 1)


def _attn_fwd(q, kv, tabs, name):
    B, S, D = q.shape
    P, H = ATT_BLOCK, N_HEADS
    scale = (D // H) ** -0.5
    half = LANE // 2
    nsteps = S // P // ATT_UNROLL

    def body(q_ref, k_ref, v_ref, tab_ref, o_ref, lse_ref, acc_ref, m_ref, l_ref):
        lo, hi = _lane_masks()
        for bi, (_, d) in enumerate(DILATED_BRANCHES):
            def block(n, carry, bi=bi, d=d):
                rows = [_block_rows(n + j * nsteps, d, S) for j in range(ATT_UNROLL)]
                loaded = []
                for cur, prv, variant in rows:
                    qb = q_ref[cur, :].astype(BF16)
                    kw = jnp.concatenate([k_ref[prv, :], k_ref[cur, :]], axis=0).astype(BF16)
                    vw = jnp.concatenate([v_ref[prv, :], v_ref[cur, :]], axis=0).astype(BF16)
                    loaded.append((qb, kw, vw))
                results = []
                for (cur, prv, variant), (qb, kw, vw) in zip(rows, loaded):
                    m_pair = l_pair = a_pair = None
                    for e, msk in enumerate((lo, hi)):
                        s = _dot_nt(jnp.where(msk, qb, 0), kw) * scale + tab_ref[bi, variant, e]
                        m_e = jnp.max(s, axis=-1, keepdims=True)
                        pe = jnp.exp(s - m_e)
                        l_e = jnp.sum(pe, axis=-1, keepdims=True)
                        a_e = _dot_nn(pe.astype(BF16), jnp.where(msk, vw, 0))
                        m_pair = m_e if e == 0 else jnp.where(lo, m_pair, m_e)
                        l_pair = l_e if e == 0 else jnp.where(lo, l_pair, l_e)
                        a_pair = a_e if e == 0 else a_pair + a_e
                    results.append((m_pair, l_pair, a_pair))
                for (cur, prv, variant), (m_pair, l_pair, a_pair) in zip(rows, results):
                    m_ref[bi, cur, :] = m_pair
                    l_ref[bi, cur, :] = l_pair
                    acc_ref[bi, cur, :] = a_pair
                return carry

            lax.fori_loop(0, nsteps, block, 0)

        nbr = len(DILATED_BRANCHES)
        chunk = 512

        def merge(t, carry):
            rs = pl.ds(pl.multiple_of(t * chunk, chunk), chunk)
            ms = [m_ref[b, rs, :] for b in range(nbr)]
            m = functools.reduce(jnp.maximum, ms)
            ws = [jnp.exp(mb - m) for mb in ms]
            l = functools.reduce(jnp.add, [w * l_ref[b, rs, :] for b, w in enumerate(ws)])
            acc = functools.reduce(jnp.add, [w * acc_ref[b, rs, :] for b, w in enumerate(ws)])
            o_ref[rs, :] = (acc / l).astype(BF16)
            lse_ref[rs, :] = m + jnp.log(l)
            return carry

        lax.fori_loop(0, S // chunk, merge, 0)

    nl = D // LANE
    col = lambda off: pl.BlockSpec((None, S, LANE), lambda b, p: (b, 0, off + p))
    tab_spec = pl.BlockSpec((3, 2, 2, P, 2 * P), lambda b, p: (0, 0, p, 0, 0))
    return pl.pallas_call(
        body, name=name, grid=(B, H // 2), in_specs=[col(0), col(0), col(nl), tab_spec],
        out_specs=[col(0), col(0)],
        out_shape=[jax.ShapeDtypeStruct((B, S, D), BF16), jax.ShapeDtypeStruct((B, S, D), F32)],
        scratch_shapes=[pltpu.VMEM((len(DILATED_BRANCHES), S, LANE), F32)] * 3,
        compiler_params=pltpu.CompilerParams(
            dimension_semantics=("parallel", "parallel"), vmem_limit_bytes=VMEM_CAP),
    )(q, kv, kv, tabs)


def _attn_bwd(q, kv, do, o, lse, tabs, name):
    B, S, D = q.shape
    P, H = ATT_BLOCK, N_HEADS
    scale = (D // H) ** -0.5
    half = LANE // 2
    nsteps = S // P // ATT_UNROLL

    def body(q_ref, k_ref, v_ref, do_ref, o_ref, lse_ref, tab_ref, dq_ref, dkv_ref, dtab_ref, delta_ref):
        lo, hi = _lane_masks()

        @pl.when(pl.program_id(1) == 0)
        def _():
            dtab_ref[...] = jnp.zeros_like(dtab_ref)

        dkv_ref[...] = jnp.zeros_like(dkv_ref)
        prod = do_ref[...] * o_ref[...].astype(F32)
        delta_ref[...] = jnp.where(lo, jnp.sum(jnp.where(lo, prod, 0.0), axis=-1, keepdims=True),
                                   jnp.sum(jnp.where(hi, prod, 0.0), axis=-1, keepdims=True))

        for bi, (_, d) in enumerate(DILATED_BRANCHES):
            def block(n, carry, bi=bi, d=d):
                rows = [_block_rows(n + j * nsteps, d, S) for j in range(ATT_UNROLL)]
                loaded = []
                for cur, prv, variant in rows:
                    qb = q_ref[cur, :].astype(BF16)
                    kw = jnp.concatenate([k_ref[prv, :], k_ref[cur, :]], axis=0).astype(BF16)
                    vw = jnp.concatenate([v_ref[prv, :], v_ref[cur, :]], axis=0).astype(BF16)
                    dob = do_ref[cur, :].astype(BF16)
                    dq_old = dq_ref[cur, :] if bi > 0 else None
                    loaded.append((qb, kw, vw, dob, lse_ref[cur, :], delta_ref[cur, :], dq_old))
                results = []
                for j, ((cur, prv, variant), (qb, kw, vw, dob, lse_b, dl_b, dq_old)) in enumerate(zip(rows, loaded)):
                    dq_pair = dk_pair = dv_pair = None
                    for e, msk in enumerate((lo, hi)):
                        qm = jnp.where(msk, qb, 0)
                        dom = jnp.where(msk, dob, 0)
                        s = _dot_nt(qm, kw) * scale + tab_ref[bi, variant, e]
                        pe = jnp.exp(s - lse_b[:, e * half:e * half + 1])
                        ds = pe * (_dot_nt(dom, vw) - dl_b[:, e * half:e * half + 1])
                        dtab_ref[bi, e] += ds
                        dsb = ds.astype(BF16)
                        dq_e = _dot_nn(dsb, jnp.where(msk, kw, 0))
                        dk_e = _dot_tn(dsb, qm)
                        dv_e = _dot_tn(pe.astype(BF16), dom)
                        dq_pair = dq_e if e == 0 else dq_pair + dq_e
                        dk_pair = dk_e if e == 0 else dk_pair + dk_e
                        dv_pair = dv_e if e == 0 else dv_pair + dv_e
                    dq_pair = dq_pair * scale
                    if bi > 0:
                        dq_pair = dq_pair + dq_old
                    results.append((dq_pair, dk_pair * scale, dv_pair))
                for (cur, prv, variant), (dq_pair, dk_pair, dv_pair) in zip(rows, results):
                    dq_ref[cur, :] = dq_pair
                    dkv_ref[0, cur, :] += dk_pair[P:, :]
                    dkv_ref[1, cur, :] += dv_pair[P:, :]
                    dkv_ref[0, prv, :] += dk_pair[:P, :]
                    dkv_ref[1, prv, :] += dv_pair[:P, :]
                return carry

            lax.fori_loop(0, nsteps, block, 0)

    nl = D // LANE
    col = lambda off: pl.BlockSpec((None, S, LANE), lambda p, b: (b, 0, off + p))
    tab_spec = pl.BlockSpec((3, 2, 2, P, 2 * P), lambda p, b: (0, 0, p, 0, 0))
    return pl.pallas_call(
        body, name=name, grid=(H // 2, B),
        in_specs=[col(0), col(0), col(nl), col(0), col(0), col(0), tab_spec],
        out_specs=[col(0), pl.BlockSpec((2, None, S, LANE), lambda p, b: (0, b, 0, p)),
                   pl.BlockSpec((None, 3, 2, P, 2 * P), lambda p, b: (p, 0, 0, 0, 0))],
        out_shape=[jax.ShapeDtypeStruct((B, S, D), F32), jax.ShapeDtypeStruct((2, B, S, D), F32),
                   jax.ShapeDtypeStruct((H // 2, 3, 2, P, 2 * P), F32)],
        scratch_shapes=[pltpu.VMEM((S, LANE), F32)],
        compiler_params=pltpu.CompilerParams(
            dimension_semantics=("parallel", "arbitrary"),
            vmem_limit_bytes=VMEM_CAP),
    )(q, kv, kv, do, o, lse, tabs)


def _attn_fwd_branch(q, kv, tab, dilation, name):
    B, S, D = q.shape
    P, d = ATT_BLOCK, dilation
    L = S // d
    nb = L // P
    H = N_HEADS
    scale = (D // H) ** -0.5
    qv = q.reshape(B, L, d * D)
    kvv = kv.reshape(B, L, d * 2 * D)

    def body(q_ref, kp_ref, kc_ref, vp_ref, vc_ref, tab_ref, o_ref, lse_ref):
        lo, hi = _lane_masks()
        for p in range(H // 2):
            sl = slice(p * LANE, (p + 1) * LANE)
            qp = q_ref[:, sl]
            kp = jnp.concatenate([kp_ref[:, sl], kc_ref[:, sl]], axis=0)
            vp = jnp.concatenate([vp_ref[:, sl], vc_ref[:, sl]], axis=0)
            o_pair = None
            lse_pair = None
            for e, msk in enumerate((lo, hi)):
                s = _dot_nt(jnp.where(msk, qp, 0), kp) * scale + tab_ref[2 * p + e]
                m = jnp.max(s, axis=-1, keepdims=True)
                pe = jnp.exp(s - m)
                den = jnp.sum(pe, axis=-1, keepdims=True)
                o_e = _dot_nn(pe.astype(BF16), jnp.where(msk, vp, 0)) / den
                lse_e = m + jnp.log(den)
                o_pair = o_e if e == 0 else o_pair + o_e
                lse_pair = lse_e if e == 0 else jnp.where(lo, lse_pair, lse_e)
            o_ref[:, sl] = o_pair
            lse_ref[:, sl] = lse_pair

    blk = lambda f: pl.BlockSpec((None, P, D), f)
    prev = lambda i: jnp.maximum(i - 1, 0)
    in_specs = [blk(lambda b, r, i: (b, i, r)),
                blk(lambda b, r, i: (b, prev(i), 2 * r)), blk(lambda b, r, i: (b, i, 2 * r)),
                blk(lambda b, r, i: (b, prev(i), 2 * r + 1)), blk(lambda b, r, i: (b, i, 2 * r + 1)),
                pl.BlockSpec((None, H, P, 2 * P), lambda b, r, i: (jnp.minimum(i, 1), 0, 0, 0))]
    o, lse = pl.pallas_call(
        body, name=name, grid=(B, d, nb), in_specs=in_specs,
        out_specs=[blk(lambda b, r, i: (b, i, r))] * 2,
        out_shape=[jax.ShapeDtypeStruct((B, L, d * D), F32)] * 2,
        compiler_params=pltpu.CompilerParams(
            dimension_semantics=("parallel", "parallel", "parallel"),
            vmem_limit_bytes=_vmem(_nbytes((H, P, 2 * P), F32) + 9 * _nbytes((P, D), F32))),
    )(qv, kvv, kvv, kvv, kvv, tab)
    return o.reshape(B * S, D), lse.reshape(B * S, D)


def _attn_merge(os_, lses, name, tm=512):
    T, D = os_[0].shape
    n = len(os_)
    tm = _tile(T, tm)

    def body(*refs):
        o_refs, l_refs = refs[:n], refs[n:2 * n]
        out_ref, lse_ref = refs[2 * n:]
        ls = [r[...] for r in l_refs]
        m = functools.reduce(jnp.maximum, ls)
        ws = [jnp.exp(l - m) for l in ls]
        tot = functools.reduce(jnp.add, ws)
        acc = functools.reduce(jnp.add, [w * r[...] for w, r in zip(ws, o_refs)])
        out_ref[...] = (acc / tot).astype(BF16)
        lse_ref[...] = m + jnp.log(tot)

    row = pl.BlockSpec((tm, D), lambda i: (i, 0))
    return pl.pallas_call(
        body, name=name, grid=(T // tm,), in_specs=[row] * (2 * n), out_specs=[row, row],
        out_shape=[jax.ShapeDtypeStruct((T, D), BF16), jax.ShapeDtypeStruct((T, D), F32)],
        compiler_params=pltpu.CompilerParams(
            dimension_semantics=("parallel",), vmem_limit_bytes=_vmem((2 * n + 4) * _nbytes((tm, D), F32))),
    )(*os_, *lses)


def _attn_bwd_q(q, kv, do, o, lse, tab, dq_acc, dilation, name):
    B, S, D = q.shape
    P, d = ATT_BLOCK, dilation
    L = S // d
    nb = L // P
    H = N_HEADS
    scale = (D // H) ** -0.5
    view = lambda t, c: t.reshape(B, L, d * c)
    has_acc = dq_acc is not None

    def body(q_ref, kp_ref, kc_ref, vp_ref, vc_ref, do_ref, o_ref, lse_ref, tab_ref, *rest):
        if has_acc:
            acc_ref, dq_ref, dtab_ref = rest
        else:
            dq_ref, dtab_ref = rest

        @pl.when((pl.program_id(0) == 0) & (pl.program_id(1) == 0) & (pl.program_id(2) == 0))
        def _():
            dtab_ref[...] = jnp.zeros_like(dtab_ref)

        lo, hi = _lane_masks()
        for p in range(H // 2):
            sl = slice(p * LANE, (p + 1) * LANE)
            qp = q_ref[:, sl]
            kp = jnp.concatenate([kp_ref[:, sl], kc_ref[:, sl]], axis=0)
            vp = jnp.concatenate([vp_ref[:, sl], vc_ref[:, sl]], axis=0)
            dop = do_ref[:, sl]
            prod = dop.astype(F32) * o_ref[:, sl].astype(F32)
            lsep = lse_ref[:, sl]
            dq_pair = None
            for e, msk in enumerate((lo, hi)):
                s = _dot_nt(jnp.where(msk, qp, 0), kp) * scale + tab_ref[2 * p + e]
                pe = jnp.exp(s - lsep[:, e * (LANE // 2):e * (LANE // 2) + 1])
                dp = _dot_nt(jnp.where(msk, dop, 0).astype(BF16), vp)
                delta = jnp.sum(jnp.where(msk, prod, 0.0), axis=-1, keepdims=True)
                ds = pe * (dp - delta)
                dtab_ref[2 * p + e] += ds
                dq_e = _dot_nn(ds.astype(BF16), jnp.where(msk, kp, 0))
                dq_pair = dq_e if e == 0 else dq_pair + dq_e
            dq_pair = dq_pair * scale
            if has_acc:
                dq_pair = dq_pair + acc_ref[:, sl]
            dq_ref[:, sl] = dq_pair

    blk = lambda f: pl.BlockSpec((None, P, D), f)
    prev = lambda i: jnp.maximum(i - 1, 0)
    own = blk(lambda b, r, i: (b, i, r))
    tab_spec = pl.BlockSpec((None, H, P, 2 * P), lambda b, r, i: (jnp.minimum(i, 1), 0, 0, 0))
    in_specs = [own,
                blk(lambda b, r, i: (b, prev(i), 2 * r)), blk(lambda b, r, i: (b, i, 2 * r)),
                blk(lambda b, r, i: (b, prev(i), 2 * r + 1)), blk(lambda b, r, i: (b, i, 2 * r + 1)),
                own, own, own, tab_spec]
    kvv = view(kv, 2 * D)
    args = [view(q, D), kvv, kvv, kvv, kvv, view(do, D), view(o, D), view(lse, D), tab]
    if has_acc:
        in_specs.append(own)
        args.append(view(dq_acc, D))
    dq, dtab = pl.pallas_call(
        body, name=name, grid=(B, d, nb), in_specs=in_specs,
        out_specs=[own, pl.BlockSpec((H, P, 2 * P), lambda b, r, i: (0, 0, 0))],
        out_shape=[jax.ShapeDtypeStruct((B, L, d * D), F32), jax.ShapeDtypeStruct((H, P, 2 * P), F32)],
        compiler_params=pltpu.CompilerParams(
            dimension_semantics=("arbitrary", "arbitrary", "arbitrary"),
            vmem_limit_bytes=_vmem(2 * _nbytes((H, P, 2 * P), F32) + 12 * _nbytes((P, D), F32))),
    )(*args)
    return dq.reshape(B, S, D), dtab


def _attn_bwd_kv(q, kv, do, o, lse, tabk, dkv_acc, dilation, name):
    B, S, D = q.shape
    P, d = ATT_BLOCK, dilation
    L = S // d
    nb = L // P
    H = N_HEADS
    scale = (D // H) ** -0.5
    view = lambda t, c: t.reshape(B, L, d * c)
    has_acc = dkv_acc is not None

    def body(k_ref, v_ref, qa_ref, qb_ref, doa_ref, dob_ref, oa_ref, ob_ref, la_ref, lb_ref, tab_ref, *rest):
        if has_acc:
            acc_ref, dkv_ref = rest
        else:
            (dkv_ref,) = rest
        lo, hi = _lane_masks()
        for p in range(H // 2):
            sl = slice(p * LANE, (p + 1) * LANE)
            kp, vp = k_ref[:, sl], v_ref[:, sl]
            q2 = jnp.concatenate([qa_ref[:, sl], qb_ref[:, sl]], axis=0)
            do2 = jnp.concatenate([doa_ref[:, sl], dob_ref[:, sl]], axis=0)
            o2 = jnp.concatenate([oa_ref[:, sl], ob_ref[:, sl]], axis=0)
            lse2 = jnp.concatenate([la_ref[:, sl], lb_ref[:, sl]], axis=0)
            prod = do2.astype(F32) * o2.astype(F32)
            dk_pair = None
            dv_pair = None
            for e, msk in enumerate((lo, hi)):
                qm = jnp.where(msk, q2, 0)
                dom = jnp.where(msk, do2, 0).astype(BF16)
                s = _dot_nt(qm, kp) * scale + tab_ref[2 * p + e]
                pe = jnp.exp(s - lse2[:, e * (LANE // 2):e * (LANE // 2) + 1])
                dp = _dot_nt(dom, vp)
                delta = jnp.sum(jnp.where(msk, prod, 0.0), axis=-1, keepdims=True)
                ds = pe * (dp - delta)
                dv_e = _dot_tn(pe.astype(BF16), dom)
                dk_e = _dot_tn(ds.astype(BF16), qm)
                dk_pair = dk_e if e == 0 else dk_pair + dk_e
                dv_pair = dv_e if e == 0 else dv_pair + dv_e
            dk_pair = dk_pair * scale
            if has_acc:
                dk_pair = dk_pair + acc_ref[0, :, sl]
                dv_pair = dv_pair + acc_ref[1, :, sl]
            dkv_ref[0, :, sl] = dk_pair
            dkv_ref[1, :, sl] = dv_pair

    blk = lambda f: pl.BlockSpec((None, P, D), f)
    nxt = lambda i: jnp.minimum(i + 1, nb - 1)
    own = blk(lambda b, r, i: (b, i, r))
    nx = blk(lambda b, r, i: (b, nxt(i), r))
    tab_spec = pl.BlockSpec((None, H, 2 * P, P), lambda b, r, i: (jnp.where(i == nb - 1, 1, 0), 0, 0, 0))
    pair = pl.BlockSpec((2, None, P, D), lambda b, r, i: (0, b, i, r))
    in_specs = [blk(lambda b, r, i: (b, i, 2 * r)), blk(lambda b, r, i: (b, i, 2 * r + 1)),
                own, nx, own, nx, own, nx, own, nx, tab_spec]
    kvv, qv, dov, ov, lv = view(kv, 2 * D), view(q, D), view(do, D), view(o, D), view(lse, D)
    args = [kvv, kvv, qv, qv, dov, dov, ov, ov, lv, lv, tabk]
    if has_acc:
        in_specs.append(pair)
        args.append(dkv_acc.reshape(2, B, L, d * D))
    dkv = pl.pallas_call(
        body, name=name, grid=(B, d, nb), in_specs=in_specs, out_specs=pair,
        out_shape=jax.ShapeDtypeStruct((2, B, L, d * D), F32),
        compiler_params=pltpu.CompilerParams(
            dimension_semantics=("parallel", "parallel", "parallel"),
            vmem_limit_bytes=_vmem(_nbytes((H, P, 2 * P), F32) + 16 * _nbytes((P, D), F32))),
    )(*args)
    return dkv.reshape(2, B, S, D)


def _mesh_place():
    x, y, c = lax.axis_index("x"), lax.axis_index("y"), lax.axis_index("c")
    return x, y, c, (x, y, 1 - c), [(1 - x, y), (x, 1 - y), (1 - x, 1 - y)]


def _rdma(src, dst, send_sems, recv_sems, idx, to):
    return pltpu.make_async_remote_copy(
        src_ref=src, dst_ref=dst, send_sem=send_sems.at[idx], recv_sem=recv_sems.at[idx],
        device_id=to, device_id_type=pl.DeviceIdType.MESH)


def _comm_call(body, name, args, out_shape, n_sems, n_local):
    any_spec = pl.BlockSpec(memory_space=pl.ANY)
    return pl.pallas_call(
        body, name=name, in_specs=[any_spec] * len(args), out_specs=[any_spec] * len(out_shape),
        out_shape=out_shape,
        scratch_shapes=[pltpu.SemaphoreType.DMA(n_sems), pltpu.SemaphoreType.DMA(n_sems),
                        pltpu.SemaphoreType.DMA((n_local,))],
        compiler_params=pltpu.CompilerParams(has_side_effects=True),
    )(*args)


def _all_gather(arrays, name):
    n = len(arrays)

    def body(*refs):
        ins, outs = refs[:n], refs[n:2 * n]
        send_sems, recv_sems, loc_sems = refs[2 * n:]
        x, y, c, sib, chips = _mesh_place()
        me = (x, y, c)

        def rows(a, dev):
            return outs[a].at[4 * dev[0] + 2 * dev[1] + dev[2]]

        def copy(a, k, block, to, src=None):
            return _rdma(rows(a, block) if src is None else src, rows(a, block), send_sems, recv_sems, (a, k), to)

        local, first, passed = [], [], []
        for a in range(n):
            cp = pltpu.make_async_copy(ins[a], rows(a, me), loc_sems.at[a])
            cp.start()
            local.append(cp)
            first.append(copy(a, 0, me, sib, src=ins[a]))
            first += [copy(a, 1 + j, me, (*chip, c), src=ins[a]) for j, chip in enumerate(chips)]
        for cp in first:
            cp.start()
        for j, chip in enumerate(chips):
            for a in range(n):
                copy(a, 1 + j, (*chip, c), me).wait_recv()
                cp = copy(a, 4 + j, (*chip, c), sib)
                cp.start()
                passed.append(cp)
        for a in range(n):
            copy(a, 0, sib, me).wait_recv()
            for j, chip in enumerate(chips):
                copy(a, 4 + j, (*chip, 1 - c), me).wait_recv()
        for cp in first + passed:
            cp.wait_send()
        for cp in local:
            cp.wait()

    out_shape = [jax.ShapeDtypeStruct((N_DEV,) + g.shape, g.dtype) for g in arrays]
    return list(_comm_call(body, name, arrays, out_shape, (n, N_DEV - 1), n))


def _core_pair_exchange(halves, gathered, name):
    nh, ng = len(halves), len(gathered)

    def body(*refs):
        h_in, g_in = refs[:nh], refs[nh:nh + ng]
        outs = refs[nh + ng:nh + ng + nh + ng]
        theirs, g_out = outs[:nh], outs[nh:]
        send_sems, recv_sems, loc_sems = refs[nh + ng + nh + ng:]
        x, y, c, sib, chips = _mesh_place()
        me = 4 * x + 2 * y + c
        peers = [sib] + [(*chip, pc) for chip in chips for pc in (c, 1 - c)]
        nchip = N_DEV // 2
        gbase = nh * nchip
        local, sends = [], []
        for a in range(nh):
            for ch in range(nchip):
                cp = _rdma(h_in[a].at[ch, 1 - c], theirs[a].at[ch], send_sems, recv_sems, a * nchip + ch, sib)
                cp.start()
                sends.append(cp)
        for g in range(ng):
            cp = pltpu.make_async_copy(g_in[g], g_out[g].at[me], loc_sems.at[g])
            cp.start()
            local.append(cp)
            for k, peer in enumerate(peers):
                cp = _rdma(g_in[g], g_out[g].at[me], send_sems, recv_sems, gbase + g * (N_DEV - 1) + k, peer)
                cp.start()
                sends.append(cp)
        for cp in sends:
            cp.wait_send()
        for a in range(nh):
            for ch in range(nchip):
                _rdma(h_in[a].at[ch, 1 - c], theirs[a].at[ch], send_sems, recv_sems, a * nchip + ch, sib).wait_recv()
        for g in range(ng):
            for k, peer in enumerate(peers):
                pid = 4 * peer[0] + 2 * peer[1] + peer[2]
                _rdma(g_in[g], g_out[g].at[pid], send_sems, recv_sems, gbase + g * (N_DEV - 1) + k, peer).wait_recv()
        for cp in local:
            cp.wait()

    half = [jax.ShapeDtypeStruct((h.shape[0],) + h.shape[2:], h.dtype) for h in halves]
    out_shape = half + [jax.ShapeDtypeStruct((N_DEV,) + g.shape, g.dtype) for g in gathered]
    outs = _comm_call(body, name, list(halves) + list(gathered), out_shape,
                      (nh * (N_DEV // 2) + ng * (N_DEV - 1),), max(ng, 1))
    return list(outs[:nh]), list(outs[nh:])


def _chip_exchange(arrays, name):
    n = len(arrays)

    def body(*refs):
        ins, outs = refs[:n], refs[n:2 * n]
        send_sems, recv_sems, _ = refs[2 * n:]
        x, y, c, sib, chips = _mesh_place()
        sends = []
        for a in range(n):
            for j, chip in enumerate(chips):
                cp = _rdma(ins[a].at[2 * chip[0] + chip[1]], outs[a].at[j], send_sems, recv_sems, (a, j), (*chip, c))
                cp.start()
                sends.append(cp)
        for cp in sends:
            cp.wait_send()
        for a in range(n):
            for j, chip in enumerate(chips):
                _rdma(ins[a].at[j], outs[a].at[j], send_sems, recv_sems, (a, j), (*chip, c)).wait_recv()

    out_shape = [jax.ShapeDtypeStruct((3,) + g.shape[1:], g.dtype) for g in arrays]
    return list(_comm_call(body, name, arrays, out_shape, (n, 3), 1))


def _gather_copies(src_refs, land_refs, send_sems, recv_sems):
    x, y, c, sib, chips = _mesh_place()
    me = 4 * x + 2 * y + c
    peers = [sib] + [(*chip, c) for chip in chips]
    return [pltpu.make_async_remote_copy(
                src_ref=src, dst_ref=land.at[me], send_sem=send_sems[4 * a + k], recv_sem=recv_sems[4 * a + k],
                device_id=peer, device_id_type=pl.DeviceIdType.MESH)
            for a, (src, land) in enumerate(zip(src_refs, land_refs)) for k, peer in enumerate(peers)]


def _forward_to_sibling(gathered, name):
    n = len(gathered)

    def body(*refs):
        outs = refs[n:2 * n]
        send_sems, recv_sems, _ = refs[2 * n:]
        x, y, c, sib, chips = _mesh_place()
        sends = []
        for a in range(n):
            for j, chip in enumerate(chips):
                rows = outs[a].at[4 * chip[0] + 2 * chip[1] + c]
                cp = _rdma(rows, rows, send_sems, recv_sems, (a, j), sib)
                cp.start()
                sends.append(cp)
        for cp in sends:
            cp.wait_send()
        for a in range(n):
            for j, chip in enumerate(chips):
                rows = outs[a].at[4 * chip[0] + 2 * chip[1] + (1 - c)]
                _rdma(rows, rows, send_sems, recv_sems, (a, j), sib).wait_recv()

    any_spec = pl.BlockSpec(memory_space=pl.ANY)
    return list(pl.pallas_call(
        body, name=name, in_specs=[any_spec] * n, out_specs=[any_spec] * n,
        out_shape=[jax.ShapeDtypeStruct(g.shape, g.dtype) for g in gathered],
        input_output_aliases={i: i for i in range(n)},
        scratch_shapes=[pltpu.SemaphoreType.DMA((n, 3)), pltpu.SemaphoreType.DMA((n, 3)),
                        pltpu.SemaphoreType.DMA((1,))],
        compiler_params=pltpu.CompilerParams(has_side_effects=True),
    )(*gathered))


def _split_start(copies_fn, srcs, land_shapes, n_sems, name, after=None):
    ns, nl = len(srcs), len(land_shapes)
    nc = n_sems
    hbm = pl.BlockSpec(memory_space=pltpu.HBM)
    sem = pl.BlockSpec(memory_space=pltpu.SEMAPHORE)

    n_in = ns + nl + (0 if after is None else 1)

    def body(*refs):
        src_refs, land_refs = refs[:ns], refs[ns:ns + nl]
        send_sems, recv_sems = refs[n_in:n_in + nc], refs[n_in + nc:n_in + 2 * nc]
        token = refs[-1]
        for cp in copies_fn(src_refs, land_refs, send_sems, recv_sems):
            cp.start()
        token[...] = jnp.zeros_like(token)

    lands = [lax.empty(s.shape, s.dtype) for s in land_shapes]
    thru = [pltpu.HBM(t.shape, t.dtype) for t in list(srcs) + lands]
    order = [] if after is None else [after]
    outs = pl.pallas_call(
        body, name=name,
        out_shape=(*([pltpu.SemaphoreType.DMA(())] * (2 * nc)), *thru, jax.ShapeDtypeStruct((8, LANE), F32)),
        in_specs=[hbm] * (ns + nl) + [pl.BlockSpec(memory_space=pl.ANY)] * len(order),
        out_specs=(*([sem] * (2 * nc)), *([hbm] * (ns + nl)), pl.BlockSpec(memory_space=pltpu.VMEM)),
        input_output_aliases={i: 2 * nc + i for i in range(ns + nl)},
        compiler_params=pltpu.CompilerParams(has_side_effects=pltpu.SideEffectType.DATAFLOW_SIDE_EFFECTING),
    )(*[pltpu.with_memory_space_constraint(t, pltpu.HBM) for t in list(srcs) + lands], *order)
    thru_out = outs[2 * nc:2 * nc + ns + nl]
    return list(outs[:nc]), list(outs[nc:2 * nc]), list(thru_out[:ns]), list(thru_out[ns:]), outs[-1]


def _split_wait(copies_fn, send_sems, recv_sems, srcs, lands, after, name):
    ns, nl, nc = len(srcs), len(lands), len(send_sems)
    hbm = pl.BlockSpec(memory_space=pltpu.HBM)
    sem = pl.BlockSpec(memory_space=pltpu.SEMAPHORE)

    def body(*refs):
        src_refs, land_refs = refs[:ns], refs[ns:ns + nl]
        send_refs, recv_refs = refs[ns + nl:ns + nl + nc], refs[ns + nl + nc:ns + nl + 2 * nc]
        for cp in copies_fn(src_refs, land_refs, send_refs, recv_refs):
            cp.wait_send()
            cp.wait_recv()

    outs = pl.pallas_call(
        body, name=name,
        out_shape=[pltpu.HBM(t.shape, t.dtype) for t in list(srcs) + list(lands)],
        in_specs=[hbm] * (ns + nl) + [sem] * (2 * nc) + [pl.BlockSpec(memory_space=pl.ANY)],
        out_specs=[hbm] * (ns + nl),
        input_output_aliases={i: i for i in range(ns + nl)},
        compiler_params=pltpu.CompilerParams(has_side_effects=pltpu.SideEffectType.DATAFLOW_SIDE_EFFECTING),
    )(*srcs, *lands, *send_sems, *recv_sems, after)
    return list(outs[:ns]), list(outs[ns:])


def _pair_copies(src_refs, land_refs, send_sems, recv_sems):
    x, y, c, sib, chips = _mesh_place()
    nchip = N_DEV // 2
    return [pltpu.make_async_remote_copy(
                src_ref=src.at[ch, 1 - c], dst_ref=land.at[ch], send_sem=send_sems[nchip * a + ch],
                recv_sem=recv_sems[nchip * a + ch], device_id=sib, device_id_type=pl.DeviceIdType.MESH)
            for a, (src, land) in enumerate(zip(src_refs, land_refs)) for ch in range(nchip)]


def _chip_copies(src_refs, land_refs, send_sems, recv_sems):
    x, y, c, sib, chips = _mesh_place()
    return [pltpu.make_async_remote_copy(
                src_ref=src.at[2 * chip[0] + chip[1]], dst_ref=land.at[j], send_sem=send_sems[3 * a + j],
                recv_sem=recv_sems[3 * a + j], device_id=(*chip, c), device_id_type=pl.DeviceIdType.MESH)
            for a, (src, land) in enumerate(zip(src_refs, land_refs)) for j, chip in enumerate(chips)]


def _pair_sum(halves, theirs, core, name, tr=256):
    nchip, _, R, C = halves.shape
    tr = _row_tile(R, tr)

    def body(core_ref, h_ref, t_ref, o_ref):
        o_ref[...] = (h_ref[...] + t_ref[...]).astype(BF16)

    grid_spec = pltpu.PrefetchScalarGridSpec(
        num_scalar_prefetch=1, grid=(nchip, R // tr),
        in_specs=[pl.BlockSpec((None, None, tr, C), lambda ch, i, core_ref: (ch, core_ref[0], i, 0)),
                  pl.BlockSpec((None, tr, C), lambda ch, i, core_ref: (ch, i, 0))],
        out_specs=pl.BlockSpec((None, tr, C), lambda ch, i, core_ref: (ch, i, 0)))
    return pl.pallas_call(
        body, name=name, grid_spec=grid_spec, out_shape=jax.ShapeDtypeStruct((nchip, R, C), BF16),
        compiler_params=pltpu.CompilerParams(
            dimension_semantics=("parallel", "parallel"), vmem_limit_bytes=_vmem(3 * _nbytes((tr, C), F32))),
    )(core, *[pltpu.with_memory_space_constraint(t, pltpu.HBM) for t in (halves, theirs)])


def _chip_sum(own, recv, chip, name, tr=256):
    _, R, C = own.shape
    tr = _row_tile(R, tr)

    def body(chip_ref, o_ref, r_ref, out_ref):
        acc = o_ref[...].astype(F32)
        for j in range(3):
            acc = acc + r_ref[j].astype(F32)
        out_ref[...] = acc

    grid_spec = pltpu.PrefetchScalarGridSpec(
        num_scalar_prefetch=1, grid=(R // tr,),
        in_specs=[pl.BlockSpec((None, tr, C), lambda i, chip_ref: (chip_ref[0], i, 0)),
                  pl.BlockSpec((3, tr, C), lambda i, chip_ref: (0, i, 0))],
        out_specs=pl.BlockSpec((tr, C), lambda i, chip_ref: (i, 0)))
    return pl.pallas_call(
        body, name=name, grid_spec=grid_spec, out_shape=jax.ShapeDtypeStruct((R, C), F32),
        compiler_params=pltpu.CompilerParams(
            dimension_semantics=("parallel",), vmem_limit_bytes=_vmem(4 * _nbytes((tr, C), F32))),
    )(chip, *[pltpu.with_memory_space_constraint(t, pltpu.HBM) for t in (own, recv)])


def _row_tile(rows, target):
    best = rows
    for t in range(8, min(rows, target) + 1, 8):
        if rows % t == 0:
            best = t
    return best


def _sum_slots(recv, name, tr=128):
    n, R, C = recv.shape
    tr = _row_tile(R, tr if recv.dtype == F32 else 2 * tr)

    def body(r_ref, o_ref):
        acc = r_ref[0].astype(F32)
        for k in range(1, n):
            acc = acc + r_ref[k].astype(F32)
        o_ref[...] = acc

    return pl.pallas_call(
        body, name=name, grid=(R // tr,),
        in_specs=[pl.BlockSpec((n, tr, C), lambda i: (0, i, 0))],
        out_specs=pl.BlockSpec((tr, C), lambda i: (i, 0)),
        out_shape=jax.ShapeDtypeStruct((R, C), F32),
        compiler_params=pltpu.CompilerParams(
            dimension_semantics=("parallel",), vmem_limit_bytes=_vmem(10 * _nbytes((tr, C), F32))),
    )(recv)


def _adamw_math(g_ref, w_ref, m_ref, v_ref, d_ref, nm_ref, nv_ref):
    gv = g_ref[...]
    nm = ADAM_B1 * m_ref[...] + (1.0 - ADAM_B1) * gv
    nv = ADAM_B2 * v_ref[...] + (1.0 - ADAM_B2) * (gv * gv)
    m_hat = nm / (1.0 - ADAM_B1 ** ADAM_STEP)
    v_hat = nv / (1.0 - ADAM_B2 ** ADAM_STEP)
    d_ref[...] = -ADAM_LR * (m_hat / (jnp.sqrt(v_hat) + ADAM_EPS) + ADAM_WD * w_ref[...])
    nm_ref[...] = nm
    nv_ref[...] = nv


def _adamw_small(params, name):
    n = len(params)
    two_d = [(-1, w.shape[-1]) for w, _, _, _ in params]
    flat = [t.reshape(two_d[i]) for i, (w, m, v, g) in enumerate(params) for t in (g, w, m, v)]

    def body(*refs):
        ins, outs = refs[:4 * n], refs[4 * n:]
        for i in range(n):
            _adamw_math(*ins[4 * i:4 * i + 4], *outs[3 * i:3 * i + 3])

    outs = pl.pallas_call(
        body, name=name,
        out_shape=[jax.ShapeDtypeStruct(flat[4 * i].shape, F32) for i in range(n) for _ in range(3)],
    )(*flat)
    return [(g.reshape(w.shape),) + tuple(outs[3 * i + t].reshape(w.shape) for t in range(3))
            for i, (w, m, v, g) in enumerate(params)]


def _adamw(g, w, m, v, name, tr=256):
    R, C = g.shape
    tr = _row_tile(R, tr)
    body = functools.partial(_adamw_math)

    row = pl.BlockSpec((tr, C), lambda i: (i, 0))
    return pl.pallas_call(
        body, name=name, grid=(R // tr,), in_specs=[row] * 4, out_specs=[row] * 3,
        out_shape=[jax.ShapeDtypeStruct((R, C), F32)] * 3,
        compiler_params=pltpu.CompilerParams(
            dimension_semantics=("parallel",), vmem_limit_bytes=_vmem(8 * _nbytes((tr, C), F32))),
    )(*[pltpu.with_memory_space_constraint(t, pltpu.HBM) for t in (g, w, m, v)])


def _pack_rows(parts):
    flat = jnp.concatenate([p.reshape(-1).astype(F32) for p in parts])
    rows = -(-flat.shape[0] // (8 * LANE)) * 8
    return jnp.pad(flat, (0, rows * LANE - flat.shape[0])).reshape(rows, LANE)


def _unpack(flat, shapes):
    out, off = [], 0
    for shp in shapes:
        n = math.prod(shp)
        out.append(flat[off:off + n].reshape(shp))
        off += n
    return out


def kernel(x, a_norm, a_w_in, a_conv, a_w_out, kv_norm, w_kv, b_norm, b_w_q, b_w_o, rel_bias, ffn_norm, ffn_w_up, ffn_conv, ffn_conv_b, ffn_w_down, final_norm, loss_target, m_a_norm, m_a_w_in, m_a_conv, m_a_w_out, m_kv_norm, m_w_kv, m_b_norm, m_b_w_q, m_b_w_o, m_rel_bias, m_ffn_norm, m_ffn_w_up, m_ffn_conv, m_ffn_conv_b, m_ffn_w_down, m_final_norm, v_a_norm, v_a_w_in, v_a_conv, v_a_w_out, v_kv_norm, v_w_kv, v_b_norm, v_b_w_q, v_b_w_o, v_rel_bias, v_ffn_norm, v_ffn_w_up, v_ffn_conv, v_ffn_conv_b, v_ffn_w_down, v_final_norm):
    B, S, D = x.shape
    T = B * S
    F = ffn_w_down.shape[1] * N_DEV
    me = 4 * lax.axis_index("x") + 2 * lax.axis_index("y") + lax.axis_index("c")

    big_shards = [a_w_in[0].T, a_w_out[0], w_kv.T, b_w_q[0], b_w_o[0],
                  ffn_w_up[0].T, ffn_w_up[1].T, ffn_w_down[0], ffn_w_down[1]]
    small_shapes = [a_norm.shape, a_conv.shape, ffn_conv.shape]
    small_pack = _pack_rows([a_norm, a_conv, ffn_conv])
    shards = [s.astype(BF16) for s in big_shards]
    first = _all_gather([shards[0], shards[1], small_pack], "gather_weights_first")
    win_t, wout = first[0].reshape(-1, D), first[1].reshape(-1, D)
    smalls = [_unpack(first[2][j].reshape(-1), small_shapes) for j in range(N_DEV)]
    a_norm_f = jnp.concatenate([s[0] for s in smalls], axis=-1)
    a_conv_f = jnp.concatenate([s[1] for s in smalls], axis=-1)[0]
    ffn_conv_f = jnp.concatenate([s[2] for s in smalls], axis=-1)

    def gather_start(tag, idxs, after):
        srcs = [shards[i] for i in idxs]
        lands = [jax.ShapeDtypeStruct((N_DEV,) + s.shape, s.dtype) for s in srcs]
        return _split_start(_gather_copies, srcs, lands, 4 * len(srcs), f"gather_start_{tag}", after=after)

    def gather_finish(tag, handle, after):
        send, recv, srcs, lands, _ = handle
        srcs, lands = _split_wait(_gather_copies, send, recv, srcs, lands, after, f"gather_wait_{tag}")
        lands = _forward_to_sibling(lands, f"gather_forward_{tag}")
        return [lax.dynamic_update_slice(g, s[None], (me, 0, 0)).reshape(-1, D) for g, s in zip(lands, srcs)]

    ffn0_w = gather_start("ffn0", [5, 7], first[0])
    rest_w = gather_start("rest", [2, 3, 4, 6, 8], ffn0_w[4])

    x2 = x.reshape(T, D)
    (xn0,) = _rmsnorm_fwd(x2, a_norm_f, "a_norm_fwd")
    bch = _matmul(xn0, win_t, mode="nt", out_dtype=BF16, after=rest_w[4], name="a_in_proj").reshape(B, S, 3 * D)
    gated = _shortconv_fwd(bch, a_conv_f, "a_gate_fwd").reshape(T, D)
    h1 = _matmul(gated, wout, mode="nn", out_dtype=F32, add=x2, name="a_out_proj")
    wup0_t, wdn0 = gather_finish("ffn0", ffn0_w, h1)
    wup_t, wdn = [wup0_t, None], [wdn0, None]

    def ffn_fwd(h, l):
        (xn,) = _rmsnorm_fwd(h, ffn_norm[l:l + 1], f"ffn{l}_norm_fwd")
        u0 = _matmul(xn, wup_t[l], mode="nt", out_dtype=BF16, name=f"ffn{l}_up", tn=1408).reshape(B, S, 2 * F)
        act = _ffn_gate_fwd(u0, ffn_conv_f[l], ffn_conv_b[l:l + 1], f"ffn{l}_gate_fwd").reshape(T, F)
        out = _matmul(act, wdn[l], mode="nn", out_dtype=F32, add=h, name=f"ffn{l}_down", tn=512, tk=2816)
        return out, (xn, u0, act)

    h2, ffn0_saved = ffn_fwd(h1, 0)
    wkv_t, wq, wo, wup_t[1], wdn[1] = gather_finish("rest", rest_w, h2)
    kvn, xnb = _rmsnorm_fwd(h2, jnp.stack([kv_norm, b_norm[0]]), "kv_b_norm_fwd")
    kv = _matmul(kvn, wkv_t, mode="nt", out_dtype=F32, name="kv_proj").reshape(B, S, 2 * D)
    q = _matmul(xnb, wq, mode="nn", out_dtype=F32, name="q_proj").reshape(B, S, D)

    tables = [_band_tables(w, d) for (w, d) in DILATED_BRANCHES]
    onehots = jnp.stack([t[0] for t in tables])
    P = ATT_BLOCK
    bias_vals = _bias_lookup(rel_bias.T, onehots, "rel_bias_lookup").reshape(3, N_HEADS, P, 2 * P)
    in_cur = (jnp.arange(2 * P) >= P)[None, :]
    tabs = []
    for bi in range(3):
        band = tables[bi][1]
        gen = jnp.where(band[None], bias_vals[bi], NEG)
        fst = jnp.where((band & in_cur)[None], bias_vals[bi], NEG)
        tabs.append(jnp.stack([fst, gen]))
    tabs = jnp.stack(tabs)

    attn3, lse3 = _attn_fwd(q, kv, tabs, "attn_fwd")
    attn = attn3.reshape(T, D)
    h3 = _matmul(attn, wo, mode="nn", out_dtype=F32, add=h2, name="o_proj")
    h4, ffn1_saved = ffn_fwd(h3, 1)

    dh4, dh4b, d_final_norm, sq = _final_loss_bwd(h4, final_norm.reshape(1, D), loss_target.reshape(T, D), "loss_bwd")
    loss_part = 0.5 * jnp.sum(sq) / D

    core_idx = lax.axis_index("c").astype(jnp.int32).reshape(1)
    chip_idx = (2 * lax.axis_index("x") + lax.axis_index("y")).astype(jnp.int32).reshape(1)
    in_flight = []

    def send_group(tag, indexed_grads, small=()):
        halves = [g.reshape(N_DEV // 2, 2, g.shape[0] // N_DEV, D) for _, g in indexed_grads]
        theirs, gath = _core_pair_exchange(halves, list(small), f"exchange_core_{tag}")
        sums = [_pair_sum(h_, t_, core_idx, f"grad_core_sum_{tag}{k}")
                for k, (h_, t_) in enumerate(zip(halves, theirs))]
        lands = [jax.ShapeDtypeStruct((3,) + s.shape[1:], s.dtype) for s in sums]
        send, recv, srcs, lnd, token = _split_start(_chip_copies, sums, lands, 3 * len(sums),
                                                    f"exchange_chip_start_{tag}")
        in_flight.append((tag, [i for i, _ in indexed_grads], send, recv, srcs, lnd))
        return token, gath

    def group_start(tag, indexed_grads):
        halves = [g.reshape(N_DEV // 2, 2, g.shape[0] // N_DEV, D) for _, g in indexed_grads]
        lands = [jax.ShapeDtypeStruct((h.shape[0],) + h.shape[2:], h.dtype) for h in halves]
        send, recv, srcs, lnd, token = _split_start(_pair_copies, halves, lands, (N_DEV // 2) * len(halves),
                                                    f"exchange_core_start_{tag}")
        return (tag, [i for i, _ in indexed_grads], send, recv, srcs, lnd), token

    def group_finish(handle, after):
        tag, idxs, send, recv, srcs, lnd = handle
        halves, theirs = _split_wait(_pair_copies, send, recv, srcs, lnd, after, f"exchange_core_wait_{tag}")
        sums = [_pair_sum(h_, t_, core_idx, f"grad_core_sum_{tag}{k}")
                for k, (h_, t_) in enumerate(zip(halves, theirs))]
        lands = [jax.ShapeDtypeStruct((3,) + s.shape[1:], s.dtype) for s in sums]
        send, recv, srcs, lnd, token = _split_start(_chip_copies, sums, lands, 3 * len(sums),
                                                    f"exchange_chip_start_{tag}")
        in_flight.append((tag, idxs, send, recv, srcs, lnd))
        return token

    def ffn_bwd(dh_out, dh_out_b, h_in, saved, l, after):
        xn, u0, act = saved
        dact = _matmul(dh_out_b, wdn[l], mode="nt", out_dtype=BF16, after=after, name=f"ffn{l}_down_dx",
                       tn=1408).reshape(B, S, F)
        d_wdn = _matmul(act, dh_out_b, mode="tn", out_dtype=F32, name=f"ffn{l}_down_dw", tm=1408, tn=1024, tk=2048)
        du0, d_conv, d_conv_b = _ffn_gate_bwd(u0, dact, ffn_conv_f[l], ffn_conv_b[l:l + 1], f"ffn{l}_gate_bwd")
        du0 = du0.reshape(2, T, F)
        dxn = _matmul(du0, wup_t[l], mode="nn", out_dtype=BF16, a_parts=2, name=f"ffn{l}_up_dx", tk=2816)
        d_wup_t = _matmul(du0, xn, mode="tn", out_dtype=F32, a_parts=2, name=f"ffn{l}_up_dw", tm=1408, tn=1024, tk=2048)
        dh_in, dh_in_b, d_norm = _rmsnorm_bwd(h_in, ffn_norm[l:l + 1], [dxn], dh_out, f"ffn{l}_norm_bwd")
        return (dh_in, dh_in_b), (d_wdn, d_wup_t, d_conv, d_conv_b, d_norm)

    (dh3, dh3b), ffn1_grads = ffn_bwd(dh4, dh4b, h3, ffn1_saved, 1, None)
    group, token = group_start("ffn1", [(6, ffn1_grads[1]), (8, ffn1_grads[0])])

    dattn = _matmul(dh3b, wo, mode="nt", out_dtype=F32, after=token, name="o_proj_dx")
    token = group_finish(group, dattn)
    dattn = dattn.reshape(B, S, D)
    d_wo = _matmul(attn, dh3b, mode="tn", out_dtype=F32, after=token, name="o_proj_dw", tm=1024, tn=1024, tk=2048)
    dq, dkv, dtab = _attn_bwd(q, kv, dattn, attn3, lse3, tabs, "attn_bwd")
    dtabs = dtab.transpose(1, 0, 2, 3, 4).reshape(3, N_HEADS, P * 2 * P)
    d_rel_bias = _bias_grad(dtabs, onehots, "rel_bias_grad").T
    dq2 = dq.reshape(T, D)
    dkv2 = dkv.reshape(2, T, D)
    d_wq = _matmul(xnb, dq2, mode="tn", out_dtype=F32, name="q_proj_dw", tm=1024, tn=1024, tk=2048)
    d_wkv_t = _matmul(dkv2, kvn, mode="tn", out_dtype=F32, a_parts=2, name="kv_proj_dw", tm=1024, tn=1024, tk=2048)
    group, token = group_start("attn", [(2, d_wkv_t), (3, d_wq), (4, d_wo)])
    dxnb = _matmul(dq2, wq, mode="nt", out_dtype=BF16, after=token, name="q_proj_dx")
    dkvn = _matmul(dkv2, wkv_t, mode="nn", out_dtype=BF16, a_parts=2, name="kv_proj_dx")
    dh2, dh2b, d_kvb_norm = _rmsnorm_bwd(h2, jnp.stack([kv_norm, b_norm[0]]), [dkvn, dxnb], dh3, "kv_b_norm_bwd")
    token = group_finish(group, dh2b)

    (dh1, dh1b), ffn0_grads = ffn_bwd(dh2, dh2b, h1, ffn0_saved, 0, token)
    group, token = group_start("ffn0", [(5, ffn0_grads[1]), (7, ffn0_grads[0])])

    dgated = _matmul(dh1b, wout, mode="nt", out_dtype=BF16, after=token, name="a_out_proj_dx")
    token = group_finish(group, dgated)
    dgated = dgated.reshape(B, S, D)
    d_wout = _matmul(gated, dh1b, mode="tn", out_dtype=F32, after=token, name="a_out_proj_dw",
                     tm=1024, tn=1024, tk=2048)
    dbch, d_a_conv = _shortconv_bwd(bch, dgated, a_conv_f, "a_gate_bwd")
    dbch = dbch.reshape(3, T, D)
    dxn0 = _matmul(dbch, win_t, mode="nn", out_dtype=BF16, a_parts=3, name="a_in_proj_dx")
    d_win_t = _matmul(dbch, xn0, mode="tn", out_dtype=F32, a_parts=3, name="a_in_proj_dw", tm=1024, tn=1024, tk=2048)
    grad_x, _, d_a_norm = _rmsnorm_bwd(x2, a_norm_f, [dxn0], dh1, "a_norm_bwd")

    small_full = [d_a_norm, d_a_conv, jnp.stack([ffn0_grads[2], ffn1_grads[2]]),
                  d_kvb_norm[0], d_kvb_norm[1], d_rel_bias, jnp.concatenate([ffn0_grads[4], ffn1_grads[4]]),
                  jnp.concatenate([ffn0_grads[3], ffn1_grads[3]]), d_final_norm, loss_part]
    small_full_shapes = [(1, D), (3, D), (2, 3, 2 * F), (D,), (1, D), rel_bias.shape, (2, D), (2, 2 * F), (D,), ()]
    token, gath = send_group("a", [(0, d_win_t), (1, d_wout)], small=[_pack_rows(small_full)])
    chip_sums, recv = {}, {}
    for tag, idxs, send, rcv, srcs, lnd in in_flight:
        srcs, lnd = _split_wait(_chip_copies, send, rcv, srcs, lnd, token, f"exchange_chip_wait_{tag}")
        for i, s_, l_ in zip(idxs, srcs, lnd):
            chip_sums[i], recv[i] = s_, l_
    small_sum = _sum_slots(gath[0], "small_grad_sum").reshape(-1)
    (g_a_norm_f, g_a_conv_f, g_ffn_conv_f, g_kv_norm, g_b_norm, g_rel_bias, g_ffn_norm, g_ffn_conv_b,
     g_final_norm, loss) = _unpack(small_sum, small_full_shapes)

    def my_cols(full, width):
        return lax.dynamic_slice_in_dim(full, me * width, width, axis=full.ndim - 1)

    g_a_norm = my_cols(g_a_norm_f, D // N_DEV)
    g_a_conv = my_cols(g_a_conv_f, D // N_DEV)[None]
    g_ffn_conv = my_cols(g_ffn_conv_f, 2 * F // N_DEV)

    big_w = [(a_w_in, m_a_w_in, v_a_w_in, True), (a_w_out, m_a_w_out, v_a_w_out, False),
             (w_kv, m_w_kv, v_w_kv, True), (b_w_q, m_b_w_q, v_b_w_q, False), (b_w_o, m_b_w_o, v_b_w_o, False),
             (ffn_w_up[0], m_ffn_w_up[0], v_ffn_w_up[0], True), (ffn_w_up[1], m_ffn_w_up[1], v_ffn_w_up[1], True),
             (ffn_w_down[0], m_ffn_w_down[0], v_ffn_w_down[0], False),
             (ffn_w_down[1], m_ffn_w_down[1], v_ffn_w_down[1], False)]
    big_out = []
    for i, (w, m, v, transposed) in enumerate(big_w):
        g = _chip_sum(chip_sums[i], recv[i], chip_idx, f"grad_sum_{i}")
        if transposed:
            g = g.T
        w2, m2, v2 = (t.reshape(g.shape) for t in (w, m, v))
        delta, nm, nv = _adamw(g, w2, m2, v2, f"adamw_{i}")
        big_out.append(tuple(t.reshape(w.shape) for t in (g, delta, nm, nv)))

    def pair(i, j):
        return tuple(jnp.stack([big_out[i][t], big_out[j][t]]) for t in range(4))

    small_w = [(a_norm, m_a_norm, v_a_norm, g_a_norm), (a_conv, m_a_conv, v_a_conv, g_a_conv),
               (ffn_conv, m_ffn_conv, v_ffn_conv, g_ffn_conv), (kv_norm, m_kv_norm, v_kv_norm, g_kv_norm),
               (b_norm, m_b_norm, v_b_norm, g_b_norm), (rel_bias, m_rel_bias, v_rel_bias, g_rel_bias),
               (ffn_norm, m_ffn_norm, v_ffn_norm, g_ffn_norm),
               (ffn_conv_b, m_ffn_conv_b, v_ffn_conv_b, g_ffn_conv_b),
               (final_norm, m_final_norm, v_final_norm, g_final_norm)]
    small_out = _adamw_small(small_w, "adamw_small")

    per_weight = [small_out[0], big_out[0], small_out[1], big_out[1], small_out[3], big_out[2], small_out[4],
                  big_out[3], big_out[4], small_out[5], small_out[6], pair(5, 6), small_out[2], small_out[7],
                  pair(7, 8), small_out[8]]
    outs = [loss, grad_x.reshape(B, S, D)]
    for t in range(4):
        outs.extend(pw[t] for pw in per_weight)
    return tuple(outs)
```

```python
import functools
import math

import jax
import jax.numpy as jnp
from jax import lax
from jax.experimental import pallas as pl
from jax.experimental.pallas import tpu as pltpu

F32 = jnp.float32
BF16 = jnp.bfloat16

N_DEV = 8
N_HEADS = 16
ATT_BLOCK = 128
ATT_UNROLL = 8
DILATED_BRANCHES = ((128, 1), (512, 4), (2048, 16))
REL_BUCKETS = 32
REL_MAX_DISTANCE = 2048
RMS_EPS = 1e-6
ADAM_LR = 0.001
ADAM_B1 = 0.9
ADAM_B2 = 0.999
ADAM_EPS = 1e-08
ADAM_WD = 0.01
ADAM_STEP = 10

LANE = 128
HALO = 16
NEG = -1e30
VMEM_CAP = 56 << 20


def _vmem(block_bytes):
    return int(min(VMEM_CAP, max(32 << 20, 3 * block_bytes + (8 << 20))))


def _nbytes(shape, dtype):
    return math.prod(shape) * jnp.dtype(dtype).itemsize


def _tile(dim, target):
    best = None
    t = LANE
    while t <= min(dim, target):
        if dim % t == 0:
            best = t
        t += LANE
    return best if best is not None else dim


def _matmul(a, b, *, mode, out_dtype, name, add=None, after=None, a_parts=1, tm=1024, tn=1024, tk=1024):
    P = a_parts
    if mode == "nn":
        M, K = (a.shape[0], a.shape[1]) if P == 1 else (a.shape[1], a.shape[2] * P)
        N = b.shape[1]
    elif mode == "nt":
        assert P == 1
        M, K = a.shape
        N = b.shape[0]
    else:
        K = a.shape[0] if P == 1 else a.shape[1]
        M = a.shape[1] if P == 1 else a.shape[2] * P
        N = b.shape[1]
    tm = _tile(M // P if mode == "tn" else M, tm)
    tn = _tile(N, tn)
    tk = _tile(K // P if mode == "nn" else K, tk)
    nm, nn_, nk = M // tm, N // tn, K // tk

    if mode == "nn":
        if P == 1:
            a_spec = pl.BlockSpec((tm, tk), lambda i, j, k: (i, k))
        else:
            nkp = nk // P
            a_spec = pl.BlockSpec((None, tm, tk), lambda i, j, k: (k // nkp, i, k % nkp))
        b_spec = pl.BlockSpec((tk, tn), lambda i, j, k: (k, j))
        dims = (((1,), (0,)), ((), ()))
    elif mode == "nt":
        a_spec = pl.BlockSpec((tm, tk), lambda i, j, k: (i, k))
        b_spec = pl.BlockSpec((tn, tk), lambda i, j, k: (j, k))
        dims = (((1,), (1,)), ((), ()))
    else:
        if P == 1:
            a_spec = pl.BlockSpec((tk, tm), lambda i, j, k: (k, i))
        else:
            nmp = nm // P
            a_spec = pl.BlockSpec((None, tk, tm), lambda i, j, k: (i // nmp, k, i % nmp))
        b_spec = pl.BlockSpec((tk, tn), lambda i, j, k: (k, j))
        dims = (((0,), (0,)), ((), ()))
    o_spec = pl.BlockSpec((tm, tn), lambda i, j, k: (i, j))
    in_specs = [a_spec, b_spec]
    args = [a, b]
    if add is not None:
        in_specs.append(o_spec)
        args.append(add)
    has_add = add is not None
    if after is not None:
        in_specs.append(pl.BlockSpec(memory_space=pl.ANY))
        args.append(after)
    n_extra = len(args) - 2

    direct = nk > 1 and out_dtype == F32 and not has_add

    def body(a_ref, b_ref, *rest):
        add_ref = rest[0] if has_add else None
        o_ref = rest[n_extra]
        acc_ref = o_ref if direct or nk == 1 else rest[-1]
        k = pl.program_id(2)

        def part():
            return lax.dot_general(a_ref[...].astype(BF16), b_ref[...].astype(BF16), dims,
                                   preferred_element_type=F32)

        def finish(r):
            if has_add:
                r = r + add_ref[...].astype(F32)
            o_ref[...] = r.astype(out_dtype)

        if nk == 1:
            finish(part())
        else:
            @pl.when(k == 0)
            def _():
                acc_ref[...] = part()

            @pl.when(jnp.logical_and(k > 0, jnp.logical_or(k < nk - 1, direct)))
            def _():
                acc_ref[...] += part()

            if not direct:
                @pl.when(k == nk - 1)
                def _():
                    finish(acc_ref[...] + part())

    blk = (_nbytes((tm, tk), a.dtype) + _nbytes((tk, tn), b.dtype) + _nbytes((tm, tn), out_dtype)
           + (_nbytes((tm, tn), add.dtype) if has_add else 0)) * 2 + 3 * _nbytes((tm, tn), F32)
    return pl.pallas_call(
        body, name=name, grid=(nm, nn_, nk),
        in_specs=in_specs, out_specs=o_spec,
        out_shape=jax.ShapeDtypeStruct((M, N), out_dtype),
        scratch_shapes=[] if direct or nk == 1 else [pltpu.VMEM((tm, tn), F32)],
        compiler_params=pltpu.CompilerParams(
            dimension_semantics=("parallel", "parallel", "arbitrary"), vmem_limit_bytes=_vmem(blk)),
    )(*args)


def _rmsnorm_fwd(x, gains, name, tm=1024):
    T, D = x.shape
    n = gains.shape[0]
    tm = _tile(T, tm)

    def body(x_ref, g_ref, *o_refs):
        xv = x_ref[...]
        xhat = xv * lax.rsqrt(jnp.mean(xv * xv, axis=-1, keepdims=True) + RMS_EPS)
        for i in range(n):
            o_refs[i][...] = (xhat * g_ref[i:i + 1, :]).astype(BF16)

    row = pl.BlockSpec((tm, D), lambda i: (i, 0))
    outs = pl.pallas_call(
        body, name=name, grid=(T // tm,),
        in_specs=[row, pl.BlockSpec((n, D), lambda i: (0, 0))],
        out_specs=[row] * n,
        out_shape=[jax.ShapeDtypeStruct((T, D), BF16)] * n,
        compiler_params=pltpu.CompilerParams(
            dimension_semantics=("parallel",), vmem_limit_bytes=_vmem(4 * _nbytes((tm, D), F32))),
    )(x, gains)
    return tuple(outs)


def _rmsnorm_bwd(x, gains, dxns, dres, name, tm=1024):
    T, D = x.shape
    n = gains.shape[0]
    tm = _tile(T, tm)

    def body(x_ref, g_ref, *rest):
        dxn_refs = rest[:n]
        dres_ref, dx_ref, dxb_ref, dg_ref = rest[n:]
        xv = x_ref[...]
        rstd = lax.rsqrt(jnp.mean(xv * xv, axis=-1, keepdims=True) + RMS_EPS)
        xhat = xv * rstd
        dx = dres_ref[...]

        @pl.when(pl.program_id(0) == 0)
        def _():
            dg_ref[...] = jnp.zeros_like(dg_ref)

        for i in range(n):
            dy = dxn_refs[i][...].astype(F32)
            dg_ref[i:i + 1, :] += jnp.sum(dy * xhat, axis=0, keepdims=True)
            dxh = dy * g_ref[i:i + 1, :]
            dx = dx + rstd * (dxh - xhat * jnp.mean(dxh * xhat, axis=-1, keepdims=True))
        dx_ref[...] = dx
        dxb_ref[...] = dx.astype(BF16)

    row = pl.BlockSpec((tm, D), lambda i: (i, 0))
    par = pl.BlockSpec((n, D), lambda i: (0, 0))
    return pl.pallas_call(
        body, name=name, grid=(T // tm,),
        in_specs=[row, par] + [row] * n + [row],
        out_specs=[row, row, par],
        out_shape=[jax.ShapeDtypeStruct((T, D), F32), jax.ShapeDtypeStruct((T, D), BF16),
                   jax.ShapeDtypeStruct((n, D), F32)],
        compiler_params=pltpu.CompilerParams(
            dimension_semantics=("arbitrary",), vmem_limit_bytes=_vmem((4 + n) * _nbytes((tm, D), F32))),
    )(x, gains, *dxns, dres)


def _final_loss_bwd(h, gain, target, name, tm=1024):
    T, D = h.shape
    tm = _tile(T, tm)

    def body(h_ref, g_ref, t_ref, dh_ref, dhb_ref, dg_ref, sq_ref):
        xv = h_ref[...]
        rstd = lax.rsqrt(jnp.mean(xv * xv, axis=-1, keepdims=True) + RMS_EPS)
        xhat = xv * rstd
        err = xhat * g_ref[...] - t_ref[...]

        @pl.when(pl.program_id(0) == 0)
        def _():
            dg_ref[...] = jnp.zeros_like(dg_ref)
            sq_ref[...] = jnp.zeros_like(sq_ref)

        sq_ref[...] += jnp.sum(err * err, axis=0, keepdims=True)
        dy = err * (1.0 / D)
        dg_ref[...] += jnp.sum(dy * xhat, axis=0, keepdims=True)
        dxh = dy * g_ref[...]
        dh = rstd * (dxh - xhat * jnp.mean(dxh * xhat, axis=-1, keepdims=True))
        dh_ref[...] = dh
        dhb_ref[...] = dh.astype(BF16)

    row = pl.BlockSpec((tm, D), lambda i: (i, 0))
    par = pl.BlockSpec((1, D), lambda i: (0, 0))
    return pl.pallas_call(
        body, name=name, grid=(T // tm,),
        in_specs=[row, par, row], out_specs=[row, row, par, par],
        out_shape=[jax.ShapeDtypeStruct((T, D), F32), jax.ShapeDtypeStruct((T, D), BF16),
                   jax.ShapeDtypeStruct((1, D), F32), jax.ShapeDtypeStruct((1, D), F32)],
        compiler_params=pltpu.CompilerParams(
            dimension_semantics=("arbitrary",), vmem_limit_bytes=_vmem(5 * _nbytes((tm, D), F32))),
    )(h, gain, target)


def _halo_specs(S, ts, tc, col_off, prev, nxt):
    r = ts // HALO
    last = S // HALO - 1
    specs = []
    if prev:
        specs.append(pl.BlockSpec((None, HALO, tc), lambda j, b, s: (b, jnp.maximum(s * r - 1, 0), col_off + j)))
    specs.append(pl.BlockSpec((None, ts, tc), lambda j, b, s: (b, s, col_off + j)))
    if nxt:
        specs.append(pl.BlockSpec((None, HALO, tc), lambda j, b, s: (b, jnp.minimum((s + 1) * r, last), col_off + j)))
    return specs


def _ext(prev_ref, main_ref, next_ref, first, last):
    main = main_ref[...].astype(F32)
    zeros = jnp.zeros((HALO, main.shape[1]), F32)
    top = zeros if prev_ref is None else jnp.where(first, 0.0, prev_ref[...].astype(F32))
    bot = zeros if next_ref is None else jnp.where(last, 0.0, next_ref[...].astype(F32))
    return jnp.concatenate([top, main, bot], axis=0)


def _shift(xe, k):
    return pltpu.roll(xe, k % xe.shape[0], axis=0)


def _cconv(xe, w):
    return w[0:1, :] * _shift(xe, 2) + w[1:2, :] * _shift(xe, 1) + w[2:3, :] * xe


def _main(xe, ts):
    return xe[HALO:HALO + ts, :]


def _cconv_t(de, w):
    return w[2:3, :] * de + w[1:2, :] * _shift(de, -1) + w[0:1, :] * _shift(de, -2)


def _conv_wgrad(dw_ref, de, xe, ts):
    d = _main(de, ts)
    dw_ref[0:1, :] += jnp.sum(d * _main(_shift(xe, 2), ts), axis=0, keepdims=True)
    dw_ref[1:2, :] += jnp.sum(d * _main(_shift(xe, 1), ts), axis=0, keepdims=True)
    dw_ref[2:3, :] += jnp.sum(d * _main(xe, ts), axis=0, keepdims=True)


def _gate_tiles(S, C, rows=1024):
    return _tile(S, rows), _tile(C, 512)


def _shortconv_fwd(bch, conv_w, name):
    B, S, D3 = bch.shape
    D = D3 // 3
    ts, tc = _gate_tiles(S, D, rows=2048)
    nj, ns = D // tc, S // ts

    def body(b_ref, cp_ref, c_ref, hp_ref, h_ref, w_ref, o_ref):
        first = pl.program_id(2) == 0
        ce = _ext(cp_ref, c_ref, None, first, False)
        he = _ext(hp_ref, h_ref, None, first, False)
        cv = _main(_cconv(ce * he, w_ref[...]), ts)
        o_ref[...] = (b_ref[...].astype(F32) * cv).astype(BF16)

    in_specs = (_halo_specs(S, ts, tc, 0, False, False) + _halo_specs(S, ts, tc, nj, True, False)
                + _halo_specs(S, ts, tc, 2 * nj, True, False) + [pl.BlockSpec((3, tc), lambda j, b, s: (0, j))])
    return pl.pallas_call(
        body, name=name, grid=(nj, B, ns), in_specs=in_specs,
        out_specs=pl.BlockSpec((None, ts, tc), lambda j, b, s: (b, s, j)),
        out_shape=jax.ShapeDtypeStruct((B, S, D), BF16),
        compiler_params=pltpu.CompilerParams(
            dimension_semantics=("parallel", "parallel", "parallel"),
            vmem_limit_bytes=_vmem(12 * _nbytes((ts + 2 * HALO, tc), F32))),
    )(bch, bch, bch, bch, bch, conv_w)


def _shortconv_bwd(bch, dg, conv_w, name):
    B, S, D3 = bch.shape
    D = D3 // 3
    ts, tc = _gate_tiles(S, D)
    nj, ns = D // tc, S // ts

    def body(b_ref, bn_ref, cp_ref, c_ref, cn_ref, hp_ref, h_ref, hn_ref, d_ref, dn_ref, w_ref, o_ref, dw_ref):
        s = pl.program_id(2)
        first, last = s == 0, s == ns - 1

        @pl.when(jnp.logical_and(pl.program_id(1) == 0, s == 0))
        def _():
            dw_ref[...] = jnp.zeros_like(dw_ref)

        w = w_ref[...]
        be = _ext(None, b_ref, bn_ref, first, last)
        ce = _ext(cp_ref, c_ref, cn_ref, first, last)
        he = _ext(hp_ref, h_ref, hn_ref, first, last)
        de = _ext(None, d_ref, dn_ref, first, last)
        ch = ce * he
        dcv = de * be
        dch = _main(_cconv_t(dcv, w), ts)
        o_ref[0] = (_main(de, ts) * _main(_cconv(ch, w), ts)).astype(BF16)
        o_ref[1] = (dch * _main(he, ts)).astype(BF16)
        o_ref[2] = (dch * _main(ce, ts)).astype(BF16)
        _conv_wgrad(dw_ref, dcv, ch, ts)

    in_specs = (_halo_specs(S, ts, tc, 0, False, True) + _halo_specs(S, ts, tc, nj, True, True)
                + _halo_specs(S, ts, tc, 2 * nj, True, True) + _halo_specs(S, ts, tc, 0, False, True)
                + [pl.BlockSpec((3, tc), lambda j, b, s: (0, j))])
    return pl.pallas_call(
        body, name=name, grid=(nj, B, ns), in_specs=in_specs,
        out_specs=[pl.BlockSpec((3, None, ts, tc), lambda j, b, s: (0, b, s, j)),
                   pl.BlockSpec((3, tc), lambda j, b, s: (0, j))],
        out_shape=[jax.ShapeDtypeStruct((3, B, S, D), BF16), jax.ShapeDtypeStruct((3, D), F32)],
        compiler_params=pltpu.CompilerParams(
            dimension_semantics=("parallel", "arbitrary", "arbitrary"),
            vmem_limit_bytes=_vmem(24 * _nbytes((ts + 2 * HALO, tc), F32))),
    )(bch, bch, bch, bch, bch, bch, bch, bch, dg, dg, conv_w)


def _sigmoid(x):
    return 1.0 / (1.0 + jnp.exp(-x))


def _ffn_gate_fwd(u0, conv_w, conv_b, name):
    B, S, F2 = u0.shape
    F = F2 // 2
    ts, tc = _gate_tiles(S, F, rows=2048)
    nj, ns = F // tc, S // ts

    def body(gp_ref, g_ref, up_ref, u_ref, wg_ref, wu_ref, bg_ref, bu_ref, o_ref):
        first = pl.program_id(2) == 0
        ug = _main(_cconv(_ext(gp_ref, g_ref, None, first, False), wg_ref[...]), ts) + bg_ref[...]
        uu = _main(_cconv(_ext(up_ref, u_ref, None, first, False), wu_ref[...]), ts) + bu_ref[...]
        o_ref[...] = (ug * _sigmoid(ug) * uu).astype(BF16)

    in_specs = (_halo_specs(S, ts, tc, 0, True, False) + _halo_specs(S, ts, tc, nj, True, False)
                + [pl.BlockSpec((3, tc), lambda j, b, s: (0, j)), pl.BlockSpec((3, tc), lambda j, b, s: (0, nj + j)),
                   pl.BlockSpec((1, tc), lambda j, b, s: (0, j)), pl.BlockSpec((1, tc), lambda j, b, s: (0, nj + j))])
    return pl.pallas_call(
        body, name=name, grid=(nj, B, ns), in_specs=in_specs,
        out_specs=pl.BlockSpec((None, ts, tc), lambda j, b, s: (b, s, j)),
        out_shape=jax.ShapeDtypeStruct((B, S, F), BF16),
        compiler_params=pltpu.CompilerParams(
            dimension_semantics=("parallel", "parallel", "parallel"),
            vmem_limit_bytes=_vmem(12 * _nbytes((ts + 2 * HALO, tc), F32))),
    )(u0, u0, u0, u0, conv_w, conv_w, conv_b, conv_b)


def _ffn_gate_bwd(u0, dact, conv_w, conv_b, name):
    B, S, F2 = u0.shape
    F = F2 // 2
    ts, tc = _gate_tiles(S, F)
    nj, ns = F // tc, S // ts

    def body(gp_ref, g_ref, gn_ref, up_ref, u_ref, un_ref, d_ref, dn_ref, wg_ref, wu_ref, bg_ref, bu_ref,
             o_ref, dwg_ref, dwu_ref, dbg_ref, dbu_ref):
        s = pl.program_id(2)
        first, last = s == 0, s == ns - 1

        @pl.when(jnp.logical_and(pl.program_id(1) == 0, s == 0))
        def _():
            for r in (dwg_ref, dwu_ref, dbg_ref, dbu_ref):
                r[...] = jnp.zeros_like(r)

        wg, wu = wg_ref[...], wu_ref[...]
        ge = _ext(gp_ref, g_ref, gn_ref, first, last)
        ue = _ext(up_ref, u_ref, un_ref, first, last)
        de = _ext(None, d_ref, dn_ref, first, last)
        ug = _cconv(ge, wg) + bg_ref[...]
        uu = _cconv(ue, wu) + bu_ref[...]
        sig = _sigmoid(ug)
        dug = de * uu * (sig * (1.0 + ug * (1.0 - sig)))
        duu = de * (ug * sig)
        o_ref[0] = _main(_cconv_t(dug, wg), ts).astype(BF16)
        o_ref[1] = _main(_cconv_t(duu, wu), ts).astype(BF16)
        _conv_wgrad(dwg_ref, dug, ge, ts)
        _conv_wgrad(dwu_ref, duu, ue, ts)
        dbg_ref[...] += jnp.sum(_main(dug, ts), axis=0, keepdims=True)
        dbu_ref[...] += jnp.sum(_main(duu, ts), axis=0, keepdims=True)

    w3 = lambda off: pl.BlockSpec((3, tc), lambda j, b, s: (0, off + j))
    w1 = lambda off: pl.BlockSpec((1, tc), lambda j, b, s: (0, off + j))
    in_specs = (_halo_specs(S, ts, tc, 0, True, True) + _halo_specs(S, ts, tc, nj, True, True)
                + _halo_specs(S, ts, tc, 0, False, True) + [w3(0), w3(nj), w1(0), w1(nj)])
    outs = pl.pallas_call(
        body, name=name, grid=(nj, B, ns), in_specs=in_specs,
        out_specs=[pl.BlockSpec((2, None, ts, tc), lambda j, b, s: (0, b, s, j)), w3(0), w3(0), w1(0), w1(0)],
        out_shape=[jax.ShapeDtypeStruct((2, B, S, F), BF16), jax.ShapeDtypeStruct((3, F), F32),
                   jax.ShapeDtypeStruct((3, F), F32), jax.ShapeDtypeStruct((1, F), F32),
                   jax.ShapeDtypeStruct((1, F), F32)],
        compiler_params=pltpu.CompilerParams(
            dimension_semantics=("parallel", "arbitrary", "arbitrary"),
            vmem_limit_bytes=_vmem(30 * _nbytes((ts + 2 * HALO, tc), F32))),
    )(u0, u0, u0, u0, u0, u0, dact, dact, conv_w, conv_w, conv_b, conv_b)
    du0, dwg, dwu, dbg, dbu = outs
    return du0, jnp.concatenate([dwg, dwu], axis=1), jnp.concatenate([dbg, dbu], axis=1)


def _t5_bucket(dist):
    max_exact = REL_BUCKETS // 2
    n = jnp.maximum(dist, 0)
    nf = jnp.maximum(n, max_exact).astype(F32)
    large = max_exact + (jnp.log(nf / max_exact) / math.log(REL_MAX_DISTANCE / max_exact)
                         * (REL_BUCKETS - max_exact)).astype(jnp.int32)
    large = jnp.minimum(large, REL_BUCKETS - 1)
    return jnp.where(n < max_exact, n, large)


def _band_tables(window, dilation):
    P = ATT_BLOCK
    qi = jnp.arange(P, dtype=jnp.int32)[:, None]
    kc = jnp.arange(2 * P, dtype=jnp.int32)[None, :]
    delta = qi + P - kc
    band = (delta >= 0) & (delta <= window // dilation)
    bucket = _t5_bucket(delta * dilation).reshape(-1)
    onehot = (bucket[None, :] == jnp.arange(REL_BUCKETS, dtype=jnp.int32)[:, None]).astype(F32)
    return onehot, band


def _bias_lookup(rel_bias_t, onehots, name):
    nb, _, Q = onehots.shape
    H = rel_bias_t.shape[0]

    def body(r_ref, oh_ref, o_ref):
        o_ref[...] = lax.dot_general(r_ref[...], oh_ref[...], (((1,), (0,)), ((), ())),
                                     precision=lax.Precision.HIGHEST, preferred_element_type=F32)

    return pl.pallas_call(
        body, name=name, grid=(nb,),
        in_specs=[pl.BlockSpec((H, REL_BUCKETS), lambda i: (0, 0)),
                  pl.BlockSpec((None, REL_BUCKETS, Q), lambda i: (i, 0, 0))],
        out_specs=pl.BlockSpec((None, H, Q), lambda i: (i, 0, 0)),
        out_shape=jax.ShapeDtypeStruct((nb, H, Q), F32),
        compiler_params=pltpu.CompilerParams(dimension_semantics=("parallel",),
                                             vmem_limit_bytes=_vmem(_nbytes((REL_BUCKETS + H, Q), F32))),
    )(rel_bias_t, onehots)


def _bias_grad(dtabs, onehots, name):
    nb, H, Q = dtabs.shape

    def body(d_ref, oh_ref, o_ref):
        @pl.when(pl.program_id(0) == 0)
        def _():
            o_ref[...] = jnp.zeros_like(o_ref)

        o_ref[...] += lax.dot_general(d_ref[...], oh_ref[...], (((1,), (1,)), ((), ())),
                                      precision=lax.Precision.HIGHEST, preferred_element_type=F32)

    return pl.pallas_call(
        body, name=name, grid=(nb,),
        in_specs=[pl.BlockSpec((None, H, Q), lambda i: (i, 0, 0)),
                  pl.BlockSpec((None, REL_BUCKETS, Q), lambda i: (i, 0, 0))],
        out_specs=pl.BlockSpec((H, REL_BUCKETS), lambda i: (0, 0)),
        out_shape=jax.ShapeDtypeStruct((H, REL_BUCKETS), F32),
        compiler_params=pltpu.CompilerParams(dimension_semantics=("arbitrary",),
                                             vmem_limit_bytes=_vmem(_nbytes((REL_BUCKETS + H, Q), F32))),
    )(dtabs, onehots)


def _lane_masks():
    lane = lax.broadcasted_iota(jnp.int32, (1, LANE), 1)
    lo = lane < LANE // 2
    return lo, jnp.logical_not(lo)


def _dot_nt(a, b):
    return lax.dot_general(a, b, (((1,), (1,)), ((), ())), preferred_element_type=F32)


def _dot_nn(a, b):
    return lax.dot_general(a, b, (((1,), (0,)), ((), ())), preferred_element_type=F32)


def _dot_tn(a, b):
    return lax.dot_general(a, b, (((0,), (0,)), ((), ())), preferred_element_type=F32)


def _block_rows(n, dilation, S):
    P = ATT_BLOCK
    nb = S // (dilation * P)
    r, i = n // nb, n % nb
    cur = pl.ds(i * (P * dilation) + r, P, stride=dilation)
    prv = pl.ds(jnp.maximum(i - 1, 0) * (P * dilation) + r, P, stride=dilation)
    return cur, prv, jnp.minimum(i, 1)


def _attn_fwd(q, kv, tabs, name):
    B, S, D = q.shape
    P, H = ATT_BLOCK, N_HEADS
    scale = (D // H) ** -0.5
    half = LANE // 2
    nsteps = S // P // ATT_UNROLL

    def body(q_ref, k_ref, v_ref, tab_ref, o_ref, lse_ref, acc_ref, m_ref, l_ref):
        lo, hi = _lane_masks()
        for bi, (_, d) in enumerate(DILATED_BRANCHES):
            def block(n, carry, bi=bi, d=d):
                rows = [_block_rows(n + j * nsteps, d, S) for j in range(ATT_UNROLL)]
                loaded = []
                for cur, prv, variant in rows:
                    qb = q_ref[cur, :].astype(BF16)
                    kw = jnp.concatenate([k_ref[prv, :], k_ref[cur, :]], axis=0).astype(BF16)
                    vw = jnp.concatenate([v_ref[prv, :], v_ref[cur, :]], axis=0).astype(BF16)
                    loaded.append((qb, kw, vw))
                results = []
                for (cur, prv, variant), (qb, kw, vw) in zip(rows, loaded):
                    m_pair = l_pair = a_pair = None
                    for e, msk in enumerate((lo, hi)):
                        s = _dot_nt(jnp.where(msk, qb, 0), kw) * scale + tab_ref[bi, variant, e]
                        m_e = jnp.max(s, axis=-1, keepdims=True)
                        pe = jnp.exp(s - m_e)
                        l_e = jnp.sum(pe, axis=-1, keepdims=True)
                        a_e = _dot_nn(pe.astype(BF16), jnp.where(msk, vw, 0))
                        m_pair = m_e if e == 0 else jnp.where(lo, m_pair, m_e)
                        l_pair = l_e if e == 0 else jnp.where(lo, l_pair, l_e)
                        a_pair = a_e if e == 0 else a_pair + a_e
                    results.append((m_pair, l_pair, a_pair))
                for (cur, prv, variant), (m_pair, l_pair, a_pair) in zip(rows, results):
                    m_ref[bi, cur, :] = m_pair
                    l_ref[bi, cur, :] = l_pair
                    acc_ref[bi, cur, :] = a_pair
                return carry

            lax.fori_loop(0, nsteps, block, 0)

        nbr = len(DILATED_BRANCHES)
        chunk = 512

        def merge(t, carry):
            rs = pl.ds(pl.multiple_of(t * chunk, chunk), chunk)
            ms = [m_ref[b, rs, :] for b in range(nbr)]
            m = functools.reduce(jnp.maximum, ms)
            ws = [jnp.exp(mb - m) for mb in ms]
            l = functools.reduce(jnp.add, [w * l_ref[b, rs, :] for b, w in enumerate(ws)])
            acc = functools.reduce(jnp.add, [w * acc_ref[b, rs, :] for b, w in enumerate(ws)])
            o_ref[rs, :] = (acc / l).astype(BF16)
            lse_ref[rs, :] = m + jnp.log(l)
            return carry

        lax.fori_loop(0, S // chunk, merge, 0)

    nl = D // LANE
    col = lambda off: pl.BlockSpec((None, S, LANE), lambda b, p: (b, 0, off + p))
    tab_spec = pl.BlockSpec((3, 2, 2, P, 2 * P), lambda b, p: (0, 0, p, 0, 0))
    return pl.pallas_call(
        body, name=name, grid=(B, H // 2), in_specs=[col(0), col(0), col(nl), tab_spec],
        out_specs=[col(0), col(0)],
        out_shape=[jax.ShapeDtypeStruct((B, S, D), BF16), jax.ShapeDtypeStruct((B, S, D), F32)],
        scratch_shapes=[pltpu.VMEM((len(DILATED_BRANCHES), S, LANE), F32)] * 3,
        compiler_params=pltpu.CompilerParams(
            dimension_semantics=("parallel", "parallel"), vmem_limit_bytes=VMEM_CAP),
    )(q, kv, kv, tabs)


def _attn_bwd(q, kv, do, o, lse, tabs, name):
    B, S, D = q.shape
    P, H = ATT_BLOCK, N_HEADS
    scale = (D // H) ** -0.5
    half = LANE // 2
    nsteps = S // P // ATT_UNROLL

    def body(q_ref, k_ref, v_ref, do_ref, o_ref, lse_ref, tab_ref, dq_ref, dkv_ref, dtab_ref, delta_ref):
        lo, hi = _lane_masks()

        @pl.when(pl.program_id(1) == 0)
        def _():
            dtab_ref[...] = jnp.zeros_like(dtab_ref)

        dkv_ref[...] = jnp.zeros_like(dkv_ref)
        prod = do_ref[...] * o_ref[...].astype(F32)
        delta_ref[...] = jnp.where(lo, jnp.sum(jnp.where(lo, prod, 0.0), axis=-1, keepdims=True),
                                   jnp.sum(jnp.where(hi, prod, 0.0), axis=-1, keepdims=True))

        for bi, (_, d) in enumerate(DILATED_BRANCHES):
            def block(n, carry, bi=bi, d=d):
                rows = [_block_rows(n + j * nsteps, d, S) for j in range(ATT_UNROLL)]
                loaded = []
                for cur, prv, variant in rows:
                    qb = q_ref[cur, :].astype(BF16)
                    kw = jnp.concatenate([k_ref[prv, :], k_ref[cur, :]], axis=0).astype(BF16)
                    vw = jnp.concatenate([v_ref[prv, :], v_ref[cur, :]], axis=0).astype(BF16)
                    dob = do_ref[cur, :].astype(BF16)
                    dq_old = dq_ref[cur, :] if bi > 0 else None
                    loaded.append((qb, kw, vw, dob, lse_ref[cur, :], delta_ref[cur, :], dq_old))
                results = []
                for j, ((cur, prv, variant), (qb, kw, vw, dob, lse_b, dl_b, dq_old)) in enumerate(zip(rows, loaded)):
                    dq_pair = dk_pair = dv_pair = None
                    for e, msk in enumerate((lo, hi)):
                        qm = jnp.where(msk, qb, 0)
                        dom = jnp.where(msk, dob, 0)
                        s = _dot_nt(qm, kw) * scale + tab_ref[bi, variant, e]
                        pe = jnp.exp(s - lse_b[:, e * half:e * half + 1])
                        ds = pe * (_dot_nt(dom, vw) - dl_b[:, e * half:e * half + 1])
                        dtab_ref[bi, e] += ds
                        dsb = ds.astype(BF16)
                        dq_e = _dot_nn(dsb, jnp.where(msk, kw, 0))
                        dk_e = _dot_tn(dsb, qm)
                        dv_e = _dot_tn(pe.astype(BF16), dom)
                        dq_pair = dq_e if e == 0 else dq_pair + dq_e
                        dk_pair = dk_e if e == 0 else dk_pair + dk_e
                        dv_pair = dv_e if e == 0 else dv_pair + dv_e
                    dq_pair = dq_pair * scale
                    if bi > 0:
                        dq_pair = dq_pair + dq_old
                    results.append((dq_pair, dk_pair * scale, dv_pair))
                for (cur, prv, variant), (dq_pair, dk_pair, dv_pair) in zip(rows, results):
                    dq_ref[cur, :] = dq_pair
                    dkv_ref[0, cur, :] += dk_pair[P:, :]
                    dkv_ref[1, cur, :] += dv_pair[P:, :]
                    dkv_ref[0, prv, :] += dk_pair[:P, :]
                    dkv_ref[1, prv, :] += dv_pair[:P, :]
                return carry

            lax.fori_loop(0, nsteps, block, 0)

    nl = D // LANE
    col = lambda off: pl.BlockSpec((None, S, LANE), lambda p, b: (b, 0, off + p))
    tab_spec = pl.BlockSpec((3, 2, 2, P, 2 * P), lambda p, b: (0, 0, p, 0, 0))
    return pl.pallas_call(
        body, name=name, grid=(H // 2, B),
        in_specs=[col(0), col(0), col(nl), col(0), col(0), col(0), tab_spec],
        out_specs=[col(0), pl.BlockSpec((2, None, S, LANE), lambda p, b: (0, b, 0, p)),
                   pl.BlockSpec((None, 3, 2, P, 2 * P), lambda p, b: (p, 0, 0, 0, 0))],
        out_shape=[jax.ShapeDtypeStruct((B, S, D), F32), jax.ShapeDtypeStruct((2, B, S, D), F32),
                   jax.ShapeDtypeStruct((H // 2, 3, 2, P, 2 * P), F32)],
        scratch_shapes=[pltpu.VMEM((S, LANE), F32)],
        compiler_params=pltpu.CompilerParams(
            dimension_semantics=("parallel", "arbitrary"),
            vmem_limit_bytes=VMEM_CAP),
    )(q, kv, kv, do, o, lse, tabs)


def _mesh_place():
    x, y, c = lax.axis_index("x"), lax.axis_index("y"), lax.axis_index("c")
    return x, y, c, (x, y, 1 - c), [(1 - x, y), (x, 1 - y), (1 - x, 1 - y)]


def _rdma(src, dst, send_sems, recv_sems, idx, to):
    return pltpu.make_async_remote_copy(
        src_ref=src, dst_ref=dst, send_sem=send_sems.at[idx], recv_sem=recv_sems.at[idx],
        device_id=to, device_id_type=pl.DeviceIdType.MESH)


def _comm_call(body, name, args, out_shape, n_sems, n_local):
    any_spec = pl.BlockSpec(memory_space=pl.ANY)
    return pl.pallas_call(
        body, name=name, in_specs=[any_spec] * len(args), out_specs=[any_spec] * len(out_shape),
        out_shape=out_shape,
        scratch_shapes=[pltpu.SemaphoreType.DMA(n_sems), pltpu.SemaphoreType.DMA(n_sems),
                        pltpu.SemaphoreType.DMA((n_local,))],
        compiler_params=pltpu.CompilerParams(has_side_effects=True),
    )(*args)


def _all_gather(arrays, name):
    n = len(arrays)

    def body(*refs):
        ins, outs = refs[:n], refs[n:2 * n]
        send_sems, recv_sems, loc_sems = refs[2 * n:]
        x, y, c, sib, chips = _mesh_place()
        me = (x, y, c)

        def rows(a, dev):
            return outs[a].at[4 * dev[0] + 2 * dev[1] + dev[2]]

        def copy(a, k, block, to, src=None):
            return _rdma(rows(a, block) if src is None else src, rows(a, block), send_sems, recv_sems, (a, k), to)

        local, first, passed = [], [], []
        for a in range(n):
            cp = pltpu.make_async_copy(ins[a], rows(a, me), loc_sems.at[a])
            cp.start()
            local.append(cp)
            first.append(copy(a, 0, me, sib, src=ins[a]))
            first += [copy(a, 1 + j, me, (*chip, c), src=ins[a]) for j, chip in enumerate(chips)]
        for cp in first:
            cp.start()
        for j, chip in enumerate(chips):
            for a in range(n):
                copy(a, 1 + j, (*chip, c), me).wait_recv()
                cp = copy(a, 4 + j, (*chip, c), sib)
                cp.start()
                passed.append(cp)
        for a in range(n):
            copy(a, 0, sib, me).wait_recv()
            for j, chip in enumerate(chips):
                copy(a, 4 + j, (*chip, 1 - c), me).wait_recv()
        for cp in first + passed:
            cp.wait_send()
        for cp in local:
            cp.wait()

    out_shape = [jax.ShapeDtypeStruct((N_DEV,) + g.shape, g.dtype) for g in arrays]
    return list(_comm_call(body, name, arrays, out_shape, (n, N_DEV - 1), n))


def _core_pair_exchange(halves, gathered, name):
    nh, ng = len(halves), len(gathered)

    def body(*refs):
        h_in, g_in = refs[:nh], refs[nh:nh + ng]
        outs = refs[nh + ng:nh + ng + nh + ng]
        theirs, g_out = outs[:nh], outs[nh:]
        send_sems, recv_sems, loc_sems = refs[nh + ng + nh + ng:]
        x, y, c, sib, chips = _mesh_place()
        me = 4 * x + 2 * y + c
        peers = [sib] + [(*chip, pc) for chip in chips for pc in (c, 1 - c)]
        nchip = N_DEV // 2
        gbase = nh * nchip
        local, sends = [], []
        for a in range(nh):
            for ch in range(nchip):
                cp = _rdma(h_in[a].at[ch, 1 - c], theirs[a].at[ch], send_sems, recv_sems, a * nchip + ch, sib)
                cp.start()
                sends.append(cp)
        for g in range(ng):
            cp = pltpu.make_async_copy(g_in[g], g_out[g].at[me], loc_sems.at[g])
            cp.start()
            local.append(cp)
            for k, peer in enumerate(peers):
                cp = _rdma(g_in[g], g_out[g].at[me], send_sems, recv_sems, gbase + g * (N_DEV - 1) + k, peer)
                cp.start()
                sends.append(cp)
        for cp in sends:
            cp.wait_send()
        for a in range(nh):
            for ch in range(nchip):
                _rdma(h_in[a].at[ch, 1 - c], theirs[a].at[ch], send_sems, recv_sems, a * nchip + ch, sib).wait_recv()
        for g in range(ng):
            for k, peer in enumerate(peers):
                pid = 4 * peer[0] + 2 * peer[1] + peer[2]
                _rdma(g_in[g], g_out[g].at[pid], send_sems, recv_sems, gbase + g * (N_DEV - 1) + k, peer).wait_recv()
        for cp in local:
            cp.wait()

    half = [jax.ShapeDtypeStruct((h.shape[0],) + h.shape[2:], h.dtype) for h in halves]
    out_shape = half + [jax.ShapeDtypeStruct((N_DEV,) + g.shape, g.dtype) for g in gathered]
    outs = _comm_call(body, name, list(halves) + list(gathered), out_shape,
                      (nh * (N_DEV // 2) + ng * (N_DEV - 1),), max(ng, 1))
    return list(outs[:nh]), list(outs[nh:])


def _gather_copies(src_refs, land_refs, send_sems, recv_sems):
    x, y, c, sib, chips = _mesh_place()
    me = 4 * x + 2 * y + c
    peers = [sib] + [(*chip, c) for chip in chips]
    return [pltpu.make_async_remote_copy(
                src_ref=src, dst_ref=land.at[me], send_sem=send_sems[4 * a + k], recv_sem=recv_sems[4 * a + k],
                device_id=peer, device_id_type=pl.DeviceIdType.MESH)
            for a, (src, land) in enumerate(zip(src_refs, land_refs)) for k, peer in enumerate(peers)]


def _forward_to_sibling(gathered, name):
    n = len(gathered)

    def body(*refs):
        outs = refs[n:2 * n]
        send_sems, recv_sems, _ = refs[2 * n:]
        x, y, c, sib, chips = _mesh_place()
        sends = []
        for a in range(n):
            for j, chip in enumerate(chips):
                rows = outs[a].at[4 * chip[0] + 2 * chip[1] + c]
                cp = _rdma(rows, rows, send_sems, recv_sems, (a, j), sib)
                cp.start()
                sends.append(cp)
        for cp in sends:
            cp.wait_send()
        for a in range(n):
            for j, chip in enumerate(chips):
                rows = outs[a].at[4 * chip[0] + 2 * chip[1] + (1 - c)]
                _rdma(rows, rows, send_sems, recv_sems, (a, j), sib).wait_recv()

    any_spec = pl.BlockSpec(memory_space=pl.ANY)
    return list(pl.pallas_call(
        body, name=name, in_specs=[any_spec] * n, out_specs=[any_spec] * n,
        out_shape=[jax.ShapeDtypeStruct(g.shape, g.dtype) for g in gathered],
        input_output_aliases={i: i for i in range(n)},
        scratch_shapes=[pltpu.SemaphoreType.DMA((n, 3)), pltpu.SemaphoreType.DMA((n, 3)),
                        pltpu.SemaphoreType.DMA((1,))],
        compiler_params=pltpu.CompilerParams(has_side_effects=True),
    )(*gathered))


def _split_start(copies_fn, srcs, land_shapes, n_sems, name, after=None):
    ns, nl = len(srcs), len(land_shapes)
    nc = n_sems
    hbm = pl.BlockSpec(memory_space=pltpu.HBM)
    sem = pl.BlockSpec(memory_space=pltpu.SEMAPHORE)

    n_in = ns + nl + (0 if after is None else 1)

    def body(*refs):
        src_refs, land_refs = refs[:ns], refs[ns:ns + nl]
        send_sems, recv_sems = refs[n_in:n_in + nc], refs[n_in + nc:n_in + 2 * nc]
        token = refs[-1]
        for cp in copies_fn(src_refs, land_refs, send_sems, recv_sems):
            cp.start()
        token[...] = jnp.zeros_like(token)

    lands = [lax.empty(s.shape, s.dtype) for s in land_shapes]
    thru = [pltpu.HBM(t.shape, t.dtype) for t in list(srcs) + lands]
    order = [] if after is None else [after]
    outs = pl.pallas_call(
        body, name=name,
        out_shape=(*([pltpu.SemaphoreType.DMA(())] * (2 * nc)), *thru, jax.ShapeDtypeStruct((8, LANE), F32)),
        in_specs=[hbm] * (ns + nl) + [pl.BlockSpec(memory_space=pl.ANY)] * len(order),
        out_specs=(*([sem] * (2 * nc)), *([hbm] * (ns + nl)), pl.BlockSpec(memory_space=pltpu.VMEM)),
        input_output_aliases={i: 2 * nc + i for i in range(ns + nl)},
        compiler_params=pltpu.CompilerParams(has_side_effects=pltpu.SideEffectType.DATAFLOW_SIDE_EFFECTING),
    )(*[pltpu.with_memory_space_constraint(t, pltpu.HBM) for t in list(srcs) + lands], *order)
    thru_out = outs[2 * nc:2 * nc + ns + nl]
    return list(outs[:nc]), list(outs[nc:2 * nc]), list(thru_out[:ns]), list(thru_out[ns:]), outs[-1]


def _split_wait(copies_fn, send_sems, recv_sems, srcs, lands, after, name):
    ns, nl, nc = len(srcs), len(lands), len(send_sems)
    hbm = pl.BlockSpec(memory_space=pltpu.HBM)
    sem = pl.BlockSpec(memory_space=pltpu.SEMAPHORE)

    def body(*refs):
        src_refs, land_refs = refs[:ns], refs[ns:ns + nl]
        send_refs, recv_refs = refs[ns + nl:ns + nl + nc], refs[ns + nl + nc:ns + nl + 2 * nc]
        for cp in copies_fn(src_refs, land_refs, send_refs, recv_refs):
            cp.wait_send()
            cp.wait_recv()

    outs = pl.pallas_call(
        body, name=name,
        out_shape=[pltpu.HBM(t.shape, t.dtype) for t in list(srcs) + list(lands)],
        in_specs=[hbm] * (ns + nl) + [sem] * (2 * nc) + [pl.BlockSpec(memory_space=pl.ANY)],
        out_specs=[hbm] * (ns + nl),
        input_output_aliases={i: i for i in range(ns + nl)},
        compiler_params=pltpu.CompilerParams(has_side_effects=pltpu.SideEffectType.DATAFLOW_SIDE_EFFECTING),
    )(*srcs, *lands, *send_sems, *recv_sems, after)
    return list(outs[:ns]), list(outs[ns:])


def _pair_copies(src_refs, land_refs, send_sems, recv_sems):
    x, y, c, sib, chips = _mesh_place()
    nchip = N_DEV // 2
    return [pltpu.make_async_remote_copy(
                src_ref=src.at[ch, 1 - c], dst_ref=land.at[ch], send_sem=send_sems[nchip * a + ch],
                recv_sem=recv_sems[nchip * a + ch], device_id=sib, device_id_type=pl.DeviceIdType.MESH)
            for a, (src, land) in enumerate(zip(src_refs, land_refs)) for ch in range(nchip)]


def _chip_copies(src_refs, land_refs, send_sems, recv_sems):
    x, y, c, sib, chips = _mesh_place()
    return [pltpu.make_async_remote_copy(
                src_ref=src.at[2 * chip[0] + chip[1]], dst_ref=land.at[j], send_sem=send_sems[3 * a + j],
                recv_sem=recv_sems[3 * a + j], device_id=(*chip, c), device_id_type=pl.DeviceIdType.MESH)
            for a, (src, land) in enumerate(zip(src_refs, land_refs)) for j, chip in enumerate(chips)]


def _pair_sum(halves, theirs, core, name, tr=256):
    nchip, _, R, C = halves.shape
    tr = _row_tile(R, tr)

    def body(core_ref, h_ref, t_ref, o_ref):
        o_ref[...] = (h_ref[...] + t_ref[...]).astype(BF16)

    grid_spec = pltpu.PrefetchScalarGridSpec(
        num_scalar_prefetch=1, grid=(nchip, R // tr),
        in_specs=[pl.BlockSpec((None, None, tr, C), lambda ch, i, core_ref: (ch, core_ref[0], i, 0)),
                  pl.BlockSpec((None, tr, C), lambda ch, i, core_ref: (ch, i, 0))],
        out_specs=pl.BlockSpec((None, tr, C), lambda ch, i, core_ref: (ch, i, 0)))
    return pl.pallas_call(
        body, name=name, grid_spec=grid_spec, out_shape=jax.ShapeDtypeStruct((nchip, R, C), BF16),
        compiler_params=pltpu.CompilerParams(
            dimension_semantics=("parallel", "parallel"), vmem_limit_bytes=_vmem(3 * _nbytes((tr, C), F32))),
    )(core, *[pltpu.with_memory_space_constraint(t, pltpu.HBM) for t in (halves, theirs)])


def _chip_sum(own, recv, chip, name, tr=256):
    _, R, C = own.shape
    tr = _row_tile(R, tr)

    def body(chip_ref, o_ref, r_ref, out_ref):
        acc = o_ref[...].astype(F32)
        for j in range(3):
            acc = acc + r_ref[j].astype(F32)
        out_ref[...] = acc

    grid_spec = pltpu.PrefetchScalarGridSpec(
        num_scalar_prefetch=1, grid=(R // tr,),
        in_specs=[pl.BlockSpec((None, tr, C), lambda i, chip_ref: (chip_ref[0], i, 0)),
                  pl.BlockSpec((3, tr, C), lambda i, chip_ref: (0, i, 0))],
        out_specs=pl.BlockSpec((tr, C), lambda i, chip_ref: (i, 0)))
    return pl.pallas_call(
        body, name=name, grid_spec=grid_spec, out_shape=jax.ShapeDtypeStruct((R, C), F32),
        compiler_params=pltpu.CompilerParams(
            dimension_semantics=("parallel",), vmem_limit_bytes=_vmem(4 * _nbytes((tr, C), F32))),
    )(chip, *[pltpu.with_memory_space_constraint(t, pltpu.HBM) for t in (own, recv)])


def _row_tile(rows, target):
    best = rows
    for t in range(8, min(rows, target) + 1, 8):
        if rows % t == 0:
            best = t
    return best


def _sum_slots(recv, name, tr=128):
    n, R, C = recv.shape
    tr = _row_tile(R, tr if recv.dtype == F32 else 2 * tr)

    def body(r_ref, o_ref):
        acc = r_ref[0].astype(F32)
        for k in range(1, n):
            acc = acc + r_ref[k].astype(F32)
        o_ref[...] = acc

    return pl.pallas_call(
        body, name=name, grid=(R // tr,),
        in_specs=[pl.BlockSpec((n, tr, C), lambda i: (0, i, 0))],
        out_specs=pl.BlockSpec((tr, C), lambda i: (i, 0)),
        out_shape=jax.ShapeDtypeStruct((R, C), F32),
        compiler_params=pltpu.CompilerParams(
            dimension_semantics=("parallel",), vmem_limit_bytes=_vmem(10 * _nbytes((tr, C), F32))),
    )(recv)


def _adamw_math(g_ref, w_ref, m_ref, v_ref, d_ref, nm_ref, nv_ref):
    gv = g_ref[...]
    nm = ADAM_B1 * m_ref[...] + (1.0 - ADAM_B1) * gv
    nv = ADAM_B2 * v_ref[...] + (1.0 - ADAM_B2) * (gv * gv)
    m_hat = nm / (1.0 - ADAM_B1 ** ADAM_STEP)
    v_hat = nv / (1.0 - ADAM_B2 ** ADAM_STEP)
    d_ref[...] = -ADAM_LR * (m_hat / (jnp.sqrt(v_hat) + ADAM_EPS) + ADAM_WD * w_ref[...])
    nm_ref[...] = nm
    nv_ref[...] = nv


def _adamw_small(params, name):
    n = len(params)
    two_d = [(-1, w.shape[-1]) for w, _, _, _ in params]
    flat = [t.reshape(two_d[i]) for i, (w, m, v, g) in enumerate(params) for t in (g, w, m, v)]

    def body(*refs):
        ins, outs = refs[:4 * n], refs[4 * n:]
        for i in range(n):
            _adamw_math(*ins[4 * i:4 * i + 4], *outs[3 * i:3 * i + 3])

    outs = pl.pallas_call(
        body, name=name,
        out_shape=[jax.ShapeDtypeStruct(flat[4 * i].shape, F32) for i in range(n) for _ in range(3)],
    )(*flat)
    return [(g.reshape(w.shape),) + tuple(outs[3 * i + t].reshape(w.shape) for t in range(3))
            for i, (w, m, v, g) in enumerate(params)]


def _adamw(g, w, m, v, name, tr=256):
    R, C = g.shape
    tr = _row_tile(R, tr)
    body = functools.partial(_adamw_math)

    row = pl.BlockSpec((tr, C), lambda i: (i, 0))
    return pl.pallas_call(
        body, name=name, grid=(R // tr,), in_specs=[row] * 4, out_specs=[row] * 3,
        out_shape=[jax.ShapeDtypeStruct((R, C), F32)] * 3,
        compiler_params=pltpu.CompilerParams(
            dimension_semantics=("parallel",), vmem_limit_bytes=_vmem(8 * _nbytes((tr, C), F32))),
    )(*[pltpu.with_memory_space_constraint(t, pltpu.HBM) for t in (g, w, m, v)])


def _pack_rows(parts):
    flat = jnp.concatenate([p.reshape(-1).astype(F32) for p in parts])
    rows = -(-flat.shape[0] // (8 * LANE)) * 8
    return jnp.pad(flat, (0, rows * LANE - flat.shape[0])).reshape(rows, LANE)


def _unpack(flat, shapes):
    out, off = [], 0
    for shp in shapes:
        n = math.prod(shp)
        out.append(flat[off:off + n].reshape(shp))
        off += n
    return out


def kernel(x, a_norm, a_w_in, a_conv, a_w_out, kv_norm, w_kv, b_norm, b_w_q, b_w_o, rel_bias, ffn_norm, ffn_w_up, ffn_conv, ffn_conv_b, ffn_w_down, final_norm, loss_target, m_a_norm, m_a_w_in, m_a_conv, m_a_w_out, m_kv_norm, m_w_kv, m_b_norm, m_b_w_q, m_b_w_o, m_rel_bias, m_ffn_norm, m_ffn_w_up, m_ffn_conv, m_ffn_conv_b, m_ffn_w_down, m_final_norm, v_a_norm, v_a_w_in, v_a_conv, v_a_w_out, v_kv_norm, v_w_kv, v_b_norm, v_b_w_q, v_b_w_o, v_rel_bias, v_ffn_norm, v_ffn_w_up, v_ffn_conv, v_ffn_conv_b, v_ffn_w_down, v_final_norm):
    B, S, D = x.shape
    T = B * S
    F = ffn_w_down.shape[1] * N_DEV
    me = 4 * lax.axis_index("x") + 2 * lax.axis_index("y") + lax.axis_index("c")

    big_shards = [a_w_in[0].T, a_w_out[0], w_kv.T, b_w_q[0], b_w_o[0],
                  ffn_w_up[0].T, ffn_w_up[1].T, ffn_w_down[0], ffn_w_down[1]]
    small_shapes = [a_norm.shape, a_conv.shape, ffn_conv.shape]
    small_pack = _pack_rows([a_norm, a_conv, ffn_conv])
    shards = [s.astype(BF16) for s in big_shards]
    first = _all_gather([shards[0], shards[1], small_pack], "gather_weights_first")
    win_t, wout = first[0].reshape(-1, D), first[1].reshape(-1, D)
    smalls = [_unpack(first[2][j].reshape(-1), small_shapes) for j in range(N_DEV)]
    a_norm_f = jnp.concatenate([s[0] for s in smalls], axis=-1)
    a_conv_f = jnp.concatenate([s[1] for s in smalls], axis=-1)[0]
    ffn_conv_f = jnp.concatenate([s[2] for s in smalls], axis=-1)

    def gather_start(tag, idxs, after):
        srcs = [shards[i] for i in idxs]
        lands = [jax.ShapeDtypeStruct((N_DEV,) + s.shape, s.dtype) for s in srcs]
        return _split_start(_gather_copies, srcs, lands, 4 * len(srcs), f"gather_start_{tag}", after=after)

    def gather_finish(tag, handle, after):
        send, recv, srcs, lands, _ = handle
        srcs, lands = _split_wait(_gather_copies, send, recv, srcs, lands, after, f"gather_wait_{tag}")
        lands = _forward_to_sibling(lands, f"gather_forward_{tag}")
        return [lax.dynamic_update_slice(g, s[None], (me, 0, 0)).reshape(-1, D) for g, s in zip(lands, srcs)]

    ffn0_w = gather_start("ffn0", [5, 7], first[0])
    rest_w = gather_start("rest", [2, 3, 4, 6, 8], ffn0_w[4])

    x2 = x.reshape(T, D)
    (xn0,) = _rmsnorm_fwd(x2, a_norm_f, "a_norm_fwd")
    bch = _matmul(xn0, win_t, mode="nt", out_dtype=BF16, after=rest_w[4], name="a_in_proj", tm=2048).reshape(B, S, 3 * D)
    gated = _shortconv_fwd(bch, a_conv_f, "a_gate_fwd").reshape(T, D)
    h1 = _matmul(gated, wout, mode="nn", out_dtype=F32, add=x2, name="a_out_proj")
    wup0_t, wdn0 = gather_finish("ffn0", ffn0_w, h1)
    wup_t, wdn = [wup0_t, None], [wdn0, None]

    def ffn_fwd(h, l):
        (xn,) = _rmsnorm_fwd(h, ffn_norm[l:l + 1], f"ffn{l}_norm_fwd")
        u0 = _matmul(xn, wup_t[l], mode="nt", out_dtype=BF16, name=f"ffn{l}_up", tm=2048, tn=1408).reshape(B, S, 2 * F)
        act = _ffn_gate_fwd(u0, ffn_conv_f[l], ffn_conv_b[l:l + 1], f"ffn{l}_gate_fwd").reshape(T, F)
        out = _matmul(act, wdn[l], mode="nn", out_dtype=F32, add=h, name=f"ffn{l}_down", tn=512, tk=2816)
        return out, (xn, u0, act)

    h2, ffn0_saved = ffn_fwd(h1, 0)
    wkv_t, wq, wo, wup_t[1], wdn[1] = gather_finish("rest", rest_w, h2)
    kvn, xnb = _rmsnorm_fwd(h2, jnp.stack([kv_norm, b_norm[0]]), "kv_b_norm_fwd")
    kv = _matmul(kvn, wkv_t, mode="nt", out_dtype=F32, name="kv_proj", tm=2048).reshape(B, S, 2 * D)
    q = _matmul(xnb, wq, mode="nn", out_dtype=F32, name="q_proj", tm=2048).reshape(B, S, D)

    tables = [_band_tables(w, d) for (w, d) in DILATED_BRANCHES]
    onehots = jnp.stack([t[0] for t in tables])
    P = ATT_BLOCK
    bias_vals = _bias_lookup(rel_bias.T, onehots, "rel_bias_lookup").reshape(3, N_HEADS, P, 2 * P)
    in_cur = (jnp.arange(2 * P) >= P)[None, :]
    masks = jnp.stack([jnp.stack([t[1] & in_cur, t[1]]) for t in tables])
    tabs = jnp.where(masks[:, :, None], bias_vals[:, None], NEG)

    attn3, lse3 = _attn_fwd(q, kv, tabs, "attn_fwd")
    attn = attn3.reshape(T, D)
    h3 = _matmul(attn, wo, mode="nn", out_dtype=F32, add=h2, name="o_proj")
    h4, ffn1_saved = ffn_fwd(h3, 1)

    dh4, dh4b, d_final_norm, sq = _final_loss_bwd(h4, final_norm.reshape(1, D), loss_target.reshape(T, D), "loss_bwd")
    loss_part = 0.5 * jnp.sum(sq) / D

    core_idx = lax.axis_index("c").astype(jnp.int32).reshape(1)
    chip_idx = (2 * lax.axis_index("x") + lax.axis_index("y")).astype(jnp.int32).reshape(1)
    in_flight = []

    def send_group(tag, indexed_grads, small=()):
        halves = [g.reshape(N_DEV // 2, 2, g.shape[0] // N_DEV, D) for _, g in indexed_grads]
        theirs, gath = _core_pair_exchange(halves, list(small), f"exchange_core_{tag}")
        sums = [_pair_sum(h_, t_, core_idx, f"grad_core_sum_{tag}{k}")
                for k, (h_, t_) in enumerate(zip(halves, theirs))]
        lands = [jax.ShapeDtypeStruct((3,) + s.shape[1:], s.dtype) for s in sums]
        send, recv, srcs, lnd, token = _split_start(_chip_copies, sums, lands, 3 * len(sums),
                                                    f"exchange_chip_start_{tag}")
        in_flight.append((tag, [i for i, _ in indexed_grads], send, recv, srcs, lnd))
        return token, gath

    def group_start(tag, indexed_grads):
        halves = [g.reshape(N_DEV // 2, 2, g.shape[0] // N_DEV, D) for _, g in indexed_grads]
        lands = [jax.ShapeDtypeStruct((h.shape[0],) + h.shape[2:], h.dtype) for h in halves]
        send, recv, srcs, lnd, token = _split_start(_pair_copies, halves, lands, (N_DEV // 2) * len(halves),
                                                    f"exchange_core_start_{tag}")
        return (tag, [i for i, _ in indexed_grads], send, recv, srcs, lnd), token

    def group_finish(handle, after):
        tag, idxs, send, recv, srcs, lnd = handle
        halves, theirs = _split_wait(_pair_copies, send, recv, srcs, lnd, after, f"exchange_core_wait_{tag}")
        sums = [_pair_sum(h_, t_, core_idx, f"grad_core_sum_{tag}{k}")
                for k, (h_, t_) in enumerate(zip(halves, theirs))]
        lands = [jax.ShapeDtypeStruct((3,) + s.shape[1:], s.dtype) for s in sums]
        send, recv, srcs, lnd, token = _split_start(_chip_copies, sums, lands, 3 * len(sums),
                                                    f"exchange_chip_start_{tag}")
        in_flight.append((tag, idxs, send, recv, srcs, lnd))
        return token

    def ffn_bwd(dh_out, dh_out_b, h_in, saved, l, after):
        xn, u0, act = saved
        dact = _matmul(dh_out_b, wdn[l], mode="nt", out_dtype=BF16, after=after, name=f"ffn{l}_down_dx",
                       tm=2048, tn=1408).reshape(B, S, F)
        d_wdn = _matmul(act, dh_out_b, mode="tn", out_dtype=F32, name=f"ffn{l}_down_dw", tm=1408, tn=1024, tk=2048)
        du0, d_conv, d_conv_b = _ffn_gate_bwd(u0, dact, ffn_conv_f[l], ffn_conv_b[l:l + 1], f"ffn{l}_gate_bwd")
        du0 = du0.reshape(2, T, F)
        dxn = _matmul(du0, wup_t[l], mode="nn", out_dtype=BF16, a_parts=2, name=f"ffn{l}_up_dx", tk=2816)
        d_wup_t = _matmul(du0, xn, mode="tn", out_dtype=F32, a_parts=2, name=f"ffn{l}_up_dw", tm=1408, tn=1024, tk=2048)
        dh_in, dh_in_b, d_norm = _rmsnorm_bwd(h_in, ffn_norm[l:l + 1], [dxn], dh_out, f"ffn{l}_norm_bwd")
        return (dh_in, dh_in_b), (d_wdn, d_wup_t, d_conv, d_conv_b, d_norm)

    (dh3, dh3b), ffn1_grads = ffn_bwd(dh4, dh4b, h3, ffn1_saved, 1, None)
    group, token = group_start("ffn1", [(6, ffn1_grads[1]), (8, ffn1_grads[0])])

    dattn = _matmul(dh3b, wo, mode="nt", out_dtype=F32, after=token, name="o_proj_dx", tm=2048)
    token = group_finish(group, dattn)
    dattn = dattn.reshape(B, S, D)
    d_wo = _matmul(attn, dh3b, mode="tn", out_dtype=F32, after=token, name="o_proj_dw", tm=1024, tn=1024, tk=2048)
    dq, dkv, dtab = _attn_bwd(q, kv, dattn, attn3, lse3, tabs, "attn_bwd")
    dtabs = dtab.transpose(1, 0, 2, 3, 4).reshape(3, N_HEADS, P * 2 * P)
    d_rel_bias = _bias_grad(dtabs, onehots, "rel_bias_grad").T
    dq2 = dq.reshape(T, D)
    dkv2 = dkv.reshape(2, T, D)
    d_wq = _matmul(xnb, dq2, mode="tn", out_dtype=F32, name="q_proj_dw", tm=1024, tn=1024, tk=2048)
    d_wkv_t = _matmul(dkv2, kvn, mode="tn", out_dtype=F32, a_parts=2, name="kv_proj_dw", tm=1024, tn=1024, tk=2048)
    group, token = group_start("attn", [(2, d_wkv_t), (3, d_wq), (4, d_wo)])
    dxnb = _matmul(dq2, wq, mode="nt", out_dtype=BF16, after=token, name="q_proj_dx")
    dkvn = _matmul(dkv2, wkv_t, mode="nn", out_dtype=BF16, a_parts=2, name="kv_proj_dx")
    dh2, dh2b, d_kvb_norm = _rmsnorm_bwd(h2, jnp.stack([kv_norm, b_norm[0]]), [dkvn, dxnb], dh3, "kv_b_norm_bwd")
    token = group_finish(group, dh2b)

    (dh1, dh1b), ffn0_grads = ffn_bwd(dh2, dh2b, h1, ffn0_saved, 0, token)
    group, token = group_start("ffn0", [(5, ffn0_grads[1]), (7, ffn0_grads[0])])

    dgated = _matmul(dh1b, wout, mode="nt", out_dtype=BF16, after=token, name="a_out_proj_dx", tm=2048)
    token = group_finish(group, dgated)
    dgated = dgated.reshape(B, S, D)
    d_wout = _matmul(gated, dh1b, mode="tn", out_dtype=F32, after=token, name="a_out_proj_dw",
                     tm=1024, tn=1024, tk=2048)
    dbch, d_a_conv = _shortconv_bwd(bch, dgated, a_conv_f, "a_gate_bwd")
    dbch = dbch.reshape(3, T, D)
    dxn0 = _matmul(dbch, win_t, mode="nn", out_dtype=BF16, a_parts=3, name="a_in_proj_dx", tm=2048)
    d_win_t = _matmul(dbch, xn0, mode="tn", out_dtype=F32, a_parts=3, name="a_in_proj_dw", tm=1024, tn=1024, tk=2048)
    grad_x, _, d_a_norm = _rmsnorm_bwd(x2, a_norm_f, [dxn0], dh1, "a_norm_bwd")

    small_full = [d_a_norm, d_a_conv, jnp.stack([ffn0_grads[2], ffn1_grads[2]]),
                  d_kvb_norm[0], d_kvb_norm[1], d_rel_bias, jnp.concatenate([ffn0_grads[4], ffn1_grads[4]]),
                  jnp.concatenate([ffn0_grads[3], ffn1_grads[3]]), d_final_norm, loss_part]
    small_full_shapes = [(1, D), (3, D), (2, 3, 2 * F), (D,), (1, D), rel_bias.shape, (2, D), (2, 2 * F), (D,), ()]
    token, gath = send_group("a", [(0, d_win_t), (1, d_wout)], small=[_pack_rows(small_full)])
    chip_sums, recv = {}, {}
    for tag, idxs, send, rcv, srcs, lnd in in_flight:
        srcs, lnd = _split_wait(_chip_copies, send, rcv, srcs, lnd, token, f"exchange_chip_wait_{tag}")
        for i, s_, l_ in zip(idxs, srcs, lnd):
            chip_sums[i], recv[i] = s_, l_
    small_sum = _sum_slots(gath[0], "small_grad_sum").reshape(-1)
    (g_a_norm_f, g_a_conv_f, g_ffn_conv_f, g_kv_norm, g_b_norm, g_rel_bias, g_ffn_norm, g_ffn_conv_b,
     g_final_norm, loss) = _unpack(small_sum, small_full_shapes)

    def my_cols(full, width):
        return lax.dynamic_slice_in_dim(full, me * width, width, axis=full.ndim - 1)

    g_a_norm = my_cols(g_a_norm_f, D // N_DEV)
    g_a_conv = my_cols(g_a_conv_f, D // N_DEV)[None]
    g_ffn_conv = my_cols(g_ffn_conv_f, 2 * F // N_DEV)

    big_w = [(a_w_in, m_a_w_in, v_a_w_in, True), (a_w_out, m_a_w_out, v_a_w_out, False),
             (w_kv, m_w_kv, v_w_kv, True), (b_w_q, m_b_w_q, v_b_w_q, False), (b_w_o, m_b_w_o, v_b_w_o, False),
             (ffn_w_up[0], m_ffn_w_up[0], v_ffn_w_up[0], True), (ffn_w_up[1], m_ffn_w_up[1], v_ffn_w_up[1], True),
             (ffn_w_down[0], m_ffn_w_down[0], v_ffn_w_down[0], False),
             (ffn_w_down[1], m_ffn_w_down[1], v_ffn_w_down[1], False)]
    big_out = []
    for i, (w, m, v, transposed) in enumerate(big_w):
        g = _chip_sum(chip_sums[i], recv[i], chip_idx, f"grad_sum_{i}")
        if transposed:
            g = g.T
        w2, m2, v2 = (t.reshape(g.shape) for t in (w, m, v))
        delta, nm, nv = _adamw(g, w2, m2, v2, f"adamw_{i}")
        big_out.append(tuple(t.reshape(w.shape) for t in (g, delta, nm, nv)))

    def pair(i, j):
        return tuple(jnp.stack([big_out[i][t], big_out[j][t]]) for t in range(4))

    small_w = [(a_norm, m_a_norm, v_a_norm, g_a_norm), (a_conv, m_a_conv, v_a_conv, g_a_conv),
               (ffn_conv, m_ffn_conv, v_ffn_conv, g_ffn_conv), (kv_norm, m_kv_norm, v_kv_norm, g_kv_norm),
               (b_norm, m_b_norm, v_b_norm, g_b_norm), (rel_bias, m_rel_bias, v_rel_bias, g_rel_bias),
               (ffn_norm, m_ffn_norm, v_ffn_norm, g_ffn_norm),
               (ffn_conv_b, m_ffn_conv_b, v_ffn_conv_b, g_ffn_conv_b),
               (final_norm, m_final_norm, v_final_norm, g_final_norm)]
    small_out = _adamw_small(small_w, "adamw_small")

    per_weight = [small_out[0], big_out[0], small_out[1], big_out[1], small_out[3], big_out[2], small_out[4],
                  big_out[3], big_out[4], small_out[5], small_out[6], pair(5, 6), small_out[2], small_out[7],
                  pair(7, 8), small_out[8]]
    outs = [loss, grad_x.reshape(B, S, D)]
    for t in range(4):
        outs.extend(pw[t] for pw in per_weight)
    return tuple(outs)
```

```python
import functools
import math

import jax
import jax.numpy as jnp
from jax import lax
from jax.experimental import pallas as pl
from jax.experimental.pallas import tpu as pltpu

F32 = jnp.float32
BF16 = jnp.bfloat16

N_DEV = 8
N_HEADS = 16
ATT_BLOCK = 128
ATT_UNROLL = 8
DILATED_BRANCHES = ((128, 1), (512, 4), (2048, 16))
REL_BUCKETS = 32
REL_MAX_DISTANCE = 2048
RMS_EPS = 1e-6
ADAM_LR = 0.001
ADAM_B1 = 0.9
ADAM_B2 = 0.999
ADAM_EPS = 1e-08
ADAM_WD = 0.01
ADAM_STEP = 10

LANE = 128
HALO = 16
NEG = -1e30
VMEM_CAP = 56 << 20


def _vmem(block_bytes):
    return int(min(VMEM_CAP, max(32 << 20, 3 * block_bytes + (8 << 20))))


def _nbytes(shape, dtype):
    return math.prod(shape) * jnp.dtype(dtype).itemsize


def _tile(dim, target):
    best = None
    t = LANE
    while t <= min(dim, target):
        if dim % t == 0:
            best = t
        t += LANE
    return best if best is not None else dim


def _matmul(a, b, *, mode, out_dtype, name, add=None, after=None, a_parts=1, tm=1024, tn=1024, tk=1024):
    P = a_parts
    if mode == "nn":
        M, K = (a.shape[0], a.shape[1]) if P == 1 else (a.shape[1], a.shape[2] * P)
        N = b.shape[1]
    elif mode == "nt":
        assert P == 1
        M, K = a.shape
        N = b.shape[0]
    else:
        K = a.shape[0] if P == 1 else a.shape[1]
        M = a.shape[1] if P == 1 else a.shape[2] * P
        N = b.shape[1]
    tm = _tile(M // P if mode == "tn" else M, tm)
    tn = _tile(N, tn)
    tk = _tile(K // P if mode == "nn" else K, tk)
    nm, nn_, nk = M // tm, N // tn, K // tk

    if mode == "nn":
        if P == 1:
            a_spec = pl.BlockSpec((tm, tk), lambda i, j, k: (i, k))
        else:
            nkp = nk // P
            a_spec = pl.BlockSpec((None, tm, tk), lambda i, j, k: (k // nkp, i, k % nkp))
        b_spec = pl.BlockSpec((tk, tn), lambda i, j, k: (k, j))
        dims = (((1,), (0,)), ((), ()))
    elif mode == "nt":
        a_spec = pl.BlockSpec((tm, tk), lambda i, j, k: (i, k))
        b_spec = pl.BlockSpec((tn, tk), lambda i, j, k: (j, k))
        dims = (((1,), (1,)), ((), ()))
    else:
        if P == 1:
            a_spec = pl.BlockSpec((tk, tm), lambda i, j, k: (k, i))
        else:
            nmp = nm // P
            a_spec = pl.BlockSpec((None, tk, tm), lambda i, j, k: (i // nmp, k, i % nmp))
        b_spec = pl.BlockSpec((tk, tn), lambda i, j, k: (k, j))
        dims = (((0,), (0,)), ((), ()))
    o_spec = pl.BlockSpec((tm, tn), lambda i, j, k: (i, j))
    in_specs = [a_spec, b_spec]
    args = [a, b]
    if add is not None:
        in_specs.append(o_spec)
        args.append(add)
    has_add = add is not None
    if after is not None:
        in_specs.append(pl.BlockSpec(memory_space=pl.ANY))
        args.append(after)
    n_extra = len(args) - 2

    direct = nk > 1 and out_dtype == F32 and not has_add

    def body(a_ref, b_ref, *rest):
        add_ref = rest[0] if has_add else None
        o_ref = rest[n_extra]
        acc_ref = o_ref if direct or nk == 1 else rest[-1]
        k = pl.program_id(2)

        def part():
            return lax.dot_general(a_ref[...].astype(BF16), b_ref[...].astype(BF16), dims,
                                   preferred_element_type=F32)

        def finish(r):
            if has_add:
                r = r + add_ref[...].astype(F32)
            o_ref[...] = r.astype(out_dtype)

        if nk == 1:
            finish(part())
        else:
            @pl.when(k == 0)
            def _():
                acc_ref[...] = part()

            @pl.when(jnp.logical_and(k > 0, jnp.logical_or(k < nk - 1, direct)))
            def _():
                acc_ref[...] += part()

            if not direct:
                @pl.when(k == nk - 1)
                def _():
                    finish(acc_ref[...] + part())

    blk = (_nbytes((tm, tk), a.dtype) + _nbytes((tk, tn), b.dtype) + _nbytes((tm, tn), out_dtype)
           + (_nbytes((tm, tn), add.dtype) if has_add else 0)) * 2 + 3 * _nbytes((tm, tn), F32)
    return pl.pallas_call(
        body, name=name, grid=(nm, nn_, nk),
        in_specs=in_specs, out_specs=o_spec,
        out_shape=jax.ShapeDtypeStruct((M, N), out_dtype),
        scratch_shapes=[] if direct or nk == 1 else [pltpu.VMEM((tm, tn), F32)],
        compiler_params=pltpu.CompilerParams(
            dimension_semantics=("parallel", "parallel", "arbitrary"), vmem_limit_bytes=_vmem(blk)),
    )(*args)


def _rmsnorm_fwd(x, gains, name, tm=1024):
    T, D = x.shape
    n = gains.shape[0]
    tm = _tile(T, tm)

    def body(x_ref, g_ref, *o_refs):
        xv = x_ref[...]
        xhat = xv * lax.rsqrt(jnp.mean(xv * xv, axis=-1, keepdims=True) + RMS_EPS)
        for i in range(n):
            o_refs[i][...] = (xhat * g_ref[i:i + 1, :]).astype(BF16)

    row = pl.BlockSpec((tm, D), lambda i: (i, 0))
    outs = pl.pallas_call(
        body, name=name, grid=(T // tm,),
        in_specs=[row, pl.BlockSpec((n, D), lambda i: (0, 0))],
        out_specs=[row] * n,
        out_shape=[jax.ShapeDtypeStruct((T, D), BF16)] * n,
        compiler_params=pltpu.CompilerParams(
            dimension_semantics=("parallel",), vmem_limit_bytes=_vmem(4 * _nbytes((tm, D), F32))),
    )(x, gains)
    return tuple(outs)


def _rmsnorm_bwd(x, gains, dxns, dres, name, tm=1024):
    T, D = x.shape
    n = gains.shape[0]
    tm = _tile(T, tm)

    def body(x_ref, g_ref, *rest):
        dxn_refs = rest[:n]
        dres_ref, dx_ref, dxb_ref, dg_ref = rest[n:]
        xv = x_ref[...]
        rstd = lax.rsqrt(jnp.mean(xv * xv, axis=-1, keepdims=True) + RMS_EPS)
        xhat = xv * rstd
        dx = dres_ref[...]

        @pl.when(pl.program_id(0) == 0)
        def _():
            dg_ref[...] = jnp.zeros_like(dg_ref)

        for i in range(n):
            dy = dxn_refs[i][...].astype(F32)
            dg_ref[i:i + 1, :] += jnp.sum(dy * xhat, axis=0, keepdims=True)
            dxh = dy * g_ref[i:i + 1, :]
            dx = dx + rstd * (dxh - xhat * jnp.mean(dxh * xhat, axis=-1, keepdims=True))
        dx_ref[...] = dx
        dxb_ref[...] = dx.astype(BF16)

    row = pl.BlockSpec((tm, D), lambda i: (i, 0))
    par = pl.BlockSpec((n, D), lambda i: (0, 0))
    return pl.pallas_call(
        body, name=name, grid=(T // tm,),
        in_specs=[row, par] + [row] * n + [row],
        out_specs=[row, row, par],
        out_shape=[jax.ShapeDtypeStruct((T, D), F32), jax.ShapeDtypeStruct((T, D), BF16),
                   jax.ShapeDtypeStruct((n, D), F32)],
        compiler_params=pltpu.CompilerParams(
            dimension_semantics=("arbitrary",), vmem_limit_bytes=_vmem((4 + n) * _nbytes((tm, D), F32))),
    )(x, gains, *dxns, dres)


def _final_loss_bwd(h, gain, target, name, tm=1024):
    T, D = h.shape
    tm = _tile(T, tm)

    def body(h_ref, g_ref, t_ref, dh_ref, dhb_ref, dg_ref, sq_ref):
        xv = h_ref[...]
        rstd = lax.rsqrt(jnp.mean(xv * xv, axis=-1, keepdims=True) + RMS_EPS)
        xhat = xv * rstd
        err = xhat * g_ref[...] - t_ref[...]

        @pl.when(pl.program_id(0) == 0)
        def _():
            dg_ref[...] = jnp.zeros_like(dg_ref)
            sq_ref[...] = jnp.zeros_like(sq_ref)

        sq_ref[...] += jnp.sum(err * err, axis=0, keepdims=True)
        dy = err * (1.0 / D)
        dg_ref[...] += jnp.sum(dy * xhat, axis=0, keepdims=True)
        dxh = dy * g_ref[...]
        dh = rstd * (dxh - xhat * jnp.mean(dxh * xhat, axis=-1, keepdims=True))
        dh_ref[...] = dh
        dhb_ref[...] = dh.astype(BF16)

    row = pl.BlockSpec((tm, D), lambda i: (i, 0))
    par = pl.BlockSpec((1, D), lambda i: (0, 0))
    return pl.pallas_call(
        body, name=name, grid=(T // tm,),
        in_specs=[row, par, row], out_specs=[row, row, par, par],
        out_shape=[jax.ShapeDtypeStruct((T, D), F32), jax.ShapeDtypeStruct((T, D), BF16),
                   jax.ShapeDtypeStruct((1, D), F32), jax.ShapeDtypeStruct((1, D), F32)],
        compiler_params=pltpu.CompilerParams(
            dimension_semantics=("arbitrary",), vmem_limit_bytes=_vmem(5 * _nbytes((tm, D), F32))),
    )(h, gain, target)


def _halo_specs(S, ts, tc, col_off, prev, nxt):
    r = ts // HALO
    last = S // HALO - 1
    specs = []
    if prev:
        specs.append(pl.BlockSpec((None, HALO, tc), lambda j, b, s: (b, jnp.maximum(s * r - 1, 0), col_off + j)))
    specs.append(pl.BlockSpec((None, ts, tc), lambda j, b, s: (b, s, col_off + j)))
    if nxt:
        specs.append(pl.BlockSpec((None, HALO, tc), lambda j, b, s: (b, jnp.minimum((s + 1) * r, last), col_off + j)))
    return specs


def _ext(prev_ref, main_ref, next_ref, first, last):
    main = main_ref[...].astype(F32)
    zeros = jnp.zeros((HALO, main.shape[1]), F32)
    top = zeros if prev_ref is None else jnp.where(first, 0.0, prev_ref[...].astype(F32))
    bot = zeros if next_ref is None else jnp.where(last, 0.0, next_ref[...].astype(F32))
    return jnp.concatenate([top, main, bot], axis=0)


def _shift(xe, k):
    return pltpu.roll(xe, k % xe.shape[0], axis=0)


def _cconv(xe, w):
    return w[0:1, :] * _shift(xe, 2) + w[1:2, :] * _shift(xe, 1) + w[2:3, :] * xe


def _main(xe, ts):
    return xe[HALO:HALO + ts, :]


def _cconv_t(de, w):
    return w[2:3, :] * de + w[1:2, :] * _shift(de, -1) + w[0:1, :] * _shift(de, -2)


def _conv_wgrad(dw_ref, de, xe, ts):
    d = _main(de, ts)
    dw_ref[0:1, :] += jnp.sum(d * _main(_shift(xe, 2), ts), axis=0, keepdims=True)
    dw_ref[1:2, :] += jnp.sum(d * _main(_shift(xe, 1), ts), axis=0, keepdims=True)
    dw_ref[2:3, :] += jnp.sum(d * _main(xe, ts), axis=0, keepdims=True)


def _gate_tiles(S, C, rows=1024):
    return _tile(S, rows), _tile(C, 512)


def _shortconv_fwd(bch, conv_w, name):
    B, S, D3 = bch.shape
    D = D3 // 3
    ts, tc = _gate_tiles(S, D, rows=2048)
    nj, ns = D // tc, S // ts

    def body(b_ref, cp_ref, c_ref, hp_ref, h_ref, w_ref, o_ref):
        first = pl.program_id(2) == 0
        ce = _ext(cp_ref, c_ref, None, first, False)
        he = _ext(hp_ref, h_ref, None, first, False)
        cv = _main(_cconv(ce * he, w_ref[...]), ts)
        o_ref[...] = (b_ref[...].astype(F32) * cv).astype(BF16)

    in_specs = (_halo_specs(S, ts, tc, 0, False, False) + _halo_specs(S, ts, tc, nj, True, False)
                + _halo_specs(S, ts, tc, 2 * nj, True, False) + [pl.BlockSpec((3, tc), lambda j, b, s: (0, j))])
    return pl.pallas_call(
        body, name=name, grid=(nj, B, ns), in_specs=in_specs,
        out_specs=pl.BlockSpec((None, ts, tc), lambda j, b, s: (b, s, j)),
        out_shape=jax.ShapeDtypeStruct((B, S, D), BF16),
        compiler_params=pltpu.CompilerParams(
            dimension_semantics=("parallel", "parallel", "parallel"),
            vmem_limit_bytes=_vmem(12 * _nbytes((ts + 2 * HALO, tc), F32))),
    )(bch, bch, bch, bch, bch, conv_w)


def _shortconv_bwd(bch, dg, conv_w, name):
    B, S, D3 = bch.shape
    D = D3 // 3
    ts, tc = _gate_tiles(S, D)
    nj, ns = D // tc, S // ts

    def body(b_ref, bn_ref, cp_ref, c_ref, cn_ref, hp_ref, h_ref, hn_ref, d_ref, dn_ref, w_ref, o_ref, dw_ref):
        s = pl.program_id(2)
        first, last = s == 0, s == ns - 1

        @pl.when(jnp.logical_and(pl.program_id(1) == 0, s == 0))
        def _():
            dw_ref[...] = jnp.zeros_like(dw_ref)

        w = w_ref[...]
        be = _ext(None, b_ref, bn_ref, first, last)
        ce = _ext(cp_ref, c_ref, cn_ref, first, last)
        he = _ext(hp_ref, h_ref, hn_ref, first, last)
        de = _ext(None, d_ref, dn_ref, first, last)
        ch = ce * he
        dcv = de * be
        dch = _main(_cconv_t(dcv, w), ts)
        o_ref[0] = (_main(de, ts) * _main(_cconv(ch, w), ts)).astype(BF16)
        o_ref[1] = (dch * _main(he, ts)).astype(BF16)
        o_ref[2] = (dch * _main(ce, ts)).astype(BF16)
        _conv_wgrad(dw_ref, dcv, ch, ts)

    in_specs = (_halo_specs(S, ts, tc, 0, False, True) + _halo_specs(S, ts, tc, nj, True, True)
                + _halo_specs(S, ts, tc, 2 * nj, True, True) + _halo_specs(S, ts, tc, 0, False, True)
                + [pl.BlockSpec((3, tc), lambda j, b, s: (0, j))])
    return pl.pallas_call(
        body, name=name, grid=(nj, B, ns), in_specs=in_specs,
        out_specs=[pl.BlockSpec((3, None, ts, tc), lambda j, b, s: (0, b, s, j)),
                   pl.BlockSpec((3, tc), lambda j, b, s: (0, j))],
        out_shape=[jax.ShapeDtypeStruct((3, B, S, D), BF16), jax.ShapeDtypeStruct((3, D), F32)],
        compiler_params=pltpu.CompilerParams(
            dimension_semantics=("parallel", "arbitrary", "arbitrary"),
            vmem_limit_bytes=_vmem(24 * _nbytes((ts + 2 * HALO, tc), F32))),
    )(bch, bch, bch, bch, bch, bch, bch, bch, dg, dg, conv_w)


def _sigmoid(x):
    return 1.0 / (1.0 + jnp.exp(-x))


def _ffn_gate_fwd(u0, conv_w, conv_b, name):
    B, S, F2 = u0.shape
    F = F2 // 2
    ts, tc = _gate_tiles(S, F, rows=2048)
    nj, ns = F // tc, S // ts

    def body(gp_ref, g_ref, up_ref, u_ref, wg_ref, wu_ref, bg_ref, bu_ref, o_ref):
        first = pl.program_id(2) == 0
        ug = _main(_cconv(_ext(gp_ref, g_ref, None, first, False), wg_ref[...]), ts) + bg_ref[...]
        uu = _main(_cconv(_ext(up_ref, u_ref, None, first, False), wu_ref[...]), ts) + bu_ref[...]
        o_ref[...] = (ug * _sigmoid(ug) * uu).astype(BF16)

    in_specs = (_halo_specs(S, ts, tc, 0, True, False) + _halo_specs(S, ts, tc, nj, True, False)
                + [pl.BlockSpec((3, tc), lambda j, b, s: (0, j)), pl.BlockSpec((3, tc), lambda j, b, s: (0, nj + j)),
                   pl.BlockSpec((1, tc), lambda j, b, s: (0, j)), pl.BlockSpec((1, tc), lambda j, b, s: (0, nj + j))])
    return pl.pallas_call(
        body, name=name, grid=(nj, B, ns), in_specs=in_specs,
        out_specs=pl.BlockSpec((None, ts, tc), lambda j, b, s: (b, s, j)),
        out_shape=jax.ShapeDtypeStruct((B, S, F), BF16),
        compiler_params=pltpu.CompilerParams(
            dimension_semantics=("parallel", "parallel", "parallel"),
            vmem_limit_bytes=_vmem(12 * _nbytes((ts + 2 * HALO, tc), F32))),
    )(u0, u0, u0, u0, conv_w, conv_w, conv_b, conv_b)


def _ffn_gate_bwd(u0, dact, conv_w, conv_b, name):
    B, S, F2 = u0.shape
    F = F2 // 2
    ts, tc = _gate_tiles(S, F)
    nj, ns = F // tc, S // ts

    def body(gp_ref, g_ref, gn_ref, up_ref, u_ref, un_ref, d_ref, dn_ref, wg_ref, wu_ref, bg_ref, bu_ref,
             o_ref, dwg_ref, dwu_ref, dbg_ref, dbu_ref):
        s = pl.program_id(2)
        first, last = s == 0, s == ns - 1

        @pl.when(jnp.logical_and(pl.program_id(1) == 0, s == 0))
        def _():
            for r in (dwg_ref, dwu_ref, dbg_ref, dbu_ref):
                r[...] = jnp.zeros_like(r)

        wg, wu = wg_ref[...], wu_ref[...]
        ge = _ext(gp_ref, g_ref, gn_ref, first, last)
        ue = _ext(up_ref, u_ref, un_ref, first, last)
        de = _ext(None, d_ref, dn_ref, first, last)
        ug = _cconv(ge, wg) + bg_ref[...]
        uu = _cconv(ue, wu) + bu_ref[...]
        sig = _sigmoid(ug)
        dug = de * uu * (sig * (1.0 + ug * (1.0 - sig)))
        duu = de * (ug * sig)
        o_ref[0] = _main(_cconv_t(dug, wg), ts).astype(BF16)
        o_ref[1] = _main(_cconv_t(duu, wu), ts).astype(BF16)
        _conv_wgrad(dwg_ref, dug, ge, ts)
        _conv_wgrad(dwu_ref, duu, ue, ts)
        dbg_ref[...] += jnp.sum(_main(dug, ts), axis=0, keepdims=True)
        dbu_ref[...] += jnp.sum(_main(duu, ts), axis=0, keepdims=True)

    w3 = lambda off: pl.BlockSpec((3, tc), lambda j, b, s: (0, off + j))
    w1 = lambda off: pl.BlockSpec((1, tc), lambda j, b, s: (0, off + j))
    in_specs = (_halo_specs(S, ts, tc, 0, True, True) + _halo_specs(S, ts, tc, nj, True, True)
                + _halo_specs(S, ts, tc, 0, False, True) + [w3(0), w3(nj), w1(0), w1(nj)])
    outs = pl.pallas_call(
        body, name=name, grid=(nj, B, ns), in_specs=in_specs,
        out_specs=[pl.BlockSpec((2, None, ts, tc), lambda j, b, s: (0, b, s, j)), w3(0), w3(0), w1(0), w1(0)],
        out_shape=[jax.ShapeDtypeStruct((2, B, S, F), BF16), jax.ShapeDtypeStruct((3, F), F32),
                   jax.ShapeDtypeStruct((3, F), F32), jax.ShapeDtypeStruct((1, F), F32),
                   jax.ShapeDtypeStruct((1, F), F32)],
        compiler_params=pltpu.CompilerParams(
            dimension_semantics=("parallel", "arbitrary", "arbitrary"),
            vmem_limit_bytes=_vmem(30 * _nbytes((ts + 2 * HALO, tc), F32))),
    )(u0, u0, u0, u0, u0, u0, dact, dact, conv_w, conv_w, conv_b, conv_b)
    du0, dwg, dwu, dbg, dbu = outs
    return du0, jnp.concatenate([dwg, dwu], axis=1), jnp.concatenate([dbg, dbu], axis=1)


def _t5_bucket(dist):
    max_exact = REL_BUCKETS // 2
    n = jnp.maximum(dist, 0)
    nf = jnp.maximum(n, max_exact).astype(F32)
    large = max_exact + (jnp.log(nf / max_exact) / math.log(REL_MAX_DISTANCE / max_exact)
                         * (REL_BUCKETS - max_exact)).astype(jnp.int32)
    large = jnp.minimum(large, REL_BUCKETS - 1)
    return jnp.where(n < max_exact, n, large)


def _band_tables(window, dilation):
    P = ATT_BLOCK
    qi = jnp.arange(P, dtype=jnp.int32)[:, None]
    kc = jnp.arange(2 * P, dtype=jnp.int32)[None, :]
    delta = qi + P - kc
    band = (delta >= 0) & (delta <= window // dilation)
    bucket = _t5_bucket(delta * dilation).reshape(-1)
    onehot = (bucket[None, :] == jnp.arange(REL_BUCKETS, dtype=jnp.int32)[:, None]).astype(F32)
    return onehot, band


def _bias_lookup(rel_bias_t, onehots, name):
    nb, _, Q = onehots.shape
    H = rel_bias_t.shape[0]

    def body(r_ref, oh_ref, o_ref):
        o_ref[...] = lax.dot_general(r_ref[...], oh_ref[...], (((1,), (0,)), ((), ())),
                                     precision=lax.Precision.HIGHEST, preferred_element_type=F32)

    return pl.pallas_call(
        body, name=name, grid=(nb,),
        in_specs=[pl.BlockSpec((H, REL_BUCKETS), lambda i: (0, 0)),
                  pl.BlockSpec((None, REL_BUCKETS, Q), lambda i: (i, 0, 0))],
        out_specs=pl.BlockSpec((None, H, Q), lambda i: (i, 0, 0)),
        out_shape=jax.ShapeDtypeStruct((nb, H, Q), F32),
        compiler_params=pltpu.CompilerParams(dimension_semantics=("parallel",),
                                             vmem_limit_bytes=_vmem(_nbytes((REL_BUCKETS + H, Q), F32))),
    )(rel_bias_t, onehots)


def _bias_grad(dtabs, onehots, name):
    nb, H, Q = dtabs.shape

    def body(d_ref, oh_ref, o_ref):
        @pl.when(pl.program_id(0) == 0)
        def _():
            o_ref[...] = jnp.zeros_like(o_ref)

        o_ref[...] += lax.dot_general(d_ref[...], oh_ref[...], (((1,), (1,)), ((), ())),
                                      precision=lax.Precision.HIGHEST, preferred_element_type=F32)

    return pl.pallas_call(
        body, name=name, grid=(nb,),
        in_specs=[pl.BlockSpec((None, H, Q), lambda i: (i, 0, 0)),
                  pl.BlockSpec((None, REL_BUCKETS, Q), lambda i: (i, 0, 0))],
        out_specs=pl.BlockSpec((H, REL_BUCKETS), lambda i: (0, 0)),
        out_shape=jax.ShapeDtypeStruct((H, REL_BUCKETS), F32),
        compiler_params=pltpu.CompilerParams(dimension_semantics=("arbitrary",),
                                             vmem_limit_bytes=_vmem(_nbytes((REL_BUCKETS + H, Q), F32))),
    )(dtabs, onehots)


def _lane_masks():
    lane = lax.broadcasted_iota(jnp.int32, (1, LANE), 1)
    lo = lane < LANE // 2
    return lo, jnp.logical_not(lo)


def _dot_nt(a, b):
    return lax.dot_general(a, b, (((1,), (1,)), ((), ())), preferred_element_type=F32)


def _dot_nn(a, b):
    return lax.dot_general(a, b, (((1,), (0,)), ((), ())), preferred_element_type=F32)


def _dot_tn(a, b):
    return lax.dot_general(a, b, (((0,), (0,)), ((), ())), preferred_element_type=F32)


def _block_rows(n, dilation, S):
    P = ATT_BLOCK
    nb = S // (dilation * P)
    r, i = n // nb, n % nb
    cur = pl.ds(i * (P * dilation) + r, P, stride=dilation)
    prv = pl.ds(jnp.maximum(i - 1, 0) * (P * dilation) + r, P, stride=dilation)
    return cur, prv, jnp.minimum(i, 1)


def _attn_fwd(q, kv, tabs, name):
    B, S, D = q.shape
    P, H = ATT_BLOCK, N_HEADS
    scale = (D // H) ** -0.5
    half = LANE // 2
    nsteps = S // P // ATT_UNROLL

    def body(q_ref, k_ref, v_ref, tab_ref, o_ref, lse_ref, acc_ref, m_ref, l_ref):
        lo, hi = _lane_masks()
        for bi, (_, d) in enumerate(DILATED_BRANCHES):
            def block(n, carry, bi=bi, d=d):
                rows = [_block_rows(n + j * nsteps, d, S) for j in range(ATT_UNROLL)]
                loaded = []
                for cur, prv, variant in rows:
                    qb = q_ref[cur, :].astype(BF16)
                    kw = jnp.concatenate([k_ref[prv, :], k_ref[cur, :]], axis=0).astype(BF16)
                    vw = jnp.concatenate([v_ref[prv, :], v_ref[cur, :]], axis=0).astype(BF16)
                    loaded.append((qb, kw, vw))
                results = []
                for (cur, prv, variant), (qb, kw, vw) in zip(rows, loaded):
                    m_pair = l_pair = a_pair = None
                    for e, msk in enumerate((lo, hi)):
                        s = _dot_nt(jnp.where(msk, qb, 0), kw) * scale + tab_ref[bi, variant, e]
                        m_e = jnp.max(s, axis=-1, keepdims=True)
                        pe = jnp.exp(s - m_e)
                        l_e = jnp.sum(pe, axis=-1, keepdims=True)
                        a_e = _dot_nn(pe.astype(BF16), jnp.where(msk, vw, 0))
                        m_pair = m_e if e == 0 else jnp.where(lo, m_pair, m_e)
                        l_pair = l_e if e == 0 else jnp.where(lo, l_pair, l_e)
                        a_pair = a_e if e == 0 else a_pair + a_e
                    results.append((m_pair, l_pair, a_pair))
                for (cur, prv, variant), (m_pair, l_pair, a_pair) in zip(rows, results):
                    m_ref[bi, cur, :] = m_pair
                    l_ref[bi, cur, :] = l_pair
                    acc_ref[bi, cur, :] = a_pair
                return carry

            lax.fori_loop(0, nsteps, block, 0)

        nbr = len(DILATED_BRANCHES)
        chunk = 512

        def merge(t, carry):
            rs = pl.ds(pl.multiple_of(t * chunk, chunk), chunk)
            ms = [m_ref[b, rs, :] for b in range(nbr)]
            m = functools.reduce(jnp.maximum, ms)
            ws = [jnp.exp(mb - m) for mb in ms]
            l = functools.reduce(jnp.add, [w * l_ref[b, rs, :] for b, w in enumerate(ws)])
            acc = functools.reduce(jnp.add, [w * acc_ref[b, rs, :] for b, w in enumerate(ws)])
            o_ref[rs, :] = (acc / l).astype(BF16)
            lse_ref[rs, :] = m + jnp.log(l)
            return carry

        lax.fori_loop(0, S // chunk, merge, 0)

    nl = D // LANE
    col = lambda off: pl.BlockSpec((None, S, LANE), lambda b, p: (b, 0, off + p))
    tab_spec = pl.BlockSpec((3, 2, 2, P, 2 * P), lambda b, p: (0, 0, p, 0, 0))
    return pl.pallas_call(
        body, name=name, grid=(B, H // 2), in_specs=[col(0), col(0), col(nl), tab_spec],
        out_specs=[col(0), col(0)],
        out_shape=[jax.ShapeDtypeStruct((B, S, D), BF16), jax.ShapeDtypeStruct((B, S, D), F32)],
        scratch_shapes=[pltpu.VMEM((len(DILATED_BRANCHES), S, LANE), F32)] * 3,
        compiler_params=pltpu.CompilerParams(
            dimension_semantics=("parallel", "parallel"), vmem_limit_bytes=VMEM_CAP),
    )(q, kv, kv, tabs)


def _attn_bwd(q, kv, do, o, lse, tabs, name):
    B, S, D = q.shape
    P, H = ATT_BLOCK, N_HEADS
    scale = (D // H) ** -0.5
    half = LANE // 2
    nsteps = S // P // ATT_UNROLL

    def body(q_ref, k_ref, v_ref, do_ref, o_ref, lse_ref, tab_ref, dq_ref, dkv_ref, dtab_ref, delta_ref):
        lo, hi = _lane_masks()

        @pl.when(pl.program_id(1) == 0)
        def _():
            dtab_ref[...] = jnp.zeros_like(dtab_ref)

        dkv_ref[...] = jnp.zeros_like(dkv_ref)
        prod = do_ref[...] * o_ref[...].astype(F32)
        delta_ref[...] = jnp.where(lo, jnp.sum(jnp.where(lo, prod, 0.0), axis=-1, keepdims=True),
                                   jnp.sum(jnp.where(hi, prod, 0.0), axis=-1, keepdims=True))

        for bi, (_, d) in enumerate(DILATED_BRANCHES):
            def block(n, carry, bi=bi, d=d):
                rows = [_block_rows(n + j * nsteps, d, S) for j in range(ATT_UNROLL)]
                loaded = []
                for cur, prv, variant in rows:
                    qb = q_ref[cur, :].astype(BF16)
                    kw = jnp.concatenate([k_ref[prv, :], k_ref[cur, :]], axis=0).astype(BF16)
                    vw = jnp.concatenate([v_ref[prv, :], v_ref[cur, :]], axis=0).astype(BF16)
                    dob = do_ref[cur, :].astype(BF16)
                    dq_old = dq_ref[cur, :] if bi > 0 else None
                    loaded.append((qb, kw, vw, dob, lse_ref[cur, :], delta_ref[cur, :], dq_old))
                results = []
                for j, ((cur, prv, variant), (qb, kw, vw, dob, lse_b, dl_b, dq_old)) in enumerate(zip(rows, loaded)):
                    dq_pair = dk_pair = dv_pair = None
                    for e, msk in enumerate((lo, hi)):
                        qm = jnp.where(msk, qb, 0)
                        dom = jnp.where(msk, dob, 0)
                        s = _dot_nt(qm, kw) * scale + tab_ref[bi, variant, e]
                        pe = jnp.exp(s - lse_b[:, e * half:e * half + 1])
                        ds = pe * (_dot_nt(dom, vw) - dl_b[:, e * half:e * half + 1])
                        dtab_ref[bi, e] += ds
                        dsb = ds.astype(BF16)
                        dq_e = _dot_nn(dsb, jnp.where(msk, kw, 0))
                        dk_e = _dot_tn(dsb, qm)
                        dv_e = _dot_tn(pe.astype(BF16), dom)
                        dq_pair = dq_e if e == 0 else dq_pair + dq_e
                        dk_pair = dk_e if e == 0 else dk_pair + dk_e
                        dv_pair = dv_e if e == 0 else dv_pair + dv_e
                    dq_pair = dq_pair * scale
                    if bi > 0:
                        dq_pair = dq_pair + dq_old
                    results.append((dq_pair, dk_pair * scale, dv_pair))
                for (cur, prv, variant), (dq_pair, dk_pair, dv_pair) in zip(rows, results):
                    dq_ref[cur, :] = dq_pair
                    dkv_ref[0, cur, :] += dk_pair[P:, :]
                    dkv_ref[1, cur, :] += dv_pair[P:, :]
                    dkv_ref[0, prv, :] += dk_pair[:P, :]
                    dkv_ref[1, prv, :] += dv_pair[:P, :]
                return carry

            lax.fori_loop(0, nsteps, block, 0)

    nl = D // LANE
    col = lambda off: pl.BlockSpec((None, S, LANE), lambda p, b: (b, 0, off + p))
    tab_spec = pl.BlockSpec((3, 2, 2, P, 2 * P), lambda p, b: (0, 0, p, 0, 0))
    return pl.pallas_call(
        body, name=name, grid=(H // 2, B),
        in_specs=[col(0), col(0), col(nl), col(0), col(0), col(0), tab_spec],
        out_specs=[col(0), pl.BlockSpec((2, None, S, LANE), lambda p, b: (0, b, 0, p)),
                   pl.BlockSpec((None, 3, 2, P, 2 * P), lambda p, b: (p, 0, 0, 0, 0))],
        out_shape=[jax.ShapeDtypeStruct((B, S, D), F32), jax.ShapeDtypeStruct((2, B, S, D), F32),
                   jax.ShapeDtypeStruct((H // 2, 3, 2, P, 2 * P), F32)],
        scratch_shapes=[pltpu.VMEM((S, LANE), F32)],
        compiler_params=pltpu.CompilerParams(
            dimension_semantics=("parallel", "arbitrary"),
            vmem_limit_bytes=VMEM_CAP),
    )(q, kv, kv, do, o, lse, tabs)


def _mesh_place():
    x, y, c = lax.axis_index("x"), lax.axis_index("y"), lax.axis_index("c")
    return x, y, c, (x, y, 1 - c), [(1 - x, y), (x, 1 - y), (1 - x, 1 - y)]


def _rdma(src, dst, send_sems, recv_sems, idx, to):
    return pltpu.make_async_remote_copy(
        src_ref=src, dst_ref=dst, send_sem=send_sems.at[idx], recv_sem=recv_sems.at[idx],
        device_id=to, device_id_type=pl.DeviceIdType.MESH)


def _comm_call(body, name, args, out_shape, n_sems, n_local):
    any_spec = pl.BlockSpec(memory_space=pl.ANY)
    return pl.pallas_call(
        body, name=name, in_specs=[any_spec] * len(args), out_specs=[any_spec] * len(out_shape),
        out_shape=out_shape,
        scratch_shapes=[pltpu.SemaphoreType.DMA(n_sems), pltpu.SemaphoreType.DMA(n_sems),
                        pltpu.SemaphoreType.DMA((n_local,))],
        compiler_params=pltpu.CompilerParams(has_side_effects=True),
    )(*args)


def _all_gather(arrays, name):
    n = len(arrays)

    def body(*refs):
        ins, outs = refs[:n], refs[n:2 * n]
        send_sems, recv_sems, loc_sems = refs[2 * n:]
        x, y, c, sib, chips = _mesh_place()
        me = (x, y, c)

        def rows(a, dev):
            return outs[a].at[4 * dev[0] + 2 * dev[1] + dev[2]]

        def copy(a, k, block, to, src=None):
            return _rdma(rows(a, block) if src is None else src, rows(a, block), send_sems, recv_sems, (a, k), to)

        local, first, passed = [], [], []
        for a in range(n):
            cp = pltpu.make_async_copy(ins[a], rows(a, me), loc_sems.at[a])
            cp.start()
            local.append(cp)
            first.append(copy(a, 0, me, sib, src=ins[a]))
            first += [copy(a, 1 + j, me, (*chip, c), src=ins[a]) for j, chip in enumerate(chips)]
        for cp in first:
            cp.start()
        for j, chip in enumerate(chips):
            for a in range(n):
                copy(a, 1 + j, (*chip, c), me).wait_recv()
                cp = copy(a, 4 + j, (*chip, c), sib)
                cp.start()
                passed.append(cp)
        for a in range(n):
            copy(a, 0, sib, me).wait_recv()
            for j, chip in enumerate(chips):
                copy(a, 4 + j, (*chip, 1 - c), me).wait_recv()
        for cp in first + passed:
            cp.wait_send()
        for cp in local:
            cp.wait()

    out_shape = [jax.ShapeDtypeStruct((N_DEV,) + g.shape, g.dtype) for g in arrays]
    return list(_comm_call(body, name, arrays, out_shape, (n, N_DEV - 1), n))


def _core_pair_exchange(halves, gathered, name):
    nh, ng = len(halves), len(gathered)

    def body(*refs):
        h_in, g_in = refs[:nh], refs[nh:nh + ng]
        outs = refs[nh + ng:nh + ng + nh + ng]
        theirs, g_out = outs[:nh], outs[nh:]
        send_sems, recv_sems, loc_sems = refs[nh + ng + nh + ng:]
        x, y, c, sib, chips = _mesh_place()
        me = 4 * x + 2 * y + c
        peers = [sib] + [(*chip, pc) for chip in chips for pc in (c, 1 - c)]
        nchip = N_DEV // 2
        gbase = nh * nchip
        local, sends = [], []
        for a in range(nh):
            for ch in range(nchip):
                cp = _rdma(h_in[a].at[ch, 1 - c], theirs[a].at[ch], send_sems, recv_sems, a * nchip + ch, sib)
                cp.start()
                sends.append(cp)
        for g in range(ng):
            cp = pltpu.make_async_copy(g_in[g], g_out[g].at[me], loc_sems.at[g])
            cp.start()
            local.append(cp)
            for k, peer in enumerate(peers):
                cp = _rdma(g_in[g], g_out[g].at[me], send_sems, recv_sems, gbase + g * (N_DEV - 1) + k, peer)
                cp.start()
                sends.append(cp)
        for cp in sends:
            cp.wait_send()
        for a in range(nh):
            for ch in range(nchip):
                _rdma(h_in[a].at[ch, 1 - c], theirs[a].at[ch], send_sems, recv_sems, a * nchip + ch, sib).wait_recv()
        for g in range(ng):
            for k, peer in enumerate(peers):
                pid = 4 * peer[0] + 2 * peer[1] + peer[2]
                _rdma(g_in[g], g_out[g].at[pid], send_sems, recv_sems, gbase + g * (N_DEV - 1) + k, peer).wait_recv()
        for cp in local:
            cp.wait()

    half = [jax.ShapeDtypeStruct((h.shape[0],) + h.shape[2:], h.dtype) for h in halves]
    out_shape = half + [jax.ShapeDtypeStruct((N_DEV,) + g.shape, g.dtype) for g in gathered]
    outs = _comm_call(body, name, list(halves) + list(gathered), out_shape,
                      (nh * (N_DEV // 2) + ng * (N_DEV - 1),), max(ng, 1))
    return list(outs[:nh]), list(outs[nh:])


def _gather_copies(src_refs, land_refs, send_sems, recv_sems):
    x, y, c, sib, chips = _mesh_place()
    me = 4 * x + 2 * y + c
    peers = [sib] + [(*chip, c) for chip in chips]
    return [pltpu.make_async_remote_copy(
                src_ref=src, dst_ref=land.at[me], send_sem=send_sems[4 * a + k], recv_sem=recv_sems[4 * a + k],
                device_id=peer, device_id_type=pl.DeviceIdType.MESH)
            for a, (src, land) in enumerate(zip(src_refs, land_refs)) for k, peer in enumerate(peers)]


def _forward_to_sibling(gathered, name):
    n = len(gathered)

    def body(*refs):
        outs = refs[n:2 * n]
        send_sems, recv_sems, _ = refs[2 * n:]
        x, y, c, sib, chips = _mesh_place()
        sends = []
        for a in range(n):
            for j, chip in enumerate(chips):
                rows = outs[a].at[4 * chip[0] + 2 * chip[1] + c]
                cp = _rdma(rows, rows, send_sems, recv_sems, (a, j), sib)
                cp.start()
                sends.append(cp)
        for cp in sends:
            cp.wait_send()
        for a in range(n):
            for j, chip in enumerate(chips):
                rows = outs[a].at[4 * chip[0] + 2 * chip[1] + (1 - c)]
                _rdma(rows, rows, send_sems, recv_sems, (a, j), sib).wait_recv()

    any_spec = pl.BlockSpec(memory_space=pl.ANY)
    return list(pl.pallas_call(
        body, name=name, in_specs=[any_spec] * n, out_specs=[any_spec] * n,
        out_shape=[jax.ShapeDtypeStruct(g.shape, g.dtype) for g in gathered],
        input_output_aliases={i: i for i in range(n)},
        scratch_shapes=[pltpu.SemaphoreType.DMA((n, 3)), pltpu.SemaphoreType.DMA((n, 3)),
                        pltpu.SemaphoreType.DMA((1,))],
        compiler_params=pltpu.CompilerParams(has_side_effects=True),
    )(*gathered))


def _split_start(copies_fn, srcs, land_shapes, n_sems, name, after=None):
    ns, nl = len(srcs), len(land_shapes)
    nc = n_sems
    hbm = pl.BlockSpec(memory_space=pltpu.HBM)
    sem = pl.BlockSpec(memory_space=pltpu.SEMAPHORE)

    n_in = ns + nl + (0 if after is None else 1)

    def body(*refs):
        src_refs, land_refs = refs[:ns], refs[ns:ns + nl]
        send_sems, recv_sems = refs[n_in:n_in + nc], refs[n_in + nc:n_in + 2 * nc]
        token = refs[-1]
        for cp in copies_fn(src_refs, land_refs, send_sems, recv_sems):
            cp.start()
        token[...] = jnp.zeros_like(token)

    lands = [lax.empty(s.shape, s.dtype) for s in land_shapes]
    thru = [pltpu.HBM(t.shape, t.dtype) for t in list(srcs) + lands]
    order = [] if after is None else [after]
    outs = pl.pallas_call(
        body, name=name,
        out_shape=(*([pltpu.SemaphoreType.DMA(())] * (2 * nc)), *thru, jax.ShapeDtypeStruct((8, LANE), F32)),
        in_specs=[hbm] * (ns + nl) + [pl.BlockSpec(memory_space=pl.ANY)] * len(order),
        out_specs=(*([sem] * (2 * nc)), *([hbm] * (ns + nl)), pl.BlockSpec(memory_space=pltpu.VMEM)),
        input_output_aliases={i: 2 * nc + i for i in range(ns + nl)},
        compiler_params=pltpu.CompilerParams(has_side_effects=pltpu.SideEffectType.DATAFLOW_SIDE_EFFECTING),
    )(*[pltpu.with_memory_space_constraint(t, pltpu.HBM) for t in list(srcs) + lands], *order)
    thru_out = outs[2 * nc:2 * nc + ns + nl]
    return list(outs[:nc]), list(outs[nc:2 * nc]), list(thru_out[:ns]), list(thru_out[ns:]), outs[-1]


def _split_wait(copies_fn, send_sems, recv_sems, srcs, lands, after, name):
    ns, nl, nc = len(srcs), len(lands), len(send_sems)
    hbm = pl.BlockSpec(memory_space=pltpu.HBM)
    sem = pl.BlockSpec(memory_space=pltpu.SEMAPHORE)

    def body(*refs):
        src_refs, land_refs = refs[:ns], refs[ns:ns + nl]
        send_refs, recv_refs = refs[ns + nl:ns + nl + nc], refs[ns + nl + nc:ns + nl + 2 * nc]
        for cp in copies_fn(src_refs, land_refs, send_refs, recv_refs):
            cp.wait_send()
            cp.wait_recv()

    outs = pl.pallas_call(
        body, name=name,
        out_shape=[pltpu.HBM(t.shape, t.dtype) for t in list(srcs) + list(lands)],
        in_specs=[hbm] * (ns + nl) + [sem] * (2 * nc) + [pl.BlockSpec(memory_space=pl.ANY)],
        out_specs=[hbm] * (ns + nl),
        input_output_aliases={i: i for i in range(ns + nl)},
        compiler_params=pltpu.CompilerParams(has_side_effects=pltpu.SideEffectType.DATAFLOW_SIDE_EFFECTING),
    )(*srcs, *lands, *send_sems, *recv_sems, after)
    return list(outs[:ns]), list(outs[ns:])


def _pair_copies(src_refs, land_refs, send_sems, recv_sems):
    x, y, c, sib, chips = _mesh_place()
    nchip = N_DEV // 2
    return [pltpu.make_async_remote_copy(
                src_ref=src.at[ch, 1 - c], dst_ref=land.at[ch], send_sem=send_sems[nchip * a + ch],
                recv_sem=recv_sems[nchip * a + ch], device_id=sib, device_id_type=pl.DeviceIdType.MESH)
            for a, (src, land) in enumerate(zip(src_refs, land_refs)) for ch in range(nchip)]


def _chip_copies(src_refs, land_refs, send_sems, recv_sems):
    x, y, c, sib, chips = _mesh_place()
    return [pltpu.make_async_remote_copy(
                src_ref=src.at[2 * chip[0] + chip[1]], dst_ref=land.at[j], send_sem=send_sems[3 * a + j],
                recv_sem=recv_sems[3 * a + j], device_id=(*chip, c), device_id_type=pl.DeviceIdType.MESH)
            for a, (src, land) in enumerate(zip(src_refs, land_refs)) for j, chip in enumerate(chips)]


def _pair_sum(halves, theirs, core, name, tr=256):
    nchip, _, R, C = halves.shape
    tr = _row_tile(R, tr)

    def body(core_ref, h_ref, t_ref, o_ref):
        o_ref[...] = (h_ref[...] + t_ref[...]).astype(BF16)

    grid_spec = pltpu.PrefetchScalarGridSpec(
        num_scalar_prefetch=1, grid=(nchip, R // tr),
        in_specs=[pl.BlockSpec((None, None, tr, C), lambda ch, i, core_ref: (ch, core_ref[0], i, 0)),
                  pl.BlockSpec((None, tr, C), lambda ch, i, core_ref: (ch, i, 0))],
        out_specs=pl.BlockSpec((None, tr, C), lambda ch, i, core_ref: (ch, i, 0)))
    return pl.pallas_call(
        body, name=name, grid_spec=grid_spec, out_shape=jax.ShapeDtypeStruct((nchip, R, C), BF16),
        compiler_params=pltpu.CompilerParams(
            dimension_semantics=("parallel", "parallel"), vmem_limit_bytes=_vmem(3 * _nbytes((tr, C), F32))),
    )(core, *[pltpu.with_memory_space_constraint(t, pltpu.HBM) for t in (halves, theirs)])


def _chip_sum(own, recv, chip, name, tr=256):
    _, R, C = own.shape
    tr = _row_tile(R, tr)

    def body(chip_ref, o_ref, r_ref, out_ref):
        acc = o_ref[...].astype(F32)
        for j in range(3):
            acc = acc + r_ref[j].astype(F32)
        out_ref[...] = acc

    grid_spec = pltpu.PrefetchScalarGridSpec(
        num_scalar_prefetch=1, grid=(R // tr,),
        in_specs=[pl.BlockSpec((None, tr, C), lambda i, chip_ref: (chip_ref[0], i, 0)),
                  pl.BlockSpec((3, tr, C), lambda i, chip_ref: (0, i, 0))],
        out_specs=pl.BlockSpec((tr, C), lambda i, chip_ref: (i, 0)))
    return pl.pallas_call(
        body, name=name, grid_spec=grid_spec, out_shape=jax.ShapeDtypeStruct((R, C), F32),
        compiler_params=pltpu.CompilerParams(
            dimension_semantics=("parallel",), vmem_limit_bytes=_vmem(4 * _nbytes((tr, C), F32))),
    )(chip, *[pltpu.with_memory_space_constraint(t, pltpu.HBM) for t in (own, recv)])


def _row_tile(rows, target):
    best = rows
    for t in range(8, min(rows, target) + 1, 8):
        if rows % t == 0:
            best = t
    return best


def _sum_slots(recv, name, tr=128):
    n, R, C = recv.shape
    tr = _row_tile(R, tr if recv.dtype == F32 else 2 * tr)

    def body(r_ref, o_ref):
        acc = r_ref[0].astype(F32)
        for k in range(1, n):
            acc = acc + r_ref[k].astype(F32)
        o_ref[...] = acc

    return pl.pallas_call(
        body, name=name, grid=(R // tr,),
        in_specs=[pl.BlockSpec((n, tr, C), lambda i: (0, i, 0))],
        out_specs=pl.BlockSpec((tr, C), lambda i: (i, 0)),
        out_shape=jax.ShapeDtypeStruct((R, C), F32),
        compiler_params=pltpu.CompilerParams(
            dimension_semantics=("parallel",), vmem_limit_bytes=_vmem(10 * _nbytes((tr, C), F32))),
    )(recv)


def _adamw_math(g_ref, w_ref, m_ref, v_ref, d_ref, nm_ref, nv_ref):
    gv = g_ref[...]
    nm = ADAM_B1 * m_ref[...] + (1.0 - ADAM_B1) * gv
    nv = ADAM_B2 * v_ref[...] + (1.0 - ADAM_B2) * (gv * gv)
    m_hat = nm / (1.0 - ADAM_B1 ** ADAM_STEP)
    v_hat = nv / (1.0 - ADAM_B2 ** ADAM_STEP)
    d_ref[...] = -ADAM_LR * (m_hat / (jnp.sqrt(v_hat) + ADAM_EPS) + ADAM_WD * w_ref[...])
    nm_ref[...] = nm
    nv_ref[...] = nv


def _adamw_small(params, name):
    n = len(params)
    two_d = [(-1, w.shape[-1]) for w, _, _, _ in params]
    flat = [t.reshape(two_d[i]) for i, (w, m, v, g) in enumerate(params) for t in (g, w, m, v)]

    def body(*refs):
        ins, outs = refs[:4 * n], refs[4 * n:]
        for i in range(n):
            _adamw_math(*ins[4 * i:4 * i + 4], *outs[3 * i:3 * i + 3])

    outs = pl.pallas_call(
        body, name=name,
        out_shape=[jax.ShapeDtypeStruct(flat[4 * i].shape, F32) for i in range(n) for _ in range(3)],
    )(*flat)
    return [(g.reshape(w.shape),) + tuple(outs[3 * i + t].reshape(w.shape) for t in range(3))
            for i, (w, m, v, g) in enumerate(params)]


def _adamw(g, w, m, v, name, tr=256):
    R, C = g.shape
    tr = _row_tile(R, tr)
    body = functools.partial(_adamw_math)

    row = pl.BlockSpec((tr, C), lambda i: (i, 0))
    return pl.pallas_call(
        body, name=name, grid=(R // tr,), in_specs=[row] * 4, out_specs=[row] * 3,
        out_shape=[jax.ShapeDtypeStruct((R, C), F32)] * 3,
        compiler_params=pltpu.CompilerParams(
            dimension_semantics=("parallel",), vmem_limit_bytes=_vmem(8 * _nbytes((tr, C), F32))),
    )(*[pltpu.with_memory_space_constraint(t, pltpu.HBM) for t in (g, w, m, v)])


def _pack_rows(parts):
    flat = jnp.concatenate([p.reshape(-1).astype(F32) for p in parts])
    rows = -(-flat.shape[0] // (8 * LANE)) * 8
    return jnp.pad(flat, (0, rows * LANE - flat.shape[0])).reshape(rows, LANE)


def _unpack(flat, shapes):
    out, off = [], 0
    for shp in shapes:
        n = math.prod(shp)
        out.append(flat[off:off + n].reshape(shp))
        off += n
    return out


def kernel(x, a_norm, a_w_in, a_conv, a_w_out, kv_norm, w_kv, b_norm, b_w_q, b_w_o, rel_bias, ffn_norm, ffn_w_up, ffn_conv, ffn_conv_b, ffn_w_down, final_norm, loss_target, m_a_norm, m_a_w_in, m_a_conv, m_a_w_out, m_kv_norm, m_w_kv, m_b_norm, m_b_w_q, m_b_w_o, m_rel_bias, m_ffn_norm, m_ffn_w_up, m_ffn_conv, m_ffn_conv_b, m_ffn_w_down, m_final_norm, v_a_norm, v_a_w_in, v_a_conv, v_a_w_out, v_kv_norm, v_w_kv, v_b_norm, v_b_w_q, v_b_w_o, v_rel_bias, v_ffn_norm, v_ffn_w_up, v_ffn_conv, v_ffn_conv_b, v_ffn_w_down, v_final_norm):
    B, S, D = x.shape
    T = B * S
    F = ffn_w_down.shape[1] * N_DEV
    me = 4 * lax.axis_index("x") + 2 * lax.axis_index("y") + lax.axis_index("c")

    big_shards = [a_w_in[0].T, a_w_out[0], w_kv.T, b_w_q[0], b_w_o[0],
                  ffn_w_up[0].T, ffn_w_up[1].T, ffn_w_down[0], ffn_w_down[1]]
    small_shapes = [a_norm.shape, a_conv.shape, ffn_conv.shape]
    small_pack = _pack_rows([a_norm, a_conv, ffn_conv])
    shards = [s.astype(BF16) for s in big_shards]
    first = _all_gather([shards[0], shards[1], small_pack], "gather_weights_first")
    win_t, wout = first[0].reshape(-1, D), first[1].reshape(-1, D)
    smalls = [_unpack(first[2][j].reshape(-1), small_shapes) for j in range(N_DEV)]
    a_norm_f = jnp.concatenate([s[0] for s in smalls], axis=-1)
    a_conv_f = jnp.concatenate([s[1] for s in smalls], axis=-1)[0]
    ffn_conv_f = jnp.concatenate([s[2] for s in smalls], axis=-1)

    def gather_start(tag, idxs, after):
        srcs = [shards[i] for i in idxs]
        lands = [jax.ShapeDtypeStruct((N_DEV,) + s.shape, s.dtype) for s in srcs]
        return _split_start(_gather_copies, srcs, lands, 4 * len(srcs), f"gather_start_{tag}", after=after)

    def gather_finish(tag, handle, after):
        send, recv, srcs, lands, _ = handle
        srcs, lands = _split_wait(_gather_copies, send, recv, srcs, lands, after, f"gather_wait_{tag}")
        lands = _forward_to_sibling(lands, f"gather_forward_{tag}")
        return [lax.dynamic_update_slice(g, s[None], (me, 0, 0)).reshape(-1, D) for g, s in zip(lands, srcs)]

    ffn0_w = gather_start("ffn0", [5, 7], first[0])
    rest_w = gather_start("rest", [2, 3, 4, 6, 8], ffn0_w[4])

    x2 = x.reshape(T, D)
    (xn0,) = _rmsnorm_fwd(x2, a_norm_f, "a_norm_fwd")
    bch = _matmul(xn0, win_t, mode="nt", out_dtype=BF16, after=rest_w[4], name="a_in_proj", tm=2048).reshape(B, S, 3 * D)
    gated = _shortconv_fwd(bch, a_conv_f, "a_gate_fwd").reshape(T, D)
    h1 = _matmul(gated, wout, mode="nn", out_dtype=F32, add=x2, name="a_out_proj")
    wup0_t, wdn0 = gather_finish("ffn0", ffn0_w, h1)
    wup_t, wdn = [wup0_t, None], [wdn0, None]

    def ffn_fwd(h, l):
        (xn,) = _rmsnorm_fwd(h, ffn_norm[l:l + 1], f"ffn{l}_norm_fwd")
        u0 = _matmul(xn, wup_t[l], mode="nt", out_dtype=BF16, name=f"ffn{l}_up", tm=2048, tn=1408).reshape(B, S, 2 * F)
        act = _ffn_gate_fwd(u0, ffn_conv_f[l], ffn_conv_b[l:l + 1], f"ffn{l}_gate_fwd").reshape(T, F)
        out = _matmul(act, wdn[l], mode="nn", out_dtype=F32, add=h, name=f"ffn{l}_down", tk=2816)
        return out, (xn, u0, act)

    h2, ffn0_saved = ffn_fwd(h1, 0)
    wkv_t, wq, wo, wup_t[1], wdn[1] = gather_finish("rest", rest_w, h2)
    kvn, xnb = _rmsnorm_fwd(h2, jnp.stack([kv_norm, b_norm[0]]), "kv_b_norm_fwd")
    kv = _matmul(kvn, wkv_t, mode="nt", out_dtype=F32, name="kv_proj", tm=2048).reshape(B, S, 2 * D)
    q = _matmul(xnb, wq, mode="nn", out_dtype=F32, name="q_proj", tm=2048).reshape(B, S, D)

    tables = [_band_tables(w, d) for (w, d) in DILATED_BRANCHES]
    onehots = jnp.stack([t[0] for t in tables])
    P = ATT_BLOCK
    bias_vals = _bias_lookup(rel_bias.T, onehots, "rel_bias_lookup").reshape(3, N_HEADS, P, 2 * P)
    in_cur = (jnp.arange(2 * P) >= P)[None, :]
    masks = jnp.stack([jnp.stack([t[1] & in_cur, t[1]]) for t in tables])
    tabs = jnp.where(masks[:, :, None], bias_vals[:, None], NEG)

    attn3, lse3 = _attn_fwd(q, kv, tabs, "attn_fwd")
    attn = attn3.reshape(T, D)
    h3 = _matmul(attn, wo, mode="nn", out_dtype=F32, add=h2, name="o_proj")
    h4, ffn1_saved = ffn_fwd(h3, 1)

    dh4, dh4b, d_final_norm, sq = _final_loss_bwd(h4, final_norm.reshape(1, D), loss_target.reshape(T, D), "loss_bwd")
    loss_part = 0.5 * jnp.sum(sq) / D

    core_idx = lax.axis_index("c").astype(jnp.int32).reshape(1)
    chip_idx = (2 * lax.axis_index("x") + lax.axis_index("y")).astype(jnp.int32).reshape(1)
    in_flight = []

    def group_start(tag, indexed_grads):
        halves = [g.reshape(N_DEV // 2, 2, g.shape[0] // N_DEV, D) for _, g in indexed_grads]
        lands = [jax.ShapeDtypeStruct((h.shape[0],) + h.shape[2:], h.dtype) for h in halves]
        send, recv, srcs, lnd, token = _split_start(_pair_copies, halves, lands, (N_DEV // 2) * len(halves),
                                                    f"exchange_core_start_{tag}")
        return (tag, [i for i, _ in indexed_grads], send, recv, srcs, lnd), token

    def group_finish(handle, after):
        tag, idxs, send, recv, srcs, lnd = handle
        halves, theirs = _split_wait(_pair_copies, send, recv, srcs, lnd, after, f"exchange_core_wait_{tag}")
        sums = [_pair_sum(h_, t_, core_idx, f"grad_core_sum_{tag}{k}")
                for k, (h_, t_) in enumerate(zip(halves, theirs))]
        lands = [jax.ShapeDtypeStruct((3,) + s.shape[1:], s.dtype) for s in sums]
        send, recv, srcs, lnd, token = _split_start(_chip_copies, sums, lands, 3 * len(sums),
                                                    f"exchange_chip_start_{tag}")
        in_flight.append((tag, idxs, send, recv, srcs, lnd))
        return token

    def ffn_bwd(dh_out, dh_out_b, h_in, saved, l, after):
        xn, u0, act = saved
        dact = _matmul(dh_out_b, wdn[l], mode="nt", out_dtype=BF16, after=after, name=f"ffn{l}_down_dx",
                       tm=2048, tn=1408).reshape(B, S, F)
        d_wdn = _matmul(act, dh_out_b, mode="tn", out_dtype=F32, name=f"ffn{l}_down_dw", tm=1408, tn=1024, tk=2048)
        du0, d_conv, d_conv_b = _ffn_gate_bwd(u0, dact, ffn_conv_f[l], ffn_conv_b[l:l + 1], f"ffn{l}_gate_bwd")
        du0 = du0.reshape(2, T, F)
        dxn = _matmul(du0, wup_t[l], mode="nn", out_dtype=BF16, a_parts=2, name=f"ffn{l}_up_dx", tk=2816)
        d_wup_t = _matmul(du0, xn, mode="tn", out_dtype=F32, a_parts=2, name=f"ffn{l}_up_dw", tm=1408, tn=1024, tk=2048)
        dh_in, dh_in_b, d_norm = _rmsnorm_bwd(h_in, ffn_norm[l:l + 1], [dxn], dh_out, f"ffn{l}_norm_bwd")
        return (dh_in, dh_in_b), (d_wdn, d_wup_t, d_conv, d_conv_b, d_norm)

    (dh3, dh3b), ffn1_grads = ffn_bwd(dh4, dh4b, h3, ffn1_saved, 1, None)
    group, token = group_start("ffn1", [(6, ffn1_grads[1]), (8, ffn1_grads[0])])

    dattn = _matmul(dh3b, wo, mode="nt", out_dtype=F32, after=token, name="o_proj_dx", tm=2048)
    token = group_finish(group, dattn)
    dattn = dattn.reshape(B, S, D)
    d_wo = _matmul(attn, dh3b, mode="tn", out_dtype=F32, after=token, name="o_proj_dw", tm=1024, tn=1024, tk=2048)
    dq, dkv, dtab = _attn_bwd(q, kv, dattn, attn3, lse3, tabs, "attn_bwd")
    dtabs = dtab.transpose(1, 0, 2, 3, 4).reshape(3, N_HEADS, P * 2 * P)
    d_rel_bias = _bias_grad(dtabs, onehots, "rel_bias_grad").T
    dq2 = dq.reshape(T, D)
    dkv2 = dkv.reshape(2, T, D)
    d_wq = _matmul(xnb, dq2, mode="tn", out_dtype=F32, name="q_proj_dw", tm=1024, tn=1024, tk=2048)
    d_wkv_t = _matmul(dkv2, kvn, mode="tn", out_dtype=F32, a_parts=2, name="kv_proj_dw", tm=1024, tn=1024, tk=2048)
    group, token = group_start("attn", [(2, d_wkv_t), (3, d_wq), (4, d_wo)])
    dxnb = _matmul(dq2, wq, mode="nt", out_dtype=BF16, after=token, name="q_proj_dx")
    dkvn = _matmul(dkv2, wkv_t, mode="nn", out_dtype=BF16, a_parts=2, name="kv_proj_dx")
    dh2, dh2b, d_kvb_norm = _rmsnorm_bwd(h2, jnp.stack([kv_norm, b_norm[0]]), [dkvn, dxnb], dh3, "kv_b_norm_bwd")
    token = group_finish(group, dh2b)

    (dh1, dh1b), ffn0_grads = ffn_bwd(dh2, dh2b, h1, ffn0_saved, 0, token)
    group, token = group_start("ffn0", [(5, ffn0_grads[1]), (7, ffn0_grads[0])])

    dgated = _matmul(dh1b, wout, mode="nt", out_dtype=BF16, after=token, name="a_out_proj_dx", tm=2048)
    token = group_finish(group, dgated)
    dgated = dgated.reshape(B, S, D)
    d_wout = _matmul(gated, dh1b, mode="tn", out_dtype=F32, after=token, name="a_out_proj_dw",
                     tm=1024, tn=1024, tk=2048)
    dbch, d_a_conv = _shortconv_bwd(bch, dgated, a_conv_f, "a_gate_bwd")
    dbch = dbch.reshape(3, T, D)
    dxn0 = _matmul(dbch, win_t, mode="nn", out_dtype=BF16, a_parts=3, name="a_in_proj_dx", tm=2048)
    d_win_t = _matmul(dbch, xn0, mode="tn", out_dtype=F32, a_parts=3, name="a_in_proj_dw", tm=1024, tn=1024, tk=2048)
    group, token = group_start("a", [(0, d_win_t), (1, d_wout)])
    grad_x, _, d_a_norm = _rmsnorm_bwd(x2, a_norm_f + token[0, 0], [dxn0], dh1, "a_norm_bwd")
    token = group_finish(group, grad_x)

    small_full = [d_a_norm, d_a_conv, jnp.stack([ffn0_grads[2], ffn1_grads[2]]),
                  d_kvb_norm[0], d_kvb_norm[1], d_rel_bias, jnp.concatenate([ffn0_grads[4], ffn1_grads[4]]),
                  jnp.concatenate([ffn0_grads[3], ffn1_grads[3]]), d_final_norm, loss_part]
    small_full_shapes = [(1, D), (3, D), (2, 3, 2 * F), (D,), (1, D), rel_bias.shape, (2, D), (2, 2 * F), (D,), ()]
    _, gath = _core_pair_exchange([], [_pack_rows(small_full)], "exchange_small")
    chip_sums, recv = {}, {}
    for tag, idxs, send, rcv, srcs, lnd in in_flight:
        srcs, lnd = _split_wait(_chip_copies, send, rcv, srcs, lnd, token, f"exchange_chip_wait_{tag}")
        for i, s_, l_ in zip(idxs, srcs, lnd):
            chip_sums[i], recv[i] = s_, l_
    small_sum = _sum_slots(gath[0], "small_grad_sum").reshape(-1)
    (g_a_norm_f, g_a_conv_f, g_ffn_conv_f, g_kv_norm, g_b_norm, g_rel_bias, g_ffn_norm, g_ffn_conv_b,
     g_final_norm, loss) = _unpack(small_sum, small_full_shapes)

    def my_cols(full, width):
        return lax.dynamic_slice_in_dim(full, me * width, width, axis=full.ndim - 1)

    g_a_norm = my_cols(g_a_norm_f, D // N_DEV)
    g_a_conv = my_cols(g_a_conv_f, D // N_DEV)[None]
    g_ffn_conv = my_cols(g_ffn_conv_f, 2 * F // N_DEV)

    big_w = [(a_w_in, m_a_w_in, v_a_w_in, True), (a_w_out, m_a_w_out, v_a_w_out, False),
             (w_kv, m_w_kv, v_w_kv, True), (b_w_q, m_b_w_q, v_b_w_q, False), (b_w_o, m_b_w_o, v_b_w_o, False),
             (ffn_w_up[0], m_ffn_w_up[0], v_ffn_w_up[0], True), (ffn_w_up[1], m_ffn_w_up[1], v_ffn_w_up[1], True),
             (ffn_w_down[0], m_ffn_w_down[0], v_ffn_w_down[0], False),
             (ffn_w_down[1], m_ffn_w_down[1], v_ffn_w_down[1], False)]
    big_out = []
    for i, (w, m, v, transposed) in enumerate(big_w):
        g = _chip_sum(chip_sums[i], recv[i], chip_idx, f"grad_sum_{i}")
        if transposed:
            g = g.T
        w2, m2, v2 = (t.reshape(g.shape) for t in (w, m, v))
        delta, nm, nv = _adamw(g, w2, m2, v2, f"adamw_{i}")
        big_out.append(tuple(t.reshape(w.shape) for t in (g, delta, nm, nv)))

    def pair(i, j):
        return tuple(jnp.stack([big_out[i][t], big_out[j][t]]) for t in range(4))

    small_w = [(a_norm, m_a_norm, v_a_norm, g_a_norm), (a_conv, m_a_conv, v_a_conv, g_a_conv),
               (ffn_conv, m_ffn_conv, v_ffn_conv, g_ffn_conv), (kv_norm, m_kv_norm, v_kv_norm, g_kv_norm),
               (b_norm, m_b_norm, v_b_norm, g_b_norm), (rel_bias, m_rel_bias, v_rel_bias, g_rel_bias),
               (ffn_norm, m_ffn_norm, v_ffn_norm, g_ffn_norm),
               (ffn_conv_b, m_ffn_conv_b, v_ffn_conv_b, g_ffn_conv_b),
               (final_norm, m_final_norm, v_final_norm, g_final_norm)]
    small_out = _adamw_small(small_w, "adamw_small")

    per_weight = [small_out[0], big_out[0], small_out[1], big_out[1], small_out[3], big_out[2], small_out[4],
                  big_out[3], big_out[4], small_out[5], small_out[6], pair(5, 6), small_out[2], small_out[7],
                  pair(7, 8), small_out[8]]
    outs = [loss, grad_x.reshape(B, S, D)]
    for t in range(4):
        outs.extend(pw[t] for pw in per_weight)
    return tuple(outs)
```

```python
import functools
import math

import jax
import jax.numpy as jnp
from jax import lax
from jax.experimental import pallas as pl
from jax.experimental.pallas import tpu as pltpu

F32 = jnp.float32
BF16 = jnp.bfloat16

N_DEV = 8
N_HEADS = 16
ATT_BLOCK = 128
ATT_UNROLL = 8
DILATED_BRANCHES = ((128, 1), (512, 4), (2048, 16))
REL_BUCKETS = 32
REL_MAX_DISTANCE = 2048
RMS_EPS = 1e-6
ADAM_LR = 0.001
ADAM_B1 = 0.9
ADAM_B2 = 0.999
ADAM_EPS = 1e-08
ADAM_WD = 0.01
ADAM_STEP = 10

LANE = 128
HALO = 16
NEG = -1e30
VMEM_CAP = 56 << 20


def _vmem(block_bytes):
    return int(min(VMEM_CAP, max(32 << 20, 3 * block_bytes + (8 << 20))))


def _nbytes(shape, dtype):
    return math.prod(shape) * jnp.dtype(dtype).itemsize


def _tile(dim, target):
    best = None
    t = LANE
    while t <= min(dim, target):
        if dim % t == 0:
            best = t
        t += LANE
    return best if best is not None else dim


def _matmul(a, b, *, mode, out_dtype, name, add=None, after=None, a_parts=1, tm=1024, tn=1024, tk=1024):
    P = a_parts
    if mode == "nn":
        M, K = (a.shape[0], a.shape[1]) if P == 1 else (a.shape[1], a.shape[2] * P)
        N = b.shape[1]
    elif mode == "nt":
        assert P == 1
        M, K = a.shape
        N = b.shape[0]
    else:
        K = a.shape[0] if P == 1 else a.shape[1]
        M = a.shape[1] if P == 1 else a.shape[2] * P
        N = b.shape[1]
    tm = _tile(M // P if mode == "tn" else M, tm)
    tn = _tile(N, tn)
    tk = _tile(K // P if mode == "nn" else K, tk)
    nm, nn_, nk = M // tm, N // tn, K // tk

    if mode == "nn":
        if P == 1:
            a_spec = pl.BlockSpec((tm, tk), lambda i, j, k: (i, k))
        else:
            nkp = nk // P
            a_spec = pl.BlockSpec((None, tm, tk), lambda i, j, k: (k // nkp, i, k % nkp))
        b_spec = pl.BlockSpec((tk, tn), lambda i, j, k: (k, j))
        dims = (((1,), (0,)), ((), ()))
    elif mode == "nt":
        a_spec = pl.BlockSpec((tm, tk), lambda i, j, k: (i, k))
        b_spec = pl.BlockSpec((tn, tk), lambda i, j, k: (j, k))
        dims = (((1,), (1,)), ((), ()))
    else:
        if P == 1:
            a_spec = pl.BlockSpec((tk, tm), lambda i, j, k: (k, i))
        else:
            nmp = nm // P
            a_spec = pl.BlockSpec((None, tk, tm), lambda i, j, k: (i // nmp, k, i % nmp))
        b_spec = pl.BlockSpec((tk, tn), lambda i, j, k: (k, j))
        dims = (((0,), (0,)), ((), ()))
    o_spec = pl.BlockSpec((tm, tn), lambda i, j, k: (i, j))
    in_specs = [a_spec, b_spec]
    args = [a, b]
    if add is not None:
        in_specs.append(o_spec)
        args.append(add)
    has_add = add is not None
    if after is not None:
        in_specs.append(pl.BlockSpec(memory_space=pl.ANY))
        args.append(after)
    n_extra = len(args) - 2

    direct = nk > 1 and out_dtype == F32 and not has_add

    def body(a_ref, b_ref, *rest):
        add_ref = rest[0] if has_add else None
        o_ref = rest[n_extra]
        acc_ref = o_ref if direct or nk == 1 else rest[-1]
        k = pl.program_id(2)

        def part():
            return lax.dot_general(a_ref[...].astype(BF16), b_ref[...].astype(BF16), dims,
                                   preferred_element_type=F32)

        def finish(r):
            if has_add:
                r = r + add_ref[...].astype(F32)
            o_ref[...] = r.astype(out_dtype)

        if nk == 1:
            finish(part())
        else:
            @pl.when(k == 0)
            def _():
                acc_ref[...] = part()

            @pl.when(jnp.logical_and(k > 0, jnp.logical_or(k < nk - 1, direct)))
            def _():
                acc_ref[...] += part()

            if not direct:
                @pl.when(k == nk - 1)
                def _():
                    finish(acc_ref[...] + part())

    blk = (_nbytes((tm, tk), a.dtype) + _nbytes((tk, tn), b.dtype) + _nbytes((tm, tn), out_dtype)
           + (_nbytes((tm, tn), add.dtype) if has_add else 0)) * 2 + 3 * _nbytes((tm, tn), F32)
    return pl.pallas_call(
        body, name=name, grid=(nm, nn_, nk),
        in_specs=in_specs, out_specs=o_spec,
        out_shape=jax.ShapeDtypeStruct((M, N), out_dtype),
        scratch_shapes=[] if direct or nk == 1 else [pltpu.VMEM((tm, tn), F32)],
        compiler_params=pltpu.CompilerParams(
            dimension_semantics=("parallel", "parallel", "arbitrary"), vmem_limit_bytes=_vmem(blk)),
    )(*args)


def _rmsnorm_fwd(x, gains, name, tm=1024):
    T, D = x.shape
    n = gains.shape[0]
    tm = _tile(T, tm)

    def body(x_ref, g_ref, *o_refs):
        xv = x_ref[...]
        xhat = xv * lax.rsqrt(jnp.mean(xv * xv, axis=-1, keepdims=True) + RMS_EPS)
        for i in range(n):
            o_refs[i][...] = (xhat * g_ref[i:i + 1, :]).astype(BF16)

    row = pl.BlockSpec((tm, D), lambda i: (i, 0))
    outs = pl.pallas_call(
        body, name=name, grid=(T // tm,),
        in_specs=[row, pl.BlockSpec((n, D), lambda i: (0, 0))],
        out_specs=[row] * n,
        out_shape=[jax.ShapeDtypeStruct((T, D), BF16)] * n,
        compiler_params=pltpu.CompilerParams(
            dimension_semantics=("parallel",), vmem_limit_bytes=_vmem(4 * _nbytes((tm, D), F32))),
    )(x, gains)
    return tuple(outs)


def _rmsnorm_bwd(x, gains, dxns, dres, name, tm=1024):
    T, D = x.shape
    n = gains.shape[0]
    tm = _tile(T, tm)

    def body(x_ref, g_ref, *rest):
        dxn_refs = rest[:n]
        dres_ref, dx_ref, dxb_ref, dg_ref = rest[n:]
        xv = x_ref[...]
        rstd = lax.rsqrt(jnp.mean(xv * xv, axis=-1, keepdims=True) + RMS_EPS)
        xhat = xv * rstd
        dx = dres_ref[...]

        @pl.when(pl.program_id(0) == 0)
        def _():
            dg_ref[...] = jnp.zeros_like(dg_ref)

        for i in range(n):
            dy = dxn_refs[i][...].astype(F32)
            dg_ref[i:i + 1, :] += jnp.sum(dy * xhat, axis=0, keepdims=True)
            dxh = dy * g_ref[i:i + 1, :]
            dx = dx + rstd * (dxh - xhat * jnp.mean(dxh * xhat, axis=-1, keepdims=True))
        dx_ref[...] = dx
        dxb_ref[...] = dx.astype(BF16)

    row = pl.BlockSpec((tm, D), lambda i: (i, 0))
    par = pl.BlockSpec((n, D), lambda i: (0, 0))
    return pl.pallas_call(
        body, name=name, grid=(T // tm,),
        in_specs=[row, par] + [row] * n + [row],
        out_specs=[row, row, par],
        out_shape=[jax.ShapeDtypeStruct((T, D), F32), jax.ShapeDtypeStruct((T, D), BF16),
                   jax.ShapeDtypeStruct((n, D), F32)],
        compiler_params=pltpu.CompilerParams(
            dimension_semantics=("arbitrary",), vmem_limit_bytes=_vmem((4 + n) * _nbytes((tm, D), F32))),
    )(x, gains, *dxns, dres)


def _final_loss_bwd(h, gain, target, name, tm=1024):
    T, D = h.shape
    tm = _tile(T, tm)

    def body(h_ref, g_ref, t_ref, dh_ref, dhb_ref, dg_ref, sq_ref):
        xv = h_ref[...]
        rstd = lax.rsqrt(jnp.mean(xv * xv, axis=-1, keepdims=True) + RMS_EPS)
        xhat = xv * rstd
        err = xhat * g_ref[...] - t_ref[...]

        @pl.when(pl.program_id(0) == 0)
        def _():
            dg_ref[...] = jnp.zeros_like(dg_ref)
            sq_ref[...] = jnp.zeros_like(sq_ref)

        sq_ref[...] += jnp.sum(err * err, axis=0, keepdims=True)
        dy = err * (1.0 / D)
        dg_ref[...] += jnp.sum(dy * xhat, axis=0, keepdims=True)
        dxh = dy * g_ref[...]
        dh = rstd * (dxh - xhat * jnp.mean(dxh * xhat, axis=-1, keepdims=True))
        dh_ref[...] = dh
        dhb_ref[...] = dh.astype(BF16)

    row = pl.BlockSpec((tm, D), lambda i: (i, 0))
    par = pl.BlockSpec((1, D), lambda i: (0, 0))
    return pl.pallas_call(
        body, name=name, grid=(T // tm,),
        in_specs=[row, par, row], out_specs=[row, row, par, par],
        out_shape=[jax.ShapeDtypeStruct((T, D), F32), jax.ShapeDtypeStruct((T, D), BF16),
                   jax.ShapeDtypeStruct((1, D), F32), jax.ShapeDtypeStruct((1, D), F32)],
        compiler_params=pltpu.CompilerParams(
            dimension_semantics=("arbitrary",), vmem_limit_bytes=_vmem(5 * _nbytes((tm, D), F32))),
    )(h, gain, target)


def _halo_specs(S, ts, tc, col_off, prev, nxt):
    r = ts // HALO
    last = S // HALO - 1
    specs = []
    if prev:
        specs.append(pl.BlockSpec((None, HALO, tc), lambda j, b, s: (b, jnp.maximum(s * r - 1, 0), col_off + j)))
    specs.append(pl.BlockSpec((None, ts, tc), lambda j, b, s: (b, s, col_off + j)))
    if nxt:
        specs.append(pl.BlockSpec((None, HALO, tc), lambda j, b, s: (b, jnp.minimum((s + 1) * r, last), col_off + j)))
    return specs


def _ext(prev_ref, main_ref, next_ref, first, last):
    main = main_ref[...].astype(F32)
    zeros = jnp.zeros((HALO, main.shape[1]), F32)
    top = zeros if prev_ref is None else jnp.where(first, 0.0, prev_ref[...].astype(F32))
    bot = zeros if next_ref is None else jnp.where(last, 0.0, next_ref[...].astype(F32))
    return jnp.concatenate([top, main, bot], axis=0)


def _shift(xe, k):
    return pltpu.roll(xe, k % xe.shape[0], axis=0)


def _cconv(xe, w):
    return w[0:1, :] * _shift(xe, 2) + w[1:2, :] * _shift(xe, 1) + w[2:3, :] * xe


def _main(xe, ts):
    return xe[HALO:HALO + ts, :]


def _cconv_t(de, w):
    return w[2:3, :] * de + w[1:2, :] * _shift(de, -1) + w[0:1, :] * _shift(de, -2)


def _conv_wgrad(dw_ref, de, xe, ts):
    d = _main(de, ts)
    dw_ref[0:1, :] += jnp.sum(d * _main(_shift(xe, 2), ts), axis=0, keepdims=True)
    dw_ref[1:2, :] += jnp.sum(d * _main(_shift(xe, 1), ts), axis=0, keepdims=True)
    dw_ref[2:3, :] += jnp.sum(d * _main(xe, ts), axis=0, keepdims=True)


def _gate_tiles(S, C, rows=1024):
    return _tile(S, rows), _tile(C, 512)


def _shortconv_fwd(bch, conv_w, name):
    B, S, D3 = bch.shape
    D = D3 // 3
    ts, tc = _gate_tiles(S, D, rows=2048)
    nj, ns = D // tc, S // ts

    def body(b_ref, cp_ref, c_ref, hp_ref, h_ref, w_ref, o_ref):
        first = pl.program_id(2) == 0
        ce = _ext(cp_ref, c_ref, None, first, False)
        he = _ext(hp_ref, h_ref, None, first, False)
        cv = _main(_cconv(ce * he, w_ref[...]), ts)
        o_ref[...] = (b_ref[...].astype(F32) * cv).astype(BF16)

    in_specs = (_halo_specs(S, ts, tc, 0, False, False) + _halo_specs(S, ts, tc, nj, True, False)
                + _halo_specs(S, ts, tc, 2 * nj, True, False) + [pl.BlockSpec((3, tc), lambda j, b, s: (0, j))])
    return pl.pallas_call(
        body, name=name, grid=(nj, B, ns), in_specs=in_specs,
        out_specs=pl.BlockSpec((None, ts, tc), lambda j, b, s: (b, s, j)),
        out_shape=jax.ShapeDtypeStruct((B, S, D), BF16),
        compiler_params=pltpu.CompilerParams(
            dimension_semantics=("parallel", "parallel", "parallel"),
            vmem_limit_bytes=_vmem(12 * _nbytes((ts + 2 * HALO, tc), F32))),
    )(bch, bch, bch, bch, bch, conv_w)


def _shortconv_bwd(bch, dg, conv_w, name):
    B, S, D3 = bch.shape
    D = D3 // 3
    ts, tc = _gate_tiles(S, D)
    nj, ns = D // tc, S // ts

    def body(b_ref, bn_ref, cp_ref, c_ref, cn_ref, hp_ref, h_ref, hn_ref, d_ref, dn_ref, w_ref, o_ref, dw_ref):
        s = pl.program_id(2)
        first, last = s == 0, s == ns - 1

        @pl.when(jnp.logical_and(pl.program_id(1) == 0, s == 0))
        def _():
            dw_ref[...] = jnp.zeros_like(dw_ref)

        w = w_ref[...]
        be = _ext(None, b_ref, bn_ref, first, last)
        ce = _ext(cp_ref, c_ref, cn_ref, first, last)
        he = _ext(hp_ref, h_ref, hn_ref, first, last)
        de = _ext(None, d_ref, dn_ref, first, last)
        ch = ce * he
        dcv = de * be
        dch = _main(_cconv_t(dcv, w), ts)
        o_ref[0] = (_main(de, ts) * _main(_cconv(ch, w), ts)).astype(BF16)
        o_ref[1] = (dch * _main(he, ts)).astype(BF16)
        o_ref[2] = (dch * _main(ce, ts)).astype(BF16)
        _conv_wgrad(dw_ref, dcv, ch, ts)

    in_specs = (_halo_specs(S, ts, tc, 0, False, True) + _halo_specs(S, ts, tc, nj, True, True)
                + _halo_specs(S, ts, tc, 2 * nj, True, True) + _halo_specs(S, ts, tc, 0, False, True)
                + [pl.BlockSpec((3, tc), lambda j, b, s: (0, j))])
    return pl.pallas_call(
        body, name=name, grid=(nj, B, ns), in_specs=in_specs,
        out_specs=[pl.BlockSpec((3, None, ts, tc), lambda j, b, s: (0, b, s, j)),
                   pl.BlockSpec((3, tc), lambda j, b, s: (0, j))],
        out_shape=[jax.ShapeDtypeStruct((3, B, S, D), BF16), jax.ShapeDtypeStruct((3, D), F32)],
        compiler_params=pltpu.CompilerParams(
            dimension_semantics=("parallel", "arbitrary", "arbitrary"),
            vmem_limit_bytes=_vmem(24 * _nbytes((ts + 2 * HALO, tc), F32))),
    )(bch, bch, bch, bch, bch, bch, bch, bch, dg, dg, conv_w)


def _sigmoid(x):
    return 1.0 / (1.0 + jnp.exp(-x))


def _ffn_gate_fwd(u0, conv_w, conv_b, name):
    B, S, F2 = u0.shape
    F = F2 // 2
    ts, tc = _gate_tiles(S, F, rows=2048)
    nj, ns = F // tc, S // ts

    def body(gp_ref, g_ref, up_ref, u_ref, wg_ref, wu_ref, bg_ref, bu_ref, o_ref):
        first = pl.program_id(2) == 0
        ug = _main(_cconv(_ext(gp_ref, g_ref, None, first, False), wg_ref[...]), ts) + bg_ref[...]
        uu = _main(_cconv(_ext(up_ref, u_ref, None, first, False), wu_ref[...]), ts) + bu_ref[...]
        o_ref[...] = (ug * _sigmoid(ug) * uu).astype(BF16)

    in_specs = (_halo_specs(S, ts, tc, 0, True, False) + _halo_specs(S, ts, tc, nj, True, False)
                + [pl.BlockSpec((3, tc), lambda j, b, s: (0, j)), pl.BlockSpec((3, tc), lambda j, b, s: (0, nj + j)),
                   pl.BlockSpec((1, tc), lambda j, b, s: (0, j)), pl.BlockSpec((1, tc), lambda j, b, s: (0, nj + j))])
    return pl.pallas_call(
        body, name=name, grid=(nj, B, ns), in_specs=in_specs,
        out_specs=pl.BlockSpec((None, ts, tc), lambda j, b, s: (b, s, j)),
        out_shape=jax.ShapeDtypeStruct((B, S, F), BF16),
        compiler_params=pltpu.CompilerParams(
            dimension_semantics=("parallel", "parallel", "parallel"),
            vmem_limit_bytes=_vmem(12 * _nbytes((ts + 2 * HALO, tc), F32))),
    )(u0, u0, u0, u0, conv_w, conv_w, conv_b, conv_b)


def _ffn_gate_bwd(u0, dact, conv_w, conv_b, name):
    B, S, F2 = u0.shape
    F = F2 // 2
    ts, tc = _gate_tiles(S, F)
    nj, ns = F // tc, S // ts

    def body(gp_ref, g_ref, gn_ref, up_ref, u_ref, un_ref, d_ref, dn_ref, wg_ref, wu_ref, bg_ref, bu_ref,
             o_ref, dwg_ref, dwu_ref, dbg_ref, dbu_ref):
        s = pl.program_id(2)
        first, last = s == 0, s == ns - 1

        @pl.when(jnp.logical_and(pl.program_id(1) == 0, s == 0))
        def _():
            for r in (dwg_ref, dwu_ref, dbg_ref, dbu_ref):
                r[...] = jnp.zeros_like(r)

        wg, wu = wg_ref[...], wu_ref[...]
        ge = _ext(gp_ref, g_ref, gn_ref, first, last)
        ue = _ext(up_ref, u_ref, un_ref, first, last)
        de = _ext(None, d_ref, dn_ref, first, last)
        ug = _cconv(ge, wg) + bg_ref[...]
        uu = _cconv(ue, wu) + bu_ref[...]
        sig = _sigmoid(ug)
        dug = de * uu * (sig * (1.0 + ug * (1.0 - sig)))
        duu = de * (ug * sig)
        o_ref[0] = _main(_cconv_t(dug, wg), ts).astype(BF16)
        o_ref[1] = _main(_cconv_t(duu, wu), ts).astype(BF16)
        _conv_wgrad(dwg_ref, dug, ge, ts)
        _conv_wgrad(dwu_ref, duu, ue, ts)
        dbg_ref[...] += jnp.sum(_main(dug, ts), axis=0, keepdims=True)
        dbu_ref[...] += jnp.sum(_main(duu, ts), axis=0, keepdims=True)

    w3 = lambda off: pl.BlockSpec((3, tc), lambda j, b, s: (0, off + j))
    w1 = lambda off: pl.BlockSpec((1, tc), lambda j, b, s: (0, off + j))
    in_specs = (_halo_specs(S, ts, tc, 0, True, True) + _halo_specs(S, ts, tc, nj, True, True)
                + _halo_specs(S, ts, tc, 0, False, True) + [w3(0), w3(nj), w1(0), w1(nj)])
    outs = pl.pallas_call(
        body, name=name, grid=(nj, B, ns), in_specs=in_specs,
        out_specs=[pl.BlockSpec((2, None, ts, tc), lambda j, b, s: (0, b, s, j)), w3(0), w3(0), w1(0), w1(0)],
        out_shape=[jax.ShapeDtypeStruct((2, B, S, F), BF16), jax.ShapeDtypeStruct((3, F), F32),
                   jax.ShapeDtypeStruct((3, F), F32), jax.ShapeDtypeStruct((1, F), F32),
                   jax.ShapeDtypeStruct((1, F), F32)],
        compiler_params=pltpu.CompilerParams(
            dimension_semantics=("parallel", "arbitrary", "arbitrary"),
            vmem_limit_bytes=_vmem(30 * _nbytes((ts + 2 * HALO, tc), F32))),
    )(u0, u0, u0, u0, u0, u0, dact, dact, conv_w, conv_w, conv_b, conv_b)
    du0, dwg, dwu, dbg, dbu = outs
    return du0, jnp.concatenate([dwg, dwu], axis=1), jnp.concatenate([dbg, dbu], axis=1)


def _t5_bucket(dist):
    max_exact = REL_BUCKETS // 2
    n = jnp.maximum(dist, 0)
    nf = jnp.maximum(n, max_exact).astype(F32)
    large = max_exact + (jnp.log(nf / max_exact) / math.log(REL_MAX_DISTANCE / max_exact)
                         * (REL_BUCKETS - max_exact)).astype(jnp.int32)
    large = jnp.minimum(large, REL_BUCKETS - 1)
    return jnp.where(n < max_exact, n, large)


def _band_tables(window, dilation):
    P = ATT_BLOCK
    qi = jnp.arange(P, dtype=jnp.int32)[:, None]
    kc = jnp.arange(2 * P, dtype=jnp.int32)[None, :]
    delta = qi + P - kc
    band = (delta >= 0) & (delta <= window // dilation)
    bucket = _t5_bucket(delta * dilation).reshape(-1)
    onehot = (bucket[None, :] == jnp.arange(REL_BUCKETS, dtype=jnp.int32)[:, None]).astype(F32)
    return onehot, band


def _bias_lookup(rel_bias_t, onehots, name):
    nb, _, Q = onehots.shape
    H = rel_bias_t.shape[0]

    def body(r_ref, oh_ref, o_ref):
        o_ref[...] = lax.dot_general(r_ref[...], oh_ref[...], (((1,), (0,)), ((), ())),
                                     precision=lax.Precision.HIGHEST, preferred_element_type=F32)

    return pl.pallas_call(
        body, name=name, grid=(nb,),
        in_specs=[pl.BlockSpec((H, REL_BUCKETS), lambda i: (0, 0)),
                  pl.BlockSpec((None, REL_BUCKETS, Q), lambda i: (i, 0, 0))],
        out_specs=pl.BlockSpec((None, H, Q), lambda i: (i, 0, 0)),
        out_shape=jax.ShapeDtypeStruct((nb, H, Q), F32),
        compiler_params=pltpu.CompilerParams(dimension_semantics=("parallel",),
                                             vmem_limit_bytes=_vmem(_nbytes((REL_BUCKETS + H, Q), F32))),
    )(rel_bias_t, onehots)


def _bias_grad(dtabs, onehots, name):
    nb, H, Q = dtabs.shape

    def body(d_ref, oh_ref, o_ref):
        @pl.when(pl.program_id(0) == 0)
        def _():
            o_ref[...] = jnp.zeros_like(o_ref)

        o_ref[...] += lax.dot_general(d_ref[...], oh_ref[...], (((1,), (1,)), ((), ())),
                                      precision=lax.Precision.HIGHEST, preferred_element_type=F32)

    return pl.pallas_call(
        body, name=name, grid=(nb,),
        in_specs=[pl.BlockSpec((None, H, Q), lambda i: (i, 0, 0)),
                  pl.BlockSpec((None, REL_BUCKETS, Q), lambda i: (i, 0, 0))],
        out_specs=pl.BlockSpec((H, REL_BUCKETS), lambda i: (0, 0)),
        out_shape=jax.ShapeDtypeStruct((H, REL_BUCKETS), F32),
        compiler_params=pltpu.CompilerParams(dimension_semantics=("arbitrary",),
                                             vmem_limit_bytes=_vmem(_nbytes((REL_BUCKETS + H, Q), F32))),
    )(dtabs, onehots)


def _lane_masks():
    lane = lax.broadcasted_iota(jnp.int32, (1, LANE), 1)
    lo = lane < LANE // 2
    return lo, jnp.logical_not(lo)


def _dot_nt(a, b):
    return lax.dot_general(a, b, (((1,), (1,)), ((), ())), preferred_element_type=F32)


def _dot_nn(a, b):
    return lax.dot_general(a, b, (((1,), (0,)), ((), ())), preferred_element_type=F32)


def _dot_tn(a, b):
    return lax.dot_general(a, b, (((0,), (0,)), ((), ())), preferred_element_type=F32)


def _block_rows(n, dilation, S):
    P = ATT_BLOCK
    nb = S // (dilation * P)
    r, i = n // nb, n % nb
    cur = pl.ds(i * (P * dilation) + r, P, stride=dilation)
    prv = pl.ds(jnp.maximum(i - 1, 0) * (P * dilation) + r, P, stride=dilation)
    return cur, prv, jnp.minimum(i, 1)


def _attn_fwd(q, kv, tabs, name):
    B, S, D = q.shape
    P, H = ATT_BLOCK, N_HEADS
    scale = (D // H) ** -0.5
    half = LANE // 2
    nsteps = S // P // ATT_UNROLL

    def body(q_ref, k_ref, v_ref, tab_ref, o_ref, lse_ref, acc_ref, m_ref, l_ref):
        lo, hi = _lane_masks()
        for bi, (_, d) in enumerate(DILATED_BRANCHES):
            def block(n, carry, bi=bi, d=d):
                rows = [_block_rows(n + j * nsteps, d, S) for j in range(ATT_UNROLL)]
                loaded = []
                for cur, prv, variant in rows:
                    qb = q_ref[cur, :].astype(BF16)
                    kw = jnp.concatenate([k_ref[prv, :], k_ref[cur, :]], axis=0).astype(BF16)
                    vw = jnp.concatenate([v_ref[prv, :], v_ref[cur, :]], axis=0).astype(BF16)
                    loaded.append((qb, kw, vw))
                results = []
                for (cur, prv, variant), (qb, kw, vw) in zip(rows, loaded):
                    m_pair = l_pair = a_pair = None
                    for e, msk in enumerate((lo, hi)):
                        s = _dot_nt(jnp.where(msk, qb, 0), kw) * scale + tab_ref[bi, variant, e]
                        m_e = jnp.max(s, axis=-1, keepdims=True)
                        pe = jnp.exp(s - m_e)
                        l_e = jnp.sum(pe, axis=-1, keepdims=True)
                        a_e = _dot_nn(pe.astype(BF16), jnp.where(msk, vw, 0))
                        m_pair = m_e if e == 0 else jnp.where(lo, m_pair, m_e)
                        l_pair = l_e if e == 0 else jnp.where(lo, l_pair, l_e)
                        a_pair = a_e if e == 0 else a_pair + a_e
                    results.append((m_pair, l_pair, a_pair))
                for (cur, prv, variant), (m_pair, l_pair, a_pair) in zip(rows, results):
                    m_ref[bi, cur, :] = m_pair
                    l_ref[bi, cur, :] = l_pair
                    acc_ref[bi, cur, :] = a_pair
                return carry

            lax.fori_loop(0, nsteps, block, 0)

        nbr = len(DILATED_BRANCHES)
        chunk = 512

        def merge(t, carry):
            rs = pl.ds(pl.multiple_of(t * chunk, chunk), chunk)
            ms = [m_ref[b, rs, :] for b in range(nbr)]
            m = functools.reduce(jnp.maximum, ms)
            ws = [jnp.exp(mb - m) for mb in ms]
            l = functools.reduce(jnp.add, [w * l_ref[b, rs, :] for b, w in enumerate(ws)])
            acc = functools.reduce(jnp.add, [w * acc_ref[b, rs, :] for b, w in enumerate(ws)])
            o_ref[rs, :] = (acc / l).astype(BF16)
            lse_ref[rs, :] = m + jnp.log(l)
            return carry

        lax.fori_loop(0, S // chunk, merge, 0)

    nl = D // LANE
    col = lambda off: pl.BlockSpec((None, S, LANE), lambda b, p: (b, 0, off + p))
    tab_spec = pl.BlockSpec((3, 2, 2, P, 2 * P), lambda b, p: (0, 0, p, 0, 0))
    return pl.pallas_call(
        body, name=name, grid=(B, H // 2), in_specs=[col(0), col(0), col(nl), tab_spec],
        out_specs=[col(0), col(0)],
        out_shape=[jax.ShapeDtypeStruct((B, S, D), BF16), jax.ShapeDtypeStruct((B, S, D), F32)],
        scratch_shapes=[pltpu.VMEM((len(DILATED_BRANCHES), S, LANE), F32)] * 3,
        compiler_params=pltpu.CompilerParams(
            dimension_semantics=("parallel", "parallel"), vmem_limit_bytes=VMEM_CAP),
    )(q, kv, kv, tabs)


def _attn_bwd(q, kv, do, o, lse, tabs, name):
    B, S, D = q.shape
    P, H = ATT_BLOCK, N_HEADS
    scale = (D // H) ** -0.5
    half = LANE // 2
    nsteps = S // P // ATT_UNROLL

    def body(q_ref, k_ref, v_ref, do_ref, o_ref, lse_ref, tab_ref, dq_out, dkv_out, dtab_ref, delta_ref,
             dq_ref, dkv_ref):
        lo, hi = _lane_masks()

        @pl.when(pl.program_id(1) == 0)
        def _():
            dtab_ref[...] = jnp.zeros_like(dtab_ref)

        dkv_ref[...] = jnp.zeros_like(dkv_ref)
        prod = do_ref[...] * o_ref[...].astype(F32)
        delta_ref[...] = jnp.where(lo, jnp.sum(jnp.where(lo, prod, 0.0), axis=-1, keepdims=True),
                                   jnp.sum(jnp.where(hi, prod, 0.0), axis=-1, keepdims=True))

        for bi, (_, d) in enumerate(DILATED_BRANCHES):
            def block(n, carry, bi=bi, d=d):
                rows = [_block_rows(n + j * nsteps, d, S) for j in range(ATT_UNROLL)]
                loaded = []
                for cur, prv, variant in rows:
                    qb = q_ref[cur, :].astype(BF16)
                    kw = jnp.concatenate([k_ref[prv, :], k_ref[cur, :]], axis=0).astype(BF16)
                    vw = jnp.concatenate([v_ref[prv, :], v_ref[cur, :]], axis=0).astype(BF16)
                    dob = do_ref[cur, :].astype(BF16)
                    dq_old = dq_ref[cur, :] if bi > 0 else None
                    loaded.append((qb, kw, vw, dob, lse_ref[cur, :], delta_ref[cur, :], dq_old))
                results = []
                for j, ((cur, prv, variant), (qb, kw, vw, dob, lse_b, dl_b, dq_old)) in enumerate(zip(rows, loaded)):
                    dq_pair = dk_pair = dv_pair = None
                    for e, msk in enumerate((lo, hi)):
                        qm = jnp.where(msk, qb, 0)
                        dom = jnp.where(msk, dob, 0)
                        s = _dot_nt(qm, kw) * scale + tab_ref[bi, variant, e]
                        pe = jnp.exp(s - lse_b[:, e * half:e * half + 1])
                        ds = pe * (_dot_nt(dom, vw) - dl_b[:, e * half:e * half + 1])
                        dtab_ref[bi, e] += ds
                        dsb = ds.astype(BF16)
                        dq_e = _dot_nn(dsb, jnp.where(msk, kw, 0))
                        dk_e = _dot_tn(dsb, qm)
                        dv_e = _dot_tn(pe.astype(BF16), dom)
                        dq_pair = dq_e if e == 0 else dq_pair + dq_e
                        dk_pair = dk_e if e == 0 else dk_pair + dk_e
                        dv_pair = dv_e if e == 0 else dv_pair + dv_e
                    dq_pair = dq_pair * scale
                    if bi > 0:
                        dq_pair = dq_pair + dq_old
                    results.append((dq_pair, dk_pair * scale, dv_pair))
                for (cur, prv, variant), (dq_pair, dk_pair, dv_pair) in zip(rows, results):
                    dq_ref[cur, :] = dq_pair
                    dkv_ref[0, cur, :] += dk_pair[P:, :]
                    dkv_ref[1, cur, :] += dv_pair[P:, :]
                    dkv_ref[0, prv, :] += dk_pair[:P, :]
                    dkv_ref[1, prv, :] += dv_pair[:P, :]
                return carry

            lax.fori_loop(0, nsteps, block, 0)
        dq_out[...] = dq_ref[...].astype(BF16)
        dkv_out[...] = dkv_ref[...].astype(BF16)

    nl = D // LANE
    col = lambda off: pl.BlockSpec((None, S, LANE), lambda p, b: (b, 0, off + p))
    tab_spec = pl.BlockSpec((3, 2, 2, P, 2 * P), lambda p, b: (0, 0, p, 0, 0))
    return pl.pallas_call(
        body, name=name, grid=(H // 2, B),
        in_specs=[col(0), col(0), col(nl), col(0), col(0), col(0), tab_spec],
        out_specs=[col(0), pl.BlockSpec((2, None, S, LANE), lambda p, b: (0, b, 0, p)),
                   pl.BlockSpec((None, 3, 2, P, 2 * P), lambda p, b: (p, 0, 0, 0, 0))],
        out_shape=[jax.ShapeDtypeStruct((B, S, D), BF16), jax.ShapeDtypeStruct((2, B, S, D), BF16),
                   jax.ShapeDtypeStruct((H // 2, 3, 2, P, 2 * P), F32)],
        scratch_shapes=[pltpu.VMEM((S, LANE), F32), pltpu.VMEM((S, LANE), F32), pltpu.VMEM((2, S, LANE), F32)],
        compiler_params=pltpu.CompilerParams(
            dimension_semantics=("parallel", "arbitrary"),
            vmem_limit_bytes=VMEM_CAP),
    )(q, kv, kv, do, o, lse, tabs)


def _mesh_place():
    x, y, c = lax.axis_index("x"), lax.axis_index("y"), lax.axis_index("c")
    return x, y, c, (x, y, 1 - c), [(1 - x, y), (x, 1 - y), (1 - x, 1 - y)]


def _rdma(src, dst, send_sems, recv_sems, idx, to):
    return pltpu.make_async_remote_copy(
        src_ref=src, dst_ref=dst, send_sem=send_sems.at[idx], recv_sem=recv_sems.at[idx],
        device_id=to, device_id_type=pl.DeviceIdType.MESH)


def _comm_call(body, name, args, out_shape, n_sems, n_local):
    any_spec = pl.BlockSpec(memory_space=pl.ANY)
    return pl.pallas_call(
        body, name=name, in_specs=[any_spec] * len(args), out_specs=[any_spec] * len(out_shape),
        out_shape=out_shape,
        scratch_shapes=[pltpu.SemaphoreType.DMA(n_sems), pltpu.SemaphoreType.DMA(n_sems),
                        pltpu.SemaphoreType.DMA((n_local,))],
        compiler_params=pltpu.CompilerParams(has_side_effects=True),
    )(*args)


def _all_gather(arrays, name):
    n = len(arrays)

    def body(*refs):
        ins, outs = refs[:n], refs[n:2 * n]
        send_sems, recv_sems, loc_sems = refs[2 * n:]
        x, y, c, sib, chips = _mesh_place()
        me = (x, y, c)

        def rows(a, dev):
            return outs[a].at[4 * dev[0] + 2 * dev[1] + dev[2]]

        def copy(a, k, block, to, src=None):
            return _rdma(rows(a, block) if src is None else src, rows(a, block), send_sems, recv_sems, (a, k), to)

        local, first, passed = [], [], []
        for a in range(n):
            cp = pltpu.make_async_copy(ins[a], rows(a, me), loc_sems.at[a])
            cp.start()
            local.append(cp)
            first.append(copy(a, 0, me, sib, src=ins[a]))
            first += [copy(a, 1 + j, me, (*chip, c), src=ins[a]) for j, chip in enumerate(chips)]
        for cp in first:
            cp.start()
        for j, chip in enumerate(chips):
            for a in range(n):
                copy(a, 1 + j, (*chip, c), me).wait_recv()
                cp = copy(a, 4 + j, (*chip, c), sib)
                cp.start()
                passed.append(cp)
        for a in range(n):
            copy(a, 0, sib, me).wait_recv()
            for j, chip in enumerate(chips):
                copy(a, 4 + j, (*chip, 1 - c), me).wait_recv()
        for cp in first + passed:
            cp.wait_send()
        for cp in local:
            cp.wait()

    out_shape = [jax.ShapeDtypeStruct((N_DEV,) + g.shape, g.dtype) for g in arrays]
    return list(_comm_call(body, name, arrays, out_shape, (n, N_DEV - 1), n))


def _core_pair_exchange(halves, gathered, name):
    nh, ng = len(halves), len(gathered)

    def body(*refs):
        h_in, g_in = refs[:nh], refs[nh:nh + ng]
        outs = refs[nh + ng:nh + ng + nh + ng]
        theirs, g_out = outs[:nh], outs[nh:]
        send_sems, recv_sems, loc_sems = refs[nh + ng + nh + ng:]
        x, y, c, sib, chips = _mesh_place()
        me = 4 * x + 2 * y + c
        peers = [sib] + [(*chip, pc) for chip in chips for pc in (c, 1 - c)]
        nchip = N_DEV // 2
        gbase = nh * nchip
        local, sends = [], []
        for a in range(nh):
            for ch in range(nchip):
                cp = _rdma(h_in[a].at[ch, 1 - c], theirs[a].at[ch], send_sems, recv_sems, a * nchip + ch, sib)
                cp.start()
                sends.append(cp)
        for g in range(ng):
            cp = pltpu.make_async_copy(g_in[g], g_out[g].at[me], loc_sems.at[g])
            cp.start()
            local.append(cp)
            for k, peer in enumerate(peers):
                cp = _rdma(g_in[g], g_out[g].at[me], send_sems, recv_sems, gbase + g * (N_DEV - 1) + k, peer)
                cp.start()
                sends.append(cp)
        for cp in sends:
            cp.wait_send()
        for a in range(nh):
            for ch in range(nchip):
                _rdma(h_in[a].at[ch, 1 - c], theirs[a].at[ch], send_sems, recv_sems, a * nchip + ch, sib).wait_recv()
        for g in range(ng):
            for k, peer in enumerate(peers):
                pid = 4 * peer[0] + 2 * peer[1] + peer[2]
                _rdma(g_in[g], g_out[g].at[pid], send_sems, recv_sems, gbase + g * (N_DEV - 1) + k, peer).wait_recv()
        for cp in local:
            cp.wait()

    half = [jax.ShapeDtypeStruct((h.shape[0],) + h.shape[2:], h.dtype) for h in halves]
    out_shape = half + [jax.ShapeDtypeStruct((N_DEV,) + g.shape, g.dtype) for g in gathered]
    outs = _comm_call(body, name, list(halves) + list(gathered), out_shape,
                      (nh * (N_DEV // 2) + ng * (N_DEV - 1),), max(ng, 1))
    return list(outs[:nh]), list(outs[nh:])


def _gather_copies(src_refs, land_refs, send_sems, recv_sems):
    x, y, c, sib, chips = _mesh_place()
    me = 4 * x + 2 * y + c
    peers = [sib] + [(*chip, c) for chip in chips]
    return [pltpu.make_async_remote_copy(
                src_ref=src, dst_ref=land.at[me], send_sem=send_sems[4 * a + k], recv_sem=recv_sems[4 * a + k],
                device_id=peer, device_id_type=pl.DeviceIdType.MESH)
            for a, (src, land) in enumerate(zip(src_refs, land_refs)) for k, peer in enumerate(peers)]


def _forward_to_sibling(gathered, name):
    n = len(gathered)

    def body(*refs):
        outs = refs[n:2 * n]
        send_sems, recv_sems, _ = refs[2 * n:]
        x, y, c, sib, chips = _mesh_place()
        sends = []
        for a in range(n):
            for j, chip in enumerate(chips):
                rows = outs[a].at[4 * chip[0] + 2 * chip[1] + c]
                cp = _rdma(rows, rows, send_sems, recv_sems, (a, j), sib)
                cp.start()
                sends.append(cp)
        for cp in sends:
            cp.wait_send()
        for a in range(n):
            for j, chip in enumerate(chips):
                rows = outs[a].at[4 * chip[0] + 2 * chip[1] + (1 - c)]
                _rdma(rows, rows, send_sems, recv_sems, (a, j), sib).wait_recv()

    any_spec = pl.BlockSpec(memory_space=pl.ANY)
    return list(pl.pallas_call(
        body, name=name, in_specs=[any_spec] * n, out_specs=[any_spec] * n,
        out_shape=[jax.ShapeDtypeStruct(g.shape, g.dtype) for g in gathered],
        input_output_aliases={i: i for i in range(n)},
        scratch_shapes=[pltpu.SemaphoreType.DMA((n, 3)), pltpu.SemaphoreType.DMA((n, 3)),
                        pltpu.SemaphoreType.DMA((1,))],
        compiler_params=pltpu.CompilerParams(has_side_effects=True),
    )(*gathered))


def _split_start(copies_fn, srcs, land_shapes, n_sems, name, after=None):
    ns, nl = len(srcs), len(land_shapes)
    nc = n_sems
    hbm = pl.BlockSpec(memory_space=pltpu.HBM)
    sem = pl.BlockSpec(memory_space=pltpu.SEMAPHORE)

    n_in = ns + nl + (0 if after is None else 1)

    def body(*refs):
        src_refs, land_refs = refs[:ns], refs[ns:ns + nl]
        send_sems, recv_sems = refs[n_in:n_in + nc], refs[n_in + nc:n_in + 2 * nc]
        token = refs[-1]
        for cp in copies_fn(src_refs, land_refs, send_sems, recv_sems):
            cp.start()
        token[...] = jnp.zeros_like(token)

    lands = [lax.empty(s.shape, s.dtype) for s in land_shapes]
    thru = [pltpu.HBM(t.shape, t.dtype) for t in list(srcs) + lands]
    order = [] if after is None else [after]
    outs = pl.pallas_call(
        body, name=name,
        out_shape=(*([pltpu.SemaphoreType.DMA(())] * (2 * nc)), *thru, jax.ShapeDtypeStruct((8, LANE), F32)),
        in_specs=[hbm] * (ns + nl) + [pl.BlockSpec(memory_space=pl.ANY)] * len(order),
        out_specs=(*([sem] * (2 * nc)), *([hbm] * (ns + nl)), pl.BlockSpec(memory_space=pltpu.VMEM)),
        input_output_aliases={i: 2 * nc + i for i in range(ns + nl)},
        compiler_params=pltpu.CompilerParams(has_side_effects=pltpu.SideEffectType.DATAFLOW_SIDE_EFFECTING),
    )(*[pltpu.with_memory_space_constraint(t, pltpu.HBM) for t in list(srcs) + lands], *order)
    thru_out = outs[2 * nc:2 * nc + ns + nl]
    return list(outs[:nc]), list(outs[nc:2 * nc]), list(thru_out[:ns]), list(thru_out[ns:]), outs[-1]


def _split_wait(copies_fn, send_sems, recv_sems, srcs, lands, after, name):
    ns, nl, nc = len(srcs), len(lands), len(send_sems)
    hbm = pl.BlockSpec(memory_space=pltpu.HBM)
    sem = pl.BlockSpec(memory_space=pltpu.SEMAPHORE)

    def body(*refs):
        src_refs, land_refs = refs[:ns], refs[ns:ns + nl]
        send_refs, recv_refs = refs[ns + nl:ns + nl + nc], refs[ns + nl + nc:ns + nl + 2 * nc]
        for cp in copies_fn(src_refs, land_refs, send_refs, recv_refs):
            cp.wait_send()
            cp.wait_recv()

    outs = pl.pallas_call(
        body, name=name,
        out_shape=[pltpu.HBM(t.shape, t.dtype) for t in list(srcs) + list(lands)],
        in_specs=[hbm] * (ns + nl) + [sem] * (2 * nc) + [pl.BlockSpec(memory_space=pl.ANY)],
        out_specs=[hbm] * (ns + nl),
        input_output_aliases={i: i for i in range(ns + nl)},
        compiler_params=pltpu.CompilerParams(has_side_effects=pltpu.SideEffectType.DATAFLOW_SIDE_EFFECTING),
    )(*srcs, *lands, *send_sems, *recv_sems, after)
    return list(outs[:ns]), list(outs[ns:])


def _pair_copies(src_refs, land_refs, send_sems, recv_sems):
    x, y, c, sib, chips = _mesh_place()
    nchip = N_DEV // 2
    return [pltpu.make_async_remote_copy(
                src_ref=src.at[ch, 1 - c], dst_ref=land.at[ch], send_sem=send_sems[nchip * a + ch],
                recv_sem=recv_sems[nchip * a + ch], device_id=sib, device_id_type=pl.DeviceIdType.MESH)
            for a, (src, land) in enumerate(zip(src_refs, land_refs)) for ch in range(nchip)]


def _chip_copies(src_refs, land_refs, send_sems, recv_sems):
    x, y, c, sib, chips = _mesh_place()
    return [pltpu.make_async_remote_copy(
                src_ref=src.at[2 * chip[0] + chip[1]], dst_ref=land.at[j], send_sem=send_sems[3 * a + j],
                recv_sem=recv_sems[3 * a + j], device_id=(*chip, c), device_id_type=pl.DeviceIdType.MESH)
            for a, (src, land) in enumerate(zip(src_refs, land_refs)) for j, chip in enumerate(chips)]


def _pair_sum(halves, theirs, core, name, tr=256):
    nchip, _, R, C = halves.shape
    tr = _row_tile(R, tr)

    def body(core_ref, h_ref, t_ref, o_ref):
        o_ref[...] = (h_ref[...] + t_ref[...]).astype(BF16)

    grid_spec = pltpu.PrefetchScalarGridSpec(
        num_scalar_prefetch=1, grid=(nchip, R // tr),
        in_specs=[pl.BlockSpec((None, None, tr, C), lambda ch, i, core_ref: (ch, core_ref[0], i, 0)),
                  pl.BlockSpec((None, tr, C), lambda ch, i, core_ref: (ch, i, 0))],
        out_specs=pl.BlockSpec((None, tr, C), lambda ch, i, core_ref: (ch, i, 0)))
    return pl.pallas_call(
        body, name=name, grid_spec=grid_spec, out_shape=jax.ShapeDtypeStruct((nchip, R, C), BF16),
        compiler_params=pltpu.CompilerParams(
            dimension_semantics=("parallel", "parallel"), vmem_limit_bytes=_vmem(3 * _nbytes((tr, C), F32))),
    )(core, *[pltpu.with_memory_space_constraint(t, pltpu.HBM) for t in (halves, theirs)])


def _chip_sum(own, recv, chip, name, tr=256):
    _, R, C = own.shape
    tr = _row_tile(R, tr)

    def body(chip_ref, o_ref, r_ref, out_ref):
        acc = o_ref[...].astype(F32)
        for j in range(3):
            acc = acc + r_ref[j].astype(F32)
        out_ref[...] = acc

    grid_spec = pltpu.PrefetchScalarGridSpec(
        num_scalar_prefetch=1, grid=(R // tr,),
        in_specs=[pl.BlockSpec((None, tr, C), lambda i, chip_ref: (chip_ref[0], i, 0)),
                  pl.BlockSpec((3, tr, C), lambda i, chip_ref: (0, i, 0))],
        out_specs=pl.BlockSpec((tr, C), lambda i, chip_ref: (i, 0)))
    return pl.pallas_call(
        body, name=name, grid_spec=grid_spec, out_shape=jax.ShapeDtypeStruct((R, C), F32),
        compiler_params=pltpu.CompilerParams(
            dimension_semantics=("parallel",), vmem_limit_bytes=_vmem(4 * _nbytes((tr, C), F32))),
    )(chip, *[pltpu.with_memory_space_constraint(t, pltpu.HBM) for t in (own, recv)])


def _row_tile(rows, target):
    best = rows
    for t in range(8, min(rows, target) + 1, 8):
        if rows % t == 0:
            best = t
    return best


def _sum_slots(recv, name, tr=128):
    n, R, C = recv.shape
    tr = _row_tile(R, tr if recv.dtype == F32 else 2 * tr)

    def body(r_ref, o_ref):
        acc = r_ref[0].astype(F32)
        for k in range(1, n):
            acc = acc + r_ref[k].astype(F32)
        o_ref[...] = acc

    return pl.pallas_call(
        body, name=name, grid=(R // tr,),
        in_specs=[pl.BlockSpec((n, tr, C), lambda i: (0, i, 0))],
        out_specs=pl.BlockSpec((tr, C), lambda i: (i, 0)),
        out_shape=jax.ShapeDtypeStruct((R, C), F32),
        compiler_params=pltpu.CompilerParams(
            dimension_semantics=("parallel",), vmem_limit_bytes=_vmem(10 * _nbytes((tr, C), F32))),
    )(recv)


def _adamw_math(g_ref, w_ref, m_ref, v_ref, d_ref, nm_ref, nv_ref):
    gv = g_ref[...]
    nm = ADAM_B1 * m_ref[...] + (1.0 - ADAM_B1) * gv
    nv = ADAM_B2 * v_ref[...] + (1.0 - ADAM_B2) * (gv * gv)
    m_hat = nm / (1.0 - ADAM_B1 ** ADAM_STEP)
    v_hat = nv / (1.0 - ADAM_B2 ** ADAM_STEP)
    d_ref[...] = -ADAM_LR * (m_hat / (jnp.sqrt(v_hat) + ADAM_EPS) + ADAM_WD * w_ref[...])
    nm_ref[...] = nm
    nv_ref[...] = nv


def _chip_sum_adamw(own, recv, chip, w, m, v, name, tr=256):
    _, R, C = own.shape
    tr = _row_tile(R, tr)

    def body(chip_ref, o_ref, r_ref, w_ref, m_ref, v_ref, g_ref, d_ref, nm_ref, nv_ref):
        acc = o_ref[...].astype(F32)
        for j in range(3):
            acc = acc + r_ref[j].astype(F32)
        g_ref[...] = acc
        _adamw_math(g_ref, w_ref, m_ref, v_ref, d_ref, nm_ref, nv_ref)

    row = pl.BlockSpec((tr, C), lambda i, chip_ref: (i, 0))
    grid_spec = pltpu.PrefetchScalarGridSpec(
        num_scalar_prefetch=1, grid=(R // tr,),
        in_specs=[pl.BlockSpec((None, tr, C), lambda i, chip_ref: (chip_ref[0], i, 0)),
                  pl.BlockSpec((3, tr, C), lambda i, chip_ref: (0, i, 0)), row, row, row],
        out_specs=[row] * 4)
    return pl.pallas_call(
        body, name=name, grid_spec=grid_spec, out_shape=[jax.ShapeDtypeStruct((R, C), F32)] * 4,
        compiler_params=pltpu.CompilerParams(
            dimension_semantics=("parallel",), vmem_limit_bytes=_vmem(10 * _nbytes((tr, C), F32))),
    )(chip, own, recv, w, m, v)


def _adamw_small(params, name):
    n = len(params)
    two_d = [(-1, w.shape[-1]) for w, _, _, _ in params]
    flat = [t.reshape(two_d[i]) for i, (w, m, v, g) in enumerate(params) for t in (g, w, m, v)]

    def body(*refs):
        ins, outs = refs[:4 * n], refs[4 * n:]
        for i in range(n):
            _adamw_math(*ins[4 * i:4 * i + 4], *outs[3 * i:3 * i + 3])

    outs = pl.pallas_call(
        body, name=name,
        out_shape=[jax.ShapeDtypeStruct(flat[4 * i].shape, F32) for i in range(n) for _ in range(3)],
    )(*flat)
    return [(g.reshape(w.shape),) + tuple(outs[3 * i + t].reshape(w.shape) for t in range(3))
            for i, (w, m, v, g) in enumerate(params)]


def _adamw(g, w, m, v, name, tr=256):
    R, C = g.shape
    tr = _row_tile(R, tr)
    body = functools.partial(_adamw_math)

    row = pl.BlockSpec((tr, C), lambda i: (i, 0))
    return pl.pallas_call(
        body, name=name, grid=(R // tr,), in_specs=[row] * 4, out_specs=[row] * 3,
        out_shape=[jax.ShapeDtypeStruct((R, C), F32)] * 3,
        compiler_params=pltpu.CompilerParams(
            dimension_semantics=("parallel",), vmem_limit_bytes=_vmem(8 * _nbytes((tr, C), F32))),
    )(*[pltpu.with_memory_space_constraint(t, pltpu.HBM) for t in (g, w, m, v)])


def _pack_rows(parts):
    flat = jnp.concatenate([p.reshape(-1).astype(F32) for p in parts])
    rows = -(-flat.shape[0] // (8 * LANE)) * 8
    return jnp.pad(flat, (0, rows * LANE - flat.shape[0])).reshape(rows, LANE)


def _unpack(flat, shapes):
    out, off = [], 0
    for shp in shapes:
        n = math.prod(shp)
        out.append(flat[off:off + n].reshape(shp))
        off += n
    return out


def kernel(x, a_norm, a_w_in, a_conv, a_w_out, kv_norm, w_kv, b_norm, b_w_q, b_w_o, rel_bias, ffn_norm, ffn_w_up, ffn_conv, ffn_conv_b, ffn_w_down, final_norm, loss_target, m_a_norm, m_a_w_in, m_a_conv, m_a_w_out, m_kv_norm, m_w_kv, m_b_norm, m_b_w_q, m_b_w_o, m_rel_bias, m_ffn_norm, m_ffn_w_up, m_ffn_conv, m_ffn_conv_b, m_ffn_w_down, m_final_norm, v_a_norm, v_a_w_in, v_a_conv, v_a_w_out, v_kv_norm, v_w_kv, v_b_norm, v_b_w_q, v_b_w_o, v_rel_bias, v_ffn_norm, v_ffn_w_up, v_ffn_conv, v_ffn_conv_b, v_ffn_w_down, v_final_norm):
    B, S, D = x.shape
    T = B * S
    F = ffn_w_down.shape[1] * N_DEV
    me = 4 * lax.axis_index("x") + 2 * lax.axis_index("y") + lax.axis_index("c")

    big_shards = [a_w_in[0].T, a_w_out[0], w_kv.T, b_w_q[0], b_w_o[0],
                  ffn_w_up[0].T, ffn_w_up[1].T, ffn_w_down[0], ffn_w_down[1]]
    small_shapes = [a_norm.shape, a_conv.shape, ffn_conv.shape]
    small_pack = _pack_rows([a_norm, a_conv, ffn_conv])
    shards = [s.astype(BF16) for s in big_shards]
    first = _all_gather([shards[0], shards[1], small_pack], "gather_weights_first")
    win_t, wout = first[0].reshape(-1, D), first[1].reshape(-1, D)
    smalls = [_unpack(first[2][j].reshape(-1), small_shapes) for j in range(N_DEV)]
    a_norm_f = jnp.concatenate([s[0] for s in smalls], axis=-1)
    a_conv_f = jnp.concatenate([s[1] for s in smalls], axis=-1)[0]
    ffn_conv_f = jnp.concatenate([s[2] for s in smalls], axis=-1)

    def gather_start(tag, idxs, after):
        srcs = [shards[i] for i in idxs]
        lands = [jax.ShapeDtypeStruct((N_DEV,) + s.shape, s.dtype) for s in srcs]
        return _split_start(_gather_copies, srcs, lands, 4 * len(srcs), f"gather_start_{tag}", after=after)

    def gather_finish(tag, handle, after):
        send, recv, srcs, lands, _ = handle
        srcs, lands = _split_wait(_gather_copies, send, recv, srcs, lands, after, f"gather_wait_{tag}")
        lands = _forward_to_sibling(lands, f"gather_forward_{tag}")
        return [lax.dynamic_update_slice(g, s[None], (me, 0, 0)).reshape(-1, D) for g, s in zip(lands, srcs)]

    ffn0_w = gather_start("ffn0", [5, 7], first[0])
    rest_w = gather_start("rest", [2, 3, 4, 6, 8], ffn0_w[4])

    x2 = x.reshape(T, D)
    (xn0,) = _rmsnorm_fwd(x2, a_norm_f, "a_norm_fwd")
    bch = _matmul(xn0, win_t, mode="nt", out_dtype=BF16, after=rest_w[4], name="a_in_proj", tm=2048).reshape(B, S, 3 * D)
    gated = _shortconv_fwd(bch, a_conv_f, "a_gate_fwd").reshape(T, D)
    h1 = _matmul(gated, wout, mode="nn", out_dtype=F32, add=x2, name="a_out_proj")
    wup0_t, wdn0 = gather_finish("ffn0", ffn0_w, h1)
    wup_t, wdn = [wup0_t, None], [wdn0, None]

    def ffn_fwd(h, l):
        (xn,) = _rmsnorm_fwd(h, ffn_norm[l:l + 1], f"ffn{l}_norm_fwd")
        u0 = _matmul(xn, wup_t[l], mode="nt", out_dtype=BF16, name=f"ffn{l}_up", tm=2048, tn=1408).reshape(B, S, 2 * F)
        act = _ffn_gate_fwd(u0, ffn_conv_f[l], ffn_conv_b[l:l + 1], f"ffn{l}_gate_fwd").reshape(T, F)
        out = _matmul(act, wdn[l], mode="nn", out_dtype=F32, add=h, name=f"ffn{l}_down", tk=2816)
        return out, (xn, u0, act)

    h2, ffn0_saved = ffn_fwd(h1, 0)
    wkv_t, wq, wo, wup_t[1], wdn[1] = gather_finish("rest", rest_w, h2)
    kvn, xnb = _rmsnorm_fwd(h2, jnp.stack([kv_norm, b_norm[0]]), "kv_b_norm_fwd")
    kv = _matmul(kvn, wkv_t, mode="nt", out_dtype=F32, name="kv_proj", tm=2048).reshape(B, S, 2 * D)
    q = _matmul(xnb, wq, mode="nn", out_dtype=F32, name="q_proj", tm=2048).reshape(B, S, D)

    tables = [_band_tables(w, d) for (w, d) in DILATED_BRANCHES]
    onehots = jnp.stack([t[0] for t in tables])
    P = ATT_BLOCK
    bias_vals = _bias_lookup(rel_bias.T, onehots, "rel_bias_lookup").reshape(3, N_HEADS, P, 2 * P)
    in_cur = (jnp.arange(2 * P) >= P)[None, :]
    masks = jnp.stack([jnp.stack([t[1] & in_cur, t[1]]) for t in tables])
    tabs = jnp.where(masks[:, :, None], bias_vals[:, None], NEG)

    attn3, lse3 = _attn_fwd(q, kv, tabs, "attn_fwd")
    attn = attn3.reshape(T, D)
    h3 = _matmul(attn, wo, mode="nn", out_dtype=F32, add=h2, name="o_proj")
    h4, ffn1_saved = ffn_fwd(h3, 1)

    dh4, dh4b, d_final_norm, sq = _final_loss_bwd(h4, final_norm.reshape(1, D), loss_target.reshape(T, D), "loss_bwd")
    loss_part = 0.5 * jnp.sum(sq) / D

    core_idx = lax.axis_index("c").astype(jnp.int32).reshape(1)
    chip_idx = (2 * lax.axis_index("x") + lax.axis_index("y")).astype(jnp.int32).reshape(1)
    in_flight = []

    def group_start(tag, indexed_grads):
        halves = [g.reshape(N_DEV // 2, 2, g.shape[0] // N_DEV, D) for _, g in indexed_grads]
        lands = [jax.ShapeDtypeStruct((h.shape[0],) + h.shape[2:], h.dtype) for h in halves]
        send, recv, srcs, lnd, token = _split_start(_pair_copies, halves, lands, (N_DEV // 2) * len(halves),
                                                    f"exchange_core_start_{tag}")
        return (tag, [i for i, _ in indexed_grads], send, recv, srcs, lnd), token

    def group_finish(handle, after):
        tag, idxs, send, recv, srcs, lnd = handle
        halves, theirs = _split_wait(_pair_copies, send, recv, srcs, lnd, after, f"exchange_core_wait_{tag}")
        sums = [_pair_sum(h_, t_, core_idx, f"grad_core_sum_{tag}{k}")
                for k, (h_, t_) in enumerate(zip(halves, theirs))]
        lands = [jax.ShapeDtypeStruct((3,) + s.shape[1:], s.dtype) for s in sums]
        send, recv, srcs, lnd, token = _split_start(_chip_copies, sums, lands, 3 * len(sums),
                                                    f"exchange_chip_start_{tag}")
        in_flight.append((tag, idxs, send, recv, srcs, lnd))
        return token

    def ffn_bwd(dh_out, dh_out_b, h_in, saved, l, after):
        xn, u0, act = saved
        dact = _matmul(dh_out_b, wdn[l], mode="nt", out_dtype=BF16, after=after, name=f"ffn{l}_down_dx",
                       tm=2048, tn=1408).reshape(B, S, F)
        d_wdn = _matmul(act, dh_out_b, mode="tn", out_dtype=F32, name=f"ffn{l}_down_dw", tm=1408, tn=1024, tk=2048)
        du0, d_conv, d_conv_b = _ffn_gate_bwd(u0, dact, ffn_conv_f[l], ffn_conv_b[l:l + 1], f"ffn{l}_gate_bwd")
        du0 = du0.reshape(2, T, F)
        dxn = _matmul(du0, wup_t[l], mode="nn", out_dtype=BF16, a_parts=2, name=f"ffn{l}_up_dx", tk=2816)
        d_wup_t = _matmul(du0, xn, mode="tn", out_dtype=F32, a_parts=2, name=f"ffn{l}_up_dw", tm=1408, tn=1024, tk=2048)
        dh_in, dh_in_b, d_norm = _rmsnorm_bwd(h_in, ffn_norm[l:l + 1], [dxn], dh_out, f"ffn{l}_norm_bwd")
        return (dh_in, dh_in_b), (d_wdn, d_wup_t, d_conv, d_conv_b, d_norm)

    (dh3, dh3b), ffn1_grads = ffn_bwd(dh4, dh4b, h3, ffn1_saved, 1, None)
    group, token = group_start("ffn1", [(6, ffn1_grads[1]), (8, ffn1_grads[0])])

    dattn = _matmul(dh3b, wo, mode="nt", out_dtype=F32, after=token, name="o_proj_dx", tm=2048)
    token = group_finish(group, dattn)
    dattn = dattn.reshape(B, S, D)
    d_wo = _matmul(attn, dh3b, mode="tn", out_dtype=F32, after=token, name="o_proj_dw", tm=1024, tn=1024, tk=2048)
    dq, dkv, dtab = _attn_bwd(q, kv, dattn, attn3, lse3, tabs, "attn_bwd")
    dtabs = dtab.transpose(1, 0, 2, 3, 4).reshape(3, N_HEADS, P * 2 * P)
    d_rel_bias = _bias_grad(dtabs, onehots, "rel_bias_grad").T
    dq2 = dq.reshape(T, D)
    dkv2 = dkv.reshape(2, T, D)
    d_wq = _matmul(xnb, dq2, mode="tn", out_dtype=F32, name="q_proj_dw", tm=1024, tn=1024, tk=2048)
    d_wkv_t = _matmul(dkv2, kvn, mode="tn", out_dtype=F32, a_parts=2, name="kv_proj_dw", tm=1024, tn=1024, tk=2048)
    group, token = group_start("attn", [(2, d_wkv_t), (3, d_wq), (4, d_wo)])
    dxnb = _matmul(dq2, wq, mode="nt", out_dtype=BF16, after=token, name="q_proj_dx")
    dkvn = _matmul(dkv2, wkv_t, mode="nn", out_dtype=BF16, a_parts=2, name="kv_proj_dx")
    dh2, dh2b, d_kvb_norm = _rmsnorm_bwd(h2, jnp.stack([kv_norm, b_norm[0]]), [dkvn, dxnb], dh3, "kv_b_norm_bwd")
    token = group_finish(group, dh2b)

    (dh1, dh1b), ffn0_grads = ffn_bwd(dh2, dh2b, h1, ffn0_saved, 0, token)
    group, token = group_start("ffn0", [(5, ffn0_grads[1]), (7, ffn0_grads[0])])

    dgated = _matmul(dh1b, wout, mode="nt", out_dtype=BF16, after=token, name="a_out_proj_dx", tm=2048)
    token = group_finish(group, dgated)
    dgated = dgated.reshape(B, S, D)
    d_wout = _matmul(gated, dh1b, mode="tn", out_dtype=F32, after=token, name="a_out_proj_dw",
                     tm=1024, tn=1024, tk=2048)
    dbch, d_a_conv = _shortconv_bwd(bch, dgated, a_conv_f, "a_gate_bwd")
    dbch = dbch.reshape(3, T, D)
    dxn0 = _matmul(dbch, win_t, mode="nn", out_dtype=BF16, a_parts=3, name="a_in_proj_dx", tm=2048)
    d_win_t = _matmul(dbch, xn0, mode="tn", out_dtype=F32, a_parts=3, name="a_in_proj_dw", tm=1024, tn=1024, tk=2048)
    group, token = group_start("a", [(0, d_win_t), (1, d_wout)])
    grad_x, _, d_a_norm = _rmsnorm_bwd(x2, a_norm_f + token[0, 0], [dxn0], dh1, "a_norm_bwd")
    token = group_finish(group, grad_x)

    small_full = [d_a_norm, d_a_conv, jnp.stack([ffn0_grads[2], ffn1_grads[2]]),
                  d_kvb_norm[0], d_kvb_norm[1], d_rel_bias, jnp.concatenate([ffn0_grads[4], ffn1_grads[4]]),
                  jnp.concatenate([ffn0_grads[3], ffn1_grads[3]]), d_final_norm, loss_part]
    small_full_shapes = [(1, D), (3, D), (2, 3, 2 * F), (D,), (1, D), rel_bias.shape, (2, D), (2, 2 * F), (D,), ()]
    _, gath = _core_pair_exchange([], [_pack_rows(small_full)], "exchange_small")
    chip_sums, recv = {}, {}
    for tag, idxs, send, rcv, srcs, lnd in in_flight:
        srcs, lnd = _split_wait(_chip_copies, send, rcv, srcs, lnd, token, f"exchange_chip_wait_{tag}")
        for i, s_, l_ in zip(idxs, srcs, lnd):
            chip_sums[i], recv[i] = s_, l_
    small_sum = _sum_slots(gath[0], "small_grad_sum").reshape(-1)
    (g_a_norm_f, g_a_conv_f, g_ffn_conv_f, g_kv_norm, g_b_norm, g_rel_bias, g_ffn_norm, g_ffn_conv_b,
     g_final_norm, loss) = _unpack(small_sum, small_full_shapes)

    def my_cols(full, width):
        return lax.dynamic_slice_in_dim(full, me * width, width, axis=full.ndim - 1)

    g_a_norm = my_cols(g_a_norm_f, D // N_DEV)
    g_a_conv = my_cols(g_a_conv_f, D // N_DEV)[None]
    g_ffn_conv = my_cols(g_ffn_conv_f, 2 * F // N_DEV)

    big_w = [(a_w_in, m_a_w_in, v_a_w_in, True), (a_w_out, m_a_w_out, v_a_w_out, False),
             (w_kv, m_w_kv, v_w_kv, True), (b_w_q, m_b_w_q, v_b_w_q, False), (b_w_o, m_b_w_o, v_b_w_o, False),
             (ffn_w_up[0], m_ffn_w_up[0], v_ffn_w_up[0], True), (ffn_w_up[1], m_ffn_w_up[1], v_ffn_w_up[1], True),
             (ffn_w_down[0], m_ffn_w_down[0], v_ffn_w_down[0], False),
             (ffn_w_down[1], m_ffn_w_down[1], v_ffn_w_down[1], False)]
    big_out = []
    for i, (w, m, v, transposed) in enumerate(big_w):
        if transposed:
            g = _chip_sum(chip_sums[i], recv[i], chip_idx, f"grad_sum_{i}").T
            w2, m2, v2 = (t.reshape(g.shape) for t in (w, m, v))
            delta, nm, nv = _adamw(g, w2, m2, v2, f"adamw_{i}")
        else:
            w2, m2, v2 = (t.reshape(chip_sums[i].shape[1:]) for t in (w, m, v))
            g, delta, nm, nv = _chip_sum_adamw(chip_sums[i], recv[i], chip_idx, w2, m2, v2, f"grad_sum_adamw_{i}")
        big_out.append(tuple(t.reshape(w.shape) for t in (g, delta, nm, nv)))

    def pair(i, j):
        return tuple(jnp.stack([big_out[i][t], big_out[j][t]]) for t in range(4))

    small_w = [(a_norm, m_a_norm, v_a_norm, g_a_norm), (a_conv, m_a_conv, v_a_conv, g_a_conv),
               (ffn_conv, m_ffn_conv, v_ffn_conv, g_ffn_conv), (kv_norm, m_kv_norm, v_kv_norm, g_kv_norm),
               (b_norm, m_b_norm, v_b_norm, g_b_norm), (rel_bias, m_rel_bias, v_rel_bias, g_rel_bias),
               (ffn_norm, m_ffn_norm, v_ffn_norm, g_ffn_norm),
               (ffn_conv_b, m_ffn_conv_b, v_ffn_conv_b, g_ffn_conv_b),
               (final_norm, m_final_norm, v_final_norm, g_final_norm)]
    small_out = _adamw_small(small_w, "adamw_small")

    per_weight = [small_out[0], big_out[0], small_out[1], big_out[1], small_out[3], big_out[2], small_out[4],
                  big_out[3], big_out[4], small_out[5], small_out[6], pair(5, 6), small_out[2], small_out[7],
                  pair(7, 8), small_out[8]]
    outs = [loss, grad_x.reshape(B, S, D)]
    for t in range(4):
        outs.extend(pw[t] for pw in per_weight)
    return tuple(outs)
```

```python
import functools
import math

import jax
import jax.numpy as jnp
from jax import lax
from jax.experimental import pallas as pl
from jax.experimental.pallas import tpu as pltpu

F32 = jnp.float32
BF16 = jnp.bfloat16

N_DEV = 8
N_HEADS = 16
ATT_BLOCK = 128
ATT_UNROLL = 8
DILATED_BRANCHES = ((128, 1), (512, 4), (2048, 16))
REL_BUCKETS = 32
REL_MAX_DISTANCE = 2048
RMS_EPS = 1e-6
ADAM_LR = 0.001
ADAM_B1 = 0.9
ADAM_B2 = 0.999
ADAM_EPS = 1e-08
ADAM_WD = 0.01
ADAM_STEP = 10

LANE = 128
HALO = 16
NEG = -1e30
VMEM_CAP = 56 << 20


def _vmem(block_bytes):
    return int(min(VMEM_CAP, max(32 << 20, 3 * block_bytes + (8 << 20))))


def _nbytes(shape, dtype):
    return math.prod(shape) * jnp.dtype(dtype).itemsize


def _tile(dim, target):
    best = None
    t = LANE
    while t <= min(dim, target):
        if dim % t == 0:
            best = t
        t += LANE
    return best if best is not None else dim


def _matmul(a, b, *, mode, out_dtype, name, add=None, after=None, a_parts=1, tm=1024, tn=1024, tk=1024):
    P = a_parts
    if mode == "nn":
        M, K = (a.shape[0], a.shape[1]) if P == 1 else (a.shape[1], a.shape[2] * P)
        N = b.shape[1]
    elif mode == "nt":
        assert P == 1
        M, K = a.shape
        N = b.shape[0]
    else:
        K = a.shape[0] if P == 1 else a.shape[1]
        M = a.shape[1] if P == 1 else a.shape[2] * P
        N = b.shape[1]
    tm = _tile(M // P if mode == "tn" else M, tm)
    tn = _tile(N, tn)
    tk = _tile(K // P if mode == "nn" else K, tk)
    nm, nn_, nk = M // tm, N // tn, K // tk

    if mode == "nn":
        if P == 1:
            a_spec = pl.BlockSpec((tm, tk), lambda i, j, k: (i, k))
        else:
            nkp = nk // P
            a_spec = pl.BlockSpec((None, tm, tk), lambda i, j, k: (k // nkp, i, k % nkp))
        b_spec = pl.BlockSpec((tk, tn), lambda i, j, k: (k, j))
        dims = (((1,), (0,)), ((), ()))
    elif mode == "nt":
        a_spec = pl.BlockSpec((tm, tk), lambda i, j, k: (i, k))
        b_spec = pl.BlockSpec((tn, tk), lambda i, j, k: (j, k))
        dims = (((1,), (1,)), ((), ()))
    else:
        if P == 1:
            a_spec = pl.BlockSpec((tk, tm), lambda i, j, k: (k, i))
        else:
            nmp = nm // P
            a_spec = pl.BlockSpec((None, tk, tm), lambda i, j, k: (i // nmp, k, i % nmp))
        b_spec = pl.BlockSpec((tk, tn), lambda i, j, k: (k, j))
        dims = (((0,), (0,)), ((), ()))
    o_spec = pl.BlockSpec((tm, tn), lambda i, j, k: (i, j))
    in_specs = [a_spec, b_spec]
    args = [a, b]
    if add is not None:
        in_specs.append(o_spec)
        args.append(add)
    has_add = add is not None
    if after is not None:
        in_specs.append(pl.BlockSpec(memory_space=pl.ANY))
        args.append(after)
    n_extra = len(args) - 2

    direct = nk > 1 and out_dtype == F32 and not has_add

    def body(a_ref, b_ref, *rest):
        add_ref = rest[0] if has_add else None
        o_ref = rest[n_extra]
        acc_ref = o_ref if direct or nk == 1 else rest[-1]
        k = pl.program_id(2)

        def part():
            return lax.dot_general(a_ref[...].astype(BF16), b_ref[...].astype(BF16), dims,
                                   preferred_element_type=F32)

        def finish(r):
            if has_add:
                r = r + add_ref[...].astype(F32)
            o_ref[...] = r.astype(out_dtype)

        if nk == 1:
            finish(part())
        else:
            @pl.when(k == 0)
            def _():
                acc_ref[...] = part()

            @pl.when(jnp.logical_and(k > 0, jnp.logical_or(k < nk - 1, direct)))
            def _():
                acc_ref[...] += part()

            if not direct:
                @pl.when(k == nk - 1)
                def _():
                    finish(acc_ref[...] + part())

    blk = (_nbytes((tm, tk), a.dtype) + _nbytes((tk, tn), b.dtype) + _nbytes((tm, tn), out_dtype)
           + (_nbytes((tm, tn), add.dtype) if has_add else 0)) * 2 + 3 * _nbytes((tm, tn), F32)
    return pl.pallas_call(
        body, name=name, grid=(nm, nn_, nk),
        in_specs=in_specs, out_specs=o_spec,
        out_shape=jax.ShapeDtypeStruct((M, N), out_dtype),
        scratch_shapes=[] if direct or nk == 1 else [pltpu.VMEM((tm, tn), F32)],
        compiler_params=pltpu.CompilerParams(
            dimension_semantics=("parallel", "parallel", "arbitrary"), vmem_limit_bytes=_vmem(blk)),
    )(*args)


def _rmsnorm_fwd(x, gains, name, tm=1024):
    T, D = x.shape
    n = gains.shape[0]
    tm = _tile(T, tm)

    def body(x_ref, g_ref, *o_refs):
        xv = x_ref[...]
        xhat = xv * lax.rsqrt(jnp.mean(xv * xv, axis=-1, keepdims=True) + RMS_EPS)
        for i in range(n):
            o_refs[i][...] = (xhat * g_ref[i:i + 1, :]).astype(BF16)

    row = pl.BlockSpec((tm, D), lambda i: (i, 0))
    outs = pl.pallas_call(
        body, name=name, grid=(T // tm,),
        in_specs=[row, pl.BlockSpec((n, D), lambda i: (0, 0))],
        out_specs=[row] * n,
        out_shape=[jax.ShapeDtypeStruct((T, D), BF16)] * n,
        compiler_params=pltpu.CompilerParams(
            dimension_semantics=("parallel",), vmem_limit_bytes=_vmem(4 * _nbytes((tm, D), F32))),
    )(x, gains)
    return tuple(outs)


def _rmsnorm_bwd(x, gains, dxns, dres, name, tm=1024):
    T, D = x.shape
    n = gains.shape[0]
    tm = _tile(T, tm)

    def body(x_ref, g_ref, *rest):
        dxn_refs = rest[:n]
        dres_ref, dx_ref, dxb_ref, dg_ref = rest[n:]
        xv = x_ref[...]
        rstd = lax.rsqrt(jnp.mean(xv * xv, axis=-1, keepdims=True) + RMS_EPS)
        xhat = xv * rstd
        dx = dres_ref[...]

        @pl.when(pl.program_id(0) == 0)
        def _():
            dg_ref[...] = jnp.zeros_like(dg_ref)

        for i in range(n):
            dy = dxn_refs[i][...].astype(F32)
            dg_ref[i:i + 1, :] += jnp.sum(dy * xhat, axis=0, keepdims=True)
            dxh = dy * g_ref[i:i + 1, :]
            dx = dx + rstd * (dxh - xhat * jnp.mean(dxh * xhat, axis=-1, keepdims=True))
        dx_ref[...] = dx
        dxb_ref[...] = dx.astype(BF16)

    row = pl.BlockSpec((tm, D), lambda i: (i, 0))
    par = pl.BlockSpec((n, D), lambda i: (0, 0))
    return pl.pallas_call(
        body, name=name, grid=(T // tm,),
        in_specs=[row, par] + [row] * n + [row],
        out_specs=[row, row, par],
        out_shape=[jax.ShapeDtypeStruct((T, D), F32), jax.ShapeDtypeStruct((T, D), BF16),
                   jax.ShapeDtypeStruct((n, D), F32)],
        compiler_params=pltpu.CompilerParams(
            dimension_semantics=("arbitrary",), vmem_limit_bytes=_vmem((4 + n) * _nbytes((tm, D), F32))),
    )(x, gains, *dxns, dres)


def _final_loss_bwd(h, gain, target, name, tm=1024):
    T, D = h.shape
    tm = _tile(T, tm)

    def body(h_ref, g_ref, t_ref, dh_ref, dhb_ref, dg_ref, sq_ref):
        xv = h_ref[...]
        rstd = lax.rsqrt(jnp.mean(xv * xv, axis=-1, keepdims=True) + RMS_EPS)
        xhat = xv * rstd
        err = xhat * g_ref[...] - t_ref[...]

        @pl.when(pl.program_id(0) == 0)
        def _():
            dg_ref[...] = jnp.zeros_like(dg_ref)
            sq_ref[...] = jnp.zeros_like(sq_ref)

        sq_ref[...] += jnp.sum(err * err, axis=0, keepdims=True)
        dy = err * (1.0 / D)
        dg_ref[...] += jnp.sum(dy * xhat, axis=0, keepdims=True)
        dxh = dy * g_ref[...]
        dh = rstd * (dxh - xhat * jnp.mean(dxh * xhat, axis=-1, keepdims=True))
        dh_ref[...] = dh
        dhb_ref[...] = dh.astype(BF16)

    row = pl.BlockSpec((tm, D), lambda i: (i, 0))
    par = pl.BlockSpec((1, D), lambda i: (0, 0))
    return pl.pallas_call(
        body, name=name, grid=(T // tm,),
        in_specs=[row, par, row], out_specs=[row, row, par, par],
        out_shape=[jax.ShapeDtypeStruct((T, D), F32), jax.ShapeDtypeStruct((T, D), BF16),
                   jax.ShapeDtypeStruct((1, D), F32), jax.ShapeDtypeStruct((1, D), F32)],
        compiler_params=pltpu.CompilerParams(
            dimension_semantics=("arbitrary",), vmem_limit_bytes=_vmem(5 * _nbytes((tm, D), F32))),
    )(h, gain, target)


def _halo_specs(S, ts, tc, col_off, prev, nxt):
    r = ts // HALO
    last = S // HALO - 1
    specs = []
    if prev:
        specs.append(pl.BlockSpec((None, HALO, tc), lambda j, b, s: (b, jnp.maximum(s * r - 1, 0), col_off + j)))
    specs.append(pl.BlockSpec((None, ts, tc), lambda j, b, s: (b, s, col_off + j)))
    if nxt:
        specs.append(pl.BlockSpec((None, HALO, tc), lambda j, b, s: (b, jnp.minimum((s + 1) * r, last), col_off + j)))
    return specs


def _ext(prev_ref, main_ref, next_ref, first, last):
    main = main_ref[...].astype(F32)
    zeros = jnp.zeros((HALO, main.shape[1]), F32)
    top = zeros if prev_ref is None else jnp.where(first, 0.0, prev_ref[...].astype(F32))
    bot = zeros if next_ref is None else jnp.where(last, 0.0, next_ref[...].astype(F32))
    return jnp.concatenate([top, main, bot], axis=0)


def _shift(xe, k):
    return pltpu.roll(xe, k % xe.shape[0], axis=0)


def _cconv(xe, w):
    return w[0:1, :] * _shift(xe, 2) + w[1:2, :] * _shift(xe, 1) + w[2:3, :] * xe


def _main(xe, ts):
    return xe[HALO:HALO + ts, :]


def _cconv_t(de, w):
    return w[2:3, :] * de + w[1:2, :] * _shift(de, -1) + w[0:1, :] * _shift(de, -2)


def _conv_wgrad(dw_ref, de, xe, ts):
    d = _main(de, ts)
    dw_ref[0:1, :] += jnp.sum(d * _main(_shift(xe, 2), ts), axis=0, keepdims=True)
    dw_ref[1:2, :] += jnp.sum(d * _main(_shift(xe, 1), ts), axis=0, keepdims=True)
    dw_ref[2:3, :] += jnp.sum(d * _main(xe, ts), axis=0, keepdims=True)


def _gate_tiles(S, C, rows=1024):
    return _tile(S, rows), _tile(C, 512)


def _shortconv_fwd(bch, conv_w, name):
    B, S, D3 = bch.shape
    D = D3 // 3
    ts, tc = _gate_tiles(S, D, rows=2048)
    nj, ns = D // tc, S // ts

    def body(b_ref, cp_ref, c_ref, hp_ref, h_ref, w_ref, o_ref):
        first = pl.program_id(2) == 0
        ce = _ext(cp_ref, c_ref, None, first, False)
        he = _ext(hp_ref, h_ref, None, first, False)
        cv = _main(_cconv(ce * he, w_ref[...]), ts)
        o_ref[...] = (b_ref[...].astype(F32) * cv).astype(BF16)

    in_specs = (_halo_specs(S, ts, tc, 0, False, False) + _halo_specs(S, ts, tc, nj, True, False)
                + _halo_specs(S, ts, tc, 2 * nj, True, False) + [pl.BlockSpec((3, tc), lambda j, b, s: (0, j))])
    return pl.pallas_call(
        body, name=name, grid=(nj, B, ns), in_specs=in_specs,
        out_specs=pl.BlockSpec((None, ts, tc), lambda j, b, s: (b, s, j)),
        out_shape=jax.ShapeDtypeStruct((B, S, D), BF16),
        compiler_params=pltpu.CompilerParams(
            dimension_semantics=("parallel", "parallel", "parallel"),
            vmem_limit_bytes=_vmem(12 * _nbytes((ts + 2 * HALO, tc), F32))),
    )(bch, bch, bch, bch, bch, conv_w)


def _shortconv_bwd(bch, dg, conv_w, name):
    B, S, D3 = bch.shape
    D = D3 // 3
    ts, tc = _gate_tiles(S, D)
    nj, ns = D // tc, S // ts

    def body(b_ref, bn_ref, cp_ref, c_ref, cn_ref, hp_ref, h_ref, hn_ref, d_ref, dn_ref, w_ref, o_ref, dw_ref):
        s = pl.program_id(2)
        first, last = s == 0, s == ns - 1

        @pl.when(jnp.logical_and(pl.program_id(1) == 0, s == 0))
        def _():
            dw_ref[...] = jnp.zeros_like(dw_ref)

        w = w_ref[...]
        be = _ext(None, b_ref, bn_ref, first, last)
        ce = _ext(cp_ref, c_ref, cn_ref, first, last)
        he = _ext(hp_ref, h_ref, hn_ref, first, last)
        de = _ext(None, d_ref, dn_ref, first, last)
        ch = ce * he
        dcv = de * be
        dch = _main(_cconv_t(dcv, w), ts)
        o_ref[0] = (_main(de, ts) * _main(_cconv(ch, w), ts)).astype(BF16)
        o_ref[1] = (dch * _main(he, ts)).astype(BF16)
        o_ref[2] = (dch * _main(ce, ts)).astype(BF16)
        _conv_wgrad(dw_ref, dcv, ch, ts)

    in_specs = (_halo_specs(S, ts, tc, 0, False, True) + _halo_specs(S, ts, tc, nj, True, True)
                + _halo_specs(S, ts, tc, 2 * nj, True, True) + _halo_specs(S, ts, tc, 0, False, True)
                + [pl.BlockSpec((3, tc), lambda j, b, s: (0, j))])
    return pl.pallas_call(
        body, name=name, grid=(nj, B, ns), in_specs=in_specs,
        out_specs=[pl.BlockSpec((3, None, ts, tc), lambda j, b, s: (0, b, s, j)),
                   pl.BlockSpec((3, tc), lambda j, b, s: (0, j))],
        out_shape=[jax.ShapeDtypeStruct((3, B, S, D), BF16), jax.ShapeDtypeStruct((3, D), F32)],
        compiler_params=pltpu.CompilerParams(
            dimension_semantics=("parallel", "arbitrary", "arbitrary"),
            vmem_limit_bytes=_vmem(24 * _nbytes((ts + 2 * HALO, tc), F32))),
    )(bch, bch, bch, bch, bch, bch, bch, bch, dg, dg, conv_w)


def _sigmoid(x):
    return 1.0 / (1.0 + jnp.exp(-x))


def _ffn_gate_fwd(u0, conv_w, conv_b, name):
    B, S, F2 = u0.shape
    F = F2 // 2
    ts, tc = _gate_tiles(S, F, rows=2048)
    nj, ns = F // tc, S // ts

    def body(gp_ref, g_ref, up_ref, u_ref, wg_ref, wu_ref, bg_ref, bu_ref, o_ref):
        first = pl.program_id(2) == 0
        ug = _main(_cconv(_ext(gp_ref, g_ref, None, first, False), wg_ref[...]), ts) + bg_ref[...]
        uu = _main(_cconv(_ext(up_ref, u_ref, None, first, False), wu_ref[...]), ts) + bu_ref[...]
        o_ref[...] = (ug * _sigmoid(ug) * uu).astype(BF16)

    in_specs = (_halo_specs(S, ts, tc, 0, True, False) + _halo_specs(S, ts, tc, nj, True, False)
                + [pl.BlockSpec((3, tc), lambda j, b, s: (0, j)), pl.BlockSpec((3, tc), lambda j, b, s: (0, nj + j)),
                   pl.BlockSpec((1, tc), lambda j, b, s: (0, j)), pl.BlockSpec((1, tc), lambda j, b, s: (0, nj + j))])
    return pl.pallas_call(
        body, name=name, grid=(nj, B, ns), in_specs=in_specs,
        out_specs=pl.BlockSpec((None, ts, tc), lambda j, b, s: (b, s, j)),
        out_shape=jax.ShapeDtypeStruct((B, S, F), BF16),
        compiler_params=pltpu.CompilerParams(
            dimension_semantics=("parallel", "parallel", "parallel"),
            vmem_limit_bytes=_vmem(12 * _nbytes((ts + 2 * HALO, tc), F32))),
    )(u0, u0, u0, u0, conv_w, conv_w, conv_b, conv_b)


def _ffn_gate_bwd(u0, dact, conv_w, conv_b, name):
    B, S, F2 = u0.shape
    F = F2 // 2
    ts, tc = _gate_tiles(S, F)
    nj, ns = F // tc, S // ts

    def body(gp_ref, g_ref, gn_ref, up_ref, u_ref, un_ref, d_ref, dn_ref, wg_ref, wu_ref, bg_ref, bu_ref,
             o_ref, dwg_ref, dwu_ref, dbg_ref, dbu_ref):
        s = pl.program_id(2)
        first, last = s == 0, s == ns - 1

        @pl.when(jnp.logical_and(pl.program_id(1) == 0, s == 0))
        def _():
            for r in (dwg_ref, dwu_ref, dbg_ref, dbu_ref):
                r[...] = jnp.zeros_like(r)

        wg, wu = wg_ref[...], wu_ref[...]
        ge = _ext(gp_ref, g_ref, gn_ref, first, last)
        ue = _ext(up_ref, u_ref, un_ref, first, last)
        de = _ext(None, d_ref, dn_ref, first, last)
        ug = _cconv(ge, wg) + bg_ref[...]
        uu = _cconv(ue, wu) + bu_ref[...]
        sig = _sigmoid(ug)
        dug = de * uu * (sig * (1.0 + ug * (1.0 - sig)))
        duu = de * (ug * sig)
        o_ref[0] = _main(_cconv_t(dug, wg), ts).astype(BF16)
        o_ref[1] = _main(_cconv_t(duu, wu), ts).astype(BF16)
        _conv_wgrad(dwg_ref, dug, ge, ts)
        _conv_wgrad(dwu_ref, duu, ue, ts)
        dbg_ref[...] += jnp.sum(_main(dug, ts), axis=0, keepdims=True)
        dbu_ref[...] += jnp.sum(_main(duu, ts), axis=0, keepdims=True)

    w3 = lambda off: pl.BlockSpec((3, tc), lambda j, b, s: (0, off + j))
    w1 = lambda off: pl.BlockSpec((1, tc), lambda j, b, s: (0, off + j))
    in_specs = (_halo_specs(S, ts, tc, 0, True, True) + _halo_specs(S, ts, tc, nj, True, True)
                + _halo_specs(S, ts, tc, 0, False, True) + [w3(0), w3(nj), w1(0), w1(nj)])
    outs = pl.pallas_call(
        body, name=name, grid=(nj, B, ns), in_specs=in_specs,
        out_specs=[pl.BlockSpec((2, None, ts, tc), lambda j, b, s: (0, b, s, j)), w3(0), w3(0), w1(0), w1(0)],
        out_shape=[jax.ShapeDtypeStruct((2, B, S, F), BF16), jax.ShapeDtypeStruct((3, F), F32),
                   jax.ShapeDtypeStruct((3, F), F32), jax.ShapeDtypeStruct((1, F), F32),
                   jax.ShapeDtypeStruct((1, F), F32)],
        compiler_params=pltpu.CompilerParams(
            dimension_semantics=("parallel", "arbitrary", "arbitrary"),
            vmem_limit_bytes=_vmem(30 * _nbytes((ts + 2 * HALO, tc), F32))),
    )(u0, u0, u0, u0, u0, u0, dact, dact, conv_w, conv_w, conv_b, conv_b)
    du0, dwg, dwu, dbg, dbu = outs
    return du0, jnp.concatenate([dwg, dwu], axis=1), jnp.concatenate([dbg, dbu], axis=1)


def _t5_bucket(dist):
    max_exact = REL_BUCKETS // 2
    n = jnp.maximum(dist, 0)
    nf = jnp.maximum(n, max_exact).astype(F32)
    large = max_exact + (jnp.log(nf / max_exact) / math.log(REL_MAX_DISTANCE / max_exact)
                         * (REL_BUCKETS - max_exact)).astype(jnp.int32)
    large = jnp.minimum(large, REL_BUCKETS - 1)
    return jnp.where(n < max_exact, n, large)


def _band_tables(window, dilation):
    P = ATT_BLOCK
    qi = jnp.arange(P, dtype=jnp.int32)[:, None]
    kc = jnp.arange(2 * P, dtype=jnp.int32)[None, :]
    delta = qi + P - kc
    band = (delta >= 0) & (delta <= window // dilation)
    bucket = _t5_bucket(delta * dilation).reshape(-1)
    onehot = (bucket[None, :] == jnp.arange(REL_BUCKETS, dtype=jnp.int32)[:, None]).astype(F32)
    return onehot, band


def _bias_lookup(rel_bias_t, onehots, name):
    nb, _, Q = onehots.shape
    H = rel_bias_t.shape[0]

    def body(r_ref, oh_ref, o_ref):
        o_ref[...] = lax.dot_general(r_ref[...], oh_ref[...], (((1,), (0,)), ((), ())),
                                     precision=lax.Precision.HIGHEST, preferred_element_type=F32)

    return pl.pallas_call(
        body, name=name, grid=(nb,),
        in_specs=[pl.BlockSpec((H, REL_BUCKETS), lambda i: (0, 0)),
                  pl.BlockSpec((None, REL_BUCKETS, Q), lambda i: (i, 0, 0))],
        out_specs=pl.BlockSpec((None, H, Q), lambda i: (i, 0, 0)),
        out_shape=jax.ShapeDtypeStruct((nb, H, Q), F32),
        compiler_params=pltpu.CompilerParams(dimension_semantics=("parallel",),
                                             vmem_limit_bytes=_vmem(_nbytes((REL_BUCKETS + H, Q), F32))),
    )(rel_bias_t, onehots)


def _bias_grad(dtabs, onehots, name):
    nb, H, Q = dtabs.shape

    def body(d_ref, oh_ref, o_ref):
        @pl.when(pl.program_id(0) == 0)
        def _():
            o_ref[...] = jnp.zeros_like(o_ref)

        o_ref[...] += lax.dot_general(d_ref[...], oh_ref[...], (((1,), (1,)), ((), ())),
                                      precision=lax.Precision.HIGHEST, preferred_element_type=F32)

    return pl.pallas_call(
        body, name=name, grid=(nb,),
        in_specs=[pl.BlockSpec((None, H, Q), lambda i: (i, 0, 0)),
                  pl.BlockSpec((None, REL_BUCKETS, Q), lambda i: (i, 0, 0))],
        out_specs=pl.BlockSpec((H, REL_BUCKETS), lambda i: (0, 0)),
        out_shape=jax.ShapeDtypeStruct((H, REL_BUCKETS), F32),
        compiler_params=pltpu.CompilerParams(dimension_semantics=("arbitrary",),
                                             vmem_limit_bytes=_vmem(_nbytes((REL_BUCKETS + H, Q), F32))),
    )(dtabs, onehots)


def _lane_masks():
    lane = lax.broadcasted_iota(jnp.int32, (1, LANE), 1)
    lo = lane < LANE // 2
    return lo, jnp.logical_not(lo)


def _dot_nt(a, b):
    return lax.dot_general(a, b, (((1,), (1,)), ((), ())), preferred_element_type=F32)


def _dot_nn(a, b):
    return lax.dot_general(a, b, (((1,), (0,)), ((), ())), preferred_element_type=F32)


def _dot_tn(a, b):
    return lax.dot_general(a, b, (((0,), (0,)), ((), ())), preferred_element_type=F32)


def _block_rows(n, dilation, S):
    P = ATT_BLOCK
    nb = S // (dilation * P)
    r, i = n // nb, n % nb
    cur = pl.ds(i * (P * dilation) + r, P, stride=dilation)
    prv = pl.ds(jnp.maximum(i - 1, 0) * (P * dilation) + r, P, stride=dilation)
    return cur, prv, jnp.minimum(i, 1)


def _attn_fwd(q, kv, tabs, name):
    B, S, D = q.shape
    P, H = ATT_BLOCK, N_HEADS
    scale = (D // H) ** -0.5
    assert math.log2(scale).is_integer(), "q is scaled before its bf16 rounding: exact only for a power of two"
    half = LANE // 2
    nsteps = S // P // ATT_UNROLL

    def body(q_ref, k_ref, v_ref, tab_ref, o_ref, lse_ref, acc_ref, m_ref, l_ref):
        lo, hi = _lane_masks()
        for bi, (_, d) in enumerate(DILATED_BRANCHES):
            def block(n, carry, bi=bi, d=d):
                rows = [_block_rows(n + j * nsteps, d, S) for j in range(ATT_UNROLL)]
                loaded = []
                for cur, prv, variant in rows:
                    qb = (q_ref[cur, :] * scale).astype(BF16)
                    kw = jnp.concatenate([k_ref[prv, :], k_ref[cur, :]], axis=0).astype(BF16)
                    vw = jnp.concatenate([v_ref[prv, :], v_ref[cur, :]], axis=0).astype(BF16)
                    loaded.append((qb, kw, vw))
                results = []
                for (cur, prv, variant), (qb, kw, vw) in zip(rows, loaded):
                    m_pair = l_pair = a_pair = None
                    for e, msk in enumerate((lo, hi)):
                        s = _dot_nt(jnp.where(msk, qb, 0), kw) + tab_ref[bi, variant, e]
                        m_e = jnp.max(s, axis=-1, keepdims=True)
                        pe = jnp.exp(s - m_e)
                        l_e = jnp.sum(pe, axis=-1, keepdims=True)
                        a_e = _dot_nn(pe.astype(BF16), jnp.where(msk, vw, 0))
                        m_pair = m_e if e == 0 else jnp.where(lo, m_pair, m_e)
                        l_pair = l_e if e == 0 else jnp.where(lo, l_pair, l_e)
                        a_pair = a_e if e == 0 else a_pair + a_e
                    results.append((m_pair, l_pair, a_pair))
                for (cur, prv, variant), (m_pair, l_pair, a_pair) in zip(rows, results):
                    m_ref[bi, cur, :] = m_pair
                    l_ref[bi, cur, :] = l_pair
                    acc_ref[bi, cur, :] = a_pair
                return carry

            lax.fori_loop(0, nsteps, block, 0)

        nbr = len(DILATED_BRANCHES)
        chunk = 512

        def merge(t, carry):
            rs = pl.ds(pl.multiple_of(t * chunk, chunk), chunk)
            ms = [m_ref[b, rs, :] for b in range(nbr)]
            m = functools.reduce(jnp.maximum, ms)
            ws = [jnp.exp(mb - m) for mb in ms]
            l = functools.reduce(jnp.add, [w * l_ref[b, rs, :] for b, w in enumerate(ws)])
            acc = functools.reduce(jnp.add, [w * acc_ref[b, rs, :] for b, w in enumerate(ws)])
            o_ref[rs, :] = (acc / l).astype(BF16)
            lse_ref[rs, :] = m + jnp.log(l)
            return carry

        lax.fori_loop(0, S // chunk, merge, 0)

    nl = D // LANE
    col = lambda off: pl.BlockSpec((None, S, LANE), lambda b, p: (b, 0, off + p))
    tab_spec = pl.BlockSpec((3, 2, 2, P, 2 * P), lambda b, p: (0, 0, p, 0, 0))
    return pl.pallas_call(
        body, name=name, grid=(B, H // 2), in_specs=[col(0), col(0), col(nl), tab_spec],
        out_specs=[col(0), col(0)],
        out_shape=[jax.ShapeDtypeStruct((B, S, D), BF16), jax.ShapeDtypeStruct((B, S, D), F32)],
        scratch_shapes=[pltpu.VMEM((len(DILATED_BRANCHES), S, LANE), F32)] * 3,
        compiler_params=pltpu.CompilerParams(
            dimension_semantics=("parallel", "parallel"), vmem_limit_bytes=VMEM_CAP),
    )(q, kv, kv, tabs)


def _attn_bwd(q, kv, do, o, lse, tabs, name):
    B, S, D = q.shape
    P, H = ATT_BLOCK, N_HEADS
    scale = (D // H) ** -0.5
    assert math.log2(scale).is_integer(), "q is scaled before its bf16 rounding: exact only for a power of two"
    half = LANE // 2
    nsteps = S // P // ATT_UNROLL

    def body(q_ref, k_ref, v_ref, do_ref, o_ref, lse_ref, tab_ref, dq_out, dkv_out, dtab_ref, delta_ref,
             dq_ref, dkv_ref):
        lo, hi = _lane_masks()

        @pl.when(pl.program_id(1) == 0)
        def _():
            dtab_ref[...] = jnp.zeros_like(dtab_ref)

        dkv_ref[...] = jnp.zeros_like(dkv_ref)
        prod = do_ref[...] * o_ref[...].astype(F32)
        delta_ref[...] = jnp.where(lo, jnp.sum(jnp.where(lo, prod, 0.0), axis=-1, keepdims=True),
                                   jnp.sum(jnp.where(hi, prod, 0.0), axis=-1, keepdims=True))

        for bi, (_, d) in enumerate(DILATED_BRANCHES):
            def block(n, carry, bi=bi, d=d):
                rows = [_block_rows(n + j * nsteps, d, S) for j in range(ATT_UNROLL)]
                loaded = []
                for cur, prv, variant in rows:
                    qb = (q_ref[cur, :] * scale).astype(BF16)
                    kw = jnp.concatenate([k_ref[prv, :], k_ref[cur, :]], axis=0).astype(BF16)
                    vw = jnp.concatenate([v_ref[prv, :], v_ref[cur, :]], axis=0).astype(BF16)
                    dob = do_ref[cur, :].astype(BF16)
                    dq_old = dq_ref[cur, :] if bi > 0 else None
                    loaded.append((qb, kw, vw, dob, lse_ref[cur, :], delta_ref[cur, :], dq_old))
                results = []
                for j, ((cur, prv, variant), (qb, kw, vw, dob, lse_b, dl_b, dq_old)) in enumerate(zip(rows, loaded)):
                    dq_pair = dk_pair = dv_pair = None
                    for e, msk in enumerate((lo, hi)):
                        qm = jnp.where(msk, qb, 0)
                        dom = jnp.where(msk, dob, 0)
                        s = _dot_nt(qm, kw) + tab_ref[bi, variant, e]
                        pe = jnp.exp(s - lse_b[:, e * half:e * half + 1])
                        ds = pe * (_dot_nt(dom, vw) - dl_b[:, e * half:e * half + 1])
                        dtab_ref[bi, e] += ds
                        dsb = ds.astype(BF16)
                        dq_e = _dot_nn(dsb, jnp.where(msk, kw, 0))
                        dk_e = _dot_tn(dsb, qm)
                        dv_e = _dot_tn(pe.astype(BF16), dom)
                        dq_pair = dq_e if e == 0 else dq_pair + dq_e
                        dk_pair = dk_e if e == 0 else dk_pair + dk_e
                        dv_pair = dv_e if e == 0 else dv_pair + dv_e
                    dq_pair = dq_pair * scale
                    if bi > 0:
                        dq_pair = dq_pair + dq_old
                    results.append((dq_pair, dk_pair, dv_pair))
                for (cur, prv, variant), (dq_pair, dk_pair, dv_pair) in zip(rows, results):
                    dq_ref[cur, :] = dq_pair
                    dkv_ref[0, cur, :] += dk_pair[P:, :]
                    dkv_ref[1, cur, :] += dv_pair[P:, :]
                    dkv_ref[0, prv, :] += dk_pair[:P, :]
                    dkv_ref[1, prv, :] += dv_pair[:P, :]
                return carry

            lax.fori_loop(0, nsteps, block, 0)
        dq_out[...] = dq_ref[...].astype(BF16)
        dkv_out[...] = dkv_ref[...].astype(BF16)

    nl = D // LANE
    col = lambda off: pl.BlockSpec((None, S, LANE), lambda p, b: (b, 0, off + p))
    tab_spec = pl.BlockSpec((3, 2, 2, P, 2 * P), lambda p, b: (0, 0, p, 0, 0))
    return pl.pallas_call(
        body, name=name, grid=(H // 2, B),
        in_specs=[col(0), col(0), col(nl), col(0), col(0), col(0), tab_spec],
        out_specs=[col(0), pl.BlockSpec((2, None, S, LANE), lambda p, b: (0, b, 0, p)),
                   pl.BlockSpec((None, 3, 2, P, 2 * P), lambda p, b: (p, 0, 0, 0, 0))],
        out_shape=[jax.ShapeDtypeStruct((B, S, D), BF16), jax.ShapeDtypeStruct((2, B, S, D), BF16),
                   jax.ShapeDtypeStruct((H // 2, 3, 2, P, 2 * P), F32)],
        scratch_shapes=[pltpu.VMEM((S, LANE), F32), pltpu.VMEM((S, LANE), F32), pltpu.VMEM((2, S, LANE), F32)],
        compiler_params=pltpu.CompilerParams(
            dimension_semantics=("parallel", "arbitrary"),
            vmem_limit_bytes=VMEM_CAP),
    )(q, kv, kv, do, o, lse, tabs)


def _mesh_place():
    x, y, c = lax.axis_index("x"), lax.axis_index("y"), lax.axis_index("c")
    return x, y, c, (x, y, 1 - c), [(1 - x, y), (x, 1 - y), (1 - x, 1 - y)]


def _rdma(src, dst, send_sems, recv_sems, idx, to):
    return pltpu.make_async_remote_copy(
        src_ref=src, dst_ref=dst, send_sem=send_sems.at[idx], recv_sem=recv_sems.at[idx],
        device_id=to, device_id_type=pl.DeviceIdType.MESH)


def _comm_call(body, name, args, out_shape, n_sems, n_local):
    any_spec = pl.BlockSpec(memory_space=pl.ANY)
    return pl.pallas_call(
        body, name=name, in_specs=[any_spec] * len(args), out_specs=[any_spec] * len(out_shape),
        out_shape=out_shape,
        scratch_shapes=[pltpu.SemaphoreType.DMA(n_sems), pltpu.SemaphoreType.DMA(n_sems),
                        pltpu.SemaphoreType.DMA((n_local,))],
        compiler_params=pltpu.CompilerParams(has_side_effects=True),
    )(*args)


def _all_gather(arrays, name):
    n = len(arrays)

    def body(*refs):
        ins, outs = refs[:n], refs[n:2 * n]
        send_sems, recv_sems, loc_sems = refs[2 * n:]
        x, y, c, sib, chips = _mesh_place()
        me = (x, y, c)

        def rows(a, dev):
            return outs[a].at[4 * dev[0] + 2 * dev[1] + dev[2]]

        def copy(a, k, block, to, src=None):
            return _rdma(rows(a, block) if src is None else src, rows(a, block), send_sems, recv_sems, (a, k), to)

        local, first, passed = [], [], []
        for a in range(n):
            cp = pltpu.make_async_copy(ins[a], rows(a, me), loc_sems.at[a])
            cp.start()
            local.append(cp)
            first.append(copy(a, 0, me, sib, src=ins[a]))
            first += [copy(a, 1 + j, me, (*chip, c), src=ins[a]) for j, chip in enumerate(chips)]
        for cp in first:
            cp.start()
        for j, chip in enumerate(chips):
            for a in range(n):
                copy(a, 1 + j, (*chip, c), me).wait_recv()
                cp = copy(a, 4 + j, (*chip, c), sib)
                cp.start()
                passed.append(cp)
        for a in range(n):
            copy(a, 0, sib, me).wait_recv()
            for j, chip in enumerate(chips):
                copy(a, 4 + j, (*chip, 1 - c), me).wait_recv()
        for cp in first + passed:
            cp.wait_send()
        for cp in local:
            cp.wait()

    out_shape = [jax.ShapeDtypeStruct((N_DEV,) + g.shape, g.dtype) for g in arrays]
    return list(_comm_call(body, name, arrays, out_shape, (n, N_DEV - 1), n))


def _core_pair_exchange(halves, gathered, name):
    nh, ng = len(halves), len(gathered)

    def body(*refs):
        h_in, g_in = refs[:nh], refs[nh:nh + ng]
        outs = refs[nh + ng:nh + ng + nh + ng]
        theirs, g_out = outs[:nh], outs[nh:]
        send_sems, recv_sems, loc_sems = refs[nh + ng + nh + ng:]
        x, y, c, sib, chips = _mesh_place()
        me = 4 * x + 2 * y + c
        peers = [sib] + [(*chip, pc) for chip in chips for pc in (c, 1 - c)]
        nchip = N_DEV // 2
        gbase = nh * nchip
        local, sends = [], []
        for a in range(nh):
            for ch in range(nchip):
                cp = _rdma(h_in[a].at[ch, 1 - c], theirs[a].at[ch], send_sems, recv_sems, a * nchip + ch, sib)
                cp.start()
                sends.append(cp)
        for g in range(ng):
            cp = pltpu.make_async_copy(g_in[g], g_out[g].at[me], loc_sems.at[g])
            cp.start()
            local.append(cp)
            for k, peer in enumerate(peers):
                cp = _rdma(g_in[g], g_out[g].at[me], send_sems, recv_sems, gbase + g * (N_DEV - 1) + k, peer)
                cp.start()
                sends.append(cp)
        for cp in sends:
            cp.wait_send()
        for a in range(nh):
            for ch in range(nchip):
                _rdma(h_in[a].at[ch, 1 - c], theirs[a].at[ch], send_sems, recv_sems, a * nchip + ch, sib).wait_recv()
        for g in range(ng):
            for k, peer in enumerate(peers):
                pid = 4 * peer[0] + 2 * peer[1] + peer[2]
                _rdma(g_in[g], g_out[g].at[pid], send_sems, recv_sems, gbase + g * (N_DEV - 1) + k, peer).wait_recv()
        for cp in local:
            cp.wait()

    half = [jax.ShapeDtypeStruct((h.shape[0],) + h.shape[2:], h.dtype) for h in halves]
    out_shape = half + [jax.ShapeDtypeStruct((N_DEV,) + g.shape, g.dtype) for g in gathered]
    outs = _comm_call(body, name, list(halves) + list(gathered), out_shape,
                      (nh * (N_DEV // 2) + ng * (N_DEV - 1),), max(ng, 1))
    return list(outs[:nh]), list(outs[nh:])


def _gather_copies(src_refs, land_refs, send_sems, recv_sems):
    x, y, c, sib, chips = _mesh_place()
    me = 4 * x + 2 * y + c
    peers = [sib] + [(*chip, c) for chip in chips]
    return [pltpu.make_async_remote_copy(
                src_ref=src, dst_ref=land.at[me], send_sem=send_sems[4 * a + k], recv_sem=recv_sems[4 * a + k],
                device_id=peer, device_id_type=pl.DeviceIdType.MESH)
            for a, (src, land) in enumerate(zip(src_refs, land_refs)) for k, peer in enumerate(peers)]


def _forward_to_sibling(gathered, name):
    n = len(gathered)

    def body(*refs):
        outs = refs[n:2 * n]
        send_sems, recv_sems, _ = refs[2 * n:]
        x, y, c, sib, chips = _mesh_place()
        sends = []
        for a in range(n):
            for j, chip in enumerate(chips):
                rows = outs[a].at[4 * chip[0] + 2 * chip[1] + c]
                cp = _rdma(rows, rows, send_sems, recv_sems, (a, j), sib)
                cp.start()
                sends.append(cp)
        for cp in sends:
            cp.wait_send()
        for a in range(n):
            for j, chip in enumerate(chips):
                rows = outs[a].at[4 * chip[0] + 2 * chip[1] + (1 - c)]
                _rdma(rows, rows, send_sems, recv_sems, (a, j), sib).wait_recv()

    any_spec = pl.BlockSpec(memory_space=pl.ANY)
    return list(pl.pallas_call(
        body, name=name, in_specs=[any_spec] * n, out_specs=[any_spec] * n,
        out_shape=[jax.ShapeDtypeStruct(g.shape, g.dtype) for g in gathered],
        input_output_aliases={i: i for i in range(n)},
        scratch_shapes=[pltpu.SemaphoreType.DMA((n, 3)), pltpu.SemaphoreType.DMA((n, 3)),
                        pltpu.SemaphoreType.DMA((1,))],
        compiler_params=pltpu.CompilerParams(has_side_effects=True),
    )(*gathered))


def _split_start(copies_fn, srcs, land_shapes, n_sems, name, after=None):
    ns, nl = len(srcs), len(land_shapes)
    nc = n_sems
    hbm = pl.BlockSpec(memory_space=pltpu.HBM)
    sem = pl.BlockSpec(memory_space=pltpu.SEMAPHORE)

    n_in = ns + nl + (0 if after is None else 1)

    def body(*refs):
        src_refs, land_refs = refs[:ns], refs[ns:ns + nl]
        send_sems, recv_sems = refs[n_in:n_in + nc], refs[n_in + nc:n_in + 2 * nc]
        token = refs[-1]
        for cp in copies_fn(src_refs, land_refs, send_sems, recv_sems):
            cp.start()
        token[...] = jnp.zeros_like(token)

    lands = [lax.empty(s.shape, s.dtype) for s in land_shapes]
    thru = [pltpu.HBM(t.shape, t.dtype) for t in list(srcs) + lands]
    order = [] if after is None else [after]
    outs = pl.pallas_call(
        body, name=name,
        out_shape=(*([pltpu.SemaphoreType.DMA(())] * (2 * nc)), *thru, jax.ShapeDtypeStruct((8, LANE), F32)),
        in_specs=[hbm] * (ns + nl) + [pl.BlockSpec(memory_space=pl.ANY)] * len(order),
        out_specs=(*([sem] * (2 * nc)), *([hbm] * (ns + nl)), pl.BlockSpec(memory_space=pltpu.VMEM)),
        input_output_aliases={i: 2 * nc + i for i in range(ns + nl)},
        compiler_params=pltpu.CompilerParams(has_side_effects=pltpu.SideEffectType.DATAFLOW_SIDE_EFFECTING),
    )(*[pltpu.with_memory_space_constraint(t, pltpu.HBM) for t in list(srcs) + lands], *order)
    thru_out = outs[2 * nc:2 * nc + ns + nl]
    return list(outs[:nc]), list(outs[nc:2 * nc]), list(thru_out[:ns]), list(thru_out[ns:]), outs[-1]


def _split_wait(copies_fn, send_sems, recv_sems, srcs, lands, after, name):
    ns, nl, nc = len(srcs), len(lands), len(send_sems)
    hbm = pl.BlockSpec(memory_space=pltpu.HBM)
    sem = pl.BlockSpec(memory_space=pltpu.SEMAPHORE)

    def body(*refs):
        src_refs, land_refs = refs[:ns], refs[ns:ns + nl]
        send_refs, recv_refs = refs[ns + nl:ns + nl + nc], refs[ns + nl + nc:ns + nl + 2 * nc]
        for cp in copies_fn(src_refs, land_refs, send_refs, recv_refs):
            cp.wait_send()
            cp.wait_recv()

    outs = pl.pallas_call(
        body, name=name,
        out_shape=[pltpu.HBM(t.shape, t.dtype) for t in list(srcs) + list(lands)],
        in_specs=[hbm] * (ns + nl) + [sem] * (2 * nc) + [pl.BlockSpec(memory_space=pl.ANY)],
        out_specs=[hbm] * (ns + nl),
        input_output_aliases={i: i for i in range(ns + nl)},
        compiler_params=pltpu.CompilerParams(has_side_effects=pltpu.SideEffectType.DATAFLOW_SIDE_EFFECTING),
    )(*srcs, *lands, *send_sems, *recv_sems, after)
    return list(outs[:ns]), list(outs[ns:])


def _pair_copies(src_refs, land_refs, send_sems, recv_sems):
    x, y, c, sib, chips = _mesh_place()
    nchip = N_DEV // 2
    return [pltpu.make_async_remote_copy(
                src_ref=src.at[ch, 1 - c], dst_ref=land.at[ch], send_sem=send_sems[nchip * a + ch],
                recv_sem=recv_sems[nchip * a + ch], device_id=sib, device_id_type=pl.DeviceIdType.MESH)
            for a, (src, land) in enumerate(zip(src_refs, land_refs)) for ch in range(nchip)]


def _chip_copies(src_refs, land_refs, send_sems, recv_sems):
    x, y, c, sib, chips = _mesh_place()
    return [pltpu.make_async_remote_copy(
                src_ref=src.at[2 * chip[0] + chip[1]], dst_ref=land.at[j], send_sem=send_sems[3 * a + j],
                recv_sem=recv_sems[3 * a + j], device_id=(*chip, c), device_id_type=pl.DeviceIdType.MESH)
            for a, (src, land) in enumerate(zip(src_refs, land_refs)) for j, chip in enumerate(chips)]


def _pair_sum(halves, theirs, core, name, tr=256):
    nchip, _, R, C = halves.shape
    tr = _row_tile(R, tr)

    def body(core_ref, h_ref, t_ref, o_ref):
        o_ref[...] = (h_ref[...] + t_ref[...]).astype(BF16)

    grid_spec = pltpu.PrefetchScalarGridSpec(
        num_scalar_prefetch=1, grid=(nchip, R // tr),
        in_specs=[pl.BlockSpec((None, None, tr, C), lambda ch, i, core_ref: (ch, core_ref[0], i, 0)),
                  pl.BlockSpec((None, tr, C), lambda ch, i, core_ref: (ch, i, 0))],
        out_specs=pl.BlockSpec((None, tr, C), lambda ch, i, core_ref: (ch, i, 0)))
    return pl.pallas_call(
        body, name=name, grid_spec=grid_spec, out_shape=jax.ShapeDtypeStruct((nchip, R, C), BF16),
        compiler_params=pltpu.CompilerParams(
            dimension_semantics=("parallel", "parallel"), vmem_limit_bytes=_vmem(3 * _nbytes((tr, C), F32))),
    )(core, *[pltpu.with_memory_space_constraint(t, pltpu.HBM) for t in (halves, theirs)])


def _chip_sum(own, recv, chip, name, tr=256):
    _, R, C = own.shape
    tr = _row_tile(R, tr)

    def body(chip_ref, o_ref, r_ref, out_ref):
        acc = o_ref[...].astype(F32)
        for j in range(3):
            acc = acc + r_ref[j].astype(F32)
        out_ref[...] = acc

    grid_spec = pltpu.PrefetchScalarGridSpec(
        num_scalar_prefetch=1, grid=(R // tr,),
        in_specs=[pl.BlockSpec((None, tr, C), lambda i, chip_ref: (chip_ref[0], i, 0)),
                  pl.BlockSpec((3, tr, C), lambda i, chip_ref: (0, i, 0))],
        out_specs=pl.BlockSpec((tr, C), lambda i, chip_ref: (i, 0)))
    return pl.pallas_call(
        body, name=name, grid_spec=grid_spec, out_shape=jax.ShapeDtypeStruct((R, C), F32),
        compiler_params=pltpu.CompilerParams(
            dimension_semantics=("parallel",), vmem_limit_bytes=_vmem(4 * _nbytes((tr, C), F32))),
    )(chip, *[pltpu.with_memory_space_constraint(t, pltpu.HBM) for t in (own, recv)])


def _row_tile(rows, target):
    best = rows
    for t in range(8, min(rows, target) + 1, 8):
        if rows % t == 0:
            best = t
    return best


def _sum_slots(recv, name, tr=128):
    n, R, C = recv.shape
    tr = _row_tile(R, tr if recv.dtype == F32 else 2 * tr)

    def body(r_ref, o_ref):
        acc = r_ref[0].astype(F32)
        for k in range(1, n):
            acc = acc + r_ref[k].astype(F32)
        o_ref[...] = acc

    return pl.pallas_call(
        body, name=name, grid=(R // tr,),
        in_specs=[pl.BlockSpec((n, tr, C), lambda i: (0, i, 0))],
        out_specs=pl.BlockSpec((tr, C), lambda i: (i, 0)),
        out_shape=jax.ShapeDtypeStruct((R, C), F32),
        compiler_params=pltpu.CompilerParams(
            dimension_semantics=("parallel",), vmem_limit_bytes=_vmem(10 * _nbytes((tr, C), F32))),
    )(recv)


def _adamw_math(g_ref, w_ref, m_ref, v_ref, d_ref, nm_ref, nv_ref):
    gv = g_ref[...]
    nm = ADAM_B1 * m_ref[...] + (1.0 - ADAM_B1) * gv
    nv = ADAM_B2 * v_ref[...] + (1.0 - ADAM_B2) * (gv * gv)
    m_hat = nm / (1.0 - ADAM_B1 ** ADAM_STEP)
    v_hat = nv / (1.0 - ADAM_B2 ** ADAM_STEP)
    d_ref[...] = -ADAM_LR * (m_hat / (jnp.sqrt(v_hat) + ADAM_EPS) + ADAM_WD * w_ref[...])
    nm_ref[...] = nm
    nv_ref[...] = nv


def _chip_sum_adamw(own, recv, chip, w, m, v, name, tr=256):
    _, R, C = own.shape
    tr = _row_tile(R, tr)

    def body(chip_ref, o_ref, r_ref, w_ref, m_ref, v_ref, g_ref, d_ref, nm_ref, nv_ref):
        acc = o_ref[...].astype(F32)
        for j in range(3):
            acc = acc + r_ref[j].astype(F32)
        g_ref[...] = acc
        _adamw_math(g_ref, w_ref, m_ref, v_ref, d_ref, nm_ref, nv_ref)

    row = pl.BlockSpec((tr, C), lambda i, chip_ref: (i, 0))
    grid_spec = pltpu.PrefetchScalarGridSpec(
        num_scalar_prefetch=1, grid=(R // tr,),
        in_specs=[pl.BlockSpec((None, tr, C), lambda i, chip_ref: (chip_ref[0], i, 0)),
                  pl.BlockSpec((3, tr, C), lambda i, chip_ref: (0, i, 0)), row, row, row],
        out_specs=[row] * 4)
    return pl.pallas_call(
        body, name=name, grid_spec=grid_spec, out_shape=[jax.ShapeDtypeStruct((R, C), F32)] * 4,
        compiler_params=pltpu.CompilerParams(
            dimension_semantics=("parallel",), vmem_limit_bytes=_vmem(10 * _nbytes((tr, C), F32))),
    )(chip, own, recv, w, m, v)


def _adamw_small(params, name):
    n = len(params)
    two_d = [(-1, w.shape[-1]) for w, _, _, _ in params]
    flat = [t.reshape(two_d[i]) for i, (w, m, v, g) in enumerate(params) for t in (g, w, m, v)]

    def body(*refs):
        ins, outs = refs[:4 * n], refs[4 * n:]
        for i in range(n):
            _adamw_math(*ins[4 * i:4 * i + 4], *outs[3 * i:3 * i + 3])

    outs = pl.pallas_call(
        body, name=name,
        out_shape=[jax.ShapeDtypeStruct(flat[4 * i].shape, F32) for i in range(n) for _ in range(3)],
    )(*flat)
    return [(g.reshape(w.shape),) + tuple(outs[3 * i + t].reshape(w.shape) for t in range(3))
            for i, (w, m, v, g) in enumerate(params)]


def _adamw(g, w, m, v, name, tr=256):
    R, C = g.shape
    tr = _row_tile(R, tr)
    body = functools.partial(_adamw_math)

    row = pl.BlockSpec((tr, C), lambda i: (i, 0))
    return pl.pallas_call(
        body, name=name, grid=(R // tr,), in_specs=[row] * 4, out_specs=[row] * 3,
        out_shape=[jax.ShapeDtypeStruct((R, C), F32)] * 3,
        compiler_params=pltpu.CompilerParams(
            dimension_semantics=("parallel",), vmem_limit_bytes=_vmem(8 * _nbytes((tr, C), F32))),
    )(*[pltpu.with_memory_space_constraint(t, pltpu.HBM) for t in (g, w, m, v)])


def _pack_rows(parts):
    flat = jnp.concatenate([p.reshape(-1).astype(F32) for p in parts])
    rows = -(-flat.shape[0] // (8 * LANE)) * 8
    return jnp.pad(flat, (0, rows * LANE - flat.shape[0])).reshape(rows, LANE)


def _unpack(flat, shapes):
    out, off = [], 0
    for shp in shapes:
        n = math.prod(shp)
        out.append(flat[off:off + n].reshape(shp))
        off += n
    return out


def kernel(x, a_norm, a_w_in, a_conv, a_w_out, kv_norm, w_kv, b_norm, b_w_q, b_w_o, rel_bias, ffn_norm, ffn_w_up, ffn_conv, ffn_conv_b, ffn_w_down, final_norm, loss_target, m_a_norm, m_a_w_in, m_a_conv, m_a_w_out, m_kv_norm, m_w_kv, m_b_norm, m_b_w_q, m_b_w_o, m_rel_bias, m_ffn_norm, m_ffn_w_up, m_ffn_conv, m_ffn_conv_b, m_ffn_w_down, m_final_norm, v_a_norm, v_a_w_in, v_a_conv, v_a_w_out, v_kv_norm, v_w_kv, v_b_norm, v_b_w_q, v_b_w_o, v_rel_bias, v_ffn_norm, v_ffn_w_up, v_ffn_conv, v_ffn_conv_b, v_ffn_w_down, v_final_norm):
    B, S, D = x.shape
    T = B * S
    F = ffn_w_down.shape[1] * N_DEV
    me = 4 * lax.axis_index("x") + 2 * lax.axis_index("y") + lax.axis_index("c")

    big_shards = [a_w_in[0].T, a_w_out[0], w_kv.T, b_w_q[0], b_w_o[0],
                  ffn_w_up[0].T, ffn_w_up[1].T, ffn_w_down[0], ffn_w_down[1]]
    small_shapes = [a_norm.shape, a_conv.shape, ffn_conv.shape]
    small_pack = _pack_rows([a_norm, a_conv, ffn_conv])
    shards = [s.astype(BF16) for s in big_shards]
    first = _all_gather([shards[0], shards[1], small_pack], "gather_weights_first")
    win_t, wout = first[0].reshape(-1, D), first[1].reshape(-1, D)
    smalls = [_unpack(first[2][j].reshape(-1), small_shapes) for j in range(N_DEV)]
    a_norm_f = jnp.concatenate([s[0] for s in smalls], axis=-1)
    a_conv_f = jnp.concatenate([s[1] for s in smalls], axis=-1)[0]
    ffn_conv_f = jnp.concatenate([s[2] for s in smalls], axis=-1)

    def gather_start(tag, idxs, after):
        srcs = [shards[i] for i in idxs]
        lands = [jax.ShapeDtypeStruct((N_DEV,) + s.shape, s.dtype) for s in srcs]
        return _split_start(_gather_copies, srcs, lands, 4 * len(srcs), f"gather_start_{tag}", after=after)

    def gather_finish(tag, handle, after):
        send, recv, srcs, lands, _ = handle
        srcs, lands = _split_wait(_gather_copies, send, recv, srcs, lands, after, f"gather_wait_{tag}")
        lands = _forward_to_sibling(lands, f"gather_forward_{tag}")
        return [lax.dynamic_update_slice(g, s[None], (me, 0, 0)).reshape(-1, D) for g, s in zip(lands, srcs)]

    ffn0_w = gather_start("ffn0", [5, 7], first[0])
    rest_w = gather_start("rest", [2, 3, 4, 6, 8], ffn0_w[4])

    x2 = x.reshape(T, D)
    (xn0,) = _rmsnorm_fwd(x2, a_norm_f, "a_norm_fwd")
    bch = _matmul(xn0, win_t, mode="nt", out_dtype=BF16, after=rest_w[4], name="a_in_proj", tm=2048).reshape(B, S, 3 * D)
    gated = _shortconv_fwd(bch, a_conv_f, "a_gate_fwd").reshape(T, D)
    h1 = _matmul(gated, wout, mode="nn", out_dtype=F32, add=x2, name="a_out_proj")
    wup0_t, wdn0 = gather_finish("ffn0", ffn0_w, h1)
    wup_t, wdn = [wup0_t, None], [wdn0, None]

    def ffn_fwd(h, l):
        (xn,) = _rmsnorm_fwd(h, ffn_norm[l:l + 1], f"ffn{l}_norm_fwd")
        u0 = _matmul(xn, wup_t[l], mode="nt", out_dtype=BF16, name=f"ffn{l}_up", tm=2048, tn=1408).reshape(B, S, 2 * F)
        act = _ffn_gate_fwd(u0, ffn_conv_f[l], ffn_conv_b[l:l + 1], f"ffn{l}_gate_fwd").reshape(T, F)
        out = _matmul(act, wdn[l], mode="nn", out_dtype=F32, add=h, name=f"ffn{l}_down", tk=2816)
        return out, (xn, u0, act)

    h2, ffn0_saved = ffn_fwd(h1, 0)
    wkv_t, wq, wo, wup_t[1], wdn[1] = gather_finish("rest", rest_w, h2)
    kvn, xnb = _rmsnorm_fwd(h2, jnp.stack([kv_norm, b_norm[0]]), "kv_b_norm_fwd")
    kv = _matmul(kvn, wkv_t, mode="nt", out_dtype=F32, name="kv_proj", tm=2048).reshape(B, S, 2 * D)
    q = _matmul(xnb, wq, mode="nn", out_dtype=F32, name="q_proj", tm=2048).reshape(B, S, D)

    tables = [_band_tables(w, d) for (w, d) in DILATED_BRANCHES]
    onehots = jnp.stack([t[0] for t in tables])
    P = ATT_BLOCK
    bias_vals = _bias_lookup(rel_bias.T, onehots, "rel_bias_lookup").reshape(3, N_HEADS, P, 2 * P)
    in_cur = (jnp.arange(2 * P) >= P)[None, :]
    masks = jnp.stack([jnp.stack([t[1] & in_cur, t[1]]) for t in tables])
    tabs = jnp.where(masks[:, :, None], bias_vals[:, None], NEG)

    attn3, lse3 = _attn_fwd(q, kv, tabs, "attn_fwd")
    attn = attn3.reshape(T, D)
    h3 = _matmul(attn, wo, mode="nn", out_dtype=F32, add=h2, name="o_proj")
    h4, ffn1_saved = ffn_fwd(h3, 1)

    dh4, dh4b, d_final_norm, sq = _final_loss_bwd(h4, final_norm.reshape(1, D), loss_target.reshape(T, D), "loss_bwd")
    loss_part = 0.5 * jnp.sum(sq) / D

    core_idx = lax.axis_index("c").astype(jnp.int32).reshape(1)
    chip_idx = (2 * lax.axis_index("x") + lax.axis_index("y")).astype(jnp.int32).reshape(1)
    in_flight = []

    def group_start(tag, indexed_grads):
        halves = [g.reshape(N_DEV // 2, 2, g.shape[0] // N_DEV, D) for _, g in indexed_grads]
        lands = [jax.ShapeDtypeStruct((h.shape[0],) + h.shape[2:], h.dtype) for h in halves]
        send, recv, srcs, lnd, token = _split_start(_pair_copies, halves, lands, (N_DEV // 2) * len(halves),
                                                    f"exchange_core_start_{tag}")
        return (tag, [i for i, _ in indexed_grads], send, recv, srcs, lnd), token

    def group_finish(handle, after):
        tag, idxs, send, recv, srcs, lnd = handle
        halves, theirs = _split_wait(_pair_copies, send, recv, srcs, lnd, after, f"exchange_core_wait_{tag}")
        sums = [_pair_sum(h_, t_, core_idx, f"grad_core_sum_{tag}{k}")
                for k, (h_, t_) in enumerate(zip(halves, theirs))]
        lands = [jax.ShapeDtypeStruct((3,) + s.shape[1:], s.dtype) for s in sums]
        send, recv, srcs, lnd, token = _split_start(_chip_copies, sums, lands, 3 * len(sums),
                                                    f"exchange_chip_start_{tag}")
        in_flight.append((tag, idxs, send, recv, srcs, lnd))
        return token

    def ffn_bwd(dh_out, dh_out_b, h_in, saved, l, after):
        xn, u0, act = saved
        dact = _matmul(dh_out_b, wdn[l], mode="nt", out_dtype=BF16, after=after, name=f"ffn{l}_down_dx",
                       tm=2048, tn=1408).reshape(B, S, F)
        d_wdn = _matmul(act, dh_out_b, mode="tn", out_dtype=F32, name=f"ffn{l}_down_dw", tm=1408, tn=1024, tk=2048)
        du0, d_conv, d_conv_b = _ffn_gate_bwd(u0, dact, ffn_conv_f[l], ffn_conv_b[l:l + 1], f"ffn{l}_gate_bwd")
        du0 = du0.reshape(2, T, F)
        dxn = _matmul(du0, wup_t[l], mode="nn", out_dtype=BF16, a_parts=2, name=f"ffn{l}_up_dx", tk=2816)
        d_wup_t = _matmul(du0, xn, mode="tn", out_dtype=F32, a_parts=2, name=f"ffn{l}_up_dw", tm=1408, tn=1024, tk=2048)
        dh_in, dh_in_b, d_norm = _rmsnorm_bwd(h_in, ffn_norm[l:l + 1], [dxn], dh_out, f"ffn{l}_norm_bwd")
        return (dh_in, dh_in_b), (d_wdn, d_wup_t, d_conv, d_conv_b, d_norm)

    (dh3, dh3b), ffn1_grads = ffn_bwd(dh4, dh4b, h3, ffn1_saved, 1, None)
    group, token = group_start("ffn1", [(6, ffn1_grads[1]), (8, ffn1_grads[0])])

    dattn = _matmul(dh3b, wo, mode="nt", out_dtype=F32, after=token, name="o_proj_dx", tm=2048)
    token = group_finish(group, dattn)
    dattn = dattn.reshape(B, S, D)
    d_wo = _matmul(attn, dh3b, mode="tn", out_dtype=F32, after=token, name="o_proj_dw", tm=1024, tn=1024, tk=2048)
    dq, dkv, dtab = _attn_bwd(q, kv, dattn, attn3, lse3, tabs, "attn_bwd")
    dtabs = dtab.transpose(1, 0, 2, 3, 4).reshape(3, N_HEADS, P * 2 * P)
    d_rel_bias = _bias_grad(dtabs, onehots, "rel_bias_grad").T
    dq2 = dq.reshape(T, D)
    dkv2 = dkv.reshape(2, T, D)
    d_wq = _matmul(xnb, dq2, mode="tn", out_dtype=F32, name="q_proj_dw", tm=1024, tn=1024, tk=2048)
    d_wkv_t = _matmul(dkv2, kvn, mode="tn", out_dtype=F32, a_parts=2, name="kv_proj_dw", tm=1024, tn=1024, tk=2048)
    group, token = group_start("attn", [(2, d_wkv_t), (3, d_wq), (4, d_wo)])
    dxnb = _matmul(dq2, wq, mode="nt", out_dtype=BF16, after=token, name="q_proj_dx")
    dkvn = _matmul(dkv2, wkv_t, mode="nn", out_dtype=BF16, a_parts=2, name="kv_proj_dx")
    dh2, dh2b, d_kvb_norm = _rmsnorm_bwd(h2, jnp.stack([kv_norm, b_norm[0]]), [dkvn, dxnb], dh3, "kv_b_norm_bwd")
    token = group_finish(group, dh2b)

    (dh1, dh1b), ffn0_grads = ffn_bwd(dh2, dh2b, h1, ffn0_saved, 0, token)
    group, token = group_start("ffn0", [(5, ffn0_grads[1]), (7, ffn0_grads[0])])

    dgated = _matmul(dh1b, wout, mode="nt", out_dtype=BF16, after=token, name="a_out_proj_dx", tm=2048)
    token = group_finish(group, dgated)
    dgated = dgated.reshape(B, S, D)
    d_wout = _matmul(gated, dh1b, mode="tn", out_dtype=F32, after=token, name="a_out_proj_dw",
                     tm=1024, tn=1024, tk=2048)
    dbch, d_a_conv = _shortconv_bwd(bch, dgated, a_conv_f, "a_gate_bwd")
    dbch = dbch.reshape(3, T, D)
    dxn0 = _matmul(dbch, win_t, mode="nn", out_dtype=BF16, a_parts=3, name="a_in_proj_dx", tm=2048)
    d_win_t = _matmul(dbch, xn0, mode="tn", out_dtype=F32, a_parts=3, name="a_in_proj_dw", tm=1024, tn=1024, tk=2048)
    group, token = group_start("a", [(0, d_win_t), (1, d_wout)])
    grad_x, _, d_a_norm = _rmsnorm_bwd(x2, a_norm_f + token[0, 0], [dxn0], dh1, "a_norm_bwd")
    token = group_finish(group, grad_x)

    small_full = [d_a_norm, d_a_conv, jnp.stack([ffn0_grads[2], ffn1_grads[2]]),
                  d_kvb_norm[0], d_kvb_norm[1], d_rel_bias, jnp.concatenate([ffn0_grads[4], ffn1_grads[4]]),
                  jnp.concatenate([ffn0_grads[3], ffn1_grads[3]]), d_final_norm, loss_part]
    small_full_shapes = [(1, D), (3, D), (2, 3, 2 * F), (D,), (1, D), rel_bias.shape, (2, D), (2, 2 * F), (D,), ()]
    _, gath = _core_pair_exchange([], [_pack_rows(small_full)], "exchange_small")
    chip_sums, recv = {}, {}
    for tag, idxs, send, rcv, srcs, lnd in in_flight:
        srcs, lnd = _split_wait(_chip_copies, send, rcv, srcs, lnd, token, f"exchange_chip_wait_{tag}")
        for i, s_, l_ in zip(idxs, srcs, lnd):
            chip_sums[i], recv[i] = s_, l_
    small_sum = _sum_slots(gath[0], "small_grad_sum").reshape(-1)
    (g_a_norm_f, g_a_conv_f, g_ffn_conv_f, g_kv_norm, g_b_norm, g_rel_bias, g_ffn_norm, g_ffn_conv_b,
     g_final_norm, loss) = _unpack(small_sum, small_full_shapes)

    def my_cols(full, width):
        return lax.dynamic_slice_in_dim(full, me * width, width, axis=full.ndim - 1)

    g_a_norm = my_cols(g_a_norm_f, D // N_DEV)
    g_a_conv = my_cols(g_a_conv_f, D // N_DEV)[None]
    g_ffn_conv = my_cols(g_ffn_conv_f, 2 * F // N_DEV)

    big_w = [(a_w_in, m_a_w_in, v_a_w_in, True), (a_w_out, m_a_w_out, v_a_w_out, False),
             (w_kv, m_w_kv, v_w_kv, True), (b_w_q, m_b_w_q, v_b_w_q, False), (b_w_o, m_b_w_o, v_b_w_o, False),
             (ffn_w_up[0], m_ffn_w_up[0], v_ffn_w_up[0], True), (ffn_w_up[1], m_ffn_w_up[1], v_ffn_w_up[1], True),
             (ffn_w_down[0], m_ffn_w_down[0], v_ffn_w_down[0], False),
             (ffn_w_down[1], m_ffn_w_down[1], v_ffn_w_down[1], False)]
    big_out = []
    for i, (w, m, v, transposed) in enumerate(big_w):
        if transposed:
            g = _chip_sum(chip_sums[i], recv[i], chip_idx, f"grad_sum_{i}").T
            w2, m2, v2 = (t.reshape(g.shape) for t in (w, m, v))
            delta, nm, nv = _adamw(g, w2, m2, v2, f"adamw_{i}")
        else:
            w2, m2, v2 = (t.reshape(chip_sums[i].shape[1:]) for t in (w, m, v))
            g, delta, nm, nv = _chip_sum_adamw(chip_sums[i], recv[i], chip_idx, w2, m2, v2, f"grad_sum_adamw_{i}")
        big_out.append(tuple(t.reshape(w.shape) for t in (g, delta, nm, nv)))

    def pair(i, j):
        return tuple(jnp.stack([big_out[i][t], big_out[j][t]]) for t in range(4))

    small_w = [(a_norm, m_a_norm, v_a_norm, g_a_norm), (a_conv, m_a_conv, v_a_conv, g_a_conv),
               (ffn_conv, m_ffn_conv, v_ffn_conv, g_ffn_conv), (kv_norm, m_kv_norm, v_kv_norm, g_kv_norm),
               (b_norm, m_b_norm, v_b_norm, g_b_norm), (rel_bias, m_rel_bias, v_rel_bias, g_rel_bias),
               (ffn_norm, m_ffn_norm, v_ffn_norm, g_ffn_norm),
               (ffn_conv_b, m_ffn_conv_b, v_ffn_conv_b, g_ffn_conv_b),
               (final_norm, m_final_norm, v_final_norm, g_final_norm)]
    small_out = _adamw_small(small_w, "adamw_small")

    per_weight = [small_out[0], big_out[0], small_out[1], big_out[1], small_out[3], big_out[2], small_out[4],
                  big_out[3], big_out[4], small_out[5], small_out[6], pair(5, 6), small_out[2], small_out[7],
                  pair(7, 8), small_out[8]]
    outs = [loss, grad_x.reshape(B, S, D)]
    for t in range(4):
        outs.extend(pw[t] for pw in per_weight)
    return tuple(outs)
```

```python
import functools
import math

import jax
import jax.numpy as jnp
from jax import lax
from jax.experimental import pallas as pl
from jax.experimental.pallas import tpu as pltpu

F32 = jnp.float32
BF16 = jnp.bfloat16

N_DEV = 8
N_HEADS = 16
ATT_BLOCK = 128
ATT_UNROLL = 8
DILATED_BRANCHES = ((128, 1), (512, 4), (2048, 16))
REL_BUCKETS = 32
REL_MAX_DISTANCE = 2048
RMS_EPS = 1e-6
ADAM_LR = 0.001
ADAM_B1 = 0.9
ADAM_B2 = 0.999
ADAM_EPS = 1e-08
ADAM_WD = 0.01
ADAM_STEP = 10

LANE = 128
HALO = 16
NEG = -1e30
VMEM_CAP = 56 << 20


def _vmem(block_bytes):
    return int(min(VMEM_CAP, max(32 << 20, 3 * block_bytes + (8 << 20))))


def _nbytes(shape, dtype):
    return math.prod(shape) * jnp.dtype(dtype).itemsize


def _tile(dim, target):
    best = None
    t = LANE
    while t <= min(dim, target):
        if dim % t == 0:
            best = t
        t += LANE
    return best if best is not None else dim


def _matmul(a, b, *, mode, out_dtype, name, add=None, after=None, a_parts=1, tm=1024, tn=1024, tk=1024):
    P = a_parts
    if mode == "nn":
        M, K = (a.shape[0], a.shape[1]) if P == 1 else (a.shape[1], a.shape[2] * P)
        N = b.shape[1]
    elif mode == "nt":
        assert P == 1
        M, K = a.shape
        N = b.shape[0]
    else:
        K = a.shape[0] if P == 1 else a.shape[1]
        M = a.shape[1] if P == 1 else a.shape[2] * P
        N = b.shape[1]
    tm = _tile(M // P if mode == "tn" else M, tm)
    tn = _tile(N, tn)
    tk = _tile(K // P if mode == "nn" else K, tk)
    nm, nn_, nk = M // tm, N // tn, K // tk

    if mode == "nn":
        if P == 1:
            a_spec = pl.BlockSpec((tm, tk), lambda i, j, k: (i, k))
        else:
            nkp = nk // P
            a_spec = pl.BlockSpec((None, tm, tk), lambda i, j, k: (k // nkp, i, k % nkp))
        b_spec = pl.BlockSpec((tk, tn), lambda i, j, k: (k, j))
        dims = (((1,), (0,)), ((), ()))
    elif mode == "nt":
        a_spec = pl.BlockSpec((tm, tk), lambda i, j, k: (i, k))
        b_spec = pl.BlockSpec((tn, tk), lambda i, j, k: (j, k))
        dims = (((1,), (1,)), ((), ()))
    else:
        if P == 1:
            a_spec = pl.BlockSpec((tk, tm), lambda i, j, k: (k, i))
        else:
            nmp = nm // P
            a_spec = pl.BlockSpec((None, tk, tm), lambda i, j, k: (i // nmp, k, i % nmp))
        b_spec = pl.BlockSpec((tk, tn), lambda i, j, k: (k, j))
        dims = (((0,), (0,)), ((), ()))
    o_spec = pl.BlockSpec((tm, tn), lambda i, j, k: (i, j))
    in_specs = [a_spec, b_spec]
    args = [a, b]
    if add is not None:
        in_specs.append(o_spec)
        args.append(add)
    has_add = add is not None
    if after is not None:
        in_specs.append(pl.BlockSpec(memory_space=pl.ANY))
        args.append(after)
    n_extra = len(args) - 2

    direct = nk > 1 and out_dtype == F32 and not has_add

    def body(a_ref, b_ref, *rest):
        add_ref = rest[0] if has_add else None
        o_ref = rest[n_extra]
        acc_ref = o_ref if direct or nk == 1 else rest[-1]
        k = pl.program_id(2)

        def part():
            return lax.dot_general(a_ref[...].astype(BF16), b_ref[...].astype(BF16), dims,
                                   preferred_element_type=F32)

        def finish(r):
            if has_add:
                r = r + add_ref[...].astype(F32)
            o_ref[...] = r.astype(out_dtype)

        if nk == 1:
            finish(part())
        else:
            @pl.when(k == 0)
            def _():
                acc_ref[...] = part()

            @pl.when(jnp.logical_and(k > 0, jnp.logical_or(k < nk - 1, direct)))
            def _():
                acc_ref[...] += part()

            if not direct:
                @pl.when(k == nk - 1)
                def _():
                    finish(acc_ref[...] + part())

    blk = (_nbytes((tm, tk), a.dtype) + _nbytes((tk, tn), b.dtype) + _nbytes((tm, tn), out_dtype)
           + (_nbytes((tm, tn), add.dtype) if has_add else 0)) * 2 + 3 * _nbytes((tm, tn), F32)
    return pl.pallas_call(
        body, name=name, grid=(nm, nn_, nk),
        in_specs=in_specs, out_specs=o_spec,
        out_shape=jax.ShapeDtypeStruct((M, N), out_dtype),
        scratch_shapes=[] if direct or nk == 1 else [pltpu.VMEM((tm, tn), F32)],
        compiler_params=pltpu.CompilerParams(
            dimension_semantics=("parallel", "parallel", "arbitrary"), vmem_limit_bytes=_vmem(blk)),
    )(*args)


def _rmsnorm_fwd(x, gains, name, tm=1024):
    T, D = x.shape
    n = gains.shape[0]
    tm = _tile(T, tm)

    def body(x_ref, g_ref, *o_refs):
        xv = x_ref[...]
        xhat = xv * lax.rsqrt(jnp.mean(xv * xv, axis=-1, keepdims=True) + RMS_EPS)
        for i in range(n):
            o_refs[i][...] = (xhat * g_ref[i:i + 1, :]).astype(BF16)

    row = pl.BlockSpec((tm, D), lambda i: (i, 0))
    outs = pl.pallas_call(
        body, name=name, grid=(T // tm,),
        in_specs=[row, pl.BlockSpec((n, D), lambda i: (0, 0))],
        out_specs=[row] * n,
        out_shape=[jax.ShapeDtypeStruct((T, D), BF16)] * n,
        compiler_params=pltpu.CompilerParams(
            dimension_semantics=("parallel",), vmem_limit_bytes=_vmem(4 * _nbytes((tm, D), F32))),
    )(x, gains)
    return tuple(outs)


def _rmsnorm_bwd(x, gains, dxns, dres, name, tm=1024):
    T, D = x.shape
    n = gains.shape[0]
    tm = _tile(T, tm)

    def body(x_ref, g_ref, *rest):
        dxn_refs = rest[:n]
        dres_ref, dx_ref, dxb_ref, dg_ref = rest[n:]
        xv = x_ref[...]
        rstd = lax.rsqrt(jnp.mean(xv * xv, axis=-1, keepdims=True) + RMS_EPS)
        xhat = xv * rstd
        dx = dres_ref[...]

        @pl.when(pl.program_id(0) == 0)
        def _():
            dg_ref[...] = jnp.zeros_like(dg_ref)

        for i in range(n):
            dy = dxn_refs[i][...].astype(F32)
            dg_ref[i:i + 1, :] += jnp.sum(dy * xhat, axis=0, keepdims=True)
            dxh = dy * g_ref[i:i + 1, :]
            dx = dx + rstd * (dxh - xhat * jnp.mean(dxh * xhat, axis=-1, keepdims=True))
        dx_ref[...] = dx
        dxb_ref[...] = dx.astype(BF16)

    row = pl.BlockSpec((tm, D), lambda i: (i, 0))
    par = pl.BlockSpec((n, D), lambda i: (0, 0))
    return pl.pallas_call(
        body, name=name, grid=(T // tm,),
        in_specs=[row, par] + [row] * n + [row],
        out_specs=[row, row, par],
        out_shape=[jax.ShapeDtypeStruct((T, D), F32), jax.ShapeDtypeStruct((T, D), BF16),
                   jax.ShapeDtypeStruct((n, D), F32)],
        compiler_params=pltpu.CompilerParams(
            dimension_semantics=("arbitrary",), vmem_limit_bytes=_vmem((4 + n) * _nbytes((tm, D), F32))),
    )(x, gains, *dxns, dres)


def _final_loss_bwd(h, gain, target, name, tm=1024):
    T, D = h.shape
    tm = _tile(T, tm)

    def body(h_ref, g_ref, t_ref, dh_ref, dhb_ref, dg_ref, sq_ref):
        xv = h_ref[...]
        rstd = lax.rsqrt(jnp.mean(xv * xv, axis=-1, keepdims=True) + RMS_EPS)
        xhat = xv * rstd
        err = xhat * g_ref[...] - t_ref[...]

        @pl.when(pl.program_id(0) == 0)
        def _():
            dg_ref[...] = jnp.zeros_like(dg_ref)
            sq_ref[...] = jnp.zeros_like(sq_ref)

        sq_ref[...] += jnp.sum(err * err, axis=0, keepdims=True)
        dy = err * (1.0 / D)
        dg_ref[...] += jnp.sum(dy * xhat, axis=0, keepdims=True)
        dxh = dy * g_ref[...]
        dh = rstd * (dxh - xhat * jnp.mean(dxh * xhat, axis=-1, keepdims=True))
        dh_ref[...] = dh
        dhb_ref[...] = dh.astype(BF16)

    row = pl.BlockSpec((tm, D), lambda i: (i, 0))
    par = pl.BlockSpec((1, D), lambda i: (0, 0))
    return pl.pallas_call(
        body, name=name, grid=(T // tm,),
        in_specs=[row, par, row], out_specs=[row, row, par, par],
        out_shape=[jax.ShapeDtypeStruct((T, D), F32), jax.ShapeDtypeStruct((T, D), BF16),
                   jax.ShapeDtypeStruct((1, D), F32), jax.ShapeDtypeStruct((1, D), F32)],
        compiler_params=pltpu.CompilerParams(
            dimension_semantics=("arbitrary",), vmem_limit_bytes=_vmem(5 * _nbytes((tm, D), F32))),
    )(h, gain, target)


def _halo_specs(S, ts, tc, col_off, prev, nxt):
    r = ts // HALO
    last = S // HALO - 1
    specs = []
    if prev:
        specs.append(pl.BlockSpec((None, HALO, tc), lambda j, b, s: (b, jnp.maximum(s * r - 1, 0), col_off + j)))
    specs.append(pl.BlockSpec((None, ts, tc), lambda j, b, s: (b, s, col_off + j)))
    if nxt:
        specs.append(pl.BlockSpec((None, HALO, tc), lambda j, b, s: (b, jnp.minimum((s + 1) * r, last), col_off + j)))
    return specs


def _ext(prev_ref, main_ref, next_ref, first, last):
    main = main_ref[...].astype(F32)
    zeros = jnp.zeros((HALO, main.shape[1]), F32)
    top = zeros if prev_ref is None else jnp.where(first, 0.0, prev_ref[...].astype(F32))
    bot = zeros if next_ref is None else jnp.where(last, 0.0, next_ref[...].astype(F32))
    return jnp.concatenate([top, main, bot], axis=0)


def _shift(xe, k):
    return pltpu.roll(xe, k % xe.shape[0], axis=0)


def _cconv(xe, w):
    return w[0:1, :] * _shift(xe, 2) + w[1:2, :] * _shift(xe, 1) + w[2:3, :] * xe


def _main(xe, ts):
    return xe[HALO:HALO + ts, :]


def _cconv_t(de, w):
    return w[2:3, :] * de + w[1:2, :] * _shift(de, -1) + w[0:1, :] * _shift(de, -2)


def _conv_wgrad(dw_ref, de, xe, ts):
    d = _main(de, ts)
    dw_ref[0:1, :] += jnp.sum(d * _main(_shift(xe, 2), ts), axis=0, keepdims=True)
    dw_ref[1:2, :] += jnp.sum(d * _main(_shift(xe, 1), ts), axis=0, keepdims=True)
    dw_ref[2:3, :] += jnp.sum(d * _main(xe, ts), axis=0, keepdims=True)


def _gate_tiles(S, C, rows=1024):
    return _tile(S, rows), _tile(C, 512)


def _shortconv_fwd(bch, conv_w, name):
    B, S, D3 = bch.shape
    D = D3 // 3
    ts, tc = _gate_tiles(S, D, rows=2048)
    nj, ns = D // tc, S // ts

    def body(b_ref, cp_ref, c_ref, hp_ref, h_ref, w_ref, o_ref):
        first = pl.program_id(2) == 0
        ce = _ext(cp_ref, c_ref, None, first, False)
        he = _ext(hp_ref, h_ref, None, first, False)
        cv = _main(_cconv(ce * he, w_ref[...]), ts)
        o_ref[...] = (b_ref[...].astype(F32) * cv).astype(BF16)

    in_specs = (_halo_specs(S, ts, tc, 0, False, False) + _halo_specs(S, ts, tc, nj, True, False)
                + _halo_specs(S, ts, tc, 2 * nj, True, False) + [pl.BlockSpec((3, tc), lambda j, b, s: (0, j))])
    return pl.pallas_call(
        body, name=name, grid=(nj, B, ns), in_specs=in_specs,
        out_specs=pl.BlockSpec((None, ts, tc), lambda j, b, s: (b, s, j)),
        out_shape=jax.ShapeDtypeStruct((B, S, D), BF16),
        compiler_params=pltpu.CompilerParams(
            dimension_semantics=("parallel", "parallel", "parallel"),
            vmem_limit_bytes=_vmem(12 * _nbytes((ts + 2 * HALO, tc), F32))),
    )(bch, bch, bch, bch, bch, conv_w)


def _shortconv_bwd(bch, dg, conv_w, name):
    B, S, D3 = bch.shape
    D = D3 // 3
    ts, tc = _gate_tiles(S, D)
    nj, ns = D // tc, S // ts

    def body(b_ref, bn_ref, cp_ref, c_ref, cn_ref, hp_ref, h_ref, hn_ref, d_ref, dn_ref, w_ref, o_ref, dw_ref):
        s = pl.program_id(2)
        first, last = s == 0, s == ns - 1

        @pl.when(jnp.logical_and(pl.program_id(1) == 0, s == 0))
        def _():
            dw_ref[...] = jnp.zeros_like(dw_ref)

        w = w_ref[...]
        be = _ext(None, b_ref, bn_ref, first, last)
        ce = _ext(cp_ref, c_ref, cn_ref, first, last)
        he = _ext(hp_ref, h_ref, hn_ref, first, last)
        de = _ext(None, d_ref, dn_ref, first, last)
        ch = ce * he
        dcv = de * be
        dch = _main(_cconv_t(dcv, w), ts)
        o_ref[0] = (_main(de, ts) * _main(_cconv(ch, w), ts)).astype(BF16)
        o_ref[1] = (dch * _main(he, ts)).astype(BF16)
        o_ref[2] = (dch * _main(ce, ts)).astype(BF16)
        _conv_wgrad(dw_ref, dcv, ch, ts)

    in_specs = (_halo_specs(S, ts, tc, 0, False, True) + _halo_specs(S, ts, tc, nj, True, True)
                + _halo_specs(S, ts, tc, 2 * nj, True, True) + _halo_specs(S, ts, tc, 0, False, True)
                + [pl.BlockSpec((3, tc), lambda j, b, s: (0, j))])
    return pl.pallas_call(
        body, name=name, grid=(nj, B, ns), in_specs=in_specs,
        out_specs=[pl.BlockSpec((3, None, ts, tc), lambda j, b, s: (0, b, s, j)),
                   pl.BlockSpec((3, tc), lambda j, b, s: (0, j))],
        out_shape=[jax.ShapeDtypeStruct((3, B, S, D), BF16), jax.ShapeDtypeStruct((3, D), F32)],
        compiler_params=pltpu.CompilerParams(
            dimension_semantics=("parallel", "arbitrary", "arbitrary"),
            vmem_limit_bytes=_vmem(24 * _nbytes((ts + 2 * HALO, tc), F32))),
    )(bch, bch, bch, bch, bch, bch, bch, bch, dg, dg, conv_w)


def _sigmoid(x):
    return 1.0 / (1.0 + jnp.exp(-x))


def _ffn_gate_fwd(u0, conv_w, conv_b, name):
    B, S, F2 = u0.shape
    F = F2 // 2
    ts, tc = _gate_tiles(S, F, rows=2048)
    nj, ns = F // tc, S // ts

    def body(gp_ref, g_ref, up_ref, u_ref, wg_ref, wu_ref, bg_ref, bu_ref, o_ref):
        first = pl.program_id(2) == 0
        ug = _main(_cconv(_ext(gp_ref, g_ref, None, first, False), wg_ref[...]), ts) + bg_ref[...]
        uu = _main(_cconv(_ext(up_ref, u_ref, None, first, False), wu_ref[...]), ts) + bu_ref[...]
        o_ref[...] = (ug * _sigmoid(ug) * uu).astype(BF16)

    in_specs = (_halo_specs(S, ts, tc, 0, True, False) + _halo_specs(S, ts, tc, nj, True, False)
                + [pl.BlockSpec((3, tc), lambda j, b, s: (0, j)), pl.BlockSpec((3, tc), lambda j, b, s: (0, nj + j)),
                   pl.BlockSpec((1, tc), lambda j, b, s: (0, j)), pl.BlockSpec((1, tc), lambda j, b, s: (0, nj + j))])
    return pl.pallas_call(
        body, name=name, grid=(nj, B, ns), in_specs=in_specs,
        out_specs=pl.BlockSpec((None, ts, tc), lambda j, b, s: (b, s, j)),
        out_shape=jax.ShapeDtypeStruct((B, S, F), BF16),
        compiler_params=pltpu.CompilerParams(
            dimension_semantics=("parallel", "parallel", "parallel"),
            vmem_limit_bytes=_vmem(12 * _nbytes((ts + 2 * HALO, tc), F32))),
    )(u0, u0, u0, u0, conv_w, conv_w, conv_b, conv_b)


def _ffn_gate_bwd(u0, dact, conv_w, conv_b, name):
    B, S, F2 = u0.shape
    F = F2 // 2
    ts, tc = _gate_tiles(S, F)
    nj, ns = F // tc, S // ts

    def body(gp_ref, g_ref, gn_ref, up_ref, u_ref, un_ref, d_ref, dn_ref, wg_ref, wu_ref, bg_ref, bu_ref,
             o_ref, dwg_ref, dwu_ref, dbg_ref, dbu_ref):
        s = pl.program_id(2)
        first, last = s == 0, s == ns - 1

        @pl.when(jnp.logical_and(pl.program_id(1) == 0, s == 0))
        def _():
            for r in (dwg_ref, dwu_ref, dbg_ref, dbu_ref):
                r[...] = jnp.zeros_like(r)

        wg, wu = wg_ref[...], wu_ref[...]
        ge = _ext(gp_ref, g_ref, gn_ref, first, last)
        ue = _ext(up_ref, u_ref, un_ref, first, last)
        de = _ext(None, d_ref, dn_ref, first, last)
        ug = _cconv(ge, wg) + bg_ref[...]
        uu = _cconv(ue, wu) + bu_ref[...]
        sig = _sigmoid(ug)
        dug = de * uu * (sig * (1.0 + ug * (1.0 - sig)))
        duu = de * (ug * sig)
        o_ref[0] = _main(_cconv_t(dug, wg), ts).astype(BF16)
        o_ref[1] = _main(_cconv_t(duu, wu), ts).astype(BF16)
        _conv_wgrad(dwg_ref, dug, ge, ts)
        _conv_wgrad(dwu_ref, duu, ue, ts)
        dbg_ref[...] += jnp.sum(_main(dug, ts), axis=0, keepdims=True)
        dbu_ref[...] += jnp.sum(_main(duu, ts), axis=0, keepdims=True)

    w3 = lambda off: pl.BlockSpec((3, tc), lambda j, b, s: (0, off + j))
    w1 = lambda off: pl.BlockSpec((1, tc), lambda j, b, s: (0, off + j))
    in_specs = (_halo_specs(S, ts, tc, 0, True, True) + _halo_specs(S, ts, tc, nj, True, True)
                + _halo_specs(S, ts, tc, 0, False, True) + [w3(0), w3(nj), w1(0), w1(nj)])
    outs = pl.pallas_call(
        body, name=name, grid=(nj, B, ns), in_specs=in_specs,
        out_specs=[pl.BlockSpec((2, None, ts, tc), lambda j, b, s: (0, b, s, j)), w3(0), w3(0), w1(0), w1(0)],
        out_shape=[jax.ShapeDtypeStruct((2, B, S, F), BF16), jax.ShapeDtypeStruct((3, F), F32),
                   jax.ShapeDtypeStruct((3, F), F32), jax.ShapeDtypeStruct((1, F), F32),
                   jax.ShapeDtypeStruct((1, F), F32)],
        compiler_params=pltpu.CompilerParams(
            dimension_semantics=("parallel", "arbitrary", "arbitrary"),
            vmem_limit_bytes=_vmem(30 * _nbytes((ts + 2 * HALO, tc), F32))),
    )(u0, u0, u0, u0, u0, u0, dact, dact, conv_w, conv_w, conv_b, conv_b)
    du0, dwg, dwu, dbg, dbu = outs
    return du0, jnp.concatenate([dwg, dwu], axis=1), jnp.concatenate([dbg, dbu], axis=1)


def _t5_bucket(dist):
    max_exact = REL_BUCKETS // 2
    n = jnp.maximum(dist, 0)
    nf = jnp.maximum(n, max_exact).astype(F32)
    large = max_exact + (jnp.log(nf / max_exact) / math.log(REL_MAX_DISTANCE / max_exact)
                         * (REL_BUCKETS - max_exact)).astype(jnp.int32)
    large = jnp.minimum(large, REL_BUCKETS - 1)
    return jnp.where(n < max_exact, n, large)


def _band_tables(window, dilation):
    P = ATT_BLOCK
    qi = jnp.arange(P, dtype=jnp.int32)[:, None]
    kc = jnp.arange(2 * P, dtype=jnp.int32)[None, :]
    delta = qi + P - kc
    band = (delta >= 0) & (delta <= window // dilation)
    bucket = _t5_bucket(delta * dilation).reshape(-1)
    onehot = (bucket[None, :] == jnp.arange(REL_BUCKETS, dtype=jnp.int32)[:, None]).astype(F32)
    return onehot, band


def _bias_lookup(rel_bias_t, onehots, name):
    nb, _, Q = onehots.shape
    H = rel_bias_t.shape[0]

    def body(r_ref, oh_ref, o_ref):
        o_ref[...] = lax.dot_general(r_ref[...], oh_ref[...], (((1,), (0,)), ((), ())),
                                     precision=lax.Precision.HIGHEST, preferred_element_type=F32)

    return pl.pallas_call(
        body, name=name, grid=(nb,),
        in_specs=[pl.BlockSpec((H, REL_BUCKETS), lambda i: (0, 0)),
                  pl.BlockSpec((None, REL_BUCKETS, Q), lambda i: (i, 0, 0))],
        out_specs=pl.BlockSpec((None, H, Q), lambda i: (i, 0, 0)),
        out_shape=jax.ShapeDtypeStruct((nb, H, Q), F32),
        compiler_params=pltpu.CompilerParams(dimension_semantics=("parallel",),
                                             vmem_limit_bytes=_vmem(_nbytes((REL_BUCKETS + H, Q), F32))),
    )(rel_bias_t, onehots)


def _bias_grad(dtabs, onehots, name):
    nb, H, Q = dtabs.shape

    def body(d_ref, oh_ref, o_ref):
        @pl.when(pl.program_id(0) == 0)
        def _():
            o_ref[...] = jnp.zeros_like(o_ref)

        o_ref[...] += lax.dot_general(d_ref[...], oh_ref[...], (((1,), (1,)), ((), ())),
                                      precision=lax.Precision.HIGHEST, preferred_element_type=F32)

    return pl.pallas_call(
        body, name=name, grid=(nb,),
        in_specs=[pl.BlockSpec((None, H, Q), lambda i: (i, 0, 0)),
                  pl.BlockSpec((None, REL_BUCKETS, Q), lambda i: (i, 0, 0))],
        out_specs=pl.BlockSpec((H, REL_BUCKETS), lambda i: (0, 0)),
        out_shape=jax.ShapeDtypeStruct((H, REL_BUCKETS), F32),
        compiler_params=pltpu.CompilerParams(dimension_semantics=("arbitrary",),
                                             vmem_limit_bytes=_vmem(_nbytes((REL_BUCKETS + H, Q), F32))),
    )(dtabs, onehots)


def _lane_masks():
    lane = lax.broadcasted_iota(jnp.int32, (1, LANE), 1)
    lo = lane < LANE // 2
    return lo, jnp.logical_not(lo)


def _dot_nt(a, b):
    return lax.dot_general(a, b, (((1,), (1,)), ((), ())), preferred_element_type=F32)


def _dot_nn(a, b):
    return lax.dot_general(a, b, (((1,), (0,)), ((), ())), preferred_element_type=F32)


def _dot_tn(a, b):
    return lax.dot_general(a, b, (((0,), (0,)), ((), ())), preferred_element_type=F32)


def _block_rows(n, dilation, S):
    P = ATT_BLOCK
    nb = S // (dilation * P)
    r, i = n // nb, n % nb
    cur = pl.ds(i * (P * dilation) + r, P, stride=dilation)
    prv = pl.ds(jnp.maximum(i - 1, 0) * (P * dilation) + r, P, stride=dilation)
    return cur, prv, jnp.minimum(i, 1)


def _attn_fwd(q, kv, tabs, name):
    B, S, D = q.shape
    P, H = ATT_BLOCK, N_HEADS
    scale = (D // H) ** -0.5
    assert math.log2(scale).is_integer(), "q is scaled before its bf16 rounding: exact only for a power of two"
    half = LANE // 2
    nsteps = S // P // ATT_UNROLL

    def body(q_ref, k_ref, v_ref, tab_ref, o_ref, lse_ref, acc_ref, m_ref, l_ref):
        lo, hi = _lane_masks()
        for bi, (_, d) in enumerate(DILATED_BRANCHES):
            def block(n, carry, bi=bi, d=d):
                rows = [_block_rows(n + j * nsteps, d, S) for j in range(ATT_UNROLL)]
                loaded = []
                for cur, prv, variant in rows:
                    qb = (q_ref[cur, :] * scale).astype(BF16)
                    kw = jnp.concatenate([k_ref[prv, :], k_ref[cur, :]], axis=0).astype(BF16)
                    vw = jnp.concatenate([v_ref[prv, :], v_ref[cur, :]], axis=0).astype(BF16)
                    loaded.append((qb, kw, vw))
                results = []
                for (cur, prv, variant), (qb, kw, vw) in zip(rows, loaded):
                    m_pair = l_pair = a_pair = None
                    for e, msk in enumerate((lo, hi)):
                        s = _dot_nt(jnp.where(msk, qb, 0), kw) + tab_ref[bi, variant, e]
                        m_e = jnp.max(s, axis=-1, keepdims=True)
                        pe = jnp.exp(s - m_e)
                        l_e = jnp.sum(pe, axis=-1, keepdims=True)
                        a_e = _dot_nn(pe.astype(BF16), jnp.where(msk, vw, 0))
                        m_pair = m_e if e == 0 else jnp.where(lo, m_pair, m_e)
                        l_pair = l_e if e == 0 else jnp.where(lo, l_pair, l_e)
                        a_pair = a_e if e == 0 else a_pair + a_e
                    results.append((m_pair, l_pair, a_pair))
                for (cur, prv, variant), (m_pair, l_pair, a_pair) in zip(rows, results):
                    m_ref[bi, cur, :] = m_pair
                    l_ref[bi, cur, :] = l_pair
                    acc_ref[bi, cur, :] = a_pair
                return carry

            lax.fori_loop(0, nsteps, block, 0)

        nbr = len(DILATED_BRANCHES)
        chunk = 512

        def merge(t, carry):
            rs = pl.ds(pl.multiple_of(t * chunk, chunk), chunk)
            ms = [m_ref[b, rs, :] for b in range(nbr)]
            m = functools.reduce(jnp.maximum, ms)
            ws = [jnp.exp(mb - m) for mb in ms]
            l = functools.reduce(jnp.add, [w * l_ref[b, rs, :] for b, w in enumerate(ws)])
            acc = functools.reduce(jnp.add, [w * acc_ref[b, rs, :] for b, w in enumerate(ws)])
            o_ref[rs, :] = (acc / l).astype(BF16)
            lse_ref[rs, :] = m + jnp.log(l)
            return carry

        lax.fori_loop(0, S // chunk, merge, 0)

    nl = D // LANE
    col = lambda off: pl.BlockSpec((None, S, LANE), lambda b, p: (b, 0, off + p))
    tab_spec = pl.BlockSpec((3, 2, 2, P, 2 * P), lambda b, p: (0, 0, p, 0, 0))
    return pl.pallas_call(
        body, name=name, grid=(B, H // 2), in_specs=[col(0), col(0), col(nl), tab_spec],
        out_specs=[col(0), col(0)],
        out_shape=[jax.ShapeDtypeStruct((B, S, D), BF16), jax.ShapeDtypeStruct((B, S, D), F32)],
        scratch_shapes=[pltpu.VMEM((len(DILATED_BRANCHES), S, LANE), F32)] * 3,
        compiler_params=pltpu.CompilerParams(
            dimension_semantics=("parallel", "parallel"), vmem_limit_bytes=VMEM_CAP),
    )(q, kv, kv, tabs)


def _attn_bwd(q, kv, do, o, lse, tabs, name):
    B, S, D = q.shape
    P, H = ATT_BLOCK, N_HEADS
    scale = (D // H) ** -0.5
    assert math.log2(scale).is_integer(), "q is scaled before its bf16 rounding: exact only for a power of two"
    half = LANE // 2
    nsteps = S // P // ATT_UNROLL

    def body(q_ref, k_ref, v_ref, do_ref, o_ref, lse_ref, tab_ref, dq_out, dkv_out, dtab_ref, delta_ref,
             dq_ref, dkv_ref):
        lo, hi = _lane_masks()

        @pl.when(pl.program_id(1) == 0)
        def _():
            dtab_ref[...] = jnp.zeros_like(dtab_ref)

        dkv_ref[...] = jnp.zeros_like(dkv_ref)
        prod = do_ref[...] * o_ref[...].astype(F32)
        delta_ref[...] = jnp.where(lo, jnp.sum(jnp.where(lo, prod, 0.0), axis=-1, keepdims=True),
                                   jnp.sum(jnp.where(hi, prod, 0.0), axis=-1, keepdims=True))

        for bi, (_, d) in enumerate(DILATED_BRANCHES):
            def block(n, carry, bi=bi, d=d):
                rows = [_block_rows(n + j * nsteps, d, S) for j in range(ATT_UNROLL)]
                loaded = []
                for cur, prv, variant in rows:
                    qb = (q_ref[cur, :] * scale).astype(BF16)
                    kw = jnp.concatenate([k_ref[prv, :], k_ref[cur, :]], axis=0).astype(BF16)
                    vw = jnp.concatenate([v_ref[prv, :], v_ref[cur, :]], axis=0).astype(BF16)
                    dob = do_ref[cur, :].astype(BF16)
                    dq_old = dq_ref[cur, :] if bi > 0 else None
                    loaded.append((qb, kw, vw, dob, lse_ref[cur, :], delta_ref[cur, :], dq_old))
                results = []
                for j, ((cur, prv, variant), (qb, kw, vw, dob, lse_b, dl_b, dq_old)) in enumerate(zip(rows, loaded)):
                    dq_pair = dk_pair = dv_pair = None
                    for e, msk in enumerate((lo, hi)):
                        qm = jnp.where(msk, qb, 0)
                        dom = jnp.where(msk, dob, 0)
                        s = _dot_nt(qm, kw) + tab_ref[bi, variant, e]
                        pe = jnp.exp(s - lse_b[:, e * half:e * half + 1])
                        ds = pe * (_dot_nt(dom, vw) - dl_b[:, e * half:e * half + 1])
                        dtab_ref[bi, e] += ds
                        dsb = ds.astype(BF16)
                        dq_e = _dot_nn(dsb, jnp.where(msk, kw, 0))
                        dk_e = _dot_tn(dsb, qm)
                        dv_e = _dot_tn(pe.astype(BF16), dom)
                        dq_pair = dq_e if e == 0 else dq_pair + dq_e
                        dk_pair = dk_e if e == 0 else dk_pair + dk_e
                        dv_pair = dv_e if e == 0 else dv_pair + dv_e
                    dq_pair = dq_pair * scale
                    if bi > 0:
                        dq_pair = dq_pair + dq_old
                    results.append((dq_pair, dk_pair, dv_pair))
                for (cur, prv, variant), (dq_pair, dk_pair, dv_pair) in zip(rows, results):
                    dq_ref[cur, :] = dq_pair
                    dkv_ref[0, cur, :] += dk_pair[P:, :]
                    dkv_ref[1, cur, :] += dv_pair[P:, :]
                    dkv_ref[0, prv, :] += dk_pair[:P, :]
                    dkv_ref[1, prv, :] += dv_pair[:P, :]
                return carry

            lax.fori_loop(0, nsteps, block, 0)
        dq_out[...] = dq_ref[...].astype(BF16)
        dkv_out[...] = dkv_ref[...].astype(BF16)

    nl = D // LANE
    col = lambda off: pl.BlockSpec((None, S, LANE), lambda p, b: (b, 0, off + p))
    tab_spec = pl.BlockSpec((3, 2, 2, P, 2 * P), lambda p, b: (0, 0, p, 0, 0))
    return pl.pallas_call(
        body, name=name, grid=(H // 2, B),
        in_specs=[col(0), col(0), col(nl), col(0), col(0), col(0), tab_spec],
        out_specs=[col(0), pl.BlockSpec((2, None, S, LANE), lambda p, b: (0, b, 0, p)),
                   pl.BlockSpec((None, 3, 2, P, 2 * P), lambda p, b: (p, 0, 0, 0, 0))],
        out_shape=[jax.ShapeDtypeStruct((B, S, D), BF16), jax.ShapeDtypeStruct((2, B, S, D), BF16),
                   jax.ShapeDtypeStruct((H // 2, 3, 2, P, 2 * P), F32)],
        scratch_shapes=[pltpu.VMEM((S, LANE), F32), pltpu.VMEM((S, LANE), F32), pltpu.VMEM((2, S, LANE), F32)],
        compiler_params=pltpu.CompilerParams(
            dimension_semantics=("parallel", "arbitrary"),
            vmem_limit_bytes=VMEM_CAP),
    )(q, kv, kv, do, o, lse, tabs)


def _mesh_place():
    x, y, c = lax.axis_index("x"), lax.axis_index("y"), lax.axis_index("c")
    return x, y, c, (x, y, 1 - c), [(1 - x, y), (x, 1 - y), (1 - x, 1 - y)]


def _rdma(src, dst, send_sems, recv_sems, idx, to):
    return pltpu.make_async_remote_copy(
        src_ref=src, dst_ref=dst, send_sem=send_sems.at[idx], recv_sem=recv_sems.at[idx],
        device_id=to, device_id_type=pl.DeviceIdType.MESH)


def _comm_call(body, name, args, out_shape, n_sems, n_local):
    any_spec = pl.BlockSpec(memory_space=pl.ANY)
    return pl.pallas_call(
        body, name=name, in_specs=[any_spec] * len(args), out_specs=[any_spec] * len(out_shape),
        out_shape=out_shape,
        scratch_shapes=[pltpu.SemaphoreType.DMA(n_sems), pltpu.SemaphoreType.DMA(n_sems),
                        pltpu.SemaphoreType.DMA((n_local,))],
        compiler_params=pltpu.CompilerParams(has_side_effects=True),
    )(*args)


def _all_gather(arrays, name):
    n = len(arrays)

    def body(*refs):
        ins, outs = refs[:n], refs[n:2 * n]
        send_sems, recv_sems, loc_sems = refs[2 * n:]
        x, y, c, sib, chips = _mesh_place()
        me = (x, y, c)

        def rows(a, dev):
            return outs[a].at[4 * dev[0] + 2 * dev[1] + dev[2]]

        def copy(a, k, block, to, src=None):
            return _rdma(rows(a, block) if src is None else src, rows(a, block), send_sems, recv_sems, (a, k), to)

        local, first, passed = [], [], []
        for a in range(n):
            cp = pltpu.make_async_copy(ins[a], rows(a, me), loc_sems.at[a])
            cp.start()
            local.append(cp)
            first.append(copy(a, 0, me, sib, src=ins[a]))
            first += [copy(a, 1 + j, me, (*chip, c), src=ins[a]) for j, chip in enumerate(chips)]
        for cp in first:
            cp.start()
        for j, chip in enumerate(chips):
            for a in range(n):
                copy(a, 1 + j, (*chip, c), me).wait_recv()
                cp = copy(a, 4 + j, (*chip, c), sib)
                cp.start()
                passed.append(cp)
        for a in range(n):
            copy(a, 0, sib, me).wait_recv()
            for j, chip in enumerate(chips):
                copy(a, 4 + j, (*chip, 1 - c), me).wait_recv()
        for cp in first + passed:
            cp.wait_send()
        for cp in local:
            cp.wait()

    out_shape = [jax.ShapeDtypeStruct((N_DEV,) + g.shape, g.dtype) for g in arrays]
    return list(_comm_call(body, name, arrays, out_shape, (n, N_DEV - 1), n))


def _core_pair_exchange(halves, gathered, name):
    nh, ng = len(halves), len(gathered)

    def body(*refs):
        h_in, g_in = refs[:nh], refs[nh:nh + ng]
        outs = refs[nh + ng:nh + ng + nh + ng]
        theirs, g_out = outs[:nh], outs[nh:]
        send_sems, recv_sems, loc_sems = refs[nh + ng + nh + ng:]
        x, y, c, sib, chips = _mesh_place()
        me = 4 * x + 2 * y + c
        peers = [sib] + [(*chip, pc) for chip in chips for pc in (c, 1 - c)]
        nchip = N_DEV // 2
        gbase = nh * nchip
        local, sends = [], []
        for a in range(nh):
            for ch in range(nchip):
                cp = _rdma(h_in[a].at[ch, 1 - c], theirs[a].at[ch], send_sems, recv_sems, a * nchip + ch, sib)
                cp.start()
                sends.append(cp)
        for g in range(ng):
            cp = pltpu.make_async_copy(g_in[g], g_out[g].at[me], loc_sems.at[g])
            cp.start()
            local.append(cp)
            for k, peer in enumerate(peers):
                cp = _rdma(g_in[g], g_out[g].at[me], send_sems, recv_sems, gbase + g * (N_DEV - 1) + k, peer)
                cp.start()
                sends.append(cp)
        for cp in sends:
            cp.wait_send()
        for a in range(nh):
            for ch in range(nchip):
                _rdma(h_in[a].at[ch, 1 - c], theirs[a].at[ch], send_sems, recv_sems, a * nchip + ch, sib).wait_recv()
        for g in range(ng):
            for k, peer in enumerate(peers):
                pid = 4 * peer[0] + 2 * peer[1] + peer[2]
                _rdma(g_in[g], g_out[g].at[pid], send_sems, recv_sems, gbase + g * (N_DEV - 1) + k, peer).wait_recv()
        for cp in local:
            cp.wait()

    half = [jax.ShapeDtypeStruct((h.shape[0],) + h.shape[2:], h.dtype) for h in halves]
    out_shape = half + [jax.ShapeDtypeStruct((N_DEV,) + g.shape, g.dtype) for g in gathered]
    outs = _comm_call(body, name, list(halves) + list(gathered), out_shape,
                      (nh * (N_DEV // 2) + ng * (N_DEV - 1),), max(ng, 1))
    return list(outs[:nh]), list(outs[nh:])


def _gather_copies(src_refs, land_refs, send_sems, recv_sems):
    x, y, c, sib, chips = _mesh_place()
    me = 4 * x + 2 * y + c
    peers = [sib] + [(*chip, c) for chip in chips]
    return [pltpu.make_async_remote_copy(
                src_ref=src, dst_ref=land.at[me], send_sem=send_sems[4 * a + k], recv_sem=recv_sems[4 * a + k],
                device_id=peer, device_id_type=pl.DeviceIdType.MESH)
            for a, (src, land) in enumerate(zip(src_refs, land_refs)) for k, peer in enumerate(peers)]


def _forward_to_sibling(gathered, name):
    n = len(gathered)

    def body(*refs):
        outs = refs[n:2 * n]
        send_sems, recv_sems, _ = refs[2 * n:]
        x, y, c, sib, chips = _mesh_place()
        sends = []
        for a in range(n):
            for j, chip in enumerate(chips):
                rows = outs[a].at[4 * chip[0] + 2 * chip[1] + c]
                cp = _rdma(rows, rows, send_sems, recv_sems, (a, j), sib)
                cp.start()
                sends.append(cp)
        for cp in sends:
            cp.wait_send()
        for a in range(n):
            for j, chip in enumerate(chips):
                rows = outs[a].at[4 * chip[0] + 2 * chip[1] + (1 - c)]
                _rdma(rows, rows, send_sems, recv_sems, (a, j), sib).wait_recv()

    any_spec = pl.BlockSpec(memory_space=pl.ANY)
    return list(pl.pallas_call(
        body, name=name, in_specs=[any_spec] * n, out_specs=[any_spec] * n,
        out_shape=[jax.ShapeDtypeStruct(g.shape, g.dtype) for g in gathered],
        input_output_aliases={i: i for i in range(n)},
        scratch_shapes=[pltpu.SemaphoreType.DMA((n, 3)), pltpu.SemaphoreType.DMA((n, 3)),
                        pltpu.SemaphoreType.DMA((1,))],
        compiler_params=pltpu.CompilerParams(has_side_effects=True),
    )(*gathered))


def _split_start(copies_fn, srcs, land_shapes, n_sems, name, after=None):
    ns, nl = len(srcs), len(land_shapes)
    nc = n_sems
    hbm = pl.BlockSpec(memory_space=pltpu.HBM)
    sem = pl.BlockSpec(memory_space=pltpu.SEMAPHORE)

    n_in = ns + nl + (0 if after is None else 1)

    def body(*refs):
        src_refs, land_refs = refs[:ns], refs[ns:ns + nl]
        send_sems, recv_sems = refs[n_in:n_in + nc], refs[n_in + nc:n_in + 2 * nc]
        token = refs[-1]
        for cp in copies_fn(src_refs, land_refs, send_sems, recv_sems):
            cp.start()
        token[...] = jnp.zeros_like(token)

    lands = [lax.empty(s.shape, s.dtype) for s in land_shapes]
    thru = [pltpu.HBM(t.shape, t.dtype) for t in list(srcs) + lands]
    order = [] if after is None else [after]
    outs = pl.pallas_call(
        body, name=name,
        out_shape=(*([pltpu.SemaphoreType.DMA(())] * (2 * nc)), *thru, jax.ShapeDtypeStruct((8, LANE), F32)),
        in_specs=[hbm] * (ns + nl) + [pl.BlockSpec(memory_space=pl.ANY)] * len(order),
        out_specs=(*([sem] * (2 * nc)), *([hbm] * (ns + nl)), pl.BlockSpec(memory_space=pltpu.VMEM)),
        input_output_aliases={i: 2 * nc + i for i in range(ns + nl)},
        compiler_params=pltpu.CompilerParams(has_side_effects=pltpu.SideEffectType.DATAFLOW_SIDE_EFFECTING),
    )(*[pltpu.with_memory_space_constraint(t, pltpu.HBM) for t in list(srcs) + lands], *order)
    thru_out = outs[2 * nc:2 * nc + ns + nl]
    return list(outs[:nc]), list(outs[nc:2 * nc]), list(thru_out[:ns]), list(thru_out[ns:]), outs[-1]


def _split_wait(copies_fn, send_sems, recv_sems, srcs, lands, after, name):
    ns, nl, nc = len(srcs), len(lands), len(send_sems)
    hbm = pl.BlockSpec(memory_space=pltpu.HBM)
    sem = pl.BlockSpec(memory_space=pltpu.SEMAPHORE)

    def body(*refs):
        src_refs, land_refs = refs[:ns], refs[ns:ns + nl]
        send_refs, recv_refs = refs[ns + nl:ns + nl + nc], refs[ns + nl + nc:ns + nl + 2 * nc]
        for cp in copies_fn(src_refs, land_refs, send_refs, recv_refs):
            cp.wait_send()
            cp.wait_recv()

    outs = pl.pallas_call(
        body, name=name,
        out_shape=[pltpu.HBM(t.shape, t.dtype) for t in list(srcs) + list(lands)],
        in_specs=[hbm] * (ns + nl) + [sem] * (2 * nc) + [pl.BlockSpec(memory_space=pl.ANY)],
        out_specs=[hbm] * (ns + nl),
        input_output_aliases={i: i for i in range(ns + nl)},
        compiler_params=pltpu.CompilerParams(has_side_effects=pltpu.SideEffectType.DATAFLOW_SIDE_EFFECTING),
    )(*srcs, *lands, *send_sems, *recv_sems, after)
    return list(outs[:ns]), list(outs[ns:])


def _pair_copies(src_refs, land_refs, send_sems, recv_sems):
    x, y, c, sib, chips = _mesh_place()
    nchip = N_DEV // 2
    return [pltpu.make_async_remote_copy(
                src_ref=src.at[ch, 1 - c], dst_ref=land.at[ch], send_sem=send_sems[nchip * a + ch],
                recv_sem=recv_sems[nchip * a + ch], device_id=sib, device_id_type=pl.DeviceIdType.MESH)
            for a, (src, land) in enumerate(zip(src_refs, land_refs)) for ch in range(nchip)]


def _chip_copies(src_refs, land_refs, send_sems, recv_sems):
    x, y, c, sib, chips = _mesh_place()
    return [pltpu.make_async_remote_copy(
                src_ref=src.at[2 * chip[0] + chip[1]], dst_ref=land.at[j], send_sem=send_sems[3 * a + j],
                recv_sem=recv_sems[3 * a + j], device_id=(*chip, c), device_id_type=pl.DeviceIdType.MESH)
            for a, (src, land) in enumerate(zip(src_refs, land_refs)) for j, chip in enumerate(chips)]


def _pair_sum(halves, theirs, core, name, tr=256):
    nchip, _, R, C = halves.shape
    tr = _row_tile(R, tr)

    def body(core_ref, h_ref, t_ref, o_ref):
        o_ref[...] = (h_ref[...] + t_ref[...]).astype(BF16)

    grid_spec = pltpu.PrefetchScalarGridSpec(
        num_scalar_prefetch=1, grid=(nchip, R // tr),
        in_specs=[pl.BlockSpec((None, None, tr, C), lambda ch, i, core_ref: (ch, core_ref[0], i, 0)),
                  pl.BlockSpec((None, tr, C), lambda ch, i, core_ref: (ch, i, 0))],
        out_specs=pl.BlockSpec((None, tr, C), lambda ch, i, core_ref: (ch, i, 0)))
    return pl.pallas_call(
        body, name=name, grid_spec=grid_spec, out_shape=jax.ShapeDtypeStruct((nchip, R, C), BF16),
        compiler_params=pltpu.CompilerParams(
            dimension_semantics=("parallel", "parallel"), vmem_limit_bytes=_vmem(3 * _nbytes((tr, C), F32))),
    )(core, *[pltpu.with_memory_space_constraint(t, pltpu.HBM) for t in (halves, theirs)])


def _chip_sum(own, recv, chip, name, tr=256):
    _, R, C = own.shape
    tr = _row_tile(R, tr)

    def body(chip_ref, o_ref, r_ref, out_ref):
        acc = o_ref[...].astype(F32)
        for j in range(3):
            acc = acc + r_ref[j].astype(F32)
        out_ref[...] = acc

    grid_spec = pltpu.PrefetchScalarGridSpec(
        num_scalar_prefetch=1, grid=(R // tr,),
        in_specs=[pl.BlockSpec((None, tr, C), lambda i, chip_ref: (chip_ref[0], i, 0)),
                  pl.BlockSpec((3, tr, C), lambda i, chip_ref: (0, i, 0))],
        out_specs=pl.BlockSpec((tr, C), lambda i, chip_ref: (i, 0)))
    return pl.pallas_call(
        body, name=name, grid_spec=grid_spec, out_shape=jax.ShapeDtypeStruct((R, C), F32),
        compiler_params=pltpu.CompilerParams(
            dimension_semantics=("parallel",), vmem_limit_bytes=_vmem(4 * _nbytes((tr, C), F32))),
    )(chip, *[pltpu.with_memory_space_constraint(t, pltpu.HBM) for t in (own, recv)])


def _row_tile(rows, target):
    best = rows
    for t in range(8, min(rows, target) + 1, 8):
        if rows % t == 0:
            best = t
    return best


def _sum_slots(recv, name, tr=128):
    n, R, C = recv.shape
    tr = _row_tile(R, tr if recv.dtype == F32 else 2 * tr)

    def body(r_ref, o_ref):
        acc = r_ref[0].astype(F32)
        for k in range(1, n):
            acc = acc + r_ref[k].astype(F32)
        o_ref[...] = acc

    return pl.pallas_call(
        body, name=name, grid=(R // tr,),
        in_specs=[pl.BlockSpec((n, tr, C), lambda i: (0, i, 0))],
        out_specs=pl.BlockSpec((tr, C), lambda i: (i, 0)),
        out_shape=jax.ShapeDtypeStruct((R, C), F32),
        compiler_params=pltpu.CompilerParams(
            dimension_semantics=("parallel",), vmem_limit_bytes=_vmem(10 * _nbytes((tr, C), F32))),
    )(recv)


def _adamw_math(g_ref, w_ref, m_ref, v_ref, d_ref, nm_ref, nv_ref):
    gv = g_ref[...]
    nm = ADAM_B1 * m_ref[...] + (1.0 - ADAM_B1) * gv
    nv = ADAM_B2 * v_ref[...] + (1.0 - ADAM_B2) * (gv * gv)
    m_hat = nm / (1.0 - ADAM_B1 ** ADAM_STEP)
    v_hat = nv / (1.0 - ADAM_B2 ** ADAM_STEP)
    d_ref[...] = -ADAM_LR * (m_hat / (jnp.sqrt(v_hat) + ADAM_EPS) + ADAM_WD * w_ref[...])
    nm_ref[...] = nm
    nv_ref[...] = nv


def _chip_sum_adamw(own, recv, chip, w, m, v, name, tr=256):
    _, R, C = own.shape
    tr = _row_tile(R, tr)

    def body(chip_ref, o_ref, r_ref, w_ref, m_ref, v_ref, g_ref, d_ref, nm_ref, nv_ref):
        acc = o_ref[...].astype(F32)
        for j in range(3):
            acc = acc + r_ref[j].astype(F32)
        g_ref[...] = acc
        _adamw_math(g_ref, w_ref, m_ref, v_ref, d_ref, nm_ref, nv_ref)

    row = pl.BlockSpec((tr, C), lambda i, chip_ref: (i, 0))
    grid_spec = pltpu.PrefetchScalarGridSpec(
        num_scalar_prefetch=1, grid=(R // tr,),
        in_specs=[pl.BlockSpec((None, tr, C), lambda i, chip_ref: (chip_ref[0], i, 0)),
                  pl.BlockSpec((3, tr, C), lambda i, chip_ref: (0, i, 0)), row, row, row],
        out_specs=[row] * 4)
    return pl.pallas_call(
        body, name=name, grid_spec=grid_spec, out_shape=[jax.ShapeDtypeStruct((R, C), F32)] * 4,
        compiler_params=pltpu.CompilerParams(
            dimension_semantics=("parallel",), vmem_limit_bytes=_vmem(10 * _nbytes((tr, C), F32))),
    )(chip, own, recv, w, m, v)


def _adamw_small(params, name):
    n = len(params)
    two_d = [(-1, w.shape[-1]) for w, _, _, _ in params]
    flat = [t.reshape(two_d[i]) for i, (w, m, v, g) in enumerate(params) for t in (g, w, m, v)]

    def body(*refs):
        ins, outs = refs[:4 * n], refs[4 * n:]
        for i in range(n):
            _adamw_math(*ins[4 * i:4 * i + 4], *outs[3 * i:3 * i + 3])

    outs = pl.pallas_call(
        body, name=name,
        out_shape=[jax.ShapeDtypeStruct(flat[4 * i].shape, F32) for i in range(n) for _ in range(3)],
    )(*flat)
    return [(g.reshape(w.shape),) + tuple(outs[3 * i + t].reshape(w.shape) for t in range(3))
            for i, (w, m, v, g) in enumerate(params)]


def _adamw(g, w, m, v, name, tr=256):
    R, C = g.shape
    tr = _row_tile(R, tr)
    body = functools.partial(_adamw_math)

    row = pl.BlockSpec((tr, C), lambda i: (i, 0))
    return pl.pallas_call(
        body, name=name, grid=(R // tr,), in_specs=[row] * 4, out_specs=[row] * 3,
        out_shape=[jax.ShapeDtypeStruct((R, C), F32)] * 3,
        compiler_params=pltpu.CompilerParams(
            dimension_semantics=("parallel",), vmem_limit_bytes=_vmem(8 * _nbytes((tr, C), F32))),
    )(*[pltpu.with_memory_space_constraint(t, pltpu.HBM) for t in (g, w, m, v)])


def _pack_rows(parts):
    flat = jnp.concatenate([p.reshape(-1).astype(F32) for p in parts])
    rows = -(-flat.shape[0] // (8 * LANE)) * 8
    return jnp.pad(flat, (0, rows * LANE - flat.shape[0])).reshape(rows, LANE)


def _unpack(flat, shapes):
    out, off = [], 0
    for shp in shapes:
        n = math.prod(shp)
        out.append(flat[off:off + n].reshape(shp))
        off += n
    return out


def kernel(x, a_norm, a_w_in, a_conv, a_w_out, kv_norm, w_kv, b_norm, b_w_q, b_w_o, rel_bias, ffn_norm, ffn_w_up, ffn_conv, ffn_conv_b, ffn_w_down, final_norm, loss_target, m_a_norm, m_a_w_in, m_a_conv, m_a_w_out, m_kv_norm, m_w_kv, m_b_norm, m_b_w_q, m_b_w_o, m_rel_bias, m_ffn_norm, m_ffn_w_up, m_ffn_conv, m_ffn_conv_b, m_ffn_w_down, m_final_norm, v_a_norm, v_a_w_in, v_a_conv, v_a_w_out, v_kv_norm, v_w_kv, v_b_norm, v_b_w_q, v_b_w_o, v_rel_bias, v_ffn_norm, v_ffn_w_up, v_ffn_conv, v_ffn_conv_b, v_ffn_w_down, v_final_norm):
    B, S, D = x.shape
    T = B * S
    F = ffn_w_down.shape[1] * N_DEV
    me = 4 * lax.axis_index("x") + 2 * lax.axis_index("y") + lax.axis_index("c")

    big_shards = [a_w_in[0].T, a_w_out[0], w_kv.T, b_w_q[0], b_w_o[0],
                  ffn_w_up[0].T, ffn_w_up[1].T, ffn_w_down[0], ffn_w_down[1]]
    small_shapes = [a_norm.shape, a_conv.shape, ffn_conv.shape]
    small_pack = _pack_rows([a_norm, a_conv, ffn_conv])
    shards = [s.astype(BF16) for s in big_shards]
    first = _all_gather([shards[0], small_pack], "gather_weights_first")
    win_t = first[0].reshape(-1, D)
    smalls = [_unpack(first[1][j].reshape(-1), small_shapes) for j in range(N_DEV)]
    a_norm_f = jnp.concatenate([s[0] for s in smalls], axis=-1)
    a_conv_f = jnp.concatenate([s[1] for s in smalls], axis=-1)[0]
    ffn_conv_f = jnp.concatenate([s[2] for s in smalls], axis=-1)

    def gather_start(tag, idxs, after):
        srcs = [shards[i] for i in idxs]
        lands = [jax.ShapeDtypeStruct((N_DEV,) + s.shape, s.dtype) for s in srcs]
        return _split_start(_gather_copies, srcs, lands, 4 * len(srcs), f"gather_start_{tag}", after=after)

    def gather_finish(tag, handle, after):
        send, recv, srcs, lands, _ = handle
        srcs, lands = _split_wait(_gather_copies, send, recv, srcs, lands, after, f"gather_wait_{tag}")
        lands = _forward_to_sibling(lands, f"gather_forward_{tag}")
        return [lax.dynamic_update_slice(g, s[None], (me, 0, 0)).reshape(-1, D) for g, s in zip(lands, srcs)]

    ffn0_w = gather_start("ffn0", [1, 5, 7], first[0])
    rest_w = gather_start("rest", [2, 3, 4, 6, 8], ffn0_w[4])

    x2 = x.reshape(T, D)
    (xn0,) = _rmsnorm_fwd(x2, a_norm_f, "a_norm_fwd")
    bch = _matmul(xn0, win_t, mode="nt", out_dtype=BF16, after=rest_w[4], name="a_in_proj", tm=2048).reshape(B, S, 3 * D)
    gated = _shortconv_fwd(bch, a_conv_f, "a_gate_fwd").reshape(T, D)
    wout, wup0_t, wdn0 = gather_finish("ffn0", ffn0_w, gated)
    h1 = _matmul(gated, wout, mode="nn", out_dtype=F32, add=x2, name="a_out_proj")
    wup_t, wdn = [wup0_t, None], [wdn0, None]

    def ffn_fwd(h, l):
        (xn,) = _rmsnorm_fwd(h, ffn_norm[l:l + 1], f"ffn{l}_norm_fwd")
        u0 = _matmul(xn, wup_t[l], mode="nt", out_dtype=BF16, name=f"ffn{l}_up", tm=2048, tn=1408).reshape(B, S, 2 * F)
        act = _ffn_gate_fwd(u0, ffn_conv_f[l], ffn_conv_b[l:l + 1], f"ffn{l}_gate_fwd").reshape(T, F)
        out = _matmul(act, wdn[l], mode="nn", out_dtype=F32, add=h, name=f"ffn{l}_down", tk=2816)
        return out, (xn, u0, act)

    h2, ffn0_saved = ffn_fwd(h1, 0)
    wkv_t, wq, wo, wup_t[1], wdn[1] = gather_finish("rest", rest_w, h2)
    kvn, xnb = _rmsnorm_fwd(h2, jnp.stack([kv_norm, b_norm[0]]), "kv_b_norm_fwd")
    kv = _matmul(kvn, wkv_t, mode="nt", out_dtype=F32, name="kv_proj", tm=2048).reshape(B, S, 2 * D)
    q = _matmul(xnb, wq, mode="nn", out_dtype=F32, name="q_proj", tm=2048).reshape(B, S, D)

    tables = [_band_tables(w, d) for (w, d) in DILATED_BRANCHES]
    onehots = jnp.stack([t[0] for t in tables])
    P = ATT_BLOCK
    bias_vals = _bias_lookup(rel_bias.T, onehots, "rel_bias_lookup").reshape(3, N_HEADS, P, 2 * P)
    in_cur = (jnp.arange(2 * P) >= P)[None, :]
    masks = jnp.stack([jnp.stack([t[1] & in_cur, t[1]]) for t in tables])
    tabs = jnp.where(masks[:, :, None], bias_vals[:, None], NEG)

    attn3, lse3 = _attn_fwd(q, kv, tabs, "attn_fwd")
    attn = attn3.reshape(T, D)
    h3 = _matmul(attn, wo, mode="nn", out_dtype=F32, add=h2, name="o_proj")
    h4, ffn1_saved = ffn_fwd(h3, 1)

    dh4, dh4b, d_final_norm, sq = _final_loss_bwd(h4, final_norm.reshape(1, D), loss_target.reshape(T, D), "loss_bwd")
    loss_part = 0.5 * jnp.sum(sq) / D

    core_idx = lax.axis_index("c").astype(jnp.int32).reshape(1)
    chip_idx = (2 * lax.axis_index("x") + lax.axis_index("y")).astype(jnp.int32).reshape(1)
    in_flight = []

    def group_start(tag, indexed_grads):
        halves = [g.reshape(N_DEV // 2, 2, g.shape[0] // N_DEV, D) for _, g in indexed_grads]
        lands = [jax.ShapeDtypeStruct((h.shape[0],) + h.shape[2:], h.dtype) for h in halves]
        send, recv, srcs, lnd, token = _split_start(_pair_copies, halves, lands, (N_DEV // 2) * len(halves),
                                                    f"exchange_core_start_{tag}")
        return (tag, [i for i, _ in indexed_grads], send, recv, srcs, lnd), token

    def group_finish(handle, after):
        tag, idxs, send, recv, srcs, lnd = handle
        halves, theirs = _split_wait(_pair_copies, send, recv, srcs, lnd, after, f"exchange_core_wait_{tag}")
        sums = [_pair_sum(h_, t_, core_idx, f"grad_core_sum_{tag}{k}")
                for k, (h_, t_) in enumerate(zip(halves, theirs))]
        lands = [jax.ShapeDtypeStruct((3,) + s.shape[1:], s.dtype) for s in sums]
        send, recv, srcs, lnd, token = _split_start(_chip_copies, sums, lands, 3 * len(sums),
                                                    f"exchange_chip_start_{tag}")
        in_flight.append((tag, idxs, send, recv, srcs, lnd))
        return token

    def ffn_bwd(dh_out, dh_out_b, h_in, saved, l, after):
        xn, u0, act = saved
        dact = _matmul(dh_out_b, wdn[l], mode="nt", out_dtype=BF16, after=after, name=f"ffn{l}_down_dx",
                       tm=2048, tn=1408).reshape(B, S, F)
        d_wdn = _matmul(act, dh_out_b, mode="tn", out_dtype=F32, name=f"ffn{l}_down_dw", tm=1408, tn=1024, tk=2048)
        du0, d_conv, d_conv_b = _ffn_gate_bwd(u0, dact, ffn_conv_f[l], ffn_conv_b[l:l + 1], f"ffn{l}_gate_bwd")
        du0 = du0.reshape(2, T, F)
        dxn = _matmul(du0, wup_t[l], mode="nn", out_dtype=BF16, a_parts=2, name=f"ffn{l}_up_dx", tk=2816)
        d_wup_t = _matmul(du0, xn, mode="tn", out_dtype=F32, a_parts=2, name=f"ffn{l}_up_dw", tm=1408, tn=1024, tk=2048)
        dh_in, dh_in_b, d_norm = _rmsnorm_bwd(h_in, ffn_norm[l:l + 1], [dxn], dh_out, f"ffn{l}_norm_bwd")
        return (dh_in, dh_in_b), (d_wdn, d_wup_t, d_conv, d_conv_b, d_norm)

    (dh3, dh3b), ffn1_grads = ffn_bwd(dh4, dh4b, h3, ffn1_saved, 1, None)
    group, token = group_start("ffn1", [(6, ffn1_grads[1]), (8, ffn1_grads[0])])

    dattn = _matmul(dh3b, wo, mode="nt", out_dtype=F32, after=token, name="o_proj_dx", tm=2048)
    token = group_finish(group, dattn)
    dattn = dattn.reshape(B, S, D)
    d_wo = _matmul(attn, dh3b, mode="tn", out_dtype=F32, after=token, name="o_proj_dw", tm=1024, tn=1024, tk=2048)
    dq, dkv, dtab = _attn_bwd(q, kv, dattn, attn3, lse3, tabs, "attn_bwd")
    dtabs = dtab.transpose(1, 0, 2, 3, 4).reshape(3, N_HEADS, P * 2 * P)
    d_rel_bias = _bias_grad(dtabs, onehots, "rel_bias_grad").T
    dq2 = dq.reshape(T, D)
    dkv2 = dkv.reshape(2, T, D)
    d_wq = _matmul(xnb, dq2, mode="tn", out_dtype=F32, name="q_proj_dw", tm=1024, tn=1024, tk=2048)
    d_wkv_t = _matmul(dkv2, kvn, mode="tn", out_dtype=F32, a_parts=2, name="kv_proj_dw", tm=1024, tn=1024, tk=2048)
    group, token = group_start("attn", [(2, d_wkv_t), (3, d_wq), (4, d_wo)])
    dxnb = _matmul(dq2, wq, mode="nt", out_dtype=BF16, after=token, name="q_proj_dx")
    dkvn = _matmul(dkv2, wkv_t, mode="nn", out_dtype=BF16, a_parts=2, name="kv_proj_dx")
    dh2, dh2b, d_kvb_norm = _rmsnorm_bwd(h2, jnp.stack([kv_norm, b_norm[0]]), [dkvn, dxnb], dh3, "kv_b_norm_bwd")
    token = group_finish(group, dh2b)

    (dh1, dh1b), ffn0_grads = ffn_bwd(dh2, dh2b, h1, ffn0_saved, 0, token)
    group, token = group_start("ffn0", [(5, ffn0_grads[1]), (7, ffn0_grads[0])])

    dgated = _matmul(dh1b, wout, mode="nt", out_dtype=BF16, after=token, name="a_out_proj_dx", tm=2048)
    token = group_finish(group, dgated)
    dgated = dgated.reshape(B, S, D)
    d_wout = _matmul(gated, dh1b, mode="tn", out_dtype=F32, after=token, name="a_out_proj_dw",
                     tm=1024, tn=1024, tk=2048)
    dbch, d_a_conv = _shortconv_bwd(bch, dgated, a_conv_f, "a_gate_bwd")
    dbch = dbch.reshape(3, T, D)
    dxn0 = _matmul(dbch, win_t, mode="nn", out_dtype=BF16, a_parts=3, name="a_in_proj_dx", tm=2048)
    d_win_t = _matmul(dbch, xn0, mode="tn", out_dtype=F32, a_parts=3, name="a_in_proj_dw", tm=1024, tn=1024, tk=2048)
    group, token = group_start("a", [(0, d_win_t), (1, d_wout)])
    grad_x, _, d_a_norm = _rmsnorm_bwd(x2, a_norm_f + token[0, 0], [dxn0], dh1, "a_norm_bwd")
    token = group_finish(group, grad_x)

    small_full = [d_a_norm, d_a_conv, jnp.stack([ffn0_grads[2], ffn1_grads[2]]),
                  d_kvb_norm[0], d_kvb_norm[1], d_rel_bias, jnp.concatenate([ffn0_grads[4], ffn1_grads[4]]),
                  jnp.concatenate([ffn0_grads[3], ffn1_grads[3]]), d_final_norm, loss_part]
    small_full_shapes = [(1, D), (3, D), (2, 3, 2 * F), (D,), (1, D), rel_bias.shape, (2, D), (2, 2 * F), (D,), ()]
    _, gath = _core_pair_exchange([], [_pack_rows(small_full)], "exchange_small")
    chip_sums, recv = {}, {}
    for tag, idxs, send, rcv, srcs, lnd in in_flight:
        srcs, lnd = _split_wait(_chip_copies, send, rcv, srcs, lnd, token, f"exchange_chip_wait_{tag}")
        for i, s_, l_ in zip(idxs, srcs, lnd):
            chip_sums[i], recv[i] = s_, l_
    small_sum = _sum_slots(gath[0], "small_grad_sum").reshape(-1)
    (g_a_norm_f, g_a_conv_f, g_ffn_conv_f, g_kv_norm, g_b_norm, g_rel_bias, g_ffn_norm, g_ffn_conv_b,
     g_final_norm, loss) = _unpack(small_sum, small_full_shapes)

    def my_cols(full, width):
        return lax.dynamic_slice_in_dim(full, me * width, width, axis=full.ndim - 1)

    g_a_norm = my_cols(g_a_norm_f, D // N_DEV)
    g_a_conv = my_cols(g_a_conv_f, D // N_DEV)[None]
    g_ffn_conv = my_cols(g_ffn_conv_f, 2 * F // N_DEV)

    big_w = [(a_w_in, m_a_w_in, v_a_w_in, True), (a_w_out, m_a_w_out, v_a_w_out, False),
             (w_kv, m_w_kv, v_w_kv, True), (b_w_q, m_b_w_q, v_b_w_q, False), (b_w_o, m_b_w_o, v_b_w_o, False),
             (ffn_w_up[0], m_ffn_w_up[0], v_ffn_w_up[0], True), (ffn_w_up[1], m_ffn_w_up[1], v_ffn_w_up[1], True),
             (ffn_w_down[0], m_ffn_w_down[0], v_ffn_w_down[0], False),
             (ffn_w_down[1], m_ffn_w_down[1], v_ffn_w_down[1], False)]
    big_out = []
    for i, (w, m, v, transposed) in enumerate(big_w):
        if transposed:
            g = _chip_sum(chip_sums[i], recv[i], chip_idx, f"grad_sum_{i}").T
            w2, m2, v2 = (t.reshape(g.shape) for t in (w, m, v))
            delta, nm, nv = _adamw(g, w2, m2, v2, f"adamw_{i}")
        else:
            w2, m2, v2 = (t.reshape(chip_sums[i].shape[1:]) for t in (w, m, v))
            g, delta, nm, nv = _chip_sum_adamw(chip_sums[i], recv[i], chip_idx, w2, m2, v2, f"grad_sum_adamw_{i}")
        big_out.append(tuple(t.reshape(w.shape) for t in (g, delta, nm, nv)))

    def pair(i, j):
        return tuple(jnp.stack([big_out[i][t], big_out[j][t]]) for t in range(4))

    small_w = [(a_norm, m_a_norm, v_a_norm, g_a_norm), (a_conv, m_a_conv, v_a_conv, g_a_conv),
               (ffn_conv, m_ffn_conv, v_ffn_conv, g_ffn_conv), (kv_norm, m_kv_norm, v_kv_norm, g_kv_norm),
               (b_norm, m_b_norm, v_b_norm, g_b_norm), (rel_bias, m_rel_bias, v_rel_bias, g_rel_bias),
               (ffn_norm, m_ffn_norm, v_ffn_norm, g_ffn_norm),
               (ffn_conv_b, m_ffn_conv_b, v_ffn_conv_b, g_ffn_conv_b),
               (final_norm, m_final_norm, v_final_norm, g_final_norm)]
    small_out = _adamw_small(small_w, "adamw_small")

    per_weight = [small_out[0], big_out[0], small_out[1], big_out[1], small_out[3], big_out[2], small_out[4],
                  big_out[3], big_out[4], small_out[5], small_out[6], pair(5, 6), small_out[2], small_out[7],
                  pair(7, 8), small_out[8]]
    outs = [loss, grad_x.reshape(B, S, D)]
    for t in range(4):
        outs.extend(pw[t] for pw in per_weight)
    return tuple(outs)
```

```python
import functools
import math

import jax
import jax.numpy as jnp
from jax import lax
from jax.experimental import pallas as pl
from jax.experimental.pallas import tpu as pltpu

F32 = jnp.float32
BF16 = jnp.bfloat16

N_DEV = 8
N_HEADS = 16
ATT_BLOCK = 128
ATT_PIPE = 4
ATT_UNROLL = 8
DILATED_BRANCHES = ((128, 1), (512, 4), (2048, 16))
REL_BUCKETS = 32
REL_MAX_DISTANCE = 2048
RMS_EPS = 1e-6
ADAM_LR = 0.001
ADAM_B1 = 0.9
ADAM_B2 = 0.999
ADAM_EPS = 1e-08
ADAM_WD = 0.01
ADAM_STEP = 10

LANE = 128
HALO = 16
NEG = -1e30
VMEM_CAP = 56 << 20


def _vmem(block_bytes):
    return int(min(VMEM_CAP, max(32 << 20, 3 * block_bytes + (8 << 20))))


def _nbytes(shape, dtype):
    return math.prod(shape) * jnp.dtype(dtype).itemsize


def _tile(dim, target):
    best = None
    t = LANE
    while t <= min(dim, target):
        if dim % t == 0:
            best = t
        t += LANE
    return best if best is not None else dim


def _matmul(a, b, *, mode, out_dtype, name, add=None, after=None, a_parts=1, tm=1024, tn=1024, tk=1024):
    P = a_parts
    if mode == "nn":
        M, K = (a.shape[0], a.shape[1]) if P == 1 else (a.shape[1], a.shape[2] * P)
        N = b.shape[1]
    elif mode == "nt":
        assert P == 1
        M, K = a.shape
        N = b.shape[0]
    else:
        K = a.shape[0] if P == 1 else a.shape[1]
        M = a.shape[1] if P == 1 else a.shape[2] * P
        N = b.shape[1]
    tm = _tile(M // P if mode == "tn" else M, tm)
    tn = _tile(N, tn)
    tk = _tile(K // P if mode == "nn" else K, tk)
    nm, nn_, nk = M // tm, N // tn, K // tk

    if mode == "nn":
        if P == 1:
            a_spec = pl.BlockSpec((tm, tk), lambda i, j, k: (i, k))
        else:
            nkp = nk // P
            a_spec = pl.BlockSpec((None, tm, tk), lambda i, j, k: (k // nkp, i, k % nkp))
        b_spec = pl.BlockSpec((tk, tn), lambda i, j, k: (k, j))
        dims = (((1,), (0,)), ((), ()))
    elif mode == "nt":
        a_spec = pl.BlockSpec((tm, tk), lambda i, j, k: (i, k))
        b_spec = pl.BlockSpec((tn, tk), lambda i, j, k: (j, k))
        dims = (((1,), (1,)), ((), ()))
    else:
        if P == 1:
            a_spec = pl.BlockSpec((tk, tm), lambda i, j, k: (k, i))
        else:
            nmp = nm // P
            a_spec = pl.BlockSpec((None, tk, tm), lambda i, j, k: (i // nmp, k, i % nmp))
        b_spec = pl.BlockSpec((tk, tn), lambda i, j, k: (k, j))
        dims = (((0,), (0,)), ((), ()))
    o_spec = pl.BlockSpec((tm, tn), lambda i, j, k: (i, j))
    in_specs = [a_spec, b_spec]
    args = [a, b]
    if add is not None:
        in_specs.append(o_spec)
        args.append(add)
    has_add = add is not None
    if after is not None:
        in_specs.append(pl.BlockSpec(memory_space=pl.ANY))
        args.append(after)
    n_extra = len(args) - 2

    direct = nk > 1 and out_dtype == F32 and not has_add

    def body(a_ref, b_ref, *rest):
        add_ref = rest[0] if has_add else None
        o_ref = rest[n_extra]
        acc_ref = o_ref if direct or nk == 1 else rest[-1]
        k = pl.program_id(2)

        def part():
            return lax.dot_general(a_ref[...].astype(BF16), b_ref[...].astype(BF16), dims,
                                   preferred_element_type=F32)

        def finish(r):
            if has_add:
                r = r + add_ref[...].astype(F32)
            o_ref[...] = r.astype(out_dtype)

        if nk == 1:
            finish(part())
        else:
            @pl.when(k == 0)
            def _():
                acc_ref[...] = part()

            @pl.when(jnp.logical_and(k > 0, jnp.logical_or(k < nk - 1, direct)))
            def _():
                acc_ref[...] += part()

            if not direct:
                @pl.when(k == nk - 1)
                def _():
                    finish(acc_ref[...] + part())

    blk = (_nbytes((tm, tk), a.dtype) + _nbytes((tk, tn), b.dtype) + _nbytes((tm, tn), out_dtype)
           + (_nbytes((tm, tn), add.dtype) if has_add else 0)) * 2 + 3 * _nbytes((tm, tn), F32)
    return pl.pallas_call(
        body, name=name, grid=(nm, nn_, nk),
        in_specs=in_specs, out_specs=o_spec,
        out_shape=jax.ShapeDtypeStruct((M, N), out_dtype),
        scratch_shapes=[] if direct or nk == 1 else [pltpu.VMEM((tm, tn), F32)],
        compiler_params=pltpu.CompilerParams(
            dimension_semantics=("parallel", "parallel", "arbitrary"), vmem_limit_bytes=_vmem(blk)),
    )(*args)


def _rmsnorm_fwd(x, gains, name, tm=1024):
    T, D = x.shape
    n = gains.shape[0]
    tm = _tile(T, tm)

    def body(x_ref, g_ref, *o_refs):
        xv = x_ref[...]
        xhat = xv * lax.rsqrt(jnp.mean(xv * xv, axis=-1, keepdims=True) + RMS_EPS)
        for i in range(n):
            o_refs[i][...] = (xhat * g_ref[i:i + 1, :]).astype(BF16)

    row = pl.BlockSpec((tm, D), lambda i: (i, 0))
    outs = pl.pallas_call(
        body, name=name, grid=(T // tm,),
        in_specs=[row, pl.BlockSpec((n, D), lambda i: (0, 0))],
        out_specs=[row] * n,
        out_shape=[jax.ShapeDtypeStruct((T, D), BF16)] * n,
        compiler_params=pltpu.CompilerParams(
            dimension_semantics=("parallel",), vmem_limit_bytes=_vmem(4 * _nbytes((tm, D), F32))),
    )(x, gains)
    return tuple(outs)


def _rmsnorm_bwd(x, gains, dxns, dres, name, tm=1024):
    T, D = x.shape
    n = gains.shape[0]
    tm = _tile(T, tm)

    def body(x_ref, g_ref, *rest):
        dxn_refs = rest[:n]
        dres_ref, dx_ref, dxb_ref, dg_ref = rest[n:]
        xv = x_ref[...]
        rstd = lax.rsqrt(jnp.mean(xv * xv, axis=-1, keepdims=True) + RMS_EPS)
        xhat = xv * rstd
        dx = dres_ref[...]

        @pl.when(pl.program_id(0) == 0)
        def _():
            dg_ref[...] = jnp.zeros_like(dg_ref)

        for i in range(n):
            dy = dxn_refs[i][...].astype(F32)
            dg_ref[i:i + 1, :] += jnp.sum(dy * xhat, axis=0, keepdims=True)
            dxh = dy * g_ref[i:i + 1, :]
            dx = dx + rstd * (dxh - xhat * jnp.mean(dxh * xhat, axis=-1, keepdims=True))
        dx_ref[...] = dx
        dxb_ref[...] = dx.astype(BF16)

    row = pl.BlockSpec((tm, D), lambda i: (i, 0))
    par = pl.BlockSpec((n, D), lambda i: (0, 0))
    return pl.pallas_call(
        body, name=name, grid=(T // tm,),
        in_specs=[row, par] + [row] * n + [row],
        out_specs=[row, row, par],
        out_shape=[jax.ShapeDtypeStruct((T, D), F32), jax.ShapeDtypeStruct((T, D), BF16),
                   jax.ShapeDtypeStruct((n, D), F32)],
        compiler_params=pltpu.CompilerParams(
            dimension_semantics=("arbitrary",), vmem_limit_bytes=_vmem((4 + n) * _nbytes((tm, D), F32))),
    )(x, gains, *dxns, dres)


def _final_loss_bwd(h, gain, target, name, tm=1024):
    T, D = h.shape
    tm = _tile(T, tm)

    def body(h_ref, g_ref, t_ref, dh_ref, dhb_ref, dg_ref, sq_ref):
        xv = h_ref[...]
        rstd = lax.rsqrt(jnp.mean(xv * xv, axis=-1, keepdims=True) + RMS_EPS)
        xhat = xv * rstd
        err = xhat * g_ref[...] - t_ref[...]

        @pl.when(pl.program_id(0) == 0)
        def _():
            dg_ref[...] = jnp.zeros_like(dg_ref)
            sq_ref[...] = jnp.zeros_like(sq_ref)

        sq_ref[...] += jnp.sum(err * err, axis=0, keepdims=True)
        dy = err * (1.0 / D)
        dg_ref[...] += jnp.sum(dy * xhat, axis=0, keepdims=True)
        dxh = dy * g_ref[...]
        dh = rstd * (dxh - xhat * jnp.mean(dxh * xhat, axis=-1, keepdims=True))
        dh_ref[...] = dh
        dhb_ref[...] = dh.astype(BF16)

    row = pl.BlockSpec((tm, D), lambda i: (i, 0))
    par = pl.BlockSpec((1, D), lambda i: (0, 0))
    return pl.pallas_call(
        body, name=name, grid=(T // tm,),
        in_specs=[row, par, row], out_specs=[row, row, par, par],
        out_shape=[jax.ShapeDtypeStruct((T, D), F32), jax.ShapeDtypeStruct((T, D), BF16),
                   jax.ShapeDtypeStruct((1, D), F32), jax.ShapeDtypeStruct((1, D), F32)],
        compiler_params=pltpu.CompilerParams(
            dimension_semantics=("arbitrary",), vmem_limit_bytes=_vmem(5 * _nbytes((tm, D), F32))),
    )(h, gain, target)


def _halo_specs(S, ts, tc, col_off, prev, nxt):
    r = ts // HALO
    last = S // HALO - 1
    specs = []
    if prev:
        specs.append(pl.BlockSpec((None, HALO, tc), lambda j, b, s: (b, jnp.maximum(s * r - 1, 0), col_off + j)))
    specs.append(pl.BlockSpec((None, ts, tc), lambda j, b, s: (b, s, col_off + j)))
    if nxt:
        specs.append(pl.BlockSpec((None, HALO, tc), lambda j, b, s: (b, jnp.minimum((s + 1) * r, last), col_off + j)))
    return specs


def _ext(prev_ref, main_ref, next_ref, first, last):
    main = main_ref[...].astype(F32)
    zeros = jnp.zeros((HALO, main.shape[1]), F32)
    top = zeros if prev_ref is None else jnp.where(first, 0.0, prev_ref[...].astype(F32))
    bot = zeros if next_ref is None else jnp.where(last, 0.0, next_ref[...].astype(F32))
    return jnp.concatenate([top, main, bot], axis=0)


def _shift(xe, k):
    return pltpu.roll(xe, k % xe.shape[0], axis=0)


def _cconv(xe, w):
    return w[0:1, :] * _shift(xe, 2) + w[1:2, :] * _shift(xe, 1) + w[2:3, :] * xe


def _main(xe, ts):
    return xe[HALO:HALO + ts, :]


def _cconv_t(de, w):
    return w[2:3, :] * de + w[1:2, :] * _shift(de, -1) + w[0:1, :] * _shift(de, -2)


def _conv_wgrad(dw_ref, de, xe, ts):
    d = _main(de, ts)
    dw_ref[0:1, :] += jnp.sum(d * _main(_shift(xe, 2), ts), axis=0, keepdims=True)
    dw_ref[1:2, :] += jnp.sum(d * _main(_shift(xe, 1), ts), axis=0, keepdims=True)
    dw_ref[2:3, :] += jnp.sum(d * _main(xe, ts), axis=0, keepdims=True)


def _gate_tiles(S, C, rows=1024):
    return _tile(S, rows), _tile(C, 512)


def _shortconv_fwd(bch, conv_w, name):
    B, S, D3 = bch.shape
    D = D3 // 3
    ts, tc = _gate_tiles(S, D, rows=2048)
    nj, ns = D // tc, S // ts

    def body(b_ref, cp_ref, c_ref, hp_ref, h_ref, w_ref, o_ref):
        first = pl.program_id(2) == 0
        ce = _ext(cp_ref, c_ref, None, first, False)
        he = _ext(hp_ref, h_ref, None, first, False)
        cv = _main(_cconv(ce * he, w_ref[...]), ts)
        o_ref[...] = (b_ref[...].astype(F32) * cv).astype(BF16)

    in_specs = (_halo_specs(S, ts, tc, 0, False, False) + _halo_specs(S, ts, tc, nj, True, False)
                + _halo_specs(S, ts, tc, 2 * nj, True, False) + [pl.BlockSpec((3, tc), lambda j, b, s: (0, j))])
    return pl.pallas_call(
        body, name=name, grid=(nj, B, ns), in_specs=in_specs,
        out_specs=pl.BlockSpec((None, ts, tc), lambda j, b, s: (b, s, j)),
        out_shape=jax.ShapeDtypeStruct((B, S, D), BF16),
        compiler_params=pltpu.CompilerParams(
            dimension_semantics=("parallel", "parallel", "parallel"),
            vmem_limit_bytes=_vmem(12 * _nbytes((ts + 2 * HALO, tc), F32))),
    )(bch, bch, bch, bch, bch, conv_w)


def _shortconv_bwd(bch, dg, conv_w, name):
    B, S, D3 = bch.shape
    D = D3 // 3
    ts, tc = _gate_tiles(S, D)
    nj, ns = D // tc, S // ts

    def body(b_ref, bn_ref, cp_ref, c_ref, cn_ref, hp_ref, h_ref, hn_ref, d_ref, dn_ref, w_ref, o_ref, dw_ref):
        s = pl.program_id(2)
        first, last = s == 0, s == ns - 1

        @pl.when(jnp.logical_and(pl.program_id(1) == 0, s == 0))
        def _():
            dw_ref[...] = jnp.zeros_like(dw_ref)

        w = w_ref[...]
        be = _ext(None, b_ref, bn_ref, first, last)
        ce = _ext(cp_ref, c_ref, cn_ref, first, last)
        he = _ext(hp_ref, h_ref, hn_ref, first, last)
        de = _ext(None, d_ref, dn_ref, first, last)
        ch = ce * he
        dcv = de * be
        dch = _main(_cconv_t(dcv, w), ts)
        o_ref[0] = (_main(de, ts) * _main(_cconv(ch, w), ts)).astype(BF16)
        o_ref[1] = (dch * _main(he, ts)).astype(BF16)
        o_ref[2] = (dch * _main(ce, ts)).astype(BF16)
        _conv_wgrad(dw_ref, dcv, ch, ts)

    in_specs = (_halo_specs(S, ts, tc, 0, False, True) + _halo_specs(S, ts, tc, nj, True, True)
                + _halo_specs(S, ts, tc, 2 * nj, True, True) + _halo_specs(S, ts, tc, 0, False, True)
                + [pl.BlockSpec((3, tc), lambda j, b, s: (0, j))])
    return pl.pallas_call(
        body, name=name, grid=(nj, B, ns), in_specs=in_specs,
        out_specs=[pl.BlockSpec((3, None, ts, tc), lambda j, b, s: (0, b, s, j)),
                   pl.BlockSpec((3, tc), lambda j, b, s: (0, j))],
        out_shape=[jax.ShapeDtypeStruct((3, B, S, D), BF16), jax.ShapeDtypeStruct((3, D), F32)],
        compiler_params=pltpu.CompilerParams(
            dimension_semantics=("parallel", "arbitrary", "arbitrary"),
            vmem_limit_bytes=_vmem(24 * _nbytes((ts + 2 * HALO, tc), F32))),
    )(bch, bch, bch, bch, bch, bch, bch, bch, dg, dg, conv_w)


def _sigmoid(x):
    return 1.0 / (1.0 + jnp.exp(-x))


def _ffn_gate_fwd(u0, conv_w, conv_b, name):
    B, S, F2 = u0.shape
    F = F2 // 2
    ts, tc = _gate_tiles(S, F, rows=2048)
    nj, ns = F // tc, S // ts

    def body(gp_ref, g_ref, up_ref, u_ref, wg_ref, wu_ref, bg_ref, bu_ref, o_ref):
        first = pl.program_id(2) == 0
        ug = _main(_cconv(_ext(gp_ref, g_ref, None, first, False), wg_ref[...]), ts) + bg_ref[...]
        uu = _main(_cconv(_ext(up_ref, u_ref, None, first, False), wu_ref[...]), ts) + bu_ref[...]
        o_ref[...] = (ug * _sigmoid(ug) * uu).astype(BF16)

    in_specs = (_halo_specs(S, ts, tc, 0, True, False) + _halo_specs(S, ts, tc, nj, True, False)
                + [pl.BlockSpec((3, tc), lambda j, b, s: (0, j)), pl.BlockSpec((3, tc), lambda j, b, s: (0, nj + j)),
                   pl.BlockSpec((1, tc), lambda j, b, s: (0, j)), pl.BlockSpec((1, tc), lambda j, b, s: (0, nj + j))])
    return pl.pallas_call(
        body, name=name, grid=(nj, B, ns), in_specs=in_specs,
        out_specs=pl.BlockSpec((None, ts, tc), lambda j, b, s: (b, s, j)),
        out_shape=jax.ShapeDtypeStruct((B, S, F), BF16),
        compiler_params=pltpu.CompilerParams(
            dimension_semantics=("parallel", "parallel", "parallel"),
            vmem_limit_bytes=_vmem(12 * _nbytes((ts + 2 * HALO, tc), F32))),
    )(u0, u0, u0, u0, conv_w, conv_w, conv_b, conv_b)


def _ffn_gate_bwd(u0, dact, conv_w, conv_b, name):
    B, S, F2 = u0.shape
    F = F2 // 2
    ts, tc = _gate_tiles(S, F)
    nj, ns = F // tc, S // ts

    def body(gp_ref, g_ref, gn_ref, up_ref, u_ref, un_ref, d_ref, dn_ref, wg_ref, wu_ref, bg_ref, bu_ref,
             o_ref, dwg_ref, dwu_ref, dbg_ref, dbu_ref):
        s = pl.program_id(2)
        first, last = s == 0, s == ns - 1

        @pl.when(jnp.logical_and(pl.program_id(1) == 0, s == 0))
        def _():
            for r in (dwg_ref, dwu_ref, dbg_ref, dbu_ref):
                r[...] = jnp.zeros_like(r)

        wg, wu = wg_ref[...], wu_ref[...]
        ge = _ext(gp_ref, g_ref, gn_ref, first, last)
        ue = _ext(up_ref, u_ref, un_ref, first, last)
        de = _ext(None, d_ref, dn_ref, first, last)
        ug = _cconv(ge, wg) + bg_ref[...]
        uu = _cconv(ue, wu) + bu_ref[...]
        sig = _sigmoid(ug)
        dug = de * uu * (sig * (1.0 + ug * (1.0 - sig)))
        duu = de * (ug * sig)
        o_ref[0] = _main(_cconv_t(dug, wg), ts).astype(BF16)
        o_ref[1] = _main(_cconv_t(duu, wu), ts).astype(BF16)
        _conv_wgrad(dwg_ref, dug, ge, ts)
        _conv_wgrad(dwu_ref, duu, ue, ts)
        dbg_ref[...] += jnp.sum(_main(dug, ts), axis=0, keepdims=True)
        dbu_ref[...] += jnp.sum(_main(duu, ts), axis=0, keepdims=True)

    w3 = lambda off: pl.BlockSpec((3, tc), lambda j, b, s: (0, off + j))
    w1 = lambda off: pl.BlockSpec((1, tc), lambda j, b, s: (0, off + j))
    in_specs = (_halo_specs(S, ts, tc, 0, True, True) + _halo_specs(S, ts, tc, nj, True, True)
                + _halo_specs(S, ts, tc, 0, False, True) + [w3(0), w3(nj), w1(0), w1(nj)])
    outs = pl.pallas_call(
        body, name=name, grid=(nj, B, ns), in_specs=in_specs,
        out_specs=[pl.BlockSpec((2, None, ts, tc), lambda j, b, s: (0, b, s, j)), w3(0), w3(0), w1(0), w1(0)],
        out_shape=[jax.ShapeDtypeStruct((2, B, S, F), BF16), jax.ShapeDtypeStruct((3, F), F32),
                   jax.ShapeDtypeStruct((3, F), F32), jax.ShapeDtypeStruct((1, F), F32),
                   jax.ShapeDtypeStruct((1, F), F32)],
        compiler_params=pltpu.CompilerParams(
            dimension_semantics=("parallel", "arbitrary", "arbitrary"),
            vmem_limit_bytes=_vmem(30 * _nbytes((ts + 2 * HALO, tc), F32))),
    )(u0, u0, u0, u0, u0, u0, dact, dact, conv_w, conv_w, conv_b, conv_b)
    du0, dwg, dwu, dbg, dbu = outs
    return du0, jnp.concatenate([dwg, dwu], axis=1), jnp.concatenate([dbg, dbu], axis=1)


def _t5_bucket(dist):
    max_exact = REL_BUCKETS // 2
    n = jnp.maximum(dist, 0)
    nf = jnp.maximum(n, max_exact).astype(F32)
    large = max_exact + (jnp.log(nf / max_exact) / math.log(REL_MAX_DISTANCE / max_exact)
                         * (REL_BUCKETS - max_exact)).astype(jnp.int32)
    large = jnp.minimum(large, REL_BUCKETS - 1)
    return jnp.where(n < max_exact, n, large)


def _band_tables(window, dilation):
    P = ATT_BLOCK
    qi = jnp.arange(P, dtype=jnp.int32)[:, None]
    kc = jnp.arange(2 * P, dtype=jnp.int32)[None, :]
    delta = qi + P - kc
    band = (delta >= 0) & (delta <= window // dilation)
    bucket = _t5_bucket(delta * dilation).reshape(-1)
    onehot = (bucket[None, :] == jnp.arange(REL_BUCKETS, dtype=jnp.int32)[:, None]).astype(F32)
    return onehot, band


def _bias_lookup(rel_bias_t, onehots, name):
    nb, _, Q = onehots.shape
    H = rel_bias_t.shape[0]

    def body(r_ref, oh_ref, o_ref):
        o_ref[...] = lax.dot_general(r_ref[...], oh_ref[...], (((1,), (0,)), ((), ())),
                                     precision=lax.Precision.HIGHEST, preferred_element_type=F32)

    return pl.pallas_call(
        body, name=name, grid=(nb,),
        in_specs=[pl.BlockSpec((H, REL_BUCKETS), lambda i: (0, 0)),
                  pl.BlockSpec((None, REL_BUCKETS, Q), lambda i: (i, 0, 0))],
        out_specs=pl.BlockSpec((None, H, Q), lambda i: (i, 0, 0)),
        out_shape=jax.ShapeDtypeStruct((nb, H, Q), F32),
        compiler_params=pltpu.CompilerParams(dimension_semantics=("parallel",),
                                             vmem_limit_bytes=_vmem(_nbytes((REL_BUCKETS + H, Q), F32))),
    )(rel_bias_t, onehots)


def _bias_grad(dtabs, onehots, name):
    nb, H, Q = dtabs.shape

    def body(d_ref, oh_ref, o_ref):
        @pl.when(pl.program_id(0) == 0)
        def _():
            o_ref[...] = jnp.zeros_like(o_ref)

        o_ref[...] += lax.dot_general(d_ref[...], oh_ref[...], (((1,), (1,)), ((), ())),
                                      precision=lax.Precision.HIGHEST, preferred_element_type=F32)

    return pl.pallas_call(
        body, name=name, grid=(nb,),
        in_specs=[pl.BlockSpec((None, H, Q), lambda i: (i, 0, 0)),
                  pl.BlockSpec((None, REL_BUCKETS, Q), lambda i: (i, 0, 0))],
        out_specs=pl.BlockSpec((H, REL_BUCKETS), lambda i: (0, 0)),
        out_shape=jax.ShapeDtypeStruct((H, REL_BUCKETS), F32),
        compiler_params=pltpu.CompilerParams(dimension_semantics=("arbitrary",),
                                             vmem_limit_bytes=_vmem(_nbytes((REL_BUCKETS + H, Q), F32))),
    )(dtabs, onehots)


def _lane_masks():
    lane = lax.broadcasted_iota(jnp.int32, (1, LANE), 1)
    lo = lane < LANE // 2
    return lo, jnp.logical_not(lo)


def _dot_nt(a, b):
    return lax.dot_general(a, b, (((1,), (1,)), ((), ())), preferred_element_type=F32)


def _dot_nn(a, b):
    return lax.dot_general(a, b, (((1,), (0,)), ((), ())), preferred_element_type=F32)


def _dot_tn(a, b):
    return lax.dot_general(a, b, (((0,), (0,)), ((), ())), preferred_element_type=F32)


def _block_rows(n, dilation, S):
    P = ATT_BLOCK
    nb = S // (dilation * P)
    r, i = n // nb, n % nb
    cur = pl.ds(i * (P * dilation) + r, P, stride=dilation)
    prv = pl.ds(jnp.maximum(i - 1, 0) * (P * dilation) + r, P, stride=dilation)
    return cur, prv, jnp.minimum(i, 1)


def _attn_fwd(q, kv, tabs, name):
    B, S, D = q.shape
    P, H = ATT_BLOCK, N_HEADS
    scale = (D // H) ** -0.5
    assert math.log2(scale).is_integer(), "q is scaled before its bf16 rounding: exact only for a power of two"
    pipe_steps = S // P // ATT_PIPE
    assert pipe_steps % 2 == 0

    def body(q_ref, k_ref, v_ref, tab_ref, o_ref, lse_ref, acc_ref, m_ref, l_ref, sa_ref, sb_ref):
        lo, hi = _lane_masks()
        for bi, (_, d) in enumerate(DILATED_BRANCHES):
            def rows_of(n, d=d):
                return [_block_rows(n + j * pipe_steps, d, S) for j in range(ATT_PIPE)]

            def scores(n, s_ref, bi=bi):
                for j, (cur, prv, variant) in enumerate(rows_of(n)):
                    qb = (q_ref[cur, :] * scale).astype(BF16)
                    kw = jnp.concatenate([k_ref[prv, :], k_ref[cur, :]], axis=0).astype(BF16)
                    for e, msk in enumerate((lo, hi)):
                        s_ref[j, e] = _dot_nt(jnp.where(msk, qb, 0), kw) + tab_ref[bi, variant, e]

            def softmax_values(n, s_ref, bi=bi):
                results = []
                for j, (cur, prv, variant) in enumerate(rows_of(n)):
                    vw = jnp.concatenate([v_ref[prv, :], v_ref[cur, :]], axis=0).astype(BF16)
                    m_pair = l_pair = a_pair = None
                    for e, msk in enumerate((lo, hi)):
                        s = s_ref[j, e]
                        m_e = jnp.max(s, axis=-1, keepdims=True)
                        pe = jnp.exp(s - m_e)
                        l_e = jnp.sum(pe, axis=-1, keepdims=True)
                        a_e = _dot_nn(pe.astype(BF16), jnp.where(msk, vw, 0))
                        m_pair = m_e if e == 0 else jnp.where(lo, m_pair, m_e)
                        l_pair = l_e if e == 0 else jnp.where(lo, l_pair, l_e)
                        a_pair = a_e if e == 0 else a_pair + a_e
                    results.append((cur, m_pair, l_pair, a_pair))
                for cur, m_pair, l_pair, a_pair in results:
                    m_ref[bi, cur, :] = m_pair
                    l_ref[bi, cur, :] = l_pair
                    acc_ref[bi, cur, :] = a_pair

            scores(0, sa_ref)

            def pair(t, carry):
                n = 2 * t
                scores(n + 1, sb_ref)
                softmax_values(n, sa_ref)
                scores(jnp.minimum(n + 2, pipe_steps - 1), sa_ref)
                softmax_values(n + 1, sb_ref)
                return carry

            lax.fori_loop(0, pipe_steps // 2, pair, 0)

        nbr = len(DILATED_BRANCHES)
        chunk = 512

        def merge(t, carry):
            rs = pl.ds(pl.multiple_of(t * chunk, chunk), chunk)
            ms = [m_ref[b, rs, :] for b in range(nbr)]
            m = functools.reduce(jnp.maximum, ms)
            ws = [jnp.exp(mb - m) for mb in ms]
            l = functools.reduce(jnp.add, [w * l_ref[b, rs, :] for b, w in enumerate(ws)])
            acc = functools.reduce(jnp.add, [w * acc_ref[b, rs, :] for b, w in enumerate(ws)])
            o_ref[rs, :] = (acc / l).astype(BF16)
            lse_ref[rs, :] = m + jnp.log(l)
            return carry

        lax.fori_loop(0, S // chunk, merge, 0)

    nl = D // LANE
    col = lambda off: pl.BlockSpec((None, S, LANE), lambda b, p: (b, 0, off + p))
    tab_spec = pl.BlockSpec((3, 2, 2, P, 2 * P), lambda b, p: (0, 0, p, 0, 0))
    return pl.pallas_call(
        body, name=name, grid=(B, H // 2), in_specs=[col(0), col(0), col(nl), tab_spec],
        out_specs=[col(0), col(0)],
        out_shape=[jax.ShapeDtypeStruct((B, S, D), BF16), jax.ShapeDtypeStruct((B, S, D), F32)],
        scratch_shapes=[pltpu.VMEM((len(DILATED_BRANCHES), S, LANE), F32)] * 3
        + [pltpu.VMEM((ATT_PIPE, 2, P, 2 * P), F32)] * 2,
        compiler_params=pltpu.CompilerParams(
            dimension_semantics=("parallel", "parallel"), vmem_limit_bytes=VMEM_CAP),
    )(q, kv, kv, tabs)


def _attn_bwd(q, kv, do, o, lse, tabs, name):
    B, S, D = q.shape
    P, H = ATT_BLOCK, N_HEADS
    scale = (D // H) ** -0.5
    assert math.log2(scale).is_integer(), "q is scaled before its bf16 rounding: exact only for a power of two"
    half = LANE // 2
    nsteps = S // P // ATT_UNROLL

    def body(q_ref, k_ref, v_ref, do_ref, o_ref, lse_ref, tab_ref, dq_out, dkv_out, dtab_ref, delta_ref,
             dq_ref, dkv_ref):
        lo, hi = _lane_masks()

        @pl.when(pl.program_id(1) == 0)
        def _():
            dtab_ref[...] = jnp.zeros_like(dtab_ref)

        dkv_ref[...] = jnp.zeros_like(dkv_ref)
        prod = do_ref[...] * o_ref[...].astype(F32)
        delta_ref[...] = jnp.where(lo, jnp.sum(jnp.where(lo, prod, 0.0), axis=-1, keepdims=True),
                                   jnp.sum(jnp.where(hi, prod, 0.0), axis=-1, keepdims=True))

        for bi, (_, d) in enumerate(DILATED_BRANCHES):
            def block(n, carry, bi=bi, d=d):
                rows = [_block_rows(n + j * nsteps, d, S) for j in range(ATT_UNROLL)]
                loaded = []
                for cur, prv, variant in rows:
                    qb = (q_ref[cur, :] * scale).astype(BF16)
                    kw = jnp.concatenate([k_ref[prv, :], k_ref[cur, :]], axis=0).astype(BF16)
                    vw = jnp.concatenate([v_ref[prv, :], v_ref[cur, :]], axis=0).astype(BF16)
                    dob = do_ref[cur, :].astype(BF16)
                    dq_old = dq_ref[cur, :] if bi > 0 else None
                    loaded.append((qb, kw, vw, dob, lse_ref[cur, :], delta_ref[cur, :], dq_old))
                results = []
                for j, ((cur, prv, variant), (qb, kw, vw, dob, lse_b, dl_b, dq_old)) in enumerate(zip(rows, loaded)):
                    dq_pair = dk_pair = dv_pair = None
                    for e, msk in enumerate((lo, hi)):
                        qm = jnp.where(msk, qb, 0)
                        dom = jnp.where(msk, dob, 0)
                        s = _dot_nt(qm, kw) + tab_ref[bi, variant, e]
                        pe = jnp.exp(s - lse_b[:, e * half:e * half + 1])
                        ds = pe * (_dot_nt(dom, vw) - dl_b[:, e * half:e * half + 1])
                        dtab_ref[bi, e] += ds
                        dsb = ds.astype(BF16)
                        dq_e = _dot_nn(dsb, jnp.where(msk, kw, 0))
                        dk_e = _dot_tn(dsb, qm)
                        dv_e = _dot_tn(pe.astype(BF16), dom)
                        dq_pair = dq_e if e == 0 else dq_pair + dq_e
                        dk_pair = dk_e if e == 0 else dk_pair + dk_e
                        dv_pair = dv_e if e == 0 else dv_pair + dv_e
                    dq_pair = dq_pair * scale
                    if bi > 0:
                        dq_pair = dq_pair + dq_old
                    results.append((dq_pair, dk_pair, dv_pair))
                for (cur, prv, variant), (dq_pair, dk_pair, dv_pair) in zip(rows, results):
                    dq_ref[cur, :] = dq_pair
                    dkv_ref[0, cur, :] += dk_pair[P:, :]
                    dkv_ref[1, cur, :] += dv_pair[P:, :]
                    dkv_ref[0, prv, :] += dk_pair[:P, :]
                    dkv_ref[1, prv, :] += dv_pair[:P, :]
                return carry

            lax.fori_loop(0, nsteps, block, 0)
        dq_out[...] = dq_ref[...].astype(BF16)
        dkv_out[...] = dkv_ref[...].astype(BF16)

    nl = D // LANE
    col = lambda off: pl.BlockSpec((None, S, LANE), lambda p, b: (b, 0, off + p))
    tab_spec = pl.BlockSpec((3, 2, 2, P, 2 * P), lambda p, b: (0, 0, p, 0, 0))
    return pl.pallas_call(
        body, name=name, grid=(H // 2, B),
        in_specs=[col(0), col(0), col(nl), col(0), col(0), col(0), tab_spec],
        out_specs=[col(0), pl.BlockSpec((2, None, S, LANE), lambda p, b: (0, b, 0, p)),
                   pl.BlockSpec((None, 3, 2, P, 2 * P), lambda p, b: (p, 0, 0, 0, 0))],
        out_shape=[jax.ShapeDtypeStruct((B, S, D), BF16), jax.ShapeDtypeStruct((2, B, S, D), BF16),
                   jax.ShapeDtypeStruct((H // 2, 3, 2, P, 2 * P), F32)],
        scratch_shapes=[pltpu.VMEM((S, LANE), F32), pltpu.VMEM((S, LANE), F32), pltpu.VMEM((2, S, LANE), F32)],
        compiler_params=pltpu.CompilerParams(
            dimension_semantics=("parallel", "arbitrary"),
            vmem_limit_bytes=VMEM_CAP),
    )(q, kv, kv, do, o, lse, tabs)


def _mesh_place():
    x, y, c = lax.axis_index("x"), lax.axis_index("y"), lax.axis_index("c")
    return x, y, c, (x, y, 1 - c), [(1 - x, y), (x, 1 - y), (1 - x, 1 - y)]


def _rdma(src, dst, send_sems, recv_sems, idx, to):
    return pltpu.make_async_remote_copy(
        src_ref=src, dst_ref=dst, send_sem=send_sems.at[idx], recv_sem=recv_sems.at[idx],
        device_id=to, device_id_type=pl.DeviceIdType.MESH)


def _comm_call(body, name, args, out_shape, n_sems, n_local):
    any_spec = pl.BlockSpec(memory_space=pl.ANY)
    return pl.pallas_call(
        body, name=name, in_specs=[any_spec] * len(args), out_specs=[any_spec] * len(out_shape),
        out_shape=out_shape,
        scratch_shapes=[pltpu.SemaphoreType.DMA(n_sems), pltpu.SemaphoreType.DMA(n_sems),
                        pltpu.SemaphoreType.DMA((n_local,))],
        compiler_params=pltpu.CompilerParams(has_side_effects=True),
    )(*args)


def _all_gather(arrays, name):
    n = len(arrays)

    def body(*refs):
        ins, outs = refs[:n], refs[n:2 * n]
        send_sems, recv_sems, loc_sems = refs[2 * n:]
        x, y, c, sib, chips = _mesh_place()
        me = (x, y, c)

        def rows(a, dev):
            return outs[a].at[4 * dev[0] + 2 * dev[1] + dev[2]]

        def copy(a, k, block, to, src=None):
            return _rdma(rows(a, block) if src is None else src, rows(a, block), send_sems, recv_sems, (a, k), to)

        local, first, passed = [], [], []
        for a in range(n):
            cp = pltpu.make_async_copy(ins[a], rows(a, me), loc_sems.at[a])
            cp.start()
            local.append(cp)
            first.append(copy(a, 0, me, sib, src=ins[a]))
            first += [copy(a, 1 + j, me, (*chip, c), src=ins[a]) for j, chip in enumerate(chips)]
        for cp in first:
            cp.start()
        for j, chip in enumerate(chips):
            for a in range(n):
                copy(a, 1 + j, (*chip, c), me).wait_recv()
                cp = copy(a, 4 + j, (*chip, c), sib)
                cp.start()
                passed.append(cp)
        for a in range(n):
            copy(a, 0, sib, me).wait_recv()
            for j, chip in enumerate(chips):
                copy(a, 4 + j, (*chip, 1 - c), me).wait_recv()
        for cp in first + passed:
            cp.wait_send()
        for cp in local:
            cp.wait()

    out_shape = [jax.ShapeDtypeStruct((N_DEV,) + g.shape, g.dtype) for g in arrays]
    return list(_comm_call(body, name, arrays, out_shape, (n, N_DEV - 1), n))


def _core_pair_exchange(halves, gathered, name):
    nh, ng = len(halves), len(gathered)

    def body(*refs):
        h_in, g_in = refs[:nh], refs[nh:nh + ng]
        outs = refs[nh + ng:nh + ng + nh + ng]
        theirs, g_out = outs[:nh], outs[nh:]
        send_sems, recv_sems, loc_sems = refs[nh + ng + nh + ng:]
        x, y, c, sib, chips = _mesh_place()
        me = 4 * x + 2 * y + c
        peers = [sib] + [(*chip, pc) for chip in chips for pc in (c, 1 - c)]
        nchip = N_DEV // 2
        gbase = nh * nchip
        local, sends = [], []
        for a in range(nh):
            for ch in range(nchip):
                cp = _rdma(h_in[a].at[ch, 1 - c], theirs[a].at[ch], send_sems, recv_sems, a * nchip + ch, sib)
                cp.start()
                sends.append(cp)
        for g in range(ng):
            cp = pltpu.make_async_copy(g_in[g], g_out[g].at[me], loc_sems.at[g])
            cp.start()
            local.append(cp)
            for k, peer in enumerate(peers):
                cp = _rdma(g_in[g], g_out[g].at[me], send_sems, recv_sems, gbase + g * (N_DEV - 1) + k, peer)
                cp.start()
                sends.append(cp)
        for cp in sends:
            cp.wait_send()
        for a in range(nh):
            for ch in range(nchip):
                _rdma(h_in[a].at[ch, 1 - c], theirs[a].at[ch], send_sems, recv_sems, a * nchip + ch, sib).wait_recv()
        for g in range(ng):
            for k, peer in enumerate(peers):
                pid = 4 * peer[0] + 2 * peer[1] + peer[2]
                _rdma(g_in[g], g_out[g].at[pid], send_sems, recv_sems, gbase + g * (N_DEV - 1) + k, peer).wait_recv()
        for cp in local:
            cp.wait()

    half = [jax.ShapeDtypeStruct((h.shape[0],) + h.shape[2:], h.dtype) for h in halves]
    out_shape = half + [jax.ShapeDtypeStruct((N_DEV,) + g.shape, g.dtype) for g in gathered]
    outs = _comm_call(body, name, list(halves) + list(gathered), out_shape,
                      (nh * (N_DEV // 2) + ng * (N_DEV - 1),), max(ng, 1))
    return list(outs[:nh]), list(outs[nh:])


def _gather_copies(src_refs, land_refs, send_sems, recv_sems):
    x, y, c, sib, chips = _mesh_place()
    me = 4 * x + 2 * y + c
    peers = [sib] + [(*chip, c) for chip in chips]
    return [pltpu.make_async_remote_copy(
                src_ref=src, dst_ref=land.at[me], send_sem=send_sems[4 * a + k], recv_sem=recv_sems[4 * a + k],
                device_id=peer, device_id_type=pl.DeviceIdType.MESH)
            for a, (src, land) in enumerate(zip(src_refs, land_refs)) for k, peer in enumerate(peers)]


def _forward_to_sibling(gathered, name):
    n = len(gathered)

    def body(*refs):
        outs = refs[n:2 * n]
        send_sems, recv_sems, _ = refs[2 * n:]
        x, y, c, sib, chips = _mesh_place()
        sends = []
        for a in range(n):
            for j, chip in enumerate(chips):
                rows = outs[a].at[4 * chip[0] + 2 * chip[1] + c]
                cp = _rdma(rows, rows, send_sems, recv_sems, (a, j), sib)
                cp.start()
                sends.append(cp)
        for cp in sends:
            cp.wait_send()
        for a in range(n):
            for j, chip in enumerate(chips):
                rows = outs[a].at[4 * chip[0] + 2 * chip[1] + (1 - c)]
                _rdma(rows, rows, send_sems, recv_sems, (a, j), sib).wait_recv()

    any_spec = pl.BlockSpec(memory_space=pl.ANY)
    return list(pl.pallas_call(
        body, name=name, in_specs=[any_spec] * n, out_specs=[any_spec] * n,
        out_shape=[jax.ShapeDtypeStruct(g.shape, g.dtype) for g in gathered],
        input_output_aliases={i: i for i in range(n)},
        scratch_shapes=[pltpu.SemaphoreType.DMA((n, 3)), pltpu.SemaphoreType.DMA((n, 3)),
                        pltpu.SemaphoreType.DMA((1,))],
        compiler_params=pltpu.CompilerParams(has_side_effects=True),
    )(*gathered))


def _split_start(copies_fn, srcs, land_shapes, n_sems, name, after=None):
    ns, nl = len(srcs), len(land_shapes)
    nc = n_sems
    hbm = pl.BlockSpec(memory_space=pltpu.HBM)
    sem = pl.BlockSpec(memory_space=pltpu.SEMAPHORE)

    n_in = ns + nl + (0 if after is None else 1)

    def body(*refs):
        src_refs, land_refs = refs[:ns], refs[ns:ns + nl]
        send_sems, recv_sems = refs[n_in:n_in + nc], refs[n_in + nc:n_in + 2 * nc]
        token = refs[-1]
        for cp in copies_fn(src_refs, land_refs, send_sems, recv_sems):
            cp.start()
        token[...] = jnp.zeros_like(token)

    lands = [lax.empty(s.shape, s.dtype) for s in land_shapes]
    thru = [pltpu.HBM(t.shape, t.dtype) for t in list(srcs) + lands]
    order = [] if after is None else [after]
    outs = pl.pallas_call(
        body, name=name,
        out_shape=(*([pltpu.SemaphoreType.DMA(())] * (2 * nc)), *thru, jax.ShapeDtypeStruct((8, LANE), F32)),
        in_specs=[hbm] * (ns + nl) + [pl.BlockSpec(memory_space=pl.ANY)] * len(order),
        out_specs=(*([sem] * (2 * nc)), *([hbm] * (ns + nl)), pl.BlockSpec(memory_space=pltpu.VMEM)),
        input_output_aliases={i: 2 * nc + i for i in range(ns + nl)},
        compiler_params=pltpu.CompilerParams(has_side_effects=pltpu.SideEffectType.DATAFLOW_SIDE_EFFECTING),
    )(*[pltpu.with_memory_space_constraint(t, pltpu.HBM) for t in list(srcs) + lands], *order)
    thru_out = outs[2 * nc:2 * nc + ns + nl]
    return list(outs[:nc]), list(outs[nc:2 * nc]), list(thru_out[:ns]), list(thru_out[ns:]), outs[-1]


def _split_wait(copies_fn, send_sems, recv_sems, srcs, lands, after, name):
    ns, nl, nc = len(srcs), len(lands), len(send_sems)
    hbm = pl.BlockSpec(memory_space=pltpu.HBM)
    sem = pl.BlockSpec(memory_space=pltpu.SEMAPHORE)

    def body(*refs):
        src_refs, land_refs = refs[:ns], refs[ns:ns + nl]
        send_refs, recv_refs = refs[ns + nl:ns + nl + nc], refs[ns + nl + nc:ns + nl + 2 * nc]
        for cp in copies_fn(src_refs, land_refs, send_refs, recv_refs):
            cp.wait_send()
            cp.wait_recv()

    outs = pl.pallas_call(
        body, name=name,
        out_shape=[pltpu.HBM(t.shape, t.dtype) for t in list(srcs) + list(lands)],
        in_specs=[hbm] * (ns + nl) + [sem] * (2 * nc) + [pl.BlockSpec(memory_space=pl.ANY)],
        out_specs=[hbm] * (ns + nl),
        input_output_aliases={i: i for i in range(ns + nl)},
        compiler_params=pltpu.CompilerParams(has_side_effects=pltpu.SideEffectType.DATAFLOW_SIDE_EFFECTING),
    )(*srcs, *lands, *send_sems, *recv_sems, after)
    return list(outs[:ns]), list(outs[ns:])


def _pair_copies(src_refs, land_refs, send_sems, recv_sems):
    x, y, c, sib, chips = _mesh_place()
    nchip = N_DEV // 2
    return [pltpu.make_async_remote_copy(
                src_ref=src.at[ch, 1 - c], dst_ref=land.at[ch], send_sem=send_sems[nchip * a + ch],
                recv_sem=recv_sems[nchip * a + ch], device_id=sib, device_id_type=pl.DeviceIdType.MESH)
            for a, (src, land) in enumerate(zip(src_refs, land_refs)) for ch in range(nchip)]


def _chip_copies(src_refs, land_refs, send_sems, recv_sems):
    x, y, c, sib, chips = _mesh_place()
    return [pltpu.make_async_remote_copy(
                src_ref=src.at[2 * chip[0] + chip[1]], dst_ref=land.at[j], send_sem=send_sems[3 * a + j],
                recv_sem=recv_sems[3 * a + j], device_id=(*chip, c), device_id_type=pl.DeviceIdType.MESH)
            for a, (src, land) in enumerate(zip(src_refs, land_refs)) for j, chip in enumerate(chips)]


def _pair_sum(halves, theirs, core, name, tr=256):
    nchip, _, R, C = halves.shape
    tr = _row_tile(R, tr)

    def body(core_ref, h_ref, t_ref, o_ref):
        o_ref[...] = (h_ref[...] + t_ref[...]).astype(BF16)

    grid_spec = pltpu.PrefetchScalarGridSpec(
        num_scalar_prefetch=1, grid=(nchip, R // tr),
        in_specs=[pl.BlockSpec((None, None, tr, C), lambda ch, i, core_ref: (ch, core_ref[0], i, 0)),
                  pl.BlockSpec((None, tr, C), lambda ch, i, core_ref: (ch, i, 0))],
        out_specs=pl.BlockSpec((None, tr, C), lambda ch, i, core_ref: (ch, i, 0)))
    return pl.pallas_call(
        body, name=name, grid_spec=grid_spec, out_shape=jax.ShapeDtypeStruct((nchip, R, C), BF16),
        compiler_params=pltpu.CompilerParams(
            dimension_semantics=("parallel", "parallel"), vmem_limit_bytes=_vmem(3 * _nbytes((tr, C), F32))),
    )(core, *[pltpu.with_memory_space_constraint(t, pltpu.HBM) for t in (halves, theirs)])


def _chip_sum(own, recv, chip, name, tr=256):
    _, R, C = own.shape
    tr = _row_tile(R, tr)

    def body(chip_ref, o_ref, r_ref, out_ref):
        acc = o_ref[...].astype(F32)
        for j in range(3):
            acc = acc + r_ref[j].astype(F32)
        out_ref[...] = acc

    grid_spec = pltpu.PrefetchScalarGridSpec(
        num_scalar_prefetch=1, grid=(R // tr,),
        in_specs=[pl.BlockSpec((None, tr, C), lambda i, chip_ref: (chip_ref[0], i, 0)),
                  pl.BlockSpec((3, tr, C), lambda i, chip_ref: (0, i, 0))],
        out_specs=pl.BlockSpec((tr, C), lambda i, chip_ref: (i, 0)))
    return pl.pallas_call(
        body, name=name, grid_spec=grid_spec, out_shape=jax.ShapeDtypeStruct((R, C), F32),
        compiler_params=pltpu.CompilerParams(
            dimension_semantics=("parallel",), vmem_limit_bytes=_vmem(4 * _nbytes((tr, C), F32))),
    )(chip, *[pltpu.with_memory_space_constraint(t, pltpu.HBM) for t in (own, recv)])


def _row_tile(rows, target):
    best = rows
    for t in range(8, min(rows, target) + 1, 8):
        if rows % t == 0:
            best = t
    return best


def _sum_slots(recv, name, tr=128):
    n, R, C = recv.shape
    tr = _row_tile(R, tr if recv.dtype == F32 else 2 * tr)

    def body(r_ref, o_ref):
        acc = r_ref[0].astype(F32)
        for k in range(1, n):
            acc = acc + r_ref[k].astype(F32)
        o_ref[...] = acc

    return pl.pallas_call(
        body, name=name, grid=(R // tr,),
        in_specs=[pl.BlockSpec((n, tr, C), lambda i: (0, i, 0))],
        out_specs=pl.BlockSpec((tr, C), lambda i: (i, 0)),
        out_shape=jax.ShapeDtypeStruct((R, C), F32),
        compiler_params=pltpu.CompilerParams(
            dimension_semantics=("parallel",), vmem_limit_bytes=_vmem(10 * _nbytes((tr, C), F32))),
    )(recv)


def _adamw_math(g_ref, w_ref, m_ref, v_ref, d_ref, nm_ref, nv_ref):
    gv = g_ref[...]
    nm = ADAM_B1 * m_ref[...] + (1.0 - ADAM_B1) * gv
    nv = ADAM_B2 * v_ref[...] + (1.0 - ADAM_B2) * (gv * gv)
    m_hat = nm / (1.0 - ADAM_B1 ** ADAM_STEP)
    v_hat = nv / (1.0 - ADAM_B2 ** ADAM_STEP)
    d_ref[...] = -ADAM_LR * (m_hat / (jnp.sqrt(v_hat) + ADAM_EPS) + ADAM_WD * w_ref[...])
    nm_ref[...] = nm
    nv_ref[...] = nv


def _chip_sum_adamw(own, recv, chip, w, m, v, name, tr=256):
    _, R, C = own.shape
    tr = _row_tile(R, tr)

    def body(chip_ref, o_ref, r_ref, w_ref, m_ref, v_ref, g_ref, d_ref, nm_ref, nv_ref):
        acc = o_ref[...].astype(F32)
        for j in range(3):
            acc = acc + r_ref[j].astype(F32)
        g_ref[...] = acc
        _adamw_math(g_ref, w_ref, m_ref, v_ref, d_ref, nm_ref, nv_ref)

    row = pl.BlockSpec((tr, C), lambda i, chip_ref: (i, 0))
    grid_spec = pltpu.PrefetchScalarGridSpec(
        num_scalar_prefetch=1, grid=(R // tr,),
        in_specs=[pl.BlockSpec((None, tr, C), lambda i, chip_ref: (chip_ref[0], i, 0)),
                  pl.BlockSpec((3, tr, C), lambda i, chip_ref: (0, i, 0)), row, row, row],
        out_specs=[row] * 4)
    return pl.pallas_call(
        body, name=name, grid_spec=grid_spec, out_shape=[jax.ShapeDtypeStruct((R, C), F32)] * 4,
        compiler_params=pltpu.CompilerParams(
            dimension_semantics=("parallel",), vmem_limit_bytes=_vmem(10 * _nbytes((tr, C), F32))),
    )(chip, own, recv, w, m, v)


def _adamw_small(params, name):
    n = len(params)
    two_d = [(-1, w.shape[-1]) for w, _, _, _ in params]
    flat = [t.reshape(two_d[i]) for i, (w, m, v, g) in enumerate(params) for t in (g, w, m, v)]

    def body(*refs):
        ins, outs = refs[:4 * n], refs[4 * n:]
        for i in range(n):
            _adamw_math(*ins[4 * i:4 * i + 4], *outs[3 * i:3 * i + 3])

    outs = pl.pallas_call(
        body, name=name,
        out_shape=[jax.ShapeDtypeStruct(flat[4 * i].shape, F32) for i in range(n) for _ in range(3)],
    )(*flat)
    return [(g.reshape(w.shape),) + tuple(outs[3 * i + t].reshape(w.shape) for t in range(3))
            for i, (w, m, v, g) in enumerate(params)]


def _adamw(g, w, m, v, name, tr=256):
    R, C = g.shape
    tr = _row_tile(R, tr)
    body = functools.partial(_adamw_math)

    row = pl.BlockSpec((tr, C), lambda i: (i, 0))
    return pl.pallas_call(
        body, name=name, grid=(R // tr,), in_specs=[row] * 4, out_specs=[row] * 3,
        out_shape=[jax.ShapeDtypeStruct((R, C), F32)] * 3,
        compiler_params=pltpu.CompilerParams(
            dimension_semantics=("parallel",), vmem_limit_bytes=_vmem(8 * _nbytes((tr, C), F32))),
    )(*[pltpu.with_memory_space_constraint(t, pltpu.HBM) for t in (g, w, m, v)])


def _pack_rows(parts):
    flat = jnp.concatenate([p.reshape(-1).astype(F32) for p in parts])
    rows = -(-flat.shape[0] // (8 * LANE)) * 8
    return jnp.pad(flat, (0, rows * LANE - flat.shape[0])).reshape(rows, LANE)


def _unpack(flat, shapes):
    out, off = [], 0
    for shp in shapes:
        n = math.prod(shp)
        out.append(flat[off:off + n].reshape(shp))
        off += n
    return out


def kernel(x, a_norm, a_w_in, a_conv, a_w_out, kv_norm, w_kv, b_norm, b_w_q, b_w_o, rel_bias, ffn_norm, ffn_w_up, ffn_conv, ffn_conv_b, ffn_w_down, final_norm, loss_target, m_a_norm, m_a_w_in, m_a_conv, m_a_w_out, m_kv_norm, m_w_kv, m_b_norm, m_b_w_q, m_b_w_o, m_rel_bias, m_ffn_norm, m_ffn_w_up, m_ffn_conv, m_ffn_conv_b, m_ffn_w_down, m_final_norm, v_a_norm, v_a_w_in, v_a_conv, v_a_w_out, v_kv_norm, v_w_kv, v_b_norm, v_b_w_q, v_b_w_o, v_rel_bias, v_ffn_norm, v_ffn_w_up, v_ffn_conv, v_ffn_conv_b, v_ffn_w_down, v_final_norm):
    B, S, D = x.shape
    T = B * S
    F = ffn_w_down.shape[1] * N_DEV
    me = 4 * lax.axis_index("x") + 2 * lax.axis_index("y") + lax.axis_index("c")

    big_shards = [a_w_in[0].T, a_w_out[0], w_kv.T, b_w_q[0], b_w_o[0],
                  ffn_w_up[0].T, ffn_w_up[1].T, ffn_w_down[0], ffn_w_down[1]]
    small_shapes = [a_norm.shape, a_conv.shape, ffn_conv.shape]
    small_pack = _pack_rows([a_norm, a_conv, ffn_conv])
    shards = [s.astype(BF16) for s in big_shards]
    first = _all_gather([shards[0], shards[1], small_pack], "gather_weights_first")
    win_t, wout = first[0].reshape(-1, D), first[1].reshape(-1, D)
    smalls = [_unpack(first[2][j].reshape(-1), small_shapes) for j in range(N_DEV)]
    a_norm_f = jnp.concatenate([s[0] for s in smalls], axis=-1)
    a_conv_f = jnp.concatenate([s[1] for s in smalls], axis=-1)[0]
    ffn_conv_f = jnp.concatenate([s[2] for s in smalls], axis=-1)

    def gather_start(tag, idxs, after):
        srcs = [shards[i] for i in idxs]
        lands = [jax.ShapeDtypeStruct((N_DEV,) + s.shape, s.dtype) for s in srcs]
        return _split_start(_gather_copies, srcs, lands, 4 * len(srcs), f"gather_start_{tag}", after=after)

    def gather_finish(tag, handle, after):
        send, recv, srcs, lands, _ = handle
        srcs, lands = _split_wait(_gather_copies, send, recv, srcs, lands, after, f"gather_wait_{tag}")
        lands = _forward_to_sibling(lands, f"gather_forward_{tag}")
        return [lax.dynamic_update_slice(g, s[None], (me, 0, 0)).reshape(-1, D) for g, s in zip(lands, srcs)]

    ffn0_w = gather_start("ffn0", [5, 7], first[0])
    rest_w = gather_start("rest", [2, 3, 4, 6, 8], ffn0_w[4])

    x2 = x.reshape(T, D)
    (xn0,) = _rmsnorm_fwd(x2, a_norm_f, "a_norm_fwd")
    bch = _matmul(xn0, win_t, mode="nt", out_dtype=BF16, after=rest_w[4], name="a_in_proj", tm=2048).reshape(B, S, 3 * D)
    gated = _shortconv_fwd(bch, a_conv_f, "a_gate_fwd").reshape(T, D)
    h1 = _matmul(gated, wout, mode="nn", out_dtype=F32, add=x2, name="a_out_proj")
    wup0_t, wdn0 = gather_finish("ffn0", ffn0_w, h1)
    wup_t, wdn = [wup0_t, None], [wdn0, None]

    def ffn_fwd(h, l):
        (xn,) = _rmsnorm_fwd(h, ffn_norm[l:l + 1], f"ffn{l}_norm_fwd")
        u0 = _matmul(xn, wup_t[l], mode="nt", out_dtype=BF16, name=f"ffn{l}_up", tm=2048, tn=1408).reshape(B, S, 2 * F)
        act = _ffn_gate_fwd(u0, ffn_conv_f[l], ffn_conv_b[l:l + 1], f"ffn{l}_gate_fwd").reshape(T, F)
        out = _matmul(act, wdn[l], mode="nn", out_dtype=F32, add=h, name=f"ffn{l}_down", tk=2816)
        return out, (xn, u0, act)

    h2, ffn0_saved = ffn_fwd(h1, 0)
    wkv_t, wq, wo, wup_t[1], wdn[1] = gather_finish("rest", rest_w, h2)
    kvn, xnb = _rmsnorm_fwd(h2, jnp.stack([kv_norm, b_norm[0]]), "kv_b_norm_fwd")
    kv = _matmul(kvn, wkv_t, mode="nt", out_dtype=F32, name="kv_proj", tm=2048).reshape(B, S, 2 * D)
    q = _matmul(xnb, wq, mode="nn", out_dtype=F32, name="q_proj", tm=2048).reshape(B, S, D)

    tables = [_band_tables(w, d) for (w, d) in DILATED_BRANCHES]
    onehots = jnp.stack([t[0] for t in tables])
    P = ATT_BLOCK
    bias_vals = _bias_lookup(rel_bias.T, onehots, "rel_bias_lookup").reshape(3, N_HEADS, P, 2 * P)
    in_cur = (jnp.arange(2 * P) >= P)[None, :]
    masks = jnp.stack([jnp.stack([t[1] & in_cur, t[1]]) for t in tables])
    tabs = jnp.where(masks[:, :, None], bias_vals[:, None], NEG)

    attn3, lse3 = _attn_fwd(q, kv, tabs, "attn_fwd")
    attn = attn3.reshape(T, D)
    h3 = _matmul(attn, wo, mode="nn", out_dtype=F32, add=h2, name="o_proj")
    h4, ffn1_saved = ffn_fwd(h3, 1)

    dh4, dh4b, d_final_norm, sq = _final_loss_bwd(h4, final_norm.reshape(1, D), loss_target.reshape(T, D), "loss_bwd")
    loss_part = 0.5 * jnp.sum(sq) / D

    core_idx = lax.axis_index("c").astype(jnp.int32).reshape(1)
    chip_idx = (2 * lax.axis_index("x") + lax.axis_index("y")).astype(jnp.int32).reshape(1)
    in_flight = []

    def group_start(tag, indexed_grads):
        halves = [g.reshape(N_DEV // 2, 2, g.shape[0] // N_DEV, D) for _, g in indexed_grads]
        lands = [jax.ShapeDtypeStruct((h.shape[0],) + h.shape[2:], h.dtype) for h in halves]
        send, recv, srcs, lnd, token = _split_start(_pair_copies, halves, lands, (N_DEV // 2) * len(halves),
                                                    f"exchange_core_start_{tag}")
        return (tag, [i for i, _ in indexed_grads], send, recv, srcs, lnd), token

    def group_finish(handle, after):
        tag, idxs, send, recv, srcs, lnd = handle
        halves, theirs = _split_wait(_pair_copies, send, recv, srcs, lnd, after, f"exchange_core_wait_{tag}")
        sums = [_pair_sum(h_, t_, core_idx, f"grad_core_sum_{tag}{k}")
                for k, (h_, t_) in enumerate(zip(halves, theirs))]
        lands = [jax.ShapeDtypeStruct((3,) + s.shape[1:], s.dtype) for s in sums]
        send, recv, srcs, lnd, token = _split_start(_chip_copies, sums, lands, 3 * len(sums),
                                                    f"exchange_chip_start_{tag}")
        in_flight.append((tag, idxs, send, recv, srcs, lnd))
        return token

    def ffn_bwd(dh_out, dh_out_b, h_in, saved, l, after):
        xn, u0, act = saved
        dact = _matmul(dh_out_b, wdn[l], mode="nt", out_dtype=BF16, after=after, name=f"ffn{l}_down_dx",
                       tm=2048, tn=1408).reshape(B, S, F)
        d_wdn = _matmul(act, dh_out_b, mode="tn", out_dtype=F32, name=f"ffn{l}_down_dw", tm=1408, tn=1024, tk=2048)
        du0, d_conv, d_conv_b = _ffn_gate_bwd(u0, dact, ffn_conv_f[l], ffn_conv_b[l:l + 1], f"ffn{l}_gate_bwd")
        du0 = du0.reshape(2, T, F)
        dxn = _matmul(du0, wup_t[l], mode="nn", out_dtype=BF16, a_parts=2, name=f"ffn{l}_up_dx", tk=2816)
        d_wup_t = _matmul(du0, xn, mode="tn", out_dtype=F32, a_parts=2, name=f"ffn{l}_up_dw", tm=1408, tn=1024, tk=2048)
        dh_in, dh_in_b, d_norm = _rmsnorm_bwd(h_in, ffn_norm[l:l + 1], [dxn], dh_out, f"ffn{l}_norm_bwd")
        return (dh_in, dh_in_b), (d_wdn, d_wup_t, d_conv, d_conv_b, d_norm)

    (dh3, dh3b), ffn1_grads = ffn_bwd(dh4, dh4b, h3, ffn1_saved, 1, None)
    group, token = group_start("ffn1", [(6, ffn1_grads[1]), (8, ffn1_grads[0])])

    dattn = _matmul(dh3b, wo, mode="nt", out_dtype=F32, after=token, name="o_proj_dx", tm=2048)
    token = group_finish(group, dattn)
    dattn = dattn.reshape(B, S, D)
    d_wo = _matmul(attn, dh3b, mode="tn", out_dtype=F32, after=token, name="o_proj_dw", tm=1024, tn=1024, tk=2048)
    dq, dkv, dtab = _attn_bwd(q, kv, dattn, attn3, lse3, tabs, "attn_bwd")
    dtabs = dtab.transpose(1, 0, 2, 3, 4).reshape(3, N_HEADS, P * 2 * P)
    d_rel_bias = _bias_grad(dtabs, onehots, "rel_bias_grad").T
    dq2 = dq.reshape(T, D)
    dkv2 = dkv.reshape(2, T, D)
    d_wq = _matmul(xnb, dq2, mode="tn", out_dtype=F32, name="q_proj_dw", tm=1024, tn=1024, tk=2048)
    d_wkv_t = _matmul(dkv2, kvn, mode="tn", out_dtype=F32, a_parts=2, name="kv_proj_dw", tm=1024, tn=1024, tk=2048)
    group, token = group_start("attn", [(2, d_wkv_t), (3, d_wq), (4, d_wo)])
    dxnb = _matmul(dq2, wq, mode="nt", out_dtype=BF16, after=token, name="q_proj_dx")
    dkvn = _matmul(dkv2, wkv_t, mode="nn", out_dtype=BF16, a_parts=2, name="kv_proj_dx")
    dh2, dh2b, d_kvb_norm = _rmsnorm_bwd(h2, jnp.stack([kv_norm, b_norm[0]]), [dkvn, dxnb], dh3, "kv_b_norm_bwd")
    token = group_finish(group, dh2b)

    (dh1, dh1b), ffn0_grads = ffn_bwd(dh2, dh2b, h1, ffn0_saved, 0, token)
    group, token = group_start("ffn0", [(5, ffn0_grads[1]), (7, ffn0_grads[0])])

    dgated = _matmul(dh1b, wout, mode="nt", out_dtype=BF16, after=token, name="a_out_proj_dx", tm=2048)
    token = group_finish(group, dgated)
    dgated = dgated.reshape(B, S, D)
    d_wout = _matmul(gated, dh1b, mode="tn", out_dtype=F32, after=token, name="a_out_proj_dw",
                     tm=1024, tn=1024, tk=2048)
    dbch, d_a_conv = _shortconv_bwd(bch, dgated, a_conv_f, "a_gate_bwd")
    dbch = dbch.reshape(3, T, D)
    dxn0 = _matmul(dbch, win_t, mode="nn", out_dtype=BF16, a_parts=3, name="a_in_proj_dx", tm=2048)
    d_win_t = _matmul(dbch, xn0, mode="tn", out_dtype=F32, a_parts=3, name="a_in_proj_dw", tm=1024, tn=1024, tk=2048)
    group, token = group_start("a", [(0, d_win_t), (1, d_wout)])
    grad_x, _, d_a_norm = _rmsnorm_bwd(x2, a_norm_f + token[0, 0], [dxn0], dh1, "a_norm_bwd")
    token = group_finish(group, grad_x)

    small_full = [d_a_norm, d_a_conv, jnp.stack([ffn0_grads[2], ffn1_grads[2]]),
                  d_kvb_norm[0], d_kvb_norm[1], d_rel_bias, jnp.concatenate([ffn0_grads[4], ffn1_grads[4]]),
                  jnp.concatenate([ffn0_grads[3], ffn1_grads[3]]), d_final_norm, loss_part]
    small_full_shapes = [(1, D), (3, D), (2, 3, 2 * F), (D,), (1, D), rel_bias.shape, (2, D), (2, 2 * F), (D,), ()]
    _, gath = _core_pair_exchange([], [_pack_rows(small_full)], "exchange_small")
    chip_sums, recv = {}, {}
    for tag, idxs, send, rcv, srcs, lnd in in_flight:
        srcs, lnd = _split_wait(_chip_copies, send, rcv, srcs, lnd, token, f"exchange_chip_wait_{tag}")
        for i, s_, l_ in zip(idxs, srcs, lnd):
            chip_sums[i], recv[i] = s_, l_
    small_sum = _sum_slots(gath[0], "small_grad_sum").reshape(-1)
    (g_a_norm_f, g_a_conv_f, g_ffn_conv_f, g_kv_norm, g_b_norm, g_rel_bias, g_ffn_norm, g_ffn_conv_b,
     g_final_norm, loss) = _unpack(small_sum, small_full_shapes)

    def my_cols(full, width):
        return lax.dynamic_slice_in_dim(full, me * width, width, axis=full.ndim - 1)

    g_a_norm = my_cols(g_a_norm_f, D // N_DEV)
    g_a_conv = my_cols(g_a_conv_f, D // N_DEV)[None]
    g_ffn_conv = my_cols(g_ffn_conv_f, 2 * F // N_DEV)

    big_w = [(a_w_in, m_a_w_in, v_a_w_in, True), (a_w_out, m_a_w_out, v_a_w_out, False),
             (w_kv, m_w_kv, v_w_kv, True), (b_w_q, m_b_w_q, v_b_w_q, False), (b_w_o, m_b_w_o, v_b_w_o, False),
             (ffn_w_up[0], m_ffn_w_up[0], v_ffn_w_up[0], True), (ffn_w_up[1], m_ffn_w_up[1], v_ffn_w_up[1], True),
             (ffn_w_down[0], m_ffn_w_down[0], v_ffn_w_down[0], False),
             (ffn_w_down[1], m_ffn_w_down[1], v_ffn_w_down[1], False)]
    big_out = []
    for i, (w, m, v, transposed) in enumerate(big_w):
        if transposed:
            g = _chip_sum(chip_sums[i], recv[i], chip_idx, f"grad_sum_{i}").T
            w2, m2, v2 = (t.reshape(g.shape) for t in (w, m, v))
            delta, nm, nv = _adamw(g, w2, m2, v2, f"adamw_{i}")
        else:
            w2, m2, v2 = (t.reshape(chip_sums[i].shape[1:]) for t in (w, m, v))
            g, delta, nm, nv = _chip_sum_adamw(chip_sums[i], recv[i], chip_idx, w2, m2, v2, f"grad_sum_adamw_{i}")
        big_out.append(tuple(t.reshape(w.shape) for t in (g, delta, nm, nv)))

    def pair(i, j):
        return tuple(jnp.stack([big_out[i][t], big_out[j][t]]) for t in range(4))

    small_w = [(a_norm, m_a_norm, v_a_norm, g_a_norm), (a_conv, m_a_conv, v_a_conv, g_a_conv),
               (ffn_conv, m_ffn_conv, v_ffn_conv, g_ffn_conv), (kv_norm, m_kv_norm, v_kv_norm, g_kv_norm),
               (b_norm, m_b_norm, v_b_norm, g_b_norm), (rel_bias, m_rel_bias, v_rel_bias, g_rel_bias),
               (ffn_norm, m_ffn_norm, v_ffn_norm, g_ffn_norm),
               (ffn_conv_b, m_ffn_conv_b, v_ffn_conv_b, g_ffn_conv_b),
               (final_norm, m_final_norm, v_final_norm, g_final_norm)]
    small_out = _adamw_small(small_w, "adamw_small")

    per_weight = [small_out[0], big_out[0], small_out[1], big_out[1], small_out[3], big_out[2], small_out[4],
                  big_out[3], big_out[4], small_out[5], small_out[6], pair(5, 6), small_out[2], small_out[7],
                  pair(7, 8), small_out[8]]
    outs = [loss, grad_x.reshape(B, S, D)]
    for t in range(4):
        outs.extend(pw[t] for pw in per_weight)
    return tuple(outs)
```

```python
import functools
import math

import jax
import jax.numpy as jnp
from jax import lax
from jax.experimental import pallas as pl
from jax.experimental.pallas import tpu as pltpu

F32 = jnp.float32
BF16 = jnp.bfloat16

N_DEV = 8
N_HEADS = 16
ATT_BLOCK = 128
ATT_PIPE = 2
ATT_UNROLL = 8
DILATED_BRANCHES = ((128, 1), (512, 4), (2048, 16))
REL_BUCKETS = 32
REL_MAX_DISTANCE = 2048
RMS_EPS = 1e-6
ADAM_LR = 0.001
ADAM_B1 = 0.9
ADAM_B2 = 0.999
ADAM_EPS = 1e-08
ADAM_WD = 0.01
ADAM_STEP = 10

LANE = 128
HALO = 16
NEG = -1e30
VMEM_CAP = 56 << 20


def _vmem(block_bytes):
    return int(min(VMEM_CAP, max(32 << 20, 3 * block_bytes + (8 << 20))))


def _nbytes(shape, dtype):
    return math.prod(shape) * jnp.dtype(dtype).itemsize


def _tile(dim, target):
    best = None
    t = LANE
    while t <= min(dim, target):
        if dim % t == 0:
            best = t
        t += LANE
    return best if best is not None else dim


def _matmul(a, b, *, mode, out_dtype, name, add=None, after=None, a_parts=1, tm=1024, tn=1024, tk=1024):
    P = a_parts
    if mode == "nn":
        M, K = (a.shape[0], a.shape[1]) if P == 1 else (a.shape[1], a.shape[2] * P)
        N = b.shape[1]
    elif mode == "nt":
        assert P == 1
        M, K = a.shape
        N = b.shape[0]
    else:
        K = a.shape[0] if P == 1 else a.shape[1]
        M = a.shape[1] if P == 1 else a.shape[2] * P
        N = b.shape[1]
    tm = _tile(M // P if mode == "tn" else M, tm)
    tn = _tile(N, tn)
    tk = _tile(K // P if mode == "nn" else K, tk)
    nm, nn_, nk = M // tm, N // tn, K // tk

    if mode == "nn":
        if P == 1:
            a_spec = pl.BlockSpec((tm, tk), lambda i, j, k: (i, k))
        else:
            nkp = nk // P
            a_spec = pl.BlockSpec((None, tm, tk), lambda i, j, k: (k // nkp, i, k % nkp))
        b_spec = pl.BlockSpec((tk, tn), lambda i, j, k: (k, j))
        dims = (((1,), (0,)), ((), ()))
    elif mode == "nt":
        a_spec = pl.BlockSpec((tm, tk), lambda i, j, k: (i, k))
        b_spec = pl.BlockSpec((tn, tk), lambda i, j, k: (j, k))
        dims = (((1,), (1,)), ((), ()))
    else:
        if P == 1:
            a_spec = pl.BlockSpec((tk, tm), lambda i, j, k: (k, i))
        else:
            nmp = nm // P
            a_spec = pl.BlockSpec((None, tk, tm), lambda i, j, k: (i // nmp, k, i % nmp))
        b_spec = pl.BlockSpec((tk, tn), lambda i, j, k: (k, j))
        dims = (((0,), (0,)), ((), ()))
    o_spec = pl.BlockSpec((tm, tn), lambda i, j, k: (i, j))
    in_specs = [a_spec, b_spec]
    args = [a, b]
    if add is not None:
        in_specs.append(o_spec)
        args.append(add)
    has_add = add is not None
    if after is not None:
        in_specs.append(pl.BlockSpec(memory_space=pl.ANY))
        args.append(after)
    n_extra = len(args) - 2

    direct = nk > 1 and out_dtype == F32 and not has_add

    def body(a_ref, b_ref, *rest):
        add_ref = rest[0] if has_add else None
        o_ref = rest[n_extra]
        acc_ref = o_ref if direct or nk == 1 else rest[-1]
        k = pl.program_id(2)

        def part():
            return lax.dot_general(a_ref[...].astype(BF16), b_ref[...].astype(BF16), dims,
                                   preferred_element_type=F32)

        def finish(r):
            if has_add:
                r = r + add_ref[...].astype(F32)
            o_ref[...] = r.astype(out_dtype)

        if nk == 1:
            finish(part())
        else:
            @pl.when(k == 0)
            def _():
                acc_ref[...] = part()

            @pl.when(jnp.logical_and(k > 0, jnp.logical_or(k < nk - 1, direct)))
            def _():
                acc_ref[...] += part()

            if not direct:
                @pl.when(k == nk - 1)
                def _():
                    finish(acc_ref[...] + part())

    blk = (_nbytes((tm, tk), a.dtype) + _nbytes((tk, tn), b.dtype) + _nbytes((tm, tn), out_dtype)
           + (_nbytes((tm, tn), add.dtype) if has_add else 0)) * 2 + 3 * _nbytes((tm, tn), F32)
    return pl.pallas_call(
        body, name=name, grid=(nm, nn_, nk),
        in_specs=in_specs, out_specs=o_spec,
        out_shape=jax.ShapeDtypeStruct((M, N), out_dtype),
        scratch_shapes=[] if direct or nk == 1 else [pltpu.VMEM((tm, tn), F32)],
        compiler_params=pltpu.CompilerParams(
            dimension_semantics=("parallel", "parallel", "arbitrary"), vmem_limit_bytes=_vmem(blk)),
    )(*args)


def _rmsnorm_fwd(x, gains, name, tm=1024):
    T, D = x.shape
    n = gains.shape[0]
    tm = _tile(T, tm)

    def body(x_ref, g_ref, *o_refs):
        xv = x_ref[...]
        xhat = xv * lax.rsqrt(jnp.mean(xv * xv, axis=-1, keepdims=True) + RMS_EPS)
        for i in range(n):
            o_refs[i][...] = (xhat * g_ref[i:i + 1, :]).astype(BF16)

    row = pl.BlockSpec((tm, D), lambda i: (i, 0))
    outs = pl.pallas_call(
        body, name=name, grid=(T // tm,),
        in_specs=[row, pl.BlockSpec((n, D), lambda i: (0, 0))],
        out_specs=[row] * n,
        out_shape=[jax.ShapeDtypeStruct((T, D), BF16)] * n,
        compiler_params=pltpu.CompilerParams(
            dimension_semantics=("parallel",), vmem_limit_bytes=_vmem(4 * _nbytes((tm, D), F32))),
    )(x, gains)
    return tuple(outs)


def _rmsnorm_bwd(x, gains, dxns, dres, name, tm=1024):
    T, D = x.shape
    n = gains.shape[0]
    tm = _tile(T, tm)

    def body(x_ref, g_ref, *rest):
        dxn_refs = rest[:n]
        dres_ref, dx_ref, dxb_ref, dg_ref = rest[n:]
        xv = x_ref[...]
        rstd = lax.rsqrt(jnp.mean(xv * xv, axis=-1, keepdims=True) + RMS_EPS)
        xhat = xv * rstd
        dx = dres_ref[...]

        @pl.when(pl.program_id(0) == 0)
        def _():
            dg_ref[...] = jnp.zeros_like(dg_ref)

        for i in range(n):
            dy = dxn_refs[i][...].astype(F32)
            dg_ref[i:i + 1, :] += jnp.sum(dy * xhat, axis=0, keepdims=True)
            dxh = dy * g_ref[i:i + 1, :]
            dx = dx + rstd * (dxh - xhat * jnp.mean(dxh * xhat, axis=-1, keepdims=True))
        dx_ref[...] = dx
        dxb_ref[...] = dx.astype(BF16)

    row = pl.BlockSpec((tm, D), lambda i: (i, 0))
    par = pl.BlockSpec((n, D), lambda i: (0, 0))
    return pl.pallas_call(
        body, name=name, grid=(T // tm,),
        in_specs=[row, par] + [row] * n + [row],
        out_specs=[row, row, par],
        out_shape=[jax.ShapeDtypeStruct((T, D), F32), jax.ShapeDtypeStruct((T, D), BF16),
                   jax.ShapeDtypeStruct((n, D), F32)],
        compiler_params=pltpu.CompilerParams(
            dimension_semantics=("arbitrary",), vmem_limit_bytes=_vmem((4 + n) * _nbytes((tm, D), F32))),
    )(x, gains, *dxns, dres)


def _final_loss_bwd(h, gain, target, name, tm=1024):
    T, D = h.shape
    tm = _tile(T, tm)

    def body(h_ref, g_ref, t_ref, dh_ref, dhb_ref, dg_ref, sq_ref):
        xv = h_ref[...]
        rstd = lax.rsqrt(jnp.mean(xv * xv, axis=-1, keepdims=True) + RMS_EPS)
        xhat = xv * rstd
        err = xhat * g_ref[...] - t_ref[...]

        @pl.when(pl.program_id(0) == 0)
        def _():
            dg_ref[...] = jnp.zeros_like(dg_ref)
            sq_ref[...] = jnp.zeros_like(sq_ref)

        sq_ref[...] += jnp.sum(err * err, axis=0, keepdims=True)
        dy = err * (1.0 / D)
        dg_ref[...] += jnp.sum(dy * xhat, axis=0, keepdims=True)
        dxh = dy * g_ref[...]
        dh = rstd * (dxh - xhat * jnp.mean(dxh * xhat, axis=-1, keepdims=True))
        dh_ref[...] = dh
        dhb_ref[...] = dh.astype(BF16)

    row = pl.BlockSpec((tm, D), lambda i: (i, 0))
    par = pl.BlockSpec((1, D), lambda i: (0, 0))
    return pl.pallas_call(
        body, name=name, grid=(T // tm,),
        in_specs=[row, par, row], out_specs=[row, row, par, par],
        out_shape=[jax.ShapeDtypeStruct((T, D), F32), jax.ShapeDtypeStruct((T, D), BF16),
                   jax.ShapeDtypeStruct((1, D), F32), jax.ShapeDtypeStruct((1, D), F32)],
        compiler_params=pltpu.CompilerParams(
            dimension_semantics=("arbitrary",), vmem_limit_bytes=_vmem(5 * _nbytes((tm, D), F32))),
    )(h, gain, target)


def _halo_specs(S, ts, tc, col_off, prev, nxt):
    r = ts // HALO
    last = S // HALO - 1
    specs = []
    if prev:
        specs.append(pl.BlockSpec((None, HALO, tc), lambda j, b, s: (b, jnp.maximum(s * r - 1, 0), col_off + j)))
    specs.append(pl.BlockSpec((None, ts, tc), lambda j, b, s: (b, s, col_off + j)))
    if nxt:
        specs.append(pl.BlockSpec((None, HALO, tc), lambda j, b, s: (b, jnp.minimum((s + 1) * r, last), col_off + j)))
    return specs


def _ext(prev_ref, main_ref, next_ref, first, last):
    main = main_ref[...].astype(F32)
    zeros = jnp.zeros((HALO, main.shape[1]), F32)
    top = zeros if prev_ref is None else jnp.where(first, 0.0, prev_ref[...].astype(F32))
    bot = zeros if next_ref is None else jnp.where(last, 0.0, next_ref[...].astype(F32))
    return jnp.concatenate([top, main, bot], axis=0)


def _shift(xe, k):
    return pltpu.roll(xe, k % xe.shape[0], axis=0)


def _cconv(xe, w):
    return w[0:1, :] * _shift(xe, 2) + w[1:2, :] * _shift(xe, 1) + w[2:3, :] * xe


def _main(xe, ts):
    return xe[HALO:HALO + ts, :]


def _cconv_t(de, w):
    return w[2:3, :] * de + w[1:2, :] * _shift(de, -1) + w[0:1, :] * _shift(de, -2)


def _conv_wgrad(dw_ref, de, xe, ts):
    d = _main(de, ts)
    dw_ref[0:1, :] += jnp.sum(d * _main(_shift(xe, 2), ts), axis=0, keepdims=True)
    dw_ref[1:2, :] += jnp.sum(d * _main(_shift(xe, 1), ts), axis=0, keepdims=True)
    dw_ref[2:3, :] += jnp.sum(d * _main(xe, ts), axis=0, keepdims=True)


def _gate_tiles(S, C, rows=1024):
    return _tile(S, rows), _tile(C, 512)


def _shortconv_fwd(bch, conv_w, name):
    B, S, D3 = bch.shape
    D = D3 // 3
    ts, tc = _gate_tiles(S, D, rows=2048)
    nj, ns = D // tc, S // ts

    def body(b_ref, cp_ref, c_ref, hp_ref, h_ref, w_ref, o_ref):
        first = pl.program_id(2) == 0
        ce = _ext(cp_ref, c_ref, None, first, False)
        he = _ext(hp_ref, h_ref, None, first, False)
        cv = _main(_cconv(ce * he, w_ref[...]), ts)
        o_ref[...] = (b_ref[...].astype(F32) * cv).astype(BF16)

    in_specs = (_halo_specs(S, ts, tc, 0, False, False) + _halo_specs(S, ts, tc, nj, True, False)
                + _halo_specs(S, ts, tc, 2 * nj, True, False) + [pl.BlockSpec((3, tc), lambda j, b, s: (0, j))])
    return pl.pallas_call(
        body, name=name, grid=(nj, B, ns), in_specs=in_specs,
        out_specs=pl.BlockSpec((None, ts, tc), lambda j, b, s: (b, s, j)),
        out_shape=jax.ShapeDtypeStruct((B, S, D), BF16),
        compiler_params=pltpu.CompilerParams(
            dimension_semantics=("parallel", "parallel", "parallel"),
            vmem_limit_bytes=_vmem(12 * _nbytes((ts + 2 * HALO, tc), F32))),
    )(bch, bch, bch, bch, bch, conv_w)


def _shortconv_bwd(bch, dg, conv_w, name):
    B, S, D3 = bch.shape
    D = D3 // 3
    ts, tc = _gate_tiles(S, D)
    nj, ns = D // tc, S // ts

    def body(b_ref, bn_ref, cp_ref, c_ref, cn_ref, hp_ref, h_ref, hn_ref, d_ref, dn_ref, w_ref, o_ref, dw_ref):
        s = pl.program_id(2)
        first, last = s == 0, s == ns - 1

        @pl.when(jnp.logical_and(pl.program_id(1) == 0, s == 0))
        def _():
            dw_ref[...] = jnp.zeros_like(dw_ref)

        w = w_ref[...]
        be = _ext(None, b_ref, bn_ref, first, last)
        ce = _ext(cp_ref, c_ref, cn_ref, first, last)
        he = _ext(hp_ref, h_ref, hn_ref, first, last)
        de = _ext(None, d_ref, dn_ref, first, last)
        ch = ce * he
        dcv = de * be
        dch = _main(_cconv_t(dcv, w), ts)
        o_ref[0] = (_main(de, ts) * _main(_cconv(ch, w), ts)).astype(BF16)
        o_ref[1] = (dch * _main(he, ts)).astype(BF16)
        o_ref[2] = (dch * _main(ce, ts)).astype(BF16)
        _conv_wgrad(dw_ref, dcv, ch, ts)

    in_specs = (_halo_specs(S, ts, tc, 0, False, True) + _halo_specs(S, ts, tc, nj, True, True)
                + _halo_specs(S, ts, tc, 2 * nj, True, True) + _halo_specs(S, ts, tc, 0, False, True)
                + [pl.BlockSpec((3, tc), lambda j, b, s: (0, j))])
    return pl.pallas_call(
        body, name=name, grid=(nj, B, ns), in_specs=in_specs,
        out_specs=[pl.BlockSpec((3, None, ts, tc), lambda j, b, s: (0, b, s, j)),
                   pl.BlockSpec((3, tc), lambda j, b, s: (0, j))],
        out_shape=[jax.ShapeDtypeStruct((3, B, S, D), BF16), jax.ShapeDtypeStruct((3, D), F32)],
        compiler_params=pltpu.CompilerParams(
            dimension_semantics=("parallel", "arbitrary", "arbitrary"),
            vmem_limit_bytes=_vmem(24 * _nbytes((ts + 2 * HALO, tc), F32))),
    )(bch, bch, bch, bch, bch, bch, bch, bch, dg, dg, conv_w)


def _sigmoid(x):
    return 1.0 / (1.0 + jnp.exp(-x))


def _ffn_gate_fwd(u0, conv_w, conv_b, name):
    B, S, F2 = u0.shape
    F = F2 // 2
    ts, tc = _gate_tiles(S, F, rows=2048)
    nj, ns = F // tc, S // ts

    def body(gp_ref, g_ref, up_ref, u_ref, wg_ref, wu_ref, bg_ref, bu_ref, o_ref):
        first = pl.program_id(2) == 0
        ug = _main(_cconv(_ext(gp_ref, g_ref, None, first, False), wg_ref[...]), ts) + bg_ref[...]
        uu = _main(_cconv(_ext(up_ref, u_ref, None, first, False), wu_ref[...]), ts) + bu_ref[...]
        o_ref[...] = (ug * _sigmoid(ug) * uu).astype(BF16)

    in_specs = (_halo_specs(S, ts, tc, 0, True, False) + _halo_specs(S, ts, tc, nj, True, False)
                + [pl.BlockSpec((3, tc), lambda j, b, s: (0, j)), pl.BlockSpec((3, tc), lambda j, b, s: (0, nj + j)),
                   pl.BlockSpec((1, tc), lambda j, b, s: (0, j)), pl.BlockSpec((1, tc), lambda j, b, s: (0, nj + j))])
    return pl.pallas_call(
        body, name=name, grid=(nj, B, ns), in_specs=in_specs,
        out_specs=pl.BlockSpec((None, ts, tc), lambda j, b, s: (b, s, j)),
        out_shape=jax.ShapeDtypeStruct((B, S, F), BF16),
        compiler_params=pltpu.CompilerParams(
            dimension_semantics=("parallel", "parallel", "parallel"),
            vmem_limit_bytes=_vmem(12 * _nbytes((ts + 2 * HALO, tc), F32))),
    )(u0, u0, u0, u0, conv_w, conv_w, conv_b, conv_b)


def _ffn_gate_bwd(u0, dact, conv_w, conv_b, name):
    B, S, F2 = u0.shape
    F = F2 // 2
    ts, tc = _gate_tiles(S, F)
    nj, ns = F // tc, S // ts

    def body(gp_ref, g_ref, gn_ref, up_ref, u_ref, un_ref, d_ref, dn_ref, wg_ref, wu_ref, bg_ref, bu_ref,
             o_ref, dwg_ref, dwu_ref, dbg_ref, dbu_ref):
        s = pl.program_id(2)
        first, last = s == 0, s == ns - 1

        @pl.when(jnp.logical_and(pl.program_id(1) == 0, s == 0))
        def _():
            for r in (dwg_ref, dwu_ref, dbg_ref, dbu_ref):
                r[...] = jnp.zeros_like(r)

        wg, wu = wg_ref[...], wu_ref[...]
        ge = _ext(gp_ref, g_ref, gn_ref, first, last)
        ue = _ext(up_ref, u_ref, un_ref, first, last)
        de = _ext(None, d_ref, dn_ref, first, last)
        ug = _cconv(ge, wg) + bg_ref[...]
        uu = _cconv(ue, wu) + bu_ref[...]
        sig = _sigmoid(ug)
        dug = de * uu * (sig * (1.0 + ug * (1.0 - sig)))
        duu = de * (ug * sig)
        o_ref[0] = _main(_cconv_t(dug, wg), ts).astype(BF16)
        o_ref[1] = _main(_cconv_t(duu, wu), ts).astype(BF16)
        _conv_wgrad(dwg_ref, dug, ge, ts)
        _conv_wgrad(dwu_ref, duu, ue, ts)
        dbg_ref[...] += jnp.sum(_main(dug, ts), axis=0, keepdims=True)
        dbu_ref[...] += jnp.sum(_main(duu, ts), axis=0, keepdims=True)

    w3 = lambda off: pl.BlockSpec((3, tc), lambda j, b, s: (0, off + j))
    w1 = lambda off: pl.BlockSpec((1, tc), lambda j, b, s: (0, off + j))
    in_specs = (_halo_specs(S, ts, tc, 0, True, True) + _halo_specs(S, ts, tc, nj, True, True)
                + _halo_specs(S, ts, tc, 0, False, True) + [w3(0), w3(nj), w1(0), w1(nj)])
    outs = pl.pallas_call(
        body, name=name, grid=(nj, B, ns), in_specs=in_specs,
        out_specs=[pl.BlockSpec((2, None, ts, tc), lambda j, b, s: (0, b, s, j)), w3(0), w3(0), w1(0), w1(0)],
        out_shape=[jax.ShapeDtypeStruct((2, B, S, F), BF16), jax.ShapeDtypeStruct((3, F), F32),
                   jax.ShapeDtypeStruct((3, F), F32), jax.ShapeDtypeStruct((1, F), F32),
                   jax.ShapeDtypeStruct((1, F), F32)],
        compiler_params=pltpu.CompilerParams(
            dimension_semantics=("parallel", "arbitrary", "arbitrary"),
            vmem_limit_bytes=_vmem(30 * _nbytes((ts + 2 * HALO, tc), F32))),
    )(u0, u0, u0, u0, u0, u0, dact, dact, conv_w, conv_w, conv_b, conv_b)
    du0, dwg, dwu, dbg, dbu = outs
    return du0, jnp.concatenate([dwg, dwu], axis=1), jnp.concatenate([dbg, dbu], axis=1)


def _t5_bucket(dist):
    max_exact = REL_BUCKETS // 2
    n = jnp.maximum(dist, 0)
    nf = jnp.maximum(n, max_exact).astype(F32)
    large = max_exact + (jnp.log(nf / max_exact) / math.log(REL_MAX_DISTANCE / max_exact)
                         * (REL_BUCKETS - max_exact)).astype(jnp.int32)
    large = jnp.minimum(large, REL_BUCKETS - 1)
    return jnp.where(n < max_exact, n, large)


def _band_tables(window, dilation):
    P = ATT_BLOCK
    qi = jnp.arange(P, dtype=jnp.int32)[:, None]
    kc = jnp.arange(2 * P, dtype=jnp.int32)[None, :]
    delta = qi + P - kc
    band = (delta >= 0) & (delta <= window // dilation)
    bucket = _t5_bucket(delta * dilation).reshape(-1)
    onehot = (bucket[None, :] == jnp.arange(REL_BUCKETS, dtype=jnp.int32)[:, None]).astype(F32)
    return onehot, band


def _bias_lookup(rel_bias_t, onehots, name):
    nb, _, Q = onehots.shape
    H = rel_bias_t.shape[0]

    def body(r_ref, oh_ref, o_ref):
        o_ref[...] = lax.dot_general(r_ref[...], oh_ref[...], (((1,), (0,)), ((), ())),
                                     precision=lax.Precision.HIGHEST, preferred_element_type=F32)

    return pl.pallas_call(
        body, name=name, grid=(nb,),
        in_specs=[pl.BlockSpec((H, REL_BUCKETS), lambda i: (0, 0)),
                  pl.BlockSpec((None, REL_BUCKETS, Q), lambda i: (i, 0, 0))],
        out_specs=pl.BlockSpec((None, H, Q), lambda i: (i, 0, 0)),
        out_shape=jax.ShapeDtypeStruct((nb, H, Q), F32),
        compiler_params=pltpu.CompilerParams(dimension_semantics=("parallel",),
                                             vmem_limit_bytes=_vmem(_nbytes((REL_BUCKETS + H, Q), F32))),
    )(rel_bias_t, onehots)


def _bias_grad(dtabs, onehots, name):
    nb, H, Q = dtabs.shape

    def body(d_ref, oh_ref, o_ref):
        @pl.when(pl.program_id(0) == 0)
        def _():
            o_ref[...] = jnp.zeros_like(o_ref)

        o_ref[...] += lax.dot_general(d_ref[...], oh_ref[...], (((1,), (1,)), ((), ())),
                                      precision=lax.Precision.HIGHEST, preferred_element_type=F32)

    return pl.pallas_call(
        body, name=name, grid=(nb,),
        in_specs=[pl.BlockSpec((None, H, Q), lambda i: (i, 0, 0)),
                  pl.BlockSpec((None, REL_BUCKETS, Q), lambda i: (i, 0, 0))],
        out_specs=pl.BlockSpec((H, REL_BUCKETS), lambda i: (0, 0)),
        out_shape=jax.ShapeDtypeStruct((H, REL_BUCKETS), F32),
        compiler_params=pltpu.CompilerParams(dimension_semantics=("arbitrary",),
                                             vmem_limit_bytes=_vmem(_nbytes((REL_BUCKETS + H, Q), F32))),
    )(dtabs, onehots)


def _lane_masks():
    lane = lax.broadcasted_iota(jnp.int32, (1, LANE), 1)
    lo = lane < LANE // 2
    return lo, jnp.logical_not(lo)


def _dot_nt(a, b):
    return lax.dot_general(a, b, (((1,), (1,)), ((), ())), preferred_element_type=F32)


def _dot_nn(a, b):
    return lax.dot_general(a, b, (((1,), (0,)), ((), ())), preferred_element_type=F32)


def _dot_tn(a, b):
    return lax.dot_general(a, b, (((0,), (0,)), ((), ())), preferred_element_type=F32)


def _block_rows(n, dilation, S):
    P = ATT_BLOCK
    nb = S // (dilation * P)
    r, i = n // nb, n % nb
    cur = pl.ds(i * (P * dilation) + r, P, stride=dilation)
    prv = pl.ds(jnp.maximum(i - 1, 0) * (P * dilation) + r, P, stride=dilation)
    return cur, prv, jnp.minimum(i, 1)


def _attn_fwd(q, kv, tabs, name):
    B, S, D = q.shape
    P, H = ATT_BLOCK, N_HEADS
    scale = (D // H) ** -0.5
    assert math.log2(scale).is_integer(), "q is scaled before its bf16 rounding: exact only for a power of two"
    pipe_steps = S // P // ATT_PIPE
    assert pipe_steps % 2 == 0

    def body(q_ref, k_ref, v_ref, tab_ref, o_ref, lse_ref, acc_ref, m_ref, l_ref, sa_ref, sb_ref):
        lo, hi = _lane_masks()
        for bi, (_, d) in enumerate(DILATED_BRANCHES):
            def rows_of(n, d=d):
                return [_block_rows(n + j * pipe_steps, d, S) for j in range(ATT_PIPE)]

            def scores(n, s_ref, bi=bi):
                for j, (cur, prv, variant) in enumerate(rows_of(n)):
                    qb = (q_ref[cur, :] * scale).astype(BF16)
                    kw = jnp.concatenate([k_ref[prv, :], k_ref[cur, :]], axis=0).astype(BF16)
                    for e, msk in enumerate((lo, hi)):
                        s_ref[j, e] = _dot_nt(jnp.where(msk, qb, 0), kw) + tab_ref[bi, variant, e]

            def softmax_values(n, s_ref, bi=bi):
                results = []
                for j, (cur, prv, variant) in enumerate(rows_of(n)):
                    vw = jnp.concatenate([v_ref[prv, :], v_ref[cur, :]], axis=0).astype(BF16)
                    m_pair = l_pair = a_pair = None
                    for e, msk in enumerate((lo, hi)):
                        s = s_ref[j, e]
                        m_e = jnp.max(s, axis=-1, keepdims=True)
                        pe = jnp.exp(s - m_e)
                        l_e = jnp.sum(pe, axis=-1, keepdims=True)
                        a_e = _dot_nn(pe.astype(BF16), jnp.where(msk, vw, 0))
                        m_pair = m_e if e == 0 else jnp.where(lo, m_pair, m_e)
                        l_pair = l_e if e == 0 else jnp.where(lo, l_pair, l_e)
                        a_pair = a_e if e == 0 else a_pair + a_e
                    results.append((cur, m_pair, l_pair, a_pair))
                for cur, m_pair, l_pair, a_pair in results:
                    m_ref[bi, cur, :] = m_pair
                    l_ref[bi, cur, :] = l_pair
                    acc_ref[bi, cur, :] = a_pair

            scores(0, sa_ref)

            def pair(t, carry):
                n = 2 * t
                scores(n + 1, sb_ref)
                softmax_values(n, sa_ref)
                scores(jnp.minimum(n + 2, pipe_steps - 1), sa_ref)
                softmax_values(n + 1, sb_ref)
                return carry

            lax.fori_loop(0, pipe_steps // 2, pair, 0)

        nbr = len(DILATED_BRANCHES)
        chunk = 512

        def merge(t, carry):
            rs = pl.ds(pl.multiple_of(t * chunk, chunk), chunk)
            ms = [m_ref[b, rs, :] for b in range(nbr)]
            m = functools.reduce(jnp.maximum, ms)
            ws = [jnp.exp(mb - m) for mb in ms]
            l = functools.reduce(jnp.add, [w * l_ref[b, rs, :] for b, w in enumerate(ws)])
            acc = functools.reduce(jnp.add, [w * acc_ref[b, rs, :] for b, w in enumerate(ws)])
            o_ref[rs, :] = (acc / l).astype(BF16)
            lse_ref[rs, :] = m + jnp.log(l)
            return carry

        lax.fori_loop(0, S // chunk, merge, 0)

    nl = D // LANE
    col = lambda off: pl.BlockSpec((None, S, LANE), lambda b, p: (b, 0, off + p))
    tab_spec = pl.BlockSpec((3, 2, 2, P, 2 * P), lambda b, p: (0, 0, p, 0, 0))
    return pl.pallas_call(
        body, name=name, grid=(B, H // 2), in_specs=[col(0), col(0), col(nl), tab_spec],
        out_specs=[col(0), col(0)],
        out_shape=[jax.ShapeDtypeStruct((B, S, D), BF16), jax.ShapeDtypeStruct((B, S, D), F32)],
        scratch_shapes=[pltpu.VMEM((len(DILATED_BRANCHES), S, LANE), F32)] * 3
        + [pltpu.VMEM((ATT_PIPE, 2, P, 2 * P), F32)] * 2,
        compiler_params=pltpu.CompilerParams(
            dimension_semantics=("parallel", "parallel"), vmem_limit_bytes=VMEM_CAP),
    )(q, kv, kv, tabs)


def _attn_bwd(q, kv, do, o, lse, tabs, name):
    B, S, D = q.shape
    P, H = ATT_BLOCK, N_HEADS
    scale = (D // H) ** -0.5
    assert math.log2(scale).is_integer(), "q is scaled before its bf16 rounding: exact only for a power of two"
    half = LANE // 2
    nsteps = S // P // ATT_UNROLL

    def body(q_ref, k_ref, v_ref, do_ref, o_ref, lse_ref, tab_ref, dq_out, dkv_out, dtab_ref, delta_ref,
             dq_ref, dkv_ref):
        lo, hi = _lane_masks()

        @pl.when(pl.program_id(1) == 0)
        def _():
            dtab_ref[...] = jnp.zeros_like(dtab_ref)

        dkv_ref[...] = jnp.zeros_like(dkv_ref)
        prod = do_ref[...] * o_ref[...].astype(F32)
        delta_ref[...] = jnp.where(lo, jnp.sum(jnp.where(lo, prod, 0.0), axis=-1, keepdims=True),
                                   jnp.sum(jnp.where(hi, prod, 0.0), axis=-1, keepdims=True))

        for bi, (_, d) in enumerate(DILATED_BRANCHES):
            def block(n, carry, bi=bi, d=d):
                rows = [_block_rows(n + j * nsteps, d, S) for j in range(ATT_UNROLL)]
                loaded = []
                for cur, prv, variant in rows:
                    qb = (q_ref[cur, :] * scale).astype(BF16)
                    kw = jnp.concatenate([k_ref[prv, :], k_ref[cur, :]], axis=0).astype(BF16)
                    vw = jnp.concatenate([v_ref[prv, :], v_ref[cur, :]], axis=0).astype(BF16)
                    dob = do_ref[cur, :].astype(BF16)
                    dq_old = dq_ref[cur, :] if bi > 0 else None
                    loaded.append((qb, kw, vw, dob, lse_ref[cur, :], delta_ref[cur, :], dq_old))
                results = []
                for j, ((cur, prv, variant), (qb, kw, vw, dob, lse_b, dl_b, dq_old)) in enumerate(zip(rows, loaded)):
                    dq_pair = dk_pair = dv_pair = None
                    for e, msk in enumerate((lo, hi)):
                        qm = jnp.where(msk, qb, 0)
                        dom = jnp.where(msk, dob, 0)
                        s = _dot_nt(qm, kw) + tab_ref[bi, variant, e]
                        pe = jnp.exp(s - lse_b[:, e * half:e * half + 1])
                        ds = pe * (_dot_nt(dom, vw) - dl_b[:, e * half:e * half + 1])
                        dtab_ref[bi, e] += ds
                        dsb = ds.astype(BF16)
                        dq_e = _dot_nn(dsb, jnp.where(msk, kw, 0))
                        dk_e = _dot_tn(dsb, qm)
                        dv_e = _dot_tn(pe.astype(BF16), dom)
                        dq_pair = dq_e if e == 0 else dq_pair + dq_e
                        dk_pair = dk_e if e == 0 else dk_pair + dk_e
                        dv_pair = dv_e if e == 0 else dv_pair + dv_e
                    dq_pair = dq_pair * scale
                    if bi > 0:
                        dq_pair = dq_pair + dq_old
                    results.append((dq_pair, dk_pair, dv_pair))
                for (cur, prv, variant), (dq_pair, dk_pair, dv_pair) in zip(rows, results):
                    dq_ref[cur, :] = dq_pair
                    dkv_ref[0, cur, :] += dk_pair[P:, :]
                    dkv_ref[1, cur, :] += dv_pair[P:, :]
                    dkv_ref[0, prv, :] += dk_pair[:P, :]
                    dkv_ref[1, prv, :] += dv_pair[:P, :]
                return carry

            lax.fori_loop(0, nsteps, block, 0)
        dq_out[...] = dq_ref[...].astype(BF16)
        dkv_out[...] = dkv_ref[...].astype(BF16)

    nl = D // LANE
    col = lambda off: pl.BlockSpec((None, S, LANE), lambda p, b: (b, 0, off + p))
    tab_spec = pl.BlockSpec((3, 2, 2, P, 2 * P), lambda p, b: (0, 0, p, 0, 0))
    return pl.pallas_call(
        body, name=name, grid=(H // 2, B),
        in_specs=[col(0), col(0), col(nl), col(0), col(0), col(0), tab_spec],
        out_specs=[col(0), pl.BlockSpec((2, None, S, LANE), lambda p, b: (0, b, 0, p)),
                   pl.BlockSpec((None, 3, 2, P, 2 * P), lambda p, b: (p, 0, 0, 0, 0))],
        out_shape=[jax.ShapeDtypeStruct((B, S, D), BF16), jax.ShapeDtypeStruct((2, B, S, D), BF16),
                   jax.ShapeDtypeStruct((H // 2, 3, 2, P, 2 * P), F32)],
        scratch_shapes=[pltpu.VMEM((S, LANE), F32), pltpu.VMEM((S, LANE), F32), pltpu.VMEM((2, S, LANE), F32)],
        compiler_params=pltpu.CompilerParams(
            dimension_semantics=("parallel", "arbitrary"),
            vmem_limit_bytes=VMEM_CAP),
    )(q, kv, kv, do, o, lse, tabs)


def _mesh_place():
    x, y, c = lax.axis_index("x"), lax.axis_index("y"), lax.axis_index("c")
    return x, y, c, (x, y, 1 - c), [(1 - x, y), (x, 1 - y), (1 - x, 1 - y)]


def _rdma(src, dst, send_sems, recv_sems, idx, to):
    return pltpu.make_async_remote_copy(
        src_ref=src, dst_ref=dst, send_sem=send_sems.at[idx], recv_sem=recv_sems.at[idx],
        device_id=to, device_id_type=pl.DeviceIdType.MESH)


def _comm_call(body, name, args, out_shape, n_sems, n_local):
    any_spec = pl.BlockSpec(memory_space=pl.ANY)
    return pl.pallas_call(
        body, name=name, in_specs=[any_spec] * len(args), out_specs=[any_spec] * len(out_shape),
        out_shape=out_shape,
        scratch_shapes=[pltpu.SemaphoreType.DMA(n_sems), pltpu.SemaphoreType.DMA(n_sems),
                        pltpu.SemaphoreType.DMA((n_local,))],
        compiler_params=pltpu.CompilerParams(has_side_effects=True),
    )(*args)


def _all_gather(arrays, name):
    n = len(arrays)

    def body(*refs):
        ins, outs = refs[:n], refs[n:2 * n]
        send_sems, recv_sems, loc_sems = refs[2 * n:]
        x, y, c, sib, chips = _mesh_place()
        me = (x, y, c)

        def rows(a, dev):
            return outs[a].at[4 * dev[0] + 2 * dev[1] + dev[2]]

        def copy(a, k, block, to, src=None):
            return _rdma(rows(a, block) if src is None else src, rows(a, block), send_sems, recv_sems, (a, k), to)

        local, first, passed = [], [], []
        for a in range(n):
            cp = pltpu.make_async_copy(ins[a], rows(a, me), loc_sems.at[a])
            cp.start()
            local.append(cp)
            first.append(copy(a, 0, me, sib, src=ins[a]))
            first += [copy(a, 1 + j, me, (*chip, c), src=ins[a]) for j, chip in enumerate(chips)]
        for cp in first:
            cp.start()
        for j, chip in enumerate(chips):
            for a in range(n):
                copy(a, 1 + j, (*chip, c), me).wait_recv()
                cp = copy(a, 4 + j, (*chip, c), sib)
                cp.start()
                passed.append(cp)
        for a in range(n):
            copy(a, 0, sib, me).wait_recv()
            for j, chip in enumerate(chips):
                copy(a, 4 + j, (*chip, 1 - c), me).wait_recv()
        for cp in first + passed:
            cp.wait_send()
        for cp in local:
            cp.wait()

    out_shape = [jax.ShapeDtypeStruct((N_DEV,) + g.shape, g.dtype) for g in arrays]
    return list(_comm_call(body, name, arrays, out_shape, (n, N_DEV - 1), n))


def _core_pair_exchange(halves, gathered, name):
    nh, ng = len(halves), len(gathered)

    def body(*refs):
        h_in, g_in = refs[:nh], refs[nh:nh + ng]
        outs = refs[nh + ng:nh + ng + nh + ng]
        theirs, g_out = outs[:nh], outs[nh:]
        send_sems, recv_sems, loc_sems = refs[nh + ng + nh + ng:]
        x, y, c, sib, chips = _mesh_place()
        me = 4 * x + 2 * y + c
        peers = [sib] + [(*chip, pc) for chip in chips for pc in (c, 1 - c)]
        nchip = N_DEV // 2
        gbase = nh * nchip
        local, sends = [], []
        for a in range(nh):
            for ch in range(nchip):
                cp = _rdma(h_in[a].at[ch, 1 - c], theirs[a].at[ch], send_sems, recv_sems, a * nchip + ch, sib)
                cp.start()
                sends.append(cp)
        for g in range(ng):
            cp = pltpu.make_async_copy(g_in[g], g_out[g].at[me], loc_sems.at[g])
            cp.start()
            local.append(cp)
            for k, peer in enumerate(peers):
                cp = _rdma(g_in[g], g_out[g].at[me], send_sems, recv_sems, gbase + g * (N_DEV - 1) + k, peer)
                cp.start()
                sends.append(cp)
        for cp in sends:
            cp.wait_send()
        for a in range(nh):
            for ch in range(nchip):
                _rdma(h_in[a].at[ch, 1 - c], theirs[a].at[ch], send_sems, recv_sems, a * nchip + ch, sib).wait_recv()
        for g in range(ng):
            for k, peer in enumerate(peers):
                pid = 4 * peer[0] + 2 * peer[1] + peer[2]
                _rdma(g_in[g], g_out[g].at[pid], send_sems, recv_sems, gbase + g * (N_DEV - 1) + k, peer).wait_recv()
        for cp in local:
            cp.wait()

    half = [jax.ShapeDtypeStruct((h.shape[0],) + h.shape[2:], h.dtype) for h in halves]
    out_shape = half + [jax.ShapeDtypeStruct((N_DEV,) + g.shape, g.dtype) for g in gathered]
    outs = _comm_call(body, name, list(halves) + list(gathered), out_shape,
                      (nh * (N_DEV // 2) + ng * (N_DEV - 1),), max(ng, 1))
    return list(outs[:nh]), list(outs[nh:])


def _gather_copies(src_refs, land_refs, send_sems, recv_sems):
    x, y, c, sib, chips = _mesh_place()
    me = 4 * x + 2 * y + c
    peers = [sib] + [(*chip, c) for chip in chips]
    return [pltpu.make_async_remote_copy(
                src_ref=src, dst_ref=land.at[me], send_sem=send_sems[4 * a + k], recv_sem=recv_sems[4 * a + k],
                device_id=peer, device_id_type=pl.DeviceIdType.MESH)
            for a, (src, land) in enumerate(zip(src_refs, land_refs)) for k, peer in enumerate(peers)]


def _forward_to_sibling(gathered, name):
    n = len(gathered)

    def body(*refs):
        outs = refs[n:2 * n]
        send_sems, recv_sems, _ = refs[2 * n:]
        x, y, c, sib, chips = _mesh_place()
        sends = []
        for a in range(n):
            for j, chip in enumerate(chips):
                rows = outs[a].at[4 * chip[0] + 2 * chip[1] + c]
                cp = _rdma(rows, rows, send_sems, recv_sems, (a, j), sib)
                cp.start()
                sends.append(cp)
        for cp in sends:
            cp.wait_send()
        for a in range(n):
            for j, chip in enumerate(chips):
                rows = outs[a].at[4 * chip[0] + 2 * chip[1] + (1 - c)]
                _rdma(rows, rows, send_sems, recv_sems, (a, j), sib).wait_recv()

    any_spec = pl.BlockSpec(memory_space=pl.ANY)
    return list(pl.pallas_call(
        body, name=name, in_specs=[any_spec] * n, out_specs=[any_spec] * n,
        out_shape=[jax.ShapeDtypeStruct(g.shape, g.dtype) for g in gathered],
        input_output_aliases={i: i for i in range(n)},
        scratch_shapes=[pltpu.SemaphoreType.DMA((n, 3)), pltpu.SemaphoreType.DMA((n, 3)),
                        pltpu.SemaphoreType.DMA((1,))],
        compiler_params=pltpu.CompilerParams(has_side_effects=True),
    )(*gathered))


def _split_start(copies_fn, srcs, land_shapes, n_sems, name, after=None):
    ns, nl = len(srcs), len(land_shapes)
    nc = n_sems
    hbm = pl.BlockSpec(memory_space=pltpu.HBM)
    sem = pl.BlockSpec(memory_space=pltpu.SEMAPHORE)

    n_in = ns + nl + (0 if after is None else 1)

    def body(*refs):
        src_refs, land_refs = refs[:ns], refs[ns:ns + nl]
        send_sems, recv_sems = refs[n_in:n_in + nc], refs[n_in + nc:n_in + 2 * nc]
        token = refs[-1]
        for cp in copies_fn(src_refs, land_refs, send_sems, recv_sems):
            cp.start()
        token[...] = jnp.zeros_like(token)

    lands = [lax.empty(s.shape, s.dtype) for s in land_shapes]
    thru = [pltpu.HBM(t.shape, t.dtype) for t in list(srcs) + lands]
    order = [] if after is None else [after]
    outs = pl.pallas_call(
        body, name=name,
        out_shape=(*([pltpu.SemaphoreType.DMA(())] * (2 * nc)), *thru, jax.ShapeDtypeStruct((8, LANE), F32)),
        in_specs=[hbm] * (ns + nl) + [pl.BlockSpec(memory_space=pl.ANY)] * len(order),
        out_specs=(*([sem] * (2 * nc)), *([hbm] * (ns + nl)), pl.BlockSpec(memory_space=pltpu.VMEM)),
        input_output_aliases={i: 2 * nc + i for i in range(ns + nl)},
        compiler_params=pltpu.CompilerParams(has_side_effects=pltpu.SideEffectType.DATAFLOW_SIDE_EFFECTING),
    )(*[pltpu.with_memory_space_constraint(t, pltpu.HBM) for t in list(srcs) + lands], *order)
    thru_out = outs[2 * nc:2 * nc + ns + nl]
    return list(outs[:nc]), list(outs[nc:2 * nc]), list(thru_out[:ns]), list(thru_out[ns:]), outs[-1]


def _split_wait(copies_fn, send_sems, recv_sems, srcs, lands, after, name):
    ns, nl, nc = len(srcs), len(lands), len(send_sems)
    hbm = pl.BlockSpec(memory_space=pltpu.HBM)
    sem = pl.BlockSpec(memory_space=pltpu.SEMAPHORE)

    def body(*refs):
        src_refs, land_refs = refs[:ns], refs[ns:ns + nl]
        send_refs, recv_refs = refs[ns + nl:ns + nl + nc], refs[ns + nl + nc:ns + nl + 2 * nc]
        for cp in copies_fn(src_refs, land_refs, send_refs, recv_refs):
            cp.wait_send()
            cp.wait_recv()

    outs = pl.pallas_call(
        body, name=name,
        out_shape=[pltpu.HBM(t.shape, t.dtype) for t in list(srcs) + list(lands)],
        in_specs=[hbm] * (ns + nl) + [sem] * (2 * nc) + [pl.BlockSpec(memory_space=pl.ANY)],
        out_specs=[hbm] * (ns + nl),
        input_output_aliases={i: i for i in range(ns + nl)},
        compiler_params=pltpu.CompilerParams(has_side_effects=pltpu.SideEffectType.DATAFLOW_SIDE_EFFECTING),
    )(*srcs, *lands, *send_sems, *recv_sems, after)
    return list(outs[:ns]), list(outs[ns:])


def _pair_copies(src_refs, land_refs, send_sems, recv_sems):
    x, y, c, sib, chips = _mesh_place()
    nchip = N_DEV // 2
    return [pltpu.make_async_remote_copy(
                src_ref=src.at[ch, 1 - c], dst_ref=land.at[ch], send_sem=send_sems[nchip * a + ch],
                recv_sem=recv_sems[nchip * a + ch], device_id=sib, device_id_type=pl.DeviceIdType.MESH)
            for a, (src, land) in enumerate(zip(src_refs, land_refs)) for ch in range(nchip)]


def _chip_copies(src_refs, land_refs, send_sems, recv_sems):
    x, y, c, sib, chips = _mesh_place()
    return [pltpu.make_async_remote_copy(
                src_ref=src.at[2 * chip[0] + chip[1]], dst_ref=land.at[j], send_sem=send_sems[3 * a + j],
                recv_sem=recv_sems[3 * a + j], device_id=(*chip, c), device_id_type=pl.DeviceIdType.MESH)
            for a, (src, land) in enumerate(zip(src_refs, land_refs)) for j, chip in enumerate(chips)]


def _pair_sum(halves, theirs, core, name, tr=256):
    nchip, _, R, C = halves.shape
    tr = _row_tile(R, tr)

    def body(core_ref, h_ref, t_ref, o_ref):
        o_ref[...] = (h_ref[...] + t_ref[...]).astype(BF16)

    grid_spec = pltpu.PrefetchScalarGridSpec(
        num_scalar_prefetch=1, grid=(nchip, R // tr),
        in_specs=[pl.BlockSpec((None, None, tr, C), lambda ch, i, core_ref: (ch, core_ref[0], i, 0)),
                  pl.BlockSpec((None, tr, C), lambda ch, i, core_ref: (ch, i, 0))],
        out_specs=pl.BlockSpec((None, tr, C), lambda ch, i, core_ref: (ch, i, 0)))
    return pl.pallas_call(
        body, name=name, grid_spec=grid_spec, out_shape=jax.ShapeDtypeStruct((nchip, R, C), BF16),
        compiler_params=pltpu.CompilerParams(
            dimension_semantics=("parallel", "parallel"), vmem_limit_bytes=_vmem(3 * _nbytes((tr, C), F32))),
    )(core, *[pltpu.with_memory_space_constraint(t, pltpu.HBM) for t in (halves, theirs)])


def _chip_sum(own, recv, chip, name, tr=256):
    _, R, C = own.shape
    tr = _row_tile(R, tr)

    def body(chip_ref, o_ref, r_ref, out_ref):
        acc = o_ref[...].astype(F32)
        for j in range(3):
            acc = acc + r_ref[j].astype(F32)
        out_ref[...] = acc

    grid_spec = pltpu.PrefetchScalarGridSpec(
        num_scalar_prefetch=1, grid=(R // tr,),
        in_specs=[pl.BlockSpec((None, tr, C), lambda i, chip_ref: (chip_ref[0], i, 0)),
                  pl.BlockSpec((3, tr, C), lambda i, chip_ref: (0, i, 0))],
        out_specs=pl.BlockSpec((tr, C), lambda i, chip_ref: (i, 0)))
    return pl.pallas_call(
        body, name=name, grid_spec=grid_spec, out_shape=jax.ShapeDtypeStruct((R, C), F32),
        compiler_params=pltpu.CompilerParams(
            dimension_semantics=("parallel",), vmem_limit_bytes=_vmem(4 * _nbytes((tr, C), F32))),
    )(chip, *[pltpu.with_memory_space_constraint(t, pltpu.HBM) for t in (own, recv)])


def _row_tile(rows, target):
    best = rows
    for t in range(8, min(rows, target) + 1, 8):
        if rows % t == 0:
            best = t
    return best


def _sum_slots(recv, name, tr=128):
    n, R, C = recv.shape
    tr = _row_tile(R, tr if recv.dtype == F32 else 2 * tr)

    def body(r_ref, o_ref):
        acc = r_ref[0].astype(F32)
        for k in range(1, n):
            acc = acc + r_ref[k].astype(F32)
        o_ref[...] = acc

    return pl.pallas_call(
        body, name=name, grid=(R // tr,),
        in_specs=[pl.BlockSpec((n, tr, C), lambda i: (0, i, 0))],
        out_specs=pl.BlockSpec((tr, C), lambda i: (i, 0)),
        out_shape=jax.ShapeDtypeStruct((R, C), F32),
        compiler_params=pltpu.CompilerParams(
            dimension_semantics=("parallel",), vmem_limit_bytes=_vmem(10 * _nbytes((tr, C), F32))),
    )(recv)


def _adamw_math(g_ref, w_ref, m_ref, v_ref, d_ref, nm_ref, nv_ref):
    gv = g_ref[...]
    nm = ADAM_B1 * m_ref[...] + (1.0 - ADAM_B1) * gv
    nv = ADAM_B2 * v_ref[...] + (1.0 - ADAM_B2) * (gv * gv)
    m_hat = nm / (1.0 - ADAM_B1 ** ADAM_STEP)
    v_hat = nv / (1.0 - ADAM_B2 ** ADAM_STEP)
    d_ref[...] = -ADAM_LR * (m_hat / (jnp.sqrt(v_hat) + ADAM_EPS) + ADAM_WD * w_ref[...])
    nm_ref[...] = nm
    nv_ref[...] = nv


def _chip_sum_adamw(own, recv, chip, w, m, v, name, tr=256):
    _, R, C = own.shape
    tr = _row_tile(R, tr)

    def body(chip_ref, o_ref, r_ref, w_ref, m_ref, v_ref, g_ref, d_ref, nm_ref, nv_ref):
        acc = o_ref[...].astype(F32)
        for j in range(3):
            acc = acc + r_ref[j].astype(F32)
        g_ref[...] = acc
        _adamw_math(g_ref, w_ref, m_ref, v_ref, d_ref, nm_ref, nv_ref)

    row = pl.BlockSpec((tr, C), lambda i, chip_ref: (i, 0))
    grid_spec = pltpu.PrefetchScalarGridSpec(
        num_scalar_prefetch=1, grid=(R // tr,),
        in_specs=[pl.BlockSpec((None, tr, C), lambda i, chip_ref: (chip_ref[0], i, 0)),
                  pl.BlockSpec((3, tr, C), lambda i, chip_ref: (0, i, 0)), row, row, row],
        out_specs=[row] * 4)
    return pl.pallas_call(
        body, name=name, grid_spec=grid_spec, out_shape=[jax.ShapeDtypeStruct((R, C), F32)] * 4,
        compiler_params=pltpu.CompilerParams(
            dimension_semantics=("parallel",), vmem_limit_bytes=_vmem(10 * _nbytes((tr, C), F32))),
    )(chip, own, recv, w, m, v)


def _adamw_small(params, name):
    n = len(params)
    two_d = [(-1, w.shape[-1]) for w, _, _, _ in params]
    flat = [t.reshape(two_d[i]) for i, (w, m, v, g) in enumerate(params) for t in (g, w, m, v)]

    def body(*refs):
        ins, outs = refs[:4 * n], refs[4 * n:]
        for i in range(n):
            _adamw_math(*ins[4 * i:4 * i + 4], *outs[3 * i:3 * i + 3])

    outs = pl.pallas_call(
        body, name=name,
        out_shape=[jax.ShapeDtypeStruct(flat[4 * i].shape, F32) for i in range(n) for _ in range(3)],
    )(*flat)
    return [(g.reshape(w.shape),) + tuple(outs[3 * i + t].reshape(w.shape) for t in range(3))
            for i, (w, m, v, g) in enumerate(params)]


def _adamw(g, w, m, v, name, tr=256):
    R, C = g.shape
    tr = _row_tile(R, tr)
    body = functools.partial(_adamw_math)

    row = pl.BlockSpec((tr, C), lambda i: (i, 0))
    return pl.pallas_call(
        body, name=name, grid=(R // tr,), in_specs=[row] * 4, out_specs=[row] * 3,
        out_shape=[jax.ShapeDtypeStruct((R, C), F32)] * 3,
        compiler_params=pltpu.CompilerParams(
            dimension_semantics=("parallel",), vmem_limit_bytes=_vmem(8 * _nbytes((tr, C), F32))),
    )(*[pltpu.with_memory_space_constraint(t, pltpu.HBM) for t in (g, w, m, v)])


def _pack_rows(parts):
    flat = jnp.concatenate([p.reshape(-1).astype(F32) for p in parts])
    rows = -(-flat.shape[0] // (8 * LANE)) * 8
    return jnp.pad(flat, (0, rows * LANE - flat.shape[0])).reshape(rows, LANE)


def _unpack(flat, shapes):
    out, off = [], 0
    for shp in shapes:
        n = math.prod(shp)
        out.append(flat[off:off + n].reshape(shp))
        off += n
    return out


def kernel(x, a_norm, a_w_in, a_conv, a_w_out, kv_norm, w_kv, b_norm, b_w_q, b_w_o, rel_bias, ffn_norm, ffn_w_up, ffn_conv, ffn_conv_b, ffn_w_down, final_norm, loss_target, m_a_norm, m_a_w_in, m_a_conv, m_a_w_out, m_kv_norm, m_w_kv, m_b_norm, m_b_w_q, m_b_w_o, m_rel_bias, m_ffn_norm, m_ffn_w_up, m_ffn_conv, m_ffn_conv_b, m_ffn_w_down, m_final_norm, v_a_norm, v_a_w_in, v_a_conv, v_a_w_out, v_kv_norm, v_w_kv, v_b_norm, v_b_w_q, v_b_w_o, v_rel_bias, v_ffn_norm, v_ffn_w_up, v_ffn_conv, v_ffn_conv_b, v_ffn_w_down, v_final_norm):
    B, S, D = x.shape
    T = B * S
    F = ffn_w_down.shape[1] * N_DEV
    me = 4 * lax.axis_index("x") + 2 * lax.axis_index("y") + lax.axis_index("c")

    big_shards = [a_w_in[0].T, a_w_out[0], w_kv.T, b_w_q[0], b_w_o[0],
                  ffn_w_up[0].T, ffn_w_up[1].T, ffn_w_down[0], ffn_w_down[1]]
    small_shapes = [a_norm.shape, a_conv.shape, ffn_conv.shape]
    small_pack = _pack_rows([a_norm, a_conv, ffn_conv])
    shards = [s.astype(BF16) for s in big_shards]
    first = _all_gather([shards[0], shards[1], small_pack], "gather_weights_first")
    win_t, wout = first[0].reshape(-1, D), first[1].reshape(-1, D)
    smalls = [_unpack(first[2][j].reshape(-1), small_shapes) for j in range(N_DEV)]
    a_norm_f = jnp.concatenate([s[0] for s in smalls], axis=-1)
    a_conv_f = jnp.concatenate([s[1] for s in smalls], axis=-1)[0]
    ffn_conv_f = jnp.concatenate([s[2] for s in smalls], axis=-1)

    def gather_start(tag, idxs, after):
        srcs = [shards[i] for i in idxs]
        lands = [jax.ShapeDtypeStruct((N_DEV,) + s.shape, s.dtype) for s in srcs]
        return _split_start(_gather_copies, srcs, lands, 4 * len(srcs), f"gather_start_{tag}", after=after)

    def gather_finish(tag, handle, after):
        send, recv, srcs, lands, _ = handle
        srcs, lands = _split_wait(_gather_copies, send, recv, srcs, lands, after, f"gather_wait_{tag}")
        lands = _forward_to_sibling(lands, f"gather_forward_{tag}")
        return [lax.dynamic_update_slice(g, s[None], (me, 0, 0)).reshape(-1, D) for g, s in zip(lands, srcs)]

    ffn0_w = gather_start("ffn0", [5, 7], first[0])
    rest_w = gather_start("rest", [2, 3, 4, 6, 8], ffn0_w[4])

    x2 = x.reshape(T, D)
    (xn0,) = _rmsnorm_fwd(x2, a_norm_f, "a_norm_fwd")
    bch = _matmul(xn0, win_t, mode="nt", out_dtype=BF16, after=rest_w[4], name="a_in_proj", tm=2048).reshape(B, S, 3 * D)
    gated = _shortconv_fwd(bch, a_conv_f, "a_gate_fwd").reshape(T, D)
    h1 = _matmul(gated, wout, mode="nn", out_dtype=F32, add=x2, name="a_out_proj")
    wup0_t, wdn0 = gather_finish("ffn0", ffn0_w, h1)
    wup_t, wdn = [wup0_t, None], [wdn0, None]

    def ffn_fwd(h, l):
        (xn,) = _rmsnorm_fwd(h, ffn_norm[l:l + 1], f"ffn{l}_norm_fwd")
        u0 = _matmul(xn, wup_t[l], mode="nt", out_dtype=BF16, name=f"ffn{l}_up", tm=2048, tn=1408).reshape(B, S, 2 * F)
        act = _ffn_gate_fwd(u0, ffn_conv_f[l], ffn_conv_b[l:l + 1], f"ffn{l}_gate_fwd").reshape(T, F)
        out = _matmul(act, wdn[l], mode="nn", out_dtype=F32, add=h, name=f"ffn{l}_down", tk=2816)
        return out, (xn, u0, act)

    h2, ffn0_saved = ffn_fwd(h1, 0)
    wkv_t, wq, wo, wup_t[1], wdn[1] = gather_finish("rest", rest_w, h2)
    kvn, xnb = _rmsnorm_fwd(h2, jnp.stack([kv_norm, b_norm[0]]), "kv_b_norm_fwd")
    kv = _matmul(kvn, wkv_t, mode="nt", out_dtype=F32, name="kv_proj", tm=2048).reshape(B, S, 2 * D)
    q = _matmul(xnb, wq, mode="nn", out_dtype=F32, name="q_proj", tm=2048).reshape(B, S, D)

    tables = [_band_tables(w, d) for (w, d) in DILATED_BRANCHES]
    onehots = jnp.stack([t[0] for t in tables])
    P = ATT_BLOCK
    bias_vals = _bias_lookup(rel_bias.T, onehots, "rel_bias_lookup").reshape(3, N_HEADS, P, 2 * P)
    in_cur = (jnp.arange(2 * P) >= P)[None, :]
    masks = jnp.stack([jnp.stack([t[1] & in_cur, t[1]]) for t in tables])
    tabs = jnp.where(masks[:, :, None], bias_vals[:, None], NEG)

    attn3, lse3 = _attn_fwd(q, kv, tabs, "attn_fwd")
    attn = attn3.reshape(T, D)
    h3 = _matmul(attn, wo, mode="nn", out_dtype=F32, add=h2, name="o_proj")
    h4, ffn1_saved = ffn_fwd(h3, 1)

    dh4, dh4b, d_final_norm, sq = _final_loss_bwd(h4, final_norm.reshape(1, D), loss_target.reshape(T, D), "loss_bwd")
    loss_part = 0.5 * jnp.sum(sq) / D

    core_idx = lax.axis_index("c").astype(jnp.int32).reshape(1)
    chip_idx = (2 * lax.axis_index("x") + lax.axis_index("y")).astype(jnp.int32).reshape(1)
    in_flight = []

    def group_start(tag, indexed_grads):
        halves = [g.reshape(N_DEV // 2, 2, g.shape[0] // N_DEV, D) for _, g in indexed_grads]
        lands = [jax.ShapeDtypeStruct((h.shape[0],) + h.shape[2:], h.dtype) for h in halves]
        send, recv, srcs, lnd, token = _split_start(_pair_copies, halves, lands, (N_DEV // 2) * len(halves),
                                                    f"exchange_core_start_{tag}")
        return (tag, [i for i, _ in indexed_grads], send, recv, srcs, lnd), token

    def group_finish(handle, after):
        tag, idxs, send, recv, srcs, lnd = handle
        halves, theirs = _split_wait(_pair_copies, send, recv, srcs, lnd, after, f"exchange_core_wait_{tag}")
        sums = [_pair_sum(h_, t_, core_idx, f"grad_core_sum_{tag}{k}")
                for k, (h_, t_) in enumerate(zip(halves, theirs))]
        lands = [jax.ShapeDtypeStruct((3,) + s.shape[1:], s.dtype) for s in sums]
        send, recv, srcs, lnd, token = _split_start(_chip_copies, sums, lands, 3 * len(sums),
                                                    f"exchange_chip_start_{tag}")
        in_flight.append((tag, idxs, send, recv, srcs, lnd))
        return token

    def ffn_bwd(dh_out, dh_out_b, h_in, saved, l, after):
        xn, u0, act = saved
        dact = _matmul(dh_out_b, wdn[l], mode="nt", out_dtype=BF16, after=after, name=f"ffn{l}_down_dx",
                       tm=2048, tn=1408).reshape(B, S, F)
        d_wdn = _matmul(act, dh_out_b, mode="tn", out_dtype=F32, name=f"ffn{l}_down_dw", tm=1408, tn=1024, tk=2048)
        du0, d_conv, d_conv_b = _ffn_gate_bwd(u0, dact, ffn_conv_f[l], ffn_conv_b[l:l + 1], f"ffn{l}_gate_bwd")
        du0 = du0.reshape(2, T, F)
        dxn = _matmul(du0, wup_t[l], mode="nn", out_dtype=BF16, a_parts=2, name=f"ffn{l}_up_dx", tk=2816)
        d_wup_t = _matmul(du0, xn, mode="tn", out_dtype=F32, a_parts=2, name=f"ffn{l}_up_dw", tm=1408, tn=1024, tk=2048)
        dh_in, dh_in_b, d_norm = _rmsnorm_bwd(h_in, ffn_norm[l:l + 1], [dxn], dh_out, f"ffn{l}_norm_bwd")
        return (dh_in, dh_in_b), (d_wdn, d_wup_t, d_conv, d_conv_b, d_norm)

    (dh3, dh3b), ffn1_grads = ffn_bwd(dh4, dh4b, h3, ffn1_saved, 1, None)
    group, token = group_start("ffn1", [(6, ffn1_grads[1]), (8, ffn1_grads[0])])

    dattn = _matmul(dh3b, wo, mode="nt", out_dtype=F32, after=token, name="o_proj_dx", tm=2048)
    token = group_finish(group, dattn)
    dattn = dattn.reshape(B, S, D)
    d_wo = _matmul(attn, dh3b, mode="tn", out_dtype=F32, after=token, name="o_proj_dw", tm=1024, tn=1024, tk=2048)
    dq, dkv, dtab = _attn_bwd(q, kv, dattn, attn3, lse3, tabs, "attn_bwd")
    dtabs = dtab.transpose(1, 0, 2, 3, 4).reshape(3, N_HEADS, P * 2 * P)
    d_rel_bias = _bias_grad(dtabs, onehots, "rel_bias_grad").T
    dq2 = dq.reshape(T, D)
    dkv2 = dkv.reshape(2, T, D)
    d_wq = _matmul(xnb, dq2, mode="tn", out_dtype=F32, name="q_proj_dw", tm=1024, tn=1024, tk=2048)
    d_wkv_t = _matmul(dkv2, kvn, mode="tn", out_dtype=F32, a_parts=2, name="kv_proj_dw", tm=1024, tn=1024, tk=2048)
    group, token = group_start("attn", [(2, d_wkv_t), (3, d_wq), (4, d_wo)])
    dxnb = _matmul(dq2, wq, mode="nt", out_dtype=BF16, after=token, name="q_proj_dx")
    dkvn = _matmul(dkv2, wkv_t, mode="nn", out_dtype=BF16, a_parts=2, name="kv_proj_dx")
    dh2, dh2b, d_kvb_norm = _rmsnorm_bwd(h2, jnp.stack([kv_norm, b_norm[0]]), [dkvn, dxnb], dh3, "kv_b_norm_bwd")
    token = group_finish(group, dh2b)

    (dh1, dh1b), ffn0_grads = ffn_bwd(dh2, dh2b, h1, ffn0_saved, 0, token)
    group, token = group_start("ffn0", [(5, ffn0_grads[1]), (7, ffn0_grads[0])])

    dgated = _matmul(dh1b, wout, mode="nt", out_dtype=BF16, after=token, name="a_out_proj_dx", tm=2048)
    token = group_finish(group, dgated)
    dgated = dgated.reshape(B, S, D)
    d_wout = _matmul(gated, dh1b, mode="tn", out_dtype=F32, after=token, name="a_out_proj_dw",
                     tm=1024, tn=1024, tk=2048)
    dbch, d_a_conv = _shortconv_bwd(bch, dgated, a_conv_f, "a_gate_bwd")
    dbch = dbch.reshape(3, T, D)
    dxn0 = _matmul(dbch, win_t, mode="nn", out_dtype=BF16, a_parts=3, name="a_in_proj_dx", tm=2048)
    d_win_t = _matmul(dbch, xn0, mode="tn", out_dtype=F32, a_parts=3, name="a_in_proj_dw", tm=1024, tn=1024, tk=2048)
    group, token = group_start("a", [(0, d_win_t), (1, d_wout)])
    grad_x, _, d_a_norm = _rmsnorm_bwd(x2, a_norm_f + token[0, 0], [dxn0], dh1, "a_norm_bwd")
    token = group_finish(group, grad_x)

    small_full = [d_a_norm, d_a_conv, jnp.stack([ffn0_grads[2], ffn1_grads[2]]),
                  d_kvb_norm[0], d_kvb_norm[1], d_rel_bias, jnp.concatenate([ffn0_grads[4], ffn1_grads[4]]),
                  jnp.concatenate([ffn0_grads[3], ffn1_grads[3]]), d_final_norm, loss_part]
    small_full_shapes = [(1, D), (3, D), (2, 3, 2 * F), (D,), (1, D), rel_bias.shape, (2, D), (2, 2 * F), (D,), ()]
    _, gath = _core_pair_exchange([], [_pack_rows(small_full)], "exchange_small")
    chip_sums, recv = {}, {}
    for tag, idxs, send, rcv, srcs, lnd in in_flight:
        srcs, lnd = _split_wait(_chip_copies, send, rcv, srcs, lnd, token, f"exchange_chip_wait_{tag}")
        for i, s_, l_ in zip(idxs, srcs, lnd):
            chip_sums[i], recv[i] = s_, l_
    small_sum = _sum_slots(gath[0], "small_grad_sum").reshape(-1)
    (g_a_norm_f, g_a_conv_f, g_ffn_conv_f, g_kv_norm, g_b_norm, g_rel_bias, g_ffn_norm, g_ffn_conv_b,
     g_final_norm, loss) = _unpack(small_sum, small_full_shapes)

    def my_cols(full, width):
        return lax.dynamic_slice_in_dim(full, me * width, width, axis=full.ndim - 1)

    g_a_norm = my_cols(g_a_norm_f, D // N_DEV)
    g_a_conv = my_cols(g_a_conv_f, D // N_DEV)[None]
    g_ffn_conv = my_cols(g_ffn_conv_f, 2 * F // N_DEV)

    big_w = [(a_w_in, m_a_w_in, v_a_w_in, True), (a_w_out, m_a_w_out, v_a_w_out, False),
             (w_kv, m_w_kv, v_w_kv, True), (b_w_q, m_b_w_q, v_b_w_q, False), (b_w_o, m_b_w_o, v_b_w_o, False),
             (ffn_w_up[0], m_ffn_w_up[0], v_ffn_w_up[0], True), (ffn_w_up[1], m_ffn_w_up[1], v_ffn_w_up[1], True),
             (ffn_w_down[0], m_ffn_w_down[0], v_ffn_w_down[0], False),
             (ffn_w_down[1], m_ffn_w_down[1], v_ffn_w_down[1], False)]
    big_out = []
    for i, (w, m, v, transposed) in enumerate(big_w):
        if transposed:
            g = _chip_sum(chip_sums[i], recv[i], chip_idx, f"grad_sum_{i}").T
            w2, m2, v2 = (t.reshape(g.shape) for t in (w, m, v))
            delta, nm, nv = _adamw(g, w2, m2, v2, f"adamw_{i}")
        else:
            w2, m2, v2 = (t.reshape(chip_sums[i].shape[1:]) for t in (w, m, v))
            g, delta, nm, nv = _chip_sum_adamw(chip_sums[i], recv[i], chip_idx, w2, m2, v2, f"grad_sum_adamw_{i}")
        big_out.append(tuple(t.reshape(w.shape) for t in (g, delta, nm, nv)))

    def pair(i, j):
        return tuple(jnp.stack([big_out[i][t], big_out[j][t]]) for t in range(4))

    small_w = [(a_norm, m_a_norm, v_a_norm, g_a_norm), (a_conv, m_a_conv, v_a_conv, g_a_conv),
               (ffn_conv, m_ffn_conv, v_ffn_conv, g_ffn_conv), (kv_norm, m_kv_norm, v_kv_norm, g_kv_norm),
               (b_norm, m_b_norm, v_b_norm, g_b_norm), (rel_bias, m_rel_bias, v_rel_bias, g_rel_bias),
               (ffn_norm, m_ffn_norm, v_ffn_norm, g_ffn_norm),
               (ffn_conv_b, m_ffn_conv_b, v_ffn_conv_b, g_ffn_conv_b),
               (final_norm, m_final_norm, v_final_norm, g_final_norm)]
    small_out = _adamw_small(small_w, "adamw_small")

    per_weight = [small_out[0], big_out[0], small_out[1], big_out[1], small_out[3], big_out[2], small_out[4],
                  big_out[3], big_out[4], small_out[5], small_out[6], pair(5, 6), small_out[2], small_out[7],
                  pair(7, 8), small_out[8]]
    outs = [loss, grad_x.reshape(B, S, D)]
    for t in range(4):
        outs.extend(pw[t] for pw in per_weight)
    return tuple(outs)
```
